```python
import math
import jax, jax.numpy as jnp
from jax import lax
import numpy as np

D_MODEL = 1024
BATCH = 8
SEQ = 2048
DEPTH = 1

CTX_LEN = 256
GRID_W = 64
EPS = 1e-6
SHORT_CONV = 3

D_HYENA = D_MODEL // 2
HYENA_ORDER = 2
HYENA_BANDS = 8
HYENA_EMB = 1 + 2 * HYENA_BANDS
HYENA_FILTER_HID = 64
HYENA_FAST_DECAY = 0.3
HYENA_SLOW_DECAY = 1.5
HYENA_TARGET = 1e-2

D_SSD = D_MODEL // 2
SSD_HEAD_DIM = 64
SSD_HEADS = D_SSD // SSD_HEAD_DIM
SSD_GROUPS = 2
SSD_HPG = SSD_HEADS // SSD_GROUPS
SSD_STATE = 128
SSD_CHUNK = 128

D_MIX = D_HYENA + D_SSD
D_XBC = D_SSD + 2 * SSD_GROUPS * SSD_STATE
HY_COLS = (HYENA_ORDER + 1) * D_HYENA
D_IN = HY_COLS + D_SSD + D_XBC + 2 * SSD_HEADS

MOE_GROUPS = 8
EXPERTS_PER_GROUP = 8
N_EXPERTS = MOE_GROUPS * EXPERTS_PER_GROUP
TOP_K = 2
D_EXPERT = 512
MOE_BLOCK = 128

kernel_name = 'hymba_hyena_ssd_hmoe_dit_block'


def rmsnorm(x, g):
    xf = x.astype(jnp.float32)
    y = xf * lax.rsqrt(jnp.mean(xf * xf, axis=-1, keepdims=True) + EPS)
    return y.astype(x.dtype) * g


def modulate(h, shift, scale):
    return h * (1 + scale) + shift


def short_conv_rows(u, w, b, n_rows):
    bsz, L, C = u.shape
    row_len = L // n_rows
    pad = SHORT_CONV // 2
    ur = jnp.pad(u.reshape(bsz, n_rows, row_len, C), ((0, 0), (0, 0), (pad, pad), (0, 0)))
    y = b + sum(ur[:, :, k:k + row_len] * w[k] for k in range(SHORT_CONV))
    return y.reshape(bsz, L, C)


def hyena_filters(L, f_w1, f_b1, f_freq, f_w2, f_b2, f_w3):
    f32 = jnp.float32
    pos = jnp.arange(L, dtype=f32)
    t = pos / max(L - 1, 1)
    bands = jnp.linspace(1e-4, HYENA_BANDS - 1, HYENA_BANDS, dtype=f32)
    ang = (2 * math.pi / L) * pos[:, None] * bands[None, :]
    feats = jnp.concatenate([t[:, None], jnp.cos(ang), -jnp.sin(ang)], axis=-1)
    freq = f_freq.astype(f32)
    h = jnp.sin(freq * (feats @ f_w1.astype(f32) + f_b1.astype(f32)))
    h = jnp.sin(freq * (h @ f_w2.astype(f32) + f_b2.astype(f32)))
    k = (h @ f_w3.astype(f32)).reshape(L, HYENA_ORDER, 2, D_HYENA)
    deltas = jnp.abs(jnp.linspace(math.log(HYENA_TARGET) / HYENA_SLOW_DECAY,
                                  math.log(HYENA_TARGET) / HYENA_FAST_DECAY, D_HYENA, dtype=f32))
    window = jnp.exp(-t[:, None] * deltas[None, :])
    return k * window[:, None, None, :]


def long_conv_bidir(z, k_f, k_b, bias):
    L = z.shape[1]
    k_full = jnp.concatenate([k_f, jnp.zeros_like(k_f[:1]), jnp.flip(k_b[1:], axis=0)], axis=0)
    zf = jnp.fft.rfft(z, n=2 * L, axis=1)
    kf = jnp.fft.rfft(k_full, axis=0)
    y = jnp.fft.irfft(zf * kf[None], n=2 * L, axis=1)[:, :L]
    return y + z * bias


def hyena_branch(p, conv_w, conv_b, f_w1, f_b1, f_freq, f_w2, f_b2, f_w3, h_bias, n_rows):
    L = p.shape[1]
    u = short_conv_rows(p, conv_w, conv_b, n_rows).astype(jnp.float32)
    parts = jnp.split(u, HYENA_ORDER + 1, axis=-1)
    k = hyena_filters(L, f_w1, f_b1, f_freq, f_w2, f_b2, f_w3)
    hb = h_bias.astype(jnp.float32)
    z = parts[0]
    for o in range(HYENA_ORDER):
        z = parts[o + 1] * long_conv_bidir(z, k[:, o, 0], k[:, o, 1], hb[o])
    return z.astype(p.dtype)


def ssd_prep(p, conv_w, conv_b, a_log, dt_bias, n_rows):
    bsz, L, _ = p.shape
    G, R, P, N = SSD_GROUPS, SSD_HPG, SSD_HEAD_DIM, SSD_STATE
    z = p[..., :D_SSD]
    xbc = jax.nn.silu(short_conv_rows(p[..., D_SSD:D_SSD + D_XBC], conv_w, conv_b, n_rows))
    xs = xbc[..., :D_SSD].reshape(bsz, L, G, R, P)
    bm = xbc[..., D_SSD:D_SSD + G * N].reshape(bsz, L, G, N)
    cm = xbc[..., D_SSD + G * N:].reshape(bsz, L, G, N)
    dt_raw = p[..., D_SSD + D_XBC:].astype(jnp.float32).reshape(bsz, L, 2, G, R)
    dt = jax.nn.softplus(dt_raw + dt_bias.astype(jnp.float32).reshape(2, G, R))
    a = -jnp.exp(a_log.astype(jnp.float32)).reshape(2, G, R) * dt
    return z, xs, bm, cm, dt, a


def ssd_scan(xs, a, bm, cm, h0):
    bsz, L, G, R, P = xs.shape
    N = bm.shape[-1]
    nc = L // SSD_CHUNK
    xs = xs.reshape(bsz, nc, SSD_CHUNK, G, R, P)
    a = a.reshape(bsz, nc, SSD_CHUNK, G, R)
    bm = bm.reshape(bsz, nc, SSD_CHUNK, G, N)
    cm = cm.reshape(bsz, nc, SSD_CHUNK, G, N)
    a_cum = jnp.cumsum(a, axis=2)
    lower = jnp.tril(jnp.ones((SSD_CHUNK, SSD_CHUNK), bool))[:, :, None, None]
    seg = a_cum[:, :, :, None] - a_cum[:, :, None, :]
    decay = jnp.exp(jnp.where(lower, seg, -jnp.inf))
    cb = jnp.einsum('bclgn,bcsgn->bclsg', cm, bm)
    y_diag = jnp.einsum('bclsg,bclsgr,bcsgrp->bclgrp', cb, decay, xs)
    to_end = jnp.exp(a_cum[:, :, -1:] - a_cum)
    chunk_states = jnp.einsum('bclgn,bclgr,bclgrp->bcgrpn', bm, to_end, xs)
    chunk_decay = jnp.exp(a_cum[:, :, -1])

    def step(h, inp):
        s_c, d_c = inp
        return h * d_c[..., None, None] + s_c, h

    h_final, h_in = lax.scan(step, h0, (jnp.moveaxis(chunk_states, 1, 0), jnp.moveaxis(chunk_decay, 1, 0)))
    h_in = jnp.moveaxis(h_in, 0, 1)
    y_off = jnp.einsum('bclgn,bcgrpn,bclgr->bclgrp', cm, h_in, jnp.exp(a_cum))
    return (y_diag + y_off).reshape(bsz, L, G, R, P), h_final


def ssd_final_state(xs, a, bm):
    a_cum = jnp.cumsum(a, axis=1)
    w = jnp.exp(a_cum[:, -1:] - a_cum)
    return jnp.einsum('blgn,blgr,blgrp->bgrpn', bm, w, xs)


def _rev(t):
    return jnp.flip(t, axis=1)


def ssd_context_states(prep):
    _, xs, bm, _, dt, a = prep
    xf = xs.astype(jnp.float32)
    bf = bm.astype(jnp.float32)
    h_f = ssd_final_state(xf * dt[:, :, 0, :, :, None], a[:, :, 0], bf)
    h_b = ssd_final_state(_rev(xf * dt[:, :, 1, :, :, None]), _rev(a[:, :, 1]), _rev(bf))
    return h_f, h_b


def ssd_branch(prep, d_skip, norm_g, h0_f, h0_b):
    z, xs, bm, cm, dt, a = prep
    bsz, L = z.shape[:2]
    xf = xs.astype(jnp.float32)
    bf = bm.astype(jnp.float32)
    cf = cm.astype(jnp.float32)
    y_f, h_f = ssd_scan(xf * dt[:, :, 0, :, :, None], a[:, :, 0], bf, cf, h0_f)
    y_b, h_b = ssd_scan(_rev(xf * dt[:, :, 1, :, :, None]), _rev(a[:, :, 1]), _rev(bf), _rev(cf), h0_b)
    y = y_f + _rev(y_b) + d_skip.astype(jnp.float32).reshape(SSD_GROUPS, SSD_HPG)[..., None] * xf
    y = y.reshape(bsz, L, D_SSD) * jax.nn.silu(z.astype(jnp.float32))
    y = y.reshape(bsz, L, SSD_GROUPS, D_SSD // SSD_GROUPS)
    y = y * lax.rsqrt(jnp.mean(y * y, axis=-1, keepdims=True) + EPS)
    y = y.reshape(bsz, L, D_SSD) * norm_g
    return y.astype(z.dtype), h_f, h_b


def moe_ffn(h, w_group, b_group, w_expert, b_expert, w1, w3, w2):
    T, D = h.shape
    group_prob = jax.nn.softmax((h @ w_group + b_group).astype(jnp.float32), axis=-1)
    p_group, g_sel = lax.top_k(group_prob, 1)
    exp_logits = (h @ w_expert + b_expert).astype(jnp.float32).reshape(T, MOE_GROUPS, EXPERTS_PER_GROUP)
    in_group = exp_logits[jnp.arange(T), g_sel[:, 0]]
    top_logit, top_idx = lax.top_k(in_group, TOP_K)
    gate = jax.nn.softmax(top_logit, axis=-1) * p_group
    expert_id = g_sel * EXPERTS_PER_GROUP + top_idx
    M = T * TOP_K
    eid = expert_id.reshape(M)
    tok = jnp.repeat(jnp.arange(T), TOP_K)
    order = jnp.argsort(eid)
    eid_s, tok_s, gate_s = eid[order], tok[order], gate.reshape(M)[order]
    counts = jnp.bincount(eid, length=N_EXPERTS)
    padded = (counts + MOE_BLOCK - 1) // MOE_BLOCK * MOE_BLOCK
    ends = jnp.cumsum(padded)
    dest = (ends - padded)[eid_s] + jnp.arange(M) - (jnp.cumsum(counts) - counts)[eid_s]
    n_blocks = -(-M // MOE_BLOCK) + N_EXPERTS
    buf = jnp.zeros((n_blocks * MOE_BLOCK, D), h.dtype).at[dest].set(h[tok_s])
    block_eid = jnp.minimum(jnp.searchsorted(ends, jnp.arange(n_blocks) * MOE_BLOCK, side='right'), N_EXPERTS - 1)

    def expert_block(args):
        xb, e = args
        return (jax.nn.silu(xb @ w1[e]) * (xb @ w3[e])) @ w2[e]

    yb = lax.map(expert_block, (buf.reshape(n_blocks, MOE_BLOCK, D), block_eid))
    contrib = yb.reshape(-1, D)[dest] * gate_s[:, None].astype(h.dtype)
    return jnp.zeros_like(h).at[tok_s].add(contrib)


def setup_inputs(seed: int = 0) -> dict:
    key = jax.random.key(seed)
    ks = jax.random.split(key, 40)

    def nrm(i, shape, s):
        return jax.random.normal(ks[i], shape, jnp.float32) * s

    NL = DEPTH
    FH = HYENA_FILTER_HID
    dt0 = jnp.exp(jax.random.uniform(ks[30], (NL, 2, SSD_HEADS)) * (math.log(0.1) - math.log(1e-3)) + math.log(1e-3))
    return {
        'x': nrm(0, (BATCH, SEQ, D_MODEL), 1.0),
        'c': nrm(1, (BATCH, D_MODEL), 1.0),
        'ctx': nrm(2, (BATCH, CTX_LEN, D_MODEL), 1.0),
        'c_ctx': nrm(3, (D_MODEL,), 1.0),
        'w_ada': nrm(4, (NL, D_MODEL, 6 * D_MODEL), D_MODEL ** -0.5),
        'b_ada': nrm(5, (NL, 6 * D_MODEL), 0.02),
        'g_norm1': 1.0 + nrm(6, (NL, D_MODEL), 0.05),
        'g_norm2': 1.0 + nrm(7, (NL, D_MODEL), 0.05),
        'w_in': nrm(8, (NL, D_MODEL, D_IN), D_MODEL ** -0.5),
        'hy_conv_w': nrm(9, (NL, SHORT_CONV, HY_COLS), SHORT_CONV ** -0.5),
        'hy_conv_b': nrm(10, (NL, HY_COLS), 0.02),
        'hy_f_w1': nrm(11, (NL, HYENA_EMB, FH), HYENA_EMB ** -0.5),
        'hy_f_b1': nrm(12, (NL, FH), 0.02),
        'hy_f_freq': 1.0 + nrm(13, (NL, FH), 0.05),
        'hy_f_w2': nrm(14, (NL, FH, FH), FH ** -0.5),
        'hy_f_b2': nrm(15, (NL, FH), 0.02),
        'hy_f_w3': nrm(16, (NL, FH, HYENA_ORDER * 2 * D_HYENA), 0.05 * FH ** -0.5),
        'hy_bias': nrm(17, (NL, HYENA_ORDER, D_HYENA), 0.5),
        'ssd_conv_w': nrm(18, (NL, SHORT_CONV, D_XBC), SHORT_CONV ** -0.5),
        'ssd_conv_b': nrm(19, (NL, D_XBC), 0.02),
        'ssd_a_log': jnp.log(jax.random.uniform(ks[31], (NL, 2, SSD_HEADS), minval=1.0, maxval=16.0)),
        'ssd_dt_bias': dt0 + jnp.log(-jnp.expm1(-dt0)),
        'ssd_d': 1.0 + nrm(20, (NL, SSD_HEADS), 0.1),
        'ssd_norm_g': 1.0 + nrm(21, (NL, D_SSD), 0.05),
        'w_out': nrm(22, (NL, D_MIX, D_MODEL), D_MIX ** -0.5),
        'w_group': nrm(23, (NL, D_MODEL, MOE_GROUPS), D_MODEL ** -0.5),
        'b_group': nrm(24, (NL, MOE_GROUPS), 0.01),
        'w_expert': nrm(25, (NL, D_MODEL, N_EXPERTS), D_MODEL ** -0.5),
        'b_expert': nrm(26, (NL, N_EXPERTS), 0.01),
        'w1': nrm(27, (NL, N_EXPERTS, D_MODEL, D_EXPERT), D_MODEL ** -0.5),
        'w3': nrm(28, (NL, N_EXPERTS, D_MODEL, D_EXPERT), D_MODEL ** -0.5),
        'w2': nrm(29, (NL, N_EXPERTS, D_EXPERT, D_MODEL), D_EXPERT ** -0.5),
        'g_final': 1.0 + nrm(32, (D_MODEL,), 0.05),
    }


def reference(x, c, ctx, c_ctx, w_ada, b_ada, g_norm1, g_norm2, w_in, hy_conv_w, hy_conv_b,
              hy_f_w1, hy_f_b1, hy_f_freq, hy_f_w2, hy_f_b2, hy_f_w3, hy_bias,
              ssd_conv_w, ssd_conv_b, ssd_a_log, ssd_dt_bias, ssd_d, ssd_norm_g, w_out,
              w_group, b_group, w_expert, b_expert, w1, w3, w2, g_final):
    bsz, seq_len, _ = x.shape
    rows = seq_len // GRID_W
    for l in range(DEPTH):
        last = l == DEPTH - 1
        hy_p = (hy_conv_w[l], hy_conv_b[l], hy_f_w1[l], hy_f_b1[l], hy_f_freq[l],
                hy_f_w2[l], hy_f_b2[l], hy_f_w3[l], hy_bias[l])
        ssd_p = (ssd_conv_w[l], ssd_conv_b[l], ssd_a_log[l], ssd_dt_bias[l])
        moe_p = (w_group[l], b_group[l], w_expert[l], b_expert[l], w1[l], w3[l], w2[l])
        sh1, sc1, ga1, sh2, sc2, ga2 = jnp.split((jax.nn.silu(c) @ w_ada[l] + b_ada[l])[:, None, :], 6, axis=-1)
        csh1, csc1, cga1, csh2, csc2, cga2 = jnp.split(jax.nn.silu(c_ctx) @ w_ada[l] + b_ada[l], 6, axis=-1)

        pc = modulate(rmsnorm(ctx, g_norm1[l]), csh1, csc1) @ w_in[l]
        prep_c = ssd_prep(pc[..., HY_COLS:], *ssd_p, 1)
        if last:
            h_f, h_b = ssd_context_states(prep_c)
        else:
            h_zero = jnp.zeros((bsz, SSD_GROUPS, SSD_HPG, SSD_HEAD_DIM, SSD_STATE), jnp.float32)
            yc_ssd, h_f, h_b = ssd_branch(prep_c, ssd_d[l], ssd_norm_g[l], h_zero, h_zero)
            yc_hy = hyena_branch(pc[..., :HY_COLS], *hy_p, 1)
            ctx = ctx + cga1 * (jnp.concatenate([yc_hy, yc_ssd], axis=-1) @ w_out[l])
            hcn = modulate(rmsnorm(ctx, g_norm2[l]), csh2, csc2)
            ctx = ctx + cga2 * moe_ffn(hcn.reshape(-1, D_MODEL), *moe_p).reshape(ctx.shape)

        px = modulate(rmsnorm(x, g_norm1[l]), sh1, sc1) @ w_in[l]
        y_hy = hyena_branch(px[..., :HY_COLS], *hy_p, rows)
        prep_x = ssd_prep(px[..., HY_COLS:], *ssd_p, rows)
        y_ssd, _, _ = ssd_branch(prep_x, ssd_d[l], ssd_norm_g[l], h_f, h_b)
        x = x + ga1 * (jnp.concatenate([y_hy, y_ssd], axis=-1) @ w_out[l])

        hxn = modulate(rmsnorm(x, g_norm2[l]), sh2, sc2)
        x = x + ga2 * moe_ffn(hxn.reshape(-1, D_MODEL), *moe_p).reshape(x.shape)
    return rmsnorm(x, g_final)
```

```python
import functools
import math

import jax
import jax.numpy as jnp
from jax import lax
from jax.experimental import pallas as pl
from jax.experimental.pallas import tpu as pltpu

D_MODEL = 1024
CTX_LEN = 256
GRID_W = 64
EPS = 1e-6
SHORT_CONV = 3

D_HYENA = D_MODEL // 2
HYENA_ORDER = 2
HYENA_BANDS = 8
HYENA_FAST_DECAY = 0.3
HYENA_SLOW_DECAY = 1.5
HYENA_TARGET = 1e-2

D_SSD = D_MODEL // 2
SSD_HEAD_DIM = 64
SSD_HEADS = D_SSD // SSD_HEAD_DIM
SSD_GROUPS = 2
SSD_HPG = SSD_HEADS // SSD_GROUPS
SSD_STATE = 128
SSD_CHUNK = 128

D_XBC = D_SSD + 2 * SSD_GROUPS * SSD_STATE
HY_COLS = (HYENA_ORDER + 1) * D_HYENA
D_IN = HY_COLS + D_SSD + D_XBC + 2 * SSD_HEADS
LANES = 128
D_IN_PAD = -(-D_IN // LANES) * LANES

MOE_GROUPS = 8
EXPERTS_PER_GROUP = 8
N_EXPERTS = MOE_GROUPS * EXPERTS_PER_GROUP
TOP_K = 2
D_EXPERT = 512
MOE_BLOCK = 128

VMEM_LIMIT_BYTES = 56 * 1024 * 1024


def _norm_mod_matmul_kernel(x_ref, g_ref, shift_ref, scale_ref, w_ref, o_ref):
    x = x_ref[0]
    y = x * lax.rsqrt(jnp.mean(x * x, axis=-1, keepdims=True) + EPS) * g_ref[...]
    h = y * (1.0 + scale_ref[0]) + shift_ref[0]
    o_ref[0] = jnp.dot(h.astype(jnp.bfloat16), w_ref[...], preferred_element_type=jnp.float32)


def norm_mod_matmul(x, g, shift, scale, w_bf16, tm):
    bsz, L, D = x.shape
    N = w_bf16.shape[1]
    return pl.pallas_call(
        _norm_mod_matmul_kernel,
        grid=(bsz, L // tm),
        in_specs=[
            pl.BlockSpec((1, tm, D), lambda b, i: (b, i, 0)),
            pl.BlockSpec((1, D), lambda b, i: (0, 0)),
            pl.BlockSpec((1, 1, D), lambda b, i: (b, 0, 0)),
            pl.BlockSpec((1, 1, D), lambda b, i: (b, 0, 0)),
            pl.BlockSpec((D, N), lambda b, i: (0, 0)),
        ],
        out_specs=pl.BlockSpec((1, tm, N), lambda b, i: (b, i, 0)),
        out_shape=jax.ShapeDtypeStruct((bsz, L, N), jnp.float32),
        compiler_params=pltpu.CompilerParams(
            dimension_semantics=("arbitrary", "arbitrary"), vmem_limit_bytes=VMEM_LIMIT_BYTES),
    )(x, g.reshape(1, D), shift, scale, w_bf16)


def _out_proj_kernel(y_ref, x_ref, gate_ref, w_ref, o_ref):
    acc = jnp.dot(y_ref[0].astype(jnp.bfloat16), w_ref[...], preferred_element_type=jnp.float32)
    o_ref[0] = x_ref[0] + gate_ref[0] * acc


def out_proj_residual(y, x, gate, w_bf16, tm):
    bsz, L, D = x.shape
    K = y.shape[-1]
    return pl.pallas_call(
        _out_proj_kernel,
        grid=(bsz, L // tm),
        in_specs=[
            pl.BlockSpec((1, tm, K), lambda b, i: (b, i, 0)),
            pl.BlockSpec((1, tm, D), lambda b, i: (b, i, 0)),
            pl.BlockSpec((1, 1, D), lambda b, i: (b, 0, 0)),
            pl.BlockSpec((K, D), lambda b, i: (0, 0)),
        ],
        out_specs=pl.BlockSpec((1, tm, D), lambda b, i: (b, i, 0)),
        out_shape=jax.ShapeDtypeStruct((bsz, L, D), jnp.float32),
        compiler_params=pltpu.CompilerParams(
            dimension_semantics=("arbitrary", "arbitrary"), vmem_limit_bytes=VMEM_LIMIT_BYTES),
    )(y, x, gate, w_bf16)


def _expert_block_kernel(eid_ref, x_ref, w1_ref, w3_ref, w2_ref, o_ref):
    del eid_ref
    xb = x_ref[...].astype(jnp.bfloat16)
    a = jnp.dot(xb, w1_ref[0].astype(jnp.bfloat16), preferred_element_type=jnp.float32)
    b = jnp.dot(xb, w3_ref[0].astype(jnp.bfloat16), preferred_element_type=jnp.float32)
    h = (a * jax.nn.sigmoid(a)) * b
    o_ref[...] = jnp.dot(h.astype(jnp.bfloat16), w2_ref[0].astype(jnp.bfloat16),
                         preferred_element_type=jnp.float32)


def expert_blocks(buf, block_eid, w1, w3, w2):
    rows, D = buf.shape
    n_blocks = rows // MOE_BLOCK
    E, _, F = w1.shape
    grid_spec = pltpu.PrefetchScalarGridSpec(
        num_scalar_prefetch=1,
        grid=(n_blocks,),
        in_specs=[
            pl.BlockSpec((MOE_BLOCK, D), lambda i, eid: (i, 0)),
            pl.BlockSpec((1, D, F), lambda i, eid: (eid[i], 0, 0)),
            pl.BlockSpec((1, D, F), lambda i, eid: (eid[i], 0, 0)),
            pl.BlockSpec((1, F, D), lambda i, eid: (eid[i], 0, 0)),
        ],
        out_specs=pl.BlockSpec((MOE_BLOCK, D), lambda i, eid: (i, 0)),
    )
    return pl.pallas_call(
        _expert_block_kernel,
        grid_spec=grid_spec,
        out_shape=jax.ShapeDtypeStruct((rows, D), jnp.float32),
        compiler_params=pltpu.CompilerParams(
            dimension_semantics=("arbitrary",), vmem_limit_bytes=VMEM_LIMIT_BYTES),
    )(block_eid, buf, w1, w3, w2)


def _rmsnorm(x, g):
    y = x * lax.rsqrt(jnp.mean(x * x, axis=-1, keepdims=True) + EPS)
    return y * g


def _short_conv_rows(u, w, b, n_rows):
    bsz, L, C = u.shape
    row_len = L // n_rows
    pad = SHORT_CONV // 2
    ur = jnp.pad(u.reshape(bsz, n_rows, row_len, C), ((0, 0), (0, 0), (pad, pad), (0, 0)))
    y = b + sum(ur[:, :, k:k + row_len] * w[k] for k in range(SHORT_CONV))
    return y.reshape(bsz, L, C)


def _hyena_filters(L, f_w1, f_b1, f_freq, f_w2, f_b2, f_w3):
    f32 = jnp.float32
    hp = lax.Precision.HIGHEST
    pos = jnp.arange(L, dtype=f32)
    t = pos / max(L - 1, 1)
    bands = jnp.linspace(1e-4, HYENA_BANDS - 1, HYENA_BANDS, dtype=f32)
    ang = (2 * math.pi / L) * pos[:, None] * bands[None, :]
    feats = jnp.concatenate([t[:, None], jnp.cos(ang), -jnp.sin(ang)], axis=-1)
    h = jnp.sin(f_freq * (jnp.dot(feats, f_w1, precision=hp) + f_b1))
    h = jnp.sin(f_freq * (jnp.dot(h, f_w2, precision=hp) + f_b2))
    k = jnp.dot(h, f_w3, precision=hp).reshape(L, HYENA_ORDER, 2, D_HYENA)
    deltas = jnp.abs(jnp.linspace(math.log(HYENA_TARGET) / HYENA_SLOW_DECAY,
                                  math.log(HYENA_TARGET) / HYENA_FAST_DECAY, D_HYENA, dtype=f32))
    window = jnp.exp(-t[:, None] * deltas[None, :])
    return k * window[:, None, None, :]


def _long_conv_bidir(z, k_f, k_b, bias):
    L = z.shape[1]
    k_full = jnp.concatenate([k_f, jnp.zeros_like(k_f[:1]), jnp.flip(k_b[1:], axis=0)], axis=0)
    zf = jnp.fft.rfft(z, n=2 * L, axis=1)
    kf = jnp.fft.rfft(k_full, axis=0)
    y = jnp.fft.irfft(zf * kf[None], n=2 * L, axis=1)[:, :L]
    return y + z * bias


def _hyena_branch(p, conv_w, conv_b, f_w1, f_b1, f_freq, f_w2, f_b2, f_w3, h_bias, n_rows):
    L = p.shape[1]
    u = _short_conv_rows(p, conv_w, conv_b, n_rows)
    parts = jnp.split(u, HYENA_ORDER + 1, axis=-1)
    k = _hyena_filters(L, f_w1, f_b1, f_freq, f_w2, f_b2, f_w3)
    z = parts[0]
    for o in range(HYENA_ORDER):
        z = parts[o + 1] * _long_conv_bidir(z, k[:, o, 0], k[:, o, 1], h_bias[o])
    return z


def _ssd_prep(p, conv_w, conv_b, a_log, dt_bias, n_rows):
    bsz, L, _ = p.shape
    G, R, P, N = SSD_GROUPS, SSD_HPG, SSD_HEAD_DIM, SSD_STATE
    z = p[..., :D_SSD]
    xbc = jax.nn.silu(_short_conv_rows(p[..., D_SSD:D_SSD + D_XBC], conv_w, conv_b, n_rows))
    xs = xbc[..., :D_SSD].reshape(bsz, L, G, R, P)
    bm = xbc[..., D_SSD:D_SSD + G * N].reshape(bsz, L, G, N)
    cm = xbc[..., D_SSD + G * N:].reshape(bsz, L, G, N)
    dt_raw = p[..., D_SSD + D_XBC:D_SSD + D_XBC + 2 * SSD_HEADS].reshape(bsz, L, 2, G, R)
    dt = jax.nn.softplus(dt_raw + dt_bias.reshape(2, G, R))
    a = -jnp.exp(a_log).reshape(2, G, R) * dt
    return z, xs, bm, cm, dt, a


def _ssd_scan(xs, a, bm, cm, h0):
    bsz, L, G, R, P = xs.shape
    N = bm.shape[-1]
    nc = L // SSD_CHUNK
    xs = xs.reshape(bsz, nc, SSD_CHUNK, G, R, P)
    a = a.reshape(bsz, nc, SSD_CHUNK, G, R)
    bm = bm.reshape(bsz, nc, SSD_CHUNK, G, N)
    cm = cm.reshape(bsz, nc, SSD_CHUNK, G, N)
    a_cum = jnp.cumsum(a, axis=2)
    lower = jnp.tril(jnp.ones((SSD_CHUNK, SSD_CHUNK), bool))[:, :, None, None]
    seg = a_cum[:, :, :, None] - a_cum[:, :, None, :]
    decay = jnp.exp(jnp.where(lower, seg, -jnp.inf))
    cb = jnp.einsum('bclgn,bcsgn->bclsg', cm, bm)
    y_diag = jnp.einsum('bclsg,bclsgr,bcsgrp->bclgrp', cb, decay, xs)
    to_end = jnp.exp(a_cum[:, :, -1:] - a_cum)
    chunk_states = jnp.einsum('bclgn,bclgr,bclgrp->bcgrpn', bm, to_end, xs)
    chunk_decay = jnp.exp(a_cum[:, :, -1])

    def step(h, inp):
        s_c, d_c = inp
        return h * d_c[..., None, None] + s_c, h

    h_final, h_in = lax.scan(step, h0, (jnp.moveaxis(chunk_states, 1, 0), jnp.moveaxis(chunk_decay, 1, 0)))
    h_in = jnp.moveaxis(h_in, 0, 1)
    y_off = jnp.einsum('bclgn,bcgrpn,bclgr->bclgrp', cm, h_in, jnp.exp(a_cum))
    return (y_diag + y_off).reshape(bsz, L, G, R, P), h_final


def _ssd_final_state(xs, a, bm):
    a_cum = jnp.cumsum(a, axis=1)
    w = jnp.exp(a_cum[:, -1:] - a_cum)
    return jnp.einsum('blgn,blgr,blgrp->bgrpn', bm, w, xs)


def _rev(t):
    return jnp.flip(t, axis=1)


def _ssd_context_states(prep):
    _, xs, bm, _, dt, a = prep
    h_f = _ssd_final_state(xs * dt[:, :, 0, :, :, None], a[:, :, 0], bm)
    h_b = _ssd_final_state(_rev(xs * dt[:, :, 1, :, :, None]), _rev(a[:, :, 1]), _rev(bm))
    return h_f, h_b


def _ssd_branch(prep, d_skip, norm_g, h0_f, h0_b):
    z, xs, bm, cm, dt, a = prep
    bsz, L = z.shape[:2]
    y_f, _ = _ssd_scan(xs * dt[:, :, 0, :, :, None], a[:, :, 0], bm, cm, h0_f)
    y_b, _ = _ssd_scan(_rev(xs * dt[:, :, 1, :, :, None]), _rev(a[:, :, 1]), _rev(bm), _rev(cm), h0_b)
    y = y_f + _rev(y_b) + d_skip.reshape(SSD_GROUPS, SSD_HPG)[..., None] * xs
    y = y.reshape(bsz, L, D_SSD) * jax.nn.silu(z)
    y = y.reshape(bsz, L, SSD_GROUPS, D_SSD // SSD_GROUPS)
    y = y * lax.rsqrt(jnp.mean(y * y, axis=-1, keepdims=True) + EPS)
    return y.reshape(bsz, L, D_SSD) * norm_g


def _moe_ffn(h, w_group, b_group, w_expert, b_expert, w1, w3, w2):
    T, D = h.shape
    hp = lax.Precision.HIGHEST
    group_prob = jax.nn.softmax(jnp.dot(h, w_group, precision=hp) + b_group, axis=-1)
    p_group, g_sel = lax.top_k(group_prob, 1)
    exp_logits = (jnp.dot(h, w_expert, precision=hp) + b_expert).reshape(T, MOE_GROUPS, EXPERTS_PER_GROUP)
    in_group = exp_logits[jnp.arange(T), g_sel[:, 0]]
    top_logit, top_idx = lax.top_k(in_group, TOP_K)
    gate = jax.nn.softmax(top_logit, axis=-1) * p_group
    expert_id = g_sel * EXPERTS_PER_GROUP + top_idx
    M = T * TOP_K
    eid = expert_id.reshape(M)
    tok = jnp.repeat(jnp.arange(T), TOP_K)
    order = jnp.argsort(eid)
    eid_s, tok_s, gate_s = eid[order], tok[order], gate.reshape(M)[order]
    counts = jnp.bincount(eid, length=N_EXPERTS)
    padded = (counts + MOE_BLOCK - 1) // MOE_BLOCK * MOE_BLOCK
    ends = jnp.cumsum(padded)
    dest = (ends - padded)[eid_s] + jnp.arange(M) - (jnp.cumsum(counts) - counts)[eid_s]
    n_blocks = -(-M // MOE_BLOCK) + N_EXPERTS
    buf = jnp.zeros((n_blocks * MOE_BLOCK, D), h.dtype).at[dest].set(h[tok_s])
    block_eid = jnp.minimum(jnp.searchsorted(ends, jnp.arange(n_blocks) * MOE_BLOCK, side='right'),
                            N_EXPERTS - 1).astype(jnp.int32)
    yb = expert_blocks(buf, block_eid, w1, w3, w2)
    contrib = yb[dest] * gate_s[:, None]
    return jnp.zeros_like(h).at[tok_s].add(contrib)


def kernel(x, c, ctx, c_ctx, w_ada, b_ada, g_norm1, g_norm2, w_in, hy_conv_w, hy_conv_b, hy_f_w1, hy_f_b1, hy_f_freq, hy_f_w2, hy_f_b2, hy_f_w3, hy_bias, ssd_conv_w, ssd_conv_b, ssd_a_log, ssd_dt_bias, ssd_d, ssd_norm_g, w_out, w_group, b_group, w_expert, b_expert, w1, w3, w2, g_final):
    bsz, seq_len, _ = x.shape
    rows = seq_len // GRID_W
    l = 0
    hp = lax.Precision.HIGHEST
    hy_p = (hy_conv_w[l], hy_conv_b[l], hy_f_w1[l], hy_f_b1[l], hy_f_freq[l],
            hy_f_w2[l], hy_f_b2[l], hy_f_w3[l], hy_bias[l])
    ssd_p = (ssd_conv_w[l], ssd_conv_b[l], ssd_a_log[l], ssd_dt_bias[l])
    moe_p = (w_group[l], b_group[l], w_expert[l], b_expert[l], w1[l], w3[l], w2[l])
    mod = (jnp.dot(jax.nn.silu(c), w_ada[l], precision=hp) + b_ada[l])[:, None, :]
    sh1, sc1, ga1, sh2, sc2, ga2 = jnp.split(mod, 6, axis=-1)
    cmod = jnp.dot(jax.nn.silu(c_ctx), w_ada[l], precision=hp) + b_ada[l]
    csh1, csc1, _, _, _, _ = jnp.split(cmod, 6, axis=-1)

    w_in_bf = jnp.pad(w_in[l].astype(jnp.bfloat16), ((0, 0), (0, D_IN_PAD - D_IN)))
    w_out_bf = w_out[l].astype(jnp.bfloat16)

    ones = jnp.ones((bsz, 1, 1), jnp.float32)
    pc = norm_mod_matmul(ctx, g_norm1[l], csh1[None, None, :] * ones, csc1[None, None, :] * ones,
                         w_in_bf[:, HY_COLS:], 256)
    prep_c = _ssd_prep(pc, *ssd_p, 1)
    h_f, h_b = _ssd_context_states(prep_c)

    px = norm_mod_matmul(x, g_norm1[l], sh1, sc1, w_in_bf, 512)
    y_hy = _hyena_branch(px[..., :HY_COLS], *hy_p, rows)
    prep_x = _ssd_prep(px[..., HY_COLS:], *ssd_p, rows)
    y_ssd = _ssd_branch(prep_x, ssd_d[l], ssd_norm_g[l], h_f, h_b)
    x = out_proj_residual(jnp.concatenate([y_hy, y_ssd], axis=-1), x, ga1, w_out_bf, 512)

    hxn = _rmsnorm(x, g_norm2[l]) * (1 + sc2) + sh2
    x = x + ga2 * _moe_ffn(hxn.reshape(-1, D_MODEL), *moe_p).reshape(x.shape)
    return _rmsnorm(x, g_final)
```

```python
import functools
import math

import jax
import jax.numpy as jnp
from jax import lax
from jax.experimental import pallas as pl
from jax.experimental.pallas import tpu as pltpu

D_MODEL = 1024
CTX_LEN = 256
GRID_W = 64
EPS = 1e-6
SHORT_CONV = 3

D_HYENA = D_MODEL // 2
HYENA_ORDER = 2
HYENA_BANDS = 8
HYENA_FAST_DECAY = 0.3
HYENA_SLOW_DECAY = 1.5
HYENA_TARGET = 1e-2

D_SSD = D_MODEL // 2
SSD_HEAD_DIM = 64
SSD_HEADS = D_SSD // SSD_HEAD_DIM
SSD_GROUPS = 2
SSD_HPG = SSD_HEADS // SSD_GROUPS
SSD_STATE = 128
SSD_CHUNK = 128

D_XBC = D_SSD + 2 * SSD_GROUPS * SSD_STATE
HY_COLS = (HYENA_ORDER + 1) * D_HYENA
D_IN = HY_COLS + D_SSD + D_XBC + 2 * SSD_HEADS
LANES = 128
D_IN_PAD = -(-D_IN // LANES) * LANES

MOE_GROUPS = 8
EXPERTS_PER_GROUP = 8
N_EXPERTS = MOE_GROUPS * EXPERTS_PER_GROUP
TOP_K = 2
D_EXPERT = 512
MOE_BLOCK = 128

VMEM_LIMIT_BYTES = 56 * 1024 * 1024
NEG_BIG = -1e30


def _norm_mod_matmul_kernel(x_ref, g_ref, shift_ref, scale_ref, w_ref, o_ref):
    x = x_ref[0]
    y = x * lax.rsqrt(jnp.mean(x * x, axis=-1, keepdims=True) + EPS) * g_ref[...]
    h = y * (1.0 + scale_ref[0]) + shift_ref[0]
    o_ref[0] = jnp.dot(h.astype(jnp.bfloat16), w_ref[...], preferred_element_type=jnp.float32)


def norm_mod_matmul(x, g, shift, scale, w_bf16, tm):
    bsz, L, D = x.shape
    N = w_bf16.shape[1]
    return pl.pallas_call(
        _norm_mod_matmul_kernel,
        grid=(bsz, L // tm),
        in_specs=[
            pl.BlockSpec((1, tm, D), lambda b, i: (b, i, 0)),
            pl.BlockSpec((1, D), lambda b, i: (0, 0)),
            pl.BlockSpec((1, 1, D), lambda b, i: (b, 0, 0)),
            pl.BlockSpec((1, 1, D), lambda b, i: (b, 0, 0)),
            pl.BlockSpec((D, N), lambda b, i: (0, 0)),
        ],
        out_specs=pl.BlockSpec((1, tm, N), lambda b, i: (b, i, 0)),
        out_shape=jax.ShapeDtypeStruct((bsz, L, N), jnp.float32),
        compiler_params=pltpu.CompilerParams(
            dimension_semantics=("arbitrary", "arbitrary"), vmem_limit_bytes=VMEM_LIMIT_BYTES),
    )(x, g.reshape(1, D), shift, scale, w_bf16)


def _out_proj_kernel(y_ref, x_ref, gate_ref, w_ref, o_ref):
    acc = jnp.dot(y_ref[0].astype(jnp.bfloat16), w_ref[...], preferred_element_type=jnp.float32)
    o_ref[0] = x_ref[0] + gate_ref[0] * acc


def out_proj_residual(y, x, gate, w_bf16, tm):
    bsz, L, D = x.shape
    K = y.shape[-1]
    return pl.pallas_call(
        _out_proj_kernel,
        grid=(bsz, L // tm),
        in_specs=[
            pl.BlockSpec((1, tm, K), lambda b, i: (b, i, 0)),
            pl.BlockSpec((1, tm, D), lambda b, i: (b, i, 0)),
            pl.BlockSpec((1, 1, D), lambda b, i: (b, 0, 0)),
            pl.BlockSpec((K, D), lambda b, i: (0, 0)),
        ],
        out_specs=pl.BlockSpec((1, tm, D), lambda b, i: (b, i, 0)),
        out_shape=jax.ShapeDtypeStruct((bsz, L, D), jnp.float32),
        compiler_params=pltpu.CompilerParams(
            dimension_semantics=("arbitrary", "arbitrary"), vmem_limit_bytes=VMEM_LIMIT_BYTES),
    )(y, x, gate, w_bf16)


def _expert_block_kernel(eid_ref, x_ref, w1_ref, w3_ref, w2_ref, o_ref):
    del eid_ref
    xb = x_ref[...].astype(jnp.bfloat16)
    a = jnp.dot(xb, w1_ref[0].astype(jnp.bfloat16), preferred_element_type=jnp.float32)
    b = jnp.dot(xb, w3_ref[0].astype(jnp.bfloat16), preferred_element_type=jnp.float32)
    h = (a * jax.nn.sigmoid(a)) * b
    o_ref[...] = jnp.dot(h.astype(jnp.bfloat16), w2_ref[0].astype(jnp.bfloat16),
                         preferred_element_type=jnp.float32)


def expert_blocks(buf, block_eid, w1, w3, w2):
    rows, D = buf.shape
    n_blocks = rows // MOE_BLOCK
    E, _, F = w1.shape
    grid_spec = pltpu.PrefetchScalarGridSpec(
        num_scalar_prefetch=1,
        grid=(n_blocks,),
        in_specs=[
            pl.BlockSpec((MOE_BLOCK, D), lambda i, eid: (i, 0)),
            pl.BlockSpec((1, D, F), lambda i, eid: (eid[i], 0, 0)),
            pl.BlockSpec((1, D, F), lambda i, eid: (eid[i], 0, 0)),
            pl.BlockSpec((1, F, D), lambda i, eid: (eid[i], 0, 0)),
        ],
        out_specs=pl.BlockSpec((MOE_BLOCK, D), lambda i, eid: (i, 0)),
    )
    return pl.pallas_call(
        _expert_block_kernel,
        grid_spec=grid_spec,
        out_shape=jax.ShapeDtypeStruct((rows, D), jnp.float32),
        compiler_params=pltpu.CompilerParams(
            dimension_semantics=("arbitrary",), vmem_limit_bytes=VMEM_LIMIT_BYTES),
    )(block_eid, buf, w1, w3, w2)


def dft_tables(L):
    n = 2 * L
    f = lax.broadcasted_iota(jnp.int32, (L, L), 0)
    t = lax.broadcasted_iota(jnp.int32, (L, L), 1)
    ang = ((f * t) % n).astype(jnp.float32) * (2.0 * math.pi / n)
    return jnp.cos(ang).astype(jnp.bfloat16), jnp.sin(ang).astype(jnp.bfloat16)


def _alt_sign(L):
    t = lax.broadcasted_iota(jnp.int32, (L, 1), 0)
    return (1 - 2 * (t & 1)).astype(jnp.float32)


def _spectrum_kernel(a_ref, b_ref, c_ref, s_ref, kr_ref, ks_ref, kn_ref):
    L = a_ref.shape[1]
    a = a_ref[0]
    row = lax.broadcasted_iota(jnp.int32, (L, 1), 0)
    scale = jnp.where(row == 0, 0.5 / L, 1.0 / L)
    kr_ref[0] = scale * jnp.dot(c_ref[...], a.astype(jnp.bfloat16), preferred_element_type=jnp.float32)
    ks_ref[0] = scale * jnp.dot(s_ref[...], b_ref[0].astype(jnp.bfloat16), preferred_element_type=jnp.float32)
    kn_ref[0] = jnp.sum(a * _alt_sign(L), axis=0, keepdims=True) * (0.5 / L)


def filter_spectrum(a, b, cos_t, sin_t, tc):
    n, L, C = a.shape
    blk = pl.BlockSpec((1, L, tc), lambda o, j: (o, 0, j))
    tab = pl.BlockSpec((L, L), lambda o, j: (0, 0))
    return pl.pallas_call(
        _spectrum_kernel,
        grid=(n, C // tc),
        in_specs=[blk, blk, tab, tab],
        out_specs=[blk, blk, pl.BlockSpec((1, 1, tc), lambda o, j: (o, 0, j))],
        out_shape=[jax.ShapeDtypeStruct((n, L, C), jnp.float32)] * 2 + [jax.ShapeDtypeStruct((n, 1, C), jnp.float32)],
        compiler_params=pltpu.CompilerParams(
            dimension_semantics=("arbitrary", "arbitrary"), vmem_limit_bytes=VMEM_LIMIT_BYTES),
    )(a, b, cos_t, sin_t)


def _long_conv_kernel(z_ref, xn_ref, kr_ref, ks_ref, kn_ref, bias_ref, c_ref, s_ref, o_ref, acc_ref, *, tf):
    L = z_ref.shape[1]
    z = z_ref[0]
    zb = z.astype(jnp.bfloat16)
    sign = _alt_sign(L)
    z_nyq = jnp.sum(z * sign, axis=0, keepdims=True)
    acc_ref[...] = z * bias_ref[0] + sign * (z_nyq * kn_ref[0])
    for ft in range(L // tf):
        rows = pl.ds(ft * tf, tf)
        zr = jnp.dot(c_ref[rows, :], zb, preferred_element_type=jnp.float32)
        zs = jnp.dot(s_ref[rows, :], zb, preferred_element_type=jnp.float32)
        kr = kr_ref[0, rows, :]
        ks = ks_ref[0, rows, :]
        yr = (zr * kr - zs * ks).astype(jnp.bfloat16)
        ys = (zr * ks + zs * kr).astype(jnp.bfloat16)
        acc_ref[...] += (jnp.dot(c_ref[:, rows], yr, preferred_element_type=jnp.float32)
                         + jnp.dot(s_ref[:, rows], ys, preferred_element_type=jnp.float32))
    o_ref[0] = xn_ref[0] * acc_ref[...]


def long_conv_gate(z_arr, z_col, xn_arr, xn_col, kr, ks, kn, bias, order, cos_t, sin_t, tc, tf):
    bsz, L, _ = z_arr.shape
    C = kr.shape[-1]
    nj = C // tc
    tab = pl.BlockSpec((L, L), lambda j, b: (0, 0), pipeline_mode=pl.Buffered(1))
    spec = pl.BlockSpec((1, L, tc), lambda j, b: (order, 0, j), pipeline_mode=pl.Buffered(1))
    vec = pl.BlockSpec((1, 1, tc), lambda j, b: (order, 0, j))
    return pl.pallas_call(
        functools.partial(_long_conv_kernel, tf=tf),
        grid=(nj, bsz),
        in_specs=[
            pl.BlockSpec((1, L, tc), lambda j, b: (b, 0, z_col * nj + j)),
            pl.BlockSpec((1, L, tc), lambda j, b: (b, 0, xn_col * nj + j)),
            spec, spec, vec, vec, tab, tab,
        ],
        out_specs=pl.BlockSpec((1, L, tc), lambda j, b: (b, 0, j)),
        out_shape=jax.ShapeDtypeStruct((bsz, L, C), jnp.float32),
        scratch_shapes=[pltpu.VMEM((L, tc), jnp.float32)],
        compiler_params=pltpu.CompilerParams(
            dimension_semantics=("arbitrary", "arbitrary"), vmem_limit_bytes=VMEM_LIMIT_BYTES),
    )(z_arr, xn_arr, kr, ks, kn, bias, cos_t, sin_t)


def hyena_long_convs(u, k, h_bias, tc, tf):
    L = u.shape[1]
    cos_t, sin_t = dft_tables(L)
    k_f = jnp.moveaxis(k[:, :, 0], 1, 0)
    k_b = jnp.moveaxis(k[:, :, 1], 1, 0).at[:, 0].set(0.0)
    kr, ks, kn = filter_spectrum(k_f + k_b, k_f - k_b, cos_t, sin_t, tc)
    bias = h_bias[:, None, :]
    z1 = long_conv_gate(u, 0, u, 1, kr, ks, kn, bias, 0, cos_t, sin_t, tc, tf)
    return long_conv_gate(z1, 0, u, 2, kr, ks, kn, bias, 1, cos_t, sin_t, tc, tf)


def _ssd_kernel(xf_ref, df_ref, xb_ref, db_ref, dskip_ref, y_ref, h_ref, *, n_ctx_chunks):
    s = pl.program_id(1)
    n_steps = pl.num_programs(1)
    Q, G, R, P, N = SSD_CHUNK, SSD_GROUPS, SSD_HPG, SSD_HEAD_DIM, SSD_STATE
    GP = R * P
    bf = jnp.bfloat16

    @pl.when(s == 0)
    def _():
        h_ref[...] = jnp.zeros_like(h_ref)
        y_ref[...] = jnp.zeros_like(y_ref)

    row = lax.broadcasted_iota(jnp.int32, (Q, Q), 0)
    col = lax.broadcasted_iota(jnp.int32, (Q, Q), 1)
    lane_head = lax.broadcasted_iota(jnp.int32, (Q, GP), 1) // P
    is_latent = s >= n_ctx_chunks
    out_chunk = (s - n_ctx_chunks, n_steps - 1 - s)

    for d, (x_ref, da_ref) in enumerate(((xf_ref, df_ref), (xb_ref, db_ref))):
        mask = (row >= col) if d == 0 else (col >= row)
        tri = mask.astype(jnp.float32)
        da = da_ref[0]
        cum = jnp.dot(tri, da, precision=lax.Precision.HIGHEST, preferred_element_type=jnp.float32)
        cum_t = cum.T
        edge = Q - 1 if d == 0 else 0
        blk = x_ref.at[0]
        for g in range(G):
            xg = blk[:, g * GP:(g + 1) * GP]
            bg = blk[:, D_SSD + g * N:D_SSD + (g + 1) * N].astype(bf)
            cg = blk[:, D_SSD + G * N + g * N:D_SSD + G * N + (g + 1) * N].astype(bf)
            heads = [d * SSD_HEADS + g * R + r for r in range(R)]
            dtm = jnp.zeros((Q, GP), jnp.float32)
            cumm = jnp.zeros((Q, GP), jnp.float32)
            totm = jnp.zeros((Q, GP), jnp.float32)
            for r, h in enumerate(heads):
                sel = lane_head == r
                dtm = jnp.where(sel, da[:, h:h + 1], dtm)
                cumm = jnp.where(sel, cum[:, SSD_HEADS * 2 + h:SSD_HEADS * 2 + h + 1], cumm)
                totm = jnp.where(sel, cum[edge:edge + 1, SSD_HEADS * 2 + h:SSD_HEADS * 2 + h + 1], totm)
            xdt = xg * dtm
            hg = h_ref[d, g * GP:(g + 1) * GP, :]

            @pl.when(is_latent)
            def _():
                gmat = lax.dot_general(cg, bg, (((1,), (1,)), ((), ())), preferred_element_type=jnp.float32)
                y_off = lax.dot_general(cg, hg.astype(bf), (((1,), (1,)), ((), ())),
                                        preferred_element_type=jnp.float32) * jnp.exp(cumm)
                if d == 0:
                    y_off = y_off + dskip_ref[:, g * GP:(g + 1) * GP] * xg
                parts = []
                for r, h in enumerate(heads):
                    a_col = cum[:, SSD_HEADS * 2 + h:SSD_HEADS * 2 + h + 1]
                    a_row = cum_t[SSD_HEADS * 2 + h:SSD_HEADS * 2 + h + 1, :]
                    decay = jnp.exp(jnp.where(mask, a_col - a_row, NEG_BIG))
                    parts.append(jnp.dot((gmat * decay).astype(bf), xdt[:, r * P:(r + 1) * P].astype(bf),
                                         preferred_element_type=jnp.float32))
                y = y_off + jnp.concatenate(parts, axis=-1)
                rows = pl.ds(pl.multiple_of(out_chunk[d] * Q, Q), Q)
                y_ref[0, rows, g * GP:(g + 1) * GP] += y

            xw = (xdt * jnp.exp(totm - cumm)).astype(bf)
            st = lax.dot_general(xw, bg, (((0,), (0,)), ((), ())), preferred_element_type=jnp.float32)
            for r, h in enumerate(heads):
                dec = jnp.exp(cum_t[SSD_HEADS * 2 + h:SSD_HEADS * 2 + h + 1, edge:edge + 1])
                rs = slice(g * GP + r * P, g * GP + (r + 1) * P)
                h_ref[d, rs, :] = h_ref[d, rs, :] * dec + st[r * P:(r + 1) * P, :]


def ssd_scan_bidir(xbc, dta, d_skip, n_ctx):
    bsz, lt, width = xbc.shape
    Q = SSD_CHUNK
    n_steps = lt // Q
    n_ctx_chunks = n_ctx // Q
    L = lt - n_ctx

    def bwd_chunk(s):
        return jnp.where(s < n_ctx_chunks, n_ctx_chunks - 1 - s, n_steps - 1 - s + n_ctx_chunks)

    return pl.pallas_call(
        functools.partial(_ssd_kernel, n_ctx_chunks=n_ctx_chunks),
        grid=(bsz, n_steps),
        in_specs=[
            pl.BlockSpec((1, Q, width), lambda b, s: (b, s, 0)),
            pl.BlockSpec((1, Q, LANES), lambda b, s: (b, s, 0)),
            pl.BlockSpec((1, Q, width), lambda b, s: (b, bwd_chunk(s), 0)),
            pl.BlockSpec((1, Q, LANES), lambda b, s: (b, bwd_chunk(s), 0)),
            pl.BlockSpec((1, D_SSD), lambda b, s: (0, 0)),
        ],
        out_specs=pl.BlockSpec((1, L, D_SSD), lambda b, s: (b, 0, 0)),
        out_shape=jax.ShapeDtypeStruct((bsz, L, D_SSD), jnp.float32),
        scratch_shapes=[pltpu.VMEM((2, SSD_GROUPS * SSD_HPG * SSD_HEAD_DIM, SSD_STATE), jnp.float32)],
        compiler_params=pltpu.CompilerParams(
            dimension_semantics=("arbitrary", "arbitrary"), vmem_limit_bytes=VMEM_LIMIT_BYTES),
    )(xbc, dta, xbc, dta, d_skip)


def _rmsnorm(x, g):
    y = x * lax.rsqrt(jnp.mean(x * x, axis=-1, keepdims=True) + EPS)
    return y * g


def _short_conv_rows(u, w, b, n_rows):
    bsz, L, C = u.shape
    row_len = L // n_rows
    pad = SHORT_CONV // 2
    ur = jnp.pad(u.reshape(bsz, n_rows, row_len, C), ((0, 0), (0, 0), (pad, pad), (0, 0)))
    y = b + sum(ur[:, :, k:k + row_len] * w[k] for k in range(SHORT_CONV))
    return y.reshape(bsz, L, C)


def _hyena_filters(L, f_w1, f_b1, f_freq, f_w2, f_b2, f_w3):
    f32 = jnp.float32
    hp = lax.Precision.HIGHEST
    pos = jnp.arange(L, dtype=f32)
    t = pos / max(L - 1, 1)
    bands = jnp.linspace(1e-4, HYENA_BANDS - 1, HYENA_BANDS, dtype=f32)
    ang = (2 * math.pi / L) * pos[:, None] * bands[None, :]
    feats = jnp.concatenate([t[:, None], jnp.cos(ang), -jnp.sin(ang)], axis=-1)
    h = jnp.sin(f_freq * (jnp.dot(feats, f_w1, precision=hp) + f_b1))
    h = jnp.sin(f_freq * (jnp.dot(h, f_w2, precision=hp) + f_b2))
    k = jnp.dot(h, f_w3, precision=hp).reshape(L, HYENA_ORDER, 2, D_HYENA)
    deltas = jnp.abs(jnp.linspace(math.log(HYENA_TARGET) / HYENA_SLOW_DECAY,
                                  math.log(HYENA_TARGET) / HYENA_FAST_DECAY, D_HYENA, dtype=f32))
    window = jnp.exp(-t[:, None] * deltas[None, :])
    return k * window[:, None, None, :]


def _hyena_branch(p, conv_w, conv_b, f_w1, f_b1, f_freq, f_w2, f_b2, f_w3, h_bias, n_rows):
    L = p.shape[1]
    u = _short_conv_rows(p, conv_w, conv_b, n_rows)
    k = _hyena_filters(L, f_w1, f_b1, f_freq, f_w2, f_b2, f_w3)
    return hyena_long_convs(u, k, h_bias, 256, 512)


def _ssd_prep(p, conv_w, conv_b, a_log, dt_bias, n_rows):
    bsz, L, _ = p.shape
    xbc = jax.nn.silu(_short_conv_rows(p[..., D_SSD:D_SSD + D_XBC], conv_w, conv_b, n_rows))
    dt_raw = p[..., D_SSD + D_XBC:D_SSD + D_XBC + 2 * SSD_HEADS]
    dt = jax.nn.softplus(dt_raw + dt_bias.reshape(2 * SSD_HEADS))
    a = -jnp.exp(a_log).reshape(2 * SSD_HEADS) * dt
    dta = jnp.concatenate([dt, a, jnp.zeros((bsz, L, LANES - 4 * SSD_HEADS), jnp.float32)], axis=-1)
    return xbc, dta


def _ssd_finish(y, z, norm_g):
    bsz, L = z.shape[:2]
    y = y * jax.nn.silu(z)
    y = y.reshape(bsz, L, SSD_GROUPS, D_SSD // SSD_GROUPS)
    y = y * lax.rsqrt(jnp.mean(y * y, axis=-1, keepdims=True) + EPS)
    return y.reshape(bsz, L, D_SSD) * norm_g


def _moe_ffn(h, w_group, b_group, w_expert, b_expert, w1, w3, w2):
    T, D = h.shape
    hp = lax.Precision.HIGHEST
    group_prob = jax.nn.softmax(jnp.dot(h, w_group, precision=hp) + b_group, axis=-1)
    p_group, g_sel = lax.top_k(group_prob, 1)
    exp_logits = (jnp.dot(h, w_expert, precision=hp) + b_expert).reshape(T, MOE_GROUPS, EXPERTS_PER_GROUP)
    in_group = exp_logits[jnp.arange(T), g_sel[:, 0]]
    top_logit, top_idx = lax.top_k(in_group, TOP_K)
    gate = jax.nn.softmax(top_logit, axis=-1) * p_group
    expert_id = g_sel * EXPERTS_PER_GROUP + top_idx
    M = T * TOP_K
    eid = expert_id.reshape(M)
    tok = jnp.repeat(jnp.arange(T), TOP_K)
    order = jnp.argsort(eid)
    eid_s, tok_s, gate_s = eid[order], tok[order], gate.reshape(M)[order]
    counts = jnp.bincount(eid, length=N_EXPERTS)
    padded = (counts + MOE_BLOCK - 1) // MOE_BLOCK * MOE_BLOCK
    ends = jnp.cumsum(padded)
    dest = (ends - padded)[eid_s] + jnp.arange(M) - (jnp.cumsum(counts) - counts)[eid_s]
    n_blocks = -(-M // MOE_BLOCK) + N_EXPERTS
    buf = jnp.zeros((n_blocks * MOE_BLOCK, D), h.dtype).at[dest].set(h[tok_s])
    block_eid = jnp.minimum(jnp.searchsorted(ends, jnp.arange(n_blocks) * MOE_BLOCK, side='right'),
                            N_EXPERTS - 1).astype(jnp.int32)
    yb = expert_blocks(buf, block_eid, w1, w3, w2)
    contrib = yb[dest] * gate_s[:, None]
    return jnp.zeros_like(h).at[tok_s].add(contrib)


def kernel(x, c, ctx, c_ctx, w_ada, b_ada, g_norm1, g_norm2, w_in, hy_conv_w, hy_conv_b, hy_f_w1, hy_f_b1, hy_f_freq, hy_f_w2, hy_f_b2, hy_f_w3, hy_bias, ssd_conv_w, ssd_conv_b, ssd_a_log, ssd_dt_bias, ssd_d, ssd_norm_g, w_out, w_group, b_group, w_expert, b_expert, w1, w3, w2, g_final):
    bsz, seq_len, _ = x.shape
    rows = seq_len // GRID_W
    l = 0
    hp = lax.Precision.HIGHEST
    hy_p = (hy_conv_w[l], hy_conv_b[l], hy_f_w1[l], hy_f_b1[l], hy_f_freq[l],
            hy_f_w2[l], hy_f_b2[l], hy_f_w3[l], hy_bias[l])
    ssd_p = (ssd_conv_w[l], ssd_conv_b[l], ssd_a_log[l], ssd_dt_bias[l])
    moe_p = (w_group[l], b_group[l], w_expert[l], b_expert[l], w1[l], w3[l], w2[l])
    mod = (jnp.dot(jax.nn.silu(c), w_ada[l], precision=hp) + b_ada[l])[:, None, :]
    sh1, sc1, ga1, sh2, sc2, ga2 = jnp.split(mod, 6, axis=-1)
    cmod = jnp.dot(jax.nn.silu(c_ctx), w_ada[l], precision=hp) + b_ada[l]
    csh1, csc1, _, _, _, _ = jnp.split(cmod, 6, axis=-1)

    w_in_bf = jnp.pad(w_in[l].astype(jnp.bfloat16), ((0, 0), (0, D_IN_PAD - D_IN)))
    w_out_bf = w_out[l].astype(jnp.bfloat16)

    ones = jnp.ones((bsz, 1, 1), jnp.float32)
    pc = norm_mod_matmul(ctx, g_norm1[l], csh1[None, None, :] * ones, csc1[None, None, :] * ones,
                         w_in_bf[:, HY_COLS:], 256)
    xbc_c, dta_c = _ssd_prep(pc, *ssd_p, 1)

    px = norm_mod_matmul(x, g_norm1[l], sh1, sc1, w_in_bf, 512)
    y_hy = _hyena_branch(px[..., :HY_COLS], *hy_p, rows)
    xbc_x, dta_x = _ssd_prep(px[..., HY_COLS:], *ssd_p, rows)
    y_scan = ssd_scan_bidir(jnp.concatenate([xbc_c, xbc_x], axis=1), jnp.concatenate([dta_c, dta_x], axis=1),
                            jnp.repeat(ssd_d[l], SSD_HEAD_DIM)[None, :], CTX_LEN)
    y_ssd = _ssd_finish(y_scan, px[..., HY_COLS:HY_COLS + D_SSD], ssd_norm_g[l])
    x = out_proj_residual(jnp.concatenate([y_hy, y_ssd], axis=-1), x, ga1, w_out_bf, 512)

    hxn = _rmsnorm(x, g_norm2[l]) * (1 + sc2) + sh2
    x = x + ga2 * _moe_ffn(hxn.reshape(-1, D_MODEL), *moe_p).reshape(x.shape)
    return _rmsnorm(x, g_final)
```

```python
import functools
import math

import jax
import jax.numpy as jnp
from jax import lax
from jax.experimental import pallas as pl
from jax.experimental.pallas import tpu as pltpu

D_MODEL = 1024
CTX_LEN = 256
GRID_W = 64
EPS = 1e-6
SHORT_CONV = 3

D_HYENA = D_MODEL // 2
HYENA_ORDER = 2
HYENA_BANDS = 8
HYENA_FAST_DECAY = 0.3
HYENA_SLOW_DECAY = 1.5
HYENA_TARGET = 1e-2

D_SSD = D_MODEL // 2
SSD_HEAD_DIM = 64
SSD_HEADS = D_SSD // SSD_HEAD_DIM
SSD_GROUPS = 2
SSD_HPG = SSD_HEADS // SSD_GROUPS
SSD_STATE = 128
SSD_CHUNK = 128

D_XBC = D_SSD + 2 * SSD_GROUPS * SSD_STATE
HY_COLS = (HYENA_ORDER + 1) * D_HYENA
D_IN = HY_COLS + D_SSD + D_XBC + 2 * SSD_HEADS
LANES = 128
D_IN_PAD = -(-D_IN // LANES) * LANES

MOE_GROUPS = 8
EXPERTS_PER_GROUP = 8
N_EXPERTS = MOE_GROUPS * EXPERTS_PER_GROUP
TOP_K = 2
D_EXPERT = 512
MOE_BLOCK = 128
ROUTE_COLS = 8

VMEM_LIMIT_BYTES = 56 * 1024 * 1024
NEG_BIG = -1e30


def _norm_mod_matmul_kernel(x_ref, g_ref, shift_ref, scale_ref, w_ref, o_ref):
    x = x_ref[0]
    y = x * lax.rsqrt(jnp.mean(x * x, axis=-1, keepdims=True) + EPS) * g_ref[...]
    h = y * (1.0 + scale_ref[0]) + shift_ref[0]
    o_ref[0] = jnp.dot(h.astype(jnp.bfloat16), w_ref[...], preferred_element_type=jnp.float32)


def norm_mod_matmul(x, g, shift, scale, w_bf16, tm):
    bsz, L, D = x.shape
    N = w_bf16.shape[1]
    return pl.pallas_call(
        _norm_mod_matmul_kernel,
        grid=(bsz, L // tm),
        in_specs=[
            pl.BlockSpec((1, tm, D), lambda b, i: (b, i, 0)),
            pl.BlockSpec((1, D), lambda b, i: (0, 0)),
            pl.BlockSpec((1, 1, D), lambda b, i: (b, 0, 0)),
            pl.BlockSpec((1, 1, D), lambda b, i: (b, 0, 0)),
            pl.BlockSpec((D, N), lambda b, i: (0, 0)),
        ],
        out_specs=pl.BlockSpec((1, tm, N), lambda b, i: (b, i, 0)),
        out_shape=jax.ShapeDtypeStruct((bsz, L, N), jnp.float32),
        compiler_params=pltpu.CompilerParams(
            dimension_semantics=("arbitrary", "arbitrary"), vmem_limit_bytes=VMEM_LIMIT_BYTES),
    )(x, g.reshape(1, D), shift, scale, w_bf16)


def _out_router_kernel(yh_ref, ys_ref, z_ref, x_ref, ga_ref, sc_ref, sh_ref, ng_ref, g2_ref, wo_ref, wr_ref, br_ref,
                       x1_ref, hn_ref, ri_ref, rf_ref, cnt_ref, carry_ref):
    first = jnp.logical_and(pl.program_id(0) == 0, pl.program_id(1) == 0)

    @pl.when(first)
    def _():
        carry_ref[...] = jnp.zeros_like(carry_ref)

    bf = jnp.bfloat16
    tm = x_ref.shape[1]
    dh = yh_ref.shape[2]
    z = z_ref[0]
    ys = ys_ref[0] * (z * jax.nn.sigmoid(z))
    gw = ys.shape[1] // SSD_GROUPS
    acc = jnp.dot(yh_ref[0].astype(bf), wo_ref[0:dh, :], preferred_element_type=jnp.float32)
    for g in range(SSD_GROUPS):
        yg = ys[:, g * gw:(g + 1) * gw]
        yg = yg * lax.rsqrt(jnp.mean(yg * yg, axis=-1, keepdims=True) + EPS) * ng_ref[:, g * gw:(g + 1) * gw]
        acc += jnp.dot(yg.astype(bf), wo_ref[dh + g * gw:dh + (g + 1) * gw, :], preferred_element_type=jnp.float32)
    x1 = x_ref[0] + ga_ref[0] * acc
    x1_ref[0] = x1
    hn = x1 * lax.rsqrt(jnp.mean(x1 * x1, axis=-1, keepdims=True) + EPS) * g2_ref[...]
    hn = hn * (1.0 + sc_ref[0]) + sh_ref[0]
    hn_ref[0] = hn

    logits = jnp.dot(hn, wr_ref[...], precision=lax.Precision.HIGHEST,
                     preferred_element_type=jnp.float32) + br_ref[...]
    lane = lax.broadcasted_iota(jnp.int32, (tm, LANES), 1)
    lane_f = lane.astype(jnp.float32)
    ninf = jnp.float32(-jnp.inf)
    big = jnp.float32(1e9)
    gl = jnp.where(lane < MOE_GROUPS, logits, ninf)
    gmax = jnp.max(gl, axis=-1, keepdims=True)
    p_group = 1.0 / jnp.sum(jnp.exp(gl - gmax), axis=-1, keepdims=True)
    g_sel = jnp.min(jnp.where(gl == gmax, lane_f, big), axis=-1, keepdims=True)
    e_lane = lane - MOE_GROUPS
    in_grp = jnp.logical_and(e_lane >= 0, (e_lane // EXPERTS_PER_GROUP).astype(jnp.float32) == g_sel)
    el = jnp.where(in_grp, logits, ninf)
    m1 = jnp.max(el, axis=-1, keepdims=True)
    i1 = jnp.min(jnp.where(el == m1, lane_f, big), axis=-1, keepdims=True)
    el2 = jnp.where(lane_f == i1, ninf, el)
    m2 = jnp.max(el2, axis=-1, keepdims=True)
    i2 = jnp.min(jnp.where(el2 == m2, lane_f, big), axis=-1, keepdims=True)
    t = jnp.exp(m2 - m1)
    w1 = 1.0 / (1.0 + t)
    gate1 = w1 * p_group
    gate2 = (t * w1) * p_group
    e1 = i1 - MOE_GROUPS
    e2 = i2 - MOE_GROUPS
    el_f = e_lane.astype(jnp.float32)
    oh1 = el_f == e1
    oh2 = el_f == e2
    oh = jnp.logical_or(oh1, oh2).astype(bf)
    r_i = lax.broadcasted_iota(jnp.int32, (tm, tm), 0)
    c_i = lax.broadcasted_iota(jnp.int32, (tm, tm), 1)
    before = jnp.dot((c_i < r_i).astype(bf), oh, preferred_element_type=jnp.float32) + carry_ref[...]
    rank1 = jnp.sum(jnp.where(oh1, before, 0.0), axis=-1, keepdims=True)
    rank2 = jnp.sum(jnp.where(oh2, before, 0.0), axis=-1, keepdims=True)
    carry_ref[...] += jnp.sum(oh.astype(jnp.float32), axis=0, keepdims=True)
    cnt_ref[...] = carry_ref[...]

    col = lax.broadcasted_iota(jnp.int32, (tm, ROUTE_COLS), 1)
    rec_i = jnp.where(col == 0, e1, jnp.where(col == 1, e2, jnp.where(col == 2, rank1, rank2)))
    ri_ref[0] = rec_i.astype(jnp.int32)
    rf_ref[0] = jnp.where(col == 0, gate1, gate2)


def out_proj_router(y_hy, y_scan, px, z_col, x, ga1, sc2, sh2, norm_g, g2, w_out_bf, w_router, b_router, tm):
    bsz, L, D = x.shape
    dh = y_hy.shape[-1]
    ds = y_scan.shape[-1]
    tok = lambda b, i: (b, i, 0)
    per_b = pl.BlockSpec((1, 1, D), lambda b, i: (b, 0, 0))
    const2 = lambda b, i: (0, 0)
    return pl.pallas_call(
        _out_router_kernel,
        grid=(bsz, L // tm),
        in_specs=[
            pl.BlockSpec((1, tm, dh), tok),
            pl.BlockSpec((1, tm, ds), tok),
            pl.BlockSpec((1, tm, ds), lambda b, i: (b, i, z_col)),
            pl.BlockSpec((1, tm, D), tok),
            per_b, per_b, per_b,
            pl.BlockSpec((1, ds), const2),
            pl.BlockSpec((1, D), const2),
            pl.BlockSpec((dh + ds, D), const2),
            pl.BlockSpec((D, LANES), const2),
            pl.BlockSpec((1, LANES), const2),
        ],
        out_specs=[
            pl.BlockSpec((1, tm, D), tok),
            pl.BlockSpec((1, tm, D), tok),
            pl.BlockSpec((1, tm, ROUTE_COLS), tok),
            pl.BlockSpec((1, tm, ROUTE_COLS), tok),
            pl.BlockSpec((1, LANES), const2),
        ],
        out_shape=[
            jax.ShapeDtypeStruct((bsz, L, D), jnp.float32),
            jax.ShapeDtypeStruct((bsz, L, D), jnp.float32),
            jax.ShapeDtypeStruct((bsz, L, ROUTE_COLS), jnp.int32),
            jax.ShapeDtypeStruct((bsz, L, ROUTE_COLS), jnp.float32),
            jax.ShapeDtypeStruct((1, LANES), jnp.float32),
        ],
        scratch_shapes=[pltpu.VMEM((1, LANES), jnp.float32)],
        compiler_params=pltpu.CompilerParams(
            dimension_semantics=("arbitrary", "arbitrary"), vmem_limit_bytes=VMEM_LIMIT_BYTES),
    )(y_hy, y_scan, px, x, ga1, sc2, sh2, norm_g, g2, w_out_bf, w_router, b_router)


def _row_copy(src_hbm, src_row, dst_ref, dst_row, sem):
    return pltpu.make_async_copy(src_hbm.at[pl.ds(src_row, 1), :], dst_ref.at[pl.ds(dst_row, 1), :], sem)


def _dispatch_kernel(dest_ref, valid_ref, hn_hbm, buf_hbm, zeros, sem, zsem, *, chunk):
    n_tok = hn_hbm.shape[0]
    n_chunks = n_tok // chunk
    blk = zeros.shape[0]
    n_blocks = buf_hbm.shape[0] // blk

    zeros[...] = jnp.zeros_like(zeros)

    def zero_copy(i):
        return pltpu.make_async_copy(zeros, buf_hbm.at[pl.ds(pl.multiple_of(i * blk, blk), blk), :], zsem)

    def zfill(i, carry):
        @pl.when(valid_ref[i] < blk)
        def _():
            zero_copy(i).start()
        return carry

    def zwait(i, carry):
        @pl.when(valid_ref[i] < blk)
        def _():
            zero_copy(i).wait()
        return carry

    lax.fori_loop(0, n_blocks, zfill, 0)
    lax.fori_loop(0, n_blocks, zwait, 0)

    def issue(c):
        def body(j, carry):
            t = c * chunk + j
            _row_copy(hn_hbm, t, buf_hbm, dest_ref[2 * t], sem).start()
            _row_copy(hn_hbm, t, buf_hbm, dest_ref[2 * t + 1], sem).start()
            return carry
        lax.fori_loop(0, chunk, body, 0, unroll=8)

    def drain():
        pltpu.make_async_copy(hn_hbm.at[pl.ds(0, 2 * chunk), :], buf_hbm.at[pl.ds(0, 2 * chunk), :], sem).wait()

    issue(0)

    def loop(c, carry):
        issue(c)
        drain()
        return carry

    lax.fori_loop(1, n_chunks, loop, 0)
    drain()


def moe_dispatch(hn, dest, block_valid, blk, chunk):
    T, D = hn.shape
    n_rows = block_valid.shape[0] * blk
    grid_spec = pltpu.PrefetchScalarGridSpec(
        num_scalar_prefetch=2,
        grid=(1,),
        in_specs=[pl.BlockSpec(memory_space=pl.ANY)],
        out_specs=pl.BlockSpec(memory_space=pl.ANY),
        scratch_shapes=[pltpu.VMEM((blk, D), jnp.float32), pltpu.SemaphoreType.DMA(()),
                        pltpu.SemaphoreType.DMA(())],
    )
    return pl.pallas_call(
        functools.partial(_dispatch_kernel, chunk=chunk),
        grid_spec=grid_spec,
        out_shape=jax.ShapeDtypeStruct((n_rows, D), jnp.float32),
        compiler_params=pltpu.CompilerParams(dimension_semantics=("arbitrary",), has_side_effects=True),
    )(dest, block_valid, hn)


def _expert_kernel(eid_ref, first_ref, valid_ref, x_ref, w1_ref, w3_ref, w2_ref, o_ref, w1b, w3b, w2b):
    i = pl.program_id(0)
    del eid_ref
    bf = jnp.bfloat16

    @pl.when(first_ref[i] == 1)
    def _():
        w1b[...] = w1_ref[0].astype(bf)
        w3b[...] = w3_ref[0].astype(bf)
        w2b[...] = w2_ref[0].astype(bf)

    valid = valid_ref[i]

    @pl.when(valid > 0)
    def _():
        xb = x_ref[...].astype(bf)
        a = jnp.dot(xb, w1b[...], preferred_element_type=jnp.float32)
        b = jnp.dot(xb, w3b[...], preferred_element_type=jnp.float32)
        h = (a * jax.nn.sigmoid(a)) * b
        o_ref[...] = jnp.dot(h.astype(bf), w2b[...], preferred_element_type=jnp.float32)

    @pl.when(valid <= 0)
    def _():
        o_ref[...] = jnp.zeros_like(o_ref)


def expert_blocks(buf, block_eid, block_first, block_valid, w1, w3, w2, blk):
    rows, D = buf.shape
    n_blocks = rows // blk
    E, _, F = w1.shape
    grid_spec = pltpu.PrefetchScalarGridSpec(
        num_scalar_prefetch=3,
        grid=(n_blocks,),
        in_specs=[
            pl.BlockSpec((blk, D), lambda i, eid, fi, va: (i, 0)),
            pl.BlockSpec((1, D, F), lambda i, eid, fi, va: (eid[i], 0, 0)),
            pl.BlockSpec((1, D, F), lambda i, eid, fi, va: (eid[i], 0, 0)),
            pl.BlockSpec((1, F, D), lambda i, eid, fi, va: (eid[i], 0, 0)),
        ],
        out_specs=pl.BlockSpec((blk, D), lambda i, eid, fi, va: (i, 0)),
        scratch_shapes=[pltpu.VMEM((D, F), jnp.bfloat16), pltpu.VMEM((D, F), jnp.bfloat16),
                        pltpu.VMEM((F, D), jnp.bfloat16)],
    )
    return pl.pallas_call(
        _expert_kernel,
        grid_spec=grid_spec,
        out_shape=jax.ShapeDtypeStruct((rows, D), jnp.float32),
        compiler_params=pltpu.CompilerParams(
            dimension_semantics=("arbitrary",), vmem_limit_bytes=VMEM_LIMIT_BYTES),
    )(block_eid, block_first, block_valid, buf, w1, w3, w2)


def _combine_kernel(dest_ref, x1_ref, rf_ref, ga_ref, gf_ref, yb_hbm, o_ref, ybuf, sem):
    b = pl.program_id(0)
    i = pl.program_id(1)
    n_i = pl.num_programs(1)
    tm = x1_ref.shape[1]
    step = b * n_i + i
    n_steps = pl.num_programs(0) * n_i
    slot = step % 2

    def issue(step_, slot_):
        def body(j, carry):
            t = step_ * tm + j
            _row_copy(yb_hbm, dest_ref[2 * t], ybuf.at[slot_, 0], j, sem.at[slot_]).start()
            _row_copy(yb_hbm, dest_ref[2 * t + 1], ybuf.at[slot_, 1], j, sem.at[slot_]).start()
            return carry
        lax.fori_loop(0, tm, body, 0, unroll=8)

    @pl.when(step == 0)
    def _():
        issue(0, 0)

    @pl.when(step + 1 < n_steps)
    def _():
        issue(step + 1, 1 - slot)

    pltpu.make_async_copy(yb_hbm.at[pl.ds(0, tm), :], ybuf.at[slot, 0], sem.at[slot]).wait()
    pltpu.make_async_copy(yb_hbm.at[pl.ds(0, tm), :], ybuf.at[slot, 1], sem.at[slot]).wait()
    rf = rf_ref[0]
    y = rf[:, 0:1] * ybuf[slot, 0] + rf[:, 1:2] * ybuf[slot, 1]
    x2 = x1_ref[0] + ga_ref[0] * y
    o_ref[0] = x2 * lax.rsqrt(jnp.mean(x2 * x2, axis=-1, keepdims=True) + EPS) * gf_ref[...]


def moe_combine(x1, route_f, ga2, g_final, yb, dest, tm):
    bsz, L, D = x1.shape
    grid_spec = pltpu.PrefetchScalarGridSpec(
        num_scalar_prefetch=1,
        grid=(bsz, L // tm),
        in_specs=[
            pl.BlockSpec((1, tm, D), lambda b, i, d: (b, i, 0)),
            pl.BlockSpec((1, tm, ROUTE_COLS), lambda b, i, d: (b, i, 0)),
            pl.BlockSpec((1, 1, D), lambda b, i, d: (b, 0, 0)),
            pl.BlockSpec((1, D), lambda b, i, d: (0, 0)),
            pl.BlockSpec(memory_space=pl.ANY),
        ],
        out_specs=pl.BlockSpec((1, tm, D), lambda b, i, d: (b, i, 0)),
        scratch_shapes=[pltpu.VMEM((2, 2, tm, D), jnp.float32), pltpu.SemaphoreType.DMA((2,))],
    )
    return pl.pallas_call(
        _combine_kernel,
        grid_spec=grid_spec,
        out_shape=jax.ShapeDtypeStruct((bsz, L, D), jnp.float32),
        compiler_params=pltpu.CompilerParams(
            dimension_semantics=("arbitrary", "arbitrary"), vmem_limit_bytes=VMEM_LIMIT_BYTES),
    )(dest, x1, route_f, ga2, g_final, yb)


def moe_plan(route_i, counts, blk, n_blocks):
    cnt = counts[0, MOE_GROUPS:MOE_GROUPS + N_EXPERTS].astype(jnp.int32)
    padded = (cnt + blk - 1) // blk * blk
    ends = jnp.cumsum(padded)
    starts = ends - padded
    e = route_i[..., 0:2].reshape(-1, 2)
    rank = route_i[..., 2:4].reshape(-1, 2)
    dest = (starts[e] + rank).reshape(-1)
    first_row = jnp.arange(n_blocks, dtype=jnp.int32) * blk
    block_eid = jnp.minimum(jnp.searchsorted(ends, first_row, side='right'), N_EXPERTS - 1).astype(jnp.int32)
    block_valid = jnp.clip(cnt[block_eid] - (first_row - starts[block_eid]), 0, blk).astype(jnp.int32)
    block_first = jnp.concatenate([jnp.ones((1,), jnp.int32),
                                   (block_eid[1:] != block_eid[:-1]).astype(jnp.int32)])
    return dest, block_eid, block_first, block_valid


def dft_tables(L):
    n = 2 * L
    f = lax.broadcasted_iota(jnp.int32, (L, L), 0)
    t = lax.broadcasted_iota(jnp.int32, (L, L), 1)
    ang = ((f * t) % n).astype(jnp.float32) * (2.0 * math.pi / n)
    return jnp.cos(ang).astype(jnp.bfloat16), jnp.sin(ang).astype(jnp.bfloat16)


def _alt_sign(L):
    t = lax.broadcasted_iota(jnp.int32, (L, 1), 0)
    return (1 - 2 * (t & 1)).astype(jnp.float32)


def _spectrum_kernel(a_ref, b_ref, c_ref, s_ref, kr_ref, ks_ref, kn_ref):
    L = a_ref.shape[1]
    a = a_ref[0]
    row = lax.broadcasted_iota(jnp.int32, (L, 1), 0)
    scale = jnp.where(row == 0, 0.5 / L, 1.0 / L)
    kr_ref[0] = scale * jnp.dot(c_ref[...], a.astype(jnp.bfloat16), preferred_element_type=jnp.float32)
    ks_ref[0] = scale * jnp.dot(s_ref[...], b_ref[0].astype(jnp.bfloat16), preferred_element_type=jnp.float32)
    kn_ref[0] = jnp.sum(a * _alt_sign(L), axis=0, keepdims=True) * (0.5 / L)


def filter_spectrum(a, b, cos_t, sin_t, tc):
    n, L, C = a.shape
    blk = pl.BlockSpec((1, L, tc), lambda o, j: (o, 0, j))
    tab = pl.BlockSpec((L, L), lambda o, j: (0, 0))
    return pl.pallas_call(
        _spectrum_kernel,
        grid=(n, C // tc),
        in_specs=[blk, blk, tab, tab],
        out_specs=[blk, blk, pl.BlockSpec((1, 1, tc), lambda o, j: (o, 0, j))],
        out_shape=[jax.ShapeDtypeStruct((n, L, C), jnp.float32)] * 2 + [jax.ShapeDtypeStruct((n, 1, C), jnp.float32)],
        compiler_params=pltpu.CompilerParams(
            dimension_semantics=("arbitrary", "arbitrary"), vmem_limit_bytes=VMEM_LIMIT_BYTES),
    )(a, b, cos_t, sin_t)


def _long_conv_kernel(z_ref, xn_ref, kr_ref, ks_ref, kn_ref, bias_ref, c_ref, s_ref, o_ref, acc_ref, *, tf):
    L = z_ref.shape[1]
    z = z_ref[0]
    zb = z.astype(jnp.bfloat16)
    sign = _alt_sign(L)
    z_nyq = jnp.sum(z * sign, axis=0, keepdims=True)
    acc_ref[...] = z * bias_ref[0] + sign * (z_nyq * kn_ref[0])
    for ft in range(L // tf):
        rows = pl.ds(ft * tf, tf)
        zr = jnp.dot(c_ref[rows, :], zb, preferred_element_type=jnp.float32)
        zs = jnp.dot(s_ref[rows, :], zb, preferred_element_type=jnp.float32)
        kr = kr_ref[0, rows, :]
        ks = ks_ref[0, rows, :]
        yr = (zr * kr - zs * ks).astype(jnp.bfloat16)
        ys = (zr * ks + zs * kr).astype(jnp.bfloat16)
        acc_ref[...] += (jnp.dot(c_ref[:, rows], yr, preferred_element_type=jnp.float32)
                         + jnp.dot(s_ref[:, rows], ys, preferred_element_type=jnp.float32))
    o_ref[0] = xn_ref[0] * acc_ref[...]


def long_conv_gate(z_arr, z_col, xn_arr, xn_col, kr, ks, kn, bias, order, cos_t, sin_t, tc, tf):
    bsz, L, _ = z_arr.shape
    C = kr.shape[-1]
    nj = C // tc
    tab = pl.BlockSpec((L, L), lambda j, b: (0, 0), pipeline_mode=pl.Buffered(1))
    spec = pl.BlockSpec((1, L, tc), lambda j, b: (order, 0, j), pipeline_mode=pl.Buffered(1))
    vec = pl.BlockSpec((1, 1, tc), lambda j, b: (order, 0, j))
    return pl.pallas_call(
        functools.partial(_long_conv_kernel, tf=tf),
        grid=(nj, bsz),
        in_specs=[
            pl.BlockSpec((1, L, tc), lambda j, b: (b, 0, z_col * nj + j)),
            pl.BlockSpec((1, L, tc), lambda j, b: (b, 0, xn_col * nj + j)),
            spec, spec, vec, vec, tab, tab,
        ],
        out_specs=pl.BlockSpec((1, L, tc), lambda j, b: (b, 0, j)),
        out_shape=jax.ShapeDtypeStruct((bsz, L, C), jnp.float32),
        scratch_shapes=[pltpu.VMEM((L, tc), jnp.float32)],
        compiler_params=pltpu.CompilerParams(
            dimension_semantics=("arbitrary", "arbitrary"), vmem_limit_bytes=VMEM_LIMIT_BYTES),
    )(z_arr, xn_arr, kr, ks, kn, bias, cos_t, sin_t)


def hyena_long_convs(u, k, h_bias, tc, tf):
    L = u.shape[1]
    cos_t, sin_t = dft_tables(L)
    k_f = jnp.moveaxis(k[:, :, 0], 1, 0)
    k_b = jnp.moveaxis(k[:, :, 1], 1, 0).at[:, 0].set(0.0)
    kr, ks, kn = filter_spectrum(k_f + k_b, k_f - k_b, cos_t, sin_t, tc)
    bias = h_bias[:, None, :]
    z1 = long_conv_gate(u, 0, u, 1, kr, ks, kn, bias, 0, cos_t, sin_t, tc, tf)
    return long_conv_gate(z1, 0, u, 2, kr, ks, kn, bias, 1, cos_t, sin_t, tc, tf)


def _ssd_kernel(xf_ref, df_ref, xb_ref, db_ref, dskip_ref, y_ref, h_ref, *, n_ctx_chunks):
    s = pl.program_id(1)
    n_steps = pl.num_programs(1)
    Q, G, R, P, N = SSD_CHUNK, SSD_GROUPS, SSD_HPG, SSD_HEAD_DIM, SSD_STATE
    GP = R * P
    bf = jnp.bfloat16

    @pl.when(s == 0)
    def _():
        h_ref[...] = jnp.zeros_like(h_ref)
        y_ref[...] = jnp.zeros_like(y_ref)

    row = lax.broadcasted_iota(jnp.int32, (Q, Q), 0)
    col = lax.broadcasted_iota(jnp.int32, (Q, Q), 1)
    lane_head = lax.broadcasted_iota(jnp.int32, (Q, GP), 1) // P
    is_latent = s >= n_ctx_chunks
    out_chunk = (s - n_ctx_chunks, n_steps - 1 - s)

    for d, (x_ref, da_ref) in enumerate(((xf_ref, df_ref), (xb_ref, db_ref))):
        mask = (row >= col) if d == 0 else (col >= row)
        tri = mask.astype(jnp.float32)
        da = da_ref[0]
        cum = jnp.dot(tri, da, precision=lax.Precision.HIGHEST, preferred_element_type=jnp.float32)
        cum_t = cum.T
        edge = Q - 1 if d == 0 else 0
        blk = x_ref.at[0]
        for g in range(G):
            xg = blk[:, g * GP:(g + 1) * GP]
            bg = blk[:, D_SSD + g * N:D_SSD + (g + 1) * N].astype(bf)
            cg = blk[:, D_SSD + G * N + g * N:D_SSD + G * N + (g + 1) * N].astype(bf)
            heads = [d * SSD_HEADS + g * R + r for r in range(R)]
            dtm = jnp.zeros((Q, GP), jnp.float32)
            cumm = jnp.zeros((Q, GP), jnp.float32)
            totm = jnp.zeros((Q, GP), jnp.float32)
            for r, h in enumerate(heads):
                sel = lane_head == r
                dtm = jnp.where(sel, da[:, h:h + 1], dtm)
                cumm = jnp.where(sel, cum[:, SSD_HEADS * 2 + h:SSD_HEADS * 2 + h + 1], cumm)
                totm = jnp.where(sel, cum[edge:edge + 1, SSD_HEADS * 2 + h:SSD_HEADS * 2 + h + 1], totm)
            xdt = xg * dtm
            hg = h_ref[d, g * GP:(g + 1) * GP, :]

            @pl.when(is_latent)
            def _():
                gmat = lax.dot_general(cg, bg, (((1,), (1,)), ((), ())), preferred_element_type=jnp.float32)
                y_off = lax.dot_general(cg, hg.astype(bf), (((1,), (1,)), ((), ())),
                                        preferred_element_type=jnp.float32) * jnp.exp(cumm)
                if d == 0:
                    y_off = y_off + dskip_ref[:, g * GP:(g + 1) * GP] * xg
                parts = []
                for r, h in enumerate(heads):
                    a_col = cum[:, SSD_HEADS * 2 + h:SSD_HEADS * 2 + h + 1]
                    a_row = cum_t[SSD_HEADS * 2 + h:SSD_HEADS * 2 + h + 1, :]
                    decay = jnp.exp(jnp.where(mask, a_col - a_row, NEG_BIG))
                    parts.append(jnp.dot((gmat * decay).astype(bf), xdt[:, r * P:(r + 1) * P].astype(bf),
                                         preferred_element_type=jnp.float32))
                y = y_off + jnp.concatenate(parts, axis=-1)
                rows = pl.ds(pl.multiple_of(out_chunk[d] * Q, Q), Q)
                y_ref[0, rows, g * GP:(g + 1) * GP] += y

            xw = (xdt * jnp.exp(totm - cumm)).astype(bf)
            st = lax.dot_general(xw, bg, (((0,), (0,)), ((), ())), preferred_element_type=jnp.float32)
            for r, h in enumerate(heads):
                dec = jnp.exp(cum_t[SSD_HEADS * 2 + h:SSD_HEADS * 2 + h + 1, edge:edge + 1])
                rs = slice(g * GP + r * P, g * GP + (r + 1) * P)
                h_ref[d, rs, :] = h_ref[d, rs, :] * dec + st[r * P:(r + 1) * P, :]


def ssd_scan_bidir(xbc, dta, d_skip, n_ctx):
    bsz, lt, width = xbc.shape
    Q = SSD_CHUNK
    n_steps = lt // Q
    n_ctx_chunks = n_ctx // Q
    L = lt - n_ctx

    def bwd_chunk(s):
        return jnp.where(s < n_ctx_chunks, n_ctx_chunks - 1 - s, n_steps - 1 - s + n_ctx_chunks)

    return pl.pallas_call(
        functools.partial(_ssd_kernel, n_ctx_chunks=n_ctx_chunks),
        grid=(bsz, n_steps),
        in_specs=[
            pl.BlockSpec((1, Q, width), lambda b, s: (b, s, 0)),
            pl.BlockSpec((1, Q, LANES), lambda b, s: (b, s, 0)),
            pl.BlockSpec((1, Q, width), lambda b, s: (b, bwd_chunk(s), 0)),
            pl.BlockSpec((1, Q, LANES), lambda b, s: (b, bwd_chunk(s), 0)),
            pl.BlockSpec((1, D_SSD), lambda b, s: (0, 0)),
        ],
        out_specs=pl.BlockSpec((1, L, D_SSD), lambda b, s: (b, 0, 0)),
        out_shape=jax.ShapeDtypeStruct((bsz, L, D_SSD), jnp.float32),
        scratch_shapes=[pltpu.VMEM((2, SSD_GROUPS * SSD_HPG * SSD_HEAD_DIM, SSD_STATE), jnp.float32)],
        compiler_params=pltpu.CompilerParams(
            dimension_semantics=("arbitrary", "arbitrary"), vmem_limit_bytes=VMEM_LIMIT_BYTES),
    )(xbc, dta, xbc, dta, d_skip)


def _short_conv_rows(u, w, b, n_rows):
    bsz, L, C = u.shape
    row_len = L // n_rows
    pad = SHORT_CONV // 2
    ur = jnp.pad(u.reshape(bsz, n_rows, row_len, C), ((0, 0), (0, 0), (pad, pad), (0, 0)))
    y = b + sum(ur[:, :, k:k + row_len] * w[k] for k in range(SHORT_CONV))
    return y.reshape(bsz, L, C)


def _hyena_filters(L, f_w1, f_b1, f_freq, f_w2, f_b2, f_w3):
    f32 = jnp.float32
    hp = lax.Precision.HIGHEST
    pos = jnp.arange(L, dtype=f32)
    t = pos / max(L - 1, 1)
    bands = jnp.linspace(1e-4, HYENA_BANDS - 1, HYENA_BANDS, dtype=f32)
    ang = (2 * math.pi / L) * pos[:, None] * bands[None, :]
    feats = jnp.concatenate([t[:, None], jnp.cos(ang), -jnp.sin(ang)], axis=-1)
    h = jnp.sin(f_freq * (jnp.dot(feats, f_w1, precision=hp) + f_b1))
    h = jnp.sin(f_freq * (jnp.dot(h, f_w2, precision=hp) + f_b2))
    k = jnp.dot(h, f_w3, precision=hp).reshape(L, HYENA_ORDER, 2, D_HYENA)
    deltas = jnp.abs(jnp.linspace(math.log(HYENA_TARGET) / HYENA_SLOW_DECAY,
                                  math.log(HYENA_TARGET) / HYENA_FAST_DECAY, D_HYENA, dtype=f32))
    window = jnp.exp(-t[:, None] * deltas[None, :])
    return k * window[:, None, None, :]


def _hyena_branch(p, conv_w, conv_b, f_w1, f_b1, f_freq, f_w2, f_b2, f_w3, h_bias, n_rows):
    L = p.shape[1]
    u = _short_conv_rows(p, conv_w, conv_b, n_rows)
    k = _hyena_filters(L, f_w1, f_b1, f_freq, f_w2, f_b2, f_w3)
    return hyena_long_convs(u, k, h_bias, 256, 512)


def _ssd_prep(p, conv_w, conv_b, a_log, dt_bias, n_rows):
    bsz, L, _ = p.shape
    xbc = jax.nn.silu(_short_conv_rows(p[..., D_SSD:D_SSD + D_XBC], conv_w, conv_b, n_rows))
    dt_raw = p[..., D_SSD + D_XBC:D_SSD + D_XBC + 2 * SSD_HEADS]
    dt = jax.nn.softplus(dt_raw + dt_bias.reshape(2 * SSD_HEADS))
    a = -jnp.exp(a_log).reshape(2 * SSD_HEADS) * dt
    dta = jnp.concatenate([dt, a, jnp.zeros((bsz, L, LANES - 4 * SSD_HEADS), jnp.float32)], axis=-1)
    return xbc, dta


def kernel(x, c, ctx, c_ctx, w_ada, b_ada, g_norm1, g_norm2, w_in, hy_conv_w, hy_conv_b, hy_f_w1, hy_f_b1, hy_f_freq, hy_f_w2, hy_f_b2, hy_f_w3, hy_bias, ssd_conv_w, ssd_conv_b, ssd_a_log, ssd_dt_bias, ssd_d, ssd_norm_g, w_out, w_group, b_group, w_expert, b_expert, w1, w3, w2, g_final):
    bsz, seq_len, _ = x.shape
    rows = seq_len // GRID_W
    l = 0
    hp = lax.Precision.HIGHEST
    hy_p = (hy_conv_w[l], hy_conv_b[l], hy_f_w1[l], hy_f_b1[l], hy_f_freq[l],
            hy_f_w2[l], hy_f_b2[l], hy_f_w3[l], hy_bias[l])
    ssd_p = (ssd_conv_w[l], ssd_conv_b[l], ssd_a_log[l], ssd_dt_bias[l])
    mod = (jnp.dot(jax.nn.silu(c), w_ada[l], precision=hp) + b_ada[l])[:, None, :]
    sh1, sc1, ga1, sh2, sc2, ga2 = jnp.split(mod, 6, axis=-1)
    cmod = jnp.dot(jax.nn.silu(c_ctx), w_ada[l], precision=hp) + b_ada[l]
    csh1, csc1, _, _, _, _ = jnp.split(cmod, 6, axis=-1)

    w_in_bf = jnp.pad(w_in[l].astype(jnp.bfloat16), ((0, 0), (0, D_IN_PAD - D_IN)))
    w_out_bf = w_out[l].astype(jnp.bfloat16)

    ones = jnp.ones((bsz, 1, 1), jnp.float32)
    pc = norm_mod_matmul(ctx, g_norm1[l], csh1[None, None, :] * ones, csc1[None, None, :] * ones,
                         w_in_bf[:, HY_COLS:], 256)
    xbc_c, dta_c = _ssd_prep(pc, *ssd_p, 1)

    px = norm_mod_matmul(x, g_norm1[l], sh1, sc1, w_in_bf, 512)
    y_hy = _hyena_branch(px[..., :HY_COLS], *hy_p, rows)
    xbc_x, dta_x = _ssd_prep(px[..., HY_COLS:], *ssd_p, rows)
    y_scan = ssd_scan_bidir(jnp.concatenate([xbc_c, xbc_x], axis=1), jnp.concatenate([dta_c, dta_x], axis=1),
                            jnp.repeat(ssd_d[l], SSD_HEAD_DIM)[None, :], CTX_LEN)

    pad = LANES - MOE_GROUPS - N_EXPERTS
    w_router = jnp.concatenate([w_group[l], w_expert[l], jnp.zeros((D_MODEL, pad), jnp.float32)], axis=1)
    b_router = jnp.concatenate([b_group[l], b_expert[l], jnp.zeros((pad,), jnp.float32)])[None, :]
    x1, hn, route_i, route_f, counts = out_proj_router(
        y_hy, y_scan, px, HY_COLS // D_SSD, x, ga1, sc2, sh2, ssd_norm_g[l][None, :], g_norm2[l][None, :],
        w_out_bf, w_router, b_router, 256)
    n_tok = bsz * seq_len
    n_blocks = -(-n_tok * TOP_K // MOE_BLOCK) + N_EXPERTS
    dest, block_eid, block_first, block_valid = moe_plan(route_i, counts, MOE_BLOCK, n_blocks)
    buf = moe_dispatch(hn.reshape(n_tok, D_MODEL), dest, block_valid, MOE_BLOCK, 256)
    yb = expert_blocks(buf, block_eid, block_first, block_valid, w1[l], w3[l], w2[l], MOE_BLOCK)
    return moe_combine(x1, route_f, ga2, g_final[None, :], yb, dest, 256)
```

```python
import functools
import math

import jax
import jax.numpy as jnp
from jax import lax
from jax.experimental import pallas as pl
from jax.experimental.pallas import tpu as pltpu

D_MODEL = 1024
CTX_LEN = 256
GRID_W = 64
EPS = 1e-6
SHORT_CONV = 3

D_HYENA = D_MODEL // 2
HYENA_ORDER = 2
HYENA_BANDS = 8
HYENA_FAST_DECAY = 0.3
HYENA_SLOW_DECAY = 1.5
HYENA_TARGET = 1e-2

D_SSD = D_MODEL // 2
SSD_HEAD_DIM = 64
SSD_HEADS = D_SSD // SSD_HEAD_DIM
SSD_GROUPS = 2
SSD_HPG = SSD_HEADS // SSD_GROUPS
SSD_STATE = 128
SSD_CHUNK = 128

D_XBC = D_SSD + 2 * SSD_GROUPS * SSD_STATE
HY_COLS = (HYENA_ORDER + 1) * D_HYENA
D_IN = HY_COLS + D_SSD + D_XBC + 2 * SSD_HEADS
LANES = 128
D_IN_PAD = -(-D_IN // LANES) * LANES

MOE_GROUPS = 8
EXPERTS_PER_GROUP = 8
N_EXPERTS = MOE_GROUPS * EXPERTS_PER_GROUP
TOP_K = 2
D_EXPERT = 512
MOE_BLOCK = 128
ROUTE_COLS = 8

VMEM_LIMIT_BYTES = 56 * 1024 * 1024
NEG_BIG = -1e30


def _conv3_rows(p, w_ref, b_ref, cols, has_prev, has_next):
    n = p.shape[0]
    prev = jnp.where(has_prev, pltpu.roll(p, 1, 0), 0.0)
    nxt = jnp.where(has_next, pltpu.roll(p, n - 1, 0), 0.0)
    return b_ref[:, cols] + w_ref[0:1, cols] * prev + w_ref[1:2, cols] * p + w_ref[2:3, cols] * nxt


def _in_proj_kernel(ctx_ref, x_ref, g_ref, csh_ref, csc_ref, sh_ref, sc_ref, w_ref, wdt_ref, hw_ref, hb_ref,
                    sw_ref, sb_ref, dtb_ref, dtm_ref, u_ref, z_ref, xbc_ref, dta_ref, h_ref,
                    *, n_ctx_steps, row_len, ctx_row_len, hy_cols, d_ssd, d_xbc, tn):
    i = pl.program_id(1)
    is_ctx = i < n_ctx_steps
    tm = x_ref.shape[1]
    xin = jnp.where(is_ctx, ctx_ref[0], x_ref[0])
    shift = jnp.where(is_ctx, csh_ref[...], sh_ref[0])
    scale = jnp.where(is_ctx, csc_ref[...], sc_ref[0])
    y = xin * lax.rsqrt(jnp.mean(xin * xin, axis=-1, keepdims=True) + EPS) * g_ref[...]
    h_ref[...] = (y * (1.0 + scale) + shift).astype(jnp.bfloat16)

    pos = lax.broadcasted_iota(jnp.int32, (tm, 1), 0)
    rl = jnp.where(is_ctx, ctx_row_len, row_len)
    in_row = pos % rl
    has_prev = in_row != 0
    has_next = in_row != rl - 1

    @pl.when(jnp.logical_not(is_ctx))
    def _():
        for c0 in range(0, hy_cols, tn):
            cols = slice(c0, c0 + tn)
            p = jnp.dot(h_ref[...], w_ref[:, cols], preferred_element_type=jnp.float32)
            u_ref[0, :, cols] = _conv3_rows(p, hw_ref, hb_ref, cols, has_prev, has_next)
        z_ref[0] = jnp.dot(h_ref[...], w_ref[:, hy_cols:hy_cols + d_ssd], preferred_element_type=jnp.float32)

    for c0 in range(0, d_xbc, tn):
        cols = slice(c0, c0 + tn)
        wc = slice(hy_cols + d_ssd + c0, hy_cols + d_ssd + c0 + tn)
        p = jnp.dot(h_ref[...], w_ref[:, wc], preferred_element_type=jnp.float32)
        v = _conv3_rows(p, sw_ref, sb_ref, cols, has_prev, has_next)
        xbc_ref[0, :, cols] = v * jax.nn.sigmoid(v)
    pd = jnp.dot(h_ref[...], wdt_ref[...], preferred_element_type=jnp.float32) + dtb_ref[...]
    sp = jnp.maximum(pd, 0.0) + jnp.log(1.0 + jnp.exp(-jnp.abs(pd)))
    dta_ref[0] = sp * dtm_ref[...]


def in_proj_fused(ctx, x, g1, csh, csc, sh, sc, w_bf, wdt_bf, hy_w, hy_b, ssd_w, ssd_b, dt_bias2, dt_mult,
                  row_len, tm, tn):
    bsz, L, D = x.shape
    lc = ctx.shape[1]
    hy_cols = hy_w.shape[1]
    d_xbc = ssd_w.shape[1]
    d_ssd = w_bf.shape[1] - hy_cols - d_xbc
    n_ctx_steps = lc // tm
    n_steps = n_ctx_steps + L // tm
    lat = lambda b, i: (b, jnp.maximum(i - n_ctx_steps, 0), 0)
    allt = lambda b, i: (b, i, 0)
    const2 = lambda b, i: (0, 0)
    per_b = pl.BlockSpec((1, 1, D), lambda b, i: (b, 0, 0))
    kern = functools.partial(_in_proj_kernel, n_ctx_steps=n_ctx_steps, row_len=row_len, ctx_row_len=lc,
                             hy_cols=hy_cols, d_ssd=d_ssd, d_xbc=d_xbc, tn=tn)
    return pl.pallas_call(
        kern,
        grid=(bsz, n_steps),
        in_specs=[
            pl.BlockSpec((1, tm, D), lambda b, i: (b, jnp.minimum(i, n_ctx_steps - 1), 0)),
            pl.BlockSpec((1, tm, D), lat),
            pl.BlockSpec((1, D), const2),
            pl.BlockSpec((1, D), const2),
            pl.BlockSpec((1, D), const2),
            per_b, per_b,
            pl.BlockSpec(w_bf.shape, const2),
            pl.BlockSpec(wdt_bf.shape, const2),
            pl.BlockSpec(hy_w.shape, const2),
            pl.BlockSpec(hy_b.shape, const2),
            pl.BlockSpec(ssd_w.shape, const2),
            pl.BlockSpec(ssd_b.shape, const2),
            pl.BlockSpec((1, LANES), const2),
            pl.BlockSpec((1, LANES), const2),
        ],
        out_specs=[
            pl.BlockSpec((1, tm, hy_cols), lat),
            pl.BlockSpec((1, tm, d_ssd), lat),
            pl.BlockSpec((1, tm, d_xbc), allt),
            pl.BlockSpec((1, tm, LANES), allt),
        ],
        out_shape=[
            jax.ShapeDtypeStruct((bsz, L, hy_cols), jnp.float32),
            jax.ShapeDtypeStruct((bsz, L, d_ssd), jnp.float32),
            jax.ShapeDtypeStruct((bsz, lc + L, d_xbc), jnp.float32),
            jax.ShapeDtypeStruct((bsz, lc + L, LANES), jnp.float32),
        ],
        scratch_shapes=[pltpu.VMEM((tm, D), jnp.bfloat16)],
        compiler_params=pltpu.CompilerParams(
            dimension_semantics=("arbitrary", "arbitrary"), vmem_limit_bytes=VMEM_LIMIT_BYTES),
    )(ctx, x, g1, csh, csc, sh, sc, w_bf, wdt_bf, hy_w, hy_b, ssd_w, ssd_b, dt_bias2, dt_mult)


def in_proj_params(w_in, a_log, dt_bias, hy_cols, d_ssd, d_xbc):
    n_h = 2 * SSD_HEADS
    main = hy_cols + d_ssd + d_xbc
    w_dt = w_in[:, main:main + n_h]
    pad = jnp.zeros((w_in.shape[0], LANES - 2 * n_h), w_in.dtype)
    wdt = jnp.concatenate([w_dt, w_dt, pad], axis=1).astype(jnp.bfloat16)
    zpad = jnp.zeros((LANES - 2 * n_h,), jnp.float32)
    bias2 = jnp.concatenate([dt_bias.reshape(n_h), dt_bias.reshape(n_h), zpad])[None, :]
    mult = jnp.concatenate([jnp.ones((n_h,), jnp.float32), -jnp.exp(a_log).reshape(n_h), zpad])[None, :]
    return w_in[:, :main].astype(jnp.bfloat16), wdt, bias2, mult


def _out_router_kernel(yh_ref, ys_ref, z_ref, x_ref, ga_ref, sc_ref, sh_ref, ng_ref, g2_ref, wo_ref, wr_ref, br_ref,
                       x1_ref, hn_ref, ri_ref, rf_ref, cnt_ref, carry_ref):
    first = jnp.logical_and(pl.program_id(0) == 0, pl.program_id(1) == 0)

    @pl.when(first)
    def _():
        carry_ref[...] = jnp.zeros_like(carry_ref)

    bf = jnp.bfloat16
    tm = x_ref.shape[1]
    dh = yh_ref.shape[2]
    z = z_ref[0]
    ys = ys_ref[0] * (z * jax.nn.sigmoid(z))
    gw = ys.shape[1] // SSD_GROUPS
    acc = jnp.dot(yh_ref[0].astype(bf), wo_ref[0:dh, :], preferred_element_type=jnp.float32)
    for g in range(SSD_GROUPS):
        yg = ys[:, g * gw:(g + 1) * gw]
        yg = yg * lax.rsqrt(jnp.mean(yg * yg, axis=-1, keepdims=True) + EPS) * ng_ref[:, g * gw:(g + 1) * gw]
        acc += jnp.dot(yg.astype(bf), wo_ref[dh + g * gw:dh + (g + 1) * gw, :], preferred_element_type=jnp.float32)
    x1 = x_ref[0] + ga_ref[0] * acc
    x1_ref[0] = x1
    hn = x1 * lax.rsqrt(jnp.mean(x1 * x1, axis=-1, keepdims=True) + EPS) * g2_ref[...]
    hn = hn * (1.0 + sc_ref[0]) + sh_ref[0]
    hn_ref[0] = hn

    logits = jnp.dot(hn, wr_ref[...], precision=lax.Precision.HIGHEST,
                     preferred_element_type=jnp.float32) + br_ref[...]
    lane = lax.broadcasted_iota(jnp.int32, (tm, LANES), 1)
    lane_f = lane.astype(jnp.float32)
    ninf = jnp.float32(-jnp.inf)
    big = jnp.float32(1e9)
    gl = jnp.where(lane < MOE_GROUPS, logits, ninf)
    gmax = jnp.max(gl, axis=-1, keepdims=True)
    p_group = 1.0 / jnp.sum(jnp.exp(gl - gmax), axis=-1, keepdims=True)
    g_sel = jnp.min(jnp.where(gl == gmax, lane_f, big), axis=-1, keepdims=True)
    e_lane = lane - MOE_GROUPS
    in_grp = jnp.logical_and(e_lane >= 0, (e_lane // EXPERTS_PER_GROUP).astype(jnp.float32) == g_sel)
    el = jnp.where(in_grp, logits, ninf)
    m1 = jnp.max(el, axis=-1, keepdims=True)
    i1 = jnp.min(jnp.where(el == m1, lane_f, big), axis=-1, keepdims=True)
    el2 = jnp.where(lane_f == i1, ninf, el)
    m2 = jnp.max(el2, axis=-1, keepdims=True)
    i2 = jnp.min(jnp.where(el2 == m2, lane_f, big), axis=-1, keepdims=True)
    t = jnp.exp(m2 - m1)
    w1 = 1.0 / (1.0 + t)
    gate1 = w1 * p_group
    gate2 = (t * w1) * p_group
    e1 = i1 - MOE_GROUPS
    e2 = i2 - MOE_GROUPS
    el_f = e_lane.astype(jnp.float32)
    oh1 = el_f == e1
    oh2 = el_f == e2
    oh = jnp.logical_or(oh1, oh2).astype(bf)
    r_i = lax.broadcasted_iota(jnp.int32, (tm, tm), 0)
    c_i = lax.broadcasted_iota(jnp.int32, (tm, tm), 1)
    before = jnp.dot((c_i < r_i).astype(bf), oh, preferred_element_type=jnp.float32) + carry_ref[...]
    rank1 = jnp.sum(jnp.where(oh1, before, 0.0), axis=-1, keepdims=True)
    rank2 = jnp.sum(jnp.where(oh2, before, 0.0), axis=-1, keepdims=True)
    carry_ref[...] += jnp.sum(oh.astype(jnp.float32), axis=0, keepdims=True)
    cnt_ref[...] = carry_ref[...]

    col = lax.broadcasted_iota(jnp.int32, (tm, ROUTE_COLS), 1)
    rec_i = jnp.where(col == 0, e1, jnp.where(col == 1, e2, jnp.where(col == 2, rank1, rank2)))
    ri_ref[0] = rec_i.astype(jnp.int32)
    rf_ref[0] = jnp.where(col == 0, gate1, gate2)


def out_proj_router(y_hy, y_scan, px, z_col, x, ga1, sc2, sh2, norm_g, g2, w_out_bf, w_router, b_router, tm):
    bsz, L, D = x.shape
    dh = y_hy.shape[-1]
    ds = y_scan.shape[-1]
    tok = lambda b, i: (b, i, 0)
    per_b = pl.BlockSpec((1, 1, D), lambda b, i: (b, 0, 0))
    const2 = lambda b, i: (0, 0)
    return pl.pallas_call(
        _out_router_kernel,
        grid=(bsz, L // tm),
        in_specs=[
            pl.BlockSpec((1, tm, dh), tok),
            pl.BlockSpec((1, tm, ds), tok),
            pl.BlockSpec((1, tm, ds), lambda b, i: (b, i, z_col)),
            pl.BlockSpec((1, tm, D), tok),
            per_b, per_b, per_b,
            pl.BlockSpec((1, ds), const2),
            pl.BlockSpec((1, D), const2),
            pl.BlockSpec((dh + ds, D), const2),
            pl.BlockSpec((D, LANES), const2),
            pl.BlockSpec((1, LANES), const2),
        ],
        out_specs=[
            pl.BlockSpec((1, tm, D), tok),
            pl.BlockSpec((1, tm, D), tok),
            pl.BlockSpec((1, tm, ROUTE_COLS), tok),
            pl.BlockSpec((1, tm, ROUTE_COLS), tok),
            pl.BlockSpec((1, LANES), const2),
        ],
        out_shape=[
            jax.ShapeDtypeStruct((bsz, L, D), jnp.float32),
            jax.ShapeDtypeStruct((bsz, L, D), jnp.float32),
            jax.ShapeDtypeStruct((bsz, L, ROUTE_COLS), jnp.int32),
            jax.ShapeDtypeStruct((bsz, L, ROUTE_COLS), jnp.float32),
            jax.ShapeDtypeStruct((1, LANES), jnp.float32),
        ],
        scratch_shapes=[pltpu.VMEM((1, LANES), jnp.float32)],
        compiler_params=pltpu.CompilerParams(
            dimension_semantics=("arbitrary", "arbitrary"), vmem_limit_bytes=VMEM_LIMIT_BYTES),
    )(y_hy, y_scan, px, x, ga1, sc2, sh2, norm_g, g2, w_out_bf, w_router, b_router)


def _row_copy(src_hbm, src_row, dst_ref, dst_row, sem):
    return pltpu.make_async_copy(src_hbm.at[pl.ds(src_row, 1), :], dst_ref.at[pl.ds(dst_row, 1), :], sem)


def _dispatch_kernel(dest_ref, valid_ref, hn_ref, buf_hbm, zeros, sem, zsem):
    step = pl.program_id(0)
    tm = hn_ref.shape[0]
    blk = zeros.shape[0]
    n_blocks = buf_hbm.shape[0] // blk

    def zero_copy(i):
        return pltpu.make_async_copy(zeros, buf_hbm.at[pl.ds(pl.multiple_of(i * blk, blk), blk), :], zsem)

    def zfill(i, carry):
        @pl.when(valid_ref[i] < blk)
        def _():
            zero_copy(i).start()
        return carry

    def zwait(i, carry):
        @pl.when(valid_ref[i] < blk)
        def _():
            zero_copy(i).wait()
        return carry

    @pl.when(step == 0)
    def _():
        zeros[...] = jnp.zeros_like(zeros)
        lax.fori_loop(0, n_blocks, zfill, 0)
        lax.fori_loop(0, n_blocks, zwait, 0)

    def body(j, carry):
        t = step * tm + j
        _row_copy(hn_ref, j, buf_hbm, dest_ref[2 * t], sem).start()
        _row_copy(hn_ref, j, buf_hbm, dest_ref[2 * t + 1], sem).start()
        return carry

    lax.fori_loop(0, tm, body, 0, unroll=8)
    for _ in range(2):
        pltpu.make_async_copy(hn_ref, buf_hbm.at[pl.ds(0, tm), :], sem).wait()


def moe_dispatch(hn, dest, block_valid, blk, tm):
    T, D = hn.shape
    n_rows = block_valid.shape[0] * blk
    grid_spec = pltpu.PrefetchScalarGridSpec(
        num_scalar_prefetch=2,
        grid=(T // tm,),
        in_specs=[pl.BlockSpec((tm, D), lambda i, d, v: (i, 0))],
        out_specs=pl.BlockSpec(memory_space=pl.ANY),
        scratch_shapes=[pltpu.VMEM((blk, D), jnp.float32), pltpu.SemaphoreType.DMA(()),
                        pltpu.SemaphoreType.DMA(())],
    )
    return pl.pallas_call(
        _dispatch_kernel,
        grid_spec=grid_spec,
        out_shape=jax.ShapeDtypeStruct((n_rows, D), jnp.float32),
        compiler_params=pltpu.CompilerParams(dimension_semantics=("arbitrary",), has_side_effects=True),
    )(dest, block_valid, hn)


def _expert_kernel(eid_ref, first_ref, valid_ref, x_ref, w1_ref, w3_ref, w2_ref, o_ref, w1b, w3b, w2b):
    i = pl.program_id(0)
    del eid_ref
    bf = jnp.bfloat16

    @pl.when(first_ref[i] == 1)
    def _():
        w1b[...] = w1_ref[0].astype(bf)
        w3b[...] = w3_ref[0].astype(bf)
        w2b[...] = w2_ref[0].astype(bf)

    valid = valid_ref[i]

    @pl.when(valid > 0)
    def _():
        xb = x_ref[...].astype(bf)
        a = jnp.dot(xb, w1b[...], preferred_element_type=jnp.float32)
        b = jnp.dot(xb, w3b[...], preferred_element_type=jnp.float32)
        h = (a * jax.nn.sigmoid(a)) * b
        o_ref[...] = jnp.dot(h.astype(bf), w2b[...], preferred_element_type=jnp.float32)

    @pl.when(valid <= 0)
    def _():
        o_ref[...] = jnp.zeros_like(o_ref)


def expert_blocks(buf, block_eid, block_first, block_valid, w1, w3, w2, blk):
    rows, D = buf.shape
    n_blocks = rows // blk
    E, _, F = w1.shape
    grid_spec = pltpu.PrefetchScalarGridSpec(
        num_scalar_prefetch=3,
        grid=(n_blocks,),
        in_specs=[
            pl.BlockSpec((blk, D), lambda i, eid, fi, va: (i, 0)),
            pl.BlockSpec((1, D, F), lambda i, eid, fi, va: (eid[i], 0, 0)),
            pl.BlockSpec((1, D, F), lambda i, eid, fi, va: (eid[i], 0, 0)),
            pl.BlockSpec((1, F, D), lambda i, eid, fi, va: (eid[i], 0, 0)),
        ],
        out_specs=pl.BlockSpec((blk, D), lambda i, eid, fi, va: (i, 0)),
        scratch_shapes=[pltpu.VMEM((D, F), jnp.bfloat16), pltpu.VMEM((D, F), jnp.bfloat16),
                        pltpu.VMEM((F, D), jnp.bfloat16)],
    )
    return pl.pallas_call(
        _expert_kernel,
        grid_spec=grid_spec,
        out_shape=jax.ShapeDtypeStruct((rows, D), jnp.float32),
        compiler_params=pltpu.CompilerParams(
            dimension_semantics=("arbitrary",), vmem_limit_bytes=VMEM_LIMIT_BYTES),
    )(block_eid, block_first, block_valid, buf, w1, w3, w2)


def _combine_kernel(dest_ref, x1_ref, rf_ref, ga_ref, gf_ref, yb_hbm, o_ref, ybuf, sem):
    b = pl.program_id(0)
    i = pl.program_id(1)
    n_i = pl.num_programs(1)
    tm = x1_ref.shape[1]
    step = b * n_i + i
    n_steps = pl.num_programs(0) * n_i
    slot = step % 2

    def issue(step_, slot_):
        def body(j, carry):
            t = step_ * tm + j
            _row_copy(yb_hbm, dest_ref[2 * t], ybuf.at[slot_, 0], j, sem.at[slot_]).start()
            _row_copy(yb_hbm, dest_ref[2 * t + 1], ybuf.at[slot_, 1], j, sem.at[slot_]).start()
            return carry
        lax.fori_loop(0, tm, body, 0, unroll=8)

    @pl.when(step == 0)
    def _():
        issue(0, 0)

    @pl.when(step + 1 < n_steps)
    def _():
        issue(step + 1, 1 - slot)

    pltpu.make_async_copy(yb_hbm.at[pl.ds(0, tm), :], ybuf.at[slot, 0], sem.at[slot]).wait()
    pltpu.make_async_copy(yb_hbm.at[pl.ds(0, tm), :], ybuf.at[slot, 1], sem.at[slot]).wait()
    rf = rf_ref[0]
    y = rf[:, 0:1] * ybuf[slot, 0] + rf[:, 1:2] * ybuf[slot, 1]
    x2 = x1_ref[0] + ga_ref[0] * y
    o_ref[0] = x2 * lax.rsqrt(jnp.mean(x2 * x2, axis=-1, keepdims=True) + EPS) * gf_ref[...]


def moe_combine(x1, route_f, ga2, g_final, yb, dest, tm):
    bsz, L, D = x1.shape
    grid_spec = pltpu.PrefetchScalarGridSpec(
        num_scalar_prefetch=1,
        grid=(bsz, L // tm),
        in_specs=[
            pl.BlockSpec((1, tm, D), lambda b, i, d: (b, i, 0)),
            pl.BlockSpec((1, tm, ROUTE_COLS), lambda b, i, d: (b, i, 0)),
            pl.BlockSpec((1, 1, D), lambda b, i, d: (b, 0, 0)),
            pl.BlockSpec((1, D), lambda b, i, d: (0, 0)),
            pl.BlockSpec(memory_space=pl.ANY),
        ],
        out_specs=pl.BlockSpec((1, tm, D), lambda b, i, d: (b, i, 0)),
        scratch_shapes=[pltpu.VMEM((2, 2, tm, D), jnp.float32), pltpu.SemaphoreType.DMA((2,))],
    )
    return pl.pallas_call(
        _combine_kernel,
        grid_spec=grid_spec,
        out_shape=jax.ShapeDtypeStruct((bsz, L, D), jnp.float32),
        compiler_params=pltpu.CompilerParams(
            dimension_semantics=("arbitrary", "arbitrary"), vmem_limit_bytes=VMEM_LIMIT_BYTES),
    )(dest, x1, route_f, ga2, g_final, yb)


def moe_plan(route_i, counts, blk, n_blocks):
    cnt = counts[0, MOE_GROUPS:MOE_GROUPS + N_EXPERTS].astype(jnp.int32)
    padded = (cnt + blk - 1) // blk * blk
    ends = jnp.cumsum(padded)
    starts = ends - padded
    e = route_i[..., 0:2].reshape(-1, 2)
    rank = route_i[..., 2:4].reshape(-1, 2)
    dest = (starts[e] + rank).reshape(-1)
    first_row = jnp.arange(n_blocks, dtype=jnp.int32) * blk
    block_eid = jnp.minimum(jnp.sum((ends[None, :] <= first_row[:, None]).astype(jnp.int32), axis=1), N_EXPERTS - 1)
    block_valid = jnp.clip(cnt[block_eid] - (first_row - starts[block_eid]), 0, blk).astype(jnp.int32)
    block_first = jnp.concatenate([jnp.ones((1,), jnp.int32),
                                   (block_eid[1:] != block_eid[:-1]).astype(jnp.int32)])
    return dest, block_eid, block_first, block_valid


def dft_tables(L):
    n = 2 * L
    f = lax.broadcasted_iota(jnp.int32, (L, L), 0)
    t = lax.broadcasted_iota(jnp.int32, (L, L), 1)
    ang = ((f * t) % n).astype(jnp.float32) * (2.0 * math.pi / n)
    return jnp.cos(ang).astype(jnp.bfloat16), jnp.sin(ang).astype(jnp.bfloat16)


def _alt_sign(L):
    t = lax.broadcasted_iota(jnp.int32, (L, 1), 0)
    return (1 - 2 * (t & 1)).astype(jnp.float32)


def _spectrum_kernel(a_ref, b_ref, c_ref, s_ref, kr_ref, ks_ref, kn_ref):
    L = a_ref.shape[1]
    a = a_ref[0]
    row = lax.broadcasted_iota(jnp.int32, (L, 1), 0)
    scale = jnp.where(row == 0, 0.5 / L, 1.0 / L)
    kr_ref[0] = scale * jnp.dot(c_ref[...], a.astype(jnp.bfloat16), preferred_element_type=jnp.float32)
    ks_ref[0] = scale * jnp.dot(s_ref[...], b_ref[0].astype(jnp.bfloat16), preferred_element_type=jnp.float32)
    kn_ref[0] = jnp.sum(a * _alt_sign(L), axis=0, keepdims=True) * (0.5 / L)


def filter_spectrum(a, b, cos_t, sin_t, tc):
    n, L, C = a.shape
    blk = pl.BlockSpec((1, L, tc), lambda o, j: (o, 0, j))
    tab = pl.BlockSpec((L, L), lambda o, j: (0, 0))
    return pl.pallas_call(
        _spectrum_kernel,
        grid=(n, C // tc),
        in_specs=[blk, blk, tab, tab],
        out_specs=[blk, blk, pl.BlockSpec((1, 1, tc), lambda o, j: (o, 0, j))],
        out_shape=[jax.ShapeDtypeStruct((n, L, C), jnp.float32)] * 2 + [jax.ShapeDtypeStruct((n, 1, C), jnp.float32)],
        compiler_params=pltpu.CompilerParams(
            dimension_semantics=("arbitrary", "arbitrary"), vmem_limit_bytes=VMEM_LIMIT_BYTES),
    )(a, b, cos_t, sin_t)


def _long_conv_kernel(z_ref, xn_ref, kr_ref, ks_ref, kn_ref, bias_ref, c_ref, s_ref, o_ref, acc_ref, *, tf):
    L = z_ref.shape[1]
    z = z_ref[0]
    zb = z.astype(jnp.bfloat16)
    sign = _alt_sign(L)
    z_nyq = jnp.sum(z * sign, axis=0, keepdims=True)
    acc_ref[...] = z * bias_ref[0] + sign * (z_nyq * kn_ref[0])
    for ft in range(L // tf):
        rows = pl.ds(ft * tf, tf)
        zr = jnp.dot(c_ref[rows, :], zb, preferred_element_type=jnp.float32)
        zs = jnp.dot(s_ref[rows, :], zb, preferred_element_type=jnp.float32)
        kr = kr_ref[0, rows, :]
        ks = ks_ref[0, rows, :]
        yr = (zr * kr - zs * ks).astype(jnp.bfloat16)
        ys = (zr * ks + zs * kr).astype(jnp.bfloat16)
        acc_ref[...] += (jnp.dot(c_ref[:, rows], yr, preferred_element_type=jnp.float32)
                         + jnp.dot(s_ref[:, rows], ys, preferred_element_type=jnp.float32))
    o_ref[0] = xn_ref[0] * acc_ref[...]


def long_conv_gate(z_arr, z_col, xn_arr, xn_col, kr, ks, kn, bias, order, cos_t, sin_t, tc, tf):
    bsz, L, _ = z_arr.shape
    C = kr.shape[-1]
    nj = C // tc
    tab = pl.BlockSpec((L, L), lambda j, b: (0, 0), pipeline_mode=pl.Buffered(1))
    spec = pl.BlockSpec((1, L, tc), lambda j, b: (order, 0, j), pipeline_mode=pl.Buffered(1))
    vec = pl.BlockSpec((1, 1, tc), lambda j, b: (order, 0, j))
    return pl.pallas_call(
        functools.partial(_long_conv_kernel, tf=tf),
        grid=(nj, bsz),
        in_specs=[
            pl.BlockSpec((1, L, tc), lambda j, b: (b, 0, z_col * nj + j)),
            pl.BlockSpec((1, L, tc), lambda j, b: (b, 0, xn_col * nj + j)),
            spec, spec, vec, vec, tab, tab,
        ],
        out_specs=pl.BlockSpec((1, L, tc), lambda j, b: (b, 0, j)),
        out_shape=jax.ShapeDtypeStruct((bsz, L, C), jnp.float32),
        scratch_shapes=[pltpu.VMEM((L, tc), jnp.float32)],
        compiler_params=pltpu.CompilerParams(
            dimension_semantics=("arbitrary", "arbitrary"), vmem_limit_bytes=VMEM_LIMIT_BYTES),
    )(z_arr, xn_arr, kr, ks, kn, bias, cos_t, sin_t)


def hyena_long_convs(u, k, h_bias, tc, tf):
    L = u.shape[1]
    cos_t, sin_t = dft_tables(L)
    k_f = jnp.moveaxis(k[:, :, 0], 1, 0)
    k_b = jnp.moveaxis(k[:, :, 1], 1, 0).at[:, 0].set(0.0)
    kr, ks, kn = filter_spectrum(k_f + k_b, k_f - k_b, cos_t, sin_t, tc)
    bias = h_bias[:, None, :]
    z1 = long_conv_gate(u, 0, u, 1, kr, ks, kn, bias, 0, cos_t, sin_t, tc, tf)
    return long_conv_gate(z1, 0, u, 2, kr, ks, kn, bias, 1, cos_t, sin_t, tc, tf)


def _ssd_kernel(xf_ref, df_ref, xb_ref, db_ref, dskip_ref, y_ref, h_ref, *, n_ctx_chunks):
    s = pl.program_id(1)
    n_steps = pl.num_programs(1)
    Q, G, R, P, N = SSD_CHUNK, SSD_GROUPS, SSD_HPG, SSD_HEAD_DIM, SSD_STATE
    GP = R * P
    bf = jnp.bfloat16

    @pl.when(s == 0)
    def _():
        h_ref[...] = jnp.zeros_like(h_ref)
        y_ref[...] = jnp.zeros_like(y_ref)

    row = lax.broadcasted_iota(jnp.int32, (Q, Q), 0)
    col = lax.broadcasted_iota(jnp.int32, (Q, Q), 1)
    lane_head = lax.broadcasted_iota(jnp.int32, (Q, GP), 1) // P
    is_latent = s >= n_ctx_chunks
    out_chunk = (s - n_ctx_chunks, n_steps - 1 - s)

    for d, (x_ref, da_ref) in enumerate(((xf_ref, df_ref), (xb_ref, db_ref))):
        mask = (row >= col) if d == 0 else (col >= row)
        tri = mask.astype(jnp.float32)
        da = da_ref[0]
        cum = jnp.dot(tri, da, precision=lax.Precision.HIGHEST, preferred_element_type=jnp.float32)
        cum_t = cum.T
        edge = Q - 1 if d == 0 else 0
        blk = x_ref.at[0]
        for g in range(G):
            xg = blk[:, g * GP:(g + 1) * GP]
            bg = blk[:, D_SSD + g * N:D_SSD + (g + 1) * N].astype(bf)
            cg = blk[:, D_SSD + G * N + g * N:D_SSD + G * N + (g + 1) * N].astype(bf)
            heads = [d * SSD_HEADS + g * R + r for r in range(R)]
            dtm = jnp.zeros((Q, GP), jnp.float32)
            cumm = jnp.zeros((Q, GP), jnp.float32)
            totm = jnp.zeros((Q, GP), jnp.float32)
            for r, h in enumerate(heads):
                sel = lane_head == r
                dtm = jnp.where(sel, da[:, h:h + 1], dtm)
                cumm = jnp.where(sel, cum[:, SSD_HEADS * 2 + h:SSD_HEADS * 2 + h + 1], cumm)
                totm = jnp.where(sel, cum[edge:edge + 1, SSD_HEADS * 2 + h:SSD_HEADS * 2 + h + 1], totm)
            xdt = xg * dtm
            hg = h_ref[d, g * GP:(g + 1) * GP, :]

            @pl.when(is_latent)
            def _():
                gmat = lax.dot_general(cg, bg, (((1,), (1,)), ((), ())), preferred_element_type=jnp.float32)
                y_off = lax.dot_general(cg, hg.astype(bf), (((1,), (1,)), ((), ())),
                                        preferred_element_type=jnp.float32) * jnp.exp(cumm)
                if d == 0:
                    y_off = y_off + dskip_ref[:, g * GP:(g + 1) * GP] * xg
                parts = []
                for r, h in enumerate(heads):
                    a_col = cum[:, SSD_HEADS * 2 + h:SSD_HEADS * 2 + h + 1]
                    a_row = cum_t[SSD_HEADS * 2 + h:SSD_HEADS * 2 + h + 1, :]
                    decay = jnp.exp(jnp.where(mask, a_col - a_row, NEG_BIG))
                    parts.append(jnp.dot((gmat * decay).astype(bf), xdt[:, r * P:(r + 1) * P].astype(bf),
                                         preferred_element_type=jnp.float32))
                y = y_off + jnp.concatenate(parts, axis=-1)
                rows = pl.ds(pl.multiple_of(out_chunk[d] * Q, Q), Q)
                y_ref[0, rows, g * GP:(g + 1) * GP] += y

            xw = (xdt * jnp.exp(totm - cumm)).astype(bf)
            st = lax.dot_general(xw, bg, (((0,), (0,)), ((), ())), preferred_element_type=jnp.float32)
            for r, h in enumerate(heads):
                dec = jnp.exp(cum_t[SSD_HEADS * 2 + h:SSD_HEADS * 2 + h + 1, edge:edge + 1])
                rs = slice(g * GP + r * P, g * GP + (r + 1) * P)
                h_ref[d, rs, :] = h_ref[d, rs, :] * dec + st[r * P:(r + 1) * P, :]


def ssd_scan_bidir(xbc, dta, d_skip, n_ctx):
    bsz, lt, width = xbc.shape
    Q = SSD_CHUNK
    n_steps = lt // Q
    n_ctx_chunks = n_ctx // Q
    L = lt - n_ctx

    def bwd_chunk(s):
        return jnp.where(s < n_ctx_chunks, n_ctx_chunks - 1 - s, n_steps - 1 - s + n_ctx_chunks)

    return pl.pallas_call(
        functools.partial(_ssd_kernel, n_ctx_chunks=n_ctx_chunks),
        grid=(bsz, n_steps),
        in_specs=[
            pl.BlockSpec((1, Q, width), lambda b, s: (b, s, 0)),
            pl.BlockSpec((1, Q, LANES), lambda b, s: (b, s, 0)),
            pl.BlockSpec((1, Q, width), lambda b, s: (b, bwd_chunk(s), 0)),
            pl.BlockSpec((1, Q, LANES), lambda b, s: (b, bwd_chunk(s), 0)),
            pl.BlockSpec((1, D_SSD), lambda b, s: (0, 0)),
        ],
        out_specs=pl.BlockSpec((1, L, D_SSD), lambda b, s: (b, 0, 0)),
        out_shape=jax.ShapeDtypeStruct((bsz, L, D_SSD), jnp.float32),
        scratch_shapes=[pltpu.VMEM((2, SSD_GROUPS * SSD_HPG * SSD_HEAD_DIM, SSD_STATE), jnp.float32)],
        compiler_params=pltpu.CompilerParams(
            dimension_semantics=("arbitrary", "arbitrary"), vmem_limit_bytes=VMEM_LIMIT_BYTES),
    )(xbc, dta, xbc, dta, d_skip)


def _hyena_filters(L, f_w1, f_b1, f_freq, f_w2, f_b2, f_w3):
    f32 = jnp.float32
    hp = lax.Precision.HIGHEST
    pos = jnp.arange(L, dtype=f32)
    t = pos / max(L - 1, 1)
    bands = jnp.linspace(1e-4, HYENA_BANDS - 1, HYENA_BANDS, dtype=f32)
    ang = (2 * math.pi / L) * pos[:, None] * bands[None, :]
    feats = jnp.concatenate([t[:, None], jnp.cos(ang), -jnp.sin(ang)], axis=-1)
    h = jnp.sin(f_freq * (jnp.dot(feats, f_w1, precision=hp) + f_b1))
    h = jnp.sin(f_freq * (jnp.dot(h, f_w2, precision=hp) + f_b2))
    k = jnp.dot(h, f_w3, precision=hp).reshape(L, HYENA_ORDER, 2, D_HYENA)
    deltas = jnp.abs(jnp.linspace(math.log(HYENA_TARGET) / HYENA_SLOW_DECAY,
                                  math.log(HYENA_TARGET) / HYENA_FAST_DECAY, D_HYENA, dtype=f32))
    window = jnp.exp(-t[:, None] * deltas[None, :])
    return k * window[:, None, None, :]


def kernel(x, c, ctx, c_ctx, w_ada, b_ada, g_norm1, g_norm2, w_in, hy_conv_w, hy_conv_b, hy_f_w1, hy_f_b1, hy_f_freq, hy_f_w2, hy_f_b2, hy_f_w3, hy_bias, ssd_conv_w, ssd_conv_b, ssd_a_log, ssd_dt_bias, ssd_d, ssd_norm_g, w_out, w_group, b_group, w_expert, b_expert, w1, w3, w2, g_final):
    bsz, seq_len, _ = x.shape
    l = 0
    hp = lax.Precision.HIGHEST
    mod =(jnp.dot(jax.nn.silu(c), w_ada[l], precision=hp) + b_ada[l])[:, None, :]
    sh1, sc1, ga1, sh2, sc2, ga2 = jnp.split(mod, 6, axis=-1)
    cmod = jnp.dot(jax.nn.silu(c_ctx), w_ada[l], precision=hp) + b_ada[l]
    csh1, csc1, _, _, _, _ = jnp.split(cmod, 6, axis=-1)

    w_out_bf = w_out[l].astype(jnp.bfloat16)
    w_in_bf, w_dt_bf, dt_bias2, dt_mult = in_proj_params(w_in[l], ssd_a_log[l], ssd_dt_bias[l],
                                                         HY_COLS, D_SSD, D_XBC)

    u, z, xbc, dta = in_proj_fused(ctx, x, g_norm1[l][None, :], csh1[None, :], csc1[None, :], sh1, sc1,
                                   w_in_bf, w_dt_bf, hy_conv_w[l], hy_conv_b[l][None, :],
                                   ssd_conv_w[l], ssd_conv_b[l][None, :], dt_bias2, dt_mult, GRID_W, 256, 512)
    k = _hyena_filters(seq_len, hy_f_w1[l], hy_f_b1[l], hy_f_freq[l], hy_f_w2[l], hy_f_b2[l], hy_f_w3[l])
    y_hy = hyena_long_convs(u, k, hy_bias[l], 256, 512)
    y_scan = ssd_scan_bidir(xbc, dta, jnp.repeat(ssd_d[l], SSD_HEAD_DIM)[None, :], CTX_LEN)

    pad = LANES - MOE_GROUPS - N_EXPERTS
    w_router = jnp.concatenate([w_group[l], w_expert[l], jnp.zeros((D_MODEL, pad), jnp.float32)], axis=1)
    b_router = jnp.concatenate([b_group[l], b_expert[l], jnp.zeros((pad,), jnp.float32)])[None, :]
    x1, hn, route_i, route_f, counts = out_proj_router(
        y_hy, y_scan, z, 0, x, ga1, sc2, sh2, ssd_norm_g[l][None, :], g_norm2[l][None, :],
        w_out_bf, w_router, b_router, 256)
    n_tok = bsz * seq_len
    n_blocks = -(-n_tok * TOP_K // MOE_BLOCK) + N_EXPERTS
    dest, block_eid, block_first, block_valid = moe_plan(route_i, counts, MOE_BLOCK, n_blocks)
    buf = moe_dispatch(hn.reshape(n_tok, D_MODEL), dest, block_valid, MOE_BLOCK, 256)
    yb = expert_blocks(buf, block_eid, block_first, block_valid, w1[l], w3[l], w2[l], MOE_BLOCK)
    return moe_combine(x1, route_f, ga2, g_final[None, :], yb, dest, 256)
```

```python
import functools
import math

import jax
import jax.numpy as jnp
from jax import lax
from jax.experimental import pallas as pl
from jax.experimental.pallas import tpu as pltpu

D_MODEL = 1024
CTX_LEN = 256
GRID_W = 64
EPS = 1e-6
SHORT_CONV = 3

D_HYENA = D_MODEL // 2
HYENA_ORDER = 2
HYENA_BANDS = 8
HYENA_FAST_DECAY = 0.3
HYENA_SLOW_DECAY = 1.5
HYENA_TARGET = 1e-2

D_SSD = D_MODEL // 2
SSD_HEAD_DIM = 64
SSD_HEADS = D_SSD // SSD_HEAD_DIM
SSD_GROUPS = 2
SSD_HPG = SSD_HEADS // SSD_GROUPS
SSD_STATE = 128
SSD_CHUNK = 128

D_XBC = D_SSD + 2 * SSD_GROUPS * SSD_STATE
HY_COLS = (HYENA_ORDER + 1) * D_HYENA
D_IN = HY_COLS + D_SSD + D_XBC + 2 * SSD_HEADS
LANES = 128
D_IN_PAD = -(-D_IN // LANES) * LANES

MOE_GROUPS = 8
EXPERTS_PER_GROUP = 8
N_EXPERTS = MOE_GROUPS * EXPERTS_PER_GROUP
TOP_K = 2
D_EXPERT = 512
MOE_BLOCK = 128
ROUTE_COLS = 8

VMEM_LIMIT_BYTES = 56 * 1024 * 1024
NEG_BIG = -1e30


def _conv3_rows(p, w_ref, b_ref, cols, has_prev, has_next):
    n = p.shape[0]
    prev = jnp.where(has_prev, pltpu.roll(p, 1, 0), 0.0)
    nxt = jnp.where(has_next, pltpu.roll(p, n - 1, 0), 0.0)
    return b_ref[:, cols] + w_ref[0:1, cols] * prev + w_ref[1:2, cols] * p + w_ref[2:3, cols] * nxt


def _in_proj_kernel(ctx_ref, x_ref, g_ref, csh_ref, csc_ref, sh_ref, sc_ref, w_ref, wdt_ref, hw_ref, hb_ref,
                    sw_ref, sb_ref, dtb_ref, dtm_ref, u_ref, z_ref, xbc_ref, dta_ref, h_ref,
                    *, n_ctx_steps, row_len, ctx_row_len, hy_cols, d_ssd, d_xbc, tn):
    i = pl.program_id(1)
    is_ctx = i < n_ctx_steps
    tm = x_ref.shape[1]
    xin = jnp.where(is_ctx, ctx_ref[0], x_ref[0])
    shift = jnp.where(is_ctx, csh_ref[...], sh_ref[0])
    scale = jnp.where(is_ctx, csc_ref[...], sc_ref[0])
    y = xin * lax.rsqrt(jnp.mean(xin * xin, axis=-1, keepdims=True) + EPS) * g_ref[...]
    h_ref[...] = (y * (1.0 + scale) + shift).astype(jnp.bfloat16)

    pos = lax.broadcasted_iota(jnp.int32, (tm, 1), 0) + jnp.where(is_ctx, i, i - n_ctx_steps) * tm
    in_row = jnp.where(is_ctx, pos % ctx_row_len, pos % row_len)
    has_prev = in_row != 0
    has_next = in_row != jnp.where(is_ctx, ctx_row_len - 1, row_len - 1)

    @pl.when(jnp.logical_not(is_ctx))
    def _():
        for c0 in range(0, hy_cols, tn):
            cols = slice(c0, c0 + tn)
            p = jnp.dot(h_ref[...], w_ref[:, cols], preferred_element_type=jnp.float32)
            u_ref[0, :, cols] = _conv3_rows(p, hw_ref, hb_ref, cols, has_prev, has_next)
        z_ref[0] = jnp.dot(h_ref[...], w_ref[:, hy_cols:hy_cols + d_ssd], preferred_element_type=jnp.float32)

    for c0 in range(0, d_xbc, tn):
        cols = slice(c0, c0 + tn)
        wc = slice(hy_cols + d_ssd + c0, hy_cols + d_ssd + c0 + tn)
        p = jnp.dot(h_ref[...], w_ref[:, wc], preferred_element_type=jnp.float32)
        v = _conv3_rows(p, sw_ref, sb_ref, cols, has_prev, has_next)
        xbc_ref[0, :, cols] = v * jax.nn.sigmoid(v)
    pd = jnp.dot(h_ref[...], wdt_ref[...], preferred_element_type=jnp.float32) + dtb_ref[...]
    sp = jnp.maximum(pd, 0.0) + jnp.log(1.0 + jnp.exp(-jnp.abs(pd)))
    dta_ref[0] = sp * dtm_ref[...]


def in_proj_fused(ctx, x, g1, csh, csc, sh, sc, w_bf, wdt_bf, hy_w, hy_b, ssd_w, ssd_b, dt_bias2, dt_mult,
                  row_len, tm, tn):
    bsz, L, D = x.shape
    lc = ctx.shape[1]
    hy_cols = hy_w.shape[1]
    d_xbc = ssd_w.shape[1]
    d_ssd = w_bf.shape[1] - hy_cols - d_xbc
    n_ctx_steps = lc // tm
    n_steps = n_ctx_steps + L // tm
    lat = lambda b, i: (b, jnp.maximum(i - n_ctx_steps, 0), 0)
    allt = lambda b, i: (b, i, 0)
    const2 = lambda b, i: (0, 0)
    per_b = pl.BlockSpec((1, 1, D), lambda b, i: (b, 0, 0))
    kern = functools.partial(_in_proj_kernel, n_ctx_steps=n_ctx_steps, row_len=row_len, ctx_row_len=lc,
                             hy_cols=hy_cols, d_ssd=d_ssd, d_xbc=d_xbc, tn=tn)
    return pl.pallas_call(
        kern,
        grid=(bsz, n_steps),
        in_specs=[
            pl.BlockSpec((1, tm, D), lambda b, i: (b, jnp.minimum(i, n_ctx_steps - 1), 0)),
            pl.BlockSpec((1, tm, D), lat),
            pl.BlockSpec((1, D), const2),
            pl.BlockSpec((1, D), const2),
            pl.BlockSpec((1, D), const2),
            per_b, per_b,
            pl.BlockSpec(w_bf.shape, const2),
            pl.BlockSpec(wdt_bf.shape, const2),
            pl.BlockSpec(hy_w.shape, const2),
            pl.BlockSpec(hy_b.shape, const2),
            pl.BlockSpec(ssd_w.shape, const2),
            pl.BlockSpec(ssd_b.shape, const2),
            pl.BlockSpec((1, LANES), const2),
            pl.BlockSpec((1, LANES), const2),
        ],
        out_specs=[
            pl.BlockSpec((1, tm, hy_cols), lat),
            pl.BlockSpec((1, tm, d_ssd), lat),
            pl.BlockSpec((1, tm, d_xbc), allt),
            pl.BlockSpec((1, tm, LANES), allt),
        ],
        out_shape=[
            jax.ShapeDtypeStruct((bsz, L, hy_cols), jnp.float32),
            jax.ShapeDtypeStruct((bsz, L, d_ssd), jnp.float32),
            jax.ShapeDtypeStruct((bsz, lc + L, d_xbc), jnp.float32),
            jax.ShapeDtypeStruct((bsz, lc + L, LANES), jnp.float32),
        ],
        scratch_shapes=[pltpu.VMEM((tm, D), jnp.bfloat16)],
        compiler_params=pltpu.CompilerParams(
            dimension_semantics=("arbitrary", "arbitrary"), vmem_limit_bytes=VMEM_LIMIT_BYTES),
    )(ctx, x, g1, csh, csc, sh, sc, w_bf, wdt_bf, hy_w, hy_b, ssd_w, ssd_b, dt_bias2, dt_mult)


def in_proj_params(w_in, a_log, dt_bias, hy_cols, d_ssd, d_xbc):
    n_h = 2 * SSD_HEADS
    main = hy_cols + d_ssd + d_xbc
    w_dt = w_in[:, main:main + n_h]
    pad = jnp.zeros((w_in.shape[0], LANES - 2 * n_h), w_in.dtype)
    wdt = jnp.concatenate([w_dt, w_dt, pad], axis=1).astype(jnp.bfloat16)
    zpad = jnp.zeros((LANES - 2 * n_h,), jnp.float32)
    bias2 = jnp.concatenate([dt_bias.reshape(n_h), dt_bias.reshape(n_h), zpad])[None, :]
    mult = jnp.concatenate([jnp.ones((n_h,), jnp.float32), -jnp.exp(a_log).reshape(n_h), zpad])[None, :]
    return w_in[:, :main].astype(jnp.bfloat16), wdt, bias2, mult


def _out_router_kernel(yh_ref, ys_ref, z_ref, x_ref, ga_ref, sc_ref, sh_ref, ng_ref, g2_ref, wo_ref, wr_ref, br_ref,
                       x1_ref, hn_ref, ri_ref, rf_ref, cnt_ref, carry_ref):
    first = jnp.logical_and(pl.program_id(0) == 0, pl.program_id(1) == 0)

    @pl.when(first)
    def _():
        carry_ref[...] = jnp.zeros_like(carry_ref)

    bf = jnp.bfloat16
    tm = x_ref.shape[1]
    dh = yh_ref.shape[2]
    z = z_ref[0]
    ys = ys_ref[0] * (z * jax.nn.sigmoid(z))
    gw = ys.shape[1] // SSD_GROUPS
    acc = jnp.dot(yh_ref[0].astype(bf), wo_ref[0:dh, :], preferred_element_type=jnp.float32)
    for g in range(SSD_GROUPS):
        yg = ys[:, g * gw:(g + 1) * gw]
        yg = yg * lax.rsqrt(jnp.mean(yg * yg, axis=-1, keepdims=True) + EPS) * ng_ref[:, g * gw:(g + 1) * gw]
        acc += jnp.dot(yg.astype(bf), wo_ref[dh + g * gw:dh + (g + 1) * gw, :], preferred_element_type=jnp.float32)
    x1 = x_ref[0] + ga_ref[0] * acc
    x1_ref[0] = x1
    hn = x1 * lax.rsqrt(jnp.mean(x1 * x1, axis=-1, keepdims=True) + EPS) * g2_ref[...]
    hn = hn * (1.0 + sc_ref[0]) + sh_ref[0]
    hn_ref[0] = hn

    hn_hi = hn.astype(bf)
    hn_lo = (hn - hn_hi.astype(jnp.float32)).astype(bf)
    logits = (jnp.dot(hn_hi, wr_ref[0], preferred_element_type=jnp.float32)
              + jnp.dot(hn_lo, wr_ref[0], preferred_element_type=jnp.float32)
              + jnp.dot(hn_hi, wr_ref[1], preferred_element_type=jnp.float32)) + br_ref[...]
    lane = lax.broadcasted_iota(jnp.int32, (tm, LANES), 1)
    lane_f = lane.astype(jnp.float32)
    ninf = jnp.float32(-jnp.inf)
    big = jnp.float32(1e9)
    gl = jnp.where(lane < MOE_GROUPS, logits, ninf)
    gmax = jnp.max(gl, axis=-1, keepdims=True)
    p_group = 1.0 / jnp.sum(jnp.exp(gl - gmax), axis=-1, keepdims=True)
    g_sel = jnp.min(jnp.where(gl == gmax, lane_f, big), axis=-1, keepdims=True)
    e_lane = lane - MOE_GROUPS
    in_grp = jnp.logical_and(e_lane >= 0, (e_lane // EXPERTS_PER_GROUP).astype(jnp.float32) == g_sel)
    el = jnp.where(in_grp, logits, ninf)
    m1 = jnp.max(el, axis=-1, keepdims=True)
    i1 = jnp.min(jnp.where(el == m1, lane_f, big), axis=-1, keepdims=True)
    el2 = jnp.where(lane_f == i1, ninf, el)
    m2 = jnp.max(el2, axis=-1, keepdims=True)
    i2 = jnp.min(jnp.where(el2 == m2, lane_f, big), axis=-1, keepdims=True)
    t = jnp.exp(m2 - m1)
    w1 = 1.0 / (1.0 + t)
    gate1 = w1 * p_group
    gate2 = (t * w1) * p_group
    e1 = i1 - MOE_GROUPS
    e2 = i2 - MOE_GROUPS
    el_f = e_lane.astype(jnp.float32)
    oh1 = el_f == e1
    oh2 = el_f == e2
    oh = jnp.logical_or(oh1, oh2).astype(bf)
    r_i = lax.broadcasted_iota(jnp.int32, (tm, tm), 0)
    c_i = lax.broadcasted_iota(jnp.int32, (tm, tm), 1)
    before = jnp.dot((c_i < r_i).astype(bf), oh, preferred_element_type=jnp.float32) + carry_ref[...]
    rank1 = jnp.sum(jnp.where(oh1, before, 0.0), axis=-1, keepdims=True)
    rank2 = jnp.sum(jnp.where(oh2, before, 0.0), axis=-1, keepdims=True)
    carry_ref[...] += jnp.sum(oh.astype(jnp.float32), axis=0, keepdims=True)
    cnt_ref[...] = carry_ref[...]

    col = lax.broadcasted_iota(jnp.int32, (tm, ROUTE_COLS), 1)
    rec_i = jnp.where(col == 0, e1, jnp.where(col == 1, e2, jnp.where(col == 2, rank1, rank2)))
    ri_ref[0] = rec_i.astype(jnp.int32)
    rf_ref[0] = jnp.where(col == 0, gate1, gate2)


def out_proj_router(y_hy, y_scan, px, z_col, x, ga1, sc2, sh2, norm_g, g2, w_out_bf, w_router, b_router, tm):
    bsz, L, D = x.shape
    dh = y_hy.shape[-1]
    ds = y_scan.shape[-1]
    tok = lambda b, i: (b, i, 0)
    per_b = pl.BlockSpec((1, 1, D), lambda b, i: (b, 0, 0))
    const2 = lambda b, i: (0, 0)
    return pl.pallas_call(
        _out_router_kernel,
        grid=(bsz, L // tm),
        in_specs=[
            pl.BlockSpec((1, tm, dh), tok),
            pl.BlockSpec((1, tm, ds), tok),
            pl.BlockSpec((1, tm, ds), lambda b, i: (b, i, z_col)),
            pl.BlockSpec((1, tm, D), tok),
            per_b, per_b, per_b,
            pl.BlockSpec((1, ds), const2),
            pl.BlockSpec((1, D), const2),
            pl.BlockSpec((dh + ds, D), const2),
            pl.BlockSpec((2, D, LANES), lambda b, i: (0, 0, 0)),
            pl.BlockSpec((1, LANES), const2),
        ],
        out_specs=[
            pl.BlockSpec((1, tm, D), tok),
            pl.BlockSpec((1, tm, D), tok),
            pl.BlockSpec((1, tm, ROUTE_COLS), tok),
            pl.BlockSpec((1, tm, ROUTE_COLS), tok),
            pl.BlockSpec((1, LANES), const2),
        ],
        out_shape=[
            jax.ShapeDtypeStruct((bsz, L, D), jnp.float32),
            jax.ShapeDtypeStruct((bsz, L, D), jnp.float32),
            jax.ShapeDtypeStruct((bsz, L, ROUTE_COLS), jnp.int32),
            jax.ShapeDtypeStruct((bsz, L, ROUTE_COLS), jnp.float32),
            jax.ShapeDtypeStruct((1, LANES), jnp.float32),
        ],
        scratch_shapes=[pltpu.VMEM((1, LANES), jnp.float32)],
        compiler_params=pltpu.CompilerParams(
            dimension_semantics=("arbitrary", "arbitrary"), vmem_limit_bytes=VMEM_LIMIT_BYTES),
    )(y_hy, y_scan, px, x, ga1, sc2, sh2, norm_g, g2, w_out_bf, w_router, b_router)


def _row_copy(src_hbm, src_row, dst_ref, dst_row, sem):
    return pltpu.make_async_copy(src_hbm.at[pl.ds(src_row, 1), :], dst_ref.at[pl.ds(dst_row, 1), :], sem)


def _dispatch_kernel(dest_ref, valid_ref, hn_ref, buf_hbm, zeros, sem, zsem):
    step = pl.program_id(0)
    tm = hn_ref.shape[0]
    blk = zeros.shape[0]
    n_blocks = buf_hbm.shape[0] // blk

    def zero_copy(i):
        return pltpu.make_async_copy(zeros, buf_hbm.at[pl.ds(pl.multiple_of(i * blk, blk), blk), :], zsem)

    def zfill(i, carry):
        @pl.when(valid_ref[i] < blk)
        def _():
            zero_copy(i).start()
        return carry

    def zwait(i, carry):
        @pl.when(valid_ref[i] < blk)
        def _():
            zero_copy(i).wait()
        return carry

    @pl.when(step == 0)
    def _():
        zeros[...] = jnp.zeros_like(zeros)
        lax.fori_loop(0, n_blocks, zfill, 0)
        lax.fori_loop(0, n_blocks, zwait, 0)

    def body(j, carry):
        t = step * tm + j
        _row_copy(hn_ref, j, buf_hbm, dest_ref[2 * t], sem).start()
        _row_copy(hn_ref, j, buf_hbm, dest_ref[2 * t + 1], sem).start()
        return carry

    lax.fori_loop(0, tm, body, 0, unroll=8)
    for _ in range(2):
        pltpu.make_async_copy(hn_ref, buf_hbm.at[pl.ds(0, tm), :], sem).wait()


def moe_dispatch(hn, dest, block_valid, blk, tm):
    T, D = hn.shape
    n_rows = block_valid.shape[0] * blk
    grid_spec = pltpu.PrefetchScalarGridSpec(
        num_scalar_prefetch=2,
        grid=(T // tm,),
        in_specs=[pl.BlockSpec((tm, D), lambda i, d, v: (i, 0))],
        out_specs=pl.BlockSpec(memory_space=pl.ANY),
        scratch_shapes=[pltpu.VMEM((blk, D), jnp.float32), pltpu.SemaphoreType.DMA(()),
                        pltpu.SemaphoreType.DMA(())],
    )
    return pl.pallas_call(
        _dispatch_kernel,
        grid_spec=grid_spec,
        out_shape=jax.ShapeDtypeStruct((n_rows, D), jnp.float32),
        compiler_params=pltpu.CompilerParams(dimension_semantics=("arbitrary",), has_side_effects=True),
    )(dest, block_valid, hn)


def _expert_kernel(eid_ref, first_ref, valid_ref, x_ref, w1_ref, w3_ref, w2_ref, o_ref, w1b, w3b, w2b):
    i = pl.program_id(0)
    del eid_ref
    bf = jnp.bfloat16

    @pl.when(first_ref[i] == 1)
    def _():
        w1b[...] = w1_ref[0].astype(bf)
        w3b[...] = w3_ref[0].astype(bf)
        w2b[...] = w2_ref[0].astype(bf)

    valid = valid_ref[i]

    @pl.when(valid > 0)
    def _():
        xb = x_ref[...].astype(bf)
        a = jnp.dot(xb, w1b[...], preferred_element_type=jnp.float32)
        b = jnp.dot(xb, w3b[...], preferred_element_type=jnp.float32)
        h = (a * jax.nn.sigmoid(a)) * b
        o_ref[...] = jnp.dot(h.astype(bf), w2b[...], preferred_element_type=jnp.float32)

    @pl.when(valid <= 0)
    def _():
        o_ref[...] = jnp.zeros_like(o_ref)


def expert_blocks(buf, block_eid, block_first, block_valid, w1, w3, w2, blk):
    rows, D = buf.shape
    n_blocks = rows // blk
    E, _, F = w1.shape
    grid_spec = pltpu.PrefetchScalarGridSpec(
        num_scalar_prefetch=3,
        grid=(n_blocks,),
        in_specs=[
            pl.BlockSpec((blk, D), lambda i, eid, fi, va: (i, 0)),
            pl.BlockSpec((1, D, F), lambda i, eid, fi, va: (eid[i], 0, 0)),
            pl.BlockSpec((1, D, F), lambda i, eid, fi, va: (eid[i], 0, 0)),
            pl.BlockSpec((1, F, D), lambda i, eid, fi, va: (eid[i], 0, 0)),
        ],
        out_specs=pl.BlockSpec((blk, D), lambda i, eid, fi, va: (i, 0)),
        scratch_shapes=[pltpu.VMEM((D, F), jnp.bfloat16), pltpu.VMEM((D, F), jnp.bfloat16),
                        pltpu.VMEM((F, D), jnp.bfloat16)],
    )
    return pl.pallas_call(
        _expert_kernel,
        grid_spec=grid_spec,
        out_shape=jax.ShapeDtypeStruct((rows, D), jnp.float32),
        compiler_params=pltpu.CompilerParams(
            dimension_semantics=("arbitrary",), vmem_limit_bytes=VMEM_LIMIT_BYTES),
    )(block_eid, block_first, block_valid, buf, w1, w3, w2)


def _combine_kernel(dest_ref, x1_ref, rf_ref, ga_ref, gf_ref, yb_hbm, o_ref, ybuf, sem):
    b = pl.program_id(0)
    i = pl.program_id(1)
    n_i = pl.num_programs(1)
    tm = x1_ref.shape[1]
    step = b * n_i + i
    n_steps = pl.num_programs(0) * n_i
    slot = step % 2

    def issue(step_, slot_):
        def body(j, carry):
            t = step_ * tm + j
            _row_copy(yb_hbm, dest_ref[2 * t], ybuf.at[slot_, 0], j, sem.at[slot_]).start()
            _row_copy(yb_hbm, dest_ref[2 * t + 1], ybuf.at[slot_, 1], j, sem.at[slot_]).start()
            return carry
        lax.fori_loop(0, tm, body, 0, unroll=8)

    @pl.when(step == 0)
    def _():
        issue(0, 0)

    @pl.when(step + 1 < n_steps)
    def _():
        issue(step + 1, 1 - slot)

    pltpu.make_async_copy(yb_hbm.at[pl.ds(0, tm), :], ybuf.at[slot, 0], sem.at[slot]).wait()
    pltpu.make_async_copy(yb_hbm.at[pl.ds(0, tm), :], ybuf.at[slot, 1], sem.at[slot]).wait()
    rf = rf_ref[0]
    y = rf[:, 0:1] * ybuf[slot, 0] + rf[:, 1:2] * ybuf[slot, 1]
    x2 = x1_ref[0] + ga_ref[0] * y
    o_ref[0] = x2 * lax.rsqrt(jnp.mean(x2 * x2, axis=-1, keepdims=True) + EPS) * gf_ref[...]


def moe_combine(x1, route_f, ga2, g_final, yb, dest, tm):
    bsz, L, D = x1.shape
    grid_spec = pltpu.PrefetchScalarGridSpec(
        num_scalar_prefetch=1,
        grid=(bsz, L // tm),
        in_specs=[
            pl.BlockSpec((1, tm, D), lambda b, i, d: (b, i, 0)),
            pl.BlockSpec((1, tm, ROUTE_COLS), lambda b, i, d: (b, i, 0)),
            pl.BlockSpec((1, 1, D), lambda b, i, d: (b, 0, 0)),
            pl.BlockSpec((1, D), lambda b, i, d: (0, 0)),
            pl.BlockSpec(memory_space=pl.ANY),
        ],
        out_specs=pl.BlockSpec((1, tm, D), lambda b, i, d: (b, i, 0)),
        scratch_shapes=[pltpu.VMEM((2, 2, tm, D), jnp.float32), pltpu.SemaphoreType.DMA((2,))],
    )
    return pl.pallas_call(
        _combine_kernel,
        grid_spec=grid_spec,
        out_shape=jax.ShapeDtypeStruct((bsz, L, D), jnp.float32),
        compiler_params=pltpu.CompilerParams(
            dimension_semantics=("arbitrary", "arbitrary"), vmem_limit_bytes=VMEM_LIMIT_BYTES),
    )(dest, x1, route_f, ga2, g_final, yb)


def moe_plan(route_i, counts, blk, n_blocks):
    cnt = counts[0, MOE_GROUPS:MOE_GROUPS + N_EXPERTS].astype(jnp.int32)
    padded = (cnt + blk - 1) // blk * blk
    ends = jnp.cumsum(padded)
    starts = ends - padded
    e = route_i[..., 0:2].reshape(-1, 2)
    rank = route_i[..., 2:4].reshape(-1, 2)
    dest = (starts[e] + rank).reshape(-1)
    first_row = jnp.arange(n_blocks, dtype=jnp.int32) * blk
    block_eid = jnp.minimum(jnp.sum((ends[None, :] <= first_row[:, None]).astype(jnp.int32), axis=1), N_EXPERTS - 1)
    block_valid = jnp.clip(cnt[block_eid] - (first_row - starts[block_eid]), 0, blk).astype(jnp.int32)
    block_first = jnp.concatenate([jnp.ones((1,), jnp.int32),
                                   (block_eid[1:] != block_eid[:-1]).astype(jnp.int32)])
    return dest, block_eid, block_first, block_valid


def dft_tables(L):
    n = 2 * L
    f = lax.broadcasted_iota(jnp.int32, (L, L), 0)
    t = lax.broadcasted_iota(jnp.int32, (L, L), 1)
    ang = ((f * t) % n).astype(jnp.float32) * (2.0 * math.pi / n)
    return jnp.cos(ang).astype(jnp.bfloat16), jnp.sin(ang).astype(jnp.bfloat16)


def _alt_sign(L):
    t = lax.broadcasted_iota(jnp.int32, (L, 1), 0)
    return (1 - 2 * (t & 1)).astype(jnp.float32)


def _spectrum_kernel(a_ref, b_ref, c_ref, s_ref, kr_ref, ks_ref, kn_ref):
    L = a_ref.shape[1]
    a = a_ref[0]
    row = lax.broadcasted_iota(jnp.int32, (L, 1), 0)
    scale = jnp.where(row == 0, 0.5 / L, 1.0 / L)
    kr_ref[0] = scale * jnp.dot(c_ref[...], a.astype(jnp.bfloat16), preferred_element_type=jnp.float32)
    ks_ref[0] = scale * jnp.dot(s_ref[...], b_ref[0].astype(jnp.bfloat16), preferred_element_type=jnp.float32)
    kn_ref[0] = jnp.sum(a * _alt_sign(L), axis=0, keepdims=True) * (0.5 / L)


def filter_spectrum(a, b, cos_t, sin_t, tc):
    n, L, C = a.shape
    blk = pl.BlockSpec((1, L, tc), lambda o, j: (o, 0, j))
    tab = pl.BlockSpec((L, L), lambda o, j: (0, 0))
    return pl.pallas_call(
        _spectrum_kernel,
        grid=(n, C // tc),
        in_specs=[blk, blk, tab, tab],
        out_specs=[blk, blk, pl.BlockSpec((1, 1, tc), lambda o, j: (o, 0, j))],
        out_shape=[jax.ShapeDtypeStruct((n, L, C), jnp.float32)] * 2 + [jax.ShapeDtypeStruct((n, 1, C), jnp.float32)],
        compiler_params=pltpu.CompilerParams(
            dimension_semantics=("arbitrary", "arbitrary"), vmem_limit_bytes=VMEM_LIMIT_BYTES),
    )(a, b, cos_t, sin_t)


def _long_conv_kernel(z_ref, xn_ref, kr_ref, ks_ref, kn_ref, bias_ref, c_ref, s_ref, o_ref, acc_ref, *, tf):
    L = z_ref.shape[1]
    z = z_ref[0]
    zb = z.astype(jnp.bfloat16)
    sign = _alt_sign(L)
    z_nyq = jnp.sum(z * sign, axis=0, keepdims=True)
    acc_ref[...] = z * bias_ref[0] + sign * (z_nyq * kn_ref[0])
    for ft in range(L // tf):
        rows = pl.ds(ft * tf, tf)
        zr = jnp.dot(c_ref[rows, :], zb, preferred_element_type=jnp.float32)
        zs = jnp.dot(s_ref[rows, :], zb, preferred_element_type=jnp.float32)
        kr = kr_ref[0, rows, :]
        ks = ks_ref[0, rows, :]
        yr = (zr * kr - zs * ks).astype(jnp.bfloat16)
        ys = (zr * ks + zs * kr).astype(jnp.bfloat16)
        acc_ref[...] += (jnp.dot(c_ref[:, rows], yr, preferred_element_type=jnp.float32)
                         + jnp.dot(s_ref[:, rows], ys, preferred_element_type=jnp.float32))
    o_ref[0] = xn_ref[0] * acc_ref[...]


def long_conv_gate(z_arr, z_col, xn_arr, xn_col, kr, ks, kn, bias, order, cos_t, sin_t, tc, tf):
    bsz, L, _ = z_arr.shape
    C = kr.shape[-1]
    nj = C // tc
    tab = pl.BlockSpec((L, L), lambda j, b: (0, 0), pipeline_mode=pl.Buffered(1))
    spec = pl.BlockSpec((1, L, tc), lambda j, b: (order, 0, j), pipeline_mode=pl.Buffered(1))
    vec = pl.BlockSpec((1, 1, tc), lambda j, b: (order, 0, j))
    return pl.pallas_call(
        functools.partial(_long_conv_kernel, tf=tf),
        grid=(nj, bsz),
        in_specs=[
            pl.BlockSpec((1, L, tc), lambda j, b: (b, 0, z_col * nj + j)),
            pl.BlockSpec((1, L, tc), lambda j, b: (b, 0, xn_col * nj + j)),
            spec, spec, vec, vec, tab, tab,
        ],
        out_specs=pl.BlockSpec((1, L, tc), lambda j, b: (b, 0, j)),
        out_shape=jax.ShapeDtypeStruct((bsz, L, C), jnp.float32),
        scratch_shapes=[pltpu.VMEM((L, tc), jnp.float32)],
        compiler_params=pltpu.CompilerParams(
            dimension_semantics=("arbitrary", "arbitrary"), vmem_limit_bytes=VMEM_LIMIT_BYTES),
    )(z_arr, xn_arr, kr, ks, kn, bias, cos_t, sin_t)


def hyena_long_convs(u, k, h_bias, tc, tf):
    L = u.shape[1]
    cos_t, sin_t = dft_tables(L)
    k_f = jnp.moveaxis(k[:, :, 0], 1, 0)
    k_b = jnp.moveaxis(k[:, :, 1], 1, 0).at[:, 0].set(0.0)
    kr, ks, kn = filter_spectrum(k_f + k_b, k_f - k_b, cos_t, sin_t, tc)
    bias = h_bias[:, None, :]
    z1 = long_conv_gate(u, 0, u, 1, kr, ks, kn, bias, 0, cos_t, sin_t, tc, tf)
    return long_conv_gate(z1, 0, u, 2, kr, ks, kn, bias, 1, cos_t, sin_t, tc, tf)


def _ssd_kernel(xf_ref, df_ref, xb_ref, db_ref, dskip_ref, y_ref, h_ref, *, n_ctx_chunks):
    s = pl.program_id(1)
    n_steps = pl.num_programs(1)
    Q, G, R, P, N = SSD_CHUNK, SSD_GROUPS, SSD_HPG, SSD_HEAD_DIM, SSD_STATE
    GP = R * P
    bf = jnp.bfloat16

    @pl.when(s == 0)
    def _():
        h_ref[...] = jnp.zeros_like(h_ref)
        y_ref[...] = jnp.zeros_like(y_ref)

    row = lax.broadcasted_iota(jnp.int32, (Q, Q), 0)
    col = lax.broadcasted_iota(jnp.int32, (Q, Q), 1)
    lane_head = lax.broadcasted_iota(jnp.int32, (Q, GP), 1) // P
    is_latent = s >= n_ctx_chunks
    n_lat = n_steps - n_ctx_chunks
    out_chunk = (jnp.clip(s - n_ctx_chunks, 0, n_lat - 1), jnp.clip(n_steps - 1 - s, 0, n_lat - 1))

    for d, (x_ref, da_ref) in enumerate(((xf_ref, df_ref), (xb_ref, db_ref))):
        mask = (row >= col) if d == 0 else (col >= row)
        tri = mask.astype(jnp.float32)
        da = da_ref[0]
        cum = jnp.dot(tri, da, precision=lax.Precision.HIGHEST, preferred_element_type=jnp.float32)
        cum_t = cum.T
        edge = Q - 1 if d == 0 else 0
        blk = x_ref.at[0]
        for g in range(G):
            xg = blk[:, g * GP:(g + 1) * GP]
            bg = blk[:, D_SSD + g * N:D_SSD + (g + 1) * N].astype(bf)
            cg = blk[:, D_SSD + G * N + g * N:D_SSD + G * N + (g + 1) * N].astype(bf)
            heads = [d * SSD_HEADS + g * R + r for r in range(R)]
            dtm = jnp.zeros((Q, GP), jnp.float32)
            cumm = jnp.zeros((Q, GP), jnp.float32)
            totm = jnp.zeros((Q, GP), jnp.float32)
            for r, h in enumerate(heads):
                sel = lane_head == r
                dtm = jnp.where(sel, da[:, h:h + 1], dtm)
                cumm = jnp.where(sel, cum[:, SSD_HEADS * 2 + h:SSD_HEADS * 2 + h + 1], cumm)
                totm = jnp.where(sel, cum[edge:edge + 1, SSD_HEADS * 2 + h:SSD_HEADS * 2 + h + 1], totm)
            xdt = xg * dtm
            hg = h_ref[d, g * GP:(g + 1) * GP, :]

            gmat = lax.dot_general(cg, bg, (((1,), (1,)), ((), ())), preferred_element_type=jnp.float32)
            y_off = lax.dot_general(cg, hg.astype(bf), (((1,), (1,)), ((), ())),
                                    preferred_element_type=jnp.float32) * jnp.exp(cumm)
            if d == 0:
                y_off = y_off + dskip_ref[:, g * GP:(g + 1) * GP] * xg
            parts = []
            for r, h in enumerate(heads):
                a_col = cum[:, SSD_HEADS * 2 + h:SSD_HEADS * 2 + h + 1]
                a_row = cum_t[SSD_HEADS * 2 + h:SSD_HEADS * 2 + h + 1, :]
                decay = jnp.exp(jnp.where(mask, a_col - a_row, NEG_BIG))
                parts.append(jnp.dot((gmat * decay).astype(bf), xdt[:, r * P:(r + 1) * P].astype(bf),
                                     preferred_element_type=jnp.float32))
            y = jnp.where(is_latent, y_off + jnp.concatenate(parts, axis=-1), 0.0)
            rows = pl.ds(pl.multiple_of(out_chunk[d] * Q, Q), Q)
            y_ref[0, rows, g * GP:(g + 1) * GP] += y

            xw = (xdt * jnp.exp(totm - cumm)).astype(bf)
            st = lax.dot_general(xw, bg, (((0,), (0,)), ((), ())), preferred_element_type=jnp.float32)
            for r, h in enumerate(heads):
                dec = jnp.exp(cum_t[SSD_HEADS * 2 + h:SSD_HEADS * 2 + h + 1, edge:edge + 1])
                rs = slice(g * GP + r * P, g * GP + (r + 1) * P)
                h_ref[d, rs, :] = h_ref[d, rs, :] * dec + st[r * P:(r + 1) * P, :]


def ssd_scan_bidir(xbc, dta, d_skip, n_ctx):
    bsz, lt, width = xbc.shape
    Q = SSD_CHUNK
    n_steps = lt // Q
    n_ctx_chunks = n_ctx // Q
    L = lt - n_ctx

    def bwd_chunk(s):
        return jnp.where(s < n_ctx_chunks, n_ctx_chunks - 1 - s, n_steps - 1 - s + n_ctx_chunks)

    return pl.pallas_call(
        functools.partial(_ssd_kernel, n_ctx_chunks=n_ctx_chunks),
        grid=(bsz, n_steps),
        in_specs=[
            pl.BlockSpec((1, Q, width), lambda b, s: (b, s, 0)),
            pl.BlockSpec((1, Q, LANES), lambda b, s: (b, s, 0)),
            pl.BlockSpec((1, Q, width), lambda b, s: (b, bwd_chunk(s), 0)),
            pl.BlockSpec((1, Q, LANES), lambda b, s: (b, bwd_chunk(s), 0)),
            pl.BlockSpec((1, D_SSD), lambda b, s: (0, 0)),
        ],
        out_specs=pl.BlockSpec((1, L, D_SSD), lambda b, s: (b, 0, 0)),
        out_shape=jax.ShapeDtypeStruct((bsz, L, D_SSD), jnp.float32),
        scratch_shapes=[pltpu.VMEM((2, SSD_GROUPS * SSD_HPG * SSD_HEAD_DIM, SSD_STATE), jnp.float32)],
        compiler_params=pltpu.CompilerParams(
            dimension_semantics=("arbitrary", "arbitrary"), vmem_limit_bytes=VMEM_LIMIT_BYTES),
    )(xbc, dta, xbc, dta, d_skip)


def _hyena_filters(L, f_w1, f_b1, f_freq, f_w2, f_b2, f_w3):
    f32 = jnp.float32
    hp = lax.Precision.HIGHEST
    pos = jnp.arange(L, dtype=f32)
    t = pos / max(L - 1, 1)
    bands = jnp.linspace(1e-4, HYENA_BANDS - 1, HYENA_BANDS, dtype=f32)
    ang = (2 * math.pi / L) * pos[:, None] * bands[None, :]
    feats = jnp.concatenate([t[:, None], jnp.cos(ang), -jnp.sin(ang)], axis=-1)
    h = jnp.sin(f_freq * (jnp.dot(feats, f_w1, precision=hp) + f_b1))
    h = jnp.sin(f_freq * (jnp.dot(h, f_w2, precision=hp) + f_b2))
    k = jnp.dot(h, f_w3, precision=hp).reshape(L, HYENA_ORDER, 2, D_HYENA)
    deltas = jnp.abs(jnp.linspace(math.log(HYENA_TARGET) / HYENA_SLOW_DECAY,
                                  math.log(HYENA_TARGET) / HYENA_FAST_DECAY, D_HYENA, dtype=f32))
    window = jnp.exp(-t[:, None] * deltas[None, :])
    return k * window[:, None, None, :]


def kernel(x, c, ctx, c_ctx, w_ada, b_ada, g_norm1, g_norm2, w_in, hy_conv_w, hy_conv_b, hy_f_w1, hy_f_b1, hy_f_freq, hy_f_w2, hy_f_b2, hy_f_w3, hy_bias, ssd_conv_w, ssd_conv_b, ssd_a_log, ssd_dt_bias, ssd_d, ssd_norm_g, w_out, w_group, b_group, w_expert, b_expert, w1, w3, w2, g_final):
    bsz, seq_len, _ = x.shape
    l = 0
    hp = lax.Precision.HIGHEST
    mod =(jnp.dot(jax.nn.silu(c), w_ada[l], precision=hp) + b_ada[l])[:, None, :]
    sh1, sc1, ga1, sh2, sc2, ga2 = jnp.split(mod, 6, axis=-1)
    cmod = jnp.dot(jax.nn.silu(c_ctx), w_ada[l], precision=hp) + b_ada[l]
    csh1, csc1, _, _, _, _ = jnp.split(cmod, 6, axis=-1)

    w_out_bf = w_out[l].astype(jnp.bfloat16)
    w_in_bf, w_dt_bf, dt_bias2, dt_mult = in_proj_params(w_in[l], ssd_a_log[l], ssd_dt_bias[l],
                                                         HY_COLS, D_SSD, D_XBC)

    u, z, xbc, dta = in_proj_fused(ctx, x, g_norm1[l][None, :], csh1[None, :], csc1[None, :], sh1, sc1,
                                   w_in_bf, w_dt_bf, hy_conv_w[l], hy_conv_b[l][None, :],
                                   ssd_conv_w[l], ssd_conv_b[l][None, :], dt_bias2, dt_mult, GRID_W, 256, 512)
    k = _hyena_filters(seq_len, hy_f_w1[l], hy_f_b1[l], hy_f_freq[l], hy_f_w2[l], hy_f_b2[l], hy_f_w3[l])
    y_hy = hyena_long_convs(u, k, hy_bias[l], 256, 512)
    y_scan = ssd_scan_bidir(xbc, dta, jnp.repeat(ssd_d[l], SSD_HEAD_DIM)[None, :], CTX_LEN)

    pad = LANES - MOE_GROUPS - N_EXPERTS
    w_router = jnp.concatenate([w_group[l], w_expert[l], jnp.zeros((D_MODEL, pad), jnp.float32)], axis=1)
    w_router_hi = w_router.astype(jnp.bfloat16)
    w_router_lo = (w_router - w_router_hi.astype(jnp.float32)).astype(jnp.bfloat16)
    w_router = jnp.stack([w_router_hi, w_router_lo])
    b_router = jnp.concatenate([b_group[l], b_expert[l], jnp.zeros((pad,), jnp.float32)])[None, :]
    x1, hn, route_i, route_f, counts = out_proj_router(
        y_hy, y_scan, z, 0, x, ga1, sc2, sh2, ssd_norm_g[l][None, :], g_norm2[l][None, :],
        w_out_bf, w_router, b_router, 256)
    n_tok = bsz * seq_len
    n_blocks = -(-n_tok * TOP_K // MOE_BLOCK) + N_EXPERTS
    dest, block_eid, block_first, block_valid = moe_plan(route_i, counts, MOE_BLOCK, n_blocks)
    buf = moe_dispatch(hn.reshape(n_tok, D_MODEL), dest, block_valid, MOE_BLOCK, 256)
    yb = expert_blocks(buf, block_eid, block_first, block_valid, w1[l], w3[l], w2[l], MOE_BLOCK)
    return moe_combine(x1, route_f, ga2, g_final[None, :], yb, dest, 256)
```

```python
import functools
import math

import jax
import jax.numpy as jnp
from jax import lax
from jax.experimental import pallas as pl
from jax.experimental.pallas import tpu as pltpu

D_MODEL = 1024
CTX_LEN = 256
GRID_W = 64
EPS = 1e-6
SHORT_CONV = 3

D_HYENA = D_MODEL // 2
HYENA_ORDER = 2
HYENA_BANDS = 8
HYENA_FAST_DECAY = 0.3
HYENA_SLOW_DECAY = 1.5
HYENA_TARGET = 1e-2

D_SSD = D_MODEL // 2
SSD_HEAD_DIM = 64
SSD_HEADS = D_SSD // SSD_HEAD_DIM
SSD_GROUPS = 2
SSD_HPG = SSD_HEADS // SSD_GROUPS
SSD_STATE = 128
SSD_CHUNK = 128

D_XBC = D_SSD + 2 * SSD_GROUPS * SSD_STATE
HY_COLS = (HYENA_ORDER + 1) * D_HYENA
D_IN = HY_COLS + D_SSD + D_XBC + 2 * SSD_HEADS
LANES = 128
D_IN_PAD = -(-D_IN // LANES) * LANES

MOE_GROUPS = 8
EXPERTS_PER_GROUP = 8
N_EXPERTS = MOE_GROUPS * EXPERTS_PER_GROUP
TOP_K = 2
D_EXPERT = 512
MOE_BLOCK = 256
ROUTE_COLS = 8

VMEM_LIMIT_BYTES = 56 * 1024 * 1024
NEG_BIG = -1e30


def _conv3_rows(p, w_ref, b_ref, cols, has_prev, has_next):
    n = p.shape[0]
    prev = jnp.where(has_prev, pltpu.roll(p, 1, 0), 0.0)
    nxt = jnp.where(has_next, pltpu.roll(p, n - 1, 0), 0.0)
    return b_ref[:, cols] + w_ref[0:1, cols] * prev + w_ref[1:2, cols] * p + w_ref[2:3, cols] * nxt


def _in_proj_kernel(ctx_ref, x_ref, g_ref, csh_ref, csc_ref, sh_ref, sc_ref, w_ref, wdt_ref, hw_ref, hb_ref,
                    sw_ref, sb_ref, dtb_ref, dtm_ref, u_ref, z_ref, xbc_ref, dta_ref, h_ref,
                    *, n_ctx_steps, row_len, ctx_row_len, hy_cols, d_ssd, d_xbc, tn):
    i = pl.program_id(1)
    is_ctx = i < n_ctx_steps
    tm = x_ref.shape[1]
    xin = jnp.where(is_ctx, ctx_ref[0], x_ref[0])
    shift = jnp.where(is_ctx, csh_ref[...], sh_ref[0])
    scale = jnp.where(is_ctx, csc_ref[...], sc_ref[0])
    y = xin * lax.rsqrt(jnp.mean(xin * xin, axis=-1, keepdims=True) + EPS) * g_ref[...]
    h_ref[...] = (y * (1.0 + scale) + shift).astype(jnp.bfloat16)

    pos = lax.broadcasted_iota(jnp.int32, (tm, 1), 0) + jnp.where(is_ctx, i, i - n_ctx_steps) * tm
    in_row = jnp.where(is_ctx, pos % ctx_row_len, pos % row_len)
    has_prev = in_row != 0
    has_next = in_row != jnp.where(is_ctx, ctx_row_len - 1, row_len - 1)

    @pl.when(jnp.logical_not(is_ctx))
    def _():
        for c0 in range(0, hy_cols, tn):
            cols = slice(c0, c0 + tn)
            p = jnp.dot(h_ref[...], w_ref[:, cols], preferred_element_type=jnp.float32)
            u_ref[0, :, cols] = _conv3_rows(p, hw_ref, hb_ref, cols, has_prev, has_next)
        z_ref[0] = jnp.dot(h_ref[...], w_ref[:, hy_cols:hy_cols + d_ssd], preferred_element_type=jnp.float32)

    for c0 in range(0, d_xbc, tn):
        cols = slice(c0, c0 + tn)
        wc = slice(hy_cols + d_ssd + c0, hy_cols + d_ssd + c0 + tn)
        p = jnp.dot(h_ref[...], w_ref[:, wc], preferred_element_type=jnp.float32)
        v = _conv3_rows(p, sw_ref, sb_ref, cols, has_prev, has_next)
        xbc_ref[0, :, cols] = v * jax.nn.sigmoid(v)
    pd = jnp.dot(h_ref[...], wdt_ref[...], preferred_element_type=jnp.float32) + dtb_ref[...]
    sp = jnp.maximum(pd, 0.0) + jnp.log(1.0 + jnp.exp(-jnp.abs(pd)))
    dta_ref[0] = sp * dtm_ref[...]


def in_proj_fused(ctx, x, g1, csh, csc, sh, sc, w_bf, wdt_bf, hy_w, hy_b, ssd_w, ssd_b, dt_bias2, dt_mult,
                  row_len, tm, tn):
    bsz, L, D = x.shape
    lc = ctx.shape[1]
    hy_cols = hy_w.shape[1]
    d_xbc = ssd_w.shape[1]
    d_ssd = w_bf.shape[1] - hy_cols - d_xbc
    n_ctx_steps = lc // tm
    n_steps = n_ctx_steps + L // tm
    lat = lambda b, i: (b, jnp.maximum(i - n_ctx_steps, 0), 0)
    allt = lambda b, i: (b, i, 0)
    const2 = lambda b, i: (0, 0)
    per_b = pl.BlockSpec((1, 1, D), lambda b, i: (b, 0, 0))
    kern = functools.partial(_in_proj_kernel, n_ctx_steps=n_ctx_steps, row_len=row_len, ctx_row_len=lc,
                             hy_cols=hy_cols, d_ssd=d_ssd, d_xbc=d_xbc, tn=tn)
    return pl.pallas_call(
        kern,
        grid=(bsz, n_steps),
        in_specs=[
            pl.BlockSpec((1, tm, D), lambda b, i: (b, jnp.minimum(i, n_ctx_steps - 1), 0)),
            pl.BlockSpec((1, tm, D), lat),
            pl.BlockSpec((1, D), const2),
            pl.BlockSpec((1, D), const2),
            pl.BlockSpec((1, D), const2),
            per_b, per_b,
            pl.BlockSpec(w_bf.shape, const2),
            pl.BlockSpec(wdt_bf.shape, const2),
            pl.BlockSpec(hy_w.shape, const2),
            pl.BlockSpec(hy_b.shape, const2),
            pl.BlockSpec(ssd_w.shape, const2),
            pl.BlockSpec(ssd_b.shape, const2),
            pl.BlockSpec((1, LANES), const2),
            pl.BlockSpec((1, LANES), const2),
        ],
        out_specs=[
            pl.BlockSpec((1, tm, hy_cols), lat),
            pl.BlockSpec((1, tm, d_ssd), lat),
            pl.BlockSpec((1, tm, d_xbc), allt),
            pl.BlockSpec((1, tm, LANES), allt),
        ],
        out_shape=[
            jax.ShapeDtypeStruct((bsz, L, hy_cols), jnp.float32),
            jax.ShapeDtypeStruct((bsz, L, d_ssd), jnp.float32),
            jax.ShapeDtypeStruct((bsz, lc + L, d_xbc), jnp.float32),
            jax.ShapeDtypeStruct((bsz, lc + L, LANES), jnp.float32),
        ],
        scratch_shapes=[pltpu.VMEM((tm, D), jnp.bfloat16)],
        compiler_params=pltpu.CompilerParams(
            dimension_semantics=("arbitrary", "arbitrary"), vmem_limit_bytes=VMEM_LIMIT_BYTES),
    )(ctx, x, g1, csh, csc, sh, sc, w_bf, wdt_bf, hy_w, hy_b, ssd_w, ssd_b, dt_bias2, dt_mult)


def in_proj_params(w_in, a_log, dt_bias, hy_cols, d_ssd, d_xbc):
    n_h = 2 * SSD_HEADS
    main = hy_cols + d_ssd + d_xbc
    w_dt = w_in[:, main:main + n_h]
    pad = jnp.zeros((w_in.shape[0], LANES - 2 * n_h), w_in.dtype)
    wdt = jnp.concatenate([w_dt, w_dt, pad], axis=1).astype(jnp.bfloat16)
    zpad = jnp.zeros((LANES - 2 * n_h,), jnp.float32)
    bias2 = jnp.concatenate([dt_bias.reshape(n_h), dt_bias.reshape(n_h), zpad])[None, :]
    mult = jnp.concatenate([jnp.ones((n_h,), jnp.float32), -jnp.exp(a_log).reshape(n_h), zpad])[None, :]
    return w_in[:, :main].astype(jnp.bfloat16), wdt, bias2, mult


def _out_router_kernel(yh_ref, ys_ref, z_ref, x_ref, ga_ref, sc_ref, sh_ref, ng_ref, g2_ref, wo_ref, wr_ref, br_ref,
                       x1_ref, hn_ref, ri_ref, rf_ref, cnt_ref, carry_ref):
    first = jnp.logical_and(pl.program_id(0) == 0, pl.program_id(1) == 0)

    @pl.when(first)
    def _():
        carry_ref[...] = jnp.zeros_like(carry_ref)

    bf = jnp.bfloat16
    tm = x_ref.shape[1]
    dh = yh_ref.shape[2]
    z = z_ref[0]
    ys = ys_ref[0] * (z * jax.nn.sigmoid(z))
    gw = ys.shape[1] // SSD_GROUPS
    acc = jnp.dot(yh_ref[0].astype(bf), wo_ref[0:dh, :], preferred_element_type=jnp.float32)
    for g in range(SSD_GROUPS):
        yg = ys[:, g * gw:(g + 1) * gw]
        yg = yg * lax.rsqrt(jnp.mean(yg * yg, axis=-1, keepdims=True) + EPS) * ng_ref[:, g * gw:(g + 1) * gw]
        acc += jnp.dot(yg.astype(bf), wo_ref[dh + g * gw:dh + (g + 1) * gw, :], preferred_element_type=jnp.float32)
    x1 = x_ref[0] + ga_ref[0] * acc
    x1_ref[0] = x1
    hn = x1 * lax.rsqrt(jnp.mean(x1 * x1, axis=-1, keepdims=True) + EPS) * g2_ref[...]
    hn = hn * (1.0 + sc_ref[0]) + sh_ref[0]
    hn_ref[0] = hn

    hn_hi = hn.astype(bf)
    hn_lo = (hn - hn_hi.astype(jnp.float32)).astype(bf)
    logits = (jnp.dot(hn_hi, wr_ref[0], preferred_element_type=jnp.float32)
              + jnp.dot(hn_lo, wr_ref[0], preferred_element_type=jnp.float32)
              + jnp.dot(hn_hi, wr_ref[1], preferred_element_type=jnp.float32)) + br_ref[...]
    lane = lax.broadcasted_iota(jnp.int32, (tm, LANES), 1)
    lane_f = lane.astype(jnp.float32)
    ninf = jnp.float32(-jnp.inf)
    big = jnp.float32(1e9)
    gl = jnp.where(lane < MOE_GROUPS, logits, ninf)
    gmax = jnp.max(gl, axis=-1, keepdims=True)
    p_group = 1.0 / jnp.sum(jnp.exp(gl - gmax), axis=-1, keepdims=True)
    g_sel = jnp.min(jnp.where(gl == gmax, lane_f, big), axis=-1, keepdims=True)
    e_lane = lane - MOE_GROUPS
    in_grp = jnp.logical_and(e_lane >= 0, (e_lane // EXPERTS_PER_GROUP).astype(jnp.float32) == g_sel)
    el = jnp.where(in_grp, logits, ninf)
    m1 = jnp.max(el, axis=-1, keepdims=True)
    i1 = jnp.min(jnp.where(el == m1, lane_f, big), axis=-1, keepdims=True)
    el2 = jnp.where(lane_f == i1, ninf, el)
    m2 = jnp.max(el2, axis=-1, keepdims=True)
    i2 = jnp.min(jnp.where(el2 == m2, lane_f, big), axis=-1, keepdims=True)
    t = jnp.exp(m2 - m1)
    w1 = 1.0 / (1.0 + t)
    gate1 = w1 * p_group
    gate2 = (t * w1) * p_group
    e1 = i1 - MOE_GROUPS
    e2 = i2 - MOE_GROUPS
    el_f = e_lane.astype(jnp.float32)
    oh1 = el_f == e1
    oh2 = el_f == e2
    oh = jnp.logical_or(oh1, oh2).astype(bf)
    r_i = lax.broadcasted_iota(jnp.int32, (tm, tm), 0)
    c_i = lax.broadcasted_iota(jnp.int32, (tm, tm), 1)
    before = jnp.dot((c_i < r_i).astype(bf), oh, preferred_element_type=jnp.float32) + carry_ref[...]
    rank1 = jnp.sum(jnp.where(oh1, before, 0.0), axis=-1, keepdims=True)
    rank2 = jnp.sum(jnp.where(oh2, before, 0.0), axis=-1, keepdims=True)
    carry_ref[...] += jnp.sum(oh.astype(jnp.float32), axis=0, keepdims=True)
    cnt_ref[...] = carry_ref[...]

    col = lax.broadcasted_iota(jnp.int32, (tm, ROUTE_COLS), 1)
    rec_i = jnp.where(col == 0, e1, jnp.where(col == 1, e2, jnp.where(col == 2, rank1, rank2)))
    ri_ref[0] = rec_i.astype(jnp.int32)
    rf_ref[0] = jnp.where(col == 0, gate1, gate2)


def out_proj_router(y_hy, y_scan, px, z_col, x, ga1, sc2, sh2, norm_g, g2, w_out_bf, w_router, b_router, tm):
    bsz, L, D = x.shape
    dh = y_hy.shape[-1]
    ds = y_scan.shape[-1]
    tok = lambda b, i: (b, i, 0)
    per_b = pl.BlockSpec((1, 1, D), lambda b, i: (b, 0, 0))
    const2 = lambda b, i: (0, 0)
    return pl.pallas_call(
        _out_router_kernel,
        grid=(bsz, L // tm),
        in_specs=[
            pl.BlockSpec((1, tm, dh), tok),
            pl.BlockSpec((1, tm, ds), tok),
            pl.BlockSpec((1, tm, ds), lambda b, i: (b, i, z_col)),
            pl.BlockSpec((1, tm, D), tok),
            per_b, per_b, per_b,
            pl.BlockSpec((1, ds), const2),
            pl.BlockSpec((1, D), const2),
            pl.BlockSpec((dh + ds, D), const2),
            pl.BlockSpec((2, D, LANES), lambda b, i: (0, 0, 0)),
            pl.BlockSpec((1, LANES), const2),
        ],
        out_specs=[
            pl.BlockSpec((1, tm, D), tok),
            pl.BlockSpec((1, tm, D), tok),
            pl.BlockSpec((1, tm, ROUTE_COLS), tok),
            pl.BlockSpec((1, tm, ROUTE_COLS), tok),
            pl.BlockSpec((1, LANES), const2),
        ],
        out_shape=[
            jax.ShapeDtypeStruct((bsz, L, D), jnp.float32),
            jax.ShapeDtypeStruct((bsz, L, D), jnp.float32),
            jax.ShapeDtypeStruct((bsz, L, ROUTE_COLS), jnp.int32),
            jax.ShapeDtypeStruct((bsz, L, ROUTE_COLS), jnp.float32),
            jax.ShapeDtypeStruct((1, LANES), jnp.float32),
        ],
        scratch_shapes=[pltpu.VMEM((1, LANES), jnp.float32)],
        compiler_params=pltpu.CompilerParams(
            dimension_semantics=("arbitrary", "arbitrary"), vmem_limit_bytes=VMEM_LIMIT_BYTES),
    )(y_hy, y_scan, px, x, ga1, sc2, sh2, norm_g, g2, w_out_bf, w_router, b_router)


def _row_copy(src_hbm, src_row, dst_ref, dst_row, sem):
    return pltpu.make_async_copy(src_hbm.at[pl.ds(src_row, 1), :], dst_ref.at[pl.ds(dst_row, 1), :], sem)


def _dispatch_kernel(dest_ref, valid_ref, hn_ref, buf_hbm, zeros, sem, zsem):
    step = pl.program_id(0)
    tm = hn_ref.shape[0]
    blk = zeros.shape[0]
    n_blocks = buf_hbm.shape[0] // blk

    def zero_copy(i):
        return pltpu.make_async_copy(zeros, buf_hbm.at[pl.ds(pl.multiple_of(i * blk, blk), blk), :], zsem)

    def zfill(i, carry):
        @pl.when(valid_ref[i] < blk)
        def _():
            zero_copy(i).start()
        return carry

    def zwait(i, carry):
        @pl.when(valid_ref[i] < blk)
        def _():
            zero_copy(i).wait()
        return carry

    @pl.when(step == 0)
    def _():
        zeros[...] = jnp.zeros_like(zeros)
        lax.fori_loop(0, n_blocks, zfill, 0)
        lax.fori_loop(0, n_blocks, zwait, 0)

    def body(j, carry):
        t = step * tm + j
        _row_copy(hn_ref, j, buf_hbm, dest_ref[2 * t], sem).start()
        _row_copy(hn_ref, j, buf_hbm, dest_ref[2 * t + 1], sem).start()
        return carry

    lax.fori_loop(0, tm, body, 0, unroll=8)
    for _ in range(2):
        pltpu.make_async_copy(hn_ref, buf_hbm.at[pl.ds(0, tm), :], sem).wait()


def moe_dispatch(hn, dest, block_valid, blk, tm):
    T, D = hn.shape
    n_rows = block_valid.shape[0] * blk
    grid_spec = pltpu.PrefetchScalarGridSpec(
        num_scalar_prefetch=2,
        grid=(T // tm,),
        in_specs=[pl.BlockSpec((tm, D), lambda i, d, v: (i, 0))],
        out_specs=pl.BlockSpec(memory_space=pl.ANY),
        scratch_shapes=[pltpu.VMEM((blk, D), jnp.float32), pltpu.SemaphoreType.DMA(()),
                        pltpu.SemaphoreType.DMA(())],
    )
    return pl.pallas_call(
        _dispatch_kernel,
        grid_spec=grid_spec,
        out_shape=jax.ShapeDtypeStruct((n_rows, D), jnp.float32),
        compiler_params=pltpu.CompilerParams(dimension_semantics=("arbitrary",), has_side_effects=True),
    )(dest, block_valid, hn)


def _expert_kernel(eid_ref, first_ref, valid_ref, x_ref, w1_ref, w3_ref, w2_ref, o_ref, w1b, w3b, w2b):
    i = pl.program_id(0)
    del eid_ref
    bf = jnp.bfloat16

    @pl.when(first_ref[i] == 1)
    def _():
        w1b[...] = w1_ref[0].astype(bf)
        w3b[...] = w3_ref[0].astype(bf)
        w2b[...] = w2_ref[0].astype(bf)

    valid = valid_ref[i]

    @pl.when(valid > 0)
    def _():
        xb = x_ref[...].astype(bf)
        a = jnp.dot(xb, w1b[...], preferred_element_type=jnp.float32)
        b = jnp.dot(xb, w3b[...], preferred_element_type=jnp.float32)
        h = (a * jax.nn.sigmoid(a)) * b
        o_ref[...] = jnp.dot(h.astype(bf), w2b[...], preferred_element_type=jnp.float32)

    @pl.when(valid <= 0)
    def _():
        o_ref[...] = jnp.zeros_like(o_ref)


def expert_blocks(buf, block_eid, block_first, block_valid, w1, w3, w2, blk):
    rows, D = buf.shape
    n_blocks = rows // blk
    E, _, F = w1.shape
    grid_spec = pltpu.PrefetchScalarGridSpec(
        num_scalar_prefetch=3,
        grid=(n_blocks,),
        in_specs=[
            pl.BlockSpec((blk, D), lambda i, eid, fi, va: (i, 0)),
            pl.BlockSpec((1, D, F), lambda i, eid, fi, va: (eid[i], 0, 0)),
            pl.BlockSpec((1, D, F), lambda i, eid, fi, va: (eid[i], 0, 0)),
            pl.BlockSpec((1, F, D), lambda i, eid, fi, va: (eid[i], 0, 0)),
        ],
        out_specs=pl.BlockSpec((blk, D), lambda i, eid, fi, va: (i, 0)),
        scratch_shapes=[pltpu.VMEM((D, F), jnp.bfloat16), pltpu.VMEM((D, F), jnp.bfloat16),
                        pltpu.VMEM((F, D), jnp.bfloat16)],
    )
    return pl.pallas_call(
        _expert_kernel,
        grid_spec=grid_spec,
        out_shape=jax.ShapeDtypeStruct((rows, D), jnp.float32),
        compiler_params=pltpu.CompilerParams(
            dimension_semantics=("arbitrary",), vmem_limit_bytes=VMEM_LIMIT_BYTES),
    )(block_eid, block_first, block_valid, buf, w1, w3, w2)


def _combine_kernel(dest_ref, x1_ref, rf_ref, ga_ref, gf_ref, yb_hbm, o_ref, ybuf, sem):
    b = pl.program_id(0)
    i = pl.program_id(1)
    n_i = pl.num_programs(1)
    tm = x1_ref.shape[1]
    step = b * n_i + i
    n_steps = pl.num_programs(0) * n_i
    slot = step % 2

    def issue(step_, slot_):
        def body(j, carry):
            t = step_ * tm + j
            _row_copy(yb_hbm, dest_ref[2 * t], ybuf.at[slot_, 0], j, sem.at[slot_]).start()
            _row_copy(yb_hbm, dest_ref[2 * t + 1], ybuf.at[slot_, 1], j, sem.at[slot_]).start()
            return carry
        lax.fori_loop(0, tm, body, 0, unroll=8)

    @pl.when(step == 0)
    def _():
        issue(0, 0)

    @pl.when(step + 1 < n_steps)
    def _():
        issue(step + 1, 1 - slot)

    pltpu.make_async_copy(yb_hbm.at[pl.ds(0, tm), :], ybuf.at[slot, 0], sem.at[slot]).wait()
    pltpu.make_async_copy(yb_hbm.at[pl.ds(0, tm), :], ybuf.at[slot, 1], sem.at[slot]).wait()
    rf = rf_ref[0]
    y = rf[:, 0:1] * ybuf[slot, 0] + rf[:, 1:2] * ybuf[slot, 1]
    x2 = x1_ref[0] + ga_ref[0] * y
    o_ref[0] = x2 * lax.rsqrt(jnp.mean(x2 * x2, axis=-1, keepdims=True) + EPS) * gf_ref[...]


def moe_combine(x1, route_f, ga2, g_final, yb, dest, tm):
    bsz, L, D = x1.shape
    grid_spec = pltpu.PrefetchScalarGridSpec(
        num_scalar_prefetch=1,
        grid=(bsz, L // tm),
        in_specs=[
            pl.BlockSpec((1, tm, D), lambda b, i, d: (b, i, 0)),
            pl.BlockSpec((1, tm, ROUTE_COLS), lambda b, i, d: (b, i, 0)),
            pl.BlockSpec((1, 1, D), lambda b, i, d: (b, 0, 0)),
            pl.BlockSpec((1, D), lambda b, i, d: (0, 0)),
            pl.BlockSpec(memory_space=pl.ANY),
        ],
        out_specs=pl.BlockSpec((1, tm, D), lambda b, i, d: (b, i, 0)),
        scratch_shapes=[pltpu.VMEM((2, 2, tm, D), jnp.float32), pltpu.SemaphoreType.DMA((2,))],
    )
    return pl.pallas_call(
        _combine_kernel,
        grid_spec=grid_spec,
        out_shape=jax.ShapeDtypeStruct((bsz, L, D), jnp.float32),
        compiler_params=pltpu.CompilerParams(
            dimension_semantics=("arbitrary", "arbitrary"), vmem_limit_bytes=VMEM_LIMIT_BYTES),
    )(dest, x1, route_f, ga2, g_final, yb)


def moe_plan(route_i, counts, blk, n_blocks):
    cnt = counts[0, MOE_GROUPS:MOE_GROUPS + N_EXPERTS].astype(jnp.int32)
    padded = (cnt + blk - 1) // blk * blk
    ends = jnp.cumsum(padded)
    starts = ends - padded
    e = route_i[..., 0:2].reshape(-1, 2)
    rank = route_i[..., 2:4].reshape(-1, 2)
    dest = (starts[e] + rank).reshape(-1)
    first_row = jnp.arange(n_blocks, dtype=jnp.int32) * blk
    block_eid = jnp.minimum(jnp.sum((ends[None, :] <= first_row[:, None]).astype(jnp.int32), axis=1), N_EXPERTS - 1)
    block_valid = jnp.clip(cnt[block_eid] - (first_row - starts[block_eid]), 0, blk).astype(jnp.int32)
    block_first = jnp.concatenate([jnp.ones((1,), jnp.int32),
                                   (block_eid[1:] != block_eid[:-1]).astype(jnp.int32)])
    return dest, block_eid, block_first, block_valid


def dft_tables(L):
    n = 2 * L
    f = lax.broadcasted_iota(jnp.int32, (L, L), 0)
    t = lax.broadcasted_iota(jnp.int32, (L, L), 1)
    ang = ((f * t) % n).astype(jnp.float32) * (2.0 * math.pi / n)
    return jnp.cos(ang).astype(jnp.bfloat16), jnp.sin(ang).astype(jnp.bfloat16)


def _alt_sign(L):
    t = lax.broadcasted_iota(jnp.int32, (L, 1), 0)
    return (1 - 2 * (t & 1)).astype(jnp.float32)


def _spectrum_kernel(a_ref, b_ref, c_ref, s_ref, kr_ref, ks_ref, kn_ref):
    L = a_ref.shape[1]
    a = a_ref[0]
    row = lax.broadcasted_iota(jnp.int32, (L, 1), 0)
    scale = jnp.where(row == 0, 0.5 / L, 1.0 / L)
    kr_ref[0] = scale * jnp.dot(c_ref[...], a.astype(jnp.bfloat16), preferred_element_type=jnp.float32)
    ks_ref[0] = scale * jnp.dot(s_ref[...], b_ref[0].astype(jnp.bfloat16), preferred_element_type=jnp.float32)
    kn_ref[0] = jnp.sum(a * _alt_sign(L), axis=0, keepdims=True) * (0.5 / L)


def filter_spectrum(a, b, cos_t, sin_t, tc):
    n, L, C = a.shape
    blk = pl.BlockSpec((1, L, tc), lambda o, j: (o, 0, j))
    tab = pl.BlockSpec((L, L), lambda o, j: (0, 0))
    return pl.pallas_call(
        _spectrum_kernel,
        grid=(n, C // tc),
        in_specs=[blk, blk, tab, tab],
        out_specs=[blk, blk, pl.BlockSpec((1, 1, tc), lambda o, j: (o, 0, j))],
        out_shape=[jax.ShapeDtypeStruct((n, L, C), jnp.float32)] * 2 + [jax.ShapeDtypeStruct((n, 1, C), jnp.float32)],
        compiler_params=pltpu.CompilerParams(
            dimension_semantics=("arbitrary", "arbitrary"), vmem_limit_bytes=VMEM_LIMIT_BYTES),
    )(a, b, cos_t, sin_t)


def _long_conv_kernel(*refs, tf, n_slab):
    z_refs = refs[:n_slab]
    xn_refs = refs[n_slab:2 * n_slab]
    kr_ref, ks_ref, kn_ref, bias_ref, c_ref, s_ref, o_ref, acc_ref, stage_ref = refs[2 * n_slab:]
    H = z_refs[0].shape[1] // 2
    f32 = jnp.float32
    bf = jnp.bfloat16
    sign = _alt_sign(H)

    def phase(slab_refs, p):
        return jnp.concatenate([r[0, pl.ds(p, H, stride=2), :] for r in slab_refs], axis=1)

    z2 = [phase(z_refs, p) for p in range(2)]
    zb = [z.astype(bf) for z in z2]
    z_nyq = [jnp.sum(z * sign, axis=0, keepdims=True) for z in z2]
    filt = ((0, 2), (1, 0))
    for p in range(2):
        nyq = z_nyq[0] * kn_ref[filt[p][0]] + z_nyq[1] * kn_ref[filt[p][1]]
        acc_ref[p] = z2[p] * bias_ref[0] + sign * nyq
    for ft in range(H // tf):
        rows = pl.ds(ft * tf, tf)
        zr = [jnp.dot(c_ref[rows, :], zb[q], preferred_element_type=f32) for q in range(2)]
        zs = [jnp.dot(s_ref[rows, :], zb[q], preferred_element_type=f32) for q in range(2)]
        for p in range(2):
            yr = 0.0
            ys = 0.0
            for q in range(2):
                kr = kr_ref[filt[p][q], rows, :]
                ks = ks_ref[filt[p][q], rows, :]
                yr = yr + zr[q] * kr - zs[q] * ks
                ys = ys + zr[q] * ks + zs[q] * kr
            acc_ref[p] += (jnp.dot(c_ref[:, rows], yr.astype(bf), preferred_element_type=f32)
                           + jnp.dot(s_ref[:, rows], ys.astype(bf), preferred_element_type=f32))
    for p in range(2):
        out_p = phase(xn_refs, p) * acc_ref[p]
        for sl in range(n_slab):
            stage_ref[sl, pl.ds(p, H, stride=2), :] = out_p[:, sl * LANES:(sl + 1) * LANES]
    for sl in range(n_slab):
        o_ref[0, :, sl * LANES:(sl + 1) * LANES] = stage_ref[sl]


def long_conv_gate(z_arr, z_col, xn_arr, xn_col, kr, ks, kn, bias, cos_t, sin_t, tc, tf):
    bsz, L, _ = z_arr.shape
    H = L // 2
    C = kr.shape[-1]
    nj = C // tc
    n_slab = tc // LANES
    tab = pl.BlockSpec((H, H), lambda j, b: (0, 0), pipeline_mode=pl.Buffered(1))
    spec = pl.BlockSpec((3, H, tc), lambda j, b: (0, 0, j), pipeline_mode=pl.Buffered(1))
    nyq = pl.BlockSpec((3, 1, tc), lambda j, b: (0, 0, j))
    vec = pl.BlockSpec((1, 1, tc), lambda j, b: (0, 0, j))

    def slabs(col):
        return [pl.BlockSpec((1, L, LANES),
                             functools.partial(lambda j, b, sl: (b, 0, (col * nj + j) * n_slab + sl), sl=sl))
                for sl in range(n_slab)]

    return pl.pallas_call(
        functools.partial(_long_conv_kernel, tf=tf, n_slab=n_slab),
        grid=(nj, bsz),
        in_specs=slabs(z_col) + slabs(xn_col) + [spec, spec, nyq, vec, tab, tab],
        out_specs=pl.BlockSpec((1, L, tc), lambda j, b: (b, 0, j)),
        out_shape=jax.ShapeDtypeStruct((bsz, L, C), jnp.float32),
        scratch_shapes=[pltpu.VMEM((2, H, tc), jnp.float32), pltpu.VMEM((n_slab, L, LANES), jnp.float32)],
        compiler_params=pltpu.CompilerParams(
            dimension_semantics=("arbitrary", "arbitrary"), vmem_limit_bytes=VMEM_LIMIT_BYTES),
    )(*([z_arr] * n_slab), *([xn_arr] * n_slab), kr, ks, kn, bias, cos_t, sin_t)


def _polyphase_taps(k_f, k_b):
    zero = jnp.zeros_like(k_f[:1])
    fe, fo = k_f[0::2], k_f[1::2]
    be, bo = k_b[0::2], k_b[1::2]
    plus = jnp.stack([fe, fo, jnp.concatenate([k_b[1:2], fo[:-1]], axis=0)])
    minus = jnp.stack([jnp.concatenate([zero, be[1:]], axis=0),
                       jnp.concatenate([zero, bo[:-1]], axis=0),
                       jnp.concatenate([zero, bo[1:]], axis=0)])
    return plus, minus


def hyena_long_convs(u, k, h_bias, tc, tf):
    L = u.shape[1]
    cos_t, sin_t = dft_tables(L // 2)
    z = u
    for o in range(k.shape[1]):
        plus, minus = _polyphase_taps(k[:, o, 0], k[:, o, 1])
        kr, ks, kn = filter_spectrum(plus + minus, plus - minus, cos_t, sin_t, tc)
        z = long_conv_gate(z, 0, u, o + 1, kr, ks, kn, h_bias[o][None, None, :], cos_t, sin_t, tc, tf)
    return z


def _ssd_kernel(xf_ref, df_ref, xb_ref, db_ref, dskip_ref, y_ref, h_ref, *, n_ctx_chunks):
    s = pl.program_id(1)
    n_steps = pl.num_programs(1)
    Q, G, R, P, N = SSD_CHUNK, SSD_GROUPS, SSD_HPG, SSD_HEAD_DIM, SSD_STATE
    GP = R * P
    bf = jnp.bfloat16

    @pl.when(s == 0)
    def _():
        h_ref[...] = jnp.zeros_like(h_ref)
        y_ref[...] = jnp.zeros_like(y_ref)

    row = lax.broadcasted_iota(jnp.int32, (Q, Q), 0)
    col = lax.broadcasted_iota(jnp.int32, (Q, Q), 1)
    lane_head = lax.broadcasted_iota(jnp.int32, (Q, GP), 1) // P
    is_latent = s >= n_ctx_chunks
    n_lat = n_steps - n_ctx_chunks
    out_chunk = (jnp.clip(s - n_ctx_chunks, 0, n_lat - 1), jnp.clip(n_steps - 1 - s, 0, n_lat - 1))

    for d, (x_ref, da_ref) in enumerate(((xf_ref, df_ref), (xb_ref, db_ref))):
        mask = (row >= col) if d == 0 else (col >= row)
        tri = mask.astype(jnp.float32)
        da = da_ref[0]
        cum = jnp.dot(tri, da, precision=lax.Precision.HIGHEST, preferred_element_type=jnp.float32)
        cum_t = cum.T
        edge = Q - 1 if d == 0 else 0
        blk = x_ref.at[0]
        for g in range(G):
            xg = blk[:, g * GP:(g + 1) * GP]
            bg = blk[:, D_SSD + g * N:D_SSD + (g + 1) * N].astype(bf)
            cg = blk[:, D_SSD + G * N + g * N:D_SSD + G * N + (g + 1) * N].astype(bf)
            heads = [d * SSD_HEADS + g * R + r for r in range(R)]
            dtm = jnp.zeros((Q, GP), jnp.float32)
            cumm = jnp.zeros((Q, GP), jnp.float32)
            totm = jnp.zeros((Q, GP), jnp.float32)
            for r, h in enumerate(heads):
                sel = lane_head == r
                dtm = jnp.where(sel, da[:, h:h + 1], dtm)
                cumm = jnp.where(sel, cum[:, SSD_HEADS * 2 + h:SSD_HEADS * 2 + h + 1], cumm)
                totm = jnp.where(sel, cum[edge:edge + 1, SSD_HEADS * 2 + h:SSD_HEADS * 2 + h + 1], totm)
            xdt = xg * dtm
            hg = h_ref[d, g * GP:(g + 1) * GP, :]

            gmat = lax.dot_general(cg, bg, (((1,), (1,)), ((), ())), preferred_element_type=jnp.float32)
            y_off = lax.dot_general(cg, hg.astype(bf), (((1,), (1,)), ((), ())),
                                    preferred_element_type=jnp.float32) * jnp.exp(cumm)
            if d == 0:
                y_off = y_off + dskip_ref[:, g * GP:(g + 1) * GP] * xg
            parts = []
            for r, h in enumerate(heads):
                a_col = cum[:, SSD_HEADS * 2 + h:SSD_HEADS * 2 + h + 1]
                a_row = cum_t[SSD_HEADS * 2 + h:SSD_HEADS * 2 + h + 1, :]
                decay = jnp.exp(jnp.where(mask, a_col - a_row, NEG_BIG))
                parts.append(jnp.dot((gmat * decay).astype(bf), xdt[:, r * P:(r + 1) * P].astype(bf),
                                     preferred_element_type=jnp.float32))
            y = jnp.where(is_latent, y_off + jnp.concatenate(parts, axis=-1), 0.0)
            rows = pl.ds(pl.multiple_of(out_chunk[d] * Q, Q), Q)
            y_ref[0, rows, g * GP:(g + 1) * GP] += y

            xw = (xdt * jnp.exp(totm - cumm)).astype(bf)
            st = lax.dot_general(xw, bg, (((0,), (0,)), ((), ())), preferred_element_type=jnp.float32)
            for r, h in enumerate(heads):
                dec = jnp.exp(cum_t[SSD_HEADS * 2 + h:SSD_HEADS * 2 + h + 1, edge:edge + 1])
                rs = slice(g * GP + r * P, g * GP + (r + 1) * P)
                h_ref[d, rs, :] = h_ref[d, rs, :] * dec + st[r * P:(r + 1) * P, :]


def ssd_scan_bidir(xbc, dta, d_skip, n_ctx):
    bsz, lt, width = xbc.shape
    Q = SSD_CHUNK
    n_steps = lt // Q
    n_ctx_chunks = n_ctx // Q
    L = lt - n_ctx

    def bwd_chunk(s):
        return jnp.where(s < n_ctx_chunks, n_ctx_chunks - 1 - s, n_steps - 1 - s + n_ctx_chunks)

    return pl.pallas_call(
        functools.partial(_ssd_kernel, n_ctx_chunks=n_ctx_chunks),
        grid=(bsz, n_steps),
        in_specs=[
            pl.BlockSpec((1, Q, width), lambda b, s: (b, s, 0)),
            pl.BlockSpec((1, Q, LANES), lambda b, s: (b, s, 0)),
            pl.BlockSpec((1, Q, width), lambda b, s: (b, bwd_chunk(s), 0)),
            pl.BlockSpec((1, Q, LANES), lambda b, s: (b, bwd_chunk(s), 0)),
            pl.BlockSpec((1, D_SSD), lambda b, s: (0, 0)),
        ],
        out_specs=pl.BlockSpec((1, L, D_SSD), lambda b, s: (b, 0, 0)),
        out_shape=jax.ShapeDtypeStruct((bsz, L, D_SSD), jnp.float32),
        scratch_shapes=[pltpu.VMEM((2, SSD_GROUPS * SSD_HPG * SSD_HEAD_DIM, SSD_STATE), jnp.float32)],
        compiler_params=pltpu.CompilerParams(
            dimension_semantics=("arbitrary", "arbitrary"), vmem_limit_bytes=VMEM_LIMIT_BYTES),
    )(xbc, dta, xbc, dta, d_skip)


def _hyena_filters(L, f_w1, f_b1, f_freq, f_w2, f_b2, f_w3):
    f32 = jnp.float32
    hp = lax.Precision.HIGHEST
    pos = jnp.arange(L, dtype=f32)
    t = pos / max(L - 1, 1)
    bands = jnp.linspace(1e-4, HYENA_BANDS - 1, HYENA_BANDS, dtype=f32)
    ang = (2 * math.pi / L) * pos[:, None] * bands[None, :]
    feats = jnp.concatenate([t[:, None], jnp.cos(ang), -jnp.sin(ang)], axis=-1)
    h = jnp.sin(f_freq * (jnp.dot(feats, f_w1, precision=hp) + f_b1))
    h = jnp.sin(f_freq * (jnp.dot(h, f_w2, precision=hp) + f_b2))
    k = jnp.dot(h, f_w3, precision=hp).reshape(L, HYENA_ORDER, 2, D_HYENA)
    deltas = jnp.abs(jnp.linspace(math.log(HYENA_TARGET) / HYENA_SLOW_DECAY,
                                  math.log(HYENA_TARGET) / HYENA_FAST_DECAY, D_HYENA, dtype=f32))
    window = jnp.exp(-t[:, None] * deltas[None, :])
    return k * window[:, None, None, :]


def kernel(x, c, ctx, c_ctx, w_ada, b_ada, g_norm1, g_norm2, w_in, hy_conv_w, hy_conv_b, hy_f_w1, hy_f_b1, hy_f_freq, hy_f_w2, hy_f_b2, hy_f_w3, hy_bias, ssd_conv_w, ssd_conv_b, ssd_a_log, ssd_dt_bias, ssd_d, ssd_norm_g, w_out, w_group, b_group, w_expert, b_expert, w1, w3, w2, g_final):
    bsz, seq_len, _ = x.shape
    l = 0
    hp = lax.Precision.HIGHEST
    mod =(jnp.dot(jax.nn.silu(c), w_ada[l], precision=hp) + b_ada[l])[:, None, :]
    sh1, sc1, ga1, sh2, sc2, ga2 = jnp.split(mod, 6, axis=-1)
    cmod = jnp.dot(jax.nn.silu(c_ctx), w_ada[l], precision=hp) + b_ada[l]
    csh1, csc1, _, _, _, _ = jnp.split(cmod, 6, axis=-1)

    w_out_bf = w_out[l].astype(jnp.bfloat16)
    w_in_bf, w_dt_bf, dt_bias2, dt_mult = in_proj_params(w_in[l], ssd_a_log[l], ssd_dt_bias[l],
                                                         HY_COLS, D_SSD, D_XBC)

    u, z, xbc, dta = in_proj_fused(ctx, x, g_norm1[l][None, :], csh1[None, :], csc1[None, :], sh1, sc1,
                                   w_in_bf, w_dt_bf, hy_conv_w[l], hy_conv_b[l][None, :],
                                   ssd_conv_w[l], ssd_conv_b[l][None, :], dt_bias2, dt_mult, GRID_W, 256, 512)
    k = _hyena_filters(seq_len, hy_f_w1[l], hy_f_b1[l], hy_f_freq[l], hy_f_w2[l], hy_f_b2[l], hy_f_w3[l])
    y_hy = hyena_long_convs(u, k, hy_bias[l], 256, 512)
    y_scan = ssd_scan_bidir(xbc, dta, jnp.repeat(ssd_d[l], SSD_HEAD_DIM)[None, :], CTX_LEN)

    pad = LANES - MOE_GROUPS - N_EXPERTS
    w_router = jnp.concatenate([w_group[l], w_expert[l], jnp.zeros((D_MODEL, pad), jnp.float32)], axis=1)
    w_router_hi = w_router.astype(jnp.bfloat16)
    w_router_lo = (w_router - w_router_hi.astype(jnp.float32)).astype(jnp.bfloat16)
    w_router = jnp.stack([w_router_hi, w_router_lo])
    b_router = jnp.concatenate([b_group[l], b_expert[l], jnp.zeros((pad,), jnp.float32)])[None, :]
    x1, hn, route_i, route_f, counts = out_proj_router(
        y_hy, y_scan, z, 0, x, ga1, sc2, sh2, ssd_norm_g[l][None, :], g_norm2[l][None, :],
        w_out_bf, w_router, b_router, 256)
    n_tok = bsz * seq_len
    n_blocks = -(-n_tok * TOP_K // MOE_BLOCK) + N_EXPERTS
    dest, block_eid, block_first, block_valid = moe_plan(route_i, counts, MOE_BLOCK, n_blocks)
    buf = moe_dispatch(hn.reshape(n_tok, D_MODEL), dest, block_valid, MOE_BLOCK, 256)
    yb = expert_blocks(buf, block_eid, block_first, block_valid, w1[l], w3[l], w2[l], MOE_BLOCK)
    return moe_combine(x1, route_f, ga2, g_final[None, :], yb, dest, 256)
```

```python
import functools
import math

import jax
import jax.numpy as jnp
from jax import lax
from jax.experimental import pallas as pl
from jax.experimental.pallas import tpu as pltpu

D_MODEL = 1024
CTX_LEN = 256
GRID_W = 64
EPS = 1e-6
SHORT_CONV = 3

D_HYENA = D_MODEL // 2
HYENA_ORDER = 2
HYENA_BANDS = 8
HYENA_FAST_DECAY = 0.3
HYENA_SLOW_DECAY = 1.5
HYENA_TARGET = 1e-2

D_SSD = D_MODEL // 2
SSD_HEAD_DIM = 64
SSD_HEADS = D_SSD // SSD_HEAD_DIM
SSD_GROUPS = 2
SSD_HPG = SSD_HEADS // SSD_GROUPS
SSD_STATE = 128
SSD_CHUNK = 128

D_XBC = D_SSD + 2 * SSD_GROUPS * SSD_STATE
HY_COLS = (HYENA_ORDER + 1) * D_HYENA
D_IN = HY_COLS + D_SSD + D_XBC + 2 * SSD_HEADS
LANES = 128
SUBLANES = 8
D_IN_PAD = -(-D_IN // LANES) * LANES

MOE_GROUPS = 8
EXPERTS_PER_GROUP = 8
N_EXPERTS = MOE_GROUPS * EXPERTS_PER_GROUP
TOP_K = 2
D_EXPERT = 512
MOE_BLOCK = 256
ROUTE_COLS = 8

VMEM_LIMIT_BYTES = 56 * 1024 * 1024
NEG_BIG = -1e30


def _conv3_rows(p, w_ref, b_ref, cols, has_prev, has_next):
    n = p.shape[0]
    prev = jnp.where(has_prev, pltpu.roll(p, 1, 0), 0.0)
    nxt = jnp.where(has_next, pltpu.roll(p, n - 1, 0), 0.0)
    return b_ref[:, cols] + w_ref[0:1, cols] * prev + w_ref[1:2, cols] * p + w_ref[2:3, cols] * nxt


def _in_proj_kernel(ctx_ref, x_ref, g_ref, csh_ref, csc_ref, sh_ref, sc_ref, w_ref, wdt_ref, hw_ref, hb_ref,
                    sw_ref, sb_ref, dtb_ref, dtm_ref, u_ref, z_ref, xbc_ref, dta_ref, h_ref,
                    *, n_ctx_steps, row_len, ctx_row_len, hy_cols, d_ssd, d_xbc, tn):
    i = pl.program_id(1)
    is_ctx = i < n_ctx_steps
    tm = x_ref.shape[1]
    xin = jnp.where(is_ctx, ctx_ref[0], x_ref[0])
    shift = jnp.where(is_ctx, csh_ref[...], sh_ref[0])
    scale = jnp.where(is_ctx, csc_ref[...], sc_ref[0])
    y = xin * lax.rsqrt(jnp.mean(xin * xin, axis=-1, keepdims=True) + EPS) * g_ref[...]
    h_ref[...] = (y * (1.0 + scale) + shift).astype(jnp.bfloat16)

    pos = lax.broadcasted_iota(jnp.int32, (tm, 1), 0) + jnp.where(is_ctx, i, i - n_ctx_steps) * tm
    in_row = jnp.where(is_ctx, pos % ctx_row_len, pos % row_len)
    has_prev = in_row != 0
    has_next = in_row != jnp.where(is_ctx, ctx_row_len - 1, row_len - 1)

    @pl.when(jnp.logical_not(is_ctx))
    def _():
        for c0 in range(0, hy_cols, tn):
            cols = slice(c0, c0 + tn)
            p = jnp.dot(h_ref[...], w_ref[:, cols], preferred_element_type=jnp.float32)
            u_ref[0, :, cols] = _conv3_rows(p, hw_ref, hb_ref, cols, has_prev, has_next)
        z_ref[0] = jnp.dot(h_ref[...], w_ref[:, hy_cols:hy_cols + d_ssd], preferred_element_type=jnp.float32)

    for c0 in range(0, d_xbc, tn):
        cols = slice(c0, c0 + tn)
        wc = slice(hy_cols + d_ssd + c0, hy_cols + d_ssd + c0 + tn)
        p = jnp.dot(h_ref[...], w_ref[:, wc], preferred_element_type=jnp.float32)
        v = _conv3_rows(p, sw_ref, sb_ref, cols, has_prev, has_next)
        xbc_ref[0, :, cols] = v * jax.nn.sigmoid(v)
    pd = jnp.dot(h_ref[...], wdt_ref[...], preferred_element_type=jnp.float32) + dtb_ref[...]
    sp = jnp.maximum(pd, 0.0) + jnp.log(1.0 + jnp.exp(-jnp.abs(pd)))
    dta_ref[0] = sp * dtm_ref[...]


def in_proj_fused(ctx, x, g1, csh, csc, sh, sc, w_bf, wdt_bf, hy_w, hy_b, ssd_w, ssd_b, dt_bias2, dt_mult,
                  row_len, tm, tn):
    bsz, L, D = x.shape
    lc = ctx.shape[1]
    hy_cols = hy_w.shape[1]
    d_xbc = ssd_w.shape[1]
    d_ssd = w_bf.shape[1] - hy_cols - d_xbc
    n_ctx_steps = lc // tm
    n_steps = n_ctx_steps + L // tm
    lat = lambda b, i: (b, jnp.maximum(i - n_ctx_steps, 0), 0)
    allt = lambda b, i: (b, i, 0)
    const2 = lambda b, i: (0, 0)
    per_b = pl.BlockSpec((1, 1, D), lambda b, i: (b, 0, 0))
    kern = functools.partial(_in_proj_kernel, n_ctx_steps=n_ctx_steps, row_len=row_len, ctx_row_len=lc,
                             hy_cols=hy_cols, d_ssd=d_ssd, d_xbc=d_xbc, tn=tn)
    return pl.pallas_call(
        kern,
        grid=(bsz, n_steps),
        in_specs=[
            pl.BlockSpec((1, tm, D), lambda b, i: (b, jnp.minimum(i, n_ctx_steps - 1), 0)),
            pl.BlockSpec((1, tm, D), lat),
            pl.BlockSpec((1, D), const2),
            pl.BlockSpec((1, D), const2),
            pl.BlockSpec((1, D), const2),
            per_b, per_b,
            pl.BlockSpec(w_bf.shape, const2),
            pl.BlockSpec(wdt_bf.shape, const2),
            pl.BlockSpec(hy_w.shape, const2),
            pl.BlockSpec(hy_b.shape, const2),
            pl.BlockSpec(ssd_w.shape, const2),
            pl.BlockSpec(ssd_b.shape, const2),
            pl.BlockSpec((1, LANES), const2),
            pl.BlockSpec((1, LANES), const2),
        ],
        out_specs=[
            pl.BlockSpec((1, tm, hy_cols), lat),
            pl.BlockSpec((1, tm, d_ssd), lat),
            pl.BlockSpec((1, tm, d_xbc), allt),
            pl.BlockSpec((1, tm, LANES), allt),
        ],
        out_shape=[
            jax.ShapeDtypeStruct((bsz, L, hy_cols), jnp.float32),
            jax.ShapeDtypeStruct((bsz, L, d_ssd), jnp.float32),
            jax.ShapeDtypeStruct((bsz, lc + L, d_xbc), jnp.float32),
            jax.ShapeDtypeStruct((bsz, lc + L, LANES), jnp.float32),
        ],
        scratch_shapes=[pltpu.VMEM((tm, D), jnp.bfloat16)],
        compiler_params=pltpu.CompilerParams(
            dimension_semantics=("arbitrary", "arbitrary"), vmem_limit_bytes=VMEM_LIMIT_BYTES),
    )(ctx, x, g1, csh, csc, sh, sc, w_bf, wdt_bf, hy_w, hy_b, ssd_w, ssd_b, dt_bias2, dt_mult)


def in_proj_params(w_in, a_log, dt_bias, hy_cols, d_ssd, d_xbc):
    n_h = 2 * SSD_HEADS
    main = hy_cols + d_ssd + d_xbc
    w_dt = w_in[:, main:main + n_h]
    pad = jnp.zeros((w_in.shape[0], LANES - 2 * n_h), w_in.dtype)
    wdt = jnp.concatenate([w_dt, w_dt, pad], axis=1).astype(jnp.bfloat16)
    zpad = jnp.zeros((LANES - 2 * n_h,), jnp.float32)
    bias2 = jnp.concatenate([dt_bias.reshape(n_h), dt_bias.reshape(n_h), zpad])[None, :]
    mult = jnp.concatenate([jnp.ones((n_h,), jnp.float32), -jnp.exp(a_log).reshape(n_h), zpad])[None, :]
    return w_in[:, :main].astype(jnp.bfloat16), wdt, bias2, mult


def _out_router_kernel(yh_ref, ys_ref, z_ref, x_ref, ga_ref, sc_ref, sh_ref, ng_ref, g2_ref, wo_ref, wr_ref, br_ref,
                       x1_ref, hn_ref, ri_ref, rf_ref, cnt_ref, carry_ref):
    first = jnp.logical_and(pl.program_id(0) == 0, pl.program_id(1) == 0)

    @pl.when(first)
    def _():
        carry_ref[...] = jnp.zeros_like(carry_ref)

    bf = jnp.bfloat16
    tm = x_ref.shape[1]
    dh = yh_ref.shape[2]
    z = z_ref[0]
    ys = ys_ref[0] * (z * jax.nn.sigmoid(z))
    gw = ys.shape[1] // SSD_GROUPS
    acc = jnp.dot(yh_ref[0].astype(bf), wo_ref[0:dh, :], preferred_element_type=jnp.float32)
    for g in range(SSD_GROUPS):
        yg = ys[:, g * gw:(g + 1) * gw]
        yg = yg * lax.rsqrt(jnp.mean(yg * yg, axis=-1, keepdims=True) + EPS) * ng_ref[:, g * gw:(g + 1) * gw]
        acc += jnp.dot(yg.astype(bf), wo_ref[dh + g * gw:dh + (g + 1) * gw, :], preferred_element_type=jnp.float32)
    x1 = x_ref[0] + ga_ref[0] * acc
    x1_ref[0] = x1
    hn = x1 * lax.rsqrt(jnp.mean(x1 * x1, axis=-1, keepdims=True) + EPS) * g2_ref[...]
    hn = hn * (1.0 + sc_ref[0]) + sh_ref[0]
    hn_ref[0] = hn

    hn_hi = hn.astype(bf)
    hn_lo = (hn - hn_hi.astype(jnp.float32)).astype(bf)
    logits = (jnp.dot(hn_hi, wr_ref[0], preferred_element_type=jnp.float32)
              + jnp.dot(hn_lo, wr_ref[0], preferred_element_type=jnp.float32)
              + jnp.dot(hn_hi, wr_ref[1], preferred_element_type=jnp.float32)) + br_ref[...]
    lane = lax.broadcasted_iota(jnp.int32, (tm, LANES), 1)
    lane_f = lane.astype(jnp.float32)
    ninf = jnp.float32(-jnp.inf)
    big = jnp.float32(1e9)
    gl = jnp.where(lane < MOE_GROUPS, logits, ninf)
    gmax = jnp.max(gl, axis=-1, keepdims=True)
    p_group = 1.0 / jnp.sum(jnp.exp(gl - gmax), axis=-1, keepdims=True)
    g_sel = jnp.min(jnp.where(gl == gmax, lane_f, big), axis=-1, keepdims=True)
    e_lane = lane - MOE_GROUPS
    in_grp = jnp.logical_and(e_lane >= 0, (e_lane // EXPERTS_PER_GROUP).astype(jnp.float32) == g_sel)
    el = jnp.where(in_grp, logits, ninf)
    m1 = jnp.max(el, axis=-1, keepdims=True)
    i1 = jnp.min(jnp.where(el == m1, lane_f, big), axis=-1, keepdims=True)
    el2 = jnp.where(lane_f == i1, ninf, el)
    m2 = jnp.max(el2, axis=-1, keepdims=True)
    i2 = jnp.min(jnp.where(el2 == m2, lane_f, big), axis=-1, keepdims=True)
    t = jnp.exp(m2 - m1)
    w1 = 1.0 / (1.0 + t)
    gate1 = w1 * p_group
    gate2 = (t * w1) * p_group
    e1 = i1 - MOE_GROUPS
    e2 = i2 - MOE_GROUPS
    el_f = e_lane.astype(jnp.float32)
    oh1 = el_f == e1
    oh2 = el_f == e2
    oh = jnp.logical_or(oh1, oh2).astype(bf)
    r_i = lax.broadcasted_iota(jnp.int32, (tm, tm), 0)
    c_i = lax.broadcasted_iota(jnp.int32, (tm, tm), 1)
    before = jnp.dot((c_i < r_i).astype(bf), oh, preferred_element_type=jnp.float32) + carry_ref[...]
    rank1 = jnp.sum(jnp.where(oh1, before, 0.0), axis=-1, keepdims=True)
    rank2 = jnp.sum(jnp.where(oh2, before, 0.0), axis=-1, keepdims=True)
    carry_ref[...] += jnp.sum(oh.astype(jnp.float32), axis=0, keepdims=True)
    cnt_ref[...] = carry_ref[...]

    rec = jnp.where(lane == 0, e1, jnp.where(lane == 1, e2, jnp.where(lane == 2, rank1,
                                                                      jnp.where(lane == 3, rank2, 0.0))))
    ri_ref[0] = rec.T[0:ROUTE_COLS, :].astype(jnp.int32)
    col = lax.broadcasted_iota(jnp.int32, (tm, ROUTE_COLS), 1)
    rf_ref[0] = jnp.where(col == 0, gate1, gate2)


def out_proj_router(y_hy, y_scan, px, z_col, x, ga1, sc2, sh2, norm_g, g2, w_out_bf, w_router, b_router, tm):
    bsz, L, D = x.shape
    dh = y_hy.shape[-1]
    ds = y_scan.shape[-1]
    tok = lambda b, i: (b, i, 0)
    per_b = pl.BlockSpec((1, 1, D), lambda b, i: (b, 0, 0))
    const2 = lambda b, i: (0, 0)
    return pl.pallas_call(
        _out_router_kernel,
        grid=(bsz, L // tm),
        in_specs=[
            pl.BlockSpec((1, tm, dh), tok),
            pl.BlockSpec((1, tm, ds), tok),
            pl.BlockSpec((1, tm, ds), lambda b, i: (b, i, z_col)),
            pl.BlockSpec((1, tm, D), tok),
            per_b, per_b, per_b,
            pl.BlockSpec((1, ds), const2),
            pl.BlockSpec((1, D), const2),
            pl.BlockSpec((dh + ds, D), const2),
            pl.BlockSpec((2, D, LANES), lambda b, i: (0, 0, 0)),
            pl.BlockSpec((1, LANES), const2),
        ],
        out_specs=[
            pl.BlockSpec((1, tm, D), tok),
            pl.BlockSpec((1, tm, D), tok),
            pl.BlockSpec((1, ROUTE_COLS, tm), lambda b, i: (b, 0, i)),
            pl.BlockSpec((1, tm, ROUTE_COLS), tok),
            pl.BlockSpec((1, LANES), const2),
        ],
        out_shape=[
            jax.ShapeDtypeStruct((bsz, L, D), jnp.float32),
            jax.ShapeDtypeStruct((bsz, L, D), jnp.float32),
            jax.ShapeDtypeStruct((bsz, ROUTE_COLS, L), jnp.int32),
            jax.ShapeDtypeStruct((bsz, L, ROUTE_COLS), jnp.float32),
            jax.ShapeDtypeStruct((1, LANES), jnp.float32),
        ],
        scratch_shapes=[pltpu.VMEM((1, LANES), jnp.float32)],
        compiler_params=pltpu.CompilerParams(
            dimension_semantics=("arbitrary", "arbitrary"), vmem_limit_bytes=VMEM_LIMIT_BYTES),
    )(y_hy, y_scan, px, x, ga1, sc2, sh2, norm_g, g2, w_out_bf, w_router, b_router)


def _row_copy(src_hbm, src_row, dst_ref, dst_row, sem):
    return pltpu.make_async_copy(src_hbm.at[pl.ds(src_row, 1), :], dst_ref.at[pl.ds(dst_row, 1), :], sem)


def _dispatch_kernel(dest_ref, valid_ref, hn_ref, buf_hbm, zeros, sem, zsem):
    step = pl.program_id(0)
    tm = hn_ref.shape[0]
    n_tok = pl.num_programs(0) * tm
    blk = zeros.shape[0]
    n_blocks = buf_hbm.shape[0] // blk

    def zero_copy(i):
        return pltpu.make_async_copy(zeros, buf_hbm.at[pl.ds(pl.multiple_of(i * blk, blk), blk), :], zsem)

    def zfill(i, carry):
        @pl.when(valid_ref[i] < blk)
        def _():
            zero_copy(i).start()
        return carry

    def zwait(i, carry):
        @pl.when(valid_ref[i] < blk)
        def _():
            zero_copy(i).wait()
        return carry

    @pl.when(step == 0)
    def _():
        zeros[...] = jnp.zeros_like(zeros)
        lax.fori_loop(0, n_blocks, zfill, 0)
        lax.fori_loop(0, n_blocks, zwait, 0)

    def body(j, carry):
        t = step * tm + j
        _row_copy(hn_ref, j, buf_hbm, dest_ref[t], sem).start()
        _row_copy(hn_ref, j, buf_hbm, dest_ref[n_tok + t], sem).start()
        return carry

    lax.fori_loop(0, tm, body, 0, unroll=8)
    for _ in range(2):
        pltpu.make_async_copy(hn_ref, buf_hbm.at[pl.ds(0, tm), :], sem).wait()


def moe_dispatch(hn, dest, block_valid, blk, tm):
    T, D = hn.shape
    n_rows = block_valid.shape[0] * blk
    grid_spec = pltpu.PrefetchScalarGridSpec(
        num_scalar_prefetch=2,
        grid=(T // tm,),
        in_specs=[pl.BlockSpec((tm, D), lambda i, d, v: (i, 0))],
        out_specs=pl.BlockSpec(memory_space=pl.ANY),
        scratch_shapes=[pltpu.VMEM((blk, D), jnp.float32), pltpu.SemaphoreType.DMA(()),
                        pltpu.SemaphoreType.DMA(())],
    )
    return pl.pallas_call(
        _dispatch_kernel,
        grid_spec=grid_spec,
        out_shape=jax.ShapeDtypeStruct((n_rows, D), jnp.float32),
        compiler_params=pltpu.CompilerParams(dimension_semantics=("arbitrary",), has_side_effects=True),
    )(dest, block_valid, hn)


def _expert_kernel(eid_ref, first_ref, valid_ref, x_ref, w1_ref, w3_ref, w2_ref, o_ref, w1b, w3b, w2b):
    i = pl.program_id(0)
    del eid_ref
    bf = jnp.bfloat16

    @pl.when(first_ref[i] == 1)
    def _():
        w1b[...] = w1_ref[0].astype(bf)
        w3b[...] = w3_ref[0].astype(bf)
        w2b[...] = w2_ref[0].astype(bf)

    valid = valid_ref[i]

    @pl.when(valid > 0)
    def _():
        xb = x_ref[...].astype(bf)
        a = jnp.dot(xb, w1b[...], preferred_element_type=jnp.float32)
        b = jnp.dot(xb, w3b[...], preferred_element_type=jnp.float32)
        h = (a * jax.nn.sigmoid(a)) * b
        o_ref[...] = jnp.dot(h.astype(bf), w2b[...], preferred_element_type=jnp.float32)

    @pl.when(valid <= 0)
    def _():
        o_ref[...] = jnp.zeros_like(o_ref)


def expert_blocks(buf, block_eid, block_first, block_valid, w1, w3, w2, blk):
    rows, D = buf.shape
    n_blocks = rows // blk
    E, _, F = w1.shape
    grid_spec = pltpu.PrefetchScalarGridSpec(
        num_scalar_prefetch=3,
        grid=(n_blocks,),
        in_specs=[
            pl.BlockSpec((blk, D), lambda i, eid, fi, va: (i, 0)),
            pl.BlockSpec((1, D, F), lambda i, eid, fi, va: (eid[i], 0, 0)),
            pl.BlockSpec((1, D, F), lambda i, eid, fi, va: (eid[i], 0, 0)),
            pl.BlockSpec((1, F, D), lambda i, eid, fi, va: (eid[i], 0, 0)),
        ],
        out_specs=pl.BlockSpec((blk, D), lambda i, eid, fi, va: (i, 0)),
        scratch_shapes=[pltpu.VMEM((D, F), jnp.bfloat16), pltpu.VMEM((D, F), jnp.bfloat16),
                        pltpu.VMEM((F, D), jnp.bfloat16)],
    )
    return pl.pallas_call(
        _expert_kernel,
        grid_spec=grid_spec,
        out_shape=jax.ShapeDtypeStruct((rows, D), jnp.float32),
        compiler_params=pltpu.CompilerParams(
            dimension_semantics=("arbitrary",), vmem_limit_bytes=VMEM_LIMIT_BYTES),
    )(block_eid, block_first, block_valid, buf, w1, w3, w2)


def _combine_kernel(dest_ref, x1_ref, rf_ref, ga_ref, gf_ref, yb_hbm, o_ref, ybuf, sem):
    b = pl.program_id(0)
    i = pl.program_id(1)
    n_i = pl.num_programs(1)
    tm = x1_ref.shape[1]
    step = b * n_i + i
    n_steps = pl.num_programs(0) * n_i
    slot = step % 2

    def issue(step_, slot_):
        def body(j, carry):
            t = step_ * tm + j
            _row_copy(yb_hbm, dest_ref[t], ybuf.at[slot_, 0], j, sem.at[slot_]).start()
            _row_copy(yb_hbm, dest_ref[n_steps * tm + t], ybuf.at[slot_, 1], j, sem.at[slot_]).start()
            return carry
        lax.fori_loop(0, tm, body, 0, unroll=8)

    @pl.when(step == 0)
    def _():
        issue(0, 0)

    @pl.when(step + 1 < n_steps)
    def _():
        issue(step + 1, 1 - slot)

    pltpu.make_async_copy(yb_hbm.at[pl.ds(0, tm), :], ybuf.at[slot, 0], sem.at[slot]).wait()
    pltpu.make_async_copy(yb_hbm.at[pl.ds(0, tm), :], ybuf.at[slot, 1], sem.at[slot]).wait()
    rf = rf_ref[0]
    y = rf[:, 0:1] * ybuf[slot, 0] + rf[:, 1:2] * ybuf[slot, 1]
    x2 = x1_ref[0] + ga_ref[0] * y
    o_ref[0] = x2 * lax.rsqrt(jnp.mean(x2 * x2, axis=-1, keepdims=True) + EPS) * gf_ref[...]


def moe_combine(x1, route_f, ga2, g_final, yb, dest, tm):
    bsz, L, D = x1.shape
    grid_spec = pltpu.PrefetchScalarGridSpec(
        num_scalar_prefetch=1,
        grid=(bsz, L // tm),
        in_specs=[
            pl.BlockSpec((1, tm, D), lambda b, i, d: (b, i, 0)),
            pl.BlockSpec((1, tm, ROUTE_COLS), lambda b, i, d: (b, i, 0)),
            pl.BlockSpec((1, 1, D), lambda b, i, d: (b, 0, 0)),
            pl.BlockSpec((1, D), lambda b, i, d: (0, 0)),
            pl.BlockSpec(memory_space=pl.ANY),
        ],
        out_specs=pl.BlockSpec((1, tm, D), lambda b, i, d: (b, i, 0)),
        scratch_shapes=[pltpu.VMEM((2, 2, tm, D), jnp.float32), pltpu.SemaphoreType.DMA((2,))],
    )
    return pl.pallas_call(
        _combine_kernel,
        grid_spec=grid_spec,
        out_shape=jax.ShapeDtypeStruct((bsz, L, D), jnp.float32),
        compiler_params=pltpu.CompilerParams(
            dimension_semantics=("arbitrary", "arbitrary"), vmem_limit_bytes=VMEM_LIMIT_BYTES),
    )(dest, x1, route_f, ga2, g_final, yb)


def moe_plan(route_i, counts, blk, n_blocks):
    cnt = counts[0, MOE_GROUPS:MOE_GROUPS + N_EXPERTS].astype(jnp.int32)
    padded = (cnt + blk - 1) // blk * blk
    ends = jnp.cumsum(padded)
    starts = ends - padded
    dest = jnp.concatenate([(jnp.take(starts, route_i[:, k]) + route_i[:, 2 + k]).reshape(-1) for k in range(TOP_K)])
    first_row = jnp.arange(n_blocks, dtype=jnp.int32) * blk
    block_eid = jnp.minimum(jnp.sum((ends[None, :] <= first_row[:, None]).astype(jnp.int32), axis=1), N_EXPERTS - 1)
    block_valid = jnp.clip(cnt[block_eid] - (first_row - starts[block_eid]), 0, blk).astype(jnp.int32)
    block_first = jnp.concatenate([jnp.ones((1,), jnp.int32),
                                   (block_eid[1:] != block_eid[:-1]).astype(jnp.int32)])
    return dest, block_eid, block_first, block_valid


def dft_tables(L):
    n = 2 * L
    f = lax.broadcasted_iota(jnp.int32, (L, L), 0)
    t = lax.broadcasted_iota(jnp.int32, (L, L), 1)
    ang = ((f * t) % n).astype(jnp.float32) * (2.0 * math.pi / n)
    return jnp.cos(ang).astype(jnp.bfloat16), jnp.sin(ang).astype(jnp.bfloat16)


def _alt_sign(L):
    t = lax.broadcasted_iota(jnp.int32, (L, 1), 0)
    return (1 - 2 * (t & 1)).astype(jnp.float32)


def _spectrum_kernel(a_ref, b_ref, c_ref, s_ref, kr_ref, ks_ref, kn_ref):
    L = a_ref.shape[1]
    a = a_ref[0]
    row = lax.broadcasted_iota(jnp.int32, (L, 1), 0)
    scale = jnp.where(row == 0, 0.5 / L, 1.0 / L)
    kr_ref[0] = scale * jnp.dot(c_ref[...], a.astype(jnp.bfloat16), preferred_element_type=jnp.float32)
    ks_ref[0] = scale * jnp.dot(s_ref[...], b_ref[0].astype(jnp.bfloat16), preferred_element_type=jnp.float32)
    kn_ref[0] = jnp.sum(a * _alt_sign(L), axis=0, keepdims=True) * (0.5 / L)


def filter_spectrum(a, b, cos_t, sin_t, tc):
    n, L, C = a.shape
    blk = pl.BlockSpec((1, L, tc), lambda o, j: (o, 0, j))
    tab = pl.BlockSpec((L, L), lambda o, j: (0, 0))
    return pl.pallas_call(
        _spectrum_kernel,
        grid=(n, C // tc),
        in_specs=[blk, blk, tab, tab],
        out_specs=[blk, blk, pl.BlockSpec((1, 1, tc), lambda o, j: (o, 0, j))],
        out_shape=[jax.ShapeDtypeStruct((n, L, C), jnp.float32)] * 2 + [jax.ShapeDtypeStruct((n, 1, C), jnp.float32)],
        compiler_params=pltpu.CompilerParams(
            dimension_semantics=("arbitrary", "arbitrary"), vmem_limit_bytes=VMEM_LIMIT_BYTES),
    )(a, b, cos_t, sin_t)


def _long_conv_kernel(*refs, tf, n_slab):
    z_refs = refs[:n_slab]
    xn_refs = refs[n_slab:2 * n_slab]
    kr_ref, ks_ref, kn_ref, bias_ref, c_ref, s_ref, o_ref, acc_ref, stage_ref = refs[2 * n_slab:]
    H = z_refs[0].shape[1] // 2
    f32 = jnp.float32
    bf = jnp.bfloat16
    sign = _alt_sign(H)

    def phase(slab_refs, p):
        return jnp.concatenate([r[0, pl.ds(p, H, stride=2), :] for r in slab_refs], axis=1)

    z2 = [phase(z_refs, p) for p in range(2)]
    zb = [z.astype(bf) for z in z2]
    z_nyq = [jnp.sum(z * sign, axis=0, keepdims=True) for z in z2]
    filt = ((0, 2), (1, 0))
    for p in range(2):
        nyq = z_nyq[0] * kn_ref[filt[p][0]] + z_nyq[1] * kn_ref[filt[p][1]]
        acc_ref[p] = z2[p] * bias_ref[0] + sign * nyq
    for ft in range(H // tf):
        rows = pl.ds(ft * tf, tf)
        zr = [jnp.dot(c_ref[rows, :], zb[q], preferred_element_type=f32) for q in range(2)]
        zs = [jnp.dot(s_ref[rows, :], zb[q], preferred_element_type=f32) for q in range(2)]
        for p in range(2):
            yr = 0.0
            ys = 0.0
            for q in range(2):
                kr = kr_ref[filt[p][q], rows, :]
                ks = ks_ref[filt[p][q], rows, :]
                yr = yr + zr[q] * kr - zs[q] * ks
                ys = ys + zr[q] * ks + zs[q] * kr
            acc_ref[p] += (jnp.dot(c_ref[:, rows], yr.astype(bf), preferred_element_type=f32)
                           + jnp.dot(s_ref[:, rows], ys.astype(bf), preferred_element_type=f32))
    for p in range(2):
        out_p = phase(xn_refs, p) * acc_ref[p]
        for sl in range(n_slab):
            stage_ref[sl, pl.ds(p, H, stride=2), :] = out_p[:, sl * LANES:(sl + 1) * LANES]
    for sl in range(n_slab):
        o_ref[0, :, sl * LANES:(sl + 1) * LANES] = stage_ref[sl]


def long_conv_gate(z_arr, z_col, xn_arr, xn_col, kr, ks, kn, bias, cos_t, sin_t, tc, tf):
    bsz, L, _ = z_arr.shape
    H = L // 2
    C = kr.shape[-1]
    nj = C // tc
    n_slab = tc // LANES
    tab = pl.BlockSpec((H, H), lambda j, b: (0, 0), pipeline_mode=pl.Buffered(1))
    spec = pl.BlockSpec((3, H, tc), lambda j, b: (0, 0, j), pipeline_mode=pl.Buffered(1))
    nyq = pl.BlockSpec((3, 1, tc), lambda j, b: (0, 0, j))
    vec = pl.BlockSpec((1, 1, tc), lambda j, b: (0, 0, j))

    def slabs(col):
        return [pl.BlockSpec((1, L, LANES),
                             functools.partial(lambda j, b, sl: (b, 0, (col * nj + j) * n_slab + sl), sl=sl))
                for sl in range(n_slab)]

    return pl.pallas_call(
        functools.partial(_long_conv_kernel, tf=tf, n_slab=n_slab),
        grid=(nj, bsz),
        in_specs=slabs(z_col) + slabs(xn_col) + [spec, spec, nyq, vec, tab, tab],
        out_specs=pl.BlockSpec((1, L, tc), lambda j, b: (b, 0, j)),
        out_shape=jax.ShapeDtypeStruct((bsz, L, C), jnp.float32),
        scratch_shapes=[pltpu.VMEM((2, H, tc), jnp.float32), pltpu.VMEM((n_slab, L, LANES), jnp.float32)],
        compiler_params=pltpu.CompilerParams(
            dimension_semantics=("arbitrary", "arbitrary"), vmem_limit_bytes=VMEM_LIMIT_BYTES),
    )(*([z_arr] * n_slab), *([xn_arr] * n_slab), kr, ks, kn, bias, cos_t, sin_t)


def _polyphase_taps(fe, fo, be, bo):
    zero = jnp.zeros_like(fe[:1])
    plus = jnp.stack([fe, fo, jnp.concatenate([bo[0:1], fo[:-1]], axis=0)])
    minus = jnp.stack([jnp.concatenate([zero, be[1:]], axis=0),
                       jnp.concatenate([zero, bo[:-1]], axis=0),
                       jnp.concatenate([zero, bo[1:]], axis=0)])
    return plus, minus


def hyena_long_convs(u, kp, h_bias, tc, tf):
    L = u.shape[1]
    cos_t, sin_t = dft_tables(L // 2)
    z = u
    for o in range(kp.shape[2]):
        plus, minus = _polyphase_taps(kp[0, :, o, 0], kp[1, :, o, 0], kp[0, :, o, 1], kp[1, :, o, 1])
        kr, ks, kn = filter_spectrum(plus + minus, plus - minus, cos_t, sin_t, tc)
        z = long_conv_gate(z, 0, u, o + 1, kr, ks, kn, h_bias[o][None, None, :], cos_t, sin_t, tc, tf)
    return z


def _filter_kernel(band_ref, w1_ref, b1_ref, fr_ref, w2_ref, b2_ref, w3_ref, dl_ref, k_ref, *, seq_len):
    hp = lax.Precision.HIGHEST
    f32 = jnp.float32
    tp = k_ref.shape[0]
    half = seq_len // 2
    g = lax.broadcasted_iota(jnp.int32, (tp, 1), 0) + pl.program_id(0) * tp
    odd = (g >= half).astype(jnp.int32)
    pos = (2 * (g - odd * half) + odd).astype(f32)
    t = pos / max(seq_len - 1, 1)
    ang = (2 * math.pi / seq_len) * pos * band_ref[...]
    lane = lax.broadcasted_iota(jnp.int32, (tp, LANES), 1)
    feats = jnp.where(lane == 0, t,
                      jnp.where(lane <= HYENA_BANDS, jnp.cos(ang),
                                jnp.where(lane <= 2 * HYENA_BANDS, -jnp.sin(ang), 0.0)))
    h = jnp.sin(fr_ref[...] * (jnp.dot(feats, w1_ref[...], precision=hp, preferred_element_type=f32) + b1_ref[...]))
    h = jnp.sin(fr_ref[...] * (jnp.dot(h, w2_ref[...], precision=hp, preferred_element_type=f32) + b2_ref[...]))
    window = jnp.exp(-t * dl_ref[...])
    c = dl_ref.shape[1]
    for j in range(w3_ref.shape[1] // c):
        cols = slice(j * c, (j + 1) * c)
        k_ref[:, cols] = jnp.dot(h, w3_ref[:, cols], precision=hp, preferred_element_type=f32) * window


def hyena_filters_polyphase(seq_len, f_w1, f_b1, f_freq, f_w2, f_b2, f_w3, d_hyena, tp):
    f32 = jnp.float32
    fh = f_w1.shape[1]
    n_emb = 1 + 2 * HYENA_BANDS
    bands = jnp.linspace(1e-4, HYENA_BANDS - 1, HYENA_BANDS, dtype=f32)
    band_row = jnp.concatenate([jnp.zeros((1,), f32), bands, bands, jnp.zeros((LANES - n_emb,), f32)])[None, :]
    w1p = jnp.concatenate([f_w1, jnp.zeros((LANES - n_emb, fh), f32)], axis=0)
    deltas = jnp.abs(jnp.linspace(math.log(HYENA_TARGET) / HYENA_SLOW_DECAY,
                                  math.log(HYENA_TARGET) / HYENA_FAST_DECAY, d_hyena, dtype=f32))[None, :]
    n_out = f_w3.shape[1]
    full = lambda a: pl.BlockSpec(a.shape, lambda i: (0,) * a.ndim)
    args = (band_row, w1p, f_b1[None, :], f_freq[None, :], f_w2, f_b2[None, :], f_w3, deltas)
    return pl.pallas_call(
        functools.partial(_filter_kernel, seq_len=seq_len),
        grid=(seq_len // tp,),
        in_specs=[full(a) for a in args],
        out_specs=pl.BlockSpec((tp, n_out), lambda i: (i, 0)),
        out_shape=jax.ShapeDtypeStruct((seq_len, n_out), f32),
        compiler_params=pltpu.CompilerParams(dimension_semantics=("arbitrary",), vmem_limit_bytes=VMEM_LIMIT_BYTES),
    )(*args)


def _ada_kernel(c_ref, w_ref, b_ref, o_ref):
    cv = c_ref[...]
    s = cv * jax.nn.sigmoid(cv)
    o_ref[...] = jnp.dot(s, w_ref[...], precision=lax.Precision.HIGHEST,
                         preferred_element_type=jnp.float32) + b_ref[...]


def ada_modulation(c_rows, w_ada, b_ada, tn):
    rows, D = c_rows.shape
    N = w_ada.shape[1]
    return pl.pallas_call(
        _ada_kernel,
        grid=(N // tn,),
        in_specs=[pl.BlockSpec((rows, D), lambda j: (0, 0)),
                  pl.BlockSpec((D, tn), lambda j: (0, j)),
                  pl.BlockSpec((1, tn), lambda j: (0, j))],
        out_specs=pl.BlockSpec((rows, tn), lambda j: (0, j)),
        out_shape=jax.ShapeDtypeStruct((rows, N), jnp.float32),
        compiler_params=pltpu.CompilerParams(dimension_semantics=("arbitrary",), vmem_limit_bytes=VMEM_LIMIT_BYTES),
    )(c_rows, w_ada, b_ada[None, :])


def _ssd_kernel(xf_ref, df_ref, xb_ref, db_ref, dskip_ref, y_ref, h_ref, *, n_ctx_chunks):
    s = pl.program_id(1)
    n_steps = pl.num_programs(1)
    Q, G, R, P, N = SSD_CHUNK, SSD_GROUPS, SSD_HPG, SSD_HEAD_DIM, SSD_STATE
    GP = R * P
    bf = jnp.bfloat16

    @pl.when(s == 0)
    def _():
        h_ref[...] = jnp.zeros_like(h_ref)
        y_ref[...] = jnp.zeros_like(y_ref)

    row = lax.broadcasted_iota(jnp.int32, (Q, Q), 0)
    col = lax.broadcasted_iota(jnp.int32, (Q, Q), 1)
    lane_head = lax.broadcasted_iota(jnp.int32, (Q, GP), 1) // P
    is_latent = s >= n_ctx_chunks
    n_lat = n_steps - n_ctx_chunks
    out_chunk = (jnp.clip(s - n_ctx_chunks, 0, n_lat - 1), jnp.clip(n_steps - 1 - s, 0, n_lat - 1))

    for d, (x_ref, da_ref) in enumerate(((xf_ref, df_ref), (xb_ref, db_ref))):
        mask = (row >= col) if d == 0 else (col >= row)
        tri = mask.astype(jnp.float32)
        da = da_ref[0]
        cum = jnp.dot(tri, da, precision=lax.Precision.HIGHEST, preferred_element_type=jnp.float32)
        cum_t = cum.T
        edge = Q - 1 if d == 0 else 0
        blk = x_ref.at[0]
        for g in range(G):
            xg = blk[:, g * GP:(g + 1) * GP]
            bg = blk[:, D_SSD + g * N:D_SSD + (g + 1) * N].astype(bf)
            cg = blk[:, D_SSD + G * N + g * N:D_SSD + G * N + (g + 1) * N].astype(bf)
            heads = [d * SSD_HEADS + g * R + r for r in range(R)]
            dtm = jnp.zeros((Q, GP), jnp.float32)
            cumm = jnp.zeros((Q, GP), jnp.float32)
            totm = jnp.zeros((Q, GP), jnp.float32)
            for r, h in enumerate(heads):
                sel = lane_head == r
                dtm = jnp.where(sel, da[:, h:h + 1], dtm)
                cumm = jnp.where(sel, cum[:, SSD_HEADS * 2 + h:SSD_HEADS * 2 + h + 1], cumm)
                totm = jnp.where(sel, cum[edge:edge + 1, SSD_HEADS * 2 + h:SSD_HEADS * 2 + h + 1], totm)
            xdt = xg * dtm
            hg = h_ref[d, g * GP:(g + 1) * GP, :]

            gmat = lax.dot_general(cg, bg, (((1,), (1,)), ((), ())), preferred_element_type=jnp.float32)
            y_off = lax.dot_general(cg, hg.astype(bf), (((1,), (1,)), ((), ())),
                                    preferred_element_type=jnp.float32) * jnp.exp(cumm)
            if d == 0:
                y_off = y_off + dskip_ref[:, g * GP:(g + 1) * GP] * xg
            parts = []
            for r, h in enumerate(heads):
                a_col = cum[:, SSD_HEADS * 2 + h:SSD_HEADS * 2 + h + 1]
                a_row = cum_t[SSD_HEADS * 2 + h:SSD_HEADS * 2 + h + 1, :]
                decay = jnp.exp(jnp.where(mask, a_col - a_row, NEG_BIG))
                parts.append(jnp.dot((gmat * decay).astype(bf), xdt[:, r * P:(r + 1) * P].astype(bf),
                                     preferred_element_type=jnp.float32))
            y = jnp.where(is_latent, y_off + jnp.concatenate(parts, axis=-1), 0.0)
            rows = pl.ds(pl.multiple_of(out_chunk[d] * Q, Q), Q)
            y_ref[0, rows, g * GP:(g + 1) * GP] += y

            xw = (xdt * jnp.exp(totm - cumm)).astype(bf)
            st = lax.dot_general(xw, bg, (((0,), (0,)), ((), ())), preferred_element_type=jnp.float32)
            for r, h in enumerate(heads):
                dec = jnp.exp(cum_t[SSD_HEADS * 2 + h:SSD_HEADS * 2 + h + 1, edge:edge + 1])
                rs = slice(g * GP + r * P, g * GP + (r + 1) * P)
                h_ref[d, rs, :] = h_ref[d, rs, :] * dec + st[r * P:(r + 1) * P, :]


def ssd_scan_bidir(xbc, dta, d_skip, n_ctx):
    bsz, lt, width = xbc.shape
    Q = SSD_CHUNK
    n_steps = lt // Q
    n_ctx_chunks = n_ctx // Q
    L = lt - n_ctx

    def bwd_chunk(s):
        return jnp.where(s < n_ctx_chunks, n_ctx_chunks - 1 - s, n_steps - 1 - s + n_ctx_chunks)

    return pl.pallas_call(
        functools.partial(_ssd_kernel, n_ctx_chunks=n_ctx_chunks),
        grid=(bsz, n_steps),
        in_specs=[
            pl.BlockSpec((1, Q, width), lambda b, s: (b, s, 0)),
            pl.BlockSpec((1, Q, LANES), lambda b, s: (b, s, 0)),
            pl.BlockSpec((1, Q, width), lambda b, s: (b, bwd_chunk(s), 0)),
            pl.BlockSpec((1, Q, LANES), lambda b, s: (b, bwd_chunk(s), 0)),
            pl.BlockSpec((1, D_SSD), lambda b, s: (0, 0)),
        ],
        out_specs=pl.BlockSpec((1, L, D_SSD), lambda b, s: (b, 0, 0)),
        out_shape=jax.ShapeDtypeStruct((bsz, L, D_SSD), jnp.float32),
        scratch_shapes=[pltpu.VMEM((2, SSD_GROUPS * SSD_HPG * SSD_HEAD_DIM, SSD_STATE), jnp.float32)],
        compiler_params=pltpu.CompilerParams(
            dimension_semantics=("arbitrary", "arbitrary"), vmem_limit_bytes=VMEM_LIMIT_BYTES),
    )(xbc, dta, xbc, dta, d_skip)


def kernel(x, c, ctx, c_ctx, w_ada, b_ada, g_norm1, g_norm2, w_in, hy_conv_w, hy_conv_b, hy_f_w1, hy_f_b1, hy_f_freq, hy_f_w2, hy_f_b2, hy_f_w3, hy_bias, ssd_conv_w, ssd_conv_b, ssd_a_log, ssd_dt_bias, ssd_d, ssd_norm_g, w_out, w_group, b_group, w_expert, b_expert, w1, w3, w2, g_final):
    bsz, seq_len, _ = x.shape
    l = 0
    rows_pad = -(bsz + 1) % SUBLANES
    c_rows = jnp.concatenate([c, c_ctx[None, :], jnp.zeros((rows_pad, D_MODEL), jnp.float32)], axis=0)
    mod_all = ada_modulation(c_rows, w_ada[l], b_ada[l], 512)
    sh1, sc1, ga1, sh2, sc2, ga2 = jnp.split(mod_all[:bsz, None, :], 6, axis=-1)
    csh1, csc1 = mod_all[bsz, :D_MODEL], mod_all[bsz, D_MODEL:2 * D_MODEL]

    w_out_bf = w_out[l].astype(jnp.bfloat16)
    w_in_bf, w_dt_bf, dt_bias2, dt_mult = in_proj_params(w_in[l], ssd_a_log[l], ssd_dt_bias[l],
                                                         HY_COLS, D_SSD, D_XBC)

    u, z, xbc, dta = in_proj_fused(ctx, x, g_norm1[l][None, :], csh1[None, :], csc1[None, :], sh1, sc1,
                                   w_in_bf, w_dt_bf, hy_conv_w[l], hy_conv_b[l][None, :],
                                   ssd_conv_w[l], ssd_conv_b[l][None, :], dt_bias2, dt_mult, GRID_W, 256, 512)
    kp = hyena_filters_polyphase(seq_len, hy_f_w1[l], hy_f_b1[l], hy_f_freq[l], hy_f_w2[l], hy_f_b2[l],
                                 hy_f_w3[l], D_HYENA, 256)
    y_hy = hyena_long_convs(u, kp.reshape(2, seq_len // 2, HYENA_ORDER, 2, D_HYENA), hy_bias[l], 256, 512)
    y_scan = ssd_scan_bidir(xbc, dta, jnp.repeat(ssd_d[l], SSD_HEAD_DIM)[None, :], CTX_LEN)

    pad = LANES - MOE_GROUPS - N_EXPERTS
    w_router = jnp.concatenate([w_group[l], w_expert[l], jnp.zeros((D_MODEL, pad), jnp.float32)], axis=1)
    w_router_hi = w_router.astype(jnp.bfloat16)
    w_router_lo = (w_router - w_router_hi.astype(jnp.float32)).astype(jnp.bfloat16)
    w_router = jnp.stack([w_router_hi, w_router_lo])
    b_router = jnp.concatenate([b_group[l], b_expert[l], jnp.zeros((pad,), jnp.float32)])[None, :]
    x1, hn, route_i, route_f, counts = out_proj_router(
        y_hy, y_scan, z, 0, x, ga1, sc2, sh2, ssd_norm_g[l][None, :], g_norm2[l][None, :],
        w_out_bf, w_router, b_router, 256)
    n_tok = bsz * seq_len
    n_blocks = -(-n_tok * TOP_K // MOE_BLOCK) + N_EXPERTS
    dest, block_eid, block_first, block_valid = moe_plan(route_i, counts, MOE_BLOCK, n_blocks)
    buf = moe_dispatch(hn.reshape(n_tok, D_MODEL), dest, block_valid, MOE_BLOCK, 256)
    yb = expert_blocks(buf, block_eid, block_first, block_valid, w1[l], w3[l], w2[l], MOE_BLOCK)
    return moe_combine(x1, route_f, ga2, g_final[None, :], yb, dest, 256)
```

```python
import functools
import math

import jax
import jax.numpy as jnp
from jax import lax
from jax.experimental import pallas as pl
from jax.experimental.pallas import tpu as pltpu

D_MODEL = 1024
CTX_LEN = 256
GRID_W = 64
EPS = 1e-6
SHORT_CONV = 3

D_HYENA = D_MODEL // 2
HYENA_ORDER = 2
HYENA_BANDS = 8
HYENA_FAST_DECAY = 0.3
HYENA_SLOW_DECAY = 1.5
HYENA_TARGET = 1e-2

D_SSD = D_MODEL // 2
SSD_HEAD_DIM = 64
SSD_HEADS = D_SSD // SSD_HEAD_DIM
SSD_GROUPS = 2
SSD_HPG = SSD_HEADS // SSD_GROUPS
SSD_STATE = 128
SSD_CHUNK = 128

D_XBC = D_SSD + 2 * SSD_GROUPS * SSD_STATE
HY_COLS = (HYENA_ORDER + 1) * D_HYENA
D_IN = HY_COLS + D_SSD + D_XBC + 2 * SSD_HEADS
LANES = 128
SUBLANES = 8
D_IN_PAD = -(-D_IN // LANES) * LANES

MOE_GROUPS = 8
EXPERTS_PER_GROUP = 8
N_EXPERTS = MOE_GROUPS * EXPERTS_PER_GROUP
TOP_K = 2
D_EXPERT = 512
MOE_BLOCK = 256
ROUTE_COLS = 8

VMEM_LIMIT_BYTES = 56 * 1024 * 1024
NEG_BIG = -1e30


def _conv3_rows(p, w_ref, b_ref, cols, has_prev, has_next):
    n = p.shape[0]
    prev = jnp.where(has_prev, pltpu.roll(p, 1, 0), 0.0)
    nxt = jnp.where(has_next, pltpu.roll(p, n - 1, 0), 0.0)
    return b_ref[:, cols] + w_ref[0:1, cols] * prev + w_ref[1:2, cols] * p + w_ref[2:3, cols] * nxt


def _in_proj_kernel(ctx_ref, x_ref, g_ref, csh_ref, csc_ref, sh_ref, sc_ref, w_ref, wdt_ref, hw_ref, hb_ref,
                    sw_ref, sb_ref, dtb_ref, dtm_ref, u_ref, z_ref, xbc_ref, dta_ref, h_ref,
                    *, n_ctx_steps, row_len, ctx_row_len, hy_cols, d_ssd, d_xbc, tn):
    i = pl.program_id(1)
    is_ctx = i < n_ctx_steps
    tm = x_ref.shape[1]
    xin = jnp.where(is_ctx, ctx_ref[0], x_ref[0])
    shift = jnp.where(is_ctx, csh_ref[...], sh_ref[0])
    scale = jnp.where(is_ctx, csc_ref[...], sc_ref[0])
    y = xin * lax.rsqrt(jnp.mean(xin * xin, axis=-1, keepdims=True) + EPS) * g_ref[...]
    h_ref[...] = (y * (1.0 + scale) + shift).astype(jnp.bfloat16)

    pos = lax.broadcasted_iota(jnp.int32, (tm, 1), 0) + jnp.where(is_ctx, i, i - n_ctx_steps) * tm
    in_row = jnp.where(is_ctx, pos % ctx_row_len, pos % row_len)
    has_prev = in_row != 0
    has_next = in_row != jnp.where(is_ctx, ctx_row_len - 1, row_len - 1)

    @pl.when(jnp.logical_not(is_ctx))
    def _():
        for c0 in range(0, hy_cols, tn):
            cols = slice(c0, c0 + tn)
            p = jnp.dot(h_ref[...], w_ref[:, cols], preferred_element_type=jnp.float32)
            u_ref[0, :, cols] = _conv3_rows(p, hw_ref, hb_ref, cols, has_prev, has_next)
        z_ref[0] = jnp.dot(h_ref[...], w_ref[:, hy_cols:hy_cols + d_ssd], preferred_element_type=jnp.float32)

    for c0 in range(0, d_xbc, tn):
        cols = slice(c0, c0 + tn)
        wc = slice(hy_cols + d_ssd + c0, hy_cols + d_ssd + c0 + tn)
        p = jnp.dot(h_ref[...], w_ref[:, wc], preferred_element_type=jnp.float32)
        v = _conv3_rows(p, sw_ref, sb_ref, cols, has_prev, has_next)
        xbc_ref[0, :, cols] = v * jax.nn.sigmoid(v)
    pd = jnp.dot(h_ref[...], wdt_ref[...], preferred_element_type=jnp.float32) + dtb_ref[...]
    sp = jnp.maximum(pd, 0.0) + jnp.log(1.0 + jnp.exp(-jnp.abs(pd)))
    dta_ref[0] = sp * dtm_ref[...]


def in_proj_fused(ctx, x, g1, csh, csc, sh, sc, w_bf, wdt_bf, hy_w, hy_b, ssd_w, ssd_b, dt_bias2, dt_mult,
                  row_len, tm, tn):
    bsz, L, D = x.shape
    lc = ctx.shape[1]
    hy_cols = hy_w.shape[1]
    d_xbc = ssd_w.shape[1]
    d_ssd = w_bf.shape[1] - hy_cols - d_xbc
    n_ctx_steps = lc // tm
    n_steps = n_ctx_steps + L // tm
    lat = lambda b, i: (b, jnp.maximum(i - n_ctx_steps, 0), 0)
    allt = lambda b, i: (b, i, 0)
    const2 = lambda b, i: (0, 0)
    per_b = pl.BlockSpec((1, 1, D), lambda b, i: (b, 0, 0))
    kern = functools.partial(_in_proj_kernel, n_ctx_steps=n_ctx_steps, row_len=row_len, ctx_row_len=lc,
                             hy_cols=hy_cols, d_ssd=d_ssd, d_xbc=d_xbc, tn=tn)
    return pl.pallas_call(
        kern,
        grid=(bsz, n_steps),
        in_specs=[
            pl.BlockSpec((1, tm, D), lambda b, i: (b, jnp.minimum(i, n_ctx_steps - 1), 0)),
            pl.BlockSpec((1, tm, D), lat),
            pl.BlockSpec((1, D), const2),
            pl.BlockSpec((1, D), const2),
            pl.BlockSpec((1, D), const2),
            per_b, per_b,
            pl.BlockSpec(w_bf.shape, const2),
            pl.BlockSpec(wdt_bf.shape, const2),
            pl.BlockSpec(hy_w.shape, const2),
            pl.BlockSpec(hy_b.shape, const2),
            pl.BlockSpec(ssd_w.shape, const2),
            pl.BlockSpec(ssd_b.shape, const2),
            pl.BlockSpec((1, LANES), const2),
            pl.BlockSpec((1, LANES), const2),
        ],
        out_specs=[
            pl.BlockSpec((1, tm, hy_cols), lat),
            pl.BlockSpec((1, tm, d_ssd), lat),
            pl.BlockSpec((1, tm, d_xbc), allt),
            pl.BlockSpec((1, tm, LANES), allt),
        ],
        out_shape=[
            jax.ShapeDtypeStruct((bsz, L, hy_cols), jnp.float32),
            jax.ShapeDtypeStruct((bsz, L, d_ssd), jnp.float32),
            jax.ShapeDtypeStruct((bsz, lc + L, d_xbc), jnp.float32),
            jax.ShapeDtypeStruct((bsz, lc + L, LANES), jnp.float32),
        ],
        scratch_shapes=[pltpu.VMEM((tm, D), jnp.bfloat16)],
        compiler_params=pltpu.CompilerParams(
            dimension_semantics=("arbitrary", "arbitrary"), vmem_limit_bytes=VMEM_LIMIT_BYTES),
    )(ctx, x, g1, csh, csc, sh, sc, w_bf, wdt_bf, hy_w, hy_b, ssd_w, ssd_b, dt_bias2, dt_mult)


def in_proj_params(w_in, a_log, dt_bias, hy_cols, d_ssd, d_xbc):
    n_h = 2 * SSD_HEADS
    main = hy_cols + d_ssd + d_xbc
    w_dt = w_in[:, main:main + n_h]
    pad = jnp.zeros((w_in.shape[0], LANES - 2 * n_h), w_in.dtype)
    wdt = jnp.concatenate([w_dt, w_dt, pad], axis=1).astype(jnp.bfloat16)
    zpad = jnp.zeros((LANES - 2 * n_h,), jnp.float32)
    bias2 = jnp.concatenate([dt_bias.reshape(n_h), dt_bias.reshape(n_h), zpad])[None, :]
    mult = jnp.concatenate([jnp.ones((n_h,), jnp.float32), -jnp.exp(a_log).reshape(n_h), zpad])[None, :]
    return w_in[:, :main].astype(jnp.bfloat16), wdt, bias2, mult


def _out_router_kernel(yh_ref, ys_ref, z_ref, x_ref, ga_ref, sc_ref, sh_ref, ng_ref, g2_ref, wo_ref, wr_ref, br_ref,
                       x1_ref, hn_ref, ri_ref, rf_ref, cnt_ref, carry_ref):
    first = jnp.logical_and(pl.program_id(0) == 0, pl.program_id(1) == 0)

    @pl.when(first)
    def _():
        carry_ref[...] = jnp.zeros_like(carry_ref)

    bf = jnp.bfloat16
    tm = x_ref.shape[1]
    dh = yh_ref.shape[2]
    z = z_ref[0]
    ys = ys_ref[0] * (z * jax.nn.sigmoid(z))
    gw = ys.shape[1] // SSD_GROUPS
    acc = jnp.dot(yh_ref[0].astype(bf), wo_ref[0:dh, :], preferred_element_type=jnp.float32)
    for g in range(SSD_GROUPS):
        yg = ys[:, g * gw:(g + 1) * gw]
        yg = yg * lax.rsqrt(jnp.mean(yg * yg, axis=-1, keepdims=True) + EPS) * ng_ref[:, g * gw:(g + 1) * gw]
        acc += jnp.dot(yg.astype(bf), wo_ref[dh + g * gw:dh + (g + 1) * gw, :], preferred_element_type=jnp.float32)
    x1 = x_ref[0] + ga_ref[0] * acc
    x1_ref[0] = x1
    hn = x1 * lax.rsqrt(jnp.mean(x1 * x1, axis=-1, keepdims=True) + EPS) * g2_ref[...]
    hn = hn * (1.0 + sc_ref[0]) + sh_ref[0]
    hn_ref[0] = hn

    hn_hi = hn.astype(bf)
    hn_lo = (hn - hn_hi.astype(jnp.float32)).astype(bf)
    logits = (jnp.dot(hn_hi, wr_ref[0], preferred_element_type=jnp.float32)
              + jnp.dot(hn_lo, wr_ref[0], preferred_element_type=jnp.float32)
              + jnp.dot(hn_hi, wr_ref[1], preferred_element_type=jnp.float32)) + br_ref[...]
    lane = lax.broadcasted_iota(jnp.int32, (tm, LANES), 1)
    lane_f = lane.astype(jnp.float32)
    ninf = jnp.float32(-jnp.inf)
    big = jnp.float32(1e9)
    gl = jnp.where(lane < MOE_GROUPS, logits, ninf)
    gmax = jnp.max(gl, axis=-1, keepdims=True)
    p_group = 1.0 / jnp.sum(jnp.exp(gl - gmax), axis=-1, keepdims=True)
    g_sel = jnp.min(jnp.where(gl == gmax, lane_f, big), axis=-1, keepdims=True)
    e_lane = lane - MOE_GROUPS
    in_grp = jnp.logical_and(e_lane >= 0, (e_lane // EXPERTS_PER_GROUP).astype(jnp.float32) == g_sel)
    el = jnp.where(in_grp, logits, ninf)
    m1 = jnp.max(el, axis=-1, keepdims=True)
    i1 = jnp.min(jnp.where(el == m1, lane_f, big), axis=-1, keepdims=True)
    el2 = jnp.where(lane_f == i1, ninf, el)
    m2 = jnp.max(el2, axis=-1, keepdims=True)
    i2 = jnp.min(jnp.where(el2 == m2, lane_f, big), axis=-1, keepdims=True)
    t = jnp.exp(m2 - m1)
    w1 = 1.0 / (1.0 + t)
    gate1 = w1 * p_group
    gate2 = (t * w1) * p_group
    e1 = i1 - MOE_GROUPS
    e2 = i2 - MOE_GROUPS
    el_f = e_lane.astype(jnp.float32)
    oh1 = el_f == e1
    oh2 = el_f == e2
    oh = jnp.logical_or(oh1, oh2).astype(bf)
    r_i = lax.broadcasted_iota(jnp.int32, (tm, tm), 0)
    c_i = lax.broadcasted_iota(jnp.int32, (tm, tm), 1)
    before = jnp.dot((c_i < r_i).astype(bf), oh, preferred_element_type=jnp.float32) + carry_ref[...]
    rank1 = jnp.sum(jnp.where(oh1, before, 0.0), axis=-1, keepdims=True)
    rank2 = jnp.sum(jnp.where(oh2, before, 0.0), axis=-1, keepdims=True)
    carry_ref[...] += jnp.sum(oh.astype(jnp.float32), axis=0, keepdims=True)
    cnt_ref[...] = carry_ref[...]

    rec = jnp.where(lane == 0, e1, jnp.where(lane == 1, e2, jnp.where(lane == 2, rank1,
                                                                      jnp.where(lane == 3, rank2, 0.0))))
    ri_ref[0] = rec.T[0:ROUTE_COLS, :].astype(jnp.int32)
    col = lax.broadcasted_iota(jnp.int32, (tm, ROUTE_COLS), 1)
    rf_ref[0] = jnp.where(col == 0, gate1, gate2)


def out_proj_router(y_hy, y_scan, px, z_col, x, ga1, sc2, sh2, norm_g, g2, w_out_bf, w_router, b_router, tm):
    bsz, L, D = x.shape
    dh = y_hy.shape[-1]
    ds = y_scan.shape[-1]
    tok = lambda b, i: (b, i, 0)
    per_b = pl.BlockSpec((1, 1, D), lambda b, i: (b, 0, 0))
    const2 = lambda b, i: (0, 0)
    return pl.pallas_call(
        _out_router_kernel,
        grid=(bsz, L // tm),
        in_specs=[
            pl.BlockSpec((1, tm, dh), tok),
            pl.BlockSpec((1, tm, ds), tok),
            pl.BlockSpec((1, tm, ds), lambda b, i: (b, i, z_col)),
            pl.BlockSpec((1, tm, D), tok),
            per_b, per_b, per_b,
            pl.BlockSpec((1, ds), const2),
            pl.BlockSpec((1, D), const2),
            pl.BlockSpec((dh + ds, D), const2),
            pl.BlockSpec((2, D, LANES), lambda b, i: (0, 0, 0)),
            pl.BlockSpec((1, LANES), const2),
        ],
        out_specs=[
            pl.BlockSpec((1, tm, D), tok),
            pl.BlockSpec((1, tm, D), tok),
            pl.BlockSpec((1, ROUTE_COLS, tm), lambda b, i: (b, 0, i)),
            pl.BlockSpec((1, tm, ROUTE_COLS), tok),
            pl.BlockSpec((1, LANES), const2),
        ],
        out_shape=[
            jax.ShapeDtypeStruct((bsz, L, D), jnp.float32),
            jax.ShapeDtypeStruct((bsz, L, D), jnp.float32),
            jax.ShapeDtypeStruct((bsz, ROUTE_COLS, L), jnp.int32),
            jax.ShapeDtypeStruct((bsz, L, ROUTE_COLS), jnp.float32),
            jax.ShapeDtypeStruct((1, LANES), jnp.float32),
        ],
        scratch_shapes=[pltpu.VMEM((1, LANES), jnp.float32)],
        compiler_params=pltpu.CompilerParams(
            dimension_semantics=("arbitrary", "arbitrary"), vmem_limit_bytes=VMEM_LIMIT_BYTES),
    )(y_hy, y_scan, px, x, ga1, sc2, sh2, norm_g, g2, w_out_bf, w_router, b_router)


def _row_copy(src_hbm, src_row, dst_ref, dst_row, sem):
    return pltpu.make_async_copy(src_hbm.at[pl.ds(src_row, 1), :], dst_ref.at[pl.ds(dst_row, 1), :], sem)


def _rows_kernel(dest_ref, tok_ref):
    n_tok = dest_ref.shape[0] // TOP_K

    def zero(r, carry):
        tok_ref[r] = 0
        return carry

    def fill(t, carry):
        for k in range(TOP_K):
            tok_ref[dest_ref[k * n_tok + t]] = t
        return carry

    lax.fori_loop(0, tok_ref.shape[0], zero, 0, unroll=8)
    lax.fori_loop(0, n_tok, fill, 0, unroll=8)


def moe_row_tokens(dest, n_rows):
    grid_spec = pltpu.PrefetchScalarGridSpec(
        num_scalar_prefetch=1,
        grid=(1,),
        in_specs=[],
        out_specs=pl.BlockSpec(memory_space=pltpu.SMEM),
    )
    return pl.pallas_call(
        _rows_kernel,
        grid_spec=grid_spec,
        out_shape=jax.ShapeDtypeStruct((n_rows,), jnp.int32),
        compiler_params=pltpu.CompilerParams(dimension_semantics=("arbitrary",)),
    )(dest)


def _expert_kernel(eid_ref, first_ref, valid_ref, tok_ref, hn_hbm, w1_ref, w3_ref, w2_ref, o_ref,
                   xbuf, w1b, w3b, w2b, sem):
    i = pl.program_id(0)
    n = pl.num_programs(0)
    del eid_ref
    bf = jnp.bfloat16
    blk = xbuf.shape[1]
    slot = i % 2

    def issue(block, slot_):
        def body(j, carry):
            _row_copy(hn_hbm, tok_ref[block * blk + j], xbuf.at[slot_], j, sem.at[slot_]).start()
            return carry
        lax.fori_loop(0, blk, body, 0, unroll=8)

    @pl.when(jnp.logical_and(i == 0, valid_ref[0] > 0))
    def _():
        issue(0, 0)

    nxt = jnp.minimum(i + 1, n - 1)

    @pl.when(jnp.logical_and(i + 1 < n, valid_ref[nxt] > 0))
    def _():
        issue(i + 1, 1 - slot)

    @pl.when(first_ref[i] == 1)
    def _():
        w1b[...] = w1_ref[0].astype(bf)
        w3b[...] = w3_ref[0].astype(bf)
        w2b[...] = w2_ref[0].astype(bf)

    valid = valid_ref[i]

    @pl.when(valid > 0)
    def _():
        pltpu.make_async_copy(hn_hbm.at[pl.ds(0, blk), :], xbuf.at[slot], sem.at[slot]).wait()
        xb = xbuf[slot].astype(bf)
        a = jnp.dot(xb, w1b[...], preferred_element_type=jnp.float32)
        b = jnp.dot(xb, w3b[...], preferred_element_type=jnp.float32)
        h = (a * jax.nn.sigmoid(a)) * b
        o_ref[...] = jnp.dot(h.astype(bf), w2b[...], preferred_element_type=jnp.float32)

    @pl.when(valid <= 0)
    def _():
        o_ref[...] = jnp.zeros_like(o_ref)


def expert_blocks(hn, row_tok, block_eid, block_first, block_valid, w1, w3, w2, blk):
    T, D = hn.shape
    n_blocks = block_eid.shape[0]
    E, _, F = w1.shape
    grid_spec = pltpu.PrefetchScalarGridSpec(
        num_scalar_prefetch=4,
        grid=(n_blocks,),
        in_specs=[
            pl.BlockSpec(memory_space=pl.ANY),
            pl.BlockSpec((1, D, F), lambda i, eid, fi, va, tk: (eid[i], 0, 0)),
            pl.BlockSpec((1, D, F), lambda i, eid, fi, va, tk: (eid[i], 0, 0)),
            pl.BlockSpec((1, F, D), lambda i, eid, fi, va, tk: (eid[i], 0, 0)),
        ],
        out_specs=pl.BlockSpec((blk, D), lambda i, eid, fi, va, tk: (i, 0)),
        scratch_shapes=[pltpu.VMEM((2, blk, D), jnp.float32),
                        pltpu.VMEM((D, F), jnp.bfloat16), pltpu.VMEM((D, F), jnp.bfloat16),
                        pltpu.VMEM((F, D), jnp.bfloat16), pltpu.SemaphoreType.DMA((2,))],
    )
    return pl.pallas_call(
        _expert_kernel,
        grid_spec=grid_spec,
        out_shape=jax.ShapeDtypeStruct((n_blocks * blk, D), jnp.float32),
        compiler_params=pltpu.CompilerParams(
            dimension_semantics=("arbitrary",), vmem_limit_bytes=VMEM_LIMIT_BYTES),
    )(block_eid, block_first, block_valid, row_tok, hn, w1, w3, w2)


def _combine_kernel(dest_ref, x1_ref, rf_ref, ga_ref, gf_ref, yb_hbm, o_ref, ybuf, sem):
    b = pl.program_id(0)
    i = pl.program_id(1)
    n_i = pl.num_programs(1)
    tm = x1_ref.shape[1]
    step = b * n_i + i
    n_steps = pl.num_programs(0) * n_i
    slot = step % 2

    def issue(step_, slot_):
        def body(j, carry):
            t = step_ * tm + j
            _row_copy(yb_hbm, dest_ref[t], ybuf.at[slot_, 0], j, sem.at[slot_]).start()
            _row_copy(yb_hbm, dest_ref[n_steps * tm + t], ybuf.at[slot_, 1], j, sem.at[slot_]).start()
            return carry
        lax.fori_loop(0, tm, body, 0, unroll=8)

    @pl.when(step == 0)
    def _():
        issue(0, 0)

    @pl.when(step + 1 < n_steps)
    def _():
        issue(step + 1, 1 - slot)

    pltpu.make_async_copy(yb_hbm.at[pl.ds(0, tm), :], ybuf.at[slot, 0], sem.at[slot]).wait()
    pltpu.make_async_copy(yb_hbm.at[pl.ds(0, tm), :], ybuf.at[slot, 1], sem.at[slot]).wait()
    rf = rf_ref[0]
    y = rf[:, 0:1] * ybuf[slot, 0] + rf[:, 1:2] * ybuf[slot, 1]
    x2 = x1_ref[0] + ga_ref[0] * y
    o_ref[0] = x2 * lax.rsqrt(jnp.mean(x2 * x2, axis=-1, keepdims=True) + EPS) * gf_ref[...]


def moe_combine(x1, route_f, ga2, g_final, yb, dest, tm):
    bsz, L, D = x1.shape
    grid_spec = pltpu.PrefetchScalarGridSpec(
        num_scalar_prefetch=1,
        grid=(bsz, L // tm),
        in_specs=[
            pl.BlockSpec((1, tm, D), lambda b, i, d: (b, i, 0)),
            pl.BlockSpec((1, tm, ROUTE_COLS), lambda b, i, d: (b, i, 0)),
            pl.BlockSpec((1, 1, D), lambda b, i, d: (b, 0, 0)),
            pl.BlockSpec((1, D), lambda b, i, d: (0, 0)),
            pl.BlockSpec(memory_space=pl.ANY),
        ],
        out_specs=pl.BlockSpec((1, tm, D), lambda b, i, d: (b, i, 0)),
        scratch_shapes=[pltpu.VMEM((2, 2, tm, D), jnp.float32), pltpu.SemaphoreType.DMA((2,))],
    )
    return pl.pallas_call(
        _combine_kernel,
        grid_spec=grid_spec,
        out_shape=jax.ShapeDtypeStruct((bsz, L, D), jnp.float32),
        compiler_params=pltpu.CompilerParams(
            dimension_semantics=("arbitrary", "arbitrary"), vmem_limit_bytes=VMEM_LIMIT_BYTES),
    )(dest, x1, route_f, ga2, g_final, yb)


def moe_plan(route_i, counts, blk, n_blocks):
    cnt = counts[0, MOE_GROUPS:MOE_GROUPS + N_EXPERTS].astype(jnp.int32)
    padded = (cnt + blk - 1) // blk * blk
    ends = jnp.cumsum(padded)
    starts = ends - padded
    experts = jnp.arange(N_EXPERTS, dtype=jnp.int32)
    dest = jnp.concatenate([
        (jnp.sum(jnp.where(route_i[:, k, :, None] == experts, starts, 0), axis=-1) + route_i[:, 2 + k]).reshape(-1)
        for k in range(TOP_K)])
    first_row = jnp.arange(n_blocks, dtype=jnp.int32) * blk
    block_eid = jnp.minimum(jnp.sum((ends[None, :] <= first_row[:, None]).astype(jnp.int32), axis=1), N_EXPERTS - 1)
    block_valid = jnp.clip(cnt[block_eid] - (first_row - starts[block_eid]), 0, blk).astype(jnp.int32)
    block_first = jnp.concatenate([jnp.ones((1,), jnp.int32),
                                   (block_eid[1:] != block_eid[:-1]).astype(jnp.int32)])
    return dest, block_eid, block_first, block_valid


def dft_tables(L):
    n = 2 * L
    f = lax.broadcasted_iota(jnp.int32, (L, L), 0)
    t = lax.broadcasted_iota(jnp.int32, (L, L), 1)
    ang = ((f * t) % n).astype(jnp.float32) * (2.0 * math.pi / n)
    return jnp.cos(ang).astype(jnp.bfloat16), jnp.sin(ang).astype(jnp.bfloat16)


def _alt_sign(L):
    t = lax.broadcasted_iota(jnp.int32, (L, 1), 0)
    return (1 - 2 * (t & 1)).astype(jnp.float32)


def _spectrum_kernel(a_ref, b_ref, c_ref, s_ref, kr_ref, ks_ref, kn_ref):
    L = a_ref.shape[1]
    a = a_ref[0]
    row = lax.broadcasted_iota(jnp.int32, (L, 1), 0)
    scale = jnp.where(row == 0, 0.5 / L, 1.0 / L)
    kr_ref[0] = scale * jnp.dot(c_ref[...], a.astype(jnp.bfloat16), preferred_element_type=jnp.float32)
    ks_ref[0] = scale * jnp.dot(s_ref[...], b_ref[0].astype(jnp.bfloat16), preferred_element_type=jnp.float32)
    kn_ref[0] = jnp.sum(a * _alt_sign(L), axis=0, keepdims=True) * (0.5 / L)


def filter_spectrum(a, b, cos_t, sin_t, tc):
    n, L, C = a.shape
    blk = pl.BlockSpec((1, L, tc), lambda o, j: (o, 0, j))
    tab = pl.BlockSpec((L, L), lambda o, j: (0, 0))
    return pl.pallas_call(
        _spectrum_kernel,
        grid=(n, C // tc),
        in_specs=[blk, blk, tab, tab],
        out_specs=[blk, blk, pl.BlockSpec((1, 1, tc), lambda o, j: (o, 0, j))],
        out_shape=[jax.ShapeDtypeStruct((n, L, C), jnp.float32)] * 2 + [jax.ShapeDtypeStruct((n, 1, C), jnp.float32)],
        compiler_params=pltpu.CompilerParams(
            dimension_semantics=("arbitrary", "arbitrary"), vmem_limit_bytes=VMEM_LIMIT_BYTES),
    )(a, b, cos_t, sin_t)


def _long_conv_kernel(*refs, tf, n_slab):
    z_refs = refs[:n_slab]
    xn_refs = refs[n_slab:2 * n_slab]
    kr_ref, ks_ref, kn_ref, bias_ref, c_ref, s_ref, o_ref, acc_ref, stage_ref = refs[2 * n_slab:]
    H = z_refs[0].shape[1] // 2
    f32 = jnp.float32
    bf = jnp.bfloat16
    sign = _alt_sign(H)

    def phase(slab_refs, p):
        return jnp.concatenate([r[0, pl.ds(p, H, stride=2), :] for r in slab_refs], axis=1)

    z2 = [phase(z_refs, p) for p in range(2)]
    zb = [z.astype(bf) for z in z2]
    z_nyq = [jnp.sum(z * sign, axis=0, keepdims=True) for z in z2]
    filt = ((0, 2), (1, 0))
    for p in range(2):
        nyq = z_nyq[0] * kn_ref[filt[p][0]] + z_nyq[1] * kn_ref[filt[p][1]]
        acc_ref[p] = z2[p] * bias_ref[0] + sign * nyq
    for ft in range(H // tf):
        rows = pl.ds(ft * tf, tf)
        zr = [jnp.dot(c_ref[rows, :], zb[q], preferred_element_type=f32) for q in range(2)]
        zs = [jnp.dot(s_ref[rows, :], zb[q], preferred_element_type=f32) for q in range(2)]
        for p in range(2):
            yr = 0.0
            ys = 0.0
            for q in range(2):
                kr = kr_ref[filt[p][q], rows, :]
                ks = ks_ref[filt[p][q], rows, :]
                yr = yr + zr[q] * kr - zs[q] * ks
                ys = ys + zr[q] * ks + zs[q] * kr
            acc_ref[p] += (jnp.dot(c_ref[:, rows], yr.astype(bf), preferred_element_type=f32)
                           + jnp.dot(s_ref[:, rows], ys.astype(bf), preferred_element_type=f32))
    for p in range(2):
        out_p = phase(xn_refs, p) * acc_ref[p]
        for sl in range(n_slab):
            stage_ref[sl, pl.ds(p, H, stride=2), :] = out_p[:, sl * LANES:(sl + 1) * LANES]
    for sl in range(n_slab):
        o_ref[0, :, sl * LANES:(sl + 1) * LANES] = stage_ref[sl]


def long_conv_gate(z_arr, z_col, xn_arr, xn_col, kr, ks, kn, bias, cos_t, sin_t, tc, tf):
    bsz, L, _ = z_arr.shape
    H = L // 2
    C = kr.shape[-1]
    nj = C // tc
    n_slab = tc // LANES
    tab = pl.BlockSpec((H, H), lambda j, b: (0, 0), pipeline_mode=pl.Buffered(1))
    spec = pl.BlockSpec((3, H, tc), lambda j, b: (0, 0, j), pipeline_mode=pl.Buffered(1))
    nyq = pl.BlockSpec((3, 1, tc), lambda j, b: (0, 0, j))
    vec = pl.BlockSpec((1, 1, tc), lambda j, b: (0, 0, j))

    def slabs(col):
        return [pl.BlockSpec((1, L, LANES),
                             functools.partial(lambda j, b, sl: (b, 0, (col * nj + j) * n_slab + sl), sl=sl))
                for sl in range(n_slab)]

    return pl.pallas_call(
        functools.partial(_long_conv_kernel, tf=tf, n_slab=n_slab),
        grid=(nj, bsz),
        in_specs=slabs(z_col) + slabs(xn_col) + [spec, spec, nyq, vec, tab, tab],
        out_specs=pl.BlockSpec((1, L, tc), lambda j, b: (b, 0, j)),
        out_shape=jax.ShapeDtypeStruct((bsz, L, C), jnp.float32),
        scratch_shapes=[pltpu.VMEM((2, H, tc), jnp.float32), pltpu.VMEM((n_slab, L, LANES), jnp.float32)],
        compiler_params=pltpu.CompilerParams(
            dimension_semantics=("arbitrary", "arbitrary"), vmem_limit_bytes=VMEM_LIMIT_BYTES),
    )(*([z_arr] * n_slab), *([xn_arr] * n_slab), kr, ks, kn, bias, cos_t, sin_t)


def _polyphase_taps(fe, fo, be, bo):
    zero = jnp.zeros_like(fe[:1])
    plus = jnp.stack([fe, fo, jnp.concatenate([bo[0:1], fo[:-1]], axis=0)])
    minus = jnp.stack([jnp.concatenate([zero, be[1:]], axis=0),
                       jnp.concatenate([zero, bo[:-1]], axis=0),
                       jnp.concatenate([zero, bo[1:]], axis=0)])
    return plus, minus


def hyena_long_convs(u, kp, h_bias, tc, tf):
    L = u.shape[1]
    H = L // 2
    C = h_bias.shape[1]
    cos_t, sin_t = dft_tables(H)
    z = u
    for o in range(h_bias.shape[0]):
        fwd = slice((2 * o) * C, (2 * o + 1) * C)
        bwd = slice((2 * o + 1) * C, (2 * o + 2) * C)
        plus, minus = _polyphase_taps(kp[:H, fwd], kp[H:, fwd], kp[:H, bwd], kp[H:, bwd])
        kr, ks, kn = filter_spectrum(plus + minus, plus - minus, cos_t, sin_t, tc)
        z = long_conv_gate(z, 0, u, o + 1, kr, ks, kn, h_bias[o][None, None, :], cos_t, sin_t, tc, tf)
    return z


def _filter_kernel(band_ref, w1_ref, b1_ref, fr_ref, w2_ref, b2_ref, w3_ref, dl_ref, k_ref, *, seq_len):
    hp = lax.Precision.HIGHEST
    f32 = jnp.float32
    tp = k_ref.shape[0]
    half = seq_len // 2
    g = lax.broadcasted_iota(jnp.int32, (tp, 1), 0) + pl.program_id(0) * tp
    odd = (g >= half).astype(jnp.int32)
    pos = (2 * (g - odd * half) + odd).astype(f32)
    t = pos / max(seq_len - 1, 1)
    ang = (2 * math.pi / seq_len) * pos * band_ref[...]
    lane = lax.broadcasted_iota(jnp.int32, (tp, LANES), 1)
    feats = jnp.where(lane == 0, t,
                      jnp.where(lane <= HYENA_BANDS, jnp.cos(ang),
                                jnp.where(lane <= 2 * HYENA_BANDS, -jnp.sin(ang), 0.0)))
    h = jnp.sin(fr_ref[...] * (jnp.dot(feats, w1_ref[...], precision=hp, preferred_element_type=f32) + b1_ref[...]))
    h = jnp.sin(fr_ref[...] * (jnp.dot(h, w2_ref[...], precision=hp, preferred_element_type=f32) + b2_ref[...]))
    window = jnp.exp(-t * dl_ref[...])
    c = dl_ref.shape[1]
    for j in range(w3_ref.shape[1] // c):
        cols = slice(j * c, (j + 1) * c)
        k_ref[:, cols] = jnp.dot(h, w3_ref[:, cols], precision=hp, preferred_element_type=f32) * window


def hyena_filters_polyphase(seq_len, f_w1, f_b1, f_freq, f_w2, f_b2, f_w3, d_hyena, tp):
    f32 = jnp.float32
    fh = f_w1.shape[1]
    n_emb = 1 + 2 * HYENA_BANDS
    bands = jnp.linspace(1e-4, HYENA_BANDS - 1, HYENA_BANDS, dtype=f32)
    band_row = jnp.concatenate([jnp.zeros((1,), f32), bands, bands, jnp.zeros((LANES - n_emb,), f32)])[None, :]
    w1p = jnp.concatenate([f_w1, jnp.zeros((LANES - n_emb, fh), f32)], axis=0)
    deltas = jnp.abs(jnp.linspace(math.log(HYENA_TARGET) / HYENA_SLOW_DECAY,
                                  math.log(HYENA_TARGET) / HYENA_FAST_DECAY, d_hyena, dtype=f32))[None, :]
    n_out = f_w3.shape[1]
    full = lambda a: pl.BlockSpec(a.shape, lambda i: (0,) * a.ndim)
    args = (band_row, w1p, f_b1[None, :], f_freq[None, :], f_w2, f_b2[None, :], f_w3, deltas)
    return pl.pallas_call(
        functools.partial(_filter_kernel, seq_len=seq_len),
        grid=(seq_len // tp,),
        in_specs=[full(a) for a in args],
        out_specs=pl.BlockSpec((tp, n_out), lambda i: (i, 0)),
        out_shape=jax.ShapeDtypeStruct((seq_len, n_out), f32),
        compiler_params=pltpu.CompilerParams(dimension_semantics=("arbitrary",), vmem_limit_bytes=VMEM_LIMIT_BYTES),
    )(*args)


def _ada_kernel(c_ref, w_ref, b_ref, o_ref):
    cv = c_ref[...]
    s = cv * jax.nn.sigmoid(cv)
    o_ref[...] = jnp.dot(s, w_ref[...], precision=lax.Precision.HIGHEST,
                         preferred_element_type=jnp.float32) + b_ref[...]


def ada_modulation(c_rows, w_ada, b_ada, tn):
    rows, D = c_rows.shape
    N = w_ada.shape[1]
    return pl.pallas_call(
        _ada_kernel,
        grid=(N // tn,),
        in_specs=[pl.BlockSpec((rows, D), lambda j: (0, 0)),
                  pl.BlockSpec((D, tn), lambda j: (0, j)),
                  pl.BlockSpec((1, tn), lambda j: (0, j))],
        out_specs=pl.BlockSpec((rows, tn), lambda j: (0, j)),
        out_shape=jax.ShapeDtypeStruct((rows, N), jnp.float32),
        compiler_params=pltpu.CompilerParams(dimension_semantics=("arbitrary",), vmem_limit_bytes=VMEM_LIMIT_BYTES),
    )(c_rows, w_ada, b_ada[None, :])


def _ssd_kernel(xf_ref, df_ref, xb_ref, db_ref, dskip_ref, y_ref, h_ref, *, n_ctx_chunks):
    s = pl.program_id(1)
    n_steps = pl.num_programs(1)
    Q, G, R, P, N = SSD_CHUNK, SSD_GROUPS, SSD_HPG, SSD_HEAD_DIM, SSD_STATE
    GP = R * P
    bf = jnp.bfloat16

    @pl.when(s == 0)
    def _():
        h_ref[...] = jnp.zeros_like(h_ref)
        y_ref[...] = jnp.zeros_like(y_ref)

    row = lax.broadcasted_iota(jnp.int32, (Q, Q), 0)
    col = lax.broadcasted_iota(jnp.int32, (Q, Q), 1)
    lane_head = lax.broadcasted_iota(jnp.int32, (Q, GP), 1) // P
    is_latent = s >= n_ctx_chunks
    n_lat = n_steps - n_ctx_chunks
    out_chunk = (jnp.clip(s - n_ctx_chunks, 0, n_lat - 1), jnp.clip(n_steps - 1 - s, 0, n_lat - 1))

    for d, (x_ref, da_ref) in enumerate(((xf_ref, df_ref), (xb_ref, db_ref))):
        mask = (row >= col) if d == 0 else (col >= row)
        tri = mask.astype(jnp.float32)
        da = da_ref[0]
        cum = jnp.dot(tri, da, precision=lax.Precision.HIGHEST, preferred_element_type=jnp.float32)
        cum_t = cum.T
        edge = Q - 1 if d == 0 else 0
        blk = x_ref.at[0]
        for g in range(G):
            xg = blk[:, g * GP:(g + 1) * GP]
            bg = blk[:, D_SSD + g * N:D_SSD + (g + 1) * N].astype(bf)
            cg = blk[:, D_SSD + G * N + g * N:D_SSD + G * N + (g + 1) * N].astype(bf)
            heads = [d * SSD_HEADS + g * R + r for r in range(R)]
            dtm = jnp.zeros((Q, GP), jnp.float32)
            cumm = jnp.zeros((Q, GP), jnp.float32)
            totm = jnp.zeros((Q, GP), jnp.float32)
            for r, h in enumerate(heads):
                sel = lane_head == r
                dtm = jnp.where(sel, da[:, h:h + 1], dtm)
                cumm = jnp.where(sel, cum[:, SSD_HEADS * 2 + h:SSD_HEADS * 2 + h + 1], cumm)
                totm = jnp.where(sel, cum[edge:edge + 1, SSD_HEADS * 2 + h:SSD_HEADS * 2 + h + 1], totm)
            xdt = xg * dtm
            hg = h_ref[d, g * GP:(g + 1) * GP, :]

            gmat = lax.dot_general(cg, bg, (((1,), (1,)), ((), ())), preferred_element_type=jnp.float32)
            y_off = lax.dot_general(cg, hg.astype(bf), (((1,), (1,)), ((), ())),
                                    preferred_element_type=jnp.float32) * jnp.exp(cumm)
            if d == 0:
                y_off = y_off + dskip_ref[:, g * GP:(g + 1) * GP] * xg
            parts = []
            for r, h in enumerate(heads):
                a_col = cum[:, SSD_HEADS * 2 + h:SSD_HEADS * 2 + h + 1]
                a_row = cum_t[SSD_HEADS * 2 + h:SSD_HEADS * 2 + h + 1, :]
                decay = jnp.exp(jnp.where(mask, a_col - a_row, NEG_BIG))
                parts.append(jnp.dot((gmat * decay).astype(bf), xdt[:, r * P:(r + 1) * P].astype(bf),
                                     preferred_element_type=jnp.float32))
            y = jnp.where(is_latent, y_off + jnp.concatenate(parts, axis=-1), 0.0)
            rows = pl.ds(pl.multiple_of(out_chunk[d] * Q, Q), Q)
            y_ref[0, rows, g * GP:(g + 1) * GP] += y

            xw = (xdt * jnp.exp(totm - cumm)).astype(bf)
            st = lax.dot_general(xw, bg, (((0,), (0,)), ((), ())), preferred_element_type=jnp.float32)
            for r, h in enumerate(heads):
                dec = jnp.exp(cum_t[SSD_HEADS * 2 + h:SSD_HEADS * 2 + h + 1, edge:edge + 1])
                rs = slice(g * GP + r * P, g * GP + (r + 1) * P)
                h_ref[d, rs, :] = h_ref[d, rs, :] * dec + st[r * P:(r + 1) * P, :]


def ssd_scan_bidir(xbc, dta, d_skip, n_ctx):
    bsz, lt, width = xbc.shape
    Q = SSD_CHUNK
    n_steps = lt // Q
    n_ctx_chunks = n_ctx // Q
    L = lt - n_ctx

    def bwd_chunk(s):
        return jnp.where(s < n_ctx_chunks, n_ctx_chunks - 1 - s, n_steps - 1 - s + n_ctx_chunks)

    return pl.pallas_call(
        functools.partial(_ssd_kernel, n_ctx_chunks=n_ctx_chunks),
        grid=(bsz, n_steps),
        in_specs=[
            pl.BlockSpec((1, Q, width), lambda b, s: (b, s, 0)),
            pl.BlockSpec((1, Q, LANES), lambda b, s: (b, s, 0)),
            pl.BlockSpec((1, Q, width), lambda b, s: (b, bwd_chunk(s), 0)),
            pl.BlockSpec((1, Q, LANES), lambda b, s: (b, bwd_chunk(s), 0)),
            pl.BlockSpec((1, D_SSD), lambda b, s: (0, 0)),
        ],
        out_specs=pl.BlockSpec((1, L, D_SSD), lambda b, s: (b, 0, 0)),
        out_shape=jax.ShapeDtypeStruct((bsz, L, D_SSD), jnp.float32),
        scratch_shapes=[pltpu.VMEM((2, SSD_GROUPS * SSD_HPG * SSD_HEAD_DIM, SSD_STATE), jnp.float32)],
        compiler_params=pltpu.CompilerParams(
            dimension_semantics=("arbitrary", "arbitrary"), vmem_limit_bytes=VMEM_LIMIT_BYTES),
    )(xbc, dta, xbc, dta, d_skip)


def kernel(x, c, ctx, c_ctx, w_ada, b_ada, g_norm1, g_norm2, w_in, hy_conv_w, hy_conv_b, hy_f_w1, hy_f_b1, hy_f_freq, hy_f_w2, hy_f_b2, hy_f_w3, hy_bias, ssd_conv_w, ssd_conv_b, ssd_a_log, ssd_dt_bias, ssd_d, ssd_norm_g, w_out, w_group, b_group, w_expert, b_expert, w1, w3, w2, g_final):
    bsz, seq_len, _ = x.shape
    l = 0
    rows_pad = -(bsz + 1) % SUBLANES
    c_rows = jnp.concatenate([c, c_ctx[None, :], jnp.zeros((rows_pad, D_MODEL), jnp.float32)], axis=0)
    mod_all = ada_modulation(c_rows, w_ada[l], b_ada[l], 512)
    sh1, sc1, ga1, sh2, sc2, ga2 = jnp.split(mod_all[:bsz, None, :], 6, axis=-1)
    csh1, csc1 = mod_all[bsz, :D_MODEL], mod_all[bsz, D_MODEL:2 * D_MODEL]

    w_out_bf = w_out[l].astype(jnp.bfloat16)
    w_in_bf, w_dt_bf, dt_bias2, dt_mult = in_proj_params(w_in[l], ssd_a_log[l], ssd_dt_bias[l],
                                                         HY_COLS, D_SSD, D_XBC)

    u, z, xbc, dta = in_proj_fused(ctx, x, g_norm1[l][None, :], csh1[None, :], csc1[None, :], sh1, sc1,
                                   w_in_bf, w_dt_bf, hy_conv_w[l], hy_conv_b[l][None, :],
                                   ssd_conv_w[l], ssd_conv_b[l][None, :], dt_bias2, dt_mult, GRID_W, 256, 512)
    kp = hyena_filters_polyphase(seq_len, hy_f_w1[l], hy_f_b1[l], hy_f_freq[l], hy_f_w2[l], hy_f_b2[l],
                                 hy_f_w3[l], D_HYENA, 256)
    y_hy = hyena_long_convs(u, kp, hy_bias[l], 256, 512)
    y_scan = ssd_scan_bidir(xbc, dta, jnp.repeat(ssd_d[l], SSD_HEAD_DIM)[None, :], CTX_LEN)

    pad = LANES - MOE_GROUPS - N_EXPERTS
    w_router = jnp.concatenate([w_group[l], w_expert[l], jnp.zeros((D_MODEL, pad), jnp.float32)], axis=1)
    w_router_hi = w_router.astype(jnp.bfloat16)
    w_router_lo = (w_router - w_router_hi.astype(jnp.float32)).astype(jnp.bfloat16)
    w_router = jnp.stack([w_router_hi, w_router_lo])
    b_router = jnp.concatenate([b_group[l], b_expert[l], jnp.zeros((pad,), jnp.float32)])[None, :]
    x1, hn, route_i, route_f, counts = out_proj_router(
        y_hy, y_scan, z, 0, x, ga1, sc2, sh2, ssd_norm_g[l][None, :], g_norm2[l][None, :],
        w_out_bf, w_router, b_router, 256)
    n_tok = bsz * seq_len
    n_blocks = -(-n_tok * TOP_K // MOE_BLOCK) + N_EXPERTS
    dest, block_eid, block_first, block_valid = moe_plan(route_i, counts, MOE_BLOCK, n_blocks)
    row_tok = moe_row_tokens(dest, n_blocks * MOE_BLOCK)
    yb = expert_blocks(hn.reshape(n_tok, D_MODEL), row_tok, block_eid, block_first, block_valid,
                       w1[l], w3[l], w2[l], MOE_BLOCK)
    return moe_combine(x1, route_f, ga2, g_final[None, :], yb, dest, 256)
```

```python
import functools
import math

import jax
import jax.numpy as jnp
from jax import lax
from jax.experimental import pallas as pl
from jax.experimental.pallas import tpu as pltpu

D_MODEL = 1024
CTX_LEN = 256
GRID_W = 64
EPS = 1e-6
SHORT_CONV = 3

D_HYENA = D_MODEL // 2
HYENA_ORDER = 2
HYENA_BANDS = 8
HYENA_FAST_DECAY = 0.3
HYENA_SLOW_DECAY = 1.5
HYENA_TARGET = 1e-2

D_SSD = D_MODEL // 2
SSD_HEAD_DIM = 64
SSD_HEADS = D_SSD // SSD_HEAD_DIM
SSD_GROUPS = 2
SSD_HPG = SSD_HEADS // SSD_GROUPS
SSD_STATE = 128
SSD_CHUNK = 128

D_XBC = D_SSD + 2 * SSD_GROUPS * SSD_STATE
HY_COLS = (HYENA_ORDER + 1) * D_HYENA
D_IN = HY_COLS + D_SSD + D_XBC + 2 * SSD_HEADS
LANES = 128
SUBLANES = 8
D_IN_PAD = -(-D_IN // LANES) * LANES

MOE_GROUPS = 8
EXPERTS_PER_GROUP = 8
N_EXPERTS = MOE_GROUPS * EXPERTS_PER_GROUP
TOP_K = 2
D_EXPERT = 512
MOE_BLOCK = 256
ROUTE_COLS = 8

VMEM_LIMIT_BYTES = 56 * 1024 * 1024
NEG_BIG = -1e30


def _conv3_rows(p, w_ref, b_ref, cols, has_prev, has_next):
    n = p.shape[0]
    prev = jnp.where(has_prev, pltpu.roll(p, 1, 0), 0.0)
    nxt = jnp.where(has_next, pltpu.roll(p, n - 1, 0), 0.0)
    return b_ref[:, cols] + w_ref[0:1, cols] * prev + w_ref[1:2, cols] * p + w_ref[2:3, cols] * nxt


def _in_proj_kernel(ctx_ref, x_ref, g_ref, csh_ref, csc_ref, sh_ref, sc_ref, w_ref, wdt_ref, hw_ref, hb_ref,
                    sw_ref, sb_ref, dtb_ref, dtm_ref, u_ref, z_ref, xbc_ref, dta_ref, h_ref,
                    *, n_ctx_steps, row_len, ctx_row_len, hy_cols, d_ssd, d_xbc, tn):
    i = pl.program_id(1)
    is_ctx = i < n_ctx_steps
    tm = x_ref.shape[1]
    xin = jnp.where(is_ctx, ctx_ref[0], x_ref[0])
    shift = jnp.where(is_ctx, csh_ref[...], sh_ref[0])
    scale = jnp.where(is_ctx, csc_ref[...], sc_ref[0])
    y = xin * lax.rsqrt(jnp.mean(xin * xin, axis=-1, keepdims=True) + EPS) * g_ref[...]
    h_ref[...] = (y * (1.0 + scale) + shift).astype(jnp.bfloat16)

    pos = lax.broadcasted_iota(jnp.int32, (tm, 1), 0) + jnp.where(is_ctx, i, i - n_ctx_steps) * tm
    in_row = jnp.where(is_ctx, pos % ctx_row_len, pos % row_len)
    has_prev = in_row != 0
    has_next = in_row != jnp.where(is_ctx, ctx_row_len - 1, row_len - 1)

    @pl.when(jnp.logical_not(is_ctx))
    def _():
        for c0 in range(0, hy_cols, tn):
            cols = slice(c0, c0 + tn)
            p = jnp.dot(h_ref[...], w_ref[:, cols], preferred_element_type=jnp.float32)
            u_ref[0, :, cols] = _conv3_rows(p, hw_ref, hb_ref, cols, has_prev, has_next)
        z_ref[0] = jnp.dot(h_ref[...], w_ref[:, hy_cols:hy_cols + d_ssd], preferred_element_type=jnp.float32)

    for c0 in range(0, d_xbc, tn):
        cols = slice(c0, c0 + tn)
        wc = slice(hy_cols + d_ssd + c0, hy_cols + d_ssd + c0 + tn)
        p = jnp.dot(h_ref[...], w_ref[:, wc], preferred_element_type=jnp.float32)
        v = _conv3_rows(p, sw_ref, sb_ref, cols, has_prev, has_next)
        xbc_ref[0, :, cols] = v * jax.nn.sigmoid(v)
    pd = jnp.dot(h_ref[...], wdt_ref[...], preferred_element_type=jnp.float32) + dtb_ref[...]
    sp = jnp.maximum(pd, 0.0) + jnp.log(1.0 + jnp.exp(-jnp.abs(pd)))
    dta_ref[0] = sp * dtm_ref[...]


def in_proj_fused(ctx, x, g1, csh, csc, sh, sc, w_bf, wdt_bf, hy_w, hy_b, ssd_w, ssd_b, dt_bias2, dt_mult,
                  row_len, tm, tn):
    bsz, L, D = x.shape
    lc = ctx.shape[1]
    hy_cols = hy_w.shape[1]
    d_xbc = ssd_w.shape[1]
    d_ssd = w_bf.shape[1] - hy_cols - d_xbc
    n_ctx_steps = lc // tm
    n_steps = n_ctx_steps + L // tm
    lat = lambda b, i: (b, jnp.maximum(i - n_ctx_steps, 0), 0)
    allt = lambda b, i: (b, i, 0)
    const2 = lambda b, i: (0, 0)
    per_b = pl.BlockSpec((1, 1, D), lambda b, i: (b, 0, 0))
    kern = functools.partial(_in_proj_kernel, n_ctx_steps=n_ctx_steps, row_len=row_len, ctx_row_len=lc,
                             hy_cols=hy_cols, d_ssd=d_ssd, d_xbc=d_xbc, tn=tn)
    return pl.pallas_call(
        kern,
        grid=(bsz, n_steps),
        in_specs=[
            pl.BlockSpec((1, tm, D), lambda b, i: (b, jnp.minimum(i, n_ctx_steps - 1), 0)),
            pl.BlockSpec((1, tm, D), lat),
            pl.BlockSpec((1, D), const2),
            pl.BlockSpec((1, D), const2),
            pl.BlockSpec((1, D), const2),
            per_b, per_b,
            pl.BlockSpec(w_bf.shape, const2),
            pl.BlockSpec(wdt_bf.shape, const2),
            pl.BlockSpec(hy_w.shape, const2),
            pl.BlockSpec(hy_b.shape, const2),
            pl.BlockSpec(ssd_w.shape, const2),
            pl.BlockSpec(ssd_b.shape, const2),
            pl.BlockSpec((1, LANES), const2),
            pl.BlockSpec((1, LANES), const2),
        ],
        out_specs=[
            pl.BlockSpec((1, tm, hy_cols), lat),
            pl.BlockSpec((1, tm, d_ssd), lat),
            pl.BlockSpec((1, tm, d_xbc), allt),
            pl.BlockSpec((1, tm, LANES), allt),
        ],
        out_shape=[
            jax.ShapeDtypeStruct((bsz, L, hy_cols), jnp.float32),
            jax.ShapeDtypeStruct((bsz, L, d_ssd), jnp.float32),
            jax.ShapeDtypeStruct((bsz, lc + L, d_xbc), jnp.float32),
            jax.ShapeDtypeStruct((bsz, lc + L, LANES), jnp.float32),
        ],
        scratch_shapes=[pltpu.VMEM((tm, D), jnp.bfloat16)],
        compiler_params=pltpu.CompilerParams(
            dimension_semantics=("arbitrary", "arbitrary"), vmem_limit_bytes=VMEM_LIMIT_BYTES),
    )(ctx, x, g1, csh, csc, sh, sc, w_bf, wdt_bf, hy_w, hy_b, ssd_w, ssd_b, dt_bias2, dt_mult)


def in_proj_params(w_in, a_log, dt_bias, hy_cols, d_ssd, d_xbc):
    n_h = 2 * SSD_HEADS
    main = hy_cols + d_ssd + d_xbc
    w_dt = w_in[:, main:main + n_h]
    pad = jnp.zeros((w_in.shape[0], LANES - 2 * n_h), w_in.dtype)
    wdt = jnp.concatenate([w_dt, w_dt, pad], axis=1).astype(jnp.bfloat16)
    zpad = jnp.zeros((LANES - 2 * n_h,), jnp.float32)
    bias2 = jnp.concatenate([dt_bias.reshape(n_h), dt_bias.reshape(n_h), zpad])[None, :]
    mult = jnp.concatenate([jnp.ones((n_h,), jnp.float32), -jnp.exp(a_log).reshape(n_h), zpad])[None, :]
    return w_in[:, :main].astype(jnp.bfloat16), wdt, bias2, mult


def _out_router_kernel(yh_ref, ys_ref, z_ref, x_ref, ga_ref, sc_ref, sh_ref, ng_ref, g2_ref, wo_ref, wr_ref, br_ref,
                       x1_ref, hn_ref, ri_ref, rf_ref, cnt_ref, carry_ref):
    first = jnp.logical_and(pl.program_id(0) == 0, pl.program_id(1) == 0)

    @pl.when(first)
    def _():
        carry_ref[...] = jnp.zeros_like(carry_ref)

    bf = jnp.bfloat16
    tm = x_ref.shape[1]
    dh = yh_ref.shape[2]
    z = z_ref[0]
    ys = ys_ref[0] * (z * jax.nn.sigmoid(z))
    gw = ys.shape[1] // SSD_GROUPS
    acc = jnp.dot(yh_ref[0].astype(bf), wo_ref[0:dh, :], preferred_element_type=jnp.float32)
    for g in range(SSD_GROUPS):
        yg = ys[:, g * gw:(g + 1) * gw]
        yg = yg * lax.rsqrt(jnp.mean(yg * yg, axis=-1, keepdims=True) + EPS) * ng_ref[:, g * gw:(g + 1) * gw]
        acc += jnp.dot(yg.astype(bf), wo_ref[dh + g * gw:dh + (g + 1) * gw, :], preferred_element_type=jnp.float32)
    x1 = x_ref[0] + ga_ref[0] * acc
    x1_ref[0] = x1
    hn = x1 * lax.rsqrt(jnp.mean(x1 * x1, axis=-1, keepdims=True) + EPS) * g2_ref[...]
    hn = hn * (1.0 + sc_ref[0]) + sh_ref[0]
    hn_ref[0] = hn

    hn_hi = hn.astype(bf)
    hn_lo = (hn - hn_hi.astype(jnp.float32)).astype(bf)
    logits = (jnp.dot(hn_hi, wr_ref[0], preferred_element_type=jnp.float32)
              + jnp.dot(hn_lo, wr_ref[0], preferred_element_type=jnp.float32)
              + jnp.dot(hn_hi, wr_ref[1], preferred_element_type=jnp.float32)) + br_ref[...]
    lane = lax.broadcasted_iota(jnp.int32, (tm, LANES), 1)
    lane_f = lane.astype(jnp.float32)
    ninf = jnp.float32(-jnp.inf)
    big = jnp.float32(1e9)
    gl = jnp.where(lane < MOE_GROUPS, logits, ninf)
    gmax = jnp.max(gl, axis=-1, keepdims=True)
    p_group = 1.0 / jnp.sum(jnp.exp(gl - gmax), axis=-1, keepdims=True)
    g_sel = jnp.min(jnp.where(gl == gmax, lane_f, big), axis=-1, keepdims=True)
    e_lane = lane - MOE_GROUPS
    in_grp = jnp.logical_and(e_lane >= 0, (e_lane // EXPERTS_PER_GROUP).astype(jnp.float32) == g_sel)
    el = jnp.where(in_grp, logits, ninf)
    m1 = jnp.max(el, axis=-1, keepdims=True)
    i1 = jnp.min(jnp.where(el == m1, lane_f, big), axis=-1, keepdims=True)
    el2 = jnp.where(lane_f == i1, ninf, el)
    m2 = jnp.max(el2, axis=-1, keepdims=True)
    i2 = jnp.min(jnp.where(el2 == m2, lane_f, big), axis=-1, keepdims=True)
    t = jnp.exp(m2 - m1)
    w1 = 1.0 / (1.0 + t)
    gate1 = w1 * p_group
    gate2 = (t * w1) * p_group
    e1 = i1 - MOE_GROUPS
    e2 = i2 - MOE_GROUPS
    el_f = e_lane.astype(jnp.float32)
    oh1 = el_f == e1
    oh2 = el_f == e2
    oh = jnp.logical_or(oh1, oh2).astype(bf)
    r_i = lax.broadcasted_iota(jnp.int32, (tm, tm), 0)
    c_i = lax.broadcasted_iota(jnp.int32, (tm, tm), 1)
    before = jnp.dot((c_i < r_i).astype(bf), oh, preferred_element_type=jnp.float32) + carry_ref[...]
    rank1 = jnp.sum(jnp.where(oh1, before, 0.0), axis=-1, keepdims=True)
    rank2 = jnp.sum(jnp.where(oh2, before, 0.0), axis=-1, keepdims=True)
    carry_ref[...] += jnp.sum(oh.astype(jnp.float32), axis=0, keepdims=True)
    cnt_ref[...] = carry_ref[...]

    rec = jnp.where(lane == 0, e1, jnp.where(lane == 1, e2, jnp.where(lane == 2, rank1,
                                                                      jnp.where(lane == 3, rank2, 0.0))))
    ri_ref[0] = rec.T[0:ROUTE_COLS, :].astype(jnp.int32)
    col = lax.broadcasted_iota(jnp.int32, (tm, ROUTE_COLS), 1)
    rf_ref[0] = jnp.where(col == 0, gate1, gate2)


def out_proj_router(y_hy, y_scan, px, z_col, x, ga1, sc2, sh2, norm_g, g2, w_out_bf, w_router, b_router, tm):
    bsz, L, D = x.shape
    dh = y_hy.shape[-1]
    ds = y_scan.shape[-1]
    tok = lambda b, i: (b, i, 0)
    per_b = pl.BlockSpec((1, 1, D), lambda b, i: (b, 0, 0))
    const2 = lambda b, i: (0, 0)
    return pl.pallas_call(
        _out_router_kernel,
        grid=(bsz, L // tm),
        in_specs=[
            pl.BlockSpec((1, tm, dh), tok),
            pl.BlockSpec((1, tm, ds), tok),
            pl.BlockSpec((1, tm, ds), lambda b, i: (b, i, z_col)),
            pl.BlockSpec((1, tm, D), tok),
            per_b, per_b, per_b,
            pl.BlockSpec((1, ds), const2),
            pl.BlockSpec((1, D), const2),
            pl.BlockSpec((dh + ds, D), const2),
            pl.BlockSpec((2, D, LANES), lambda b, i: (0, 0, 0)),
            pl.BlockSpec((1, LANES), const2),
        ],
        out_specs=[
            pl.BlockSpec((1, tm, D), tok),
            pl.BlockSpec((1, tm, D), tok),
            pl.BlockSpec((1, ROUTE_COLS, tm), lambda b, i: (b, 0, i)),
            pl.BlockSpec((1, tm, ROUTE_COLS), tok),
            pl.BlockSpec((1, LANES), const2),
        ],
        out_shape=[
            jax.ShapeDtypeStruct((bsz, L, D), jnp.float32),
            jax.ShapeDtypeStruct((bsz, L, D), jnp.float32),
            jax.ShapeDtypeStruct((bsz, ROUTE_COLS, L), jnp.int32),
            jax.ShapeDtypeStruct((bsz, L, ROUTE_COLS), jnp.float32),
            jax.ShapeDtypeStruct((1, LANES), jnp.float32),
        ],
        scratch_shapes=[pltpu.VMEM((1, LANES), jnp.float32)],
        compiler_params=pltpu.CompilerParams(
            dimension_semantics=("arbitrary", "arbitrary"), vmem_limit_bytes=VMEM_LIMIT_BYTES),
    )(y_hy, y_scan, px, x, ga1, sc2, sh2, norm_g, g2, w_out_bf, w_router, b_router)


def _row_copy(src_hbm, src_row, dst_ref, dst_row, sem):
    return pltpu.make_async_copy(src_hbm.at[pl.ds(src_row, 1), :], dst_ref.at[pl.ds(dst_row, 1), :], sem)


def _dispatch_kernel(dest_ref, valid_ref, hn_ref, buf_hbm, zeros, sem, zsem):
    step = pl.program_id(0)
    tm = hn_ref.shape[0]
    n_tok = pl.num_programs(0) * tm
    blk = zeros.shape[0]
    n_blocks = buf_hbm.shape[0] // blk

    def zero_copy(i):
        return pltpu.make_async_copy(zeros, buf_hbm.at[pl.ds(pl.multiple_of(i * blk, blk), blk), :], zsem)

    def zfill(i, carry):
        @pl.when(valid_ref[i] < blk)
        def _():
            zero_copy(i).start()
        return carry

    def zwait(i, carry):
        @pl.when(valid_ref[i] < blk)
        def _():
            zero_copy(i).wait()
        return carry

    @pl.when(step == 0)
    def _():
        zeros[...] = jnp.zeros_like(zeros)
        lax.fori_loop(0, n_blocks, zfill, 0)
        lax.fori_loop(0, n_blocks, zwait, 0)

    def body(j, carry):
        t = step * tm + j
        _row_copy(hn_ref, j, buf_hbm, dest_ref[t], sem).start()
        _row_copy(hn_ref, j, buf_hbm, dest_ref[n_tok + t], sem).start()
        return carry

    lax.fori_loop(0, tm, body, 0, unroll=8)
    for _ in range(2):
        pltpu.make_async_copy(hn_ref, buf_hbm.at[pl.ds(0, tm), :], sem).wait()


def moe_dispatch(hn, dest, block_valid, blk, tm):
    T, D = hn.shape
    n_rows = block_valid.shape[0] * blk
    grid_spec = pltpu.PrefetchScalarGridSpec(
        num_scalar_prefetch=2,
        grid=(T // tm,),
        in_specs=[pl.BlockSpec((tm, D), lambda i, d, v: (i, 0))],
        out_specs=pl.BlockSpec(memory_space=pl.ANY),
        scratch_shapes=[pltpu.VMEM((blk, D), jnp.float32), pltpu.SemaphoreType.DMA(()),
                        pltpu.SemaphoreType.DMA(())],
    )
    return pl.pallas_call(
        _dispatch_kernel,
        grid_spec=grid_spec,
        out_shape=jax.ShapeDtypeStruct((n_rows, D), jnp.float32),
        compiler_params=pltpu.CompilerParams(dimension_semantics=("arbitrary",), has_side_effects=True),
    )(dest, block_valid, hn)


def _expert_kernel(eid_ref, first_ref, valid_ref, x_ref, w1_ref, w3_ref, w2_ref, o_ref, w1b, w3b, w2b):
    i = pl.program_id(0)
    del eid_ref
    bf = jnp.bfloat16

    @pl.when(first_ref[i] == 1)
    def _():
        w1b[...] = w1_ref[0].astype(bf)
        w3b[...] = w3_ref[0].astype(bf)
        w2b[...] = w2_ref[0].astype(bf)

    valid = valid_ref[i]

    @pl.when(valid > 0)
    def _():
        xb = x_ref[...].astype(bf)
        a = jnp.dot(xb, w1b[...], preferred_element_type=jnp.float32)
        b = jnp.dot(xb, w3b[...], preferred_element_type=jnp.float32)
        h = (a * jax.nn.sigmoid(a)) * b
        o_ref[...] = jnp.dot(h.astype(bf), w2b[...], preferred_element_type=jnp.float32)

    @pl.when(valid <= 0)
    def _():
        o_ref[...] = jnp.zeros_like(o_ref)


def expert_blocks(buf, block_eid, block_first, block_valid, w1, w3, w2, blk):
    rows, D = buf.shape
    n_blocks = rows // blk
    E, _, F = w1.shape
    grid_spec = pltpu.PrefetchScalarGridSpec(
        num_scalar_prefetch=3,
        grid=(n_blocks,),
        in_specs=[
            pl.BlockSpec((blk, D), lambda i, eid, fi, va: (i, 0)),
            pl.BlockSpec((1, D, F), lambda i, eid, fi, va: (eid[i], 0, 0)),
            pl.BlockSpec((1, D, F), lambda i, eid, fi, va: (eid[i], 0, 0)),
            pl.BlockSpec((1, F, D), lambda i, eid, fi, va: (eid[i], 0, 0)),
        ],
        out_specs=pl.BlockSpec((blk, D), lambda i, eid, fi, va: (i, 0)),
        scratch_shapes=[pltpu.VMEM((D, F), jnp.bfloat16), pltpu.VMEM((D, F), jnp.bfloat16),
                        pltpu.VMEM((F, D), jnp.bfloat16)],
    )
    return pl.pallas_call(
        _expert_kernel,
        grid_spec=grid_spec,
        out_shape=jax.ShapeDtypeStruct((rows, D), jnp.float32),
        compiler_params=pltpu.CompilerParams(
            dimension_semantics=("arbitrary",), vmem_limit_bytes=VMEM_LIMIT_BYTES),
    )(block_eid, block_first, block_valid, buf, w1, w3, w2)


def _combine_kernel(dest_ref, x1_ref, rf_ref, ga_ref, gf_ref, yb_hbm, o_ref, ybuf, sem):
    b = pl.program_id(0)
    i = pl.program_id(1)
    n_i = pl.num_programs(1)
    tm = x1_ref.shape[1]
    step = b * n_i + i
    n_steps = pl.num_programs(0) * n_i
    slot = step % 2

    def issue(step_, slot_):
        def body(j, carry):
            t = step_ * tm + j
            _row_copy(yb_hbm, dest_ref[t], ybuf.at[slot_, 0], j, sem.at[slot_]).start()
            _row_copy(yb_hbm, dest_ref[n_steps * tm + t], ybuf.at[slot_, 1], j, sem.at[slot_]).start()
            return carry
        lax.fori_loop(0, tm, body, 0, unroll=8)

    @pl.when(step == 0)
    def _():
        issue(0, 0)

    @pl.when(step + 1 < n_steps)
    def _():
        issue(step + 1, 1 - slot)

    pltpu.make_async_copy(yb_hbm.at[pl.ds(0, tm), :], ybuf.at[slot, 0], sem.at[slot]).wait()
    pltpu.make_async_copy(yb_hbm.at[pl.ds(0, tm), :], ybuf.at[slot, 1], sem.at[slot]).wait()
    rf = rf_ref[0]
    y = rf[:, 0:1] * ybuf[slot, 0] + rf[:, 1:2] * ybuf[slot, 1]
    x2 = x1_ref[0] + ga_ref[0] * y
    o_ref[0] = x2 * lax.rsqrt(jnp.mean(x2 * x2, axis=-1, keepdims=True) + EPS) * gf_ref[...]


def moe_combine(x1, route_f, ga2, g_final, yb, dest, tm):
    bsz, L, D = x1.shape
    grid_spec = pltpu.PrefetchScalarGridSpec(
        num_scalar_prefetch=1,
        grid=(bsz, L // tm),
        in_specs=[
            pl.BlockSpec((1, tm, D), lambda b, i, d: (b, i, 0)),
            pl.BlockSpec((1, tm, ROUTE_COLS), lambda b, i, d: (b, i, 0)),
            pl.BlockSpec((1, 1, D), lambda b, i, d: (b, 0, 0)),
            pl.BlockSpec((1, D), lambda b, i, d: (0, 0)),
            pl.BlockSpec(memory_space=pl.ANY),
        ],
        out_specs=pl.BlockSpec((1, tm, D), lambda b, i, d: (b, i, 0)),
        scratch_shapes=[pltpu.VMEM((2, 2, tm, D), jnp.float32), pltpu.SemaphoreType.DMA((2,))],
    )
    return pl.pallas_call(
        _combine_kernel,
        grid_spec=grid_spec,
        out_shape=jax.ShapeDtypeStruct((bsz, L, D), jnp.float32),
        compiler_params=pltpu.CompilerParams(
            dimension_semantics=("arbitrary", "arbitrary"), vmem_limit_bytes=VMEM_LIMIT_BYTES),
    )(dest, x1, route_f, ga2, g_final, yb)


def moe_plan(route_i, counts, blk, n_blocks):
    cnt = counts[0, MOE_GROUPS:MOE_GROUPS + N_EXPERTS].astype(jnp.int32)
    padded = (cnt + blk - 1) // blk * blk
    ends = jnp.cumsum(padded)
    starts = ends - padded
    experts = jnp.arange(N_EXPERTS, dtype=jnp.int32)
    dest = jnp.concatenate([
        (jnp.sum(jnp.where(route_i[:, k, :, None] == experts, starts, 0), axis=-1) + route_i[:, 2 + k]).reshape(-1)
        for k in range(TOP_K)])
    first_row = jnp.arange(n_blocks, dtype=jnp.int32) * blk
    block_eid = jnp.minimum(jnp.sum((ends[None, :] <= first_row[:, None]).astype(jnp.int32), axis=1), N_EXPERTS - 1)
    block_valid = jnp.clip(cnt[block_eid] - (first_row - starts[block_eid]), 0, blk).astype(jnp.int32)
    block_first = jnp.concatenate([jnp.ones((1,), jnp.int32),
                                   (block_eid[1:] != block_eid[:-1]).astype(jnp.int32)])
    return dest, block_eid, block_first, block_valid


def dft_tables(L):
    n = 2 * L
    f = lax.broadcasted_iota(jnp.int32, (L, L), 0)
    t = lax.broadcasted_iota(jnp.int32, (L, L), 1)
    ang = ((f * t) % n).astype(jnp.float32) * (2.0 * math.pi / n)
    return jnp.cos(ang).astype(jnp.bfloat16), jnp.sin(ang).astype(jnp.bfloat16)


def _alt_sign(L):
    t = lax.broadcasted_iota(jnp.int32, (L, 1), 0)
    return (1 - 2 * (t & 1)).astype(jnp.float32)


def _spectrum_kernel(a_ref, b_ref, c_ref, s_ref, kr_ref, ks_ref, kn_ref):
    L = a_ref.shape[1]
    a = a_ref[0]
    row = lax.broadcasted_iota(jnp.int32, (L, 1), 0)
    scale = jnp.where(row == 0, 0.5 / L, 1.0 / L)
    kr_ref[0] = scale * jnp.dot(c_ref[...], a.astype(jnp.bfloat16), preferred_element_type=jnp.float32)
    ks_ref[0] = scale * jnp.dot(s_ref[...], b_ref[0].astype(jnp.bfloat16), preferred_element_type=jnp.float32)
    kn_ref[0] = jnp.sum(a * _alt_sign(L), axis=0, keepdims=True) * (0.5 / L)


def filter_spectrum(a, b, cos_t, sin_t, tc):
    n, L, C = a.shape
    blk = pl.BlockSpec((1, L, tc), lambda o, j: (o, 0, j))
    tab = pl.BlockSpec((L, L), lambda o, j: (0, 0))
    return pl.pallas_call(
        _spectrum_kernel,
        grid=(n, C // tc),
        in_specs=[blk, blk, tab, tab],
        out_specs=[blk, blk, pl.BlockSpec((1, 1, tc), lambda o, j: (o, 0, j))],
        out_shape=[jax.ShapeDtypeStruct((n, L, C), jnp.float32)] * 2 + [jax.ShapeDtypeStruct((n, 1, C), jnp.float32)],
        compiler_params=pltpu.CompilerParams(
            dimension_semantics=("arbitrary", "arbitrary"), vmem_limit_bytes=VMEM_LIMIT_BYTES),
    )(a, b, cos_t, sin_t)


def _long_conv_kernel(*refs, tf, n_slab):
    z_refs = refs[:n_slab]
    xn_refs = refs[n_slab:2 * n_slab]
    kr_ref, ks_ref, kn_ref, bias_ref, c_ref, s_ref, o_ref, acc_ref, stage_ref = refs[2 * n_slab:]
    H = z_refs[0].shape[1] // 2
    f32 = jnp.float32
    bf = jnp.bfloat16
    sign = _alt_sign(H)

    def phase(slab_refs, p):
        return jnp.concatenate([r[0, pl.ds(p, H, stride=2), :] for r in slab_refs], axis=1)

    z2 = [phase(z_refs, p) for p in range(2)]
    zb = [z.astype(bf) for z in z2]
    z_nyq = [jnp.sum(z * sign, axis=0, keepdims=True) for z in z2]
    filt = ((0, 2), (1, 0))
    for p in range(2):
        nyq = z_nyq[0] * kn_ref[filt[p][0]] + z_nyq[1] * kn_ref[filt[p][1]]
        acc_ref[p] = z2[p] * bias_ref[0] + sign * nyq
    for ft in range(H // tf):
        rows = pl.ds(ft * tf, tf)
        zr = [jnp.dot(c_ref[rows, :], zb[q], preferred_element_type=f32) for q in range(2)]
        zs = [jnp.dot(s_ref[rows, :], zb[q], preferred_element_type=f32) for q in range(2)]
        for p in range(2):
            yr = 0.0
            ys = 0.0
            for q in range(2):
                kr = kr_ref[filt[p][q], rows, :]
                ks = ks_ref[filt[p][q], rows, :]
                yr = yr + zr[q] * kr - zs[q] * ks
                ys = ys + zr[q] * ks + zs[q] * kr
            acc_ref[p] += (jnp.dot(c_ref[:, rows], yr.astype(bf), preferred_element_type=f32)
                           + jnp.dot(s_ref[:, rows], ys.astype(bf), preferred_element_type=f32))
    for p in range(2):
        out_p = phase(xn_refs, p) * acc_ref[p]
        for sl in range(n_slab):
            stage_ref[sl, pl.ds(p, H, stride=2), :] = out_p[:, sl * LANES:(sl + 1) * LANES]
    for sl in range(n_slab):
        o_ref[0, :, sl * LANES:(sl + 1) * LANES] = stage_ref[sl]


def long_conv_gate(z_arr, z_col, xn_arr, xn_col, kr, ks, kn, bias, cos_t, sin_t, tc, tf):
    bsz, L, _ = z_arr.shape
    H = L // 2
    C = kr.shape[-1]
    nj = C // tc
    n_slab = tc // LANES
    tab = pl.BlockSpec((H, H), lambda j, b: (0, 0), pipeline_mode=pl.Buffered(1))
    spec = pl.BlockSpec((3, H, tc), lambda j, b: (0, 0, j), pipeline_mode=pl.Buffered(1))
    nyq = pl.BlockSpec((3, 1, tc), lambda j, b: (0, 0, j))
    vec = pl.BlockSpec((1, 1, tc), lambda j, b: (0, 0, j))

    def slabs(col):
        return [pl.BlockSpec((1, L, LANES),
                             functools.partial(lambda j, b, sl: (b, 0, (col * nj + j) * n_slab + sl), sl=sl))
                for sl in range(n_slab)]

    return pl.pallas_call(
        functools.partial(_long_conv_kernel, tf=tf, n_slab=n_slab),
        grid=(nj, bsz),
        in_specs=slabs(z_col) + slabs(xn_col) + [spec, spec, nyq, vec, tab, tab],
        out_specs=pl.BlockSpec((1, L, tc), lambda j, b: (b, 0, j)),
        out_shape=jax.ShapeDtypeStruct((bsz, L, C), jnp.float32),
        scratch_shapes=[pltpu.VMEM((2, H, tc), jnp.float32), pltpu.VMEM((n_slab, L, LANES), jnp.float32)],
        compiler_params=pltpu.CompilerParams(
            dimension_semantics=("arbitrary", "arbitrary"), vmem_limit_bytes=VMEM_LIMIT_BYTES),
    )(*([z_arr] * n_slab), *([xn_arr] * n_slab), kr, ks, kn, bias, cos_t, sin_t)


def _polyphase_taps(fe, fo, be, bo):
    zero = jnp.zeros_like(fe[:1])
    plus = jnp.stack([fe, fo, jnp.concatenate([bo[0:1], fo[:-1]], axis=0)])
    minus = jnp.stack([jnp.concatenate([zero, be[1:]], axis=0),
                       jnp.concatenate([zero, bo[:-1]], axis=0),
                       jnp.concatenate([zero, bo[1:]], axis=0)])
    return plus, minus


def hyena_long_convs(u, kp, h_bias, tc, tf):
    L = u.shape[1]
    H = L // 2
    C = h_bias.shape[1]
    cos_t, sin_t = dft_tables(H)
    z = u
    for o in range(h_bias.shape[0]):
        fwd = slice((2 * o) * C, (2 * o + 1) * C)
        bwd = slice((2 * o + 1) * C, (2 * o + 2) * C)
        plus, minus = _polyphase_taps(kp[:H, fwd], kp[H:, fwd], kp[:H, bwd], kp[H:, bwd])
        kr, ks, kn = filter_spectrum(plus + minus, plus - minus, cos_t, sin_t, tc)
        z = long_conv_gate(z, 0, u, o + 1, kr, ks, kn, h_bias[o][None, None, :], cos_t, sin_t, tc, tf)
    return z


def _filter_kernel(band_ref, w1_ref, b1_ref, fr_ref, w2_ref, b2_ref, w3_ref, dl_ref, k_ref, *, seq_len):
    hp = lax.Precision.HIGHEST
    f32 = jnp.float32
    tp = k_ref.shape[0]
    half = seq_len // 2
    g = lax.broadcasted_iota(jnp.int32, (tp, 1), 0) + pl.program_id(0) * tp
    odd = (g >= half).astype(jnp.int32)
    pos = (2 * (g - odd * half) + odd).astype(f32)
    t = pos / max(seq_len - 1, 1)
    ang = (2 * math.pi / seq_len) * pos * band_ref[...]
    lane = lax.broadcasted_iota(jnp.int32, (tp, LANES), 1)
    feats = jnp.where(lane == 0, t,
                      jnp.where(lane <= HYENA_BANDS, jnp.cos(ang),
                                jnp.where(lane <= 2 * HYENA_BANDS, -jnp.sin(ang), 0.0)))
    h = jnp.sin(fr_ref[...] * (jnp.dot(feats, w1_ref[...], precision=hp, preferred_element_type=f32) + b1_ref[...]))
    h = jnp.sin(fr_ref[...] * (jnp.dot(h, w2_ref[...], precision=hp, preferred_element_type=f32) + b2_ref[...]))
    window = jnp.exp(-t * dl_ref[...])
    c = dl_ref.shape[1]
    for j in range(w3_ref.shape[1] // c):
        cols = slice(j * c, (j + 1) * c)
        k_ref[:, cols] = jnp.dot(h, w3_ref[:, cols], precision=hp, preferred_element_type=f32) * window


def hyena_filters_polyphase(seq_len, f_w1, f_b1, f_freq, f_w2, f_b2, f_w3, d_hyena, tp):
    f32 = jnp.float32
    fh = f_w1.shape[1]
    n_emb = 1 + 2 * HYENA_BANDS
    bands = jnp.linspace(1e-4, HYENA_BANDS - 1, HYENA_BANDS, dtype=f32)
    band_row = jnp.concatenate([jnp.zeros((1,), f32), bands, bands, jnp.zeros((LANES - n_emb,), f32)])[None, :]
    w1p = jnp.concatenate([f_w1, jnp.zeros((LANES - n_emb, fh), f32)], axis=0)
    deltas = jnp.abs(jnp.linspace(math.log(HYENA_TARGET) / HYENA_SLOW_DECAY,
                                  math.log(HYENA_TARGET) / HYENA_FAST_DECAY, d_hyena, dtype=f32))[None, :]
    n_out = f_w3.shape[1]
    full = lambda a: pl.BlockSpec(a.shape, lambda i: (0,) * a.ndim)
    args = (band_row, w1p, f_b1[None, :], f_freq[None, :], f_w2, f_b2[None, :], f_w3, deltas)
    return pl.pallas_call(
        functools.partial(_filter_kernel, seq_len=seq_len),
        grid=(seq_len // tp,),
        in_specs=[full(a) for a in args],
        out_specs=pl.BlockSpec((tp, n_out), lambda i: (i, 0)),
        out_shape=jax.ShapeDtypeStruct((seq_len, n_out), f32),
        compiler_params=pltpu.CompilerParams(dimension_semantics=("arbitrary",), vmem_limit_bytes=VMEM_LIMIT_BYTES),
    )(*args)


def _ada_kernel(c_ref, w_ref, b_ref, o_ref):
    cv = c_ref[...]
    s = cv * jax.nn.sigmoid(cv)
    o_ref[...] = jnp.dot(s, w_ref[...], precision=lax.Precision.HIGHEST,
                         preferred_element_type=jnp.float32) + b_ref[...]


def ada_modulation(c_rows, w_ada, b_ada, tn):
    rows, D = c_rows.shape
    N = w_ada.shape[1]
    return pl.pallas_call(
        _ada_kernel,
        grid=(N // tn,),
        in_specs=[pl.BlockSpec((rows, D), lambda j: (0, 0)),
                  pl.BlockSpec((D, tn), lambda j: (0, j)),
                  pl.BlockSpec((1, tn), lambda j: (0, j))],
        out_specs=pl.BlockSpec((rows, tn), lambda j: (0, j)),
        out_shape=jax.ShapeDtypeStruct((rows, N), jnp.float32),
        compiler_params=pltpu.CompilerParams(dimension_semantics=("arbitrary",), vmem_limit_bytes=VMEM_LIMIT_BYTES),
    )(c_rows, w_ada, b_ada[None, :])


def _ssd_kernel(xf_ref, df_ref, xb_ref, db_ref, dskip_ref, y_ref, h_ref, *, n_ctx_chunks):
    s = pl.program_id(1)
    n_steps = pl.num_programs(1)
    Q, G, R, P, N = SSD_CHUNK, SSD_GROUPS, SSD_HPG, SSD_HEAD_DIM, SSD_STATE
    GP = R * P
    bf = jnp.bfloat16

    @pl.when(s == 0)
    def _():
        h_ref[...] = jnp.zeros_like(h_ref)
        y_ref[...] = jnp.zeros_like(y_ref)

    row = lax.broadcasted_iota(jnp.int32, (Q, Q), 0)
    col = lax.broadcasted_iota(jnp.int32, (Q, Q), 1)
    lane_head = lax.broadcasted_iota(jnp.int32, (Q, GP), 1) // P
    is_latent = s >= n_ctx_chunks
    n_lat = n_steps - n_ctx_chunks
    out_chunk = (jnp.clip(s - n_ctx_chunks, 0, n_lat - 1), jnp.clip(n_steps - 1 - s, 0, n_lat - 1))

    for d, (x_ref, da_ref) in enumerate(((xf_ref, df_ref), (xb_ref, db_ref))):
        mask = (row >= col) if d == 0 else (col >= row)
        tri = mask.astype(jnp.float32)
        da = da_ref[0]
        cum = jnp.dot(tri, da, precision=lax.Precision.HIGHEST, preferred_element_type=jnp.float32)
        cum_t = cum.T
        edge = Q - 1 if d == 0 else 0
        blk = x_ref.at[0]
        for g in range(G):
            xg = blk[:, g * GP:(g + 1) * GP]
            bg = blk[:, D_SSD + g * N:D_SSD + (g + 1) * N].astype(bf)
            cg = blk[:, D_SSD + G * N + g * N:D_SSD + G * N + (g + 1) * N].astype(bf)
            heads = [d * SSD_HEADS + g * R + r for r in range(R)]
            dtm = jnp.zeros((Q, GP), jnp.float32)
            cumm = jnp.zeros((Q, GP), jnp.float32)
            totm = jnp.zeros((Q, GP), jnp.float32)
            for r, h in enumerate(heads):
                sel = lane_head == r
                dtm = jnp.where(sel, da[:, h:h + 1], dtm)
                cumm = jnp.where(sel, cum[:, SSD_HEADS * 2 + h:SSD_HEADS * 2 + h + 1], cumm)
                totm = jnp.where(sel, cum[edge:edge + 1, SSD_HEADS * 2 + h:SSD_HEADS * 2 + h + 1], totm)
            xdt = xg * dtm
            hg = h_ref[d, g * GP:(g + 1) * GP, :]

            gmat = lax.dot_general(cg, bg, (((1,), (1,)), ((), ())), preferred_element_type=jnp.float32)
            y_off = lax.dot_general(cg, hg.astype(bf), (((1,), (1,)), ((), ())),
                                    preferred_element_type=jnp.float32) * jnp.exp(cumm)
            if d == 0:
                y_off = y_off + dskip_ref[:, g * GP:(g + 1) * GP] * xg
            parts = []
            for r, h in enumerate(heads):
                a_col = cum[:, SSD_HEADS * 2 + h:SSD_HEADS * 2 + h + 1]
                a_row = cum_t[SSD_HEADS * 2 + h:SSD_HEADS * 2 + h + 1, :]
                decay = jnp.exp(jnp.where(mask, a_col - a_row, NEG_BIG))
                parts.append(jnp.dot((gmat * decay).astype(bf), xdt[:, r * P:(r + 1) * P].astype(bf),
                                     preferred_element_type=jnp.float32))
            y = jnp.where(is_latent, y_off + jnp.concatenate(parts, axis=-1), 0.0)
            rows = pl.ds(pl.multiple_of(out_chunk[d] * Q, Q), Q)
            y_ref[0, rows, g * GP:(g + 1) * GP] += y

            xw = (xdt * jnp.exp(totm - cumm)).astype(bf)
            st = lax.dot_general(xw, bg, (((0,), (0,)), ((), ())), preferred_element_type=jnp.float32)
            for r, h in enumerate(heads):
                dec = jnp.exp(cum_t[SSD_HEADS * 2 + h:SSD_HEADS * 2 + h + 1, edge:edge + 1])
                rs = slice(g * GP + r * P, g * GP + (r + 1) * P)
                h_ref[d, rs, :] = h_ref[d, rs, :] * dec + st[r * P:(r + 1) * P, :]


def ssd_scan_bidir(xbc, dta, d_skip, n_ctx):
    bsz, lt, width = xbc.shape
    Q = SSD_CHUNK
    n_steps = lt // Q
    n_ctx_chunks = n_ctx // Q
    L = lt - n_ctx

    def bwd_chunk(s):
        return jnp.where(s < n_ctx_chunks, n_ctx_chunks - 1 - s, n_steps - 1 - s + n_ctx_chunks)

    return pl.pallas_call(
        functools.partial(_ssd_kernel, n_ctx_chunks=n_ctx_chunks),
        grid=(bsz, n_steps),
        in_specs=[
            pl.BlockSpec((1, Q, width), lambda b, s: (b, s, 0)),
            pl.BlockSpec((1, Q, LANES), lambda b, s: (b, s, 0)),
            pl.BlockSpec((1, Q, width), lambda b, s: (b, bwd_chunk(s), 0)),
            pl.BlockSpec((1, Q, LANES), lambda b, s: (b, bwd_chunk(s), 0)),
            pl.BlockSpec((1, D_SSD), lambda b, s: (0, 0)),
        ],
        out_specs=pl.BlockSpec((1, L, D_SSD), lambda b, s: (b, 0, 0)),
        out_shape=jax.ShapeDtypeStruct((bsz, L, D_SSD), jnp.float32),
        scratch_shapes=[pltpu.VMEM((2, SSD_GROUPS * SSD_HPG * SSD_HEAD_DIM, SSD_STATE), jnp.float32)],
        compiler_params=pltpu.CompilerParams(
            dimension_semantics=("arbitrary", "arbitrary"), vmem_limit_bytes=VMEM_LIMIT_BYTES),
    )(xbc, dta, xbc, dta, d_skip)


def kernel(x, c, ctx, c_ctx, w_ada, b_ada, g_norm1, g_norm2, w_in, hy_conv_w, hy_conv_b, hy_f_w1, hy_f_b1, hy_f_freq, hy_f_w2, hy_f_b2, hy_f_w3, hy_bias, ssd_conv_w, ssd_conv_b, ssd_a_log, ssd_dt_bias, ssd_d, ssd_norm_g, w_out, w_group, b_group, w_expert, b_expert, w1, w3, w2, g_final):
    bsz, seq_len, _ = x.shape
    l = 0
    rows_pad = -(bsz + 1) % SUBLANES
    c_rows = jnp.concatenate([c, c_ctx[None, :], jnp.zeros((rows_pad, D_MODEL), jnp.float32)], axis=0)
    mod_all = ada_modulation(c_rows, w_ada[l], b_ada[l], 512)
    sh1, sc1, ga1, sh2, sc2, ga2 = jnp.split(mod_all[:bsz, None, :], 6, axis=-1)
    csh1, csc1 = mod_all[bsz, :D_MODEL], mod_all[bsz, D_MODEL:2 * D_MODEL]

    w_out_bf = w_out[l].astype(jnp.bfloat16)
    w_in_bf, w_dt_bf, dt_bias2, dt_mult = in_proj_params(w_in[l], ssd_a_log[l], ssd_dt_bias[l],
                                                         HY_COLS, D_SSD, D_XBC)

    u, z, xbc, dta = in_proj_fused(ctx, x, g_norm1[l][None, :], csh1[None, :], csc1[None, :], sh1, sc1,
                                   w_in_bf, w_dt_bf, hy_conv_w[l], hy_conv_b[l][None, :],
                                   ssd_conv_w[l], ssd_conv_b[l][None, :], dt_bias2, dt_mult, GRID_W, 256, 512)
    kp = hyena_filters_polyphase(seq_len, hy_f_w1[l], hy_f_b1[l], hy_f_freq[l], hy_f_w2[l], hy_f_b2[l],
                                 hy_f_w3[l], D_HYENA, 256)
    y_hy = hyena_long_convs(u, kp, hy_bias[l], 256, 512)
    y_scan = ssd_scan_bidir(xbc, dta, jnp.repeat(ssd_d[l], SSD_HEAD_DIM)[None, :], CTX_LEN)

    pad = LANES - MOE_GROUPS - N_EXPERTS
    w_router = jnp.concatenate([w_group[l], w_expert[l], jnp.zeros((D_MODEL, pad), jnp.float32)], axis=1)
    w_router_hi = w_router.astype(jnp.bfloat16)
    w_router_lo = (w_router - w_router_hi.astype(jnp.float32)).astype(jnp.bfloat16)
    w_router = jnp.stack([w_router_hi, w_router_lo])
    b_router = jnp.concatenate([b_group[l], b_expert[l], jnp.zeros((pad,), jnp.float32)])[None, :]
    x1, hn, route_i, route_f, counts = out_proj_router(
        y_hy, y_scan, z, 0, x, ga1, sc2, sh2, ssd_norm_g[l][None, :], g_norm2[l][None, :],
        w_out_bf, w_router, b_router, 256)
    n_tok = bsz * seq_len
    n_blocks = -(-n_tok * TOP_K // MOE_BLOCK) + N_EXPERTS
    dest, block_eid, block_first, block_valid = moe_plan(route_i, counts, MOE_BLOCK, n_blocks)
    buf = moe_dispatch(hn.reshape(n_tok, D_MODEL), dest, block_valid, MOE_BLOCK, 256)
    yb = expert_blocks(buf, block_eid, block_first, block_valid, w1[l], w3[l], w2[l], MOE_BLOCK)
    return moe_combine(x1, route_f, ga2, g_final[None, :], yb, dest, 256)
```

```python
import functools
import math

import jax
import jax.numpy as jnp
from jax import lax
from jax.experimental import pallas as pl
from jax.experimental.pallas import tpu as pltpu

D_MODEL = 1024
CTX_LEN = 256
GRID_W = 64
EPS = 1e-6
SHORT_CONV = 3

D_HYENA = D_MODEL // 2
HYENA_ORDER = 2
HYENA_BANDS = 8
HYENA_FAST_DECAY = 0.3
HYENA_SLOW_DECAY = 1.5
HYENA_TARGET = 1e-2
HYENA_PHASES = 4

D_SSD = D_MODEL // 2
SSD_HEAD_DIM = 64
SSD_HEADS = D_SSD // SSD_HEAD_DIM
SSD_GROUPS = 2
SSD_HPG = SSD_HEADS // SSD_GROUPS
SSD_STATE = 128
SSD_CHUNK = 128

D_XBC = D_SSD + 2 * SSD_GROUPS * SSD_STATE
HY_COLS = (HYENA_ORDER + 1) * D_HYENA
D_IN = HY_COLS + D_SSD + D_XBC + 2 * SSD_HEADS
LANES = 128
SUBLANES = 8
D_IN_PAD = -(-D_IN // LANES) * LANES

MOE_GROUPS = 8
EXPERTS_PER_GROUP = 8
N_EXPERTS = MOE_GROUPS * EXPERTS_PER_GROUP
TOP_K = 2
D_EXPERT = 512
MOE_BLOCK = 256
ROUTE_COLS = 8

VMEM_LIMIT_BYTES = 56 * 1024 * 1024
NEG_BIG = -1e30


def _conv3_rows(p, w_ref, b_ref, cols, has_prev, has_next):
    n = p.shape[0]
    prev = jnp.where(has_prev, pltpu.roll(p, 1, 0), 0.0)
    nxt = jnp.where(has_next, pltpu.roll(p, n - 1, 0), 0.0)
    return b_ref[:, cols] + w_ref[0:1, cols] * prev + w_ref[1:2, cols] * p + w_ref[2:3, cols] * nxt


def _in_proj_kernel(ctx_ref, x_ref, g_ref, csh_ref, csc_ref, sh_ref, sc_ref, w_ref, wdt_ref,
                    sw_ref, sb_ref, dtb_ref, dtm_ref, u_ref, z_ref, xbc_ref, dta_ref, h_ref,
                    *, n_ctx_steps, row_len, ctx_row_len, hy_cols, d_ssd, d_xbc, tn):
    i = pl.program_id(1)
    is_ctx = i < n_ctx_steps
    tm = x_ref.shape[1]
    xin = jnp.where(is_ctx, ctx_ref[0], x_ref[0])
    shift = jnp.where(is_ctx, csh_ref[...], sh_ref[0])
    scale = jnp.where(is_ctx, csc_ref[...], sc_ref[0])
    y = xin * lax.rsqrt(jnp.mean(xin * xin, axis=-1, keepdims=True) + EPS) * g_ref[...]
    h_ref[...] = (y * (1.0 + scale) + shift).astype(jnp.bfloat16)

    pos = lax.broadcasted_iota(jnp.int32, (tm, 1), 0) + jnp.where(is_ctx, i, i - n_ctx_steps) * tm
    in_row = jnp.where(is_ctx, pos % ctx_row_len, pos % row_len)
    has_prev = in_row != 0
    has_next = in_row != jnp.where(is_ctx, ctx_row_len - 1, row_len - 1)

    @pl.when(jnp.logical_not(is_ctx))
    def _():
        for c0 in range(0, hy_cols, tn):
            cols = slice(c0, c0 + tn)
            u_ref[0, :, cols] = jnp.dot(h_ref[...], w_ref[:, cols], preferred_element_type=jnp.float32)
        z_ref[0] = jnp.dot(h_ref[...], w_ref[:, hy_cols:hy_cols + d_ssd], preferred_element_type=jnp.float32)

    for c0 in range(0, d_xbc, tn):
        cols = slice(c0, c0 + tn)
        wc = slice(hy_cols + d_ssd + c0, hy_cols + d_ssd + c0 + tn)
        p = jnp.dot(h_ref[...], w_ref[:, wc], preferred_element_type=jnp.float32)
        v = _conv3_rows(p, sw_ref, sb_ref, cols, has_prev, has_next)
        xbc_ref[0, :, cols] = v * jax.nn.sigmoid(v)
    pd = jnp.dot(h_ref[...], wdt_ref[...], preferred_element_type=jnp.float32) + dtb_ref[...]
    sp = jnp.maximum(pd, 0.0) + jnp.log(1.0 + jnp.exp(-jnp.abs(pd)))
    dta_ref[0] = sp * dtm_ref[...]


def in_proj_fused(ctx, x, g1, csh, csc, sh, sc, w_bf, wdt_bf, hy_cols, ssd_w, ssd_b, dt_bias2, dt_mult,
                  row_len, tm, tn):
    bsz, L, D = x.shape
    lc = ctx.shape[1]
    d_xbc = ssd_w.shape[1]
    d_ssd = w_bf.shape[1] - hy_cols - d_xbc
    n_ctx_steps = lc // tm
    n_steps = n_ctx_steps + L // tm
    lat = lambda b, i: (b, jnp.maximum(i - n_ctx_steps, 0), 0)
    allt = lambda b, i: (b, i, 0)
    const2 = lambda b, i: (0, 0)
    per_b = pl.BlockSpec((1, 1, D), lambda b, i: (b, 0, 0))
    kern = functools.partial(_in_proj_kernel, n_ctx_steps=n_ctx_steps, row_len=row_len, ctx_row_len=lc,
                             hy_cols=hy_cols, d_ssd=d_ssd, d_xbc=d_xbc, tn=tn)
    return pl.pallas_call(
        kern,
        grid=(bsz, n_steps),
        in_specs=[
            pl.BlockSpec((1, tm, D), lambda b, i: (b, jnp.minimum(i, n_ctx_steps - 1), 0)),
            pl.BlockSpec((1, tm, D), lat),
            pl.BlockSpec((1, D), const2),
            pl.BlockSpec((1, D), const2),
            pl.BlockSpec((1, D), const2),
            per_b, per_b,
            pl.BlockSpec(w_bf.shape, const2),
            pl.BlockSpec(wdt_bf.shape, const2),
            pl.BlockSpec(ssd_w.shape, const2),
            pl.BlockSpec(ssd_b.shape, const2),
            pl.BlockSpec((1, LANES), const2),
            pl.BlockSpec((1, LANES), const2),
        ],
        out_specs=[
            pl.BlockSpec((1, tm, hy_cols), lat),
            pl.BlockSpec((1, tm, d_ssd), lat),
            pl.BlockSpec((1, tm, d_xbc), allt),
            pl.BlockSpec((1, tm, LANES), allt),
        ],
        out_shape=[
            jax.ShapeDtypeStruct((bsz, L, hy_cols), jnp.float32),
            jax.ShapeDtypeStruct((bsz, L, d_ssd), jnp.float32),
            jax.ShapeDtypeStruct((bsz, lc + L, d_xbc), jnp.float32),
            jax.ShapeDtypeStruct((bsz, lc + L, LANES), jnp.float32),
        ],
        scratch_shapes=[pltpu.VMEM((tm, D), jnp.bfloat16)],
        compiler_params=pltpu.CompilerParams(
            dimension_semantics=("arbitrary", "arbitrary"), vmem_limit_bytes=VMEM_LIMIT_BYTES),
    )(ctx, x, g1, csh, csc, sh, sc, w_bf, wdt_bf, ssd_w, ssd_b, dt_bias2, dt_mult)


def in_proj_params(w_in, a_log, dt_bias, hy_cols, d_ssd, d_xbc):
    n_h = 2 * SSD_HEADS
    main = hy_cols + d_ssd + d_xbc
    w_dt = w_in[:, main:main + n_h]
    pad = jnp.zeros((w_in.shape[0], LANES - 2 * n_h), w_in.dtype)
    wdt = jnp.concatenate([w_dt, w_dt, pad], axis=1).astype(jnp.bfloat16)
    zpad = jnp.zeros((LANES - 2 * n_h,), jnp.float32)
    bias2 = jnp.concatenate([dt_bias.reshape(n_h), dt_bias.reshape(n_h), zpad])[None, :]
    mult = jnp.concatenate([jnp.ones((n_h,), jnp.float32), -jnp.exp(a_log).reshape(n_h), zpad])[None, :]
    return w_in[:, :main].astype(jnp.bfloat16), wdt, bias2, mult


def _out_router_kernel(yh_ref, ys_ref, z_ref, x_ref, ga_ref, sc_ref, sh_ref, ng_ref, g2_ref, wo_ref, wr_ref, br_ref,
                       x1_ref, hn_ref, ri_ref, rf_ref, cnt_ref, carry_ref):
    first = jnp.logical_and(pl.program_id(0) == 0, pl.program_id(1) == 0)

    @pl.when(first)
    def _():
        carry_ref[...] = jnp.zeros_like(carry_ref)

    bf = jnp.bfloat16
    tm = x_ref.shape[1]
    dh = yh_ref.shape[2]
    z = z_ref[0]
    ys = ys_ref[0] * (z * jax.nn.sigmoid(z))
    gw = ys.shape[1] // SSD_GROUPS
    acc = jnp.dot(yh_ref[0].astype(bf), wo_ref[0:dh, :], preferred_element_type=jnp.float32)
    for g in range(SSD_GROUPS):
        yg = ys[:, g * gw:(g + 1) * gw]
        yg = yg * lax.rsqrt(jnp.mean(yg * yg, axis=-1, keepdims=True) + EPS) * ng_ref[:, g * gw:(g + 1) * gw]
        acc += jnp.dot(yg.astype(bf), wo_ref[dh + g * gw:dh + (g + 1) * gw, :], preferred_element_type=jnp.float32)
    x1 = x_ref[0] + ga_ref[0] * acc
    x1_ref[0] = x1
    hn = x1 * lax.rsqrt(jnp.mean(x1 * x1, axis=-1, keepdims=True) + EPS) * g2_ref[...]
    hn = hn * (1.0 + sc_ref[0]) + sh_ref[0]
    hn_ref[0] = hn

    hn_hi = hn.astype(bf)
    hn_lo = (hn - hn_hi.astype(jnp.float32)).astype(bf)
    logits = (jnp.dot(hn_hi, wr_ref[0], preferred_element_type=jnp.float32)
              + jnp.dot(hn_lo, wr_ref[0], preferred_element_type=jnp.float32)
              + jnp.dot(hn_hi, wr_ref[1], preferred_element_type=jnp.float32)) + br_ref[...]
    lane = lax.broadcasted_iota(jnp.int32, (tm, LANES), 1)
    lane_f = lane.astype(jnp.float32)
    ninf = jnp.float32(-jnp.inf)
    big = jnp.float32(1e9)
    gl = jnp.where(lane < MOE_GROUPS, logits, ninf)
    gmax = jnp.max(gl, axis=-1, keepdims=True)
    p_group = 1.0 / jnp.sum(jnp.exp(gl - gmax), axis=-1, keepdims=True)
    g_sel = jnp.min(jnp.where(gl == gmax, lane_f, big), axis=-1, keepdims=True)
    e_lane = lane - MOE_GROUPS
    in_grp = jnp.logical_and(e_lane >= 0, (e_lane // EXPERTS_PER_GROUP).astype(jnp.float32) == g_sel)
    el = jnp.where(in_grp, logits, ninf)
    m1 = jnp.max(el, axis=-1, keepdims=True)
    i1 = jnp.min(jnp.where(el == m1, lane_f, big), axis=-1, keepdims=True)
    el2 = jnp.where(lane_f == i1, ninf, el)
    m2 = jnp.max(el2, axis=-1, keepdims=True)
    i2 = jnp.min(jnp.where(el2 == m2, lane_f, big), axis=-1, keepdims=True)
    t = jnp.exp(m2 - m1)
    w1 = 1.0 / (1.0 + t)
    gate1 = w1 * p_group
    gate2 = (t * w1) * p_group
    e1 = i1 - MOE_GROUPS
    e2 = i2 - MOE_GROUPS
    el_f = e_lane.astype(jnp.float32)
    oh1 = el_f == e1
    oh2 = el_f == e2
    oh = jnp.logical_or(oh1, oh2).astype(bf)
    r_i = lax.broadcasted_iota(jnp.int32, (tm, tm), 0)
    c_i = lax.broadcasted_iota(jnp.int32, (tm, tm), 1)
    before = jnp.dot((c_i < r_i).astype(bf), oh, preferred_element_type=jnp.float32) + carry_ref[...]
    rank1 = jnp.sum(jnp.where(oh1, before, 0.0), axis=-1, keepdims=True)
    rank2 = jnp.sum(jnp.where(oh2, before, 0.0), axis=-1, keepdims=True)
    carry_ref[...] += jnp.sum(oh.astype(jnp.float32), axis=0, keepdims=True)
    cnt_ref[...] = carry_ref[...]

    rec = jnp.where(lane == 0, e1, jnp.where(lane == 1, e2, jnp.where(lane == 2, rank1,
                                                                      jnp.where(lane == 3, rank2, 0.0))))
    ri_ref[0] = rec.T[0:ROUTE_COLS, :].astype(jnp.int32)
    col = lax.broadcasted_iota(jnp.int32, (tm, ROUTE_COLS), 1)
    rf_ref[0] = jnp.where(col == 0, gate1, gate2)


def out_proj_router(y_hy, y_scan, px, z_col, x, ga1, sc2, sh2, norm_g, g2, w_out_bf, w_router, b_router, tm):
    bsz, L, D = x.shape
    dh = y_hy.shape[-1]
    ds = y_scan.shape[-1]
    tok = lambda b, i: (b, i, 0)
    per_b = pl.BlockSpec((1, 1, D), lambda b, i: (b, 0, 0))
    const2 = lambda b, i: (0, 0)
    return pl.pallas_call(
        _out_router_kernel,
        grid=(bsz, L // tm),
        in_specs=[
            pl.BlockSpec((1, tm, dh), tok),
            pl.BlockSpec((1, tm, ds), tok),
            pl.BlockSpec((1, tm, ds), lambda b, i: (b, i, z_col)),
            pl.BlockSpec((1, tm, D), tok),
            per_b, per_b, per_b,
            pl.BlockSpec((1, ds), const2),
            pl.BlockSpec((1, D), const2),
            pl.BlockSpec((dh + ds, D), const2),
            pl.BlockSpec((2, D, LANES), lambda b, i: (0, 0, 0)),
            pl.BlockSpec((1, LANES), const2),
        ],
        out_specs=[
            pl.BlockSpec((1, tm, D), tok),
            pl.BlockSpec((1, tm, D), tok),
            pl.BlockSpec((1, ROUTE_COLS, tm), lambda b, i: (b, 0, i)),
            pl.BlockSpec((1, tm, ROUTE_COLS), tok),
            pl.BlockSpec((1, LANES), const2),
        ],
        out_shape=[
            jax.ShapeDtypeStruct((bsz, L, D), jnp.float32),
            jax.ShapeDtypeStruct((bsz, L, D), jnp.float32),
            jax.ShapeDtypeStruct((bsz, ROUTE_COLS, L), jnp.int32),
            jax.ShapeDtypeStruct((bsz, L, ROUTE_COLS), jnp.float32),
            jax.ShapeDtypeStruct((1, LANES), jnp.float32),
        ],
        scratch_shapes=[pltpu.VMEM((1, LANES), jnp.float32)],
        compiler_params=pltpu.CompilerParams(
            dimension_semantics=("arbitrary", "arbitrary"), vmem_limit_bytes=VMEM_LIMIT_BYTES),
    )(y_hy, y_scan, px, x, ga1, sc2, sh2, norm_g, g2, w_out_bf, w_router, b_router)


def _row_copy(src_hbm, src_row, dst_ref, dst_row, sem):
    return pltpu.make_async_copy(src_hbm.at[pl.ds(src_row, 1), :], dst_ref.at[pl.ds(dst_row, 1), :], sem)


def _dispatch_kernel(dest_ref, valid_ref, hn_ref, buf_hbm, zeros, sem, zsem):
    step = pl.program_id(0)
    tm = hn_ref.shape[0]
    n_tok = pl.num_programs(0) * tm
    blk = zeros.shape[0]
    n_blocks = buf_hbm.shape[0] // blk

    def zero_copy(i):
        return pltpu.make_async_copy(zeros, buf_hbm.at[pl.ds(pl.multiple_of(i * blk, blk), blk), :], zsem)

    def zfill(i, carry):
        @pl.when(valid_ref[i] < blk)
        def _():
            zero_copy(i).start()
        return carry

    def zwait(i, carry):
        @pl.when(valid_ref[i] < blk)
        def _():
            zero_copy(i).wait()
        return carry

    @pl.when(step == 0)
    def _():
        zeros[...] = jnp.zeros_like(zeros)
        lax.fori_loop(0, n_blocks, zfill, 0)
        lax.fori_loop(0, n_blocks, zwait, 0)

    def body(j, carry):
        t = step * tm + j
        _row_copy(hn_ref, j, buf_hbm, dest_ref[t], sem).start()
        _row_copy(hn_ref, j, buf_hbm, dest_ref[n_tok + t], sem).start()
        return carry

    lax.fori_loop(0, tm, body, 0, unroll=8)
    for _ in range(2):
        pltpu.make_async_copy(hn_ref, buf_hbm.at[pl.ds(0, tm), :], sem).wait()


def moe_dispatch(hn, dest, block_valid, blk, tm):
    T, D = hn.shape
    n_rows = block_valid.shape[0] * blk
    grid_spec = pltpu.PrefetchScalarGridSpec(
        num_scalar_prefetch=2,
        grid=(T // tm,),
        in_specs=[pl.BlockSpec((tm, D), lambda i, d, v: (i, 0))],
        out_specs=pl.BlockSpec(memory_space=pl.ANY),
        scratch_shapes=[pltpu.VMEM((blk, D), jnp.float32), pltpu.SemaphoreType.DMA(()),
                        pltpu.SemaphoreType.DMA(())],
    )
    return pl.pallas_call(
        _dispatch_kernel,
        grid_spec=grid_spec,
        out_shape=jax.ShapeDtypeStruct((n_rows, D), jnp.float32),
        compiler_params=pltpu.CompilerParams(dimension_semantics=("arbitrary",), has_side_effects=True),
    )(dest, block_valid, hn)


def _expert_kernel(eid_ref, first_ref, valid_ref, x_ref, w1_ref, w3_ref, w2_ref, o_ref, w1b, w3b, w2b):
    i = pl.program_id(0)
    del eid_ref
    bf = jnp.bfloat16

    @pl.when(first_ref[i] == 1)
    def _():
        w1b[...] = w1_ref[0].astype(bf)
        w3b[...] = w3_ref[0].astype(bf)
        w2b[...] = w2_ref[0].astype(bf)

    valid = valid_ref[i]

    @pl.when(valid > 0)
    def _():
        xb = x_ref[...].astype(bf)
        a = jnp.dot(xb, w1b[...], preferred_element_type=jnp.float32)
        b = jnp.dot(xb, w3b[...], preferred_element_type=jnp.float32)
        h = (a * jax.nn.sigmoid(a)) * b
        o_ref[...] = jnp.dot(h.astype(bf), w2b[...], preferred_element_type=jnp.float32)

    @pl.when(valid <= 0)
    def _():
        o_ref[...] = jnp.zeros_like(o_ref)


def expert_blocks(buf, block_eid, block_first, block_valid, w1, w3, w2, blk):
    rows, D = buf.shape
    n_blocks = rows // blk
    E, _, F = w1.shape
    grid_spec = pltpu.PrefetchScalarGridSpec(
        num_scalar_prefetch=3,
        grid=(n_blocks,),
        in_specs=[
            pl.BlockSpec((blk, D), lambda i, eid, fi, va: (i, 0)),
            pl.BlockSpec((1, D, F), lambda i, eid, fi, va: (eid[i], 0, 0)),
            pl.BlockSpec((1, D, F), lambda i, eid, fi, va: (eid[i], 0, 0)),
            pl.BlockSpec((1, F, D), lambda i, eid, fi, va: (eid[i], 0, 0)),
        ],
        out_specs=pl.BlockSpec((blk, D), lambda i, eid, fi, va: (i, 0)),
        scratch_shapes=[pltpu.VMEM((D, F), jnp.bfloat16), pltpu.VMEM((D, F), jnp.bfloat16),
                        pltpu.VMEM((F, D), jnp.bfloat16)],
    )
    return pl.pallas_call(
        _expert_kernel,
        grid_spec=grid_spec,
        out_shape=jax.ShapeDtypeStruct((rows, D), jnp.float32),
        compiler_params=pltpu.CompilerParams(
            dimension_semantics=("arbitrary",), vmem_limit_bytes=VMEM_LIMIT_BYTES),
    )(block_eid, block_first, block_valid, buf, w1, w3, w2)


def _combine_kernel(dest_ref, x1_ref, rf_ref, ga_ref, gf_ref, yb_hbm, o_ref, ybuf, sem):
    b = pl.program_id(0)
    i = pl.program_id(1)
    n_i = pl.num_programs(1)
    tm = x1_ref.shape[1]
    step = b * n_i + i
    n_steps = pl.num_programs(0) * n_i
    slot = step % 2

    def issue(step_, slot_):
        def body(j, carry):
            t = step_ * tm + j
            _row_copy(yb_hbm, dest_ref[t], ybuf.at[slot_, 0], j, sem.at[slot_]).start()
            _row_copy(yb_hbm, dest_ref[n_steps * tm + t], ybuf.at[slot_, 1], j, sem.at[slot_]).start()
            return carry
        lax.fori_loop(0, tm, body, 0, unroll=8)

    @pl.when(step == 0)
    def _():
        issue(0, 0)

    @pl.when(step + 1 < n_steps)
    def _():
        issue(step + 1, 1 - slot)

    pltpu.make_async_copy(yb_hbm.at[pl.ds(0, tm), :], ybuf.at[slot, 0], sem.at[slot]).wait()
    pltpu.make_async_copy(yb_hbm.at[pl.ds(0, tm), :], ybuf.at[slot, 1], sem.at[slot]).wait()
    rf = rf_ref[0]
    y = rf[:, 0:1] * ybuf[slot, 0] + rf[:, 1:2] * ybuf[slot, 1]
    x2 = x1_ref[0] + ga_ref[0] * y
    o_ref[0] = x2 * lax.rsqrt(jnp.mean(x2 * x2, axis=-1, keepdims=True) + EPS) * gf_ref[...]


def moe_combine(x1, route_f, ga2, g_final, yb, dest, tm):
    bsz, L, D = x1.shape
    grid_spec = pltpu.PrefetchScalarGridSpec(
        num_scalar_prefetch=1,
        grid=(bsz, L // tm),
        in_specs=[
            pl.BlockSpec((1, tm, D), lambda b, i, d: (b, i, 0)),
            pl.BlockSpec((1, tm, ROUTE_COLS), lambda b, i, d: (b, i, 0)),
            pl.BlockSpec((1, 1, D), lambda b, i, d: (b, 0, 0)),
            pl.BlockSpec((1, D), lambda b, i, d: (0, 0)),
            pl.BlockSpec(memory_space=pl.ANY),
        ],
        out_specs=pl.BlockSpec((1, tm, D), lambda b, i, d: (b, i, 0)),
        scratch_shapes=[pltpu.VMEM((2, 2, tm, D), jnp.float32), pltpu.SemaphoreType.DMA((2,))],
    )
    return pl.pallas_call(
        _combine_kernel,
        grid_spec=grid_spec,
        out_shape=jax.ShapeDtypeStruct((bsz, L, D), jnp.float32),
        compiler_params=pltpu.CompilerParams(
            dimension_semantics=("arbitrary", "arbitrary"), vmem_limit_bytes=VMEM_LIMIT_BYTES),
    )(dest, x1, route_f, ga2, g_final, yb)


def moe_plan(route_i, counts, blk, n_blocks):
    cnt = counts[0, MOE_GROUPS:MOE_GROUPS + N_EXPERTS].astype(jnp.int32)
    padded = (cnt + blk - 1) // blk * blk
    ends = jnp.cumsum(padded)
    starts = ends - padded
    experts = jnp.arange(N_EXPERTS, dtype=jnp.int32)
    dest = jnp.concatenate([
        (jnp.sum(jnp.where(route_i[:, k, :, None] == experts, starts, 0), axis=-1) + route_i[:, 2 + k]).reshape(-1)
        for k in range(TOP_K)])
    first_row = jnp.arange(n_blocks, dtype=jnp.int32) * blk
    block_eid = jnp.minimum(jnp.sum((ends[None, :] <= first_row[:, None]).astype(jnp.int32), axis=1), N_EXPERTS - 1)
    block_valid = jnp.clip(cnt[block_eid] - (first_row - starts[block_eid]), 0, blk).astype(jnp.int32)
    block_first = jnp.concatenate([jnp.ones((1,), jnp.int32),
                                   (block_eid[1:] != block_eid[:-1]).astype(jnp.int32)])
    return dest, block_eid, block_first, block_valid


def dft_tables(L):
    n = 2 * L
    f = lax.broadcasted_iota(jnp.int32, (L, L), 0)
    t = lax.broadcasted_iota(jnp.int32, (L, L), 1)
    ang = ((f * t) % n).astype(jnp.float32) * (2.0 * math.pi / n)
    return jnp.cos(ang).astype(jnp.bfloat16), jnp.sin(ang).astype(jnp.bfloat16)


def _alt_sign(L):
    t = lax.broadcasted_iota(jnp.int32, (L, 1), 0)
    return (1 - 2 * (t & 1)).astype(jnp.float32)


def _spectrum_kernel(a_ref, b_ref, c_ref, s_ref, kr_ref, ks_ref, kn_ref):
    L = a_ref.shape[1]
    a = a_ref[0]
    row = lax.broadcasted_iota(jnp.int32, (L, 1), 0)
    scale = jnp.where(row == 0, 0.5 / L, 1.0 / L)
    kr_ref[0] = scale * jnp.dot(c_ref[...], a.astype(jnp.bfloat16), preferred_element_type=jnp.float32)
    ks_ref[0] = scale * jnp.dot(s_ref[...], b_ref[0].astype(jnp.bfloat16), preferred_element_type=jnp.float32)
    kn_ref[0] = jnp.sum(a * _alt_sign(L), axis=0, keepdims=True) * (0.5 / L)


def filter_spectrum(a, b, cos_t, sin_t, tc):
    n, L, C = a.shape
    blk = pl.BlockSpec((1, L, tc), lambda o, j: (o, 0, j))
    tab = pl.BlockSpec((L, L), lambda o, j: (0, 0))
    return pl.pallas_call(
        _spectrum_kernel,
        grid=(n, C // tc),
        in_specs=[blk, blk, tab, tab],
        out_specs=[blk, blk, pl.BlockSpec((1, 1, tc), lambda o, j: (o, 0, j))],
        out_shape=[jax.ShapeDtypeStruct((n, L, C), jnp.float32)] * 2 + [jax.ShapeDtypeStruct((n, 1, C), jnp.float32)],
        compiler_params=pltpu.CompilerParams(
            dimension_semantics=("arbitrary", "arbitrary"), vmem_limit_bytes=VMEM_LIMIT_BYTES),
    )(a, b, cos_t, sin_t)


def _phase_conv3(raw, w_ref, b_ref, rows_per_phase):
    n_ph = len(raw)
    h = raw[0].shape[0]
    j = lax.broadcasted_iota(jnp.int32, (h, 1), 0) % rows_per_phase
    prev0 = jnp.where(j != 0, pltpu.roll(raw[n_ph - 1], 1, 0), 0.0)
    next_last = jnp.where(j != rows_per_phase - 1, pltpu.roll(raw[0], h - 1, 0), 0.0)
    out = []
    for p in range(n_ph):
        prev = raw[p - 1] if p > 0 else prev0
        nxt = raw[p + 1] if p < n_ph - 1 else next_last
        out.append(b_ref[...] + w_ref[0:1, :] * prev + w_ref[1:2, :] * raw[p] + w_ref[2:3, :] * nxt)
    return out


def _long_conv_kernel(*refs, n_ph, n_slab, conv_z, rows_per_phase):
    z_refs = refs[:n_slab]
    xn_refs = refs[n_slab:2 * n_slab]
    (kr_ref, ks_ref, kn_ref, bias_ref, cwz_ref, cbz_ref, cwx_ref, cbx_ref, c_ref, s_ref,
     o_ref, acc_ref, zr_ref, zs_ref, yr_ref, ys_ref, stage_ref) = refs[2 * n_slab:]
    H = z_refs[0].shape[1] // n_ph
    f32 = jnp.float32
    bf = jnp.bfloat16
    sign = _alt_sign(H)

    def phases(slab_refs):
        return [jnp.concatenate([r[0, pl.ds(p, H, stride=n_ph), :] for r in slab_refs], axis=1)
                for p in range(n_ph)]

    z_ph = phases(z_refs)
    if conv_z:
        z_ph = _phase_conv3(z_ph, cwz_ref, cbz_ref, rows_per_phase)
    for q in range(n_ph):
        zb = z_ph[q].astype(bf)
        zr_ref[q] = jnp.dot(c_ref[...], zb, preferred_element_type=f32)
        zs_ref[q] = jnp.dot(s_ref[...], zb, preferred_element_type=f32)
    z_nyq = [jnp.sum(z * sign, axis=0, keepdims=True) for z in z_ph]
    for p in range(n_ph):
        nyq = sum(z_nyq[q] * kn_ref[p - q + n_ph - 1] for q in range(n_ph))
        acc_ref[p] = z_ph[p] * bias_ref[0] + sign * nyq
        yr = 0.0
        ys = 0.0
        for q in range(n_ph):
            slot = p - q + n_ph - 1
            yr = yr + zr_ref[q] * kr_ref[slot] - zs_ref[q] * ks_ref[slot]
            ys = ys + zr_ref[q] * ks_ref[slot] + zs_ref[q] * kr_ref[slot]
        yr_ref[p] = yr.astype(bf)
        ys_ref[p] = ys.astype(bf)
    for p in range(n_ph):
        acc_ref[p] += (jnp.dot(c_ref[...], yr_ref[p], preferred_element_type=f32)
                       + jnp.dot(s_ref[...], ys_ref[p], preferred_element_type=f32))
    x_ph = _phase_conv3(phases(xn_refs), cwx_ref, cbx_ref, rows_per_phase)
    for p in range(n_ph):
        out_p = x_ph[p] * acc_ref[p]
        for sl in range(n_slab):
            stage_ref[sl, pl.ds(p, H, stride=n_ph), :] = out_p[:, sl * LANES:(sl + 1) * LANES]
    for sl in range(n_slab):
        o_ref[0, :, sl * LANES:(sl + 1) * LANES] = stage_ref[sl]


def long_conv_gate(z_arr, z_col, conv_z, xn_arr, xn_col, conv_w, conv_b, kr, ks, kn, bias, cos_t, sin_t,
                   tc, n_ph, row_len):
    bsz, L, _ = z_arr.shape
    H = L // n_ph
    C = kr.shape[-1]
    nj = C // tc
    n_slab = tc // LANES
    n_f = 2 * n_ph - 1
    tab = pl.BlockSpec((H, H), lambda j, b: (0, 0), pipeline_mode=pl.Buffered(1))
    spec = pl.BlockSpec((n_f, H, tc), lambda j, b: (0, 0, j), pipeline_mode=pl.Buffered(1))
    nyq = pl.BlockSpec((n_f, 1, tc), lambda j, b: (0, 0, j))
    vec = pl.BlockSpec((1, 1, tc), lambda j, b: (0, 0, j))

    def slabs(col):
        return [pl.BlockSpec((1, L, LANES),
                             functools.partial(lambda j, b, sl: (b, 0, (col * nj + j) * n_slab + sl), sl=sl))
                for sl in range(n_slab)]

    def conv_specs(col):
        return [pl.BlockSpec((3, tc), lambda j, b: (0, col * nj + j)),
                pl.BlockSpec((1, tc), lambda j, b: (0, col * nj + j))]

    zc = z_col if conv_z else 0
    kern = functools.partial(_long_conv_kernel, n_ph=n_ph, n_slab=n_slab, conv_z=conv_z,
                             rows_per_phase=row_len // n_ph)
    return pl.pallas_call(
        kern,
        grid=(nj, bsz),
        in_specs=(slabs(z_col) + slabs(xn_col) + [spec, spec, nyq, vec] + conv_specs(zc) + conv_specs(xn_col)
                  + [tab, tab]),
        out_specs=pl.BlockSpec((1, L, tc), lambda j, b: (b, 0, j)),
        out_shape=jax.ShapeDtypeStruct((bsz, L, C), jnp.float32),
        scratch_shapes=[pltpu.VMEM((n_ph, H, tc), jnp.float32), pltpu.VMEM((n_ph, H, tc), jnp.float32),
                        pltpu.VMEM((n_ph, H, tc), jnp.float32), pltpu.VMEM((n_ph, H, tc), jnp.bfloat16),
                        pltpu.VMEM((n_ph, H, tc), jnp.bfloat16), pltpu.VMEM((n_slab, L, LANES), jnp.float32)],
        compiler_params=pltpu.CompilerParams(
            dimension_semantics=("arbitrary", "arbitrary"), vmem_limit_bytes=VMEM_LIMIT_BYTES),
    )(*([z_arr] * n_slab), *([xn_arr] * n_slab), kr, ks, kn, bias, conv_w, conv_b, conv_w, conv_b, cos_t, sin_t)


def _polyphase_taps(kf, kb, n_ph):
    H = kf.shape[0] // n_ph
    ph = lambda a, p: a[p * H:(p + 1) * H]
    zero = jnp.zeros_like(kf[:1])
    plus, minus = [], []
    for r in range(-(n_ph - 1), n_ph):
        if r >= 0:
            plus.append(ph(kf, r))
        else:
            plus.append(jnp.concatenate([ph(kb, -r)[0:1], ph(kf, n_ph + r)[:-1]], axis=0))
        if r <= 0:
            minus.append(jnp.concatenate([zero, ph(kb, -r)[1:]], axis=0))
        else:
            minus.append(jnp.concatenate([zero, ph(kb, n_ph - r)[:-1]], axis=0))
    return jnp.stack(plus), jnp.stack(minus)


def hyena_long_convs(p_hy, conv_w, conv_b, kp, h_bias, tc, n_ph, row_len):
    L = p_hy.shape[1]
    C = h_bias.shape[1]
    cos_t, sin_t = dft_tables(L // n_ph)
    z = p_hy
    for o in range(h_bias.shape[0]):
        fwd = slice((2 * o) * C, (2 * o + 1) * C)
        bwd = slice((2 * o + 1) * C, (2 * o + 2) * C)
        plus, minus = _polyphase_taps(kp[:, fwd], kp[:, bwd], n_ph)
        kr, ks, kn = filter_spectrum(plus + minus, plus - minus, cos_t, sin_t, tc)
        z = long_conv_gate(z, 0, o == 0, p_hy, o + 1, conv_w, conv_b, kr, ks, kn, h_bias[o][None, None, :],
                           cos_t, sin_t, tc, n_ph, row_len)
    return z


def _filter_kernel(band_ref, w1_ref, b1_ref, fr_ref, w2_ref, b2_ref, w3_ref, dl_ref, k_ref, *, seq_len, n_ph):
    hp = lax.Precision.HIGHEST
    f32 = jnp.float32
    tp = k_ref.shape[0]
    per_phase = seq_len // n_ph
    g = lax.broadcasted_iota(jnp.int32, (tp, 1), 0) + pl.program_id(0) * tp
    phase = g // per_phase
    pos = (n_ph * (g - phase * per_phase) + phase).astype(f32)
    t = pos / max(seq_len - 1, 1)
    ang = (2 * math.pi / seq_len) * pos * band_ref[...]
    lane = lax.broadcasted_iota(jnp.int32, (tp, LANES), 1)
    feats = jnp.where(lane == 0, t,
                      jnp.where(lane <= HYENA_BANDS, jnp.cos(ang),
                                jnp.where(lane <= 2 * HYENA_BANDS, -jnp.sin(ang), 0.0)))
    h = jnp.sin(fr_ref[...] * (jnp.dot(feats, w1_ref[...], precision=hp, preferred_element_type=f32) + b1_ref[...]))
    h = jnp.sin(fr_ref[...] * (jnp.dot(h, w2_ref[...], precision=hp, preferred_element_type=f32) + b2_ref[...]))
    window = jnp.exp(-t * dl_ref[...])
    c = dl_ref.shape[1]
    for j in range(w3_ref.shape[1] // c):
        cols = slice(j * c, (j + 1) * c)
        k_ref[:, cols] = jnp.dot(h, w3_ref[:, cols], precision=hp, preferred_element_type=f32) * window


def hyena_filters_polyphase(seq_len, f_w1, f_b1, f_freq, f_w2, f_b2, f_w3, d_hyena, tp, n_ph):
    f32 = jnp.float32
    fh = f_w1.shape[1]
    n_emb = 1 + 2 * HYENA_BANDS
    bands = jnp.linspace(1e-4, HYENA_BANDS - 1, HYENA_BANDS, dtype=f32)
    band_row = jnp.concatenate([jnp.zeros((1,), f32), bands, bands, jnp.zeros((LANES - n_emb,), f32)])[None, :]
    w1p = jnp.concatenate([f_w1, jnp.zeros((LANES - n_emb, fh), f32)], axis=0)
    deltas = jnp.abs(jnp.linspace(math.log(HYENA_TARGET) / HYENA_SLOW_DECAY,
                                  math.log(HYENA_TARGET) / HYENA_FAST_DECAY, d_hyena, dtype=f32))[None, :]
    n_out = f_w3.shape[1]
    full = lambda a: pl.BlockSpec(a.shape, lambda i: (0,) * a.ndim)
    args = (band_row, w1p, f_b1[None, :], f_freq[None, :], f_w2, f_b2[None, :], f_w3, deltas)
    return pl.pallas_call(
        functools.partial(_filter_kernel, seq_len=seq_len, n_ph=n_ph),
        grid=(seq_len // tp,),
        in_specs=[full(a) for a in args],
        out_specs=pl.BlockSpec((tp, n_out), lambda i: (i, 0)),
        out_shape=jax.ShapeDtypeStruct((seq_len, n_out), f32),
        compiler_params=pltpu.CompilerParams(dimension_semantics=("arbitrary",), vmem_limit_bytes=VMEM_LIMIT_BYTES),
    )(*args)


def _ada_kernel(c_ref, w_ref, b_ref, o_ref):
    cv = c_ref[...]
    s = cv * jax.nn.sigmoid(cv)
    o_ref[...] = jnp.dot(s, w_ref[...], precision=lax.Precision.HIGHEST,
                         preferred_element_type=jnp.float32) + b_ref[...]


def ada_modulation(c_rows, w_ada, b_ada, tn):
    rows, D = c_rows.shape
    N = w_ada.shape[1]
    return pl.pallas_call(
        _ada_kernel,
        grid=(N // tn,),
        in_specs=[pl.BlockSpec((rows, D), lambda j: (0, 0)),
                  pl.BlockSpec((D, tn), lambda j: (0, j)),
                  pl.BlockSpec((1, tn), lambda j: (0, j))],
        out_specs=pl.BlockSpec((rows, tn), lambda j: (0, j)),
        out_shape=jax.ShapeDtypeStruct((rows, N), jnp.float32),
        compiler_params=pltpu.CompilerParams(dimension_semantics=("arbitrary",), vmem_limit_bytes=VMEM_LIMIT_BYTES),
    )(c_rows, w_ada, b_ada[None, :])


def _ssd_kernel(xf_ref, df_ref, xb_ref, db_ref, dskip_ref, y_ref, h_ref, *, n_ctx_chunks):
    s = pl.program_id(1)
    n_steps = pl.num_programs(1)
    Q, G, R, P, N = SSD_CHUNK, SSD_GROUPS, SSD_HPG, SSD_HEAD_DIM, SSD_STATE
    GP = R * P
    bf = jnp.bfloat16

    @pl.when(s == 0)
    def _():
        h_ref[...] = jnp.zeros_like(h_ref)
        y_ref[...] = jnp.zeros_like(y_ref)

    row = lax.broadcasted_iota(jnp.int32, (Q, Q), 0)
    col = lax.broadcasted_iota(jnp.int32, (Q, Q), 1)
    lane_head = lax.broadcasted_iota(jnp.int32, (Q, GP), 1) // P
    is_latent = s >= n_ctx_chunks
    n_lat = n_steps - n_ctx_chunks
    out_chunk = (jnp.clip(s - n_ctx_chunks, 0, n_lat - 1), jnp.clip(n_steps - 1 - s, 0, n_lat - 1))

    for d, (x_ref, da_ref) in enumerate(((xf_ref, df_ref), (xb_ref, db_ref))):
        mask = (row >= col) if d == 0 else (col >= row)
        tri = mask.astype(jnp.float32)
        da = da_ref[0]
        cum = jnp.dot(tri, da, precision=lax.Precision.HIGHEST, preferred_element_type=jnp.float32)
        cum_t = cum.T
        edge = Q - 1 if d == 0 else 0
        blk = x_ref.at[0]
        for g in range(G):
            xg = blk[:, g * GP:(g + 1) * GP]
            bg = blk[:, D_SSD + g * N:D_SSD + (g + 1) * N].astype(bf)
            cg = blk[:, D_SSD + G * N + g * N:D_SSD + G * N + (g + 1) * N].astype(bf)
            heads = [d * SSD_HEADS + g * R + r for r in range(R)]
            dtm = jnp.zeros((Q, GP), jnp.float32)
            cumm = jnp.zeros((Q, GP), jnp.float32)
            totm = jnp.zeros((Q, GP), jnp.float32)
            for r, h in enumerate(heads):
                sel = lane_head == r
                dtm = jnp.where(sel, da[:, h:h + 1], dtm)
                cumm = jnp.where(sel, cum[:, SSD_HEADS * 2 + h:SSD_HEADS * 2 + h + 1], cumm)
                totm = jnp.where(sel, cum[edge:edge + 1, SSD_HEADS * 2 + h:SSD_HEADS * 2 + h + 1], totm)
            xdt = xg * dtm
            hg = h_ref[d, g * GP:(g + 1) * GP, :]

            gmat = lax.dot_general(cg, bg, (((1,), (1,)), ((), ())), preferred_element_type=jnp.float32)
            y_off = lax.dot_general(cg, hg.astype(bf), (((1,), (1,)), ((), ())),
                                    preferred_element_type=jnp.float32) * jnp.exp(cumm)
            if d == 0:
                y_off = y_off + dskip_ref[:, g * GP:(g + 1) * GP] * xg
            parts = []
            for r, h in enumerate(heads):
                a_col = cum[:, SSD_HEADS * 2 + h:SSD_HEADS * 2 + h + 1]
                a_row = cum_t[SSD_HEADS * 2 + h:SSD_HEADS * 2 + h + 1, :]
                decay = jnp.exp(jnp.where(mask, a_col - a_row, NEG_BIG))
                parts.append(jnp.dot((gmat * decay).astype(bf), xdt[:, r * P:(r + 1) * P].astype(bf),
                                     preferred_element_type=jnp.float32))
            y = jnp.where(is_latent, y_off + jnp.concatenate(parts, axis=-1), 0.0)
            rows = pl.ds(pl.multiple_of(out_chunk[d] * Q, Q), Q)
            y_ref[0, rows, g * GP:(g + 1) * GP] += y

            xw = (xdt * jnp.exp(totm - cumm)).astype(bf)
            st = lax.dot_general(xw, bg, (((0,), (0,)), ((), ())), preferred_element_type=jnp.float32)
            for r, h in enumerate(heads):
                dec = jnp.exp(cum_t[SSD_HEADS * 2 + h:SSD_HEADS * 2 + h + 1, edge:edge + 1])
                rs = slice(g * GP + r * P, g * GP + (r + 1) * P)
                h_ref[d, rs, :] = h_ref[d, rs, :] * dec + st[r * P:(r + 1) * P, :]


def ssd_scan_bidir(xbc, dta, d_skip, n_ctx):
    bsz, lt, width = xbc.shape
    Q = SSD_CHUNK
    n_steps = lt // Q
    n_ctx_chunks = n_ctx // Q
    L = lt - n_ctx

    def bwd_chunk(s):
        return jnp.where(s < n_ctx_chunks, n_ctx_chunks - 1 - s, n_steps - 1 - s + n_ctx_chunks)

    return pl.pallas_call(
        functools.partial(_ssd_kernel, n_ctx_chunks=n_ctx_chunks),
        grid=(bsz, n_steps),
        in_specs=[
            pl.BlockSpec((1, Q, width), lambda b, s: (b, s, 0)),
            pl.BlockSpec((1, Q, LANES), lambda b, s: (b, s, 0)),
            pl.BlockSpec((1, Q, width), lambda b, s: (b, bwd_chunk(s), 0)),
            pl.BlockSpec((1, Q, LANES), lambda b, s: (b, bwd_chunk(s), 0)),
            pl.BlockSpec((1, D_SSD), lambda b, s: (0, 0)),
        ],
        out_specs=pl.BlockSpec((1, L, D_SSD), lambda b, s: (b, 0, 0)),
        out_shape=jax.ShapeDtypeStruct((bsz, L, D_SSD), jnp.float32),
        scratch_shapes=[pltpu.VMEM((2, SSD_GROUPS * SSD_HPG * SSD_HEAD_DIM, SSD_STATE), jnp.float32)],
        compiler_params=pltpu.CompilerParams(
            dimension_semantics=("arbitrary", "arbitrary"), vmem_limit_bytes=VMEM_LIMIT_BYTES),
    )(xbc, dta, xbc, dta, d_skip)


def kernel(x, c, ctx, c_ctx, w_ada, b_ada, g_norm1, g_norm2, w_in, hy_conv_w, hy_conv_b, hy_f_w1, hy_f_b1, hy_f_freq, hy_f_w2, hy_f_b2, hy_f_w3, hy_bias, ssd_conv_w, ssd_conv_b, ssd_a_log, ssd_dt_bias, ssd_d, ssd_norm_g, w_out, w_group, b_group, w_expert, b_expert, w1, w3, w2, g_final):
    bsz, seq_len, _ = x.shape
    l = 0
    rows_pad = -(bsz + 1) % SUBLANES
    c_rows = jnp.concatenate([c, c_ctx[None, :], jnp.zeros((rows_pad, D_MODEL), jnp.float32)], axis=0)
    mod_all = ada_modulation(c_rows, w_ada[l], b_ada[l], 512)
    sh1, sc1, ga1, sh2, sc2, ga2 = jnp.split(mod_all[:bsz, None, :], 6, axis=-1)
    csh1, csc1 = mod_all[bsz, :D_MODEL], mod_all[bsz, D_MODEL:2 * D_MODEL]

    w_out_bf = w_out[l].astype(jnp.bfloat16)
    w_in_bf, w_dt_bf, dt_bias2, dt_mult = in_proj_params(w_in[l], ssd_a_log[l], ssd_dt_bias[l],
                                                         HY_COLS, D_SSD, D_XBC)

    p_hy, z, xbc, dta = in_proj_fused(ctx, x, g_norm1[l][None, :], csh1[None, :], csc1[None, :], sh1, sc1,
                                      w_in_bf, w_dt_bf, HY_COLS, ssd_conv_w[l], ssd_conv_b[l][None, :],
                                      dt_bias2, dt_mult, GRID_W, 256, 512)
    kp = hyena_filters_polyphase(seq_len, hy_f_w1[l], hy_f_b1[l], hy_f_freq[l], hy_f_w2[l], hy_f_b2[l],
                                 hy_f_w3[l], D_HYENA, 256, HYENA_PHASES)
    y_hy = hyena_long_convs(p_hy, hy_conv_w[l], hy_conv_b[l][None, :], kp, hy_bias[l], 256, HYENA_PHASES, GRID_W)
    y_scan = ssd_scan_bidir(xbc, dta, jnp.repeat(ssd_d[l], SSD_HEAD_DIM)[None, :], CTX_LEN)

    pad = LANES - MOE_GROUPS - N_EXPERTS
    w_router = jnp.concatenate([w_group[l], w_expert[l], jnp.zeros((D_MODEL, pad), jnp.float32)], axis=1)
    w_router_hi = w_router.astype(jnp.bfloat16)
    w_router_lo = (w_router - w_router_hi.astype(jnp.float32)).astype(jnp.bfloat16)
    w_router = jnp.stack([w_router_hi, w_router_lo])
    b_router = jnp.concatenate([b_group[l], b_expert[l], jnp.zeros((pad,), jnp.float32)])[None, :]
    x1, hn, route_i, route_f, counts = out_proj_router(
        y_hy, y_scan, z, 0, x, ga1, sc2, sh2, ssd_norm_g[l][None, :], g_norm2[l][None, :],
        w_out_bf, w_router, b_router, 256)
    n_tok = bsz * seq_len
    n_blocks = -(-n_tok * TOP_K // MOE_BLOCK) + N_EXPERTS
    dest, block_eid, block_first, block_valid = moe_plan(route_i, counts, MOE_BLOCK, n_blocks)
    buf = moe_dispatch(hn.reshape(n_tok, D_MODEL), dest, block_valid, MOE_BLOCK, 256)
    yb = expert_blocks(buf, block_eid, block_first, block_valid, w1[l], w3[l], w2[l], MOE_BLOCK)
    return moe_combine(x1, route_f, ga2, g_final[None, :], yb, dest, 256)
```

```python
import functools
import math

import jax
import jax.numpy as jnp
from jax import lax
from jax.experimental import pallas as pl
from jax.experimental.pallas import tpu as pltpu

D_MODEL = 1024
CTX_LEN = 256
GRID_W = 64
EPS = 1e-6
SHORT_CONV = 3

D_HYENA = D_MODEL // 2
HYENA_ORDER = 2
HYENA_BANDS = 8
HYENA_FAST_DECAY = 0.3
HYENA_SLOW_DECAY = 1.5
HYENA_TARGET = 1e-2
HYENA_PHASES = 4

D_SSD = D_MODEL // 2
SSD_HEAD_DIM = 64
SSD_HEADS = D_SSD // SSD_HEAD_DIM
SSD_GROUPS = 2
SSD_HPG = SSD_HEADS // SSD_GROUPS
SSD_STATE = 128
SSD_CHUNK = 128

D_XBC = D_SSD + 2 * SSD_GROUPS * SSD_STATE
HY_COLS = (HYENA_ORDER + 1) * D_HYENA
D_IN = HY_COLS + D_SSD + D_XBC + 2 * SSD_HEADS
LANES = 128
SUBLANES = 8
D_IN_PAD = -(-D_IN // LANES) * LANES

MOE_GROUPS = 8
EXPERTS_PER_GROUP = 8
N_EXPERTS = MOE_GROUPS * EXPERTS_PER_GROUP
TOP_K = 2
D_EXPERT = 512
MOE_BLOCK = 256
ROUTE_COLS = 8
IN_PROJ_ROWS = 512

VMEM_LIMIT_BYTES = 56 * 1024 * 1024
NEG_BIG = -1e30


def _conv3_rows(p, w_ref, b_ref, cols, has_prev, has_next):
    n = p.shape[0]
    prev = jnp.where(has_prev, pltpu.roll(p, 1, 0), 0.0)
    nxt = jnp.where(has_next, pltpu.roll(p, n - 1, 0), 0.0)
    return b_ref[:, cols] + w_ref[0:1, cols] * prev + w_ref[1:2, cols] * p + w_ref[2:3, cols] * nxt


def _in_proj_kernel(ctx_ref, x_ref, g_ref, csh_ref, csc_ref, sh_ref, sc_ref, w_ref, wdt_ref,
                    sw_ref, sb_ref, dtb_ref, dtm_ref, u_ref, z_ref, xbc_ref, dta_ref, h_ref,
                    *, n_ctx_steps, row_len, ctx_row_len, hy_cols, d_ssd, d_xbc, tn):
    i = pl.program_id(1)
    is_ctx = i < n_ctx_steps
    tm = x_ref.shape[1]
    xin = jnp.where(is_ctx, ctx_ref[0], x_ref[0])
    shift = jnp.where(is_ctx, csh_ref[...], sh_ref[0])
    scale = jnp.where(is_ctx, csc_ref[...], sc_ref[0])
    y = xin * lax.rsqrt(jnp.mean(xin * xin, axis=-1, keepdims=True) + EPS) * g_ref[...]
    h_ref[...] = (y * (1.0 + scale) + shift).astype(jnp.bfloat16)

    pos = lax.broadcasted_iota(jnp.int32, (tm, 1), 0) + jnp.where(is_ctx, i, i - n_ctx_steps) * tm
    in_row = jnp.where(is_ctx, pos % ctx_row_len, pos % row_len)
    has_prev = in_row != 0
    has_next = in_row != jnp.where(is_ctx, ctx_row_len - 1, row_len - 1)

    @pl.when(jnp.logical_not(is_ctx))
    def _():
        for c0 in range(0, hy_cols, tn):
            cols = slice(c0, c0 + tn)
            u_ref[0, :, cols] = jnp.dot(h_ref[...], w_ref[:, cols], preferred_element_type=jnp.float32)
        z_ref[0] = jnp.dot(h_ref[...], w_ref[:, hy_cols:hy_cols + d_ssd], preferred_element_type=jnp.float32)

    for c0 in range(0, d_xbc, tn):
        cols = slice(c0, c0 + tn)
        wc = slice(hy_cols + d_ssd + c0, hy_cols + d_ssd + c0 + tn)
        p = jnp.dot(h_ref[...], w_ref[:, wc], preferred_element_type=jnp.float32)
        v = _conv3_rows(p, sw_ref, sb_ref, cols, has_prev, has_next)
        xbc_ref[0, :, cols] = v * jax.nn.sigmoid(v)
    pd = jnp.dot(h_ref[...], wdt_ref[...], preferred_element_type=jnp.float32) + dtb_ref[...]
    sp = jnp.maximum(pd, 0.0) + jnp.log(1.0 + jnp.exp(-jnp.abs(pd)))
    dta_ref[0] = sp * dtm_ref[...]


def in_proj_fused(ctx, x, g1, csh, csc, sh, sc, w_bf, wdt_bf, hy_cols, ssd_w, ssd_b, dt_bias2, dt_mult,
                  row_len, ctx_row_len, tm, tn):
    bsz, L, D = x.shape
    lc = ctx.shape[1]
    d_xbc = ssd_w.shape[1]
    d_ssd = w_bf.shape[1] - hy_cols - d_xbc
    n_ctx_steps = lc // tm
    n_steps = n_ctx_steps + L // tm
    lat = lambda b, i: (b, jnp.maximum(i - n_ctx_steps, 0), 0)
    allt = lambda b, i: (b, i, 0)
    const2 = lambda b, i: (0, 0)
    per_b = pl.BlockSpec((1, 1, D), lambda b, i: (b, 0, 0))
    kern = functools.partial(_in_proj_kernel, n_ctx_steps=n_ctx_steps, row_len=row_len, ctx_row_len=ctx_row_len,
                             hy_cols=hy_cols, d_ssd=d_ssd, d_xbc=d_xbc, tn=tn)
    return pl.pallas_call(
        kern,
        grid=(bsz, n_steps),
        in_specs=[
            pl.BlockSpec((1, tm, D), lambda b, i: (b, jnp.minimum(i, n_ctx_steps - 1), 0)),
            pl.BlockSpec((1, tm, D), lat),
            pl.BlockSpec((1, D), const2),
            pl.BlockSpec((1, D), const2),
            pl.BlockSpec((1, D), const2),
            per_b, per_b,
            pl.BlockSpec(w_bf.shape, const2),
            pl.BlockSpec(wdt_bf.shape, const2),
            pl.BlockSpec(ssd_w.shape, const2),
            pl.BlockSpec(ssd_b.shape, const2),
            pl.BlockSpec((1, LANES), const2),
            pl.BlockSpec((1, LANES), const2),
        ],
        out_specs=[
            pl.BlockSpec((1, tm, hy_cols), lat),
            pl.BlockSpec((1, tm, d_ssd), lat),
            pl.BlockSpec((1, tm, d_xbc), allt),
            pl.BlockSpec((1, tm, LANES), allt),
        ],
        out_shape=[
            jax.ShapeDtypeStruct((bsz, L, hy_cols), jnp.float32),
            jax.ShapeDtypeStruct((bsz, L, d_ssd), jnp.float32),
            jax.ShapeDtypeStruct((bsz, lc + L, d_xbc), jnp.float32),
            jax.ShapeDtypeStruct((bsz, lc + L, LANES), jnp.float32),
        ],
        scratch_shapes=[pltpu.VMEM((tm, D), jnp.bfloat16)],
        compiler_params=pltpu.CompilerParams(
            dimension_semantics=("arbitrary", "arbitrary"), vmem_limit_bytes=VMEM_LIMIT_BYTES),
    )(ctx, x, g1, csh, csc, sh, sc, w_bf, wdt_bf, ssd_w, ssd_b, dt_bias2, dt_mult)


def in_proj_params(w_in, a_log, dt_bias, hy_cols, d_ssd, d_xbc):
    n_h = 2 * SSD_HEADS
    main = hy_cols + d_ssd + d_xbc
    w_dt = w_in[:, main:main + n_h]
    pad = jnp.zeros((w_in.shape[0], LANES - 2 * n_h), w_in.dtype)
    wdt = jnp.concatenate([w_dt, w_dt, pad], axis=1).astype(jnp.bfloat16)
    zpad = jnp.zeros((LANES - 2 * n_h,), jnp.float32)
    bias2 = jnp.concatenate([dt_bias.reshape(n_h), dt_bias.reshape(n_h), zpad])[None, :]
    mult = jnp.concatenate([jnp.ones((n_h,), jnp.float32), -jnp.exp(a_log).reshape(n_h), zpad])[None, :]
    return w_in[:, :main].astype(jnp.bfloat16), wdt, bias2, mult


def _out_router_kernel(yh_ref, ys_ref, z_ref, x_ref, ga_ref, sc_ref, sh_ref, ng_ref, g2_ref, wo_ref, wr_ref, br_ref,
                       x1_ref, hn_ref, ri_ref, rf_ref, cnt_ref, carry_ref):
    first = jnp.logical_and(pl.program_id(0) == 0, pl.program_id(1) == 0)

    @pl.when(first)
    def _():
        carry_ref[...] = jnp.zeros_like(carry_ref)

    bf = jnp.bfloat16
    tm = x_ref.shape[1]
    dh = yh_ref.shape[2]
    z = z_ref[0]
    ys = ys_ref[0] * (z * jax.nn.sigmoid(z))
    gw = ys.shape[1] // SSD_GROUPS
    acc = jnp.dot(yh_ref[0].astype(bf), wo_ref[0:dh, :], preferred_element_type=jnp.float32)
    for g in range(SSD_GROUPS):
        yg = ys[:, g * gw:(g + 1) * gw]
        yg = yg * lax.rsqrt(jnp.mean(yg * yg, axis=-1, keepdims=True) + EPS) * ng_ref[:, g * gw:(g + 1) * gw]
        acc += jnp.dot(yg.astype(bf), wo_ref[dh + g * gw:dh + (g + 1) * gw, :], preferred_element_type=jnp.float32)
    x1 = x_ref[0] + ga_ref[0] * acc
    x1_ref[0] = x1
    hn = x1 * lax.rsqrt(jnp.mean(x1 * x1, axis=-1, keepdims=True) + EPS) * g2_ref[...]
    hn = hn * (1.0 + sc_ref[0]) + sh_ref[0]
    hn_ref[0] = hn

    hn_hi = hn.astype(bf)
    hn_lo = (hn - hn_hi.astype(jnp.float32)).astype(bf)
    logits = (jnp.dot(hn_hi, wr_ref[0], preferred_element_type=jnp.float32)
              + jnp.dot(hn_lo, wr_ref[0], preferred_element_type=jnp.float32)
              + jnp.dot(hn_hi, wr_ref[1], preferred_element_type=jnp.float32)) + br_ref[...]
    lane = lax.broadcasted_iota(jnp.int32, (tm, LANES), 1)
    lane_f = lane.astype(jnp.float32)
    ninf = jnp.float32(-jnp.inf)
    big = jnp.float32(1e9)
    gl = jnp.where(lane < MOE_GROUPS, logits, ninf)
    gmax = jnp.max(gl, axis=-1, keepdims=True)
    p_group = 1.0 / jnp.sum(jnp.exp(gl - gmax), axis=-1, keepdims=True)
    g_sel = jnp.min(jnp.where(gl == gmax, lane_f, big), axis=-1, keepdims=True)
    e_lane = lane - MOE_GROUPS
    in_grp = jnp.logical_and(e_lane >= 0, (e_lane // EXPERTS_PER_GROUP).astype(jnp.float32) == g_sel)
    el = jnp.where(in_grp, logits, ninf)
    m1 = jnp.max(el, axis=-1, keepdims=True)
    i1 = jnp.min(jnp.where(el == m1, lane_f, big), axis=-1, keepdims=True)
    el2 = jnp.where(lane_f == i1, ninf, el)
    m2 = jnp.max(el2, axis=-1, keepdims=True)
    i2 = jnp.min(jnp.where(el2 == m2, lane_f, big), axis=-1, keepdims=True)
    t = jnp.exp(m2 - m1)
    w1 = 1.0 / (1.0 + t)
    gate1 = w1 * p_group
    gate2 = (t * w1) * p_group
    e1 = i1 - MOE_GROUPS
    e2 = i2 - MOE_GROUPS
    el_f = e_lane.astype(jnp.float32)
    oh1 = el_f == e1
    oh2 = el_f == e2
    oh = jnp.logical_or(oh1, oh2).astype(bf)
    r_i = lax.broadcasted_iota(jnp.int32, (tm, tm), 0)
    c_i = lax.broadcasted_iota(jnp.int32, (tm, tm), 1)
    before = jnp.dot((c_i < r_i).astype(bf), oh, preferred_element_type=jnp.float32) + carry_ref[...]
    rank1 = jnp.sum(jnp.where(oh1, before, 0.0), axis=-1, keepdims=True)
    rank2 = jnp.sum(jnp.where(oh2, before, 0.0), axis=-1, keepdims=True)
    carry_ref[...] += jnp.sum(oh.astype(jnp.float32), axis=0, keepdims=True)
    cnt_ref[...] = carry_ref[...]

    rec = jnp.where(lane == 0, e1, jnp.where(lane == 1, e2, jnp.where(lane == 2, rank1,
                                                                      jnp.where(lane == 3, rank2, 0.0))))
    ri_ref[0] = rec.T[0:ROUTE_COLS, :].astype(jnp.int32)
    col = lax.broadcasted_iota(jnp.int32, (tm, ROUTE_COLS), 1)
    rf_ref[0] = jnp.where(col == 0, gate1, gate2)


def out_proj_router(y_hy, y_scan, px, z_col, x, ga1, sc2, sh2, norm_g, g2, w_out_bf, w_router, b_router, tm):
    bsz, L, D = x.shape
    dh = y_hy.shape[-1]
    ds = y_scan.shape[-1]
    tok = lambda b, i: (b, i, 0)
    per_b = pl.BlockSpec((1, 1, D), lambda b, i: (b, 0, 0))
    const2 = lambda b, i: (0, 0)
    return pl.pallas_call(
        _out_router_kernel,
        grid=(bsz, L // tm),
        in_specs=[
            pl.BlockSpec((1, tm, dh), tok),
            pl.BlockSpec((1, tm, ds), tok),
            pl.BlockSpec((1, tm, ds), lambda b, i: (b, i, z_col)),
            pl.BlockSpec((1, tm, D), tok),
            per_b, per_b, per_b,
            pl.BlockSpec((1, ds), const2),
            pl.BlockSpec((1, D), const2),
            pl.BlockSpec((dh + ds, D), const2),
            pl.BlockSpec((2, D, LANES), lambda b, i: (0, 0, 0)),
            pl.BlockSpec((1, LANES), const2),
        ],
        out_specs=[
            pl.BlockSpec((1, tm, D), tok),
            pl.BlockSpec((1, tm, D), tok),
            pl.BlockSpec((1, ROUTE_COLS, tm), lambda b, i: (b, 0, i)),
            pl.BlockSpec((1, tm, ROUTE_COLS), tok),
            pl.BlockSpec((1, LANES), const2),
        ],
        out_shape=[
            jax.ShapeDtypeStruct((bsz, L, D), jnp.float32),
            jax.ShapeDtypeStruct((bsz, L, D), jnp.float32),
            jax.ShapeDtypeStruct((bsz, ROUTE_COLS, L), jnp.int32),
            jax.ShapeDtypeStruct((bsz, L, ROUTE_COLS), jnp.float32),
            jax.ShapeDtypeStruct((1, LANES), jnp.float32),
        ],
        scratch_shapes=[pltpu.VMEM((1, LANES), jnp.float32)],
        compiler_params=pltpu.CompilerParams(
            dimension_semantics=("arbitrary", "arbitrary"), vmem_limit_bytes=VMEM_LIMIT_BYTES),
    )(y_hy, y_scan, px, x, ga1, sc2, sh2, norm_g, g2, w_out_bf, w_router, b_router)


def _row_copy(src_hbm, src_row, dst_ref, dst_row, sem):
    return pltpu.make_async_copy(src_hbm.at[pl.ds(src_row, 1), :], dst_ref.at[pl.ds(dst_row, 1), :], sem)


def _dispatch_kernel(dest_ref, valid_ref, hn_ref, buf_hbm, zeros, sem, zsem):
    step = pl.program_id(0)
    tm = hn_ref.shape[0]
    n_tok = pl.num_programs(0) * tm
    blk = zeros.shape[0]
    n_blocks = buf_hbm.shape[0] // blk

    def zero_copy(i):
        return pltpu.make_async_copy(zeros, buf_hbm.at[pl.ds(pl.multiple_of(i * blk, blk), blk), :], zsem)

    def zfill(i, carry):
        @pl.when(valid_ref[i] < blk)
        def _():
            zero_copy(i).start()
        return carry

    def zwait(i, carry):
        @pl.when(valid_ref[i] < blk)
        def _():
            zero_copy(i).wait()
        return carry

    @pl.when(step == 0)
    def _():
        zeros[...] = jnp.zeros_like(zeros)
        lax.fori_loop(0, n_blocks, zfill, 0)
        lax.fori_loop(0, n_blocks, zwait, 0)

    def body(j, carry):
        t = step * tm + j
        _row_copy(hn_ref, j, buf_hbm, dest_ref[t], sem).start()
        _row_copy(hn_ref, j, buf_hbm, dest_ref[n_tok + t], sem).start()
        return carry

    lax.fori_loop(0, tm, body, 0, unroll=8)
    for _ in range(2):
        pltpu.make_async_copy(hn_ref, buf_hbm.at[pl.ds(0, tm), :], sem).wait()


def moe_dispatch(hn, dest, block_valid, blk, tm):
    T, D = hn.shape
    n_rows = block_valid.shape[0] * blk
    grid_spec = pltpu.PrefetchScalarGridSpec(
        num_scalar_prefetch=2,
        grid=(T // tm,),
        in_specs=[pl.BlockSpec((tm, D), lambda i, d, v: (i, 0))],
        out_specs=pl.BlockSpec(memory_space=pl.ANY),
        scratch_shapes=[pltpu.VMEM((blk, D), jnp.float32), pltpu.SemaphoreType.DMA(()),
                        pltpu.SemaphoreType.DMA(())],
    )
    return pl.pallas_call(
        _dispatch_kernel,
        grid_spec=grid_spec,
        out_shape=jax.ShapeDtypeStruct((n_rows, D), jnp.float32),
        compiler_params=pltpu.CompilerParams(dimension_semantics=("arbitrary",), has_side_effects=True),
    )(dest, block_valid, hn)


def _expert_kernel(eid_ref, first_ref, valid_ref, x_ref, w1_ref, w3_ref, w2_ref, o_ref, w1b, w3b, w2b):
    i = pl.program_id(0)
    del eid_ref
    bf = jnp.bfloat16

    @pl.when(first_ref[i] == 1)
    def _():
        w1b[...] = w1_ref[0].astype(bf)
        w3b[...] = w3_ref[0].astype(bf)
        w2b[...] = w2_ref[0].astype(bf)

    valid = valid_ref[i]

    @pl.when(valid > 0)
    def _():
        xb = x_ref[...].astype(bf)
        a = jnp.dot(xb, w1b[...], preferred_element_type=jnp.float32)
        b = jnp.dot(xb, w3b[...], preferred_element_type=jnp.float32)
        h = (a * jax.nn.sigmoid(a)) * b
        o_ref[...] = jnp.dot(h.astype(bf), w2b[...], preferred_element_type=jnp.float32)

    @pl.when(valid <= 0)
    def _():
        o_ref[...] = jnp.zeros_like(o_ref)


def expert_blocks(buf, block_eid, block_first, block_valid, w1, w3, w2, blk):
    rows, D = buf.shape
    n_blocks = rows // blk
    E, _, F = w1.shape
    grid_spec = pltpu.PrefetchScalarGridSpec(
        num_scalar_prefetch=3,
        grid=(n_blocks,),
        in_specs=[
            pl.BlockSpec((blk, D), lambda i, eid, fi, va: (i, 0)),
            pl.BlockSpec((1, D, F), lambda i, eid, fi, va: (eid[i], 0, 0)),
            pl.BlockSpec((1, D, F), lambda i, eid, fi, va: (eid[i], 0, 0)),
            pl.BlockSpec((1, F, D), lambda i, eid, fi, va: (eid[i], 0, 0)),
        ],
        out_specs=pl.BlockSpec((blk, D), lambda i, eid, fi, va: (i, 0)),
        scratch_shapes=[pltpu.VMEM((D, F), jnp.bfloat16), pltpu.VMEM((D, F), jnp.bfloat16),
                        pltpu.VMEM((F, D), jnp.bfloat16)],
    )
    return pl.pallas_call(
        _expert_kernel,
        grid_spec=grid_spec,
        out_shape=jax.ShapeDtypeStruct((rows, D), jnp.float32),
        compiler_params=pltpu.CompilerParams(
            dimension_semantics=("arbitrary",), vmem_limit_bytes=VMEM_LIMIT_BYTES),
    )(block_eid, block_first, block_valid, buf, w1, w3, w2)


def _combine_kernel(dest_ref, x1_ref, rf_ref, ga_ref, gf_ref, yb_hbm, o_ref, ybuf, sem):
    b = pl.program_id(0)
    i = pl.program_id(1)
    n_i = pl.num_programs(1)
    tm = x1_ref.shape[1]
    step = b * n_i + i
    n_steps = pl.num_programs(0) * n_i
    slot = step % 2

    def issue(step_, slot_):
        def body(j, carry):
            t = step_ * tm + j
            _row_copy(yb_hbm, dest_ref[t], ybuf.at[slot_, 0], j, sem.at[slot_]).start()
            _row_copy(yb_hbm, dest_ref[n_steps * tm + t], ybuf.at[slot_, 1], j, sem.at[slot_]).start()
            return carry
        lax.fori_loop(0, tm, body, 0, unroll=8)

    @pl.when(step == 0)
    def _():
        issue(0, 0)

    @pl.when(step + 1 < n_steps)
    def _():
        issue(step + 1, 1 - slot)

    pltpu.make_async_copy(yb_hbm.at[pl.ds(0, tm), :], ybuf.at[slot, 0], sem.at[slot]).wait()
    pltpu.make_async_copy(yb_hbm.at[pl.ds(0, tm), :], ybuf.at[slot, 1], sem.at[slot]).wait()
    rf = rf_ref[0]
    y = rf[:, 0:1] * ybuf[slot, 0] + rf[:, 1:2] * ybuf[slot, 1]
    x2 = x1_ref[0] + ga_ref[0] * y
    o_ref[0] = x2 * lax.rsqrt(jnp.mean(x2 * x2, axis=-1, keepdims=True) + EPS) * gf_ref[...]


def moe_combine(x1, route_f, ga2, g_final, yb, dest, tm):
    bsz, L, D = x1.shape
    grid_spec = pltpu.PrefetchScalarGridSpec(
        num_scalar_prefetch=1,
        grid=(bsz, L // tm),
        in_specs=[
            pl.BlockSpec((1, tm, D), lambda b, i, d: (b, i, 0)),
            pl.BlockSpec((1, tm, ROUTE_COLS), lambda b, i, d: (b, i, 0)),
            pl.BlockSpec((1, 1, D), lambda b, i, d: (b, 0, 0)),
            pl.BlockSpec((1, D), lambda b, i, d: (0, 0)),
            pl.BlockSpec(memory_space=pl.ANY),
        ],
        out_specs=pl.BlockSpec((1, tm, D), lambda b, i, d: (b, i, 0)),
        scratch_shapes=[pltpu.VMEM((2, 2, tm, D), jnp.float32), pltpu.SemaphoreType.DMA((2,))],
    )
    return pl.pallas_call(
        _combine_kernel,
        grid_spec=grid_spec,
        out_shape=jax.ShapeDtypeStruct((bsz, L, D), jnp.float32),
        compiler_params=pltpu.CompilerParams(
            dimension_semantics=("arbitrary", "arbitrary"), vmem_limit_bytes=VMEM_LIMIT_BYTES),
    )(dest, x1, route_f, ga2, g_final, yb)


def moe_plan(route_i, counts, blk, n_blocks):
    cnt = counts[0, MOE_GROUPS:MOE_GROUPS + N_EXPERTS].astype(jnp.int32)
    padded = (cnt + blk - 1) // blk * blk
    ends = jnp.cumsum(padded)
    starts = ends - padded
    experts = jnp.arange(N_EXPERTS, dtype=jnp.int32)
    dest = jnp.concatenate([
        (jnp.sum(jnp.where(route_i[:, k, :, None] == experts, starts, 0), axis=-1) + route_i[:, 2 + k]).reshape(-1)
        for k in range(TOP_K)])
    first_row = jnp.arange(n_blocks, dtype=jnp.int32) * blk
    block_eid = jnp.minimum(jnp.sum((ends[None, :] <= first_row[:, None]).astype(jnp.int32), axis=1), N_EXPERTS - 1)
    block_valid = jnp.clip(cnt[block_eid] - (first_row - starts[block_eid]), 0, blk).astype(jnp.int32)
    block_first = jnp.concatenate([jnp.ones((1,), jnp.int32),
                                   (block_eid[1:] != block_eid[:-1]).astype(jnp.int32)])
    return dest, block_eid, block_first, block_valid


def dft_tables(L):
    n = 2 * L
    f = lax.broadcasted_iota(jnp.int32, (L, L), 0)
    t = lax.broadcasted_iota(jnp.int32, (L, L), 1)
    ang = ((f * t) % n).astype(jnp.float32) * (2.0 * math.pi / n)
    return jnp.cos(ang).astype(jnp.bfloat16), jnp.sin(ang).astype(jnp.bfloat16)


def _alt_sign(L):
    t = lax.broadcasted_iota(jnp.int32, (L, 1), 0)
    return (1 - 2 * (t & 1)).astype(jnp.float32)


def _spectrum_kernel(a_ref, b_ref, c_ref, s_ref, kr_ref, ks_ref, kn_ref):
    L = a_ref.shape[1]
    a = a_ref[0]
    row = lax.broadcasted_iota(jnp.int32, (L, 1), 0)
    scale = jnp.where(row == 0, 0.5 / L, 1.0 / L)
    kr_ref[0] = scale * jnp.dot(c_ref[...], a.astype(jnp.bfloat16), preferred_element_type=jnp.float32)
    ks_ref[0] = scale * jnp.dot(s_ref[...], b_ref[0].astype(jnp.bfloat16), preferred_element_type=jnp.float32)
    kn_ref[0] = jnp.sum(a * _alt_sign(L), axis=0, keepdims=True) * (0.5 / L)


def filter_spectrum(a, b, cos_t, sin_t, tc):
    n, L, C = a.shape
    blk = pl.BlockSpec((1, L, tc), lambda o, j: (o, 0, j))
    tab = pl.BlockSpec((L, L), lambda o, j: (0, 0))
    return pl.pallas_call(
        _spectrum_kernel,
        grid=(n, C // tc),
        in_specs=[blk, blk, tab, tab],
        out_specs=[blk, blk, pl.BlockSpec((1, 1, tc), lambda o, j: (o, 0, j))],
        out_shape=[jax.ShapeDtypeStruct((n, L, C), jnp.float32)] * 2 + [jax.ShapeDtypeStruct((n, 1, C), jnp.float32)],
        compiler_params=pltpu.CompilerParams(
            dimension_semantics=("arbitrary", "arbitrary"), vmem_limit_bytes=VMEM_LIMIT_BYTES),
    )(a, b, cos_t, sin_t)


def _phase_conv3(raw, w_ref, b_ref, rows_per_phase):
    n_ph = len(raw)
    h = raw[0].shape[0]
    j = lax.broadcasted_iota(jnp.int32, (h, 1), 0) % rows_per_phase
    prev0 = jnp.where(j != 0, pltpu.roll(raw[n_ph - 1], 1, 0), 0.0)
    next_last = jnp.where(j != rows_per_phase - 1, pltpu.roll(raw[0], h - 1, 0), 0.0)
    out = []
    for p in range(n_ph):
        prev = raw[p - 1] if p > 0 else prev0
        nxt = raw[p + 1] if p < n_ph - 1 else next_last
        out.append(b_ref[...] + w_ref[0:1, :] * prev + w_ref[1:2, :] * raw[p] + w_ref[2:3, :] * nxt)
    return out


def _long_conv_kernel(*refs, n_ph, n_slab, conv_z, rows_per_phase):
    z_refs = refs[:n_slab]
    xn_refs = refs[n_slab:2 * n_slab]
    (kr_ref, ks_ref, kn_ref, bias_ref, cwz_ref, cbz_ref, cwx_ref, cbx_ref, c_ref, s_ref,
     o_ref, acc_ref, zr_ref, zs_ref, yr_ref, ys_ref, stage_ref) = refs[2 * n_slab:]
    H = z_refs[0].shape[1] // n_ph
    f32 = jnp.float32
    bf = jnp.bfloat16
    sign = _alt_sign(H)

    def phases(slab_refs):
        return [jnp.concatenate([r[0, pl.ds(p, H, stride=n_ph), :] for r in slab_refs], axis=1)
                for p in range(n_ph)]

    z_ph = phases(z_refs)
    if conv_z:
        z_ph = _phase_conv3(z_ph, cwz_ref, cbz_ref, rows_per_phase)
    for q in range(n_ph):
        zb = z_ph[q].astype(bf)
        zr_ref[q] = jnp.dot(c_ref[...], zb, preferred_element_type=f32)
        zs_ref[q] = jnp.dot(s_ref[...], zb, preferred_element_type=f32)
    z_nyq = [jnp.sum(z * sign, axis=0, keepdims=True) for z in z_ph]
    for p in range(n_ph):
        nyq = sum(z_nyq[q] * kn_ref[p - q + n_ph - 1] for q in range(n_ph))
        acc_ref[p] = z_ph[p] * bias_ref[0] + sign * nyq
        yr = 0.0
        ys = 0.0
        for q in range(n_ph):
            slot = p - q + n_ph - 1
            yr = yr + zr_ref[q] * kr_ref[slot] - zs_ref[q] * ks_ref[slot]
            ys = ys + zr_ref[q] * ks_ref[slot] + zs_ref[q] * kr_ref[slot]
        yr_ref[p] = yr.astype(bf)
        ys_ref[p] = ys.astype(bf)
    for p in range(n_ph):
        acc_ref[p] += (jnp.dot(c_ref[...], yr_ref[p], preferred_element_type=f32)
                       + jnp.dot(s_ref[...], ys_ref[p], preferred_element_type=f32))
    x_ph = _phase_conv3(phases(xn_refs), cwx_ref, cbx_ref, rows_per_phase)
    for p in range(n_ph):
        out_p = x_ph[p] * acc_ref[p]
        for sl in range(n_slab):
            stage_ref[sl, pl.ds(p, H, stride=n_ph), :] = out_p[:, sl * LANES:(sl + 1) * LANES]
    for sl in range(n_slab):
        o_ref[0, :, sl * LANES:(sl + 1) * LANES] = stage_ref[sl]


def long_conv_gate(z_arr, z_col, conv_z, xn_arr, xn_col, conv_w, conv_b, kr, ks, kn, bias, cos_t, sin_t,
                   tc, n_ph, row_len):
    bsz, L, _ = z_arr.shape
    H = L // n_ph
    C = kr.shape[-1]
    nj = C // tc
    n_slab = tc // LANES
    n_f = 2 * n_ph - 1
    tab = pl.BlockSpec((H, H), lambda j, b: (0, 0), pipeline_mode=pl.Buffered(1))
    spec = pl.BlockSpec((n_f, H, tc), lambda j, b: (0, 0, j), pipeline_mode=pl.Buffered(1))
    nyq = pl.BlockSpec((n_f, 1, tc), lambda j, b: (0, 0, j))
    vec = pl.BlockSpec((1, 1, tc), lambda j, b: (0, 0, j))

    def slabs(col):
        return [pl.BlockSpec((1, L, LANES),
                             functools.partial(lambda j, b, sl: (b, 0, (col * nj + j) * n_slab + sl), sl=sl))
                for sl in range(n_slab)]

    def conv_specs(col):
        return [pl.BlockSpec((3, tc), lambda j, b: (0, col * nj + j)),
                pl.BlockSpec((1, tc), lambda j, b: (0, col * nj + j))]

    zc = z_col if conv_z else 0
    kern = functools.partial(_long_conv_kernel, n_ph=n_ph, n_slab=n_slab, conv_z=conv_z,
                             rows_per_phase=row_len // n_ph)
    return pl.pallas_call(
        kern,
        grid=(nj, bsz),
        in_specs=(slabs(z_col) + slabs(xn_col) + [spec, spec, nyq, vec] + conv_specs(zc) + conv_specs(xn_col)
                  + [tab, tab]),
        out_specs=pl.BlockSpec((1, L, tc), lambda j, b: (b, 0, j)),
        out_shape=jax.ShapeDtypeStruct((bsz, L, C), jnp.float32),
        scratch_shapes=[pltpu.VMEM((n_ph, H, tc), jnp.float32), pltpu.VMEM((n_ph, H, tc), jnp.float32),
                        pltpu.VMEM((n_ph, H, tc), jnp.float32), pltpu.VMEM((n_ph, H, tc), jnp.bfloat16),
                        pltpu.VMEM((n_ph, H, tc), jnp.bfloat16), pltpu.VMEM((n_slab, L, LANES), jnp.float32)],
        compiler_params=pltpu.CompilerParams(
            dimension_semantics=("arbitrary", "arbitrary"), vmem_limit_bytes=VMEM_LIMIT_BYTES),
    )(*([z_arr] * n_slab), *([xn_arr] * n_slab), kr, ks, kn, bias, conv_w, conv_b, conv_w, conv_b, cos_t, sin_t)


def _polyphase_taps(kf, kb, n_ph):
    H = kf.shape[0] // n_ph
    ph = lambda a, p: a[p * H:(p + 1) * H]
    zero = jnp.zeros_like(kf[:1])
    plus, minus = [], []
    for r in range(-(n_ph - 1), n_ph):
        if r >= 0:
            plus.append(ph(kf, r))
        else:
            plus.append(jnp.concatenate([ph(kb, -r)[0:1], ph(kf, n_ph + r)[:-1]], axis=0))
        if r <= 0:
            minus.append(jnp.concatenate([zero, ph(kb, -r)[1:]], axis=0))
        else:
            minus.append(jnp.concatenate([zero, ph(kb, n_ph - r)[:-1]], axis=0))
    return jnp.stack(plus), jnp.stack(minus)


def hyena_long_convs(p_hy, conv_w, conv_b, kp, h_bias, tc, n_ph, row_len):
    L = p_hy.shape[1]
    C = h_bias.shape[1]
    cos_t, sin_t = dft_tables(L // n_ph)
    z = p_hy
    for o in range(h_bias.shape[0]):
        fwd = slice((2 * o) * C, (2 * o + 1) * C)
        bwd = slice((2 * o + 1) * C, (2 * o + 2) * C)
        plus, minus = _polyphase_taps(kp[:, fwd], kp[:, bwd], n_ph)
        kr, ks, kn = filter_spectrum(plus + minus, plus - minus, cos_t, sin_t, tc)
        z = long_conv_gate(z, 0, o == 0, p_hy, o + 1, conv_w, conv_b, kr, ks, kn, h_bias[o][None, None, :],
                           cos_t, sin_t, tc, n_ph, row_len)
    return z


def _filter_kernel(band_ref, w1_ref, b1_ref, fr_ref, w2_ref, b2_ref, w3_ref, dl_ref, k_ref, *, seq_len, n_ph):
    hp = lax.Precision.HIGHEST
    f32 = jnp.float32
    tp = k_ref.shape[0]
    per_phase = seq_len // n_ph
    g = lax.broadcasted_iota(jnp.int32, (tp, 1), 0) + pl.program_id(0) * tp
    phase = g // per_phase
    pos = (n_ph * (g - phase * per_phase) + phase).astype(f32)
    t = pos / max(seq_len - 1, 1)
    ang = (2 * math.pi / seq_len) * pos * band_ref[...]
    lane = lax.broadcasted_iota(jnp.int32, (tp, LANES), 1)
    feats = jnp.where(lane == 0, t,
                      jnp.where(lane <= HYENA_BANDS, jnp.cos(ang),
                                jnp.where(lane <= 2 * HYENA_BANDS, -jnp.sin(ang), 0.0)))
    h = jnp.sin(fr_ref[...] * (jnp.dot(feats, w1_ref[...], precision=hp, preferred_element_type=f32) + b1_ref[...]))
    h = jnp.sin(fr_ref[...] * (jnp.dot(h, w2_ref[...], precision=hp, preferred_element_type=f32) + b2_ref[...]))
    window = jnp.exp(-t * dl_ref[...])
    c = dl_ref.shape[1]
    for j in range(w3_ref.shape[1] // c):
        cols = slice(j * c, (j + 1) * c)
        k_ref[:, cols] = jnp.dot(h, w3_ref[:, cols], precision=hp, preferred_element_type=f32) * window


def hyena_filters_polyphase(seq_len, f_w1, f_b1, f_freq, f_w2, f_b2, f_w3, d_hyena, tp, n_ph):
    f32 = jnp.float32
    fh = f_w1.shape[1]
    n_emb = 1 + 2 * HYENA_BANDS
    bands = jnp.linspace(1e-4, HYENA_BANDS - 1, HYENA_BANDS, dtype=f32)
    band_row = jnp.concatenate([jnp.zeros((1,), f32), bands, bands, jnp.zeros((LANES - n_emb,), f32)])[None, :]
    w1p = jnp.concatenate([f_w1, jnp.zeros((LANES - n_emb, fh), f32)], axis=0)
    deltas = jnp.abs(jnp.linspace(math.log(HYENA_TARGET) / HYENA_SLOW_DECAY,
                                  math.log(HYENA_TARGET) / HYENA_FAST_DECAY, d_hyena, dtype=f32))[None, :]
    n_out = f_w3.shape[1]
    full = lambda a: pl.BlockSpec(a.shape, lambda i: (0,) * a.ndim)
    args = (band_row, w1p, f_b1[None, :], f_freq[None, :], f_w2, f_b2[None, :], f_w3, deltas)
    return pl.pallas_call(
        functools.partial(_filter_kernel, seq_len=seq_len, n_ph=n_ph),
        grid=(seq_len // tp,),
        in_specs=[full(a) for a in args],
        out_specs=pl.BlockSpec((tp, n_out), lambda i: (i, 0)),
        out_shape=jax.ShapeDtypeStruct((seq_len, n_out), f32),
        compiler_params=pltpu.CompilerParams(dimension_semantics=("arbitrary",), vmem_limit_bytes=VMEM_LIMIT_BYTES),
    )(*args)


def _ada_kernel(c_ref, w_ref, b_ref, o_ref):
    cv = c_ref[...]
    s = cv * jax.nn.sigmoid(cv)
    o_ref[...] = jnp.dot(s, w_ref[...], precision=lax.Precision.HIGHEST,
                         preferred_element_type=jnp.float32) + b_ref[...]


def ada_modulation(c_rows, w_ada, b_ada, tn):
    rows, D = c_rows.shape
    N = w_ada.shape[1]
    return pl.pallas_call(
        _ada_kernel,
        grid=(N // tn,),
        in_specs=[pl.BlockSpec((rows, D), lambda j: (0, 0)),
                  pl.BlockSpec((D, tn), lambda j: (0, j)),
                  pl.BlockSpec((1, tn), lambda j: (0, j))],
        out_specs=pl.BlockSpec((rows, tn), lambda j: (0, j)),
        out_shape=jax.ShapeDtypeStruct((rows, N), jnp.float32),
        compiler_params=pltpu.CompilerParams(dimension_semantics=("arbitrary",), vmem_limit_bytes=VMEM_LIMIT_BYTES),
    )(c_rows, w_ada, b_ada[None, :])


def _ssd_kernel(xf_ref, df_ref, xb_ref, db_ref, dskip_ref, y_ref, h_ref, *, n_ctx_chunks):
    s = pl.program_id(1)
    n_steps = pl.num_programs(1)
    Q, G, R, P, N = SSD_CHUNK, SSD_GROUPS, SSD_HPG, SSD_HEAD_DIM, SSD_STATE
    GP = R * P
    bf = jnp.bfloat16

    @pl.when(s == 0)
    def _():
        h_ref[...] = jnp.zeros_like(h_ref)
        y_ref[...] = jnp.zeros_like(y_ref)

    row = lax.broadcasted_iota(jnp.int32, (Q, Q), 0)
    col = lax.broadcasted_iota(jnp.int32, (Q, Q), 1)
    lane_head = lax.broadcasted_iota(jnp.int32, (Q, GP), 1) // P
    is_latent = s >= n_ctx_chunks
    n_lat = n_steps - n_ctx_chunks
    out_chunk = (jnp.clip(s - n_ctx_chunks, 0, n_lat - 1), jnp.clip(n_steps - 1 - s, 0, n_lat - 1))

    for d, (x_ref, da_ref) in enumerate(((xf_ref, df_ref), (xb_ref, db_ref))):
        mask = (row >= col) if d == 0 else (col >= row)
        tri = mask.astype(jnp.float32)
        da = da_ref[0]
        cum = jnp.dot(tri, da, precision=lax.Precision.HIGHEST, preferred_element_type=jnp.float32)
        cum_t = cum.T
        edge = Q - 1 if d == 0 else 0
        blk = x_ref.at[0]
        for g in range(G):
            xg = blk[:, g * GP:(g + 1) * GP]
            bg = blk[:, D_SSD + g * N:D_SSD + (g + 1) * N].astype(bf)
            cg = blk[:, D_SSD + G * N + g * N:D_SSD + G * N + (g + 1) * N].astype(bf)
            heads = [d * SSD_HEADS + g * R + r for r in range(R)]
            dtm = jnp.zeros((Q, GP), jnp.float32)
            cumm = jnp.zeros((Q, GP), jnp.float32)
            totm = jnp.zeros((Q, GP), jnp.float32)
            for r, h in enumerate(heads):
                sel = lane_head == r
                dtm = jnp.where(sel, da[:, h:h + 1], dtm)
                cumm = jnp.where(sel, cum[:, SSD_HEADS * 2 + h:SSD_HEADS * 2 + h + 1], cumm)
                totm = jnp.where(sel, cum[edge:edge + 1, SSD_HEADS * 2 + h:SSD_HEADS * 2 + h + 1], totm)
            xdt = xg * dtm
            hg = h_ref[d, g * GP:(g + 1) * GP, :]

            gmat = lax.dot_general(cg, bg, (((1,), (1,)), ((), ())), preferred_element_type=jnp.float32)
            y_off = lax.dot_general(cg, hg.astype(bf), (((1,), (1,)), ((), ())),
                                    preferred_element_type=jnp.float32) * jnp.exp(cumm)
            if d == 0:
                y_off = y_off + dskip_ref[:, g * GP:(g + 1) * GP] * xg
            parts = []
            for r, h in enumerate(heads):
                a_col = cum[:, SSD_HEADS * 2 + h:SSD_HEADS * 2 + h + 1]
                a_row = cum_t[SSD_HEADS * 2 + h:SSD_HEADS * 2 + h + 1, :]
                decay = jnp.exp(jnp.where(mask, a_col - a_row, NEG_BIG))
                parts.append(jnp.dot((gmat * decay).astype(bf), xdt[:, r * P:(r + 1) * P].astype(bf),
                                     preferred_element_type=jnp.float32))
            y = jnp.where(is_latent, y_off + jnp.concatenate(parts, axis=-1), 0.0)
            rows = pl.ds(pl.multiple_of(out_chunk[d] * Q, Q), Q)
            y_ref[0, rows, g * GP:(g + 1) * GP] += y

            xw = (xdt * jnp.exp(totm - cumm)).astype(bf)
            st = lax.dot_general(xw, bg, (((0,), (0,)), ((), ())), preferred_element_type=jnp.float32)
            for r, h in enumerate(heads):
                dec = jnp.exp(cum_t[SSD_HEADS * 2 + h:SSD_HEADS * 2 + h + 1, edge:edge + 1])
                rs = slice(g * GP + r * P, g * GP + (r + 1) * P)
                h_ref[d, rs, :] = h_ref[d, rs, :] * dec + st[r * P:(r + 1) * P, :]


def ssd_scan_bidir(xbc, dta, d_skip, n_ctx, lat_off):
    bsz, lt, width = xbc.shape
    Q = SSD_CHUNK
    n_ctx_chunks = n_ctx // Q
    L = lt - lat_off
    n_lat = L // Q
    n_steps = n_ctx_chunks + n_lat
    lat0 = lat_off // Q

    def fwd_chunk(s):
        return jnp.where(s < n_ctx_chunks, s, s - n_ctx_chunks + lat0)

    def bwd_chunk(s):
        return jnp.where(s < n_ctx_chunks, n_ctx_chunks - 1 - s, n_steps - 1 - s + lat0)

    return pl.pallas_call(
        functools.partial(_ssd_kernel, n_ctx_chunks=n_ctx_chunks),
        grid=(bsz, n_steps),
        in_specs=[
            pl.BlockSpec((1, Q, width), lambda b, s: (b, fwd_chunk(s), 0)),
            pl.BlockSpec((1, Q, LANES), lambda b, s: (b, fwd_chunk(s), 0)),
            pl.BlockSpec((1, Q, width), lambda b, s: (b, bwd_chunk(s), 0)),
            pl.BlockSpec((1, Q, LANES), lambda b, s: (b, bwd_chunk(s), 0)),
            pl.BlockSpec((1, D_SSD), lambda b, s: (0, 0)),
        ],
        out_specs=pl.BlockSpec((1, L, D_SSD), lambda b, s: (b, 0, 0)),
        out_shape=jax.ShapeDtypeStruct((bsz, L, D_SSD), jnp.float32),
        scratch_shapes=[pltpu.VMEM((2, SSD_GROUPS * SSD_HPG * SSD_HEAD_DIM, SSD_STATE), jnp.float32)],
        compiler_params=pltpu.CompilerParams(
            dimension_semantics=("arbitrary", "arbitrary"), vmem_limit_bytes=VMEM_LIMIT_BYTES),
    )(xbc, dta, xbc, dta, d_skip)


def kernel(x, c, ctx, c_ctx, w_ada, b_ada, g_norm1, g_norm2, w_in, hy_conv_w, hy_conv_b, hy_f_w1, hy_f_b1, hy_f_freq, hy_f_w2, hy_f_b2, hy_f_w3, hy_bias, ssd_conv_w, ssd_conv_b, ssd_a_log, ssd_dt_bias, ssd_d, ssd_norm_g, w_out, w_group, b_group, w_expert, b_expert, w1, w3, w2, g_final):
    bsz, seq_len, _ = x.shape
    l = 0
    rows_pad = -(bsz + 1) % SUBLANES
    c_rows = jnp.concatenate([c, c_ctx[None, :], jnp.zeros((rows_pad, D_MODEL), jnp.float32)], axis=0)
    mod_all = ada_modulation(c_rows, w_ada[l], b_ada[l], 512)
    sh1, sc1, ga1, sh2, sc2, ga2 = jnp.split(mod_all[:bsz, None, :], 6, axis=-1)
    csh1, csc1 = mod_all[bsz, :D_MODEL], mod_all[bsz, D_MODEL:2 * D_MODEL]

    w_out_bf = w_out[l].astype(jnp.bfloat16)
    w_in_bf, w_dt_bf, dt_bias2, dt_mult = in_proj_params(w_in[l], ssd_a_log[l], ssd_dt_bias[l],
                                                         HY_COLS, D_SSD, D_XBC)

    ctx_len = ctx.shape[1]
    ctx_pad = -ctx_len % IN_PROJ_ROWS
    ctx_p = jnp.pad(ctx, ((0, 0), (0, ctx_pad), (0, 0)))
    p_hy, z, xbc, dta = in_proj_fused(ctx_p, x, g_norm1[l][None, :], csh1[None, :], csc1[None, :], sh1, sc1,
                                      w_in_bf, w_dt_bf, HY_COLS, ssd_conv_w[l], ssd_conv_b[l][None, :],
                                      dt_bias2, dt_mult, GRID_W, ctx_len, IN_PROJ_ROWS, 512)
    kp = hyena_filters_polyphase(seq_len, hy_f_w1[l], hy_f_b1[l], hy_f_freq[l], hy_f_w2[l], hy_f_b2[l],
                                 hy_f_w3[l], D_HYENA, 256, HYENA_PHASES)
    y_hy = hyena_long_convs(p_hy, hy_conv_w[l], hy_conv_b[l][None, :], kp, hy_bias[l], 256, HYENA_PHASES, GRID_W)
    y_scan = ssd_scan_bidir(xbc, dta, jnp.repeat(ssd_d[l], SSD_HEAD_DIM)[None, :], ctx_len, ctx_len + ctx_pad)

    pad = LANES - MOE_GROUPS - N_EXPERTS
    w_router = jnp.concatenate([w_group[l], w_expert[l], jnp.zeros((D_MODEL, pad), jnp.float32)], axis=1)
    w_router_hi = w_router.astype(jnp.bfloat16)
    w_router_lo = (w_router - w_router_hi.astype(jnp.float32)).astype(jnp.bfloat16)
    w_router = jnp.stack([w_router_hi, w_router_lo])
    b_router = jnp.concatenate([b_group[l], b_expert[l], jnp.zeros((pad,), jnp.float32)])[None, :]
    x1, hn, route_i, route_f, counts = out_proj_router(
        y_hy, y_scan, z, 0, x, ga1, sc2, sh2, ssd_norm_g[l][None, :], g_norm2[l][None, :],
        w_out_bf, w_router, b_router, 256)
    n_tok = bsz * seq_len
    n_blocks = -(-n_tok * TOP_K // MOE_BLOCK) + N_EXPERTS
    dest, block_eid, block_first, block_valid = moe_plan(route_i, counts, MOE_BLOCK, n_blocks)
    buf = moe_dispatch(hn.reshape(n_tok, D_MODEL), dest, block_valid, MOE_BLOCK, 256)
    yb = expert_blocks(buf, block_eid, block_first, block_valid, w1[l], w3[l], w2[l], MOE_BLOCK)
    return moe_combine(x1, route_f, ga2, g_final[None, :], yb, dest, 256)
```

```python
import functools
import math

import jax
import jax.numpy as jnp
from jax import lax
from jax.experimental import pallas as pl
from jax.experimental.pallas import tpu as pltpu

D_MODEL = 1024
CTX_LEN = 256
GRID_W = 64
EPS = 1e-6
SHORT_CONV = 3

D_HYENA = D_MODEL // 2
HYENA_ORDER = 2
HYENA_BANDS = 8
HYENA_FAST_DECAY = 0.3
HYENA_SLOW_DECAY = 1.5
HYENA_TARGET = 1e-2
HYENA_PHASES = 4

D_SSD = D_MODEL // 2
SSD_HEAD_DIM = 64
SSD_HEADS = D_SSD // SSD_HEAD_DIM
SSD_GROUPS = 2
SSD_HPG = SSD_HEADS // SSD_GROUPS
SSD_STATE = 128
SSD_CHUNK = 128

D_XBC = D_SSD + 2 * SSD_GROUPS * SSD_STATE
HY_COLS = (HYENA_ORDER + 1) * D_HYENA
D_IN = HY_COLS + D_SSD + D_XBC + 2 * SSD_HEADS
LANES = 128
SUBLANES = 8
D_IN_PAD = -(-D_IN // LANES) * LANES

MOE_GROUPS = 8
EXPERTS_PER_GROUP = 8
N_EXPERTS = MOE_GROUPS * EXPERTS_PER_GROUP
TOP_K = 2
D_EXPERT = 512
MOE_BLOCK = 256
ROUTE_COLS = 8
IN_PROJ_ROWS = 512

VMEM_LIMIT_BYTES = 56 * 1024 * 1024
NEG_BIG = -1e30


def _conv3_rows(p, w_ref, b_ref, cols, has_prev, has_next):
    n = p.shape[0]
    prev = jnp.where(has_prev, pltpu.roll(p, 1, 0), 0.0)
    nxt = jnp.where(has_next, pltpu.roll(p, n - 1, 0), 0.0)
    return b_ref[:, cols] + w_ref[0:1, cols] * prev + w_ref[1:2, cols] * p + w_ref[2:3, cols] * nxt


def _in_proj_kernel(ctx_ref, x_ref, g_ref, csh_ref, csc_ref, sh_ref, sc_ref, w_ref, wdt_ref,
                    sw_ref, sb_ref, dtb_ref, dtm_ref, u_ref, z_ref, xbc_ref, dta_ref, h_ref,
                    *, n_ctx_steps, row_len, ctx_row_len, hy_cols, d_ssd, d_xbc, tn):
    i = pl.program_id(1)
    is_ctx = i < n_ctx_steps
    tm = x_ref.shape[1]
    xin = jnp.where(is_ctx, ctx_ref[0], x_ref[0])
    shift = jnp.where(is_ctx, csh_ref[...], sh_ref[0])
    scale = jnp.where(is_ctx, csc_ref[...], sc_ref[0])
    y = xin * lax.rsqrt(jnp.mean(xin * xin, axis=-1, keepdims=True) + EPS) * g_ref[...]
    h_ref[...] = (y * (1.0 + scale) + shift).astype(jnp.bfloat16)

    pos = lax.broadcasted_iota(jnp.int32, (tm, 1), 0) + jnp.where(is_ctx, i, i - n_ctx_steps) * tm
    in_row = jnp.where(is_ctx, pos % ctx_row_len, pos % row_len)
    has_prev = in_row != 0
    has_next = in_row != jnp.where(is_ctx, ctx_row_len - 1, row_len - 1)

    for c0 in range(0, hy_cols, tn):
        cols = slice(c0, c0 + tn)
        u_ref[0, :, cols] = jnp.dot(h_ref[...], w_ref[:, cols], preferred_element_type=jnp.float32)
    z_ref[0] = jnp.dot(h_ref[...], w_ref[:, hy_cols:hy_cols + d_ssd], preferred_element_type=jnp.float32)

    for c0 in range(0, d_xbc, tn):
        cols = slice(c0, c0 + tn)
        wc = slice(hy_cols + d_ssd + c0, hy_cols + d_ssd + c0 + tn)
        p = jnp.dot(h_ref[...], w_ref[:, wc], preferred_element_type=jnp.float32)
        v = _conv3_rows(p, sw_ref, sb_ref, cols, has_prev, has_next)
        xbc_ref[0, :, cols] = v * jax.nn.sigmoid(v)
    pd = jnp.dot(h_ref[...], wdt_ref[...], preferred_element_type=jnp.float32) + dtb_ref[...]
    sp = jnp.maximum(pd, 0.0) + jnp.log(1.0 + jnp.exp(-jnp.abs(pd)))
    dta_ref[0] = sp * dtm_ref[...]


def in_proj_fused(ctx, x, g1, csh, csc, sh, sc, w_bf, wdt_bf, hy_cols, ssd_w, ssd_b, dt_bias2, dt_mult,
                  row_len, ctx_row_len, tm, tn):
    bsz, L, D = x.shape
    lc = ctx.shape[1]
    d_xbc = ssd_w.shape[1]
    d_ssd = w_bf.shape[1] - hy_cols - d_xbc
    n_ctx_steps = lc // tm
    n_steps = n_ctx_steps + L // tm
    lat = lambda b, i: (b, jnp.maximum(i - n_ctx_steps, 0), 0)
    allt = lambda b, i: (b, i, 0)
    const2 = lambda b, i: (0, 0)
    per_b = pl.BlockSpec((1, 1, D), lambda b, i: (b, 0, 0))
    kern = functools.partial(_in_proj_kernel, n_ctx_steps=n_ctx_steps, row_len=row_len, ctx_row_len=ctx_row_len,
                             hy_cols=hy_cols, d_ssd=d_ssd, d_xbc=d_xbc, tn=tn)
    return pl.pallas_call(
        kern,
        grid=(bsz, n_steps),
        in_specs=[
            pl.BlockSpec((1, tm, D), lambda b, i: (b, jnp.minimum(i, n_ctx_steps - 1), 0)),
            pl.BlockSpec((1, tm, D), lat),
            pl.BlockSpec((1, D), const2),
            pl.BlockSpec((1, D), const2),
            pl.BlockSpec((1, D), const2),
            per_b, per_b,
            pl.BlockSpec(w_bf.shape, const2),
            pl.BlockSpec(wdt_bf.shape, const2),
            pl.BlockSpec(ssd_w.shape, const2),
            pl.BlockSpec(ssd_b.shape, const2),
            pl.BlockSpec((1, LANES), const2),
            pl.BlockSpec((1, LANES), const2),
        ],
        out_specs=[
            pl.BlockSpec((1, tm, hy_cols), lat),
            pl.BlockSpec((1, tm, d_ssd), lat),
            pl.BlockSpec((1, tm, d_xbc), allt),
            pl.BlockSpec((1, tm, LANES), allt),
        ],
        out_shape=[
            jax.ShapeDtypeStruct((bsz, L, hy_cols), jnp.float32),
            jax.ShapeDtypeStruct((bsz, L, d_ssd), jnp.float32),
            jax.ShapeDtypeStruct((bsz, lc + L, d_xbc), jnp.float32),
            jax.ShapeDtypeStruct((bsz, lc + L, LANES), jnp.float32),
        ],
        scratch_shapes=[pltpu.VMEM((tm, D), jnp.bfloat16)],
        compiler_params=pltpu.CompilerParams(
            dimension_semantics=("arbitrary", "arbitrary"), vmem_limit_bytes=VMEM_LIMIT_BYTES),
    )(ctx, x, g1, csh, csc, sh, sc, w_bf, wdt_bf, ssd_w, ssd_b, dt_bias2, dt_mult)


def in_proj_params(w_in, a_log, dt_bias, hy_cols, d_ssd, d_xbc):
    n_h = 2 * SSD_HEADS
    main = hy_cols + d_ssd + d_xbc
    w_dt = w_in[:, main:main + n_h]
    pad = jnp.zeros((w_in.shape[0], LANES - 2 * n_h), w_in.dtype)
    wdt = jnp.concatenate([w_dt, w_dt, pad], axis=1).astype(jnp.bfloat16)
    zpad = jnp.zeros((LANES - 2 * n_h,), jnp.float32)
    bias2 = jnp.concatenate([dt_bias.reshape(n_h), dt_bias.reshape(n_h), zpad])[None, :]
    mult = jnp.concatenate([jnp.ones((n_h,), jnp.float32), -jnp.exp(a_log).reshape(n_h), zpad])[None, :]
    return w_in[:, :main].astype(jnp.bfloat16), wdt, bias2, mult


def _out_router_kernel(yh_ref, ys_ref, z_ref, x_ref, ga_ref, sc_ref, sh_ref, ng_ref, g2_ref, wo_ref, wr_ref, br_ref,
                       x1_ref, hn_ref, ri_ref, rf_ref, cnt_ref, carry_ref):
    first = jnp.logical_and(pl.program_id(0) == 0, pl.program_id(1) == 0)

    @pl.when(first)
    def _():
        carry_ref[...] = jnp.zeros_like(carry_ref)

    bf = jnp.bfloat16
    tm = x_ref.shape[1]
    dh = yh_ref.shape[2]
    z = z_ref[0]
    ys = ys_ref[0] * (z * jax.nn.sigmoid(z))
    gw = ys.shape[1] // SSD_GROUPS
    acc = jnp.dot(yh_ref[0].astype(bf), wo_ref[0:dh, :], preferred_element_type=jnp.float32)
    for g in range(SSD_GROUPS):
        yg = ys[:, g * gw:(g + 1) * gw]
        yg = yg * lax.rsqrt(jnp.mean(yg * yg, axis=-1, keepdims=True) + EPS) * ng_ref[:, g * gw:(g + 1) * gw]
        acc += jnp.dot(yg.astype(bf), wo_ref[dh + g * gw:dh + (g + 1) * gw, :], preferred_element_type=jnp.float32)
    x1 = x_ref[0] + ga_ref[0] * acc
    x1_ref[0] = x1
    hn = x1 * lax.rsqrt(jnp.mean(x1 * x1, axis=-1, keepdims=True) + EPS) * g2_ref[...]
    hn = hn * (1.0 + sc_ref[0]) + sh_ref[0]
    hn_ref[0] = hn

    hn_hi = hn.astype(bf)
    hn_lo = (hn - hn_hi.astype(jnp.float32)).astype(bf)
    logits = (jnp.dot(hn_hi, wr_ref[0], preferred_element_type=jnp.float32)
              + jnp.dot(hn_lo, wr_ref[0], preferred_element_type=jnp.float32)
              + jnp.dot(hn_hi, wr_ref[1], preferred_element_type=jnp.float32)) + br_ref[...]
    lane = lax.broadcasted_iota(jnp.int32, (tm, LANES), 1)
    lane_f = lane.astype(jnp.float32)
    ninf = jnp.float32(-jnp.inf)
    big = jnp.float32(1e9)
    gl = jnp.where(lane < MOE_GROUPS, logits, ninf)
    gmax = jnp.max(gl, axis=-1, keepdims=True)
    p_group = 1.0 / jnp.sum(jnp.exp(gl - gmax), axis=-1, keepdims=True)
    g_sel = jnp.min(jnp.where(gl == gmax, lane_f, big), axis=-1, keepdims=True)
    e_lane = lane - MOE_GROUPS
    in_grp = jnp.logical_and(e_lane >= 0, (e_lane // EXPERTS_PER_GROUP).astype(jnp.float32) == g_sel)
    el = jnp.where(in_grp, logits, ninf)
    m1 = jnp.max(el, axis=-1, keepdims=True)
    i1 = jnp.min(jnp.where(el == m1, lane_f, big), axis=-1, keepdims=True)
    el2 = jnp.where(lane_f == i1, ninf, el)
    m2 = jnp.max(el2, axis=-1, keepdims=True)
    i2 = jnp.min(jnp.where(el2 == m2, lane_f, big), axis=-1, keepdims=True)
    t = jnp.exp(m2 - m1)
    w1 = 1.0 / (1.0 + t)
    gate1 = w1 * p_group
    gate2 = (t * w1) * p_group
    e1 = i1 - MOE_GROUPS
    e2 = i2 - MOE_GROUPS
    el_f = e_lane.astype(jnp.float32)
    oh1 = el_f == e1
    oh2 = el_f == e2
    oh = jnp.logical_or(oh1, oh2).astype(bf)
    r_i = lax.broadcasted_iota(jnp.int32, (tm, tm), 0)
    c_i = lax.broadcasted_iota(jnp.int32, (tm, tm), 1)
    before = jnp.dot((c_i < r_i).astype(bf), oh, preferred_element_type=jnp.float32) + carry_ref[...]
    rank1 = jnp.sum(jnp.where(oh1, before, 0.0), axis=-1, keepdims=True)
    rank2 = jnp.sum(jnp.where(oh2, before, 0.0), axis=-1, keepdims=True)
    carry_ref[...] += jnp.sum(oh.astype(jnp.float32), axis=0, keepdims=True)
    cnt_ref[...] = carry_ref[...]

    rec = jnp.where(lane == 0, e1, jnp.where(lane == 1, e2, jnp.where(lane == 2, rank1,
                                                                      jnp.where(lane == 3, rank2, 0.0))))
    ri_ref[0] = rec.T[0:ROUTE_COLS, :].astype(jnp.int32)
    col = lax.broadcasted_iota(jnp.int32, (tm, ROUTE_COLS), 1)
    rf_ref[0] = jnp.where(col == 0, gate1, gate2)


def out_proj_router(y_hy, y_scan, px, z_col, x, ga1, sc2, sh2, norm_g, g2, w_out_bf, w_router, b_router, tm):
    bsz, L, D = x.shape
    dh = y_hy.shape[-1]
    ds = y_scan.shape[-1]
    tok = lambda b, i: (b, i, 0)
    per_b = pl.BlockSpec((1, 1, D), lambda b, i: (b, 0, 0))
    const2 = lambda b, i: (0, 0)
    return pl.pallas_call(
        _out_router_kernel,
        grid=(bsz, L // tm),
        in_specs=[
            pl.BlockSpec((1, tm, dh), tok),
            pl.BlockSpec((1, tm, ds), tok),
            pl.BlockSpec((1, tm, ds), lambda b, i: (b, i, z_col)),
            pl.BlockSpec((1, tm, D), tok),
            per_b, per_b, per_b,
            pl.BlockSpec((1, ds), const2),
            pl.BlockSpec((1, D), const2),
            pl.BlockSpec((dh + ds, D), const2),
            pl.BlockSpec((2, D, LANES), lambda b, i: (0, 0, 0)),
            pl.BlockSpec((1, LANES), const2),
        ],
        out_specs=[
            pl.BlockSpec((1, tm, D), tok),
            pl.BlockSpec((1, tm, D), tok),
            pl.BlockSpec((1, ROUTE_COLS, tm), lambda b, i: (b, 0, i)),
            pl.BlockSpec((1, tm, ROUTE_COLS), tok),
            pl.BlockSpec((1, LANES), const2),
        ],
        out_shape=[
            jax.ShapeDtypeStruct((bsz, L, D), jnp.float32),
            jax.ShapeDtypeStruct((bsz, L, D), jnp.float32),
            jax.ShapeDtypeStruct((bsz, ROUTE_COLS, L), jnp.int32),
            jax.ShapeDtypeStruct((bsz, L, ROUTE_COLS), jnp.float32),
            jax.ShapeDtypeStruct((1, LANES), jnp.float32),
        ],
        scratch_shapes=[pltpu.VMEM((1, LANES), jnp.float32)],
        compiler_params=pltpu.CompilerParams(
            dimension_semantics=("arbitrary", "arbitrary"), vmem_limit_bytes=VMEM_LIMIT_BYTES),
    )(y_hy, y_scan, px, x, ga1, sc2, sh2, norm_g, g2, w_out_bf, w_router, b_router)


def _row_copy(src_hbm, src_row, dst_ref, dst_row, sem):
    return pltpu.make_async_copy(src_hbm.at[pl.ds(src_row, 1), :], dst_ref.at[pl.ds(dst_row, 1), :], sem)


def _dispatch_kernel(dest_ref, valid_ref, hn_ref, buf_hbm, zeros, sem, zsem):
    step = pl.program_id(0)
    tm = hn_ref.shape[0]
    n_tok = pl.num_programs(0) * tm
    blk = zeros.shape[0]
    n_blocks = buf_hbm.shape[0] // blk

    def zero_copy(i):
        return pltpu.make_async_copy(zeros, buf_hbm.at[pl.ds(pl.multiple_of(i * blk, blk), blk), :], zsem)

    def zfill(i, carry):
        @pl.when(valid_ref[i] < blk)
        def _():
            zero_copy(i).start()
        return carry

    def zwait(i, carry):
        @pl.when(valid_ref[i] < blk)
        def _():
            zero_copy(i).wait()
        return carry

    @pl.when(step == 0)
    def _():
        zeros[...] = jnp.zeros_like(zeros)
        lax.fori_loop(0, n_blocks, zfill, 0)
        lax.fori_loop(0, n_blocks, zwait, 0)

    def body(j, carry):
        t = step * tm + j
        _row_copy(hn_ref, j, buf_hbm, dest_ref[t], sem).start()
        _row_copy(hn_ref, j, buf_hbm, dest_ref[n_tok + t], sem).start()
        return carry

    lax.fori_loop(0, tm, body, 0, unroll=8)
    for _ in range(2):
        pltpu.make_async_copy(hn_ref, buf_hbm.at[pl.ds(0, tm), :], sem).wait()


def moe_dispatch(hn, dest, block_valid, blk, tm):
    T, D = hn.shape
    n_rows = block_valid.shape[0] * blk
    grid_spec = pltpu.PrefetchScalarGridSpec(
        num_scalar_prefetch=2,
        grid=(T // tm,),
        in_specs=[pl.BlockSpec((tm, D), lambda i, d, v: (i, 0))],
        out_specs=pl.BlockSpec(memory_space=pl.ANY),
        scratch_shapes=[pltpu.VMEM((blk, D), jnp.float32), pltpu.SemaphoreType.DMA(()),
                        pltpu.SemaphoreType.DMA(())],
    )
    return pl.pallas_call(
        _dispatch_kernel,
        grid_spec=grid_spec,
        out_shape=jax.ShapeDtypeStruct((n_rows, D), jnp.float32),
        compiler_params=pltpu.CompilerParams(dimension_semantics=("arbitrary",), has_side_effects=True),
    )(dest, block_valid, hn)


def _expert_kernel(eid_ref, first_ref, valid_ref, x_ref, w1_ref, w3_ref, w2_ref, o_ref, w1b, w3b, w2b):
    i = pl.program_id(0)
    del eid_ref
    bf = jnp.bfloat16

    @pl.when(first_ref[i] == 1)
    def _():
        w1b[...] = w1_ref[0].astype(bf)
        w3b[...] = w3_ref[0].astype(bf)
        w2b[...] = w2_ref[0].astype(bf)

    valid = valid_ref[i]

    @pl.when(valid > 0)
    def _():
        xb = x_ref[...].astype(bf)
        a = jnp.dot(xb, w1b[...], preferred_element_type=jnp.float32)
        b = jnp.dot(xb, w3b[...], preferred_element_type=jnp.float32)
        h = (a * jax.nn.sigmoid(a)) * b
        o_ref[...] = jnp.dot(h.astype(bf), w2b[...], preferred_element_type=jnp.float32)

    @pl.when(valid <= 0)
    def _():
        o_ref[...] = jnp.zeros_like(o_ref)


def expert_blocks(buf, block_eid, block_first, block_valid, w1, w3, w2, blk):
    rows, D = buf.shape
    n_blocks = rows // blk
    E, _, F = w1.shape
    grid_spec = pltpu.PrefetchScalarGridSpec(
        num_scalar_prefetch=3,
        grid=(n_blocks,),
        in_specs=[
            pl.BlockSpec((blk, D), lambda i, eid, fi, va: (i, 0)),
            pl.BlockSpec((1, D, F), lambda i, eid, fi, va: (eid[i], 0, 0)),
            pl.BlockSpec((1, D, F), lambda i, eid, fi, va: (eid[i], 0, 0)),
            pl.BlockSpec((1, F, D), lambda i, eid, fi, va: (eid[i], 0, 0)),
        ],
        out_specs=pl.BlockSpec((blk, D), lambda i, eid, fi, va: (i, 0)),
        scratch_shapes=[pltpu.VMEM((D, F), jnp.bfloat16), pltpu.VMEM((D, F), jnp.bfloat16),
                        pltpu.VMEM((F, D), jnp.bfloat16)],
    )
    return pl.pallas_call(
        _expert_kernel,
        grid_spec=grid_spec,
        out_shape=jax.ShapeDtypeStruct((rows, D), jnp.float32),
        compiler_params=pltpu.CompilerParams(
            dimension_semantics=("arbitrary",), vmem_limit_bytes=VMEM_LIMIT_BYTES),
    )(block_eid, block_first, block_valid, buf, w1, w3, w2)


def _combine_kernel(dest_ref, x1_ref, rf_ref, ga_ref, gf_ref, yb_hbm, o_ref, ybuf, sem):
    b = pl.program_id(0)
    i = pl.program_id(1)
    n_i = pl.num_programs(1)
    tm = x1_ref.shape[1]
    step = b * n_i + i
    n_steps = pl.num_programs(0) * n_i
    slot = step % 2

    def issue(step_, slot_):
        def body(j, carry):
            t = step_ * tm + j
            _row_copy(yb_hbm, dest_ref[t], ybuf.at[slot_, 0], j, sem.at[slot_]).start()
            _row_copy(yb_hbm, dest_ref[n_steps * tm + t], ybuf.at[slot_, 1], j, sem.at[slot_]).start()
            return carry
        lax.fori_loop(0, tm, body, 0, unroll=8)

    @pl.when(step == 0)
    def _():
        issue(0, 0)

    @pl.when(step + 1 < n_steps)
    def _():
        issue(step + 1, 1 - slot)

    pltpu.make_async_copy(yb_hbm.at[pl.ds(0, tm), :], ybuf.at[slot, 0], sem.at[slot]).wait()
    pltpu.make_async_copy(yb_hbm.at[pl.ds(0, tm), :], ybuf.at[slot, 1], sem.at[slot]).wait()
    rf = rf_ref[0]
    y = rf[:, 0:1] * ybuf[slot, 0] + rf[:, 1:2] * ybuf[slot, 1]
    x2 = x1_ref[0] + ga_ref[0] * y
    o_ref[0] = x2 * lax.rsqrt(jnp.mean(x2 * x2, axis=-1, keepdims=True) + EPS) * gf_ref[...]


def moe_combine(x1, route_f, ga2, g_final, yb, dest, tm):
    bsz, L, D = x1.shape
    grid_spec = pltpu.PrefetchScalarGridSpec(
        num_scalar_prefetch=1,
        grid=(bsz, L // tm),
        in_specs=[
            pl.BlockSpec((1, tm, D), lambda b, i, d: (b, i, 0)),
            pl.BlockSpec((1, tm, ROUTE_COLS), lambda b, i, d: (b, i, 0)),
            pl.BlockSpec((1, 1, D), lambda b, i, d: (b, 0, 0)),
            pl.BlockSpec((1, D), lambda b, i, d: (0, 0)),
            pl.BlockSpec(memory_space=pl.ANY),
        ],
        out_specs=pl.BlockSpec((1, tm, D), lambda b, i, d: (b, i, 0)),
        scratch_shapes=[pltpu.VMEM((2, 2, tm, D), jnp.float32), pltpu.SemaphoreType.DMA((2,))],
    )
    return pl.pallas_call(
        _combine_kernel,
        grid_spec=grid_spec,
        out_shape=jax.ShapeDtypeStruct((bsz, L, D), jnp.float32),
        compiler_params=pltpu.CompilerParams(
            dimension_semantics=("arbitrary", "arbitrary"), vmem_limit_bytes=VMEM_LIMIT_BYTES),
    )(dest, x1, route_f, ga2, g_final, yb)


def moe_plan(route_i, counts, blk, n_blocks):
    cnt = counts[0, MOE_GROUPS:MOE_GROUPS + N_EXPERTS].astype(jnp.int32)
    padded = (cnt + blk - 1) // blk * blk
    ends = jnp.cumsum(padded)
    starts = ends - padded
    experts = jnp.arange(N_EXPERTS, dtype=jnp.int32)
    dest = jnp.concatenate([
        (jnp.sum(jnp.where(route_i[:, k, :, None] == experts, starts, 0), axis=-1) + route_i[:, 2 + k]).reshape(-1)
        for k in range(TOP_K)])
    first_row = jnp.arange(n_blocks, dtype=jnp.int32) * blk
    block_eid = jnp.minimum(jnp.sum((ends[None, :] <= first_row[:, None]).astype(jnp.int32), axis=1), N_EXPERTS - 1)
    block_valid = jnp.clip(cnt[block_eid] - (first_row - starts[block_eid]), 0, blk).astype(jnp.int32)
    block_first = jnp.concatenate([jnp.ones((1,), jnp.int32),
                                   (block_eid[1:] != block_eid[:-1]).astype(jnp.int32)])
    return dest, block_eid, block_first, block_valid


def dft_tables(L):
    n = 2 * L
    f = lax.broadcasted_iota(jnp.int32, (L, L), 0)
    t = lax.broadcasted_iota(jnp.int32, (L, L), 1)
    ang = ((f * t) % n).astype(jnp.float32) * (2.0 * math.pi / n)
    return jnp.cos(ang).astype(jnp.bfloat16), jnp.sin(ang).astype(jnp.bfloat16)


def _alt_sign(L):
    t = lax.broadcasted_iota(jnp.int32, (L, 1), 0)
    return (1 - 2 * (t & 1)).astype(jnp.float32)


def _spectrum_kernel(a_ref, b_ref, c_ref, s_ref, kr_ref, ks_ref, kn_ref):
    L = a_ref.shape[1]
    a = a_ref[0]
    row = lax.broadcasted_iota(jnp.int32, (L, 1), 0)
    scale = jnp.where(row == 0, 0.5 / L, 1.0 / L)
    kr_ref[0] = scale * jnp.dot(c_ref[...], a.astype(jnp.bfloat16), preferred_element_type=jnp.float32)
    ks_ref[0] = scale * jnp.dot(s_ref[...], b_ref[0].astype(jnp.bfloat16), preferred_element_type=jnp.float32)
    kn_ref[0] = jnp.sum(a * _alt_sign(L), axis=0, keepdims=True) * (0.5 / L)


def filter_spectrum(a, b, cos_t, sin_t, tc):
    n, L, C = a.shape
    blk = pl.BlockSpec((1, L, tc), lambda o, j: (o, 0, j))
    tab = pl.BlockSpec((L, L), lambda o, j: (0, 0))
    return pl.pallas_call(
        _spectrum_kernel,
        grid=(n, C // tc),
        in_specs=[blk, blk, tab, tab],
        out_specs=[blk, blk, pl.BlockSpec((1, 1, tc), lambda o, j: (o, 0, j))],
        out_shape=[jax.ShapeDtypeStruct((n, L, C), jnp.float32)] * 2 + [jax.ShapeDtypeStruct((n, 1, C), jnp.float32)],
        compiler_params=pltpu.CompilerParams(
            dimension_semantics=("arbitrary", "arbitrary"), vmem_limit_bytes=VMEM_LIMIT_BYTES),
    )(a, b, cos_t, sin_t)


def _phase_conv3(raw, w_ref, b_ref, rows_per_phase):
    n_ph = len(raw)
    h = raw[0].shape[0]
    j = lax.broadcasted_iota(jnp.int32, (h, 1), 0) % rows_per_phase
    prev0 = jnp.where(j != 0, pltpu.roll(raw[n_ph - 1], 1, 0), 0.0)
    next_last = jnp.where(j != rows_per_phase - 1, pltpu.roll(raw[0], h - 1, 0), 0.0)
    out = []
    for p in range(n_ph):
        prev = raw[p - 1] if p > 0 else prev0
        nxt = raw[p + 1] if p < n_ph - 1 else next_last
        out.append(b_ref[...] + w_ref[0:1, :] * prev + w_ref[1:2, :] * raw[p] + w_ref[2:3, :] * nxt)
    return out


def _long_conv_kernel(*refs, n_ph, n_slab, conv_z, rows_per_phase):
    z_refs = refs[:n_slab]
    xn_refs = refs[n_slab:2 * n_slab]
    (kr_ref, ks_ref, kn_ref, bias_ref, cwz_ref, cbz_ref, cwx_ref, cbx_ref, c_ref, s_ref,
     o_ref, acc_ref, zr_ref, zs_ref, yr_ref, ys_ref, stage_ref) = refs[2 * n_slab:]
    H = z_refs[0].shape[1] // n_ph
    f32 = jnp.float32
    bf = jnp.bfloat16
    sign = _alt_sign(H)

    def phases(slab_refs):
        return [jnp.concatenate([r[0, pl.ds(p, H, stride=n_ph), :] for r in slab_refs], axis=1)
                for p in range(n_ph)]

    z_ph = phases(z_refs)
    if conv_z:
        z_ph = _phase_conv3(z_ph, cwz_ref, cbz_ref, rows_per_phase)
    for q in range(n_ph):
        zb = z_ph[q].astype(bf)
        zr_ref[q] = jnp.dot(c_ref[...], zb, preferred_element_type=f32)
        zs_ref[q] = jnp.dot(s_ref[...], zb, preferred_element_type=f32)
    z_nyq = [jnp.sum(z * sign, axis=0, keepdims=True) for z in z_ph]
    for p in range(n_ph):
        nyq = sum(z_nyq[q] * kn_ref[p - q + n_ph - 1] for q in range(n_ph))
        acc_ref[p] = z_ph[p] * bias_ref[0] + sign * nyq
        yr = 0.0
        ys = 0.0
        for q in range(n_ph):
            slot = p - q + n_ph - 1
            yr = yr + zr_ref[q] * kr_ref[slot] - zs_ref[q] * ks_ref[slot]
            ys = ys + zr_ref[q] * ks_ref[slot] + zs_ref[q] * kr_ref[slot]
        yr_ref[p] = yr.astype(bf)
        ys_ref[p] = ys.astype(bf)
    for p in range(n_ph):
        acc_ref[p] += (jnp.dot(c_ref[...], yr_ref[p], preferred_element_type=f32)
                       + jnp.dot(s_ref[...], ys_ref[p], preferred_element_type=f32))
    x_ph = _phase_conv3(phases(xn_refs), cwx_ref, cbx_ref, rows_per_phase)
    for p in range(n_ph):
        out_p = x_ph[p] * acc_ref[p]
        for sl in range(n_slab):
            stage_ref[sl, pl.ds(p, H, stride=n_ph), :] = out_p[:, sl * LANES:(sl + 1) * LANES]
    for sl in range(n_slab):
        o_ref[0, :, sl * LANES:(sl + 1) * LANES] = stage_ref[sl]


def long_conv_gate(z_arr, z_col, conv_z, xn_arr, xn_col, conv_w, conv_b, kr, ks, kn, bias, cos_t, sin_t,
                   tc, n_ph, row_len):
    bsz, L, _ = z_arr.shape
    H = L // n_ph
    C = kr.shape[-1]
    nj = C // tc
    n_slab = tc // LANES
    n_f = 2 * n_ph - 1
    tab = pl.BlockSpec((H, H), lambda j, b: (0, 0), pipeline_mode=pl.Buffered(1))
    spec = pl.BlockSpec((n_f, H, tc), lambda j, b: (0, 0, j), pipeline_mode=pl.Buffered(1))
    nyq = pl.BlockSpec((n_f, 1, tc), lambda j, b: (0, 0, j))
    vec = pl.BlockSpec((1, 1, tc), lambda j, b: (0, 0, j))

    def slabs(col):
        return [pl.BlockSpec((1, L, LANES),
                             functools.partial(lambda j, b, sl: (b, 0, (col * nj + j) * n_slab + sl), sl=sl))
                for sl in range(n_slab)]

    def conv_specs(col):
        return [pl.BlockSpec((3, tc), lambda j, b: (0, col * nj + j)),
                pl.BlockSpec((1, tc), lambda j, b: (0, col * nj + j))]

    zc = z_col if conv_z else 0
    kern = functools.partial(_long_conv_kernel, n_ph=n_ph, n_slab=n_slab, conv_z=conv_z,
                             rows_per_phase=row_len // n_ph)
    return pl.pallas_call(
        kern,
        grid=(nj, bsz),
        in_specs=(slabs(z_col) + slabs(xn_col) + [spec, spec, nyq, vec] + conv_specs(zc) + conv_specs(xn_col)
                  + [tab, tab]),
        out_specs=pl.BlockSpec((1, L, tc), lambda j, b: (b, 0, j)),
        out_shape=jax.ShapeDtypeStruct((bsz, L, C), jnp.float32),
        scratch_shapes=[pltpu.VMEM((n_ph, H, tc), jnp.float32), pltpu.VMEM((n_ph, H, tc), jnp.float32),
                        pltpu.VMEM((n_ph, H, tc), jnp.float32), pltpu.VMEM((n_ph, H, tc), jnp.bfloat16),
                        pltpu.VMEM((n_ph, H, tc), jnp.bfloat16), pltpu.VMEM((n_slab, L, LANES), jnp.float32)],
        compiler_params=pltpu.CompilerParams(
            dimension_semantics=("arbitrary", "arbitrary"), vmem_limit_bytes=VMEM_LIMIT_BYTES),
    )(*([z_arr] * n_slab), *([xn_arr] * n_slab), kr, ks, kn, bias, conv_w, conv_b, conv_w, conv_b, cos_t, sin_t)


def _polyphase_taps(kf, kb, n_ph):
    H = kf.shape[0] // n_ph
    ph = lambda a, p: a[p * H:(p + 1) * H]
    zero = jnp.zeros_like(kf[:1])
    plus, minus = [], []
    for r in range(-(n_ph - 1), n_ph):
        if r >= 0:
            plus.append(ph(kf, r))
        else:
            plus.append(jnp.concatenate([ph(kb, -r)[0:1], ph(kf, n_ph + r)[:-1]], axis=0))
        if r <= 0:
            minus.append(jnp.concatenate([zero, ph(kb, -r)[1:]], axis=0))
        else:
            minus.append(jnp.concatenate([zero, ph(kb, n_ph - r)[:-1]], axis=0))
    return jnp.stack(plus), jnp.stack(minus)


def hyena_long_convs(p_hy, conv_w, conv_b, kp, h_bias, tc, n_ph, row_len):
    L = p_hy.shape[1]
    C = h_bias.shape[1]
    cos_t, sin_t = dft_tables(L // n_ph)
    z = p_hy
    for o in range(h_bias.shape[0]):
        fwd = slice((2 * o) * C, (2 * o + 1) * C)
        bwd = slice((2 * o + 1) * C, (2 * o + 2) * C)
        plus, minus = _polyphase_taps(kp[:, fwd], kp[:, bwd], n_ph)
        kr, ks, kn = filter_spectrum(plus + minus, plus - minus, cos_t, sin_t, tc)
        z = long_conv_gate(z, 0, o == 0, p_hy, o + 1, conv_w, conv_b, kr, ks, kn, h_bias[o][None, None, :],
                           cos_t, sin_t, tc, n_ph, row_len)
    return z


def _filter_kernel(band_ref, w1_ref, b1_ref, fr_ref, w2_ref, b2_ref, w3_ref, dl_ref, k_ref, *, seq_len, n_ph):
    hp = lax.Precision.HIGHEST
    f32 = jnp.float32
    tp = k_ref.shape[0]
    per_phase = seq_len // n_ph
    g = lax.broadcasted_iota(jnp.int32, (tp, 1), 0) + pl.program_id(0) * tp
    phase = g // per_phase
    pos = (n_ph * (g - phase * per_phase) + phase).astype(f32)
    t = pos / max(seq_len - 1, 1)
    ang = (2 * math.pi / seq_len) * pos * band_ref[...]
    lane = lax.broadcasted_iota(jnp.int32, (tp, LANES), 1)
    feats = jnp.where(lane == 0, t,
                      jnp.where(lane <= HYENA_BANDS, jnp.cos(ang),
                                jnp.where(lane <= 2 * HYENA_BANDS, -jnp.sin(ang), 0.0)))
    h = jnp.sin(fr_ref[...] * (jnp.dot(feats, w1_ref[...], precision=hp, preferred_element_type=f32) + b1_ref[...]))
    h = jnp.sin(fr_ref[...] * (jnp.dot(h, w2_ref[...], precision=hp, preferred_element_type=f32) + b2_ref[...]))
    window = jnp.exp(-t * dl_ref[...])
    c = dl_ref.shape[1]
    for j in range(w3_ref.shape[1] // c):
        cols = slice(j * c, (j + 1) * c)
        k_ref[:, cols] = jnp.dot(h, w3_ref[:, cols], precision=hp, preferred_element_type=f32) * window


def hyena_filters_polyphase(seq_len, f_w1, f_b1, f_freq, f_w2, f_b2, f_w3, d_hyena, tp, n_ph):
    f32 = jnp.float32
    fh = f_w1.shape[1]
    n_emb = 1 + 2 * HYENA_BANDS
    bands = jnp.linspace(1e-4, HYENA_BANDS - 1, HYENA_BANDS, dtype=f32)
    band_row = jnp.concatenate([jnp.zeros((1,), f32), bands, bands, jnp.zeros((LANES - n_emb,), f32)])[None, :]
    w1p = jnp.concatenate([f_w1, jnp.zeros((LANES - n_emb, fh), f32)], axis=0)
    deltas = jnp.abs(jnp.linspace(math.log(HYENA_TARGET) / HYENA_SLOW_DECAY,
                                  math.log(HYENA_TARGET) / HYENA_FAST_DECAY, d_hyena, dtype=f32))[None, :]
    n_out = f_w3.shape[1]
    full = lambda a: pl.BlockSpec(a.shape, lambda i: (0,) * a.ndim)
    args = (band_row, w1p, f_b1[None, :], f_freq[None, :], f_w2, f_b2[None, :], f_w3, deltas)
    return pl.pallas_call(
        functools.partial(_filter_kernel, seq_len=seq_len, n_ph=n_ph),
        grid=(seq_len // tp,),
        in_specs=[full(a) for a in args],
        out_specs=pl.BlockSpec((tp, n_out), lambda i: (i, 0)),
        out_shape=jax.ShapeDtypeStruct((seq_len, n_out), f32),
        compiler_params=pltpu.CompilerParams(dimension_semantics=("arbitrary",), vmem_limit_bytes=VMEM_LIMIT_BYTES),
    )(*args)


def _ada_kernel(c_ref, w_ref, b_ref, o_ref):
    cv = c_ref[...]
    s = cv * jax.nn.sigmoid(cv)
    o_ref[...] = jnp.dot(s, w_ref[...], precision=lax.Precision.HIGHEST,
                         preferred_element_type=jnp.float32) + b_ref[...]


def ada_modulation(c_rows, w_ada, b_ada, tn):
    rows, D = c_rows.shape
    N = w_ada.shape[1]
    return pl.pallas_call(
        _ada_kernel,
        grid=(N // tn,),
        in_specs=[pl.BlockSpec((rows, D), lambda j: (0, 0)),
                  pl.BlockSpec((D, tn), lambda j: (0, j)),
                  pl.BlockSpec((1, tn), lambda j: (0, j))],
        out_specs=pl.BlockSpec((rows, tn), lambda j: (0, j)),
        out_shape=jax.ShapeDtypeStruct((rows, N), jnp.float32),
        compiler_params=pltpu.CompilerParams(dimension_semantics=("arbitrary",), vmem_limit_bytes=VMEM_LIMIT_BYTES),
    )(c_rows, w_ada, b_ada[None, :])


def _ssd_kernel(xf_ref, df_ref, xb_ref, db_ref, dskip_ref, y_ref, h_ref, *, n_ctx_chunks):
    s = pl.program_id(1)
    n_steps = pl.num_programs(1)
    Q, G, R, P, N = SSD_CHUNK, SSD_GROUPS, SSD_HPG, SSD_HEAD_DIM, SSD_STATE
    GP = R * P
    bf = jnp.bfloat16

    @pl.when(s == 0)
    def _():
        h_ref[...] = jnp.zeros_like(h_ref)
        y_ref[...] = jnp.zeros_like(y_ref)

    row = lax.broadcasted_iota(jnp.int32, (Q, Q), 0)
    col = lax.broadcasted_iota(jnp.int32, (Q, Q), 1)
    lane_head = lax.broadcasted_iota(jnp.int32, (Q, GP), 1) // P
    is_latent = s >= n_ctx_chunks
    n_lat = n_steps - n_ctx_chunks
    out_chunk = (jnp.clip(s - n_ctx_chunks, 0, n_lat - 1), jnp.clip(n_steps - 1 - s, 0, n_lat - 1))

    for bb, d in [(bb, d) for bb in range(y_ref.shape[0]) for d in range(2)]:
        x_ref, da_ref = ((xf_ref, df_ref), (xb_ref, db_ref))[d]
        mask = (row >= col) if d == 0 else (col >= row)
        tri = mask.astype(jnp.float32)
        da = da_ref[bb]
        cum = jnp.dot(tri, da, precision=lax.Precision.HIGHEST, preferred_element_type=jnp.float32)
        cum_t = cum.T
        edge = Q - 1 if d == 0 else 0
        blk = x_ref.at[bb]
        for g in range(G):
            xg = blk[:, g * GP:(g + 1) * GP]
            bg = blk[:, D_SSD + g * N:D_SSD + (g + 1) * N].astype(bf)
            cg = blk[:, D_SSD + G * N + g * N:D_SSD + G * N + (g + 1) * N].astype(bf)
            heads = [d * SSD_HEADS + g * R + r for r in range(R)]
            dtm = jnp.zeros((Q, GP), jnp.float32)
            cumm = jnp.zeros((Q, GP), jnp.float32)
            totm = jnp.zeros((Q, GP), jnp.float32)
            for r, h in enumerate(heads):
                sel = lane_head == r
                dtm = jnp.where(sel, da[:, h:h + 1], dtm)
                cumm = jnp.where(sel, cum[:, SSD_HEADS * 2 + h:SSD_HEADS * 2 + h + 1], cumm)
                totm = jnp.where(sel, cum[edge:edge + 1, SSD_HEADS * 2 + h:SSD_HEADS * 2 + h + 1], totm)
            xdt = xg * dtm
            hg = h_ref[bb, d, g * GP:(g + 1) * GP, :]

            gmat = lax.dot_general(cg, bg, (((1,), (1,)), ((), ())), preferred_element_type=jnp.float32)
            y_off = lax.dot_general(cg, hg.astype(bf), (((1,), (1,)), ((), ())),
                                    preferred_element_type=jnp.float32) * jnp.exp(cumm)
            if d == 0:
                y_off = y_off + dskip_ref[:, g * GP:(g + 1) * GP] * xg
            parts = []
            for r, h in enumerate(heads):
                a_col = cum[:, SSD_HEADS * 2 + h:SSD_HEADS * 2 + h + 1]
                a_row = cum_t[SSD_HEADS * 2 + h:SSD_HEADS * 2 + h + 1, :]
                decay = jnp.exp(jnp.where(mask, a_col - a_row, NEG_BIG))
                parts.append(jnp.dot((gmat * decay).astype(bf), xdt[:, r * P:(r + 1) * P].astype(bf),
                                     preferred_element_type=jnp.float32))
            y = jnp.where(is_latent, y_off + jnp.concatenate(parts, axis=-1), 0.0)
            rows = pl.ds(pl.multiple_of(out_chunk[d] * Q, Q), Q)
            y_ref[bb, rows, g * GP:(g + 1) * GP] += y

            xw = (xdt * jnp.exp(totm - cumm)).astype(bf)
            st = lax.dot_general(xw, bg, (((0,), (0,)), ((), ())), preferred_element_type=jnp.float32)
            for r, h in enumerate(heads):
                dec = jnp.exp(cum_t[SSD_HEADS * 2 + h:SSD_HEADS * 2 + h + 1, edge:edge + 1])
                rs = slice(g * GP + r * P, g * GP + (r + 1) * P)
                h_ref[bb, d, rs, :] = h_ref[bb, d, rs, :] * dec + st[r * P:(r + 1) * P, :]


def ssd_scan_bidir(xbc, dta, d_skip, n_ctx, lat_off, nb):
    bsz, lt, width = xbc.shape
    Q = SSD_CHUNK
    n_ctx_chunks = n_ctx // Q
    L = lt - lat_off
    n_lat = L // Q
    n_steps = n_ctx_chunks + n_lat
    lat0 = lat_off // Q

    def fwd_chunk(s):
        return jnp.where(s < n_ctx_chunks, s, s - n_ctx_chunks + lat0)

    def bwd_chunk(s):
        return jnp.where(s < n_ctx_chunks, n_ctx_chunks - 1 - s, n_steps - 1 - s + lat0)

    return pl.pallas_call(
        functools.partial(_ssd_kernel, n_ctx_chunks=n_ctx_chunks),
        grid=(bsz // nb, n_steps),
        in_specs=[
            pl.BlockSpec((nb, Q, width), lambda b, s: (b, fwd_chunk(s), 0)),
            pl.BlockSpec((nb, Q, LANES), lambda b, s: (b, fwd_chunk(s), 0)),
            pl.BlockSpec((nb, Q, width), lambda b, s: (b, bwd_chunk(s), 0)),
            pl.BlockSpec((nb, Q, LANES), lambda b, s: (b, bwd_chunk(s), 0)),
            pl.BlockSpec((1, D_SSD), lambda b, s: (0, 0)),
        ],
        out_specs=pl.BlockSpec((nb, L, D_SSD), lambda b, s: (b, 0, 0)),
        out_shape=jax.ShapeDtypeStruct((bsz, L, D_SSD), jnp.float32),
        scratch_shapes=[pltpu.VMEM((nb, 2, SSD_GROUPS * SSD_HPG * SSD_HEAD_DIM, SSD_STATE), jnp.float32)],
        compiler_params=pltpu.CompilerParams(
            dimension_semantics=("arbitrary", "arbitrary"), vmem_limit_bytes=VMEM_LIMIT_BYTES),
    )(xbc, dta, xbc, dta, d_skip)


def kernel(x, c, ctx, c_ctx, w_ada, b_ada, g_norm1, g_norm2, w_in, hy_conv_w, hy_conv_b, hy_f_w1, hy_f_b1, hy_f_freq, hy_f_w2, hy_f_b2, hy_f_w3, hy_bias, ssd_conv_w, ssd_conv_b, ssd_a_log, ssd_dt_bias, ssd_d, ssd_norm_g, w_out, w_group, b_group, w_expert, b_expert, w1, w3, w2, g_final):
    bsz, seq_len, _ = x.shape
    l = 0
    rows_pad = -(bsz + 1) % SUBLANES
    c_rows = jnp.concatenate([c, c_ctx[None, :], jnp.zeros((rows_pad, D_MODEL), jnp.float32)], axis=0)
    mod_all = ada_modulation(c_rows, w_ada[l], b_ada[l], 512)
    sh1, sc1, ga1, sh2, sc2, ga2 = jnp.split(mod_all[:bsz, None, :], 6, axis=-1)
    csh1, csc1 = mod_all[bsz, :D_MODEL], mod_all[bsz, D_MODEL:2 * D_MODEL]

    w_out_bf = w_out[l].astype(jnp.bfloat16)
    w_in_bf, w_dt_bf, dt_bias2, dt_mult = in_proj_params(w_in[l], ssd_a_log[l], ssd_dt_bias[l],
                                                         HY_COLS, D_SSD, D_XBC)

    ctx_len = ctx.shape[1]
    ctx_pad = -ctx_len % IN_PROJ_ROWS
    ctx_p = jnp.pad(ctx, ((0, 0), (0, ctx_pad), (0, 0)))
    p_hy, z, xbc, dta = in_proj_fused(ctx_p, x, g_norm1[l][None, :], csh1[None, :], csc1[None, :], sh1, sc1,
                                      w_in_bf, w_dt_bf, HY_COLS, ssd_conv_w[l], ssd_conv_b[l][None, :],
                                      dt_bias2, dt_mult, GRID_W, ctx_len, IN_PROJ_ROWS, 512)
    kp = hyena_filters_polyphase(seq_len, hy_f_w1[l], hy_f_b1[l], hy_f_freq[l], hy_f_w2[l], hy_f_b2[l],
                                 hy_f_w3[l], D_HYENA, 256, HYENA_PHASES)
    y_hy = hyena_long_convs(p_hy, hy_conv_w[l], hy_conv_b[l][None, :], kp, hy_bias[l], 256, HYENA_PHASES, GRID_W)
    y_scan = ssd_scan_bidir(xbc, dta, jnp.repeat(ssd_d[l], SSD_HEAD_DIM)[None, :], ctx_len, ctx_len + ctx_pad, 2)

    pad = LANES - MOE_GROUPS - N_EXPERTS
    w_router = jnp.concatenate([w_group[l], w_expert[l], jnp.zeros((D_MODEL, pad), jnp.float32)], axis=1)
    w_router_hi = w_router.astype(jnp.bfloat16)
    w_router_lo = (w_router - w_router_hi.astype(jnp.float32)).astype(jnp.bfloat16)
    w_router = jnp.stack([w_router_hi, w_router_lo])
    b_router = jnp.concatenate([b_group[l], b_expert[l], jnp.zeros((pad,), jnp.float32)])[None, :]
    x1, hn, route_i, route_f, counts = out_proj_router(
        y_hy, y_scan, z, 0, x, ga1, sc2, sh2, ssd_norm_g[l][None, :], g_norm2[l][None, :],
        w_out_bf, w_router, b_router, 256)
    n_tok = bsz * seq_len
    n_blocks = -(-n_tok * TOP_K // MOE_BLOCK) + N_EXPERTS
    dest, block_eid, block_first, block_valid = moe_plan(route_i, counts, MOE_BLOCK, n_blocks)
    buf = moe_dispatch(hn.reshape(n_tok, D_MODEL), dest, block_valid, MOE_BLOCK, 256)
    yb = expert_blocks(buf, block_eid, block_first, block_valid, w1[l], w3[l], w2[l], MOE_BLOCK)
    return moe_combine(x1, route_f, ga2, g_final[None, :], yb, dest, 256)
```

```python
import functools
import math

import jax
import jax.numpy as jnp
from jax import lax
from jax.experimental import pallas as pl
from jax.experimental.pallas import tpu as pltpu

D_MODEL = 1024
CTX_LEN = 256
GRID_W = 64
EPS = 1e-6
SHORT_CONV = 3

D_HYENA = D_MODEL // 2
HYENA_ORDER = 2
HYENA_BANDS = 8
HYENA_FAST_DECAY = 0.3
HYENA_SLOW_DECAY = 1.5
HYENA_TARGET = 1e-2
HYENA_PHASES = 4

D_SSD = D_MODEL // 2
SSD_HEAD_DIM = 64
SSD_HEADS = D_SSD // SSD_HEAD_DIM
SSD_GROUPS = 2
SSD_HPG = SSD_HEADS // SSD_GROUPS
SSD_STATE = 128
SSD_CHUNK = 128

D_XBC = D_SSD + 2 * SSD_GROUPS * SSD_STATE
HY_COLS = (HYENA_ORDER + 1) * D_HYENA
D_IN = HY_COLS + D_SSD + D_XBC + 2 * SSD_HEADS
LANES = 128
SUBLANES = 8
D_IN_PAD = -(-D_IN // LANES) * LANES

MOE_GROUPS = 8
EXPERTS_PER_GROUP = 8
N_EXPERTS = MOE_GROUPS * EXPERTS_PER_GROUP
TOP_K = 2
D_EXPERT = 512
MOE_BLOCK = 256
ROUTE_COLS = 8
IN_PROJ_ROWS = 512

VMEM_LIMIT_BYTES = 56 * 1024 * 1024
NEG_BIG = -1e30


def _conv3_rows(p, w_ref, b_ref, cols, has_prev, has_next):
    n = p.shape[0]
    prev = jnp.where(has_prev, pltpu.roll(p, 1, 0), 0.0)
    nxt = jnp.where(has_next, pltpu.roll(p, n - 1, 0), 0.0)
    return b_ref[:, cols] + w_ref[0:1, cols] * prev + w_ref[1:2, cols] * p + w_ref[2:3, cols] * nxt


def _in_proj_kernel(ctx_ref, x_ref, g_ref, csh_ref, csc_ref, sh_ref, sc_ref, w_ref, wdt_ref,
                    sw_ref, sb_ref, dtb_ref, dtm_ref, u_ref, z_ref, xbc_ref, dta_ref, h_ref,
                    *, n_ctx_steps, row_len, ctx_row_len, hy_cols, d_ssd, d_xbc, tn):
    i = pl.program_id(1)
    is_ctx = i < n_ctx_steps
    tm = x_ref.shape[1]
    ctx_t = ctx_ref[0]
    if ctx_t.shape[0] < tm:
        ctx_t = jnp.concatenate([ctx_t, jnp.zeros((tm - ctx_t.shape[0], ctx_t.shape[1]), ctx_t.dtype)], axis=0)
    xin = jnp.where(is_ctx, ctx_t, x_ref[0])
    shift = jnp.where(is_ctx, csh_ref[...], sh_ref[0])
    scale = jnp.where(is_ctx, csc_ref[...], sc_ref[0])
    y = xin * lax.rsqrt(jnp.mean(xin * xin, axis=-1, keepdims=True) + EPS) * g_ref[...]
    h_ref[...] = (y * (1.0 + scale) + shift).astype(jnp.bfloat16)

    pos = lax.broadcasted_iota(jnp.int32, (tm, 1), 0) + jnp.where(is_ctx, i, i - n_ctx_steps) * tm
    in_row = jnp.where(is_ctx, pos % ctx_row_len, pos % row_len)
    has_prev = in_row != 0
    has_next = in_row != jnp.where(is_ctx, ctx_row_len - 1, row_len - 1)

    for c0 in range(0, hy_cols, tn):
        cols = slice(c0, c0 + tn)
        u_ref[0, :, cols] = jnp.dot(h_ref[...], w_ref[:, cols], preferred_element_type=jnp.float32)
    z_ref[0] = jnp.dot(h_ref[...], w_ref[:, hy_cols:hy_cols + d_ssd], preferred_element_type=jnp.float32)

    for c0 in range(0, d_xbc, tn):
        cols = slice(c0, c0 + tn)
        wc = slice(hy_cols + d_ssd + c0, hy_cols + d_ssd + c0 + tn)
        p = jnp.dot(h_ref[...], w_ref[:, wc], preferred_element_type=jnp.float32)
        v = _conv3_rows(p, sw_ref, sb_ref, cols, has_prev, has_next)
        xbc_ref[0, :, cols] = v * jax.nn.sigmoid(v)
    pd = jnp.dot(h_ref[...], wdt_ref[...], preferred_element_type=jnp.float32) + dtb_ref[...]
    sp = jnp.maximum(pd, 0.0) + jnp.log(1.0 + jnp.exp(-jnp.abs(pd)))
    dta_ref[0] = sp * dtm_ref[...]


def in_proj_fused(ctx, x, g1, csh, csc, sh, sc, w_bf, wdt_bf, hy_cols, ssd_w, ssd_b, dt_bias2, dt_mult,
                  row_len, ctx_row_len, tm, tn):
    bsz, L, D = x.shape
    lc = ctx.shape[1]
    d_xbc = ssd_w.shape[1]
    d_ssd = w_bf.shape[1] - hy_cols - d_xbc
    n_ctx_steps = -(-lc // tm)
    ctx_rows = min(lc, tm)
    lc = n_ctx_steps * tm
    n_steps = n_ctx_steps + L // tm
    lat = lambda b, i: (b, jnp.maximum(i - n_ctx_steps, 0), 0)
    allt = lambda b, i: (b, i, 0)
    const2 = lambda b, i: (0, 0)
    per_b = pl.BlockSpec((1, 1, D), lambda b, i: (b, 0, 0))
    kern = functools.partial(_in_proj_kernel, n_ctx_steps=n_ctx_steps, row_len=row_len, ctx_row_len=ctx_row_len,
                             hy_cols=hy_cols, d_ssd=d_ssd, d_xbc=d_xbc, tn=tn)
    return pl.pallas_call(
        kern,
        grid=(bsz, n_steps),
        in_specs=[
            pl.BlockSpec((1, ctx_rows, D), lambda b, i: (b, jnp.minimum(i, n_ctx_steps - 1), 0)),
            pl.BlockSpec((1, tm, D), lat),
            pl.BlockSpec((1, D), const2),
            pl.BlockSpec((1, D), const2),
            pl.BlockSpec((1, D), const2),
            per_b, per_b,
            pl.BlockSpec(w_bf.shape, const2),
            pl.BlockSpec(wdt_bf.shape, const2),
            pl.BlockSpec(ssd_w.shape, const2),
            pl.BlockSpec(ssd_b.shape, const2),
            pl.BlockSpec((1, LANES), const2),
            pl.BlockSpec((1, LANES), const2),
        ],
        out_specs=[
            pl.BlockSpec((1, tm, hy_cols), lat),
            pl.BlockSpec((1, tm, d_ssd), lat),
            pl.BlockSpec((1, tm, d_xbc), allt),
            pl.BlockSpec((1, tm, LANES), allt),
        ],
        out_shape=[
            jax.ShapeDtypeStruct((bsz, L, hy_cols), jnp.float32),
            jax.ShapeDtypeStruct((bsz, L, d_ssd), jnp.float32),
            jax.ShapeDtypeStruct((bsz, lc + L, d_xbc), jnp.float32),
            jax.ShapeDtypeStruct((bsz, lc + L, LANES), jnp.float32),
        ],
        scratch_shapes=[pltpu.VMEM((tm, D), jnp.bfloat16)],
        compiler_params=pltpu.CompilerParams(
            dimension_semantics=("arbitrary", "arbitrary"), vmem_limit_bytes=VMEM_LIMIT_BYTES),
    )(ctx, x, g1, csh, csc, sh, sc, w_bf, wdt_bf, ssd_w, ssd_b, dt_bias2, dt_mult)


def in_proj_params(w_in, a_log, dt_bias, hy_cols, d_ssd, d_xbc):
    n_h = 2 * SSD_HEADS
    main = hy_cols + d_ssd + d_xbc
    w_dt = w_in[:, main:main + n_h]
    pad = jnp.zeros((w_in.shape[0], LANES - 2 * n_h), w_in.dtype)
    wdt = jnp.concatenate([w_dt, w_dt, pad], axis=1).astype(jnp.bfloat16)
    zpad = jnp.zeros((LANES - 2 * n_h,), jnp.float32)
    bias2 = jnp.concatenate([dt_bias.reshape(n_h), dt_bias.reshape(n_h), zpad])[None, :]
    mult = jnp.concatenate([jnp.ones((n_h,), jnp.float32), -jnp.exp(a_log).reshape(n_h), zpad])[None, :]
    return w_in[:, :main].astype(jnp.bfloat16), wdt, bias2, mult


def _out_router_kernel(yh_ref, ys_ref, z_ref, x_ref, ga_ref, sc_ref, sh_ref, ng_ref, g2_ref, wo_ref, wr_ref, br_ref,
                       x1_ref, hn_ref, ri_ref, rf_ref, cnt_ref, carry_ref):
    first = jnp.logical_and(pl.program_id(0) == 0, pl.program_id(1) == 0)

    @pl.when(first)
    def _():
        carry_ref[...] = jnp.zeros_like(carry_ref)

    bf = jnp.bfloat16
    tm = x_ref.shape[1]
    dh = yh_ref.shape[2]
    z = z_ref[0]
    ys = ys_ref[0] * (z * jax.nn.sigmoid(z))
    gw = ys.shape[1] // SSD_GROUPS
    acc = jnp.dot(yh_ref[0].astype(bf), wo_ref[0:dh, :], preferred_element_type=jnp.float32)
    for g in range(SSD_GROUPS):
        yg = ys[:, g * gw:(g + 1) * gw]
        yg = yg * lax.rsqrt(jnp.mean(yg * yg, axis=-1, keepdims=True) + EPS) * ng_ref[:, g * gw:(g + 1) * gw]
        acc += jnp.dot(yg.astype(bf), wo_ref[dh + g * gw:dh + (g + 1) * gw, :], preferred_element_type=jnp.float32)
    x1 = x_ref[0] + ga_ref[0] * acc
    x1_ref[0] = x1
    hn = x1 * lax.rsqrt(jnp.mean(x1 * x1, axis=-1, keepdims=True) + EPS) * g2_ref[...]
    hn = hn * (1.0 + sc_ref[0]) + sh_ref[0]
    hn_ref[0] = hn

    hn_hi = hn.astype(bf)
    hn_lo = (hn - hn_hi.astype(jnp.float32)).astype(bf)
    logits = (jnp.dot(hn_hi, wr_ref[0], preferred_element_type=jnp.float32)
              + jnp.dot(hn_lo, wr_ref[0], preferred_element_type=jnp.float32)
              + jnp.dot(hn_hi, wr_ref[1], preferred_element_type=jnp.float32)) + br_ref[...]
    lane = lax.broadcasted_iota(jnp.int32, (tm, LANES), 1)
    lane_f = lane.astype(jnp.float32)
    ninf = jnp.float32(-jnp.inf)
    big = jnp.float32(1e9)
    gl = jnp.where(lane < MOE_GROUPS, logits, ninf)
    gmax = jnp.max(gl, axis=-1, keepdims=True)
    p_group = 1.0 / jnp.sum(jnp.exp(gl - gmax), axis=-1, keepdims=True)
    g_sel = jnp.min(jnp.where(gl == gmax, lane_f, big), axis=-1, keepdims=True)
    e_lane = lane - MOE_GROUPS
    in_grp = jnp.logical_and(e_lane >= 0, (e_lane // EXPERTS_PER_GROUP).astype(jnp.float32) == g_sel)
    el = jnp.where(in_grp, logits, ninf)
    m1 = jnp.max(el, axis=-1, keepdims=True)
    i1 = jnp.min(jnp.where(el == m1, lane_f, big), axis=-1, keepdims=True)
    el2 = jnp.where(lane_f == i1, ninf, el)
    m2 = jnp.max(el2, axis=-1, keepdims=True)
    i2 = jnp.min(jnp.where(el2 == m2, lane_f, big), axis=-1, keepdims=True)
    t = jnp.exp(m2 - m1)
    w1 = 1.0 / (1.0 + t)
    gate1 = w1 * p_group
    gate2 = (t * w1) * p_group
    e1 = i1 - MOE_GROUPS
    e2 = i2 - MOE_GROUPS
    el_f = e_lane.astype(jnp.float32)
    oh1 = el_f == e1
    oh2 = el_f == e2
    oh = jnp.logical_or(oh1, oh2).astype(bf)
    r_i = lax.broadcasted_iota(jnp.int32, (tm, tm), 0)
    c_i = lax.broadcasted_iota(jnp.int32, (tm, tm), 1)
    before = jnp.dot((c_i < r_i).astype(bf), oh, preferred_element_type=jnp.float32) + carry_ref[...]
    rank1 = jnp.sum(jnp.where(oh1, before, 0.0), axis=-1, keepdims=True)
    rank2 = jnp.sum(jnp.where(oh2, before, 0.0), axis=-1, keepdims=True)
    carry_ref[...] += jnp.sum(oh.astype(jnp.float32), axis=0, keepdims=True)
    cnt_ref[...] = carry_ref[...]

    rec = jnp.where(lane == 0, e1, jnp.where(lane == 1, e2, jnp.where(lane == 2, rank1,
                                                                      jnp.where(lane == 3, rank2, 0.0))))
    ri_ref[0] = rec.T[0:ROUTE_COLS, :].astype(jnp.int32)
    col = lax.broadcasted_iota(jnp.int32, (tm, ROUTE_COLS), 1)
    rf_ref[0] = jnp.where(col == 0, gate1, gate2)


def out_proj_router(y_hy, y_scan, px, z_col, x, ga1, sc2, sh2, norm_g, g2, w_out_bf, w_router, b_router, tm):
    bsz, L, D = x.shape
    dh = y_hy.shape[-1]
    ds = y_scan.shape[-1]
    tok = lambda b, i: (b, i, 0)
    per_b = pl.BlockSpec((1, 1, D), lambda b, i: (b, 0, 0))
    const2 = lambda b, i: (0, 0)
    return pl.pallas_call(
        _out_router_kernel,
        grid=(bsz, L // tm),
        in_specs=[
            pl.BlockSpec((1, tm, dh), tok),
            pl.BlockSpec((1, tm, ds), tok),
            pl.BlockSpec((1, tm, ds), lambda b, i: (b, i, z_col)),
            pl.BlockSpec((1, tm, D), tok),
            per_b, per_b, per_b,
            pl.BlockSpec((1, ds), const2),
            pl.BlockSpec((1, D), const2),
            pl.BlockSpec((dh + ds, D), const2),
            pl.BlockSpec((2, D, LANES), lambda b, i: (0, 0, 0)),
            pl.BlockSpec((1, LANES), const2),
        ],
        out_specs=[
            pl.BlockSpec((1, tm, D), tok),
            pl.BlockSpec((1, tm, D), tok),
            pl.BlockSpec((1, ROUTE_COLS, tm), lambda b, i: (b, 0, i)),
            pl.BlockSpec((1, tm, ROUTE_COLS), tok),
            pl.BlockSpec((1, LANES), const2),
        ],
        out_shape=[
            jax.ShapeDtypeStruct((bsz, L, D), jnp.float32),
            jax.ShapeDtypeStruct((bsz, L, D), jnp.float32),
            jax.ShapeDtypeStruct((bsz, ROUTE_COLS, L), jnp.int32),
            jax.ShapeDtypeStruct((bsz, L, ROUTE_COLS), jnp.float32),
            jax.ShapeDtypeStruct((1, LANES), jnp.float32),
        ],
        scratch_shapes=[pltpu.VMEM((1, LANES), jnp.float32)],
        compiler_params=pltpu.CompilerParams(
            dimension_semantics=("arbitrary", "arbitrary"), vmem_limit_bytes=VMEM_LIMIT_BYTES),
    )(y_hy, y_scan, px, x, ga1, sc2, sh2, norm_g, g2, w_out_bf, w_router, b_router)


def _row_copy(src_hbm, src_row, dst_ref, dst_row, sem):
    return pltpu.make_async_copy(src_hbm.at[pl.ds(src_row, 1), :], dst_ref.at[pl.ds(dst_row, 1), :], sem)


def _dispatch_kernel(dest_ref, valid_ref, hn_ref, buf_hbm, zeros, sem, zsem):
    step = pl.program_id(0)
    tm = hn_ref.shape[0]
    n_tok = pl.num_programs(0) * tm
    blk = zeros.shape[0]
    n_blocks = buf_hbm.shape[0] // blk

    def zero_copy(i):
        return pltpu.make_async_copy(zeros, buf_hbm.at[pl.ds(pl.multiple_of(i * blk, blk), blk), :], zsem)

    def zfill(i, carry):
        @pl.when(valid_ref[i] < blk)
        def _():
            zero_copy(i).start()
        return carry

    def zwait(i, carry):
        @pl.when(valid_ref[i] < blk)
        def _():
            zero_copy(i).wait()
        return carry

    @pl.when(step == 0)
    def _():
        zeros[...] = jnp.zeros_like(zeros)
        lax.fori_loop(0, n_blocks, zfill, 0)
        lax.fori_loop(0, n_blocks, zwait, 0)

    def body(j, carry):
        t = step * tm + j
        _row_copy(hn_ref, j, buf_hbm, dest_ref[t], sem).start()
        _row_copy(hn_ref, j, buf_hbm, dest_ref[n_tok + t], sem).start()
        return carry

    lax.fori_loop(0, tm, body, 0, unroll=8)
    for _ in range(2):
        pltpu.make_async_copy(hn_ref, buf_hbm.at[pl.ds(0, tm), :], sem).wait()


def moe_dispatch(hn, dest, block_valid, blk, tm):
    T, D = hn.shape
    n_rows = block_valid.shape[0] * blk
    grid_spec = pltpu.PrefetchScalarGridSpec(
        num_scalar_prefetch=2,
        grid=(T // tm,),
        in_specs=[pl.BlockSpec((tm, D), lambda i, d, v: (i, 0))],
        out_specs=pl.BlockSpec(memory_space=pl.ANY),
        scratch_shapes=[pltpu.VMEM((blk, D), hn.dtype), pltpu.SemaphoreType.DMA(()),
                        pltpu.SemaphoreType.DMA(())],
    )
    return pl.pallas_call(
        _dispatch_kernel,
        grid_spec=grid_spec,
        out_shape=jax.ShapeDtypeStruct((n_rows, D), hn.dtype),
        compiler_params=pltpu.CompilerParams(dimension_semantics=("arbitrary",), has_side_effects=True),
    )(dest, block_valid, hn)


def _expert_kernel(eid_ref, first_ref, valid_ref, x_ref, w1_ref, w3_ref, w2_ref, o_ref, w1b, w3b, w2b):
    i = pl.program_id(0)
    del eid_ref
    bf = jnp.bfloat16

    @pl.when(first_ref[i] == 1)
    def _():
        w1b[...] = w1_ref[0].astype(bf)
        w3b[...] = w3_ref[0].astype(bf)
        w2b[...] = w2_ref[0].astype(bf)

    valid = valid_ref[i]

    @pl.when(valid > 0)
    def _():
        xb = x_ref[...].astype(bf)
        a = jnp.dot(xb, w1b[...], preferred_element_type=jnp.float32)
        b = jnp.dot(xb, w3b[...], preferred_element_type=jnp.float32)
        h = (a * jax.nn.sigmoid(a)) * b
        o_ref[...] = jnp.dot(h.astype(bf), w2b[...], preferred_element_type=jnp.float32)

    @pl.when(valid <= 0)
    def _():
        o_ref[...] = jnp.zeros_like(o_ref)


def expert_blocks(buf, block_eid, block_first, block_valid, w1, w3, w2, blk):
    rows, D = buf.shape
    n_blocks = rows // blk
    E, _, F = w1.shape
    grid_spec = pltpu.PrefetchScalarGridSpec(
        num_scalar_prefetch=3,
        grid=(n_blocks,),
        in_specs=[
            pl.BlockSpec((blk, D), lambda i, eid, fi, va: (i, 0)),
            pl.BlockSpec((1, D, F), lambda i, eid, fi, va: (eid[i], 0, 0)),
            pl.BlockSpec((1, D, F), lambda i, eid, fi, va: (eid[i], 0, 0)),
            pl.BlockSpec((1, F, D), lambda i, eid, fi, va: (eid[i], 0, 0)),
        ],
        out_specs=pl.BlockSpec((blk, D), lambda i, eid, fi, va: (i, 0)),
        scratch_shapes=[pltpu.VMEM((D, F), jnp.bfloat16), pltpu.VMEM((D, F), jnp.bfloat16),
                        pltpu.VMEM((F, D), jnp.bfloat16)],
    )
    return pl.pallas_call(
        _expert_kernel,
        grid_spec=grid_spec,
        out_shape=jax.ShapeDtypeStruct((rows, D), jnp.float32),
        compiler_params=pltpu.CompilerParams(
            dimension_semantics=("arbitrary",), vmem_limit_bytes=VMEM_LIMIT_BYTES),
    )(block_eid, block_first, block_valid, buf, w1, w3, w2)


def _combine_kernel(dest_ref, x1_ref, rf_ref, ga_ref, gf_ref, yb_hbm, o_ref, ybuf, sem):
    b = pl.program_id(0)
    i = pl.program_id(1)
    n_i = pl.num_programs(1)
    tm = x1_ref.shape[1]
    step = b * n_i + i
    n_steps = pl.num_programs(0) * n_i
    slot = step % 2

    def issue(step_, slot_):
        def body(j, carry):
            t = step_ * tm + j
            _row_copy(yb_hbm, dest_ref[t], ybuf.at[slot_, 0], j, sem.at[slot_]).start()
            _row_copy(yb_hbm, dest_ref[n_steps * tm + t], ybuf.at[slot_, 1], j, sem.at[slot_]).start()
            return carry
        lax.fori_loop(0, tm, body, 0, unroll=8)

    @pl.when(step == 0)
    def _():
        issue(0, 0)

    @pl.when(step + 1 < n_steps)
    def _():
        issue(step + 1, 1 - slot)

    pltpu.make_async_copy(yb_hbm.at[pl.ds(0, tm), :], ybuf.at[slot, 0], sem.at[slot]).wait()
    pltpu.make_async_copy(yb_hbm.at[pl.ds(0, tm), :], ybuf.at[slot, 1], sem.at[slot]).wait()
    rf = rf_ref[0]
    y = rf[:, 0:1] * ybuf[slot, 0] + rf[:, 1:2] * ybuf[slot, 1]
    x2 = x1_ref[0] + ga_ref[0] * y
    o_ref[0] = x2 * lax.rsqrt(jnp.mean(x2 * x2, axis=-1, keepdims=True) + EPS) * gf_ref[...]


def moe_combine(x1, route_f, ga2, g_final, yb, dest, tm):
    bsz, L, D = x1.shape
    grid_spec = pltpu.PrefetchScalarGridSpec(
        num_scalar_prefetch=1,
        grid=(bsz, L // tm),
        in_specs=[
            pl.BlockSpec((1, tm, D), lambda b, i, d: (b, i, 0)),
            pl.BlockSpec((1, tm, ROUTE_COLS), lambda b, i, d: (b, i, 0)),
            pl.BlockSpec((1, 1, D), lambda b, i, d: (b, 0, 0)),
            pl.BlockSpec((1, D), lambda b, i, d: (0, 0)),
            pl.BlockSpec(memory_space=pl.ANY),
        ],
        out_specs=pl.BlockSpec((1, tm, D), lambda b, i, d: (b, i, 0)),
        scratch_shapes=[pltpu.VMEM((2, 2, tm, D), jnp.float32), pltpu.SemaphoreType.DMA((2,))],
    )
    return pl.pallas_call(
        _combine_kernel,
        grid_spec=grid_spec,
        out_shape=jax.ShapeDtypeStruct((bsz, L, D), jnp.float32),
        compiler_params=pltpu.CompilerParams(
            dimension_semantics=("arbitrary", "arbitrary"), vmem_limit_bytes=VMEM_LIMIT_BYTES),
    )(dest, x1, route_f, ga2, g_final, yb)


def moe_plan(route_i, counts, blk, n_blocks):
    cnt = counts[0, MOE_GROUPS:MOE_GROUPS + N_EXPERTS].astype(jnp.int32)
    padded = (cnt + blk - 1) // blk * blk
    ends = jnp.cumsum(padded)
    starts = ends - padded
    experts = jnp.arange(N_EXPERTS, dtype=jnp.int32)
    dest = jnp.concatenate([
        (jnp.sum(jnp.where(route_i[:, k, :, None] == experts, starts, 0), axis=-1) + route_i[:, 2 + k]).reshape(-1)
        for k in range(TOP_K)])
    first_row = jnp.arange(n_blocks, dtype=jnp.int32) * blk
    block_eid = jnp.minimum(jnp.sum((ends[None, :] <= first_row[:, None]).astype(jnp.int32), axis=1), N_EXPERTS - 1)
    block_valid = jnp.clip(cnt[block_eid] - (first_row - starts[block_eid]), 0, blk).astype(jnp.int32)
    block_first = jnp.concatenate([jnp.ones((1,), jnp.int32),
                                   (block_eid[1:] != block_eid[:-1]).astype(jnp.int32)])
    return dest, block_eid, block_first, block_valid


def dft_tables(L):
    n = 2 * L
    f = lax.broadcasted_iota(jnp.int32, (L, L), 0)
    t = lax.broadcasted_iota(jnp.int32, (L, L), 1)
    ang = ((f * t) % n).astype(jnp.float32) * (2.0 * math.pi / n)
    return jnp.cos(ang).astype(jnp.bfloat16), jnp.sin(ang).astype(jnp.bfloat16)


def _alt_sign(L):
    t = lax.broadcasted_iota(jnp.int32, (L, 1), 0)
    return (1 - 2 * (t & 1)).astype(jnp.float32)


def _spectrum_kernel(a_ref, b_ref, c_ref, s_ref, kr_ref, ks_ref, kn_ref):
    L = a_ref.shape[1]
    a = a_ref[0]
    row = lax.broadcasted_iota(jnp.int32, (L, 1), 0)
    scale = jnp.where(row == 0, 0.5 / L, 1.0 / L)
    kr_ref[0] = scale * jnp.dot(c_ref[...], a.astype(jnp.bfloat16), preferred_element_type=jnp.float32)
    ks_ref[0] = scale * jnp.dot(s_ref[...], b_ref[0].astype(jnp.bfloat16), preferred_element_type=jnp.float32)
    kn_ref[0] = jnp.sum(a * _alt_sign(L), axis=0, keepdims=True) * (0.5 / L)


def filter_spectrum(a, b, cos_t, sin_t, tc):
    n, L, C = a.shape
    blk = pl.BlockSpec((1, L, tc), lambda o, j: (o, 0, j))
    tab = pl.BlockSpec((L, L), lambda o, j: (0, 0))
    return pl.pallas_call(
        _spectrum_kernel,
        grid=(n, C // tc),
        in_specs=[blk, blk, tab, tab],
        out_specs=[blk, blk, pl.BlockSpec((1, 1, tc), lambda o, j: (o, 0, j))],
        out_shape=[jax.ShapeDtypeStruct((n, L, C), jnp.float32)] * 2 + [jax.ShapeDtypeStruct((n, 1, C), jnp.float32)],
        compiler_params=pltpu.CompilerParams(
            dimension_semantics=("arbitrary", "arbitrary"), vmem_limit_bytes=VMEM_LIMIT_BYTES),
    )(a, b, cos_t, sin_t)


def _phase_conv3(raw, w_ref, b_ref, rows_per_phase):
    n_ph = len(raw)
    h = raw[0].shape[0]
    j = lax.broadcasted_iota(jnp.int32, (h, 1), 0) % rows_per_phase
    prev0 = jnp.where(j != 0, pltpu.roll(raw[n_ph - 1], 1, 0), 0.0)
    next_last = jnp.where(j != rows_per_phase - 1, pltpu.roll(raw[0], h - 1, 0), 0.0)
    out = []
    for p in range(n_ph):
        prev = raw[p - 1] if p > 0 else prev0
        nxt = raw[p + 1] if p < n_ph - 1 else next_last
        out.append(b_ref[...] + w_ref[0:1, :] * prev + w_ref[1:2, :] * raw[p] + w_ref[2:3, :] * nxt)
    return out


def _long_conv_kernel(*refs, n_ph, n_slab, conv_z, rows_per_phase):
    z_refs = refs[:n_slab]
    xn_refs = refs[n_slab:2 * n_slab]
    (kr_ref, ks_ref, kn_ref, bias_ref, cwz_ref, cbz_ref, cwx_ref, cbx_ref, c_ref, s_ref,
     o_ref, acc_ref, zr_ref, zs_ref, yr_ref, ys_ref, stage_ref) = refs[2 * n_slab:]
    H = z_refs[0].shape[1] // n_ph
    f32 = jnp.float32
    bf = jnp.bfloat16
    sign = _alt_sign(H)

    def phases(slab_refs):
        return [jnp.concatenate([r[0, pl.ds(p, H, stride=n_ph), :] for r in slab_refs], axis=1)
                for p in range(n_ph)]

    z_ph = phases(z_refs)
    if conv_z:
        z_ph = _phase_conv3(z_ph, cwz_ref, cbz_ref, rows_per_phase)
    for q in range(n_ph):
        zb = z_ph[q].astype(bf)
        zr_ref[q] = jnp.dot(c_ref[...], zb, preferred_element_type=f32)
        zs_ref[q] = jnp.dot(s_ref[...], zb, preferred_element_type=f32)
    z_nyq = [jnp.sum(z * sign, axis=0, keepdims=True) for z in z_ph]
    for p in range(n_ph):
        nyq = sum(z_nyq[q] * kn_ref[p - q + n_ph - 1] for q in range(n_ph))
        acc_ref[p] = z_ph[p] * bias_ref[0] + sign * nyq
        yr = 0.0
        ys = 0.0
        for q in range(n_ph):
            slot = p - q + n_ph - 1
            yr = yr + zr_ref[q] * kr_ref[slot] - zs_ref[q] * ks_ref[slot]
            ys = ys + zr_ref[q] * ks_ref[slot] + zs_ref[q] * kr_ref[slot]
        yr_ref[p] = yr.astype(bf)
        ys_ref[p] = ys.astype(bf)
    for p in range(n_ph):
        acc_ref[p] += (jnp.dot(c_ref[...], yr_ref[p], preferred_element_type=f32)
                       + jnp.dot(s_ref[...], ys_ref[p], preferred_element_type=f32))
    x_ph = _phase_conv3(phases(xn_refs), cwx_ref, cbx_ref, rows_per_phase)
    for p in range(n_ph):
        out_p = x_ph[p] * acc_ref[p]
        for sl in range(n_slab):
            stage_ref[sl, pl.ds(p, H, stride=n_ph), :] = out_p[:, sl * LANES:(sl + 1) * LANES]
    for sl in range(n_slab):
        o_ref[0, :, sl * LANES:(sl + 1) * LANES] = stage_ref[sl].astype(o_ref.dtype)


def long_conv_gate(z_arr, z_col, conv_z, xn_arr, xn_col, conv_w, conv_b, kr, ks, kn, bias, cos_t, sin_t,
                   tc, n_ph, row_len, out_dtype):
    bsz, L, _ = z_arr.shape
    H = L // n_ph
    C = kr.shape[-1]
    nj = C // tc
    n_slab = tc // LANES
    n_f = 2 * n_ph - 1
    tab = pl.BlockSpec((H, H), lambda j, b: (0, 0), pipeline_mode=pl.Buffered(1))
    spec = pl.BlockSpec((n_f, H, tc), lambda j, b: (0, 0, j), pipeline_mode=pl.Buffered(1))
    nyq = pl.BlockSpec((n_f, 1, tc), lambda j, b: (0, 0, j))
    vec = pl.BlockSpec((1, 1, tc), lambda j, b: (0, 0, j))

    def slabs(col):
        return [pl.BlockSpec((1, L, LANES),
                             functools.partial(lambda j, b, sl: (b, 0, (col * nj + j) * n_slab + sl), sl=sl))
                for sl in range(n_slab)]

    def conv_specs(col):
        return [pl.BlockSpec((3, tc), lambda j, b: (0, col * nj + j)),
                pl.BlockSpec((1, tc), lambda j, b: (0, col * nj + j))]

    zc = z_col if conv_z else 0
    kern = functools.partial(_long_conv_kernel, n_ph=n_ph, n_slab=n_slab, conv_z=conv_z,
                             rows_per_phase=row_len // n_ph)
    return pl.pallas_call(
        kern,
        grid=(nj, bsz),
        in_specs=(slabs(z_col) + slabs(xn_col) + [spec, spec, nyq, vec] + conv_specs(zc) + conv_specs(xn_col)
                  + [tab, tab]),
        out_specs=pl.BlockSpec((1, L, tc), lambda j, b: (b, 0, j)),
        out_shape=jax.ShapeDtypeStruct((bsz, L, C), out_dtype),
        scratch_shapes=[pltpu.VMEM((n_ph, H, tc), jnp.float32), pltpu.VMEM((n_ph, H, tc), jnp.float32),
                        pltpu.VMEM((n_ph, H, tc), jnp.float32), pltpu.VMEM((n_ph, H, tc), jnp.bfloat16),
                        pltpu.VMEM((n_ph, H, tc), jnp.bfloat16), pltpu.VMEM((n_slab, L, LANES), jnp.float32)],
        compiler_params=pltpu.CompilerParams(
            dimension_semantics=("arbitrary", "arbitrary"), vmem_limit_bytes=VMEM_LIMIT_BYTES),
    )(*([z_arr] * n_slab), *([xn_arr] * n_slab), kr, ks, kn, bias, conv_w, conv_b, conv_w, conv_b, cos_t, sin_t)


def _polyphase_taps(kf, kb, n_ph):
    H = kf.shape[0] // n_ph
    ph = lambda a, p: a[p * H:(p + 1) * H]
    zero = jnp.zeros_like(kf[:1])
    plus, minus = [], []
    for r in range(-(n_ph - 1), n_ph):
        if r >= 0:
            plus.append(ph(kf, r))
        else:
            plus.append(jnp.concatenate([ph(kb, -r)[0:1], ph(kf, n_ph + r)[:-1]], axis=0))
        if r <= 0:
            minus.append(jnp.concatenate([zero, ph(kb, -r)[1:]], axis=0))
        else:
            minus.append(jnp.concatenate([zero, ph(kb, n_ph - r)[:-1]], axis=0))
    return jnp.stack(plus), jnp.stack(minus)


def hyena_long_convs(p_hy, conv_w, conv_b, kp, h_bias, tc, n_ph, row_len):
    L = p_hy.shape[1]
    C = h_bias.shape[1]
    cos_t, sin_t = dft_tables(L // n_ph)
    z = p_hy
    for o in range(h_bias.shape[0]):
        fwd = slice((2 * o) * C, (2 * o + 1) * C)
        bwd = slice((2 * o + 1) * C, (2 * o + 2) * C)
        plus, minus = _polyphase_taps(kp[:, fwd], kp[:, bwd], n_ph)
        kr, ks, kn = filter_spectrum(plus + minus, plus - minus, cos_t, sin_t, tc)
        last = o == h_bias.shape[0] - 1
        z = long_conv_gate(z, 0, o == 0, p_hy, o + 1, conv_w, conv_b, kr, ks, kn, h_bias[o][None, None, :],
                           cos_t, sin_t, tc, n_ph, row_len, jnp.bfloat16 if last else jnp.float32)
    return z


def _filter_kernel(band_ref, w1_ref, b1_ref, fr_ref, w2_ref, b2_ref, w3_ref, dl_ref, k_ref, *, seq_len, n_ph):
    hp = lax.Precision.HIGHEST
    f32 = jnp.float32
    tp = k_ref.shape[0]
    per_phase = seq_len // n_ph
    g = lax.broadcasted_iota(jnp.int32, (tp, 1), 0) + pl.program_id(0) * tp
    phase = g // per_phase
    pos = (n_ph * (g - phase * per_phase) + phase).astype(f32)
    t = pos / max(seq_len - 1, 1)
    ang = (2 * math.pi / seq_len) * pos * band_ref[...]
    lane = lax.broadcasted_iota(jnp.int32, (tp, LANES), 1)
    feats = jnp.where(lane == 0, t,
                      jnp.where(lane <= HYENA_BANDS, jnp.cos(ang),
                                jnp.where(lane <= 2 * HYENA_BANDS, -jnp.sin(ang), 0.0)))
    h = jnp.sin(fr_ref[...] * (jnp.dot(feats, w1_ref[...], precision=hp, preferred_element_type=f32) + b1_ref[...]))
    h = jnp.sin(fr_ref[...] * (jnp.dot(h, w2_ref[...], precision=hp, preferred_element_type=f32) + b2_ref[...]))
    window = jnp.exp(-t * dl_ref[...])
    c = dl_ref.shape[1]
    for j in range(w3_ref.shape[1] // c):
        cols = slice(j * c, (j + 1) * c)
        k_ref[:, cols] = jnp.dot(h, w3_ref[:, cols], precision=hp, preferred_element_type=f32) * window


def hyena_filters_polyphase(seq_len, f_w1, f_b1, f_freq, f_w2, f_b2, f_w3, d_hyena, tp, n_ph):
    f32 = jnp.float32
    fh = f_w1.shape[1]
    n_emb = 1 + 2 * HYENA_BANDS
    bands = jnp.linspace(1e-4, HYENA_BANDS - 1, HYENA_BANDS, dtype=f32)
    band_row = jnp.concatenate([jnp.zeros((1,), f32), bands, bands, jnp.zeros((LANES - n_emb,), f32)])[None, :]
    w1p = jnp.concatenate([f_w1, jnp.zeros((LANES - n_emb, fh), f32)], axis=0)
    deltas = jnp.abs(jnp.linspace(math.log(HYENA_TARGET) / HYENA_SLOW_DECAY,
                                  math.log(HYENA_TARGET) / HYENA_FAST_DECAY, d_hyena, dtype=f32))[None, :]
    n_out = f_w3.shape[1]
    full = lambda a: pl.BlockSpec(a.shape, lambda i: (0,) * a.ndim)
    args = (band_row, w1p, f_b1[None, :], f_freq[None, :], f_w2, f_b2[None, :], f_w3, deltas)
    return pl.pallas_call(
        functools.partial(_filter_kernel, seq_len=seq_len, n_ph=n_ph),
        grid=(seq_len // tp,),
        in_specs=[full(a) for a in args],
        out_specs=pl.BlockSpec((tp, n_out), lambda i: (i, 0)),
        out_shape=jax.ShapeDtypeStruct((seq_len, n_out), f32),
        compiler_params=pltpu.CompilerParams(dimension_semantics=("arbitrary",), vmem_limit_bytes=VMEM_LIMIT_BYTES),
    )(*args)


def _ada_kernel(c_ref, w_ref, b_ref, o_ref):
    cv = c_ref[...]
    s = cv * jax.nn.sigmoid(cv)
    o_ref[...] = jnp.dot(s, w_ref[...], precision=lax.Precision.HIGHEST,
                         preferred_element_type=jnp.float32) + b_ref[...]


def ada_modulation(c_rows, w_ada, b_ada, tn):
    rows, D = c_rows.shape
    N = w_ada.shape[1]
    return pl.pallas_call(
        _ada_kernel,
        grid=(N // tn,),
        in_specs=[pl.BlockSpec((rows, D), lambda j: (0, 0)),
                  pl.BlockSpec((D, tn), lambda j: (0, j)),
                  pl.BlockSpec((1, tn), lambda j: (0, j))],
        out_specs=pl.BlockSpec((rows, tn), lambda j: (0, j)),
        out_shape=jax.ShapeDtypeStruct((rows, N), jnp.float32),
        compiler_params=pltpu.CompilerParams(dimension_semantics=("arbitrary",), vmem_limit_bytes=VMEM_LIMIT_BYTES),
    )(c_rows, w_ada, b_ada[None, :])


def _ssd_kernel(xf_ref, df_ref, xb_ref, db_ref, dskip_ref, y_ref, h_ref, *, n_ctx_chunks):
    s = pl.program_id(1)
    n_steps = pl.num_programs(1)
    Q, G, R, P, N = SSD_CHUNK, SSD_GROUPS, SSD_HPG, SSD_HEAD_DIM, SSD_STATE
    GP = R * P
    bf = jnp.bfloat16

    @pl.when(s == 0)
    def _():
        h_ref[...] = jnp.zeros_like(h_ref)
        y_ref[...] = jnp.zeros_like(y_ref)

    row = lax.broadcasted_iota(jnp.int32, (Q, Q), 0)
    col = lax.broadcasted_iota(jnp.int32, (Q, Q), 1)
    lane_head = lax.broadcasted_iota(jnp.int32, (Q, GP), 1) // P
    is_latent = s >= n_ctx_chunks
    n_lat = n_steps - n_ctx_chunks
    out_chunk = (jnp.clip(s - n_ctx_chunks, 0, n_lat - 1), jnp.clip(n_steps - 1 - s, 0, n_lat - 1))

    for bb, d in [(bb, d) for bb in range(y_ref.shape[0]) for d in range(2)]:
        x_ref, da_ref = ((xf_ref, df_ref), (xb_ref, db_ref))[d]
        mask = (row >= col) if d == 0 else (col >= row)
        tri = mask.astype(jnp.float32)
        da = da_ref[bb]
        cum = jnp.dot(tri, da, precision=lax.Precision.HIGHEST, preferred_element_type=jnp.float32)
        cum_t = cum.T
        edge = Q - 1 if d == 0 else 0
        blk = x_ref.at[bb]
        for g in range(G):
            xg = blk[:, g * GP:(g + 1) * GP]
            bg = blk[:, D_SSD + g * N:D_SSD + (g + 1) * N].astype(bf)
            cg = blk[:, D_SSD + G * N + g * N:D_SSD + G * N + (g + 1) * N].astype(bf)
            heads = [d * SSD_HEADS + g * R + r for r in range(R)]
            dtm = jnp.zeros((Q, GP), jnp.float32)
            cumm = jnp.zeros((Q, GP), jnp.float32)
            for r, h in enumerate(heads):
                sel = lane_head == r
                dtm = jnp.where(sel, da[:, h:h + 1], dtm)
                cumm = jnp.where(sel, cum[:, SSD_HEADS * 2 + h:SSD_HEADS * 2 + h + 1], cumm)
            totm = cumm[edge:edge + 1, :]
            xdt = xg * dtm
            hg = h_ref[bb, d, g * GP:(g + 1) * GP, :]

            gmat = lax.dot_general(cg, bg, (((1,), (1,)), ((), ())), preferred_element_type=jnp.float32)
            y_off = lax.dot_general(cg, hg.astype(bf), (((1,), (1,)), ((), ())),
                                    preferred_element_type=jnp.float32) * jnp.exp(cumm)
            if d == 0:
                y_off = y_off + dskip_ref[:, g * GP:(g + 1) * GP] * xg
            parts = []
            for r, h in enumerate(heads):
                a_col = cum[:, SSD_HEADS * 2 + h:SSD_HEADS * 2 + h + 1]
                a_row = cum_t[SSD_HEADS * 2 + h:SSD_HEADS * 2 + h + 1, :]
                decay = jnp.exp(jnp.where(mask, a_col - a_row, NEG_BIG))
                parts.append(jnp.dot((gmat * decay).astype(bf), xdt[:, r * P:(r + 1) * P].astype(bf),
                                     preferred_element_type=jnp.float32))
            y = jnp.where(is_latent, y_off + jnp.concatenate(parts, axis=-1), 0.0)
            rows = pl.ds(pl.multiple_of(out_chunk[d] * Q, Q), Q)
            y_ref[bb, rows, g * GP:(g + 1) * GP] += y

            xw = (xdt * jnp.exp(totm - cumm)).astype(bf)
            st = lax.dot_general(xw, bg, (((0,), (0,)), ((), ())), preferred_element_type=jnp.float32)
            for r, h in enumerate(heads):
                dec = jnp.exp(cum_t[SSD_HEADS * 2 + h:SSD_HEADS * 2 + h + 1, edge:edge + 1])
                rs = slice(g * GP + r * P, g * GP + (r + 1) * P)
                h_ref[bb, d, rs, :] = h_ref[bb, d, rs, :] * dec + st[r * P:(r + 1) * P, :]


def ssd_scan_bidir(xbc, dta, d_skip, n_ctx, lat_off, nb):
    bsz, lt, width = xbc.shape
    Q = SSD_CHUNK
    n_ctx_chunks = n_ctx // Q
    L = lt - lat_off
    n_lat = L // Q
    n_steps = n_ctx_chunks + n_lat
    lat0 = lat_off // Q

    def fwd_chunk(s):
        return jnp.where(s < n_ctx_chunks, s, s - n_ctx_chunks + lat0)

    def bwd_chunk(s):
        return jnp.where(s < n_ctx_chunks, n_ctx_chunks - 1 - s, n_steps - 1 - s + lat0)

    return pl.pallas_call(
        functools.partial(_ssd_kernel, n_ctx_chunks=n_ctx_chunks),
        grid=(bsz // nb, n_steps),
        in_specs=[
            pl.BlockSpec((nb, Q, width), lambda b, s: (b, fwd_chunk(s), 0)),
            pl.BlockSpec((nb, Q, LANES), lambda b, s: (b, fwd_chunk(s), 0)),
            pl.BlockSpec((nb, Q, width), lambda b, s: (b, bwd_chunk(s), 0)),
            pl.BlockSpec((nb, Q, LANES), lambda b, s: (b, bwd_chunk(s), 0)),
            pl.BlockSpec((1, D_SSD), lambda b, s: (0, 0)),
        ],
        out_specs=pl.BlockSpec((nb, L, D_SSD), lambda b, s: (b, 0, 0)),
        out_shape=jax.ShapeDtypeStruct((bsz, L, D_SSD), jnp.float32),
        scratch_shapes=[pltpu.VMEM((nb, 2, SSD_GROUPS * SSD_HPG * SSD_HEAD_DIM, SSD_STATE), jnp.float32)],
        compiler_params=pltpu.CompilerParams(
            dimension_semantics=("arbitrary", "arbitrary"), vmem_limit_bytes=VMEM_LIMIT_BYTES),
    )(xbc, dta, xbc, dta, d_skip)


def kernel(x, c, ctx, c_ctx, w_ada, b_ada, g_norm1, g_norm2, w_in, hy_conv_w, hy_conv_b, hy_f_w1, hy_f_b1, hy_f_freq, hy_f_w2, hy_f_b2, hy_f_w3, hy_bias, ssd_conv_w, ssd_conv_b, ssd_a_log, ssd_dt_bias, ssd_d, ssd_norm_g, w_out, w_group, b_group, w_expert, b_expert, w1, w3, w2, g_final):
    bsz, seq_len, _ = x.shape
    l = 0
    rows_pad = -(bsz + 1) % SUBLANES
    c_rows = jnp.concatenate([c, c_ctx[None, :], jnp.zeros((rows_pad, D_MODEL), jnp.float32)], axis=0)
    mod_all = ada_modulation(c_rows, w_ada[l], b_ada[l], 512)
    sh1, sc1, ga1, sh2, sc2, ga2 = jnp.split(mod_all[:bsz, None, :], 6, axis=-1)
    csh1, csc1 = mod_all[bsz, :D_MODEL], mod_all[bsz, D_MODEL:2 * D_MODEL]

    w_out_bf = w_out[l].astype(jnp.bfloat16)
    w_in_bf, w_dt_bf, dt_bias2, dt_mult = in_proj_params(w_in[l], ssd_a_log[l], ssd_dt_bias[l],
                                                         HY_COLS, D_SSD, D_XBC)

    ctx_len = ctx.shape[1]
    lat_off = -(-ctx_len // IN_PROJ_ROWS) * IN_PROJ_ROWS
    p_hy, z, xbc, dta = in_proj_fused(ctx, x, g_norm1[l][None, :], csh1[None, :], csc1[None, :], sh1, sc1,
                                      w_in_bf, w_dt_bf, HY_COLS, ssd_conv_w[l], ssd_conv_b[l][None, :],
                                      dt_bias2, dt_mult, GRID_W, ctx_len, IN_PROJ_ROWS, 512)
    kp = hyena_filters_polyphase(seq_len, hy_f_w1[l], hy_f_b1[l], hy_f_freq[l], hy_f_w2[l], hy_f_b2[l],
                                 hy_f_w3[l], D_HYENA, 256, HYENA_PHASES)
    y_hy = hyena_long_convs(p_hy, hy_conv_w[l], hy_conv_b[l][None, :], kp, hy_bias[l], 256, HYENA_PHASES, GRID_W)
    y_scan = ssd_scan_bidir(xbc, dta, jnp.repeat(ssd_d[l], SSD_HEAD_DIM)[None, :], ctx_len, lat_off, 2)

    pad = LANES - MOE_GROUPS - N_EXPERTS
    w_router = jnp.concatenate([w_group[l], w_expert[l], jnp.zeros((D_MODEL, pad), jnp.float32)], axis=1)
    w_router_hi = w_router.astype(jnp.bfloat16)
    w_router_lo = (w_router - w_router_hi.astype(jnp.float32)).astype(jnp.bfloat16)
    w_router = jnp.stack([w_router_hi, w_router_lo])
    b_router = jnp.concatenate([b_group[l], b_expert[l], jnp.zeros((pad,), jnp.float32)])[None, :]
    x1, hn, route_i, route_f, counts = out_proj_router(
        y_hy, y_scan, z, 0, x, ga1, sc2, sh2, ssd_norm_g[l][None, :], g_norm2[l][None, :],
        w_out_bf, w_router, b_router, 256)
    n_tok = bsz * seq_len
    n_blocks = -(-n_tok * TOP_K // MOE_BLOCK) + N_EXPERTS
    dest, block_eid, block_first, block_valid = moe_plan(route_i, counts, MOE_BLOCK, n_blocks)
    buf = moe_dispatch(hn.reshape(n_tok, D_MODEL), dest, block_valid, MOE_BLOCK, 256)
    yb = expert_blocks(buf, block_eid, block_first, block_valid, w1[l], w3[l], w2[l], MOE_BLOCK)
    return moe_combine(x1, route_f, ga2, g_final[None, :], yb, dest, 256)
```

```python
import functools
import math

import jax
import jax.numpy as jnp
from jax import lax
from jax.experimental import pallas as pl
from jax.experimental.pallas import tpu as pltpu

D_MODEL = 1024
CTX_LEN = 256
GRID_W = 64
EPS = 1e-6
SHORT_CONV = 3

D_HYENA = D_MODEL // 2
HYENA_ORDER = 2
HYENA_BANDS = 8
HYENA_FAST_DECAY = 0.3
HYENA_SLOW_DECAY = 1.5
HYENA_TARGET = 1e-2
HYENA_PHASES = 4

D_SSD = D_MODEL // 2
SSD_HEAD_DIM = 64
SSD_HEADS = D_SSD // SSD_HEAD_DIM
SSD_GROUPS = 2
SSD_HPG = SSD_HEADS // SSD_GROUPS
SSD_STATE = 128
SSD_CHUNK = 128

D_XBC = D_SSD + 2 * SSD_GROUPS * SSD_STATE
HY_COLS = (HYENA_ORDER + 1) * D_HYENA
D_IN = HY_COLS + D_SSD + D_XBC + 2 * SSD_HEADS
LANES = 128
SUBLANES = 8
D_IN_PAD = -(-D_IN // LANES) * LANES

MOE_GROUPS = 8
EXPERTS_PER_GROUP = 8
N_EXPERTS = MOE_GROUPS * EXPERTS_PER_GROUP
TOP_K = 2
D_EXPERT = 512
MOE_BLOCK = 256
ROUTE_COLS = 8
IN_PROJ_ROWS = 512

VMEM_LIMIT_BYTES = 56 * 1024 * 1024
NEG_BIG = -1e30


def _conv3_rows(p, w_ref, b_ref, cols, has_prev, has_next):
    n = p.shape[0]
    prev = jnp.where(has_prev, pltpu.roll(p, 1, 0), 0.0)
    nxt = jnp.where(has_next, pltpu.roll(p, n - 1, 0), 0.0)
    return b_ref[:, cols] + w_ref[0:1, cols] * prev + w_ref[1:2, cols] * p + w_ref[2:3, cols] * nxt


def _in_proj_kernel(ctx_ref, x_ref, g_ref, csh_ref, csc_ref, sh_ref, sc_ref, w_ref, wdt_ref,
                    sw_ref, sb_ref, dtb_ref, dtm_ref, u_ref, z_ref, xbc_ref, dta_ref, h_ref,
                    *, n_ctx_steps, row_len, ctx_row_len, hy_cols, d_ssd, d_xbc, tn):
    i = pl.program_id(1)
    is_ctx = i < n_ctx_steps
    tm = x_ref.shape[1]
    ctx_t = ctx_ref[0]
    if ctx_t.shape[0] < tm:
        ctx_t = jnp.concatenate([ctx_t, jnp.zeros((tm - ctx_t.shape[0], ctx_t.shape[1]), ctx_t.dtype)], axis=0)
    xin = jnp.where(is_ctx, ctx_t, x_ref[0])
    shift = jnp.where(is_ctx, csh_ref[...], sh_ref[0])
    scale = jnp.where(is_ctx, csc_ref[...], sc_ref[0])
    y = xin * lax.rsqrt(jnp.mean(xin * xin, axis=-1, keepdims=True) + EPS) * g_ref[...]
    h_ref[...] = (y * (1.0 + scale) + shift).astype(jnp.bfloat16)

    pos = lax.broadcasted_iota(jnp.int32, (tm, 1), 0) + jnp.where(is_ctx, i, i - n_ctx_steps) * tm
    in_row = jnp.where(is_ctx, pos % ctx_row_len, pos % row_len)
    has_prev = in_row != 0
    has_next = in_row != jnp.where(is_ctx, ctx_row_len - 1, row_len - 1)

    for c0 in range(0, hy_cols, tn):
        cols = slice(c0, c0 + tn)
        u_ref[0, :, cols] = jnp.dot(h_ref[...], w_ref[:, cols], preferred_element_type=jnp.float32)
    z_ref[0] = jnp.dot(h_ref[...], w_ref[:, hy_cols:hy_cols + d_ssd], preferred_element_type=jnp.float32)

    for c0 in range(0, d_xbc, tn):
        cols = slice(c0, c0 + tn)
        wc = slice(hy_cols + d_ssd + c0, hy_cols + d_ssd + c0 + tn)
        p = jnp.dot(h_ref[...], w_ref[:, wc], preferred_element_type=jnp.float32)
        v = _conv3_rows(p, sw_ref, sb_ref, cols, has_prev, has_next)
        xbc_ref[0, :, cols] = v * jax.nn.sigmoid(v)
    pd = jnp.dot(h_ref[...], wdt_ref[...], preferred_element_type=jnp.float32) + dtb_ref[...]
    sp = jnp.maximum(pd, 0.0) + jnp.log(1.0 + jnp.exp(-jnp.abs(pd)))
    dta_ref[0] = sp * dtm_ref[...]


def in_proj_fused(ctx, x, g1, csh, csc, sh, sc, w_bf, wdt_bf, hy_cols, ssd_w, ssd_b, dt_bias2, dt_mult,
                  row_len, ctx_row_len, tm, tn):
    bsz, L, D = x.shape
    lc = ctx.shape[1]
    d_xbc = ssd_w.shape[1]
    d_ssd = w_bf.shape[1] - hy_cols - d_xbc
    n_ctx_steps = -(-lc // tm)
    ctx_rows = min(lc, tm)
    lc = n_ctx_steps * tm
    n_steps = n_ctx_steps + L // tm
    lat = lambda b, i: (b, jnp.maximum(i - n_ctx_steps, 0), 0)
    allt = lambda b, i: (b, i, 0)
    const2 = lambda b, i: (0, 0)
    per_b = pl.BlockSpec((1, 1, D), lambda b, i: (b, 0, 0))
    kern = functools.partial(_in_proj_kernel, n_ctx_steps=n_ctx_steps, row_len=row_len, ctx_row_len=ctx_row_len,
                             hy_cols=hy_cols, d_ssd=d_ssd, d_xbc=d_xbc, tn=tn)
    return pl.pallas_call(
        kern,
        grid=(bsz, n_steps),
        in_specs=[
            pl.BlockSpec((1, ctx_rows, D), lambda b, i: (b, jnp.minimum(i, n_ctx_steps - 1), 0)),
            pl.BlockSpec((1, tm, D), lat),
            pl.BlockSpec((1, D), const2),
            pl.BlockSpec((1, D), const2),
            pl.BlockSpec((1, D), const2),
            per_b, per_b,
            pl.BlockSpec(w_bf.shape, const2),
            pl.BlockSpec(wdt_bf.shape, const2),
            pl.BlockSpec(ssd_w.shape, const2),
            pl.BlockSpec(ssd_b.shape, const2),
            pl.BlockSpec((1, LANES), const2),
            pl.BlockSpec((1, LANES), const2),
        ],
        out_specs=[
            pl.BlockSpec((1, tm, hy_cols), lat),
            pl.BlockSpec((1, tm, d_ssd), lat),
            pl.BlockSpec((1, tm, d_xbc), allt),
            pl.BlockSpec((1, tm, LANES), allt),
        ],
        out_shape=[
            jax.ShapeDtypeStruct((bsz, L, hy_cols), jnp.float32),
            jax.ShapeDtypeStruct((bsz, L, d_ssd), jnp.float32),
            jax.ShapeDtypeStruct((bsz, lc + L, d_xbc), jnp.float32),
            jax.ShapeDtypeStruct((bsz, lc + L, LANES), jnp.float32),
        ],
        scratch_shapes=[pltpu.VMEM((tm, D), jnp.bfloat16)],
        compiler_params=pltpu.CompilerParams(
            dimension_semantics=("arbitrary", "arbitrary"), vmem_limit_bytes=VMEM_LIMIT_BYTES),
    )(ctx, x, g1, csh, csc, sh, sc, w_bf, wdt_bf, ssd_w, ssd_b, dt_bias2, dt_mult)


def in_proj_params(w_in, a_log, dt_bias, hy_cols, d_ssd, d_xbc):
    n_h = 2 * SSD_HEADS
    main = hy_cols + d_ssd + d_xbc
    w_dt = w_in[:, main:main + n_h]
    pad = jnp.zeros((w_in.shape[0], LANES - 2 * n_h), w_in.dtype)
    wdt = jnp.concatenate([w_dt, w_dt, pad], axis=1).astype(jnp.bfloat16)
    zpad = jnp.zeros((LANES - 2 * n_h,), jnp.float32)
    bias2 = jnp.concatenate([dt_bias.reshape(n_h), dt_bias.reshape(n_h), zpad])[None, :]
    mult = jnp.concatenate([jnp.ones((n_h,), jnp.float32), -jnp.exp(a_log).reshape(n_h), zpad])[None, :]
    return w_in[:, :main].astype(jnp.bfloat16), wdt, bias2, mult


def _out_router_kernel(yh_ref, ys_ref, z_ref, x_ref, ga_ref, sc_ref, sh_ref, ng_ref, g2_ref, wo_ref, wr_ref, br_ref,
                       x1_ref, hn_ref, ri_ref, rf_ref, cnt_ref, carry_ref):
    first = jnp.logical_and(pl.program_id(0) == 0, pl.program_id(1) == 0)

    @pl.when(first)
    def _():
        carry_ref[...] = jnp.zeros_like(carry_ref)

    bf = jnp.bfloat16
    tm = x_ref.shape[1]
    dh = yh_ref.shape[2]
    z = z_ref[0]
    ys = ys_ref[0] * (z * jax.nn.sigmoid(z))
    gw = ys.shape[1] // SSD_GROUPS
    acc = jnp.dot(yh_ref[0].astype(bf), wo_ref[0:dh, :], preferred_element_type=jnp.float32)
    for g in range(SSD_GROUPS):
        yg = ys[:, g * gw:(g + 1) * gw]
        yg = yg * lax.rsqrt(jnp.mean(yg * yg, axis=-1, keepdims=True) + EPS) * ng_ref[:, g * gw:(g + 1) * gw]
        acc += jnp.dot(yg.astype(bf), wo_ref[dh + g * gw:dh + (g + 1) * gw, :], preferred_element_type=jnp.float32)
    x1 = x_ref[0] + ga_ref[0] * acc
    x1_ref[0] = x1
    hn = x1 * lax.rsqrt(jnp.mean(x1 * x1, axis=-1, keepdims=True) + EPS) * g2_ref[...]
    hn = hn * (1.0 + sc_ref[0]) + sh_ref[0]
    hn_ref[0] = hn

    hn_hi = hn.astype(bf)
    hn_lo = (hn - hn_hi.astype(jnp.float32)).astype(bf)
    logits = (jnp.dot(hn_hi, wr_ref[0], preferred_element_type=jnp.float32)
              + jnp.dot(hn_lo, wr_ref[0], preferred_element_type=jnp.float32)
              + jnp.dot(hn_hi, wr_ref[1], preferred_element_type=jnp.float32)) + br_ref[...]
    lane = lax.broadcasted_iota(jnp.int32, (tm, LANES), 1)
    lane_f = lane.astype(jnp.float32)
    ninf = jnp.float32(-jnp.inf)
    big = jnp.float32(1e9)
    gl = jnp.where(lane < MOE_GROUPS, logits, ninf)
    gmax = jnp.max(gl, axis=-1, keepdims=True)
    p_group = 1.0 / jnp.sum(jnp.exp(gl - gmax), axis=-1, keepdims=True)
    g_sel = jnp.min(jnp.where(gl == gmax, lane_f, big), axis=-1, keepdims=True)
    e_lane = lane - MOE_GROUPS
    in_grp = jnp.logical_and(e_lane >= 0, (e_lane // EXPERTS_PER_GROUP).astype(jnp.float32) == g_sel)
    el = jnp.where(in_grp, logits, ninf)
    m1 = jnp.max(el, axis=-1, keepdims=True)
    i1 = jnp.min(jnp.where(el == m1, lane_f, big), axis=-1, keepdims=True)
    el2 = jnp.where(lane_f == i1, ninf, el)
    m2 = jnp.max(el2, axis=-1, keepdims=True)
    i2 = jnp.min(jnp.where(el2 == m2, lane_f, big), axis=-1, keepdims=True)
    t = jnp.exp(m2 - m1)
    w1 = 1.0 / (1.0 + t)
    gate1 = w1 * p_group
    gate2 = (t * w1) * p_group
    e1 = i1 - MOE_GROUPS
    e2 = i2 - MOE_GROUPS
    el_f = e_lane.astype(jnp.float32)
    oh1 = el_f == e1
    oh2 = el_f == e2
    oh = jnp.logical_or(oh1, oh2).astype(bf)
    r_i = lax.broadcasted_iota(jnp.int32, (tm, tm), 0)
    c_i = lax.broadcasted_iota(jnp.int32, (tm, tm), 1)
    before = jnp.dot((c_i < r_i).astype(bf), oh, preferred_element_type=jnp.float32) + carry_ref[...]
    rank1 = jnp.sum(jnp.where(oh1, before, 0.0), axis=-1, keepdims=True)
    rank2 = jnp.sum(jnp.where(oh2, before, 0.0), axis=-1, keepdims=True)
    carry_ref[...] += jnp.sum(oh.astype(jnp.float32), axis=0, keepdims=True)
    cnt_ref[...] = carry_ref[...]

    rec = jnp.where(lane == 0, e1, jnp.where(lane == 1, e2, jnp.where(lane == 2, rank1,
                                                                      jnp.where(lane == 3, rank2, 0.0))))
    ri_ref[0] = rec.T[0:ROUTE_COLS, :].astype(jnp.int32)
    col = lax.broadcasted_iota(jnp.int32, (tm, ROUTE_COLS), 1)
    rf_ref[0] = jnp.where(col == 0, gate1, gate2)


def out_proj_router(y_hy, y_scan, px, z_col, x, ga1, sc2, sh2, norm_g, g2, w_out_bf, w_router, b_router, tm):
    bsz, L, D = x.shape
    dh = y_hy.shape[-1]
    ds = y_scan.shape[-1]
    tok = lambda b, i: (b, i, 0)
    per_b = pl.BlockSpec((1, 1, D), lambda b, i: (b, 0, 0))
    const2 = lambda b, i: (0, 0)
    return pl.pallas_call(
        _out_router_kernel,
        grid=(bsz, L // tm),
        in_specs=[
            pl.BlockSpec((1, tm, dh), tok),
            pl.BlockSpec((1, tm, ds), tok),
            pl.BlockSpec((1, tm, ds), lambda b, i: (b, i, z_col)),
            pl.BlockSpec((1, tm, D), tok),
            per_b, per_b, per_b,
            pl.BlockSpec((1, ds), const2),
            pl.BlockSpec((1, D), const2),
            pl.BlockSpec((dh + ds, D), const2),
            pl.BlockSpec((2, D, LANES), lambda b, i: (0, 0, 0)),
            pl.BlockSpec((1, LANES), const2),
        ],
        out_specs=[
            pl.BlockSpec((1, tm, D), tok),
            pl.BlockSpec((1, tm, D), tok),
            pl.BlockSpec((1, ROUTE_COLS, tm), lambda b, i: (b, 0, i)),
            pl.BlockSpec((1, tm, ROUTE_COLS), tok),
            pl.BlockSpec((1, LANES), const2),
        ],
        out_shape=[
            jax.ShapeDtypeStruct((bsz, L, D), jnp.float32),
            jax.ShapeDtypeStruct((bsz, L, D), jnp.float32),
            jax.ShapeDtypeStruct((bsz, ROUTE_COLS, L), jnp.int32),
            jax.ShapeDtypeStruct((bsz, L, ROUTE_COLS), jnp.float32),
            jax.ShapeDtypeStruct((1, LANES), jnp.float32),
        ],
        scratch_shapes=[pltpu.VMEM((1, LANES), jnp.float32)],
        compiler_params=pltpu.CompilerParams(
            dimension_semantics=("arbitrary", "arbitrary"), vmem_limit_bytes=VMEM_LIMIT_BYTES),
    )(y_hy, y_scan, px, x, ga1, sc2, sh2, norm_g, g2, w_out_bf, w_router, b_router)


def _row_copy(src_hbm, src_row, dst_ref, dst_row, sem):
    return pltpu.make_async_copy(src_hbm.at[pl.ds(src_row, 1), :], dst_ref.at[pl.ds(dst_row, 1), :], sem)


def _dispatch_kernel(dest_ref, valid_ref, hn_ref, buf_hbm, zeros, sem, zsem):
    step = pl.program_id(0)
    tm = hn_ref.shape[0]
    n_tok = pl.num_programs(0) * tm
    blk = zeros.shape[0]
    n_blocks = buf_hbm.shape[0] // blk

    def zero_copy(i):
        return pltpu.make_async_copy(zeros, buf_hbm.at[pl.ds(pl.multiple_of(i * blk, blk), blk), :], zsem)

    def zfill(i, carry):
        @pl.when(valid_ref[i] < blk)
        def _():
            zero_copy(i).start()
        return carry

    def zwait(i, carry):
        @pl.when(valid_ref[i] < blk)
        def _():
            zero_copy(i).wait()
        return carry

    @pl.when(step == 0)
    def _():
        zeros[...] = jnp.zeros_like(zeros)
        lax.fori_loop(0, n_blocks, zfill, 0)
        lax.fori_loop(0, n_blocks, zwait, 0)

    def body(j, carry):
        t = step * tm + j
        _row_copy(hn_ref, j, buf_hbm, dest_ref[t], sem).start()
        _row_copy(hn_ref, j, buf_hbm, dest_ref[n_tok + t], sem).start()
        return carry

    lax.fori_loop(0, tm, body, 0, unroll=8)
    for _ in range(2):
        pltpu.make_async_copy(hn_ref, buf_hbm.at[pl.ds(0, tm), :], sem).wait()


def moe_dispatch(hn, dest, block_valid, blk, tm):
    T, D = hn.shape
    n_rows = block_valid.shape[0] * blk
    grid_spec = pltpu.PrefetchScalarGridSpec(
        num_scalar_prefetch=2,
        grid=(T // tm,),
        in_specs=[pl.BlockSpec((tm, D), lambda i, d, v: (i, 0))],
        out_specs=pl.BlockSpec(memory_space=pl.ANY),
        scratch_shapes=[pltpu.VMEM((blk, D), hn.dtype), pltpu.SemaphoreType.DMA(()),
                        pltpu.SemaphoreType.DMA(())],
    )
    return pl.pallas_call(
        _dispatch_kernel,
        grid_spec=grid_spec,
        out_shape=jax.ShapeDtypeStruct((n_rows, D), hn.dtype),
        compiler_params=pltpu.CompilerParams(dimension_semantics=("arbitrary",), has_side_effects=True),
    )(dest, block_valid, hn)


def _expert_kernel(eid_ref, first_ref, valid_ref, x_ref, w1_ref, w3_ref, w2_ref, o_ref, w1b, w3b, w2b):
    i = pl.program_id(0)
    del eid_ref
    bf = jnp.bfloat16

    @pl.when(first_ref[i] == 1)
    def _():
        w1b[...] = w1_ref[0].astype(bf)
        w3b[...] = w3_ref[0].astype(bf)
        w2b[...] = w2_ref[0].astype(bf)

    valid = valid_ref[i]

    @pl.when(valid > 0)
    def _():
        xb = x_ref[...].astype(bf)
        a = jnp.dot(xb, w1b[...], preferred_element_type=jnp.float32)
        b = jnp.dot(xb, w3b[...], preferred_element_type=jnp.float32)
        h = (a * jax.nn.sigmoid(a)) * b
        o_ref[...] = jnp.dot(h.astype(bf), w2b[...], preferred_element_type=jnp.float32)

    @pl.when(valid <= 0)
    def _():
        o_ref[...] = jnp.zeros_like(o_ref)


def expert_blocks(buf, block_eid, block_first, block_valid, w1, w3, w2, blk):
    rows, D = buf.shape
    n_blocks = rows // blk
    E, _, F = w1.shape
    grid_spec = pltpu.PrefetchScalarGridSpec(
        num_scalar_prefetch=3,
        grid=(n_blocks,),
        in_specs=[
            pl.BlockSpec((blk, D), lambda i, eid, fi, va: (i, 0)),
            pl.BlockSpec((1, D, F), lambda i, eid, fi, va: (eid[i], 0, 0)),
            pl.BlockSpec((1, D, F), lambda i, eid, fi, va: (eid[i], 0, 0)),
            pl.BlockSpec((1, F, D), lambda i, eid, fi, va: (eid[i], 0, 0)),
        ],
        out_specs=pl.BlockSpec((blk, D), lambda i, eid, fi, va: (i, 0)),
        scratch_shapes=[pltpu.VMEM((D, F), jnp.bfloat16), pltpu.VMEM((D, F), jnp.bfloat16),
                        pltpu.VMEM((F, D), jnp.bfloat16)],
    )
    return pl.pallas_call(
        _expert_kernel,
        grid_spec=grid_spec,
        out_shape=jax.ShapeDtypeStruct((rows, D), jnp.float32),
        compiler_params=pltpu.CompilerParams(
            dimension_semantics=("arbitrary",), vmem_limit_bytes=VMEM_LIMIT_BYTES),
    )(block_eid, block_first, block_valid, buf, w1, w3, w2)


def _combine_kernel(dest_ref, x1_ref, rf_ref, ga_ref, gf_ref, yb_hbm, o_ref, ybuf, sem):
    b = pl.program_id(0)
    i = pl.program_id(1)
    n_i = pl.num_programs(1)
    tm = x1_ref.shape[1]
    step = b * n_i + i
    n_steps = pl.num_programs(0) * n_i
    slot = step % 2

    def issue(step_, slot_):
        def body(j, carry):
            t = step_ * tm + j
            _row_copy(yb_hbm, dest_ref[t], ybuf.at[slot_, 0], j, sem.at[slot_]).start()
            _row_copy(yb_hbm, dest_ref[n_steps * tm + t], ybuf.at[slot_, 1], j, sem.at[slot_]).start()
            return carry
        lax.fori_loop(0, tm, body, 0, unroll=8)

    @pl.when(step == 0)
    def _():
        issue(0, 0)

    @pl.when(step + 1 < n_steps)
    def _():
        issue(step + 1, 1 - slot)

    pltpu.make_async_copy(yb_hbm.at[pl.ds(0, tm), :], ybuf.at[slot, 0], sem.at[slot]).wait()
    pltpu.make_async_copy(yb_hbm.at[pl.ds(0, tm), :], ybuf.at[slot, 1], sem.at[slot]).wait()
    rf = rf_ref[0]
    y = rf[:, 0:1] * ybuf[slot, 0] + rf[:, 1:2] * ybuf[slot, 1]
    x2 = x1_ref[0] + ga_ref[0] * y
    o_ref[0] = x2 * lax.rsqrt(jnp.mean(x2 * x2, axis=-1, keepdims=True) + EPS) * gf_ref[...]


def moe_combine(x1, route_f, ga2, g_final, yb, dest, tm):
    bsz, L, D = x1.shape
    grid_spec = pltpu.PrefetchScalarGridSpec(
        num_scalar_prefetch=1,
        grid=(bsz, L // tm),
        in_specs=[
            pl.BlockSpec((1, tm, D), lambda b, i, d: (b, i, 0)),
            pl.BlockSpec((1, tm, ROUTE_COLS), lambda b, i, d: (b, i, 0)),
            pl.BlockSpec((1, 1, D), lambda b, i, d: (b, 0, 0)),
            pl.BlockSpec((1, D), lambda b, i, d: (0, 0)),
            pl.BlockSpec(memory_space=pl.ANY),
        ],
        out_specs=pl.BlockSpec((1, tm, D), lambda b, i, d: (b, i, 0)),
        scratch_shapes=[pltpu.VMEM((2, 2, tm, D), jnp.float32), pltpu.SemaphoreType.DMA((2,))],
    )
    return pl.pallas_call(
        _combine_kernel,
        grid_spec=grid_spec,
        out_shape=jax.ShapeDtypeStruct((bsz, L, D), jnp.float32),
        compiler_params=pltpu.CompilerParams(
            dimension_semantics=("arbitrary", "arbitrary"), vmem_limit_bytes=VMEM_LIMIT_BYTES),
    )(dest, x1, route_f, ga2, g_final, yb)


def moe_plan(route_i, counts, blk, n_blocks):
    cnt = counts[0, MOE_GROUPS:MOE_GROUPS + N_EXPERTS].astype(jnp.int32)
    padded = (cnt + blk - 1) // blk * blk
    ends = jnp.cumsum(padded)
    starts = ends - padded
    experts = jnp.arange(N_EXPERTS, dtype=jnp.int32)
    dest = jnp.concatenate([
        (jnp.sum(jnp.where(route_i[:, k, :, None] == experts, starts, 0), axis=-1) + route_i[:, 2 + k]).reshape(-1)
        for k in range(TOP_K)])
    first_row = jnp.arange(n_blocks, dtype=jnp.int32) * blk
    block_eid = jnp.minimum(jnp.sum((ends[None, :] <= first_row[:, None]).astype(jnp.int32), axis=1), N_EXPERTS - 1)
    block_valid = jnp.clip(cnt[block_eid] - (first_row - starts[block_eid]), 0, blk).astype(jnp.int32)
    block_first = jnp.concatenate([jnp.ones((1,), jnp.int32),
                                   (block_eid[1:] != block_eid[:-1]).astype(jnp.int32)])
    idx = jnp.arange(n_blocks, dtype=jnp.int32)
    next_first = lax.cummin(jnp.where(block_first == 1, idx, n_blocks), axis=0, reverse=True)
    block_wsel = jnp.where(next_first < n_blocks, block_eid[jnp.minimum(next_first, n_blocks - 1)], block_eid)
    return dest, block_wsel, block_first, block_valid


def dft_tables(L):
    n = 2 * L
    f = lax.broadcasted_iota(jnp.int32, (L, L), 0)
    t = lax.broadcasted_iota(jnp.int32, (L, L), 1)
    ang = ((f * t) % n).astype(jnp.float32) * (2.0 * math.pi / n)
    return jnp.cos(ang).astype(jnp.bfloat16), jnp.sin(ang).astype(jnp.bfloat16)


def _alt_sign(L):
    t = lax.broadcasted_iota(jnp.int32, (L, 1), 0)
    return (1 - 2 * (t & 1)).astype(jnp.float32)


def _spectrum_kernel(a_ref, b_ref, c_ref, s_ref, kr_ref, ks_ref, kn_ref):
    L = a_ref.shape[1]
    a = a_ref[0]
    row = lax.broadcasted_iota(jnp.int32, (L, 1), 0)
    scale = jnp.where(row == 0, 0.5 / L, 1.0 / L)
    kr_ref[0] = scale * jnp.dot(c_ref[...], a.astype(jnp.bfloat16), preferred_element_type=jnp.float32)
    ks_ref[0] = scale * jnp.dot(s_ref[...], b_ref[0].astype(jnp.bfloat16), preferred_element_type=jnp.float32)
    kn_ref[0] = jnp.sum(a * _alt_sign(L), axis=0, keepdims=True) * (0.5 / L)


def filter_spectrum(a, b, cos_t, sin_t, tc):
    n, L, C = a.shape
    blk = pl.BlockSpec((1, L, tc), lambda o, j: (o, 0, j))
    tab = pl.BlockSpec((L, L), lambda o, j: (0, 0))
    return pl.pallas_call(
        _spectrum_kernel,
        grid=(n, C // tc),
        in_specs=[blk, blk, tab, tab],
        out_specs=[blk, blk, pl.BlockSpec((1, 1, tc), lambda o, j: (o, 0, j))],
        out_shape=[jax.ShapeDtypeStruct((n, L, C), jnp.float32)] * 2 + [jax.ShapeDtypeStruct((n, 1, C), jnp.float32)],
        compiler_params=pltpu.CompilerParams(
            dimension_semantics=("arbitrary", "arbitrary"), vmem_limit_bytes=VMEM_LIMIT_BYTES),
    )(a, b, cos_t, sin_t)


def _phase_conv3(raw, w_ref, b_ref, rows_per_phase):
    n_ph = len(raw)
    h = raw[0].shape[0]
    j = lax.broadcasted_iota(jnp.int32, (h, 1), 0) % rows_per_phase
    prev0 = jnp.where(j != 0, pltpu.roll(raw[n_ph - 1], 1, 0), 0.0)
    next_last = jnp.where(j != rows_per_phase - 1, pltpu.roll(raw[0], h - 1, 0), 0.0)
    out = []
    for p in range(n_ph):
        prev = raw[p - 1] if p > 0 else prev0
        nxt = raw[p + 1] if p < n_ph - 1 else next_last
        out.append(b_ref[...] + w_ref[0:1, :] * prev + w_ref[1:2, :] * raw[p] + w_ref[2:3, :] * nxt)
    return out


def _long_conv_kernel(*refs, n_ph, n_slab, conv_z, rows_per_phase):
    z_refs = refs[:n_slab]
    xn_refs = refs[n_slab:2 * n_slab]
    (kr_ref, ks_ref, kn_ref, bias_ref, cwz_ref, cbz_ref, cwx_ref, cbx_ref, c_ref, s_ref,
     o_ref, acc_ref, zr_ref, zs_ref, yr_ref, ys_ref, stage_ref) = refs[2 * n_slab:]
    H = z_refs[0].shape[1] // n_ph
    f32 = jnp.float32
    bf = jnp.bfloat16
    sign = _alt_sign(H)

    def phases(slab_refs):
        return [jnp.concatenate([r[0, pl.ds(p, H, stride=n_ph), :] for r in slab_refs], axis=1)
                for p in range(n_ph)]

    z_ph = phases(z_refs)
    if conv_z:
        z_ph = _phase_conv3(z_ph, cwz_ref, cbz_ref, rows_per_phase)
    for q in range(n_ph):
        zb = z_ph[q].astype(bf)
        zr_ref[q] = jnp.dot(c_ref[...], zb, preferred_element_type=f32)
        zs_ref[q] = jnp.dot(s_ref[...], zb, preferred_element_type=f32)
    z_nyq = [jnp.sum(z * sign, axis=0, keepdims=True) for z in z_ph]
    for p in range(n_ph):
        nyq = sum(z_nyq[q] * kn_ref[p - q + n_ph - 1] for q in range(n_ph))
        acc_ref[p] = z_ph[p] * bias_ref[0] + sign * nyq
        yr = 0.0
        ys = 0.0
        for q in range(n_ph):
            slot = p - q + n_ph - 1
            yr = yr + zr_ref[q] * kr_ref[slot] - zs_ref[q] * ks_ref[slot]
            ys = ys + zr_ref[q] * ks_ref[slot] + zs_ref[q] * kr_ref[slot]
        yr_ref[p] = yr.astype(bf)
        ys_ref[p] = ys.astype(bf)
    for p in range(n_ph):
        acc_ref[p] += (jnp.dot(c_ref[...], yr_ref[p], preferred_element_type=f32)
                       + jnp.dot(s_ref[...], ys_ref[p], preferred_element_type=f32))
    x_ph = _phase_conv3(phases(xn_refs), cwx_ref, cbx_ref, rows_per_phase)
    for p in range(n_ph):
        out_p = x_ph[p] * acc_ref[p]
        for sl in range(n_slab):
            stage_ref[sl, pl.ds(p, H, stride=n_ph), :] = out_p[:, sl * LANES:(sl + 1) * LANES]
    for sl in range(n_slab):
        o_ref[0, :, sl * LANES:(sl + 1) * LANES] = stage_ref[sl].astype(o_ref.dtype)


def long_conv_gate(z_arr, z_col, conv_z, xn_arr, xn_col, conv_w, conv_b, kr, ks, kn, bias, cos_t, sin_t,
                   tc, n_ph, row_len, out_dtype):
    bsz, L, _ = z_arr.shape
    H = L // n_ph
    C = kr.shape[-1]
    nj = C // tc
    n_slab = tc // LANES
    n_f = 2 * n_ph - 1
    tab = pl.BlockSpec((H, H), lambda j, b: (0, 0), pipeline_mode=pl.Buffered(1))
    spec = pl.BlockSpec((n_f, H, tc), lambda j, b: (0, 0, j), pipeline_mode=pl.Buffered(1))
    nyq = pl.BlockSpec((n_f, 1, tc), lambda j, b: (0, 0, j))
    vec = pl.BlockSpec((1, 1, tc), lambda j, b: (0, 0, j))

    def slabs(col):
        return [pl.BlockSpec((1, L, LANES),
                             functools.partial(lambda j, b, sl: (b, 0, (col * nj + j) * n_slab + sl), sl=sl))
                for sl in range(n_slab)]

    def conv_specs(col):
        return [pl.BlockSpec((3, tc), lambda j, b: (0, col * nj + j)),
                pl.BlockSpec((1, tc), lambda j, b: (0, col * nj + j))]

    zc = z_col if conv_z else 0
    kern = functools.partial(_long_conv_kernel, n_ph=n_ph, n_slab=n_slab, conv_z=conv_z,
                             rows_per_phase=row_len // n_ph)
    return pl.pallas_call(
        kern,
        grid=(nj, bsz),
        in_specs=(slabs(z_col) + slabs(xn_col) + [spec, spec, nyq, vec] + conv_specs(zc) + conv_specs(xn_col)
                  + [tab, tab]),
        out_specs=pl.BlockSpec((1, L, tc), lambda j, b: (b, 0, j)),
        out_shape=jax.ShapeDtypeStruct((bsz, L, C), out_dtype),
        scratch_shapes=[pltpu.VMEM((n_ph, H, tc), jnp.float32), pltpu.VMEM((n_ph, H, tc), jnp.float32),
                        pltpu.VMEM((n_ph, H, tc), jnp.float32), pltpu.VMEM((n_ph, H, tc), jnp.bfloat16),
                        pltpu.VMEM((n_ph, H, tc), jnp.bfloat16), pltpu.VMEM((n_slab, L, LANES), jnp.float32)],
        compiler_params=pltpu.CompilerParams(
            dimension_semantics=("arbitrary", "arbitrary"), vmem_limit_bytes=VMEM_LIMIT_BYTES),
    )(*([z_arr] * n_slab), *([xn_arr] * n_slab), kr, ks, kn, bias, conv_w, conv_b, conv_w, conv_b, cos_t, sin_t)


def _polyphase_taps(kf, kb, n_ph):
    H = kf.shape[0] // n_ph
    ph = lambda a, p: a[p * H:(p + 1) * H]
    zero = jnp.zeros_like(kf[:1])
    plus, minus = [], []
    for r in range(-(n_ph - 1), n_ph):
        if r >= 0:
            plus.append(ph(kf, r))
        else:
            plus.append(jnp.concatenate([ph(kb, -r)[0:1], ph(kf, n_ph + r)[:-1]], axis=0))
        if r <= 0:
            minus.append(jnp.concatenate([zero, ph(kb, -r)[1:]], axis=0))
        else:
            minus.append(jnp.concatenate([zero, ph(kb, n_ph - r)[:-1]], axis=0))
    return jnp.stack(plus), jnp.stack(minus)


def hyena_long_convs(p_hy, conv_w, conv_b, kp, h_bias, tc, n_ph, row_len):
    L = p_hy.shape[1]
    C = h_bias.shape[1]
    cos_t, sin_t = dft_tables(L // n_ph)
    z = p_hy
    for o in range(h_bias.shape[0]):
        fwd = slice((2 * o) * C, (2 * o + 1) * C)
        bwd = slice((2 * o + 1) * C, (2 * o + 2) * C)
        plus, minus = _polyphase_taps(kp[:, fwd], kp[:, bwd], n_ph)
        kr, ks, kn = filter_spectrum(plus + minus, plus - minus, cos_t, sin_t, tc)
        last = o == h_bias.shape[0] - 1
        z = long_conv_gate(z, 0, o == 0, p_hy, o + 1, conv_w, conv_b, kr, ks, kn, h_bias[o][None, None, :],
                           cos_t, sin_t, tc, n_ph, row_len, jnp.bfloat16 if last else jnp.float32)
    return z


def _filter_kernel(band_ref, w1_ref, b1_ref, fr_ref, w2_ref, b2_ref, w3_ref, dl_ref, k_ref, *, seq_len, n_ph):
    hp = lax.Precision.HIGHEST
    f32 = jnp.float32
    tp = k_ref.shape[0]
    per_phase = seq_len // n_ph
    g = lax.broadcasted_iota(jnp.int32, (tp, 1), 0) + pl.program_id(0) * tp
    phase = g // per_phase
    pos = (n_ph * (g - phase * per_phase) + phase).astype(f32)
    t = pos / max(seq_len - 1, 1)
    ang = (2 * math.pi / seq_len) * pos * band_ref[...]
    lane = lax.broadcasted_iota(jnp.int32, (tp, LANES), 1)
    feats = jnp.where(lane == 0, t,
                      jnp.where(lane <= HYENA_BANDS, jnp.cos(ang),
                                jnp.where(lane <= 2 * HYENA_BANDS, -jnp.sin(ang), 0.0)))
    h = jnp.sin(fr_ref[...] * (jnp.dot(feats, w1_ref[...], precision=hp, preferred_element_type=f32) + b1_ref[...]))
    h = jnp.sin(fr_ref[...] * (jnp.dot(h, w2_ref[...], precision=hp, preferred_element_type=f32) + b2_ref[...]))
    window = jnp.exp(-t * dl_ref[...])
    c = dl_ref.shape[1]
    for j in range(w3_ref.shape[1] // c):
        cols = slice(j * c, (j + 1) * c)
        k_ref[:, cols] = jnp.dot(h, w3_ref[:, cols], precision=hp, preferred_element_type=f32) * window


def hyena_filters_polyphase(seq_len, f_w1, f_b1, f_freq, f_w2, f_b2, f_w3, d_hyena, tp, n_ph):
    f32 = jnp.float32
    fh = f_w1.shape[1]
    n_emb = 1 + 2 * HYENA_BANDS
    bands = jnp.linspace(1e-4, HYENA_BANDS - 1, HYENA_BANDS, dtype=f32)
    band_row = jnp.concatenate([jnp.zeros((1,), f32), bands, bands, jnp.zeros((LANES - n_emb,), f32)])[None, :]
    w1p = jnp.concatenate([f_w1, jnp.zeros((LANES - n_emb, fh), f32)], axis=0)
    deltas = jnp.abs(jnp.linspace(math.log(HYENA_TARGET) / HYENA_SLOW_DECAY,
                                  math.log(HYENA_TARGET) / HYENA_FAST_DECAY, d_hyena, dtype=f32))[None, :]
    n_out = f_w3.shape[1]
    full = lambda a: pl.BlockSpec(a.shape, lambda i: (0,) * a.ndim)
    args = (band_row, w1p, f_b1[None, :], f_freq[None, :], f_w2, f_b2[None, :], f_w3, deltas)
    return pl.pallas_call(
        functools.partial(_filter_kernel, seq_len=seq_len, n_ph=n_ph),
        grid=(seq_len // tp,),
        in_specs=[full(a) for a in args],
        out_specs=pl.BlockSpec((tp, n_out), lambda i: (i, 0)),
        out_shape=jax.ShapeDtypeStruct((seq_len, n_out), f32),
        compiler_params=pltpu.CompilerParams(dimension_semantics=("arbitrary",), vmem_limit_bytes=VMEM_LIMIT_BYTES),
    )(*args)


def _ada_kernel(c_ref, w_ref, b_ref, o_ref):
    cv = c_ref[...]
    s = cv * jax.nn.sigmoid(cv)
    o_ref[...] = jnp.dot(s, w_ref[...], precision=lax.Precision.HIGHEST,
                         preferred_element_type=jnp.float32) + b_ref[...]


def ada_modulation(c_rows, w_ada, b_ada, tn):
    rows, D = c_rows.shape
    N = w_ada.shape[1]
    return pl.pallas_call(
        _ada_kernel,
        grid=(N // tn,),
        in_specs=[pl.BlockSpec((rows, D), lambda j: (0, 0)),
                  pl.BlockSpec((D, tn), lambda j: (0, j)),
                  pl.BlockSpec((1, tn), lambda j: (0, j))],
        out_specs=pl.BlockSpec((rows, tn), lambda j: (0, j)),
        out_shape=jax.ShapeDtypeStruct((rows, N), jnp.float32),
        compiler_params=pltpu.CompilerParams(dimension_semantics=("arbitrary",), vmem_limit_bytes=VMEM_LIMIT_BYTES),
    )(c_rows, w_ada, b_ada[None, :])


def _ssd_kernel(xf_ref, df_ref, xb_ref, db_ref, dskip_ref, y_ref, h_ref, *, n_ctx_chunks):
    s = pl.program_id(1)
    n_steps = pl.num_programs(1)
    Q, G, R, P, N = SSD_CHUNK, SSD_GROUPS, SSD_HPG, SSD_HEAD_DIM, SSD_STATE
    GP = R * P
    bf = jnp.bfloat16

    @pl.when(s == 0)
    def _():
        h_ref[...] = jnp.zeros_like(h_ref)
        y_ref[...] = jnp.zeros_like(y_ref)

    row = lax.broadcasted_iota(jnp.int32, (Q, Q), 0)
    col = lax.broadcasted_iota(jnp.int32, (Q, Q), 1)
    lane_head = lax.broadcasted_iota(jnp.int32, (Q, GP), 1) // P
    is_latent = s >= n_ctx_chunks
    n_lat = n_steps - n_ctx_chunks
    out_chunk = (jnp.clip(s - n_ctx_chunks, 0, n_lat - 1), jnp.clip(n_steps - 1 - s, 0, n_lat - 1))

    for bb, d in [(bb, d) for bb in range(y_ref.shape[0]) for d in range(2)]:
        x_ref, da_ref = ((xf_ref, df_ref), (xb_ref, db_ref))[d]
        mask = (row >= col) if d == 0 else (col >= row)
        tri = mask.astype(jnp.float32)
        da = da_ref[bb]
        cum = jnp.dot(tri, da, precision=lax.Precision.HIGHEST, preferred_element_type=jnp.float32)
        cum_t = cum.T
        edge = Q - 1 if d == 0 else 0
        blk = x_ref.at[bb]
        for g in range(G):
            xg = blk[:, g * GP:(g + 1) * GP]
            bg = blk[:, D_SSD + g * N:D_SSD + (g + 1) * N].astype(bf)
            cg = blk[:, D_SSD + G * N + g * N:D_SSD + G * N + (g + 1) * N].astype(bf)
            heads = [d * SSD_HEADS + g * R + r for r in range(R)]
            dtm = jnp.zeros((Q, GP), jnp.float32)
            cumm = jnp.zeros((Q, GP), jnp.float32)
            for r, h in enumerate(heads):
                sel = lane_head == r
                dtm = jnp.where(sel, da[:, h:h + 1], dtm)
                cumm = jnp.where(sel, cum[:, SSD_HEADS * 2 + h:SSD_HEADS * 2 + h + 1], cumm)
            totm = cumm[edge:edge + 1, :]
            xdt = xg * dtm
            hg = h_ref[bb, d, g * GP:(g + 1) * GP, :]

            gmat = lax.dot_general(cg, bg, (((1,), (1,)), ((), ())), preferred_element_type=jnp.float32)
            y_off = lax.dot_general(cg, hg.astype(bf), (((1,), (1,)), ((), ())),
                                    preferred_element_type=jnp.float32) * jnp.exp(cumm)
            if d == 0:
                y_off = y_off + dskip_ref[:, g * GP:(g + 1) * GP] * xg
            parts = []
            for r, h in enumerate(heads):
                a_col = cum[:, SSD_HEADS * 2 + h:SSD_HEADS * 2 + h + 1]
                a_row = cum_t[SSD_HEADS * 2 + h:SSD_HEADS * 2 + h + 1, :]
                decay = jnp.exp(jnp.where(mask, a_col - a_row, NEG_BIG))
                parts.append(jnp.dot((gmat * decay).astype(bf), xdt[:, r * P:(r + 1) * P].astype(bf),
                                     preferred_element_type=jnp.float32))
            y = jnp.where(is_latent, y_off + jnp.concatenate(parts, axis=-1), 0.0)
            rows = pl.ds(pl.multiple_of(out_chunk[d] * Q, Q), Q)
            y_ref[bb, rows, g * GP:(g + 1) * GP] += y

            xw = (xdt * jnp.exp(totm - cumm)).astype(bf)
            st = lax.dot_general(xw, bg, (((0,), (0,)), ((), ())), preferred_element_type=jnp.float32)
            for r, h in enumerate(heads):
                dec = jnp.exp(cum_t[SSD_HEADS * 2 + h:SSD_HEADS * 2 + h + 1, edge:edge + 1])
                rs = slice(g * GP + r * P, g * GP + (r + 1) * P)
                h_ref[bb, d, rs, :] = h_ref[bb, d, rs, :] * dec + st[r * P:(r + 1) * P, :]


def ssd_scan_bidir(xbc, dta, d_skip, n_ctx, lat_off, nb):
    bsz, lt, width = xbc.shape
    Q = SSD_CHUNK
    n_ctx_chunks = n_ctx // Q
    L = lt - lat_off
    n_lat = L // Q
    n_steps = n_ctx_chunks + n_lat
    lat0 = lat_off // Q

    def fwd_chunk(s):
        return jnp.where(s < n_ctx_chunks, s, s - n_ctx_chunks + lat0)

    def bwd_chunk(s):
        return jnp.where(s < n_ctx_chunks, n_ctx_chunks - 1 - s, n_steps - 1 - s + lat0)

    return pl.pallas_call(
        functools.partial(_ssd_kernel, n_ctx_chunks=n_ctx_chunks),
        grid=(bsz // nb, n_steps),
        in_specs=[
            pl.BlockSpec((nb, Q, width), lambda b, s: (b, fwd_chunk(s), 0)),
            pl.BlockSpec((nb, Q, LANES), lambda b, s: (b, fwd_chunk(s), 0)),
            pl.BlockSpec((nb, Q, width), lambda b, s: (b, bwd_chunk(s), 0)),
            pl.BlockSpec((nb, Q, LANES), lambda b, s: (b, bwd_chunk(s), 0)),
            pl.BlockSpec((1, D_SSD), lambda b, s: (0, 0)),
        ],
        out_specs=pl.BlockSpec((nb, L, D_SSD), lambda b, s: (b, 0, 0)),
        out_shape=jax.ShapeDtypeStruct((bsz, L, D_SSD), jnp.float32),
        scratch_shapes=[pltpu.VMEM((nb, 2, SSD_GROUPS * SSD_HPG * SSD_HEAD_DIM, SSD_STATE), jnp.float32)],
        compiler_params=pltpu.CompilerParams(
            dimension_semantics=("arbitrary", "arbitrary"), vmem_limit_bytes=VMEM_LIMIT_BYTES),
    )(xbc, dta, xbc, dta, d_skip)


def kernel(x, c, ctx, c_ctx, w_ada, b_ada, g_norm1, g_norm2, w_in, hy_conv_w, hy_conv_b, hy_f_w1, hy_f_b1, hy_f_freq, hy_f_w2, hy_f_b2, hy_f_w3, hy_bias, ssd_conv_w, ssd_conv_b, ssd_a_log, ssd_dt_bias, ssd_d, ssd_norm_g, w_out, w_group, b_group, w_expert, b_expert, w1, w3, w2, g_final):
    bsz, seq_len, _ = x.shape
    l = 0
    rows_pad = -(bsz + 1) % SUBLANES
    c_rows = jnp.concatenate([c, c_ctx[None, :], jnp.zeros((rows_pad, D_MODEL), jnp.float32)], axis=0)
    mod_all = ada_modulation(c_rows, w_ada[l], b_ada[l], 512)
    sh1, sc1, ga1, sh2, sc2, ga2 = jnp.split(mod_all[:bsz, None, :], 6, axis=-1)
    csh1, csc1 = mod_all[bsz, :D_MODEL], mod_all[bsz, D_MODEL:2 * D_MODEL]

    w_out_bf = w_out[l].astype(jnp.bfloat16)
    w_in_bf, w_dt_bf, dt_bias2, dt_mult = in_proj_params(w_in[l], ssd_a_log[l], ssd_dt_bias[l],
                                                         HY_COLS, D_SSD, D_XBC)

    ctx_len = ctx.shape[1]
    lat_off = -(-ctx_len // IN_PROJ_ROWS) * IN_PROJ_ROWS
    p_hy, z, xbc, dta = in_proj_fused(ctx, x, g_norm1[l][None, :], csh1[None, :], csc1[None, :], sh1, sc1,
                                      w_in_bf, w_dt_bf, HY_COLS, ssd_conv_w[l], ssd_conv_b[l][None, :],
                                      dt_bias2, dt_mult, GRID_W, ctx_len, IN_PROJ_ROWS, 512)
    kp = hyena_filters_polyphase(seq_len, hy_f_w1[l], hy_f_b1[l], hy_f_freq[l], hy_f_w2[l], hy_f_b2[l],
                                 hy_f_w3[l], D_HYENA, 256, HYENA_PHASES)
    y_hy = hyena_long_convs(p_hy, hy_conv_w[l], hy_conv_b[l][None, :], kp, hy_bias[l], 256, HYENA_PHASES, GRID_W)
    y_scan = ssd_scan_bidir(xbc, dta, jnp.repeat(ssd_d[l], SSD_HEAD_DIM)[None, :], ctx_len, lat_off, 2)

    pad = LANES - MOE_GROUPS - N_EXPERTS
    w_router = jnp.concatenate([w_group[l], w_expert[l], jnp.zeros((D_MODEL, pad), jnp.float32)], axis=1)
    w_router_hi = w_router.astype(jnp.bfloat16)
    w_router_lo = (w_router - w_router_hi.astype(jnp.float32)).astype(jnp.bfloat16)
    w_router = jnp.stack([w_router_hi, w_router_lo])
    b_router = jnp.concatenate([b_group[l], b_expert[l], jnp.zeros((pad,), jnp.float32)])[None, :]
    x1, hn, route_i, route_f, counts = out_proj_router(
        y_hy, y_scan, z, 0, x, ga1, sc2, sh2, ssd_norm_g[l][None, :], g_norm2[l][None, :],
        w_out_bf, w_router, b_router, 256)
    n_tok = bsz * seq_len
    n_blocks = -(-n_tok * TOP_K // MOE_BLOCK) + N_EXPERTS
    dest, block_eid, block_first, block_valid = moe_plan(route_i, counts, MOE_BLOCK, n_blocks)
    buf = moe_dispatch(hn.reshape(n_tok, D_MODEL), dest, block_valid, MOE_BLOCK, 256)
    yb = expert_blocks(buf, block_eid, block_first, block_valid, w1[l], w3[l], w2[l], MOE_BLOCK)
    return moe_combine(x1, route_f, ga2, g_final[None, :], yb, dest, 256)
```

```python
import functools
import math

import jax
import jax.numpy as jnp
from jax import lax
from jax.experimental import pallas as pl
from jax.experimental.pallas import tpu as pltpu

D_MODEL = 1024
CTX_LEN = 256
GRID_W = 64
EPS = 1e-6
SHORT_CONV = 3

D_HYENA = D_MODEL // 2
HYENA_ORDER = 2
HYENA_BANDS = 8
HYENA_FAST_DECAY = 0.3
HYENA_SLOW_DECAY = 1.5
HYENA_TARGET = 1e-2
HYENA_PHASES = 4

D_SSD = D_MODEL // 2
SSD_HEAD_DIM = 64
SSD_HEADS = D_SSD // SSD_HEAD_DIM
SSD_GROUPS = 2
SSD_HPG = SSD_HEADS // SSD_GROUPS
SSD_STATE = 128
SSD_CHUNK = 128

D_XBC = D_SSD + 2 * SSD_GROUPS * SSD_STATE
HY_COLS = (HYENA_ORDER + 1) * D_HYENA
D_IN = HY_COLS + D_SSD + D_XBC + 2 * SSD_HEADS
LANES = 128
SUBLANES = 8
D_IN_PAD = -(-D_IN // LANES) * LANES

MOE_GROUPS = 8
EXPERTS_PER_GROUP = 8
N_EXPERTS = MOE_GROUPS * EXPERTS_PER_GROUP
TOP_K = 2
D_EXPERT = 512
MOE_BLOCK = 256
ROUTE_COLS = 8

IN_PROJ_ROWS = 512
IN_PROJ_COLS = 512
TOKEN_ROWS = 256
HYENA_COLS = 256
FILTER_ROWS = 256
ADA_COLS = 512
SSD_BATCH = 2

VMEM_LIMIT_BYTES = 56 * 1024 * 1024
NEG_BIG = -1e30


def _conv3_rows(p, w_ref, b_ref, cols, has_prev, has_next):
    n = p.shape[0]
    prev = jnp.where(has_prev, pltpu.roll(p, 1, 0), 0.0)
    nxt = jnp.where(has_next, pltpu.roll(p, n - 1, 0), 0.0)
    return b_ref[:, cols] + w_ref[0:1, cols] * prev + w_ref[1:2, cols] * p + w_ref[2:3, cols] * nxt


def _in_proj_kernel(ctx_ref, x_ref, g_ref, csh_ref, csc_ref, sh_ref, sc_ref, w_ref, wdt_ref,
                    sw_ref, sb_ref, dtb_ref, dtm_ref, u_ref, z_ref, xbc_ref, dta_ref, h_ref,
                    *, n_ctx_steps, row_len, ctx_row_len, hy_cols, d_ssd, d_xbc, tn):
    i = pl.program_id(1)
    is_ctx = i < n_ctx_steps
    tm = x_ref.shape[1]
    ctx_t = ctx_ref[0]
    if ctx_t.shape[0] < tm:
        ctx_t = jnp.concatenate([ctx_t, jnp.zeros((tm - ctx_t.shape[0], ctx_t.shape[1]), ctx_t.dtype)], axis=0)
    xin = jnp.where(is_ctx, ctx_t, x_ref[0])
    shift = jnp.where(is_ctx, csh_ref[...], sh_ref[0])
    scale = jnp.where(is_ctx, csc_ref[...], sc_ref[0])
    y = xin * lax.rsqrt(jnp.mean(xin * xin, axis=-1, keepdims=True) + EPS) * g_ref[...]
    h_ref[...] = (y * (1.0 + scale) + shift).astype(jnp.bfloat16)

    pos = lax.broadcasted_iota(jnp.int32, (tm, 1), 0) + jnp.where(is_ctx, i, i - n_ctx_steps) * tm
    in_row = jnp.where(is_ctx, pos % ctx_row_len, pos % row_len)
    has_prev = in_row != 0
    has_next = in_row != jnp.where(is_ctx, ctx_row_len - 1, row_len - 1)

    for c0 in range(0, hy_cols, tn):
        cols = slice(c0, c0 + tn)
        u_ref[0, :, cols] = jnp.dot(h_ref[...], w_ref[:, cols], preferred_element_type=jnp.float32)
    z_ref[0] = jnp.dot(h_ref[...], w_ref[:, hy_cols:hy_cols + d_ssd], preferred_element_type=jnp.float32)

    for c0 in range(0, d_xbc, tn):
        cols = slice(c0, c0 + tn)
        wc = slice(hy_cols + d_ssd + c0, hy_cols + d_ssd + c0 + tn)
        p = jnp.dot(h_ref[...], w_ref[:, wc], preferred_element_type=jnp.float32)
        v = _conv3_rows(p, sw_ref, sb_ref, cols, has_prev, has_next)
        xbc_ref[0, :, cols] = v * jax.nn.sigmoid(v)
    pd = jnp.dot(h_ref[...], wdt_ref[...], preferred_element_type=jnp.float32) + dtb_ref[...]
    sp = jnp.maximum(pd, 0.0) + jnp.log(1.0 + jnp.exp(-jnp.abs(pd)))
    dta_ref[0] = sp * dtm_ref[...]


def in_proj_fused(ctx, x, g1, csh, csc, sh, sc, w_bf, wdt_bf, hy_cols, ssd_w, ssd_b, dt_bias2, dt_mult,
                  row_len, ctx_row_len, tm, tn):
    bsz, L, D = x.shape
    lc = ctx.shape[1]
    d_xbc = ssd_w.shape[1]
    d_ssd = w_bf.shape[1] - hy_cols - d_xbc
    n_ctx_steps = -(-lc // tm)
    ctx_rows = min(lc, tm)
    lc = n_ctx_steps * tm
    n_steps = n_ctx_steps + L // tm
    lat = lambda b, i: (b, jnp.maximum(i - n_ctx_steps, 0), 0)
    allt = lambda b, i: (b, i, 0)
    const2 = lambda b, i: (0, 0)
    per_b = pl.BlockSpec((1, 1, D), lambda b, i: (b, 0, 0))
    kern = functools.partial(_in_proj_kernel, n_ctx_steps=n_ctx_steps, row_len=row_len, ctx_row_len=ctx_row_len,
                             hy_cols=hy_cols, d_ssd=d_ssd, d_xbc=d_xbc, tn=tn)
    return pl.pallas_call(
        kern,
        grid=(bsz, n_steps),
        in_specs=[
            pl.BlockSpec((1, ctx_rows, D), lambda b, i: (b, jnp.minimum(i, n_ctx_steps - 1), 0)),
            pl.BlockSpec((1, tm, D), lat),
            pl.BlockSpec((1, D), const2),
            pl.BlockSpec((1, D), const2),
            pl.BlockSpec((1, D), const2),
            per_b, per_b,
            pl.BlockSpec(w_bf.shape, const2),
            pl.BlockSpec(wdt_bf.shape, const2),
            pl.BlockSpec(ssd_w.shape, const2),
            pl.BlockSpec(ssd_b.shape, const2),
            pl.BlockSpec((1, LANES), const2),
            pl.BlockSpec((1, LANES), const2),
        ],
        out_specs=[
            pl.BlockSpec((1, tm, hy_cols), lat),
            pl.BlockSpec((1, tm, d_ssd), lat),
            pl.BlockSpec((1, tm, d_xbc), allt),
            pl.BlockSpec((1, tm, LANES), allt),
        ],
        out_shape=[
            jax.ShapeDtypeStruct((bsz, L, hy_cols), jnp.float32),
            jax.ShapeDtypeStruct((bsz, L, d_ssd), jnp.float32),
            jax.ShapeDtypeStruct((bsz, lc + L, d_xbc), jnp.float32),
            jax.ShapeDtypeStruct((bsz, lc + L, LANES), jnp.float32),
        ],
        scratch_shapes=[pltpu.VMEM((tm, D), jnp.bfloat16)],
        compiler_params=pltpu.CompilerParams(
            dimension_semantics=("arbitrary", "arbitrary"), vmem_limit_bytes=VMEM_LIMIT_BYTES),
    )(ctx, x, g1, csh, csc, sh, sc, w_bf, wdt_bf, ssd_w, ssd_b, dt_bias2, dt_mult)


def in_proj_params(w_in, a_log, dt_bias, hy_cols, d_ssd, d_xbc):
    n_h = 2 * SSD_HEADS
    main = hy_cols + d_ssd + d_xbc
    w_dt = w_in[:, main:main + n_h]
    pad = jnp.zeros((w_in.shape[0], LANES - 2 * n_h), w_in.dtype)
    wdt = jnp.concatenate([w_dt, w_dt, pad], axis=1).astype(jnp.bfloat16)
    zpad = jnp.zeros((LANES - 2 * n_h,), jnp.float32)
    bias2 = jnp.concatenate([dt_bias.reshape(n_h), dt_bias.reshape(n_h), zpad])[None, :]
    mult = jnp.concatenate([jnp.ones((n_h,), jnp.float32), -jnp.exp(a_log).reshape(n_h), zpad])[None, :]
    return w_in[:, :main].astype(jnp.bfloat16), wdt, bias2, mult


def _out_router_kernel(yh_ref, ys_ref, z_ref, x_ref, ga_ref, sc_ref, sh_ref, ng_ref, g2_ref, wo_ref, wr_ref, br_ref,
                       x1_ref, hn_ref, ri_ref, rf_ref, cnt_ref, carry_ref):
    first = jnp.logical_and(pl.program_id(0) == 0, pl.program_id(1) == 0)

    @pl.when(first)
    def _():
        carry_ref[...] = jnp.zeros_like(carry_ref)

    bf = jnp.bfloat16
    tm = x_ref.shape[1]
    dh = yh_ref.shape[2]
    z = z_ref[0]
    ys = ys_ref[0] * (z * jax.nn.sigmoid(z))
    gw = ys.shape[1] // SSD_GROUPS
    acc = jnp.dot(yh_ref[0].astype(bf), wo_ref[0:dh, :], preferred_element_type=jnp.float32)
    for g in range(SSD_GROUPS):
        yg = ys[:, g * gw:(g + 1) * gw]
        yg = yg * lax.rsqrt(jnp.mean(yg * yg, axis=-1, keepdims=True) + EPS) * ng_ref[:, g * gw:(g + 1) * gw]
        acc += jnp.dot(yg.astype(bf), wo_ref[dh + g * gw:dh + (g + 1) * gw, :], preferred_element_type=jnp.float32)
    x1 = x_ref[0] + ga_ref[0] * acc
    x1_ref[0] = x1
    hn = x1 * lax.rsqrt(jnp.mean(x1 * x1, axis=-1, keepdims=True) + EPS) * g2_ref[...]
    hn = hn * (1.0 + sc_ref[0]) + sh_ref[0]
    hn_ref[0] = hn

    hn_hi = hn.astype(bf)
    hn_lo = (hn - hn_hi.astype(jnp.float32)).astype(bf)
    logits = (jnp.dot(hn_hi, wr_ref[0], preferred_element_type=jnp.float32)
              + jnp.dot(hn_lo, wr_ref[0], preferred_element_type=jnp.float32)
              + jnp.dot(hn_hi, wr_ref[1], preferred_element_type=jnp.float32)) + br_ref[...]
    lane = lax.broadcasted_iota(jnp.int32, (tm, LANES), 1)
    lane_f = lane.astype(jnp.float32)
    ninf = jnp.float32(-jnp.inf)
    big = jnp.float32(1e9)
    gl = jnp.where(lane < MOE_GROUPS, logits, ninf)
    gmax = jnp.max(gl, axis=-1, keepdims=True)
    p_group = 1.0 / jnp.sum(jnp.exp(gl - gmax), axis=-1, keepdims=True)
    g_sel = jnp.min(jnp.where(gl == gmax, lane_f, big), axis=-1, keepdims=True)
    e_lane = lane - MOE_GROUPS
    in_grp = jnp.logical_and(e_lane >= 0, (e_lane // EXPERTS_PER_GROUP).astype(jnp.float32) == g_sel)
    el = jnp.where(in_grp, logits, ninf)
    m1 = jnp.max(el, axis=-1, keepdims=True)
    i1 = jnp.min(jnp.where(el == m1, lane_f, big), axis=-1, keepdims=True)
    el2 = jnp.where(lane_f == i1, ninf, el)
    m2 = jnp.max(el2, axis=-1, keepdims=True)
    i2 = jnp.min(jnp.where(el2 == m2, lane_f, big), axis=-1, keepdims=True)
    t = jnp.exp(m2 - m1)
    w1 = 1.0 / (1.0 + t)
    gate1 = w1 * p_group
    gate2 = (t * w1) * p_group
    e1 = i1 - MOE_GROUPS
    e2 = i2 - MOE_GROUPS
    el_f = e_lane.astype(jnp.float32)
    oh1 = el_f == e1
    oh2 = el_f == e2
    oh = jnp.logical_or(oh1, oh2).astype(bf)
    r_i = lax.broadcasted_iota(jnp.int32, (tm, tm), 0)
    c_i = lax.broadcasted_iota(jnp.int32, (tm, tm), 1)
    before = jnp.dot((c_i < r_i).astype(bf), oh, preferred_element_type=jnp.float32) + carry_ref[...]
    rank1 = jnp.sum(jnp.where(oh1, before, 0.0), axis=-1, keepdims=True)
    rank2 = jnp.sum(jnp.where(oh2, before, 0.0), axis=-1, keepdims=True)
    carry_ref[...] += jnp.sum(oh.astype(jnp.float32), axis=0, keepdims=True)
    cnt_ref[...] = carry_ref[...]

    rec = jnp.where(lane == 0, e1, jnp.where(lane == 1, e2, jnp.where(lane == 2, rank1,
                                                                      jnp.where(lane == 3, rank2, 0.0))))
    ri_ref[0] = rec.T[0:ROUTE_COLS, :].astype(jnp.int32)
    col = lax.broadcasted_iota(jnp.int32, (tm, ROUTE_COLS), 1)
    rf_ref[0] = jnp.where(col == 0, gate1, gate2)


def out_proj_router(y_hy, y_scan, px, z_col, x, ga1, sc2, sh2, norm_g, g2, w_out_bf, w_router, b_router, tm):
    bsz, L, D = x.shape
    dh = y_hy.shape[-1]
    ds = y_scan.shape[-1]
    tok = lambda b, i: (b, i, 0)
    per_b = pl.BlockSpec((1, 1, D), lambda b, i: (b, 0, 0))
    const2 = lambda b, i: (0, 0)
    return pl.pallas_call(
        _out_router_kernel,
        grid=(bsz, L // tm),
        in_specs=[
            pl.BlockSpec((1, tm, dh), tok),
            pl.BlockSpec((1, tm, ds), tok),
            pl.BlockSpec((1, tm, ds), lambda b, i: (b, i, z_col)),
            pl.BlockSpec((1, tm, D), tok),
            per_b, per_b, per_b,
            pl.BlockSpec((1, ds), const2),
            pl.BlockSpec((1, D), const2),
            pl.BlockSpec((dh + ds, D), const2),
            pl.BlockSpec((2, D, LANES), lambda b, i: (0, 0, 0)),
            pl.BlockSpec((1, LANES), const2),
        ],
        out_specs=[
            pl.BlockSpec((1, tm, D), tok),
            pl.BlockSpec((1, tm, D), tok),
            pl.BlockSpec((1, ROUTE_COLS, tm), lambda b, i: (b, 0, i)),
            pl.BlockSpec((1, tm, ROUTE_COLS), tok),
            pl.BlockSpec((1, LANES), const2),
        ],
        out_shape=[
            jax.ShapeDtypeStruct((bsz, L, D), jnp.float32),
            jax.ShapeDtypeStruct((bsz, L, D), jnp.float32),
            jax.ShapeDtypeStruct((bsz, ROUTE_COLS, L), jnp.int32),
            jax.ShapeDtypeStruct((bsz, L, ROUTE_COLS), jnp.float32),
            jax.ShapeDtypeStruct((1, LANES), jnp.float32),
        ],
        scratch_shapes=[pltpu.VMEM((1, LANES), jnp.float32)],
        compiler_params=pltpu.CompilerParams(
            dimension_semantics=("arbitrary", "arbitrary"), vmem_limit_bytes=VMEM_LIMIT_BYTES),
    )(y_hy, y_scan, px, x, ga1, sc2, sh2, norm_g, g2, w_out_bf, w_router, b_router)


def _row_copy(src_hbm, src_row, dst_ref, dst_row, sem):
    return pltpu.make_async_copy(src_hbm.at[pl.ds(src_row, 1), :], dst_ref.at[pl.ds(dst_row, 1), :], sem)


def _dispatch_kernel(dest_ref, valid_ref, hn_ref, buf_hbm, zeros, sem, zsem):
    step = pl.program_id(0)
    tm = hn_ref.shape[0]
    n_tok = pl.num_programs(0) * tm
    blk = zeros.shape[0]
    n_blocks = buf_hbm.shape[0] // blk

    def zero_copy(i):
        return pltpu.make_async_copy(zeros, buf_hbm.at[pl.ds(pl.multiple_of(i * blk, blk), blk), :], zsem)

    def zfill(i, carry):
        @pl.when(valid_ref[i] < blk)
        def _():
            zero_copy(i).start()
        return carry

    def zwait(i, carry):
        @pl.when(valid_ref[i] < blk)
        def _():
            zero_copy(i).wait()
        return carry

    @pl.when(step == 0)
    def _():
        zeros[...] = jnp.zeros_like(zeros)
        lax.fori_loop(0, n_blocks, zfill, 0)
        lax.fori_loop(0, n_blocks, zwait, 0)

    def body(j, carry):
        t = step * tm + j
        _row_copy(hn_ref, j, buf_hbm, dest_ref[t], sem).start()
        _row_copy(hn_ref, j, buf_hbm, dest_ref[n_tok + t], sem).start()
        return carry

    lax.fori_loop(0, tm, body, 0, unroll=8)
    for _ in range(2):
        pltpu.make_async_copy(hn_ref, buf_hbm.at[pl.ds(0, tm), :], sem).wait()


def moe_dispatch(hn, dest, block_valid, blk, tm):
    T, D = hn.shape
    n_rows = block_valid.shape[0] * blk
    grid_spec = pltpu.PrefetchScalarGridSpec(
        num_scalar_prefetch=2,
        grid=(T // tm,),
        in_specs=[pl.BlockSpec((tm, D), lambda i, d, v: (i, 0))],
        out_specs=pl.BlockSpec(memory_space=pl.ANY),
        scratch_shapes=[pltpu.VMEM((blk, D), hn.dtype), pltpu.SemaphoreType.DMA(()),
                        pltpu.SemaphoreType.DMA(())],
    )
    return pl.pallas_call(
        _dispatch_kernel,
        grid_spec=grid_spec,
        out_shape=jax.ShapeDtypeStruct((n_rows, D), hn.dtype),
        compiler_params=pltpu.CompilerParams(dimension_semantics=("arbitrary",), has_side_effects=True),
    )(dest, block_valid, hn)


def _expert_kernel(wsel_ref, first_ref, valid_ref, src_ref, x_ref, w1_ref, w3_ref, w2_ref, o_ref, w1b, w3b, w2b):
    i = pl.program_id(0)
    del wsel_ref, src_ref
    bf = jnp.bfloat16

    @pl.when(first_ref[i] == 1)
    def _():
        w1b[...] = w1_ref[0].astype(bf)
        w3b[...] = w3_ref[0].astype(bf)
        w2b[...] = w2_ref[0].astype(bf)

    valid = valid_ref[i]

    @pl.when(valid > 0)
    def _():
        xb = x_ref[...].astype(bf)
        a = jnp.dot(xb, w1b[...], preferred_element_type=jnp.float32)
        b = jnp.dot(xb, w3b[...], preferred_element_type=jnp.float32)
        h = (a * jax.nn.sigmoid(a)) * b
        o_ref[...] = jnp.dot(h.astype(bf), w2b[...], preferred_element_type=jnp.float32)

    @pl.when(valid <= 0)
    def _():
        o_ref[...] = jnp.zeros_like(o_ref)


def expert_blocks(buf, block_wsel, block_first, block_valid, block_src, w1, w3, w2, blk):
    rows, D = buf.shape
    n_blocks = rows // blk
    E, _, F = w1.shape
    grid_spec = pltpu.PrefetchScalarGridSpec(
        num_scalar_prefetch=4,
        grid=(n_blocks,),
        in_specs=[
            pl.BlockSpec((blk, D), lambda i, ws, fi, va, src: (src[i], 0)),
            pl.BlockSpec((1, D, F), lambda i, ws, fi, va, src: (ws[i], 0, 0)),
            pl.BlockSpec((1, D, F), lambda i, ws, fi, va, src: (ws[i], 0, 0)),
            pl.BlockSpec((1, F, D), lambda i, ws, fi, va, src: (ws[i], 0, 0)),
        ],
        out_specs=pl.BlockSpec((blk, D), lambda i, ws, fi, va, src: (i, 0)),
        scratch_shapes=[pltpu.VMEM((D, F), jnp.bfloat16), pltpu.VMEM((D, F), jnp.bfloat16),
                        pltpu.VMEM((F, D), jnp.bfloat16)],
    )
    return pl.pallas_call(
        _expert_kernel,
        grid_spec=grid_spec,
        out_shape=jax.ShapeDtypeStruct((rows, D), jnp.float32),
        compiler_params=pltpu.CompilerParams(
            dimension_semantics=("arbitrary",), vmem_limit_bytes=VMEM_LIMIT_BYTES),
    )(block_wsel, block_first, block_valid, block_src, buf, w1, w3, w2)


def _combine_kernel(dest_ref, x1_ref, rf_ref, ga_ref, gf_ref, yb_hbm, o_ref, ybuf, sem):
    b = pl.program_id(0)
    i = pl.program_id(1)
    n_i = pl.num_programs(1)
    tm = x1_ref.shape[1]
    step = b * n_i + i
    n_steps = pl.num_programs(0) * n_i
    slot = step % 2

    def issue(step_, slot_):
        def body(j, carry):
            t = step_ * tm + j
            _row_copy(yb_hbm, dest_ref[t], ybuf.at[slot_, 0], j, sem.at[slot_]).start()
            _row_copy(yb_hbm, dest_ref[n_steps * tm + t], ybuf.at[slot_, 1], j, sem.at[slot_]).start()
            return carry
        lax.fori_loop(0, tm, body, 0, unroll=8)

    @pl.when(step == 0)
    def _():
        issue(0, 0)

    @pl.when(step + 1 < n_steps)
    def _():
        issue(step + 1, 1 - slot)

    pltpu.make_async_copy(yb_hbm.at[pl.ds(0, tm), :], ybuf.at[slot, 0], sem.at[slot]).wait()
    pltpu.make_async_copy(yb_hbm.at[pl.ds(0, tm), :], ybuf.at[slot, 1], sem.at[slot]).wait()
    rf = rf_ref[0]
    y = rf[:, 0:1] * ybuf[slot, 0] + rf[:, 1:2] * ybuf[slot, 1]
    x2 = x1_ref[0] + ga_ref[0] * y
    o_ref[0] = x2 * lax.rsqrt(jnp.mean(x2 * x2, axis=-1, keepdims=True) + EPS) * gf_ref[...]


def moe_combine(x1, route_f, ga2, g_final, yb, dest, tm):
    bsz, L, D = x1.shape
    grid_spec = pltpu.PrefetchScalarGridSpec(
        num_scalar_prefetch=1,
        grid=(bsz, L // tm),
        in_specs=[
            pl.BlockSpec((1, tm, D), lambda b, i, d: (b, i, 0)),
            pl.BlockSpec((1, tm, ROUTE_COLS), lambda b, i, d: (b, i, 0)),
            pl.BlockSpec((1, 1, D), lambda b, i, d: (b, 0, 0)),
            pl.BlockSpec((1, D), lambda b, i, d: (0, 0)),
            pl.BlockSpec(memory_space=pl.ANY),
        ],
        out_specs=pl.BlockSpec((1, tm, D), lambda b, i, d: (b, i, 0)),
        scratch_shapes=[pltpu.VMEM((2, 2, tm, D), jnp.float32), pltpu.SemaphoreType.DMA((2,))],
    )
    return pl.pallas_call(
        _combine_kernel,
        grid_spec=grid_spec,
        out_shape=jax.ShapeDtypeStruct((bsz, L, D), jnp.float32),
        compiler_params=pltpu.CompilerParams(
            dimension_semantics=("arbitrary", "arbitrary"), vmem_limit_bytes=VMEM_LIMIT_BYTES),
    )(dest, x1, route_f, ga2, g_final, yb)


def moe_plan(route_i, counts, blk, n_blocks):
    cnt = counts[0, MOE_GROUPS:MOE_GROUPS + N_EXPERTS].astype(jnp.int32)
    padded = (cnt + blk - 1) // blk * blk
    ends = jnp.cumsum(padded)
    starts = ends - padded
    experts = jnp.arange(N_EXPERTS, dtype=jnp.int32)
    dest = jnp.concatenate([
        (jnp.sum(jnp.where(route_i[:, k, :, None] == experts, starts, 0), axis=-1) + route_i[:, 2 + k]).reshape(-1)
        for k in range(TOP_K)])
    first_row = jnp.arange(n_blocks, dtype=jnp.int32) * blk
    block_eid = jnp.minimum(jnp.sum((ends[None, :] <= first_row[:, None]).astype(jnp.int32), axis=1), N_EXPERTS - 1)
    block_valid = jnp.clip(cnt[block_eid] - (first_row - starts[block_eid]), 0, blk).astype(jnp.int32)
    block_first = jnp.concatenate([jnp.ones((1,), jnp.int32),
                                   (block_eid[1:] != block_eid[:-1]).astype(jnp.int32)])
    idx = jnp.arange(n_blocks, dtype=jnp.int32)
    next_first = lax.cummin(jnp.where(block_first == 1, idx, n_blocks), axis=0, reverse=True)
    block_wsel = jnp.where(next_first < n_blocks, block_eid[jnp.minimum(next_first, n_blocks - 1)], block_eid)
    n_used = jnp.sum((block_valid > 0).astype(jnp.int32))
    block_src = jnp.minimum(idx, jnp.maximum(n_used - 1, 0))
    return dest, block_wsel, block_first, block_valid, block_src


def dft_tables(L):
    n = 2 * L
    f = lax.broadcasted_iota(jnp.int32, (L, L), 0)
    t = lax.broadcasted_iota(jnp.int32, (L, L), 1)
    ang = ((f * t) % n).astype(jnp.float32) * (2.0 * math.pi / n)
    return jnp.cos(ang).astype(jnp.bfloat16), jnp.sin(ang).astype(jnp.bfloat16)


def _alt_sign(L):
    t = lax.broadcasted_iota(jnp.int32, (L, 1), 0)
    return (1 - 2 * (t & 1)).astype(jnp.float32)


def _spectrum_kernel(a_ref, b_ref, c_ref, s_ref, kr_ref, ks_ref, kn_ref):
    L = a_ref.shape[1]
    a = a_ref[0]
    row = lax.broadcasted_iota(jnp.int32, (L, 1), 0)
    scale = jnp.where(row == 0, 0.5 / L, 1.0 / L)
    kr_ref[0] = scale * jnp.dot(c_ref[...], a.astype(jnp.bfloat16), preferred_element_type=jnp.float32)
    ks_ref[0] = scale * jnp.dot(s_ref[...], b_ref[0].astype(jnp.bfloat16), preferred_element_type=jnp.float32)
    kn_ref[0] = jnp.sum(a * _alt_sign(L), axis=0, keepdims=True) * (0.5 / L)


def filter_spectrum(a, b, cos_t, sin_t, tc):
    n, L, C = a.shape
    blk = pl.BlockSpec((1, L, tc), lambda o, j: (o, 0, j))
    tab = pl.BlockSpec((L, L), lambda o, j: (0, 0))
    return pl.pallas_call(
        _spectrum_kernel,
        grid=(n, C // tc),
        in_specs=[blk, blk, tab, tab],
        out_specs=[blk, blk, pl.BlockSpec((1, 1, tc), lambda o, j: (o, 0, j))],
        out_shape=[jax.ShapeDtypeStruct((n, L, C), jnp.float32)] * 2 + [jax.ShapeDtypeStruct((n, 1, C), jnp.float32)],
        compiler_params=pltpu.CompilerParams(
            dimension_semantics=("arbitrary", "arbitrary"), vmem_limit_bytes=VMEM_LIMIT_BYTES),
    )(a, b, cos_t, sin_t)


def _phase_conv3(raw, w_ref, b_ref, rows_per_phase):
    n_ph = len(raw)
    h = raw[0].shape[0]
    j = lax.broadcasted_iota(jnp.int32, (h, 1), 0) % rows_per_phase
    prev0 = jnp.where(j != 0, pltpu.roll(raw[n_ph - 1], 1, 0), 0.0)
    next_last = jnp.where(j != rows_per_phase - 1, pltpu.roll(raw[0], h - 1, 0), 0.0)
    out = []
    for p in range(n_ph):
        prev = raw[p - 1] if p > 0 else prev0
        nxt = raw[p + 1] if p < n_ph - 1 else next_last
        out.append(b_ref[...] + w_ref[0:1, :] * prev + w_ref[1:2, :] * raw[p] + w_ref[2:3, :] * nxt)
    return out


def _long_conv_kernel(*refs, n_ph, n_slab, conv_z, rows_per_phase):
    z_refs = refs[:n_slab]
    xn_refs = refs[n_slab:2 * n_slab]
    (kr_ref, ks_ref, kn_ref, bias_ref, cwz_ref, cbz_ref, cwx_ref, cbx_ref, c_ref, s_ref,
     o_ref, acc_ref, zr_ref, zs_ref, yr_ref, ys_ref, stage_ref) = refs[2 * n_slab:]
    H = z_refs[0].shape[1] // n_ph
    f32 = jnp.float32
    bf = jnp.bfloat16
    sign = _alt_sign(H)

    def phases(slab_refs):
        return [jnp.concatenate([r[0, pl.ds(p, H, stride=n_ph), :] for r in slab_refs], axis=1)
                for p in range(n_ph)]

    z_ph = phases(z_refs)
    if conv_z:
        z_ph = _phase_conv3(z_ph, cwz_ref, cbz_ref, rows_per_phase)
    for q in range(n_ph):
        zb = z_ph[q].astype(bf)
        zr_ref[q] = jnp.dot(c_ref[...], zb, preferred_element_type=f32)
        zs_ref[q] = jnp.dot(s_ref[...], zb, preferred_element_type=f32)
    z_nyq = [jnp.sum(z * sign, axis=0, keepdims=True) for z in z_ph]
    for p in range(n_ph):
        nyq = sum(z_nyq[q] * kn_ref[p - q + n_ph - 1] for q in range(n_ph))
        acc_ref[p] = z_ph[p] * bias_ref[0] + sign * nyq
        yr = 0.0
        ys = 0.0
        for q in range(n_ph):
            slot = p - q + n_ph - 1
            yr = yr + zr_ref[q] * kr_ref[slot] - zs_ref[q] * ks_ref[slot]
            ys = ys + zr_ref[q] * ks_ref[slot] + zs_ref[q] * kr_ref[slot]
        yr_ref[p] = yr.astype(bf)
        ys_ref[p] = ys.astype(bf)
    for p in range(n_ph):
        acc_ref[p] += (jnp.dot(c_ref[...], yr_ref[p], preferred_element_type=f32)
                       + jnp.dot(s_ref[...], ys_ref[p], preferred_element_type=f32))
    x_ph = _phase_conv3(phases(xn_refs), cwx_ref, cbx_ref, rows_per_phase)
    for p in range(n_ph):
        out_p = x_ph[p] * acc_ref[p]
        for sl in range(n_slab):
            stage_ref[sl, pl.ds(p, H, stride=n_ph), :] = out_p[:, sl * LANES:(sl + 1) * LANES]
    for sl in range(n_slab):
        o_ref[0, :, sl * LANES:(sl + 1) * LANES] = stage_ref[sl].astype(o_ref.dtype)


def long_conv_gate(z_arr, z_col, conv_z, xn_arr, xn_col, conv_w, conv_b, kr, ks, kn, bias, cos_t, sin_t,
                   tc, n_ph, row_len, out_dtype):
    bsz, L, _ = z_arr.shape
    H = L // n_ph
    C = kr.shape[-1]
    nj = C // tc
    n_slab = tc // LANES
    n_f = 2 * n_ph - 1
    tab = pl.BlockSpec((H, H), lambda j, b: (0, 0), pipeline_mode=pl.Buffered(1))
    spec = pl.BlockSpec((n_f, H, tc), lambda j, b: (0, 0, j), pipeline_mode=pl.Buffered(1))
    nyq = pl.BlockSpec((n_f, 1, tc), lambda j, b: (0, 0, j))
    vec = pl.BlockSpec((1, 1, tc), lambda j, b: (0, 0, j))

    def slabs(col):
        return [pl.BlockSpec((1, L, LANES),
                             functools.partial(lambda j, b, sl: (b, 0, (col * nj + j) * n_slab + sl), sl=sl))
                for sl in range(n_slab)]

    def conv_specs(col):
        return [pl.BlockSpec((3, tc), lambda j, b: (0, col * nj + j)),
                pl.BlockSpec((1, tc), lambda j, b: (0, col * nj + j))]

    zc = z_col if conv_z else 0
    kern = functools.partial(_long_conv_kernel, n_ph=n_ph, n_slab=n_slab, conv_z=conv_z,
                             rows_per_phase=row_len // n_ph)
    return pl.pallas_call(
        kern,
        grid=(nj, bsz),
        in_specs=(slabs(z_col) + slabs(xn_col) + [spec, spec, nyq, vec] + conv_specs(zc) + conv_specs(xn_col)
                  + [tab, tab]),
        out_specs=pl.BlockSpec((1, L, tc), lambda j, b: (b, 0, j)),
        out_shape=jax.ShapeDtypeStruct((bsz, L, C), out_dtype),
        scratch_shapes=[pltpu.VMEM((n_ph, H, tc), jnp.float32), pltpu.VMEM((n_ph, H, tc), jnp.float32),
                        pltpu.VMEM((n_ph, H, tc), jnp.float32), pltpu.VMEM((n_ph, H, tc), jnp.bfloat16),
                        pltpu.VMEM((n_ph, H, tc), jnp.bfloat16), pltpu.VMEM((n_slab, L, LANES), jnp.float32)],
        compiler_params=pltpu.CompilerParams(
            dimension_semantics=("arbitrary", "arbitrary"), vmem_limit_bytes=VMEM_LIMIT_BYTES),
    )(*([z_arr] * n_slab), *([xn_arr] * n_slab), kr, ks, kn, bias, conv_w, conv_b, conv_w, conv_b, cos_t, sin_t)


def _polyphase_taps(kf, kb, n_ph):
    H = kf.shape[0] // n_ph
    ph = lambda a, p: a[p * H:(p + 1) * H]
    zero = jnp.zeros_like(kf[:1])
    plus, minus = [], []
    for r in range(-(n_ph - 1), n_ph):
        if r >= 0:
            plus.append(ph(kf, r))
        else:
            plus.append(jnp.concatenate([ph(kb, -r)[0:1], ph(kf, n_ph + r)[:-1]], axis=0))
        if r <= 0:
            minus.append(jnp.concatenate([zero, ph(kb, -r)[1:]], axis=0))
        else:
            minus.append(jnp.concatenate([zero, ph(kb, n_ph - r)[:-1]], axis=0))
    return jnp.stack(plus), jnp.stack(minus)


def hyena_long_convs(p_hy, conv_w, conv_b, kp, h_bias, tc, n_ph, row_len):
    L = p_hy.shape[1]
    C = h_bias.shape[1]
    cos_t, sin_t = dft_tables(L // n_ph)
    z = p_hy
    for o in range(h_bias.shape[0]):
        fwd = slice((2 * o) * C, (2 * o + 1) * C)
        bwd = slice((2 * o + 1) * C, (2 * o + 2) * C)
        plus, minus = _polyphase_taps(kp[:, fwd], kp[:, bwd], n_ph)
        kr, ks, kn = filter_spectrum(plus + minus, plus - minus, cos_t, sin_t, tc)
        last = o == h_bias.shape[0] - 1
        z = long_conv_gate(z, 0, o == 0, p_hy, o + 1, conv_w, conv_b, kr, ks, kn, h_bias[o][None, None, :],
                           cos_t, sin_t, tc, n_ph, row_len, jnp.bfloat16 if last else jnp.float32)
    return z


def _filter_kernel(band_ref, w1_ref, b1_ref, fr_ref, w2_ref, b2_ref, w3_ref, dl_ref, k_ref, *, seq_len, n_ph):
    hp = lax.Precision.HIGHEST
    f32 = jnp.float32
    tp = k_ref.shape[0]
    per_phase = seq_len // n_ph
    g = lax.broadcasted_iota(jnp.int32, (tp, 1), 0) + pl.program_id(0) * tp
    phase = g // per_phase
    pos = (n_ph * (g - phase * per_phase) + phase).astype(f32)
    t = pos / max(seq_len - 1, 1)
    ang = (2 * math.pi / seq_len) * pos * band_ref[...]
    lane = lax.broadcasted_iota(jnp.int32, (tp, LANES), 1)
    feats = jnp.where(lane == 0, t,
                      jnp.where(lane <= HYENA_BANDS, jnp.cos(ang),
                                jnp.where(lane <= 2 * HYENA_BANDS, -jnp.sin(ang), 0.0)))
    h = jnp.sin(fr_ref[...] * (jnp.dot(feats, w1_ref[...], precision=hp, preferred_element_type=f32) + b1_ref[...]))
    h = jnp.sin(fr_ref[...] * (jnp.dot(h, w2_ref[...], precision=hp, preferred_element_type=f32) + b2_ref[...]))
    window = jnp.exp(-t * dl_ref[...])
    c = dl_ref.shape[1]
    for j in range(w3_ref.shape[1] // c):
        cols = slice(j * c, (j + 1) * c)
        k_ref[:, cols] = jnp.dot(h, w3_ref[:, cols], precision=hp, preferred_element_type=f32) * window


def hyena_filters_polyphase(seq_len, f_w1, f_b1, f_freq, f_w2, f_b2, f_w3, d_hyena, tp, n_ph):
    f32 = jnp.float32
    fh = f_w1.shape[1]
    n_emb = 1 + 2 * HYENA_BANDS
    bands = jnp.linspace(1e-4, HYENA_BANDS - 1, HYENA_BANDS, dtype=f32)
    band_row = jnp.concatenate([jnp.zeros((1,), f32), bands, bands, jnp.zeros((LANES - n_emb,), f32)])[None, :]
    w1p = jnp.concatenate([f_w1, jnp.zeros((LANES - n_emb, fh), f32)], axis=0)
    deltas = jnp.abs(jnp.linspace(math.log(HYENA_TARGET) / HYENA_SLOW_DECAY,
                                  math.log(HYENA_TARGET) / HYENA_FAST_DECAY, d_hyena, dtype=f32))[None, :]
    n_out = f_w3.shape[1]
    full = lambda a: pl.BlockSpec(a.shape, lambda i: (0,) * a.ndim)
    args = (band_row, w1p, f_b1[None, :], f_freq[None, :], f_w2, f_b2[None, :], f_w3, deltas)
    return pl.pallas_call(
        functools.partial(_filter_kernel, seq_len=seq_len, n_ph=n_ph),
        grid=(seq_len // tp,),
        in_specs=[full(a) for a in args],
        out_specs=pl.BlockSpec((tp, n_out), lambda i: (i, 0)),
        out_shape=jax.ShapeDtypeStruct((seq_len, n_out), f32),
        compiler_params=pltpu.CompilerParams(dimension_semantics=("arbitrary",), vmem_limit_bytes=VMEM_LIMIT_BYTES),
    )(*args)


def _ada_kernel(c_ref, w_ref, b_ref, o_ref):
    cv = c_ref[...]
    s = cv * jax.nn.sigmoid(cv)
    o_ref[...] = jnp.dot(s, w_ref[...], precision=lax.Precision.HIGHEST,
                         preferred_element_type=jnp.float32) + b_ref[...]


def ada_modulation(c_rows, w_ada, b_ada, tn):
    rows, D = c_rows.shape
    N = w_ada.shape[1]
    return pl.pallas_call(
        _ada_kernel,
        grid=(N // tn,),
        in_specs=[pl.BlockSpec((rows, D), lambda j: (0, 0)),
                  pl.BlockSpec((D, tn), lambda j: (0, j)),
                  pl.BlockSpec((1, tn), lambda j: (0, j))],
        out_specs=pl.BlockSpec((rows, tn), lambda j: (0, j)),
        out_shape=jax.ShapeDtypeStruct((rows, N), jnp.float32),
        compiler_params=pltpu.CompilerParams(dimension_semantics=("arbitrary",), vmem_limit_bytes=VMEM_LIMIT_BYTES),
    )(c_rows, w_ada, b_ada[None, :])


def _ssd_kernel(xf_ref, df_ref, xb_ref, db_ref, dskip_ref, y_ref, h_ref, *, n_ctx_chunks):
    s = pl.program_id(1)
    n_steps = pl.num_programs(1)
    Q, G, R, P, N = SSD_CHUNK, SSD_GROUPS, SSD_HPG, SSD_HEAD_DIM, SSD_STATE
    GP = R * P
    bf = jnp.bfloat16

    @pl.when(s == 0)
    def _():
        h_ref[...] = jnp.zeros_like(h_ref)
        y_ref[...] = jnp.zeros_like(y_ref)

    row = lax.broadcasted_iota(jnp.int32, (Q, Q), 0)
    col = lax.broadcasted_iota(jnp.int32, (Q, Q), 1)
    lane_head = lax.broadcasted_iota(jnp.int32, (Q, GP), 1) // P
    is_latent = s >= n_ctx_chunks
    n_lat = n_steps - n_ctx_chunks
    out_chunk = (jnp.clip(s - n_ctx_chunks, 0, n_lat - 1), jnp.clip(n_steps - 1 - s, 0, n_lat - 1))

    for bb, d in [(bb, d) for bb in range(y_ref.shape[0]) for d in range(2)]:
        x_ref, da_ref = ((xf_ref, df_ref), (xb_ref, db_ref))[d]
        mask = (row >= col) if d == 0 else (col >= row)
        tri = mask.astype(jnp.float32)
        da = da_ref[bb]
        cum = jnp.dot(tri, da, precision=lax.Precision.HIGHEST, preferred_element_type=jnp.float32)
        cum_t = cum.T
        edge = Q - 1 if d == 0 else 0
        blk = x_ref.at[bb]
        for g in range(G):
            xg = blk[:, g * GP:(g + 1) * GP]
            bg = blk[:, D_SSD + g * N:D_SSD + (g + 1) * N].astype(bf)
            cg = blk[:, D_SSD + G * N + g * N:D_SSD + G * N + (g + 1) * N].astype(bf)
            heads = [d * SSD_HEADS + g * R + r for r in range(R)]
            dtm = jnp.zeros((Q, GP), jnp.float32)
            cumm = jnp.zeros((Q, GP), jnp.float32)
            for r, h in enumerate(heads):
                sel = lane_head == r
                dtm = jnp.where(sel, da[:, h:h + 1], dtm)
                cumm = jnp.where(sel, cum[:, SSD_HEADS * 2 + h:SSD_HEADS * 2 + h + 1], cumm)
            totm = cumm[edge:edge + 1, :]
            xdt = xg * dtm
            hg = h_ref[bb, d, g * GP:(g + 1) * GP, :]

            gmat = lax.dot_general(cg, bg, (((1,), (1,)), ((), ())), preferred_element_type=jnp.float32)
            y_off = lax.dot_general(cg, hg.astype(bf), (((1,), (1,)), ((), ())),
                                    preferred_element_type=jnp.float32) * jnp.exp(cumm)
            if d == 0:
                y_off = y_off + dskip_ref[:, g * GP:(g + 1) * GP] * xg
            parts = []
            for r, h in enumerate(heads):
                a_col = cum[:, SSD_HEADS * 2 + h:SSD_HEADS * 2 + h + 1]
                a_row = cum_t[SSD_HEADS * 2 + h:SSD_HEADS * 2 + h + 1, :]
                decay = jnp.exp(jnp.where(mask, a_col - a_row, NEG_BIG))
                parts.append(jnp.dot((gmat * decay).astype(bf), xdt[:, r * P:(r + 1) * P].astype(bf),
                                     preferred_element_type=jnp.float32))
            y = jnp.where(is_latent, y_off + jnp.concatenate(parts, axis=-1), 0.0)
            rows = pl.ds(pl.multiple_of(out_chunk[d] * Q, Q), Q)
            y_ref[bb, rows, g * GP:(g + 1) * GP] += y

            xw = (xdt * jnp.exp(totm - cumm)).astype(bf)
            st = lax.dot_general(xw, bg, (((0,), (0,)), ((), ())), preferred_element_type=jnp.float32)
            for r, h in enumerate(heads):
                dec = jnp.exp(cum_t[SSD_HEADS * 2 + h:SSD_HEADS * 2 + h + 1, edge:edge + 1])
                rs = slice(g * GP + r * P, g * GP + (r + 1) * P)
                h_ref[bb, d, rs, :] = h_ref[bb, d, rs, :] * dec + st[r * P:(r + 1) * P, :]


def ssd_scan_bidir(xbc, dta, d_skip, n_ctx, lat_off, nb):
    bsz, lt, width = xbc.shape
    Q = SSD_CHUNK
    n_ctx_chunks = n_ctx // Q
    L = lt - lat_off
    n_lat = L // Q
    n_steps = n_ctx_chunks + n_lat
    lat0 = lat_off // Q

    def fwd_chunk(s):
        return jnp.where(s < n_ctx_chunks, s, s - n_ctx_chunks + lat0)

    def bwd_chunk(s):
        return jnp.where(s < n_ctx_chunks, n_ctx_chunks - 1 - s, n_steps - 1 - s + lat0)

    return pl.pallas_call(
        functools.partial(_ssd_kernel, n_ctx_chunks=n_ctx_chunks),
        grid=(bsz // nb, n_steps),
        in_specs=[
            pl.BlockSpec((nb, Q, width), lambda b, s: (b, fwd_chunk(s), 0)),
            pl.BlockSpec((nb, Q, LANES), lambda b, s: (b, fwd_chunk(s), 0)),
            pl.BlockSpec((nb, Q, width), lambda b, s: (b, bwd_chunk(s), 0)),
            pl.BlockSpec((nb, Q, LANES), lambda b, s: (b, bwd_chunk(s), 0)),
            pl.BlockSpec((1, D_SSD), lambda b, s: (0, 0)),
        ],
        out_specs=pl.BlockSpec((nb, L, D_SSD), lambda b, s: (b, 0, 0)),
        out_shape=jax.ShapeDtypeStruct((bsz, L, D_SSD), jnp.float32),
        scratch_shapes=[pltpu.VMEM((nb, 2, SSD_GROUPS * SSD_HPG * SSD_HEAD_DIM, SSD_STATE), jnp.float32)],
        compiler_params=pltpu.CompilerParams(
            dimension_semantics=("arbitrary", "arbitrary"), vmem_limit_bytes=VMEM_LIMIT_BYTES),
    )(xbc, dta, xbc, dta, d_skip)


def kernel(x, c, ctx, c_ctx, w_ada, b_ada, g_norm1, g_norm2, w_in, hy_conv_w, hy_conv_b, hy_f_w1, hy_f_b1, hy_f_freq, hy_f_w2, hy_f_b2, hy_f_w3, hy_bias, ssd_conv_w, ssd_conv_b, ssd_a_log, ssd_dt_bias, ssd_d, ssd_norm_g, w_out, w_group, b_group, w_expert, b_expert, w1, w3, w2, g_final):
    bsz, seq_len, _ = x.shape
    assert w_in.shape[0] == 1, "single-layer block: the context stream only supplies SSD states"
    l = 0
    rows_pad = -(bsz + 1) % SUBLANES
    c_rows = jnp.concatenate([c, c_ctx[None, :], jnp.zeros((rows_pad, D_MODEL), jnp.float32)], axis=0)
    mod_all = ada_modulation(c_rows, w_ada[l], b_ada[l], ADA_COLS)
    sh1, sc1, ga1, sh2, sc2, ga2 = jnp.split(mod_all[:bsz, None, :], 6, axis=-1)
    csh1, csc1 = mod_all[bsz, :D_MODEL], mod_all[bsz, D_MODEL:2 * D_MODEL]

    w_out_bf = w_out[l].astype(jnp.bfloat16)
    w_in_bf, w_dt_bf, dt_bias2, dt_mult = in_proj_params(w_in[l], ssd_a_log[l], ssd_dt_bias[l],
                                                         HY_COLS, D_SSD, D_XBC)

    ctx_len = ctx.shape[1]
    lat_off = -(-ctx_len // IN_PROJ_ROWS) * IN_PROJ_ROWS
    p_hy, z, xbc, dta = in_proj_fused(ctx, x, g_norm1[l][None, :], csh1[None, :], csc1[None, :], sh1, sc1,
                                      w_in_bf, w_dt_bf, HY_COLS, ssd_conv_w[l], ssd_conv_b[l][None, :],
                                      dt_bias2, dt_mult, GRID_W, ctx_len, IN_PROJ_ROWS, IN_PROJ_COLS)
    kp = hyena_filters_polyphase(seq_len, hy_f_w1[l], hy_f_b1[l], hy_f_freq[l], hy_f_w2[l], hy_f_b2[l],
                                 hy_f_w3[l], D_HYENA, FILTER_ROWS, HYENA_PHASES)
    y_hy = hyena_long_convs(p_hy, hy_conv_w[l], hy_conv_b[l][None, :], kp, hy_bias[l], HYENA_COLS, HYENA_PHASES,
                            GRID_W)
    y_scan = ssd_scan_bidir(xbc, dta, jnp.repeat(ssd_d[l], SSD_HEAD_DIM)[None, :], ctx_len, lat_off, SSD_BATCH)

    pad = LANES - MOE_GROUPS - N_EXPERTS
    w_router = jnp.concatenate([w_group[l], w_expert[l], jnp.zeros((D_MODEL, pad), jnp.float32)], axis=1)
    w_router_hi = w_router.astype(jnp.bfloat16)
    w_router_lo = (w_router - w_router_hi.astype(jnp.float32)).astype(jnp.bfloat16)
    w_router = jnp.stack([w_router_hi, w_router_lo])
    b_router = jnp.concatenate([b_group[l], b_expert[l], jnp.zeros((pad,), jnp.float32)])[None, :]
    x1, hn, route_i, route_f, counts = out_proj_router(
        y_hy, y_scan, z, 0, x, ga1, sc2, sh2, ssd_norm_g[l][None, :], g_norm2[l][None, :],
        w_out_bf, w_router, b_router, TOKEN_ROWS)
    n_tok = bsz * seq_len
    n_blocks = -(-n_tok * TOP_K // MOE_BLOCK) + N_EXPERTS
    dest, block_wsel, block_first, block_valid, block_src = moe_plan(route_i, counts, MOE_BLOCK, n_blocks)
    buf = moe_dispatch(hn.reshape(n_tok, D_MODEL), dest, block_valid, MOE_BLOCK, TOKEN_ROWS)
    yb = expert_blocks(buf, block_wsel, block_first, block_valid, block_src, w1[l], w3[l], w2[l], MOE_BLOCK)
    return moe_combine(x1, route_f, ga2, g_final[None, :], yb, dest, TOKEN_ROWS)
```

```python
import functools
import math

import jax
import jax.numpy as jnp
from jax import lax
from jax.experimental import pallas as pl
from jax.experimental.pallas import tpu as pltpu

D_MODEL = 1024
CTX_LEN = 256
GRID_W = 64
EPS = 1e-6
SHORT_CONV = 3

D_HYENA = D_MODEL // 2
HYENA_ORDER = 2
HYENA_BANDS = 8
HYENA_FAST_DECAY = 0.3
HYENA_SLOW_DECAY = 1.5
HYENA_TARGET = 1e-2
HYENA_PHASES = 4

D_SSD = D_MODEL // 2
SSD_HEAD_DIM = 64
SSD_HEADS = D_SSD // SSD_HEAD_DIM
SSD_GROUPS = 2
SSD_HPG = SSD_HEADS // SSD_GROUPS
SSD_STATE = 128
SSD_CHUNK = 128

D_XBC = D_SSD + 2 * SSD_GROUPS * SSD_STATE
HY_COLS = (HYENA_ORDER + 1) * D_HYENA
D_IN = HY_COLS + D_SSD + D_XBC + 2 * SSD_HEADS
LANES = 128
SUBLANES = 8
D_IN_PAD = -(-D_IN // LANES) * LANES

MOE_GROUPS = 8
EXPERTS_PER_GROUP = 8
N_EXPERTS = MOE_GROUPS * EXPERTS_PER_GROUP
TOP_K = 2
D_EXPERT = 512
MOE_BLOCK = 256
ROUTE_COLS = 8

IN_PROJ_ROWS = 512
IN_PROJ_COLS = 512
TOKEN_ROWS = 256
HYENA_COLS = 256
FILTER_ROWS = 256
ADA_COLS = 512
SSD_BATCH = 2

VMEM_LIMIT_BYTES = 56 * 1024 * 1024
NEG_BIG = -1e30


def _conv3_rows(p, w_ref, b_ref, cols, has_prev, has_next):
    n = p.shape[0]
    prev = jnp.where(has_prev, pltpu.roll(p, 1, 0), 0.0)
    nxt = jnp.where(has_next, pltpu.roll(p, n - 1, 0), 0.0)
    return b_ref[:, cols] + w_ref[0:1, cols] * prev + w_ref[1:2, cols] * p + w_ref[2:3, cols] * nxt


def _in_proj_kernel(ctx_ref, x_ref, g_ref, csh_ref, csc_ref, sh_ref, sc_ref, w_ref, wdt_ref,
                    sw_ref, sb_ref, dtb_ref, dtm_ref, u_ref, z_ref, xbc_ref, dta_ref, h_ref,
                    *, n_ctx_steps, row_len, ctx_row_len, hy_cols, d_ssd, d_xbc, tn):
    i = pl.program_id(1)
    is_ctx = i < n_ctx_steps
    tm = x_ref.shape[1]
    ctx_t = ctx_ref[0]
    if ctx_t.shape[0] < tm:
        ctx_t = jnp.concatenate([ctx_t, jnp.zeros((tm - ctx_t.shape[0], ctx_t.shape[1]), ctx_t.dtype)], axis=0)
    xin = jnp.where(is_ctx, ctx_t, x_ref[0])
    shift = jnp.where(is_ctx, csh_ref[...], sh_ref[0])
    scale = jnp.where(is_ctx, csc_ref[...], sc_ref[0])
    y = xin * lax.rsqrt(jnp.mean(xin * xin, axis=-1, keepdims=True) + EPS) * g_ref[...]
    h_ref[...] = (y * (1.0 + scale) + shift).astype(jnp.bfloat16)

    pos = lax.broadcasted_iota(jnp.int32, (tm, 1), 0) + jnp.where(is_ctx, i, i - n_ctx_steps) * tm
    in_row = jnp.where(is_ctx, pos % ctx_row_len, pos % row_len)
    has_prev = in_row != 0
    has_next = in_row != jnp.where(is_ctx, ctx_row_len - 1, row_len - 1)

    for c0 in range(0, hy_cols, tn):
        cols = slice(c0, c0 + tn)
        u_ref[0, :, cols] = jnp.dot(h_ref[...], w_ref[:, cols], preferred_element_type=jnp.float32)
    z_ref[0] = jnp.dot(h_ref[...], w_ref[:, hy_cols:hy_cols + d_ssd], preferred_element_type=jnp.float32)

    for c0 in range(0, d_xbc, tn):
        cols = slice(c0, c0 + tn)
        wc = slice(hy_cols + d_ssd + c0, hy_cols + d_ssd + c0 + tn)
        p = jnp.dot(h_ref[...], w_ref[:, wc], preferred_element_type=jnp.float32)
        v = _conv3_rows(p, sw_ref, sb_ref, cols, has_prev, has_next)
        xbc_ref[0, :, cols] = v * jax.nn.sigmoid(v)
    pd = jnp.dot(h_ref[...], wdt_ref[...], preferred_element_type=jnp.float32) + dtb_ref[...]
    sp = jnp.maximum(pd, 0.0) + jnp.log(1.0 + jnp.exp(-jnp.abs(pd)))
    dta_ref[0] = sp * dtm_ref[...]


def in_proj_fused(ctx, x, g1, csh, csc, sh, sc, w_bf, wdt_bf, hy_cols, ssd_w, ssd_b, dt_bias2, dt_mult,
                  row_len, ctx_row_len, tm, tn):
    bsz, L, D = x.shape
    lc = ctx.shape[1]
    d_xbc = ssd_w.shape[1]
    d_ssd = w_bf.shape[1] - hy_cols - d_xbc
    n_ctx_steps = -(-lc // tm)
    ctx_rows = min(lc, tm)
    lc = n_ctx_steps * tm
    n_steps = n_ctx_steps + L // tm
    lat = lambda b, i: (b, jnp.maximum(i - n_ctx_steps, 0), 0)
    allt = lambda b, i: (b, i, 0)
    const2 = lambda b, i: (0, 0)
    per_b = pl.BlockSpec((1, 1, D), lambda b, i: (b, 0, 0))
    kern = functools.partial(_in_proj_kernel, n_ctx_steps=n_ctx_steps, row_len=row_len, ctx_row_len=ctx_row_len,
                             hy_cols=hy_cols, d_ssd=d_ssd, d_xbc=d_xbc, tn=tn)
    return pl.pallas_call(
        kern,
        grid=(bsz, n_steps),
        in_specs=[
            pl.BlockSpec((1, ctx_rows, D), lambda b, i: (b, jnp.minimum(i, n_ctx_steps - 1), 0)),
            pl.BlockSpec((1, tm, D), lat),
            pl.BlockSpec((1, D), const2),
            pl.BlockSpec((1, D), const2),
            pl.BlockSpec((1, D), const2),
            per_b, per_b,
            pl.BlockSpec(w_bf.shape, const2),
            pl.BlockSpec(wdt_bf.shape, const2),
            pl.BlockSpec(ssd_w.shape, const2),
            pl.BlockSpec(ssd_b.shape, const2),
            pl.BlockSpec((1, LANES), const2),
            pl.BlockSpec((1, LANES), const2),
        ],
        out_specs=[
            pl.BlockSpec((1, tm, hy_cols), lat),
            pl.BlockSpec((1, tm, d_ssd), lat),
            pl.BlockSpec((1, tm, d_xbc), allt),
            pl.BlockSpec((1, tm, LANES), allt),
        ],
        out_shape=[
            jax.ShapeDtypeStruct((bsz, L, hy_cols), jnp.float32),
            jax.ShapeDtypeStruct((bsz, L, d_ssd), jnp.float32),
            jax.ShapeDtypeStruct((bsz, lc + L, d_xbc), jnp.float32),
            jax.ShapeDtypeStruct((bsz, lc + L, LANES), jnp.float32),
        ],
        scratch_shapes=[pltpu.VMEM((tm, D), jnp.bfloat16)],
        compiler_params=pltpu.CompilerParams(
            dimension_semantics=("arbitrary", "arbitrary"), vmem_limit_bytes=VMEM_LIMIT_BYTES),
    )(ctx, x, g1, csh, csc, sh, sc, w_bf, wdt_bf, ssd_w, ssd_b, dt_bias2, dt_mult)


def in_proj_params(w_in, a_log, dt_bias, hy_cols, d_ssd, d_xbc):
    n_h = 2 * SSD_HEADS
    main = hy_cols + d_ssd + d_xbc
    w_dt = w_in[:, main:main + n_h]
    pad = jnp.zeros((w_in.shape[0], LANES - 2 * n_h), w_in.dtype)
    wdt = jnp.concatenate([w_dt, w_dt, pad], axis=1).astype(jnp.bfloat16)
    zpad = jnp.zeros((LANES - 2 * n_h,), jnp.float32)
    bias2 = jnp.concatenate([dt_bias.reshape(n_h), dt_bias.reshape(n_h), zpad])[None, :]
    mult = jnp.concatenate([jnp.ones((n_h,), jnp.float32), -jnp.exp(a_log).reshape(n_h), zpad])[None, :]
    return w_in[:, :main].astype(jnp.bfloat16), wdt, bias2, mult


def _out_router_kernel(yh_ref, ys_ref, z_ref, x_ref, ga_ref, sc_ref, sh_ref, ng_ref, g2_ref, wo_ref, wr_ref, br_ref,
                       x1_ref, hn_ref, ri_ref, rf_ref, cnt_ref, carry_ref):
    first = jnp.logical_and(pl.program_id(0) == 0, pl.program_id(1) == 0)

    @pl.when(first)
    def _():
        carry_ref[...] = jnp.zeros_like(carry_ref)

    bf = jnp.bfloat16
    tm = x_ref.shape[1]
    dh = yh_ref.shape[2]
    z = z_ref[0]
    ys = ys_ref[0] * (z * jax.nn.sigmoid(z))
    gw = ys.shape[1] // SSD_GROUPS
    acc = jnp.dot(yh_ref[0].astype(bf), wo_ref[0:dh, :], preferred_element_type=jnp.float32)
    for g in range(SSD_GROUPS):
        yg = ys[:, g * gw:(g + 1) * gw]
        yg = yg * lax.rsqrt(jnp.mean(yg * yg, axis=-1, keepdims=True) + EPS) * ng_ref[:, g * gw:(g + 1) * gw]
        acc += jnp.dot(yg.astype(bf), wo_ref[dh + g * gw:dh + (g + 1) * gw, :], preferred_element_type=jnp.float32)
    x1 = x_ref[0] + ga_ref[0] * acc
    x1_ref[0] = x1
    hn = x1 * lax.rsqrt(jnp.mean(x1 * x1, axis=-1, keepdims=True) + EPS) * g2_ref[...]
    hn = hn * (1.0 + sc_ref[0]) + sh_ref[0]
    hn_ref[0] = hn

    hn_hi = hn.astype(bf)
    hn_lo = (hn - hn_hi.astype(jnp.float32)).astype(bf)
    logits = (jnp.dot(hn_hi, wr_ref[0], preferred_element_type=jnp.float32)
              + jnp.dot(hn_lo, wr_ref[0], preferred_element_type=jnp.float32)
              + jnp.dot(hn_hi, wr_ref[1], preferred_element_type=jnp.float32)) + br_ref[...]
    lane = lax.broadcasted_iota(jnp.int32, (tm, LANES), 1)
    lane_f = lane.astype(jnp.float32)
    ninf = jnp.float32(-jnp.inf)
    big = jnp.float32(1e9)
    gl = jnp.where(lane < MOE_GROUPS, logits, ninf)
    gmax = jnp.max(gl, axis=-1, keepdims=True)
    p_group = 1.0 / jnp.sum(jnp.exp(gl - gmax), axis=-1, keepdims=True)
    g_sel = jnp.min(jnp.where(gl == gmax, lane_f, big), axis=-1, keepdims=True)
    e_lane = lane - MOE_GROUPS
    in_grp = jnp.logical_and(e_lane >= 0, (e_lane // EXPERTS_PER_GROUP).astype(jnp.float32) == g_sel)
    el = jnp.where(in_grp, logits, ninf)
    m1 = jnp.max(el, axis=-1, keepdims=True)
    i1 = jnp.min(jnp.where(el == m1, lane_f, big), axis=-1, keepdims=True)
    el2 = jnp.where(lane_f == i1, ninf, el)
    m2 = jnp.max(el2, axis=-1, keepdims=True)
    i2 = jnp.min(jnp.where(el2 == m2, lane_f, big), axis=-1, keepdims=True)
    t = jnp.exp(m2 - m1)
    w1 = 1.0 / (1.0 + t)
    gate1 = w1 * p_group
    gate2 = (t * w1) * p_group
    e1 = i1 - MOE_GROUPS
    e2 = i2 - MOE_GROUPS
    el_f = e_lane.astype(jnp.float32)
    oh1 = el_f == e1
    oh2 = el_f == e2
    oh = jnp.logical_or(oh1, oh2).astype(bf)
    r_i = lax.broadcasted_iota(jnp.int32, (tm, tm), 0)
    c_i = lax.broadcasted_iota(jnp.int32, (tm, tm), 1)
    before = jnp.dot((c_i < r_i).astype(bf), oh, preferred_element_type=jnp.float32) + carry_ref[...]
    rank1 = jnp.sum(jnp.where(oh1, before, 0.0), axis=-1, keepdims=True)
    rank2 = jnp.sum(jnp.where(oh2, before, 0.0), axis=-1, keepdims=True)
    carry_ref[...] += jnp.sum(oh.astype(jnp.float32), axis=0, keepdims=True)
    cnt_ref[...] = carry_ref[...]

    rec = jnp.where(lane == 0, e1, jnp.where(lane == 1, e2, jnp.where(lane == 2, rank1,
                                                                      jnp.where(lane == 3, rank2, 0.0))))
    ri_ref[0] = rec.T[0:ROUTE_COLS, :].astype(jnp.int32)
    col = lax.broadcasted_iota(jnp.int32, (tm, ROUTE_COLS), 1)
    rf_ref[0] = jnp.where(col == 0, gate1, gate2)


def out_proj_router(y_hy, y_scan, px, z_col, x, ga1, sc2, sh2, norm_g, g2, w_out_bf, w_router, b_router, tm):
    bsz, L, D = x.shape
    dh = y_hy.shape[-1]
    ds = y_scan.shape[-1]
    tok = lambda b, i: (b, i, 0)
    per_b = pl.BlockSpec((1, 1, D), lambda b, i: (b, 0, 0))
    const2 = lambda b, i: (0, 0)
    return pl.pallas_call(
        _out_router_kernel,
        grid=(bsz, L // tm),
        in_specs=[
            pl.BlockSpec((1, tm, dh), tok),
            pl.BlockSpec((1, tm, ds), tok),
            pl.BlockSpec((1, tm, ds), lambda b, i: (b, i, z_col)),
            pl.BlockSpec((1, tm, D), tok),
            per_b, per_b, per_b,
            pl.BlockSpec((1, ds), const2),
            pl.BlockSpec((1, D), const2),
            pl.BlockSpec((dh + ds, D), const2),
            pl.BlockSpec((2, D, LANES), lambda b, i: (0, 0, 0)),
            pl.BlockSpec((1, LANES), const2),
        ],
        out_specs=[
            pl.BlockSpec((1, tm, D), tok),
            pl.BlockSpec((1, tm, D), tok),
            pl.BlockSpec((1, ROUTE_COLS, tm), lambda b, i: (b, 0, i)),
            pl.BlockSpec((1, tm, ROUTE_COLS), tok),
            pl.BlockSpec((1, LANES), const2),
        ],
        out_shape=[
            jax.ShapeDtypeStruct((bsz, L, D), jnp.float32),
            jax.ShapeDtypeStruct((bsz, L, D), jnp.float32),
            jax.ShapeDtypeStruct((bsz, ROUTE_COLS, L), jnp.int32),
            jax.ShapeDtypeStruct((bsz, L, ROUTE_COLS), jnp.float32),
            jax.ShapeDtypeStruct((1, LANES), jnp.float32),
        ],
        scratch_shapes=[pltpu.VMEM((1, LANES), jnp.float32)],
        compiler_params=pltpu.CompilerParams(
            dimension_semantics=("arbitrary", "arbitrary"), vmem_limit_bytes=VMEM_LIMIT_BYTES),
    )(y_hy, y_scan, px, x, ga1, sc2, sh2, norm_g, g2, w_out_bf, w_router, b_router)


def _row_copy(src_hbm, src_row, dst_ref, dst_row, sem):
    return pltpu.make_async_copy(src_hbm.at[pl.ds(src_row, 1), :], dst_ref.at[pl.ds(dst_row, 1), :], sem)


def _dispatch_kernel(dest_ref, valid_ref, hn_ref, buf_hbm, zeros, sem, zsem):
    step = pl.program_id(0)
    tm = hn_ref.shape[0]
    n_tok = pl.num_programs(0) * tm
    blk = zeros.shape[0]
    n_blocks = buf_hbm.shape[0] // blk

    def zero_copy(i):
        return pltpu.make_async_copy(zeros, buf_hbm.at[pl.ds(pl.multiple_of(i * blk, blk), blk), :], zsem)

    def zfill(i, carry):
        @pl.when(valid_ref[i] < blk)
        def _():
            zero_copy(i).start()
        return carry

    def zwait(i, carry):
        @pl.when(valid_ref[i] < blk)
        def _():
            zero_copy(i).wait()
        return carry

    @pl.when(step == 0)
    def _():
        zeros[...] = jnp.zeros_like(zeros)
        lax.fori_loop(0, n_blocks, zfill, 0)
        lax.fori_loop(0, n_blocks, zwait, 0)

    def body(j, carry):
        t = step * tm + j
        _row_copy(hn_ref, j, buf_hbm, dest_ref[t], sem).start()
        _row_copy(hn_ref, j, buf_hbm, dest_ref[n_tok + t], sem).start()
        return carry

    lax.fori_loop(0, tm, body, 0, unroll=8)
    for _ in range(2):
        pltpu.make_async_copy(hn_ref, buf_hbm.at[pl.ds(0, tm), :], sem).wait()


def moe_dispatch(hn, dest, block_valid, blk, tm):
    T, D = hn.shape
    n_rows = block_valid.shape[0] * blk
    grid_spec = pltpu.PrefetchScalarGridSpec(
        num_scalar_prefetch=2,
        grid=(T // tm,),
        in_specs=[pl.BlockSpec((tm, D), lambda i, d, v: (i, 0))],
        out_specs=pl.BlockSpec(memory_space=pl.ANY),
        scratch_shapes=[pltpu.VMEM((blk, D), hn.dtype), pltpu.SemaphoreType.DMA(()),
                        pltpu.SemaphoreType.DMA(())],
    )
    return pl.pallas_call(
        _dispatch_kernel,
        grid_spec=grid_spec,
        out_shape=jax.ShapeDtypeStruct((n_rows, D), hn.dtype),
        compiler_params=pltpu.CompilerParams(dimension_semantics=("arbitrary",), has_side_effects=True),
    )(dest, block_valid, hn)


def _expert_kernel(wsel_ref, first_ref, valid_ref, src_ref, x_ref, w1_ref, w3_ref, w2_ref, o_ref, w1b, w3b, w2b):
    i = pl.program_id(0)
    del wsel_ref, src_ref
    bf = jnp.bfloat16

    @pl.when(first_ref[i] == 1)
    def _():
        w1b[...] = w1_ref[0].astype(bf)
        w3b[...] = w3_ref[0].astype(bf)
        w2b[...] = w2_ref[0].astype(bf)

    valid = valid_ref[i]

    @pl.when(valid > 0)
    def _():
        xb = x_ref[...].astype(bf)
        a = jnp.dot(xb, w1b[...], preferred_element_type=jnp.float32)
        b = jnp.dot(xb, w3b[...], preferred_element_type=jnp.float32)
        h = (a * jax.nn.sigmoid(a)) * b
        o_ref[...] = jnp.dot(h.astype(bf), w2b[...], preferred_element_type=jnp.float32)

    @pl.when(valid <= 0)
    def _():
        o_ref[...] = jnp.zeros_like(o_ref)


def expert_blocks(buf, block_wsel, block_first, block_valid, block_src, w1, w3, w2, blk):
    rows, D = buf.shape
    n_blocks = rows // blk
    E, _, F = w1.shape
    grid_spec = pltpu.PrefetchScalarGridSpec(
        num_scalar_prefetch=4,
        grid=(n_blocks,),
        in_specs=[
            pl.BlockSpec((blk, D), lambda i, ws, fi, va, src: (src[i], 0)),
            pl.BlockSpec((1, D, F), lambda i, ws, fi, va, src: (ws[i], 0, 0)),
            pl.BlockSpec((1, D, F), lambda i, ws, fi, va, src: (ws[i], 0, 0)),
            pl.BlockSpec((1, F, D), lambda i, ws, fi, va, src: (ws[i], 0, 0)),
        ],
        out_specs=pl.BlockSpec((blk, D), lambda i, ws, fi, va, src: (i, 0)),
        scratch_shapes=[pltpu.VMEM((D, F), jnp.bfloat16), pltpu.VMEM((D, F), jnp.bfloat16),
                        pltpu.VMEM((F, D), jnp.bfloat16)],
    )
    return pl.pallas_call(
        _expert_kernel,
        grid_spec=grid_spec,
        out_shape=jax.ShapeDtypeStruct((rows, D), jnp.float32),
        compiler_params=pltpu.CompilerParams(
            dimension_semantics=("arbitrary",), vmem_limit_bytes=VMEM_LIMIT_BYTES),
    )(block_wsel, block_first, block_valid, block_src, buf, w1, w3, w2)


def _combine_kernel(dest_ref, x1_ref, rf_ref, ga_ref, gf_ref, yb_hbm, o_ref, ybuf, sem):
    b = pl.program_id(0)
    i = pl.program_id(1)
    n_i = pl.num_programs(1)
    tm = x1_ref.shape[1]
    step = b * n_i + i
    n_steps = pl.num_programs(0) * n_i
    slot = step % 2

    def issue(step_, slot_):
        def body(j, carry):
            t = step_ * tm + j
            _row_copy(yb_hbm, dest_ref[t], ybuf.at[slot_, 0], j, sem.at[slot_]).start()
            _row_copy(yb_hbm, dest_ref[n_steps * tm + t], ybuf.at[slot_, 1], j, sem.at[slot_]).start()
            return carry
        lax.fori_loop(0, tm, body, 0, unroll=8)

    @pl.when(step == 0)
    def _():
        issue(0, 0)

    @pl.when(step + 1 < n_steps)
    def _():
        issue(step + 1, 1 - slot)

    pltpu.make_async_copy(yb_hbm.at[pl.ds(0, tm), :], ybuf.at[slot, 0], sem.at[slot]).wait()
    pltpu.make_async_copy(yb_hbm.at[pl.ds(0, tm), :], ybuf.at[slot, 1], sem.at[slot]).wait()
    rf = rf_ref[0]
    y = rf[:, 0:1] * ybuf[slot, 0] + rf[:, 1:2] * ybuf[slot, 1]
    x2 = x1_ref[0] + ga_ref[0] * y
    o_ref[0] = x2 * lax.rsqrt(jnp.mean(x2 * x2, axis=-1, keepdims=True) + EPS) * gf_ref[...]


def moe_combine(x1, route_f, ga2, g_final, yb, dest, tm):
    bsz, L, D = x1.shape
    grid_spec = pltpu.PrefetchScalarGridSpec(
        num_scalar_prefetch=1,
        grid=(bsz, L // tm),
        in_specs=[
            pl.BlockSpec((1, tm, D), lambda b, i, d: (b, i, 0)),
            pl.BlockSpec((1, tm, ROUTE_COLS), lambda b, i, d: (b, i, 0)),
            pl.BlockSpec((1, 1, D), lambda b, i, d: (b, 0, 0)),
            pl.BlockSpec((1, D), lambda b, i, d: (0, 0)),
            pl.BlockSpec(memory_space=pl.ANY),
        ],
        out_specs=pl.BlockSpec((1, tm, D), lambda b, i, d: (b, i, 0)),
        scratch_shapes=[pltpu.VMEM((2, 2, tm, D), jnp.float32), pltpu.SemaphoreType.DMA((2,))],
    )
    return pl.pallas_call(
        _combine_kernel,
        grid_spec=grid_spec,
        out_shape=jax.ShapeDtypeStruct((bsz, L, D), jnp.float32),
        compiler_params=pltpu.CompilerParams(
            dimension_semantics=("arbitrary", "arbitrary"), vmem_limit_bytes=VMEM_LIMIT_BYTES),
    )(dest, x1, route_f, ga2, g_final, yb)


def moe_plan(route_i, counts, blk, n_blocks):
    cnt = counts[0, MOE_GROUPS:MOE_GROUPS + N_EXPERTS].astype(jnp.int32)
    padded = (cnt + blk - 1) // blk * blk
    ends = jnp.cumsum(padded)
    starts = ends - padded
    experts = jnp.arange(N_EXPERTS, dtype=jnp.int32)
    dest = jnp.concatenate([
        (jnp.sum(jnp.where(route_i[:, k, :, None] == experts, starts, 0), axis=-1) + route_i[:, 2 + k]).reshape(-1)
        for k in range(TOP_K)])
    first_row = jnp.arange(n_blocks, dtype=jnp.int32) * blk
    block_eid = jnp.minimum(jnp.sum((ends[None, :] <= first_row[:, None]).astype(jnp.int32), axis=1), N_EXPERTS - 1)
    block_valid = jnp.clip(cnt[block_eid] - (first_row - starts[block_eid]), 0, blk).astype(jnp.int32)
    block_first = jnp.concatenate([jnp.ones((1,), jnp.int32),
                                   (block_eid[1:] != block_eid[:-1]).astype(jnp.int32)])
    idx = jnp.arange(n_blocks, dtype=jnp.int32)
    next_first = lax.cummin(jnp.where(block_first == 1, idx, n_blocks), axis=0, reverse=True)
    block_wsel = jnp.where(next_first < n_blocks, block_eid[jnp.minimum(next_first, n_blocks - 1)], block_eid)
    n_used = jnp.sum((block_valid > 0).astype(jnp.int32))
    block_src = jnp.minimum(idx, jnp.maximum(n_used - 1, 0))
    return dest, block_wsel, block_first, block_valid, block_src


def dft_tables(L):
    n = 2 * L
    f = lax.broadcasted_iota(jnp.int32, (L, L), 0)
    t = lax.broadcasted_iota(jnp.int32, (L, L), 1)
    ang = ((f * t) % n).astype(jnp.float32) * (2.0 * math.pi / n)
    return jnp.cos(ang).astype(jnp.bfloat16), jnp.sin(ang).astype(jnp.bfloat16)


def _alt_sign(L):
    t = lax.broadcasted_iota(jnp.int32, (L, 1), 0)
    return (1 - 2 * (t & 1)).astype(jnp.float32)


def _spectrum_kernel(a_ref, b_ref, c_ref, s_ref, kr_ref, ks_ref, kn_ref):
    L = a_ref.shape[1]
    a = a_ref[0]
    row = lax.broadcasted_iota(jnp.int32, (L, 1), 0)
    scale = jnp.where(row == 0, 0.5 / L, 1.0 / L)
    kr_ref[0] = scale * jnp.dot(c_ref[...], a.astype(jnp.bfloat16), preferred_element_type=jnp.float32)
    ks_ref[0] = scale * jnp.dot(s_ref[...], b_ref[0].astype(jnp.bfloat16), preferred_element_type=jnp.float32)
    kn_ref[0] = jnp.sum(a * _alt_sign(L), axis=0, keepdims=True) * (0.5 / L)


def filter_spectrum(a, b, cos_t, sin_t, tc):
    n, L, C = a.shape
    blk = pl.BlockSpec((1, L, tc), lambda o, j: (o, 0, j))
    tab = pl.BlockSpec((L, L), lambda o, j: (0, 0))
    return pl.pallas_call(
        _spectrum_kernel,
        grid=(n, C // tc),
        in_specs=[blk, blk, tab, tab],
        out_specs=[blk, blk, pl.BlockSpec((1, 1, tc), lambda o, j: (o, 0, j))],
        out_shape=[jax.ShapeDtypeStruct((n, L, C), jnp.float32)] * 2 + [jax.ShapeDtypeStruct((n, 1, C), jnp.float32)],
        compiler_params=pltpu.CompilerParams(
            dimension_semantics=("arbitrary", "arbitrary"), vmem_limit_bytes=VMEM_LIMIT_BYTES),
    )(a, b, cos_t, sin_t)


def _phase_conv3(raw, w_ref, b_ref, rows_per_phase):
    n_ph = len(raw)
    h = raw[0].shape[0]
    j = lax.broadcasted_iota(jnp.int32, (h, 1), 0) % rows_per_phase
    prev0 = jnp.where(j != 0, pltpu.roll(raw[n_ph - 1], 1, 0), 0.0)
    next_last = jnp.where(j != rows_per_phase - 1, pltpu.roll(raw[0], h - 1, 0), 0.0)
    out = []
    for p in range(n_ph):
        prev = raw[p - 1] if p > 0 else prev0
        nxt = raw[p + 1] if p < n_ph - 1 else next_last
        out.append(b_ref[...] + w_ref[0:1, :] * prev + w_ref[1:2, :] * raw[p] + w_ref[2:3, :] * nxt)
    return out


def _long_conv_kernel(*refs, n_ph, n_slab, conv_z, rows_per_phase):
    z_refs = refs[:n_slab]
    xn_refs = refs[n_slab:2 * n_slab]
    (kr_ref, ks_ref, kn_ref, bias_ref, cwz_ref, cbz_ref, cwx_ref, cbx_ref, c_ref, s_ref,
     o_ref, acc_ref, zr_ref, zs_ref, yr_ref, ys_ref, stage_ref) = refs[2 * n_slab:]
    H = z_refs[0].shape[1] // n_ph
    f32 = jnp.float32
    bf = jnp.bfloat16
    sign = _alt_sign(H)

    def phases(slab_refs):
        return [jnp.concatenate([r[0, pl.ds(p, H, stride=n_ph), :] for r in slab_refs], axis=1)
                for p in range(n_ph)]

    z_ph = phases(z_refs)
    if conv_z:
        z_ph = _phase_conv3(z_ph, cwz_ref, cbz_ref, rows_per_phase)
    for q in range(n_ph):
        zb = z_ph[q].astype(bf)
        zr_ref[q] = jnp.dot(c_ref[...], zb, preferred_element_type=f32)
        zs_ref[q] = jnp.dot(s_ref[...], zb, preferred_element_type=f32)
    z_nyq = [jnp.sum(z * sign, axis=0, keepdims=True) for z in z_ph]
    for p in range(n_ph):
        nyq = sum(z_nyq[q] * kn_ref[p - q + n_ph - 1] for q in range(n_ph))
        acc_ref[p] = z_ph[p] * bias_ref[0] + sign * nyq
        yr = 0.0
        ys = 0.0
        for q in range(n_ph):
            slot = p - q + n_ph - 1
            yr = yr + zr_ref[q] * kr_ref[slot] - zs_ref[q] * ks_ref[slot]
            ys = ys + zr_ref[q] * ks_ref[slot] + zs_ref[q] * kr_ref[slot]
        yr_ref[p] = yr.astype(bf)
        ys_ref[p] = ys.astype(bf)
    for p in range(n_ph):
        acc_ref[p] += (jnp.dot(c_ref[...], yr_ref[p], preferred_element_type=f32)
                       + jnp.dot(s_ref[...], ys_ref[p], preferred_element_type=f32))
    x_ph = _phase_conv3(phases(xn_refs), cwx_ref, cbx_ref, rows_per_phase)
    for p in range(n_ph):
        out_p = x_ph[p] * acc_ref[p]
        for sl in range(n_slab):
            stage_ref[sl, pl.ds(p, H, stride=n_ph), :] = out_p[:, sl * LANES:(sl + 1) * LANES]
    for sl in range(n_slab):
        o_ref[0, :, sl * LANES:(sl + 1) * LANES] = stage_ref[sl].astype(o_ref.dtype)


def long_conv_gate(z_arr, z_col, conv_z, xn_arr, xn_col, conv_w, conv_b, kr, ks, kn, bias, cos_t, sin_t,
                   tc, n_ph, row_len, out_dtype):
    bsz, L, _ = z_arr.shape
    H = L // n_ph
    C = kr.shape[-1]
    nj = C // tc
    n_slab = tc // LANES
    n_f = 2 * n_ph - 1
    tab = pl.BlockSpec((H, H), lambda j, b: (0, 0), pipeline_mode=pl.Buffered(1))
    spec = pl.BlockSpec((n_f, H, tc), lambda j, b: (0, 0, j), pipeline_mode=pl.Buffered(1))
    nyq = pl.BlockSpec((n_f, 1, tc), lambda j, b: (0, 0, j))
    vec = pl.BlockSpec((1, 1, tc), lambda j, b: (0, 0, j))

    def slabs(col):
        return [pl.BlockSpec((1, L, LANES),
                             functools.partial(lambda j, b, sl: (b, 0, (col * nj + j) * n_slab + sl), sl=sl))
                for sl in range(n_slab)]

    def conv_specs(col):
        return [pl.BlockSpec((3, tc), lambda j, b: (0, col * nj + j)),
                pl.BlockSpec((1, tc), lambda j, b: (0, col * nj + j))]

    zc = z_col if conv_z else 0
    kern = functools.partial(_long_conv_kernel, n_ph=n_ph, n_slab=n_slab, conv_z=conv_z,
                             rows_per_phase=row_len // n_ph)
    return pl.pallas_call(
        kern,
        grid=(nj, bsz),
        in_specs=(slabs(z_col) + slabs(xn_col) + [spec, spec, nyq, vec] + conv_specs(zc) + conv_specs(xn_col)
                  + [tab, tab]),
        out_specs=pl.BlockSpec((1, L, tc), lambda j, b: (b, 0, j)),
        out_shape=jax.ShapeDtypeStruct((bsz, L, C), out_dtype),
        scratch_shapes=[pltpu.VMEM((n_ph, H, tc), jnp.float32), pltpu.VMEM((n_ph, H, tc), jnp.float32),
                        pltpu.VMEM((n_ph, H, tc), jnp.float32), pltpu.VMEM((n_ph, H, tc), jnp.bfloat16),
                        pltpu.VMEM((n_ph, H, tc), jnp.bfloat16), pltpu.VMEM((n_slab, L, LANES), jnp.float32)],
        compiler_params=pltpu.CompilerParams(
            dimension_semantics=("arbitrary", "arbitrary"), vmem_limit_bytes=VMEM_LIMIT_BYTES),
    )(*([z_arr] * n_slab), *([xn_arr] * n_slab), kr, ks, kn, bias, conv_w, conv_b, conv_w, conv_b, cos_t, sin_t)


def _polyphase_taps(kf, kb, n_ph):
    H = kf.shape[0] // n_ph
    ph = lambda a, p: a[p * H:(p + 1) * H]
    zero = jnp.zeros_like(kf[:1])
    plus, minus = [], []
    for r in range(-(n_ph - 1), n_ph):
        if r >= 0:
            plus.append(ph(kf, r))
        else:
            plus.append(jnp.concatenate([ph(kb, -r)[0:1], ph(kf, n_ph + r)[:-1]], axis=0))
        if r <= 0:
            minus.append(jnp.concatenate([zero, ph(kb, -r)[1:]], axis=0))
        else:
            minus.append(jnp.concatenate([zero, ph(kb, n_ph - r)[:-1]], axis=0))
    return jnp.stack(plus), jnp.stack(minus)


def hyena_long_convs(p_hy, conv_w, conv_b, kp, h_bias, tc, n_ph, row_len):
    L = p_hy.shape[1]
    C = h_bias.shape[1]
    cos_t, sin_t = dft_tables(L // n_ph)
    z = p_hy
    for o in range(h_bias.shape[0]):
        fwd = slice((2 * o) * C, (2 * o + 1) * C)
        bwd = slice((2 * o + 1) * C, (2 * o + 2) * C)
        plus, minus = _polyphase_taps(kp[:, fwd], kp[:, bwd], n_ph)
        kr, ks, kn = filter_spectrum(plus + minus, plus - minus, cos_t, sin_t, tc)
        last = o == h_bias.shape[0] - 1
        z = long_conv_gate(z, 0, o == 0, p_hy, o + 1, conv_w, conv_b, kr, ks, kn, h_bias[o][None, None, :],
                           cos_t, sin_t, tc, n_ph, row_len, jnp.bfloat16 if last else jnp.float32)
    return z


def _filter_kernel(band_ref, w1_ref, b1_ref, fr_ref, w2_ref, b2_ref, w3_ref, dl_ref, k_ref, *, seq_len, n_ph):
    hp = lax.Precision.HIGHEST
    f32 = jnp.float32
    tp = k_ref.shape[0]
    per_phase = seq_len // n_ph
    g = lax.broadcasted_iota(jnp.int32, (tp, 1), 0) + pl.program_id(0) * tp
    phase = g // per_phase
    pos = (n_ph * (g - phase * per_phase) + phase).astype(f32)
    t = pos / max(seq_len - 1, 1)
    ang = (2 * math.pi / seq_len) * pos * band_ref[...]
    lane = lax.broadcasted_iota(jnp.int32, (tp, LANES), 1)
    feats = jnp.where(lane == 0, t,
                      jnp.where(lane <= HYENA_BANDS, jnp.cos(ang),
                                jnp.where(lane <= 2 * HYENA_BANDS, -jnp.sin(ang), 0.0)))
    h = jnp.sin(fr_ref[...] * (jnp.dot(feats, w1_ref[...], precision=hp, preferred_element_type=f32) + b1_ref[...]))
    h = jnp.sin(fr_ref[...] * (jnp.dot(h, w2_ref[...], precision=hp, preferred_element_type=f32) + b2_ref[...]))
    window = jnp.exp(-t * dl_ref[...])
    c = dl_ref.shape[1]
    for j in range(w3_ref.shape[1] // c):
        cols = slice(j * c, (j + 1) * c)
        k_ref[:, cols] = jnp.dot(h, w3_ref[:, cols], precision=hp, preferred_element_type=f32) * window


def hyena_filters_polyphase(seq_len, f_w1, f_b1, f_freq, f_w2, f_b2, f_w3, d_hyena, tp, n_ph):
    f32 = jnp.float32
    fh = f_w1.shape[1]
    n_emb = 1 + 2 * HYENA_BANDS
    bands = jnp.linspace(1e-4, HYENA_BANDS - 1, HYENA_BANDS, dtype=f32)
    band_row = jnp.concatenate([jnp.zeros((1,), f32), bands, bands, jnp.zeros((LANES - n_emb,), f32)])[None, :]
    w1p = jnp.concatenate([f_w1, jnp.zeros((LANES - n_emb, fh), f32)], axis=0)
    deltas = jnp.abs(jnp.linspace(math.log(HYENA_TARGET) / HYENA_SLOW_DECAY,
                                  math.log(HYENA_TARGET) / HYENA_FAST_DECAY, d_hyena, dtype=f32))[None, :]
    n_out = f_w3.shape[1]
    full = lambda a: pl.BlockSpec(a.shape, lambda i: (0,) * a.ndim)
    args = (band_row, w1p, f_b1[None, :], f_freq[None, :], f_w2, f_b2[None, :], f_w3, deltas)
    return pl.pallas_call(
        functools.partial(_filter_kernel, seq_len=seq_len, n_ph=n_ph),
        grid=(seq_len // tp,),
        in_specs=[full(a) for a in args],
        out_specs=pl.BlockSpec((tp, n_out), lambda i: (i, 0)),
        out_shape=jax.ShapeDtypeStruct((seq_len, n_out), f32),
        compiler_params=pltpu.CompilerParams(dimension_semantics=("arbitrary",), vmem_limit_bytes=VMEM_LIMIT_BYTES),
    )(*args)


def _ada_kernel(c_ref, w_ref, b_ref, o_ref):
    cv = c_ref[...]
    s = cv * jax.nn.sigmoid(cv)
    o_ref[...] = jnp.dot(s, w_ref[...], precision=lax.Precision.HIGHEST,
                         preferred_element_type=jnp.float32) + b_ref[...]


def ada_modulation(c_rows, w_ada, b_ada, tn):
    rows, D = c_rows.shape
    N = w_ada.shape[1]
    return pl.pallas_call(
        _ada_kernel,
        grid=(N // tn,),
        in_specs=[pl.BlockSpec((rows, D), lambda j: (0, 0)),
                  pl.BlockSpec((D, tn), lambda j: (0, j)),
                  pl.BlockSpec((1, tn), lambda j: (0, j))],
        out_specs=pl.BlockSpec((rows, tn), lambda j: (0, j)),
        out_shape=jax.ShapeDtypeStruct((rows, N), jnp.float32),
        compiler_params=pltpu.CompilerParams(dimension_semantics=("arbitrary",), vmem_limit_bytes=VMEM_LIMIT_BYTES),
    )(c_rows, w_ada, b_ada[None, :])


def _ssd_kernel(xf_ref, df_ref, xb_ref, db_ref, dskip_ref, y_ref, h_ref, *, n_ctx_chunks):
    s = pl.program_id(1)
    n_steps = pl.num_programs(1)
    Q, G, R, P, N = SSD_CHUNK, SSD_GROUPS, SSD_HPG, SSD_HEAD_DIM, SSD_STATE
    GP = R * P
    bf = jnp.bfloat16

    @pl.when(s == 0)
    def _():
        h_ref[...] = jnp.zeros_like(h_ref)
        y_ref[...] = jnp.zeros_like(y_ref)

    row = lax.broadcasted_iota(jnp.int32, (Q, Q), 0)
    col = lax.broadcasted_iota(jnp.int32, (Q, Q), 1)
    lane_head = lax.broadcasted_iota(jnp.int32, (Q, GP), 1) // P
    block_head = lax.broadcasted_iota(jnp.int32, (Q, LANES), 1) // P
    is_latent = s >= n_ctx_chunks
    n_lat = n_steps - n_ctx_chunks
    out_chunk = (jnp.clip(s - n_ctx_chunks, 0, n_lat - 1), jnp.clip(n_steps - 1 - s, 0, n_lat - 1))

    for bb, d in [(bb, d) for bb in range(y_ref.shape[0]) for d in range(2)]:
        x_ref, da_ref = ((xf_ref, df_ref), (xb_ref, db_ref))[d]
        mask = (row >= col) if d == 0 else (col >= row)
        tri = mask.astype(jnp.float32)
        da = da_ref[bb]
        cum = jnp.dot(tri, da, precision=lax.Precision.HIGHEST, preferred_element_type=jnp.float32)
        cum_t = cum.T
        edge = Q - 1 if d == 0 else 0
        blk = x_ref.at[bb]
        for g in range(G):
            xg = blk[:, g * GP:(g + 1) * GP]
            bg = blk[:, D_SSD + g * N:D_SSD + (g + 1) * N].astype(bf)
            cg = blk[:, D_SSD + G * N + g * N:D_SSD + G * N + (g + 1) * N].astype(bf)
            heads = [d * SSD_HEADS + g * R + r for r in range(R)]
            dtm = jnp.zeros((Q, GP), jnp.float32)
            cumm = jnp.zeros((Q, GP), jnp.float32)
            for r, h in enumerate(heads):
                sel = lane_head == r
                dtm = jnp.where(sel, da[:, h:h + 1], dtm)
                cumm = jnp.where(sel, cum[:, SSD_HEADS * 2 + h:SSD_HEADS * 2 + h + 1], cumm)
            totm = cumm[edge:edge + 1, :]
            xdt = xg * dtm
            hg = h_ref[bb, d, g * GP:(g + 1) * GP, :]

            gmat = lax.dot_general(cg, bg, (((1,), (1,)), ((), ())), preferred_element_type=jnp.float32)
            y_off = lax.dot_general(cg, hg.astype(bf), (((1,), (1,)), ((), ())),
                                    preferred_element_type=jnp.float32) * jnp.exp(cumm)
            if d == 0:
                y_off = y_off + dskip_ref[:, g * GP:(g + 1) * GP] * xg
            per_block = LANES // P
            parts = [jnp.zeros((Q, LANES), jnp.float32) for _ in range(R // per_block)]
            for r, h in enumerate(heads):
                a_col = cum[:, SSD_HEADS * 2 + h:SSD_HEADS * 2 + h + 1]
                a_row = cum_t[SSD_HEADS * 2 + h:SSD_HEADS * 2 + h + 1, :]
                decay = jnp.exp(jnp.where(mask, a_col - a_row, NEG_BIG))
                j = r // per_block
                xblk = xdt[:, j * LANES:(j + 1) * LANES]
                own = block_head == r % per_block
                parts[j] = parts[j] + jnp.dot((gmat * decay).astype(bf), jnp.where(own, xblk, 0.0).astype(bf),
                                              preferred_element_type=jnp.float32)
            y = jnp.where(is_latent, y_off + jnp.concatenate(parts, axis=-1), 0.0)
            rows = pl.ds(pl.multiple_of(out_chunk[d] * Q, Q), Q)
            y_ref[bb, rows, g * GP:(g + 1) * GP] += y

            xw = (xdt * jnp.exp(totm - cumm)).astype(bf)
            st = lax.dot_general(xw, bg, (((0,), (0,)), ((), ())), preferred_element_type=jnp.float32)
            for r, h in enumerate(heads):
                dec = jnp.exp(cum_t[SSD_HEADS * 2 + h:SSD_HEADS * 2 + h + 1, edge:edge + 1])
                rs = slice(g * GP + r * P, g * GP + (r + 1) * P)
                h_ref[bb, d, rs, :] = h_ref[bb, d, rs, :] * dec + st[r * P:(r + 1) * P, :]


def ssd_scan_bidir(xbc, dta, d_skip, n_ctx, lat_off, nb):
    bsz, lt, width = xbc.shape
    Q = SSD_CHUNK
    n_ctx_chunks = n_ctx // Q
    L = lt - lat_off
    n_lat = L // Q
    n_steps = n_ctx_chunks + n_lat
    lat0 = lat_off // Q

    def fwd_chunk(s):
        return jnp.where(s < n_ctx_chunks, s, s - n_ctx_chunks + lat0)

    def bwd_chunk(s):
        return jnp.where(s < n_ctx_chunks, n_ctx_chunks - 1 - s, n_steps - 1 - s + lat0)

    return pl.pallas_call(
        functools.partial(_ssd_kernel, n_ctx_chunks=n_ctx_chunks),
        grid=(bsz // nb, n_steps),
        in_specs=[
            pl.BlockSpec((nb, Q, width), lambda b, s: (b, fwd_chunk(s), 0)),
            pl.BlockSpec((nb, Q, LANES), lambda b, s: (b, fwd_chunk(s), 0)),
            pl.BlockSpec((nb, Q, width), lambda b, s: (b, bwd_chunk(s), 0)),
            pl.BlockSpec((nb, Q, LANES), lambda b, s: (b, bwd_chunk(s), 0)),
            pl.BlockSpec((1, D_SSD), lambda b, s: (0, 0)),
        ],
        out_specs=pl.BlockSpec((nb, L, D_SSD), lambda b, s: (b, 0, 0)),
        out_shape=jax.ShapeDtypeStruct((bsz, L, D_SSD), jnp.float32),
        scratch_shapes=[pltpu.VMEM((nb, 2, SSD_GROUPS * SSD_HPG * SSD_HEAD_DIM, SSD_STATE), jnp.float32)],
        compiler_params=pltpu.CompilerParams(
            dimension_semantics=("arbitrary", "arbitrary"), vmem_limit_bytes=VMEM_LIMIT_BYTES),
    )(xbc, dta, xbc, dta, d_skip)


def kernel(x, c, ctx, c_ctx, w_ada, b_ada, g_norm1, g_norm2, w_in, hy_conv_w, hy_conv_b, hy_f_w1, hy_f_b1, hy_f_freq, hy_f_w2, hy_f_b2, hy_f_w3, hy_bias, ssd_conv_w, ssd_conv_b, ssd_a_log, ssd_dt_bias, ssd_d, ssd_norm_g, w_out, w_group, b_group, w_expert, b_expert, w1, w3, w2, g_final):
    bsz, seq_len, _ = x.shape
    assert w_in.shape[0] == 1, "single-layer block: the context stream only supplies SSD states"
    l = 0
    rows_pad = -(bsz + 1) % SUBLANES
    c_rows = jnp.concatenate([c, c_ctx[None, :], jnp.zeros((rows_pad, D_MODEL), jnp.float32)], axis=0)
    mod_all = ada_modulation(c_rows, w_ada[l], b_ada[l], ADA_COLS)
    sh1, sc1, ga1, sh2, sc2, ga2 = jnp.split(mod_all[:bsz, None, :], 6, axis=-1)
    csh1, csc1 = mod_all[bsz, :D_MODEL], mod_all[bsz, D_MODEL:2 * D_MODEL]

    w_out_bf = w_out[l].astype(jnp.bfloat16)
    w_in_bf, w_dt_bf, dt_bias2, dt_mult = in_proj_params(w_in[l], ssd_a_log[l], ssd_dt_bias[l],
                                                         HY_COLS, D_SSD, D_XBC)

    ctx_len = ctx.shape[1]
    lat_off = -(-ctx_len // IN_PROJ_ROWS) * IN_PROJ_ROWS
    p_hy, z, xbc, dta = in_proj_fused(ctx, x, g_norm1[l][None, :], csh1[None, :], csc1[None, :], sh1, sc1,
                                      w_in_bf, w_dt_bf, HY_COLS, ssd_conv_w[l], ssd_conv_b[l][None, :],
                                      dt_bias2, dt_mult, GRID_W, ctx_len, IN_PROJ_ROWS, IN_PROJ_COLS)
    kp = hyena_filters_polyphase(seq_len, hy_f_w1[l], hy_f_b1[l], hy_f_freq[l], hy_f_w2[l], hy_f_b2[l],
                                 hy_f_w3[l], D_HYENA, FILTER_ROWS, HYENA_PHASES)
    y_hy = hyena_long_convs(p_hy, hy_conv_w[l], hy_conv_b[l][None, :], kp, hy_bias[l], HYENA_COLS, HYENA_PHASES,
                            GRID_W)
    y_scan = ssd_scan_bidir(xbc, dta, jnp.repeat(ssd_d[l], SSD_HEAD_DIM)[None, :], ctx_len, lat_off, SSD_BATCH)

    pad = LANES - MOE_GROUPS - N_EXPERTS
    w_router = jnp.concatenate([w_group[l], w_expert[l], jnp.zeros((D_MODEL, pad), jnp.float32)], axis=1)
    w_router_hi = w_router.astype(jnp.bfloat16)
    w_router_lo = (w_router - w_router_hi.astype(jnp.float32)).astype(jnp.bfloat16)
    w_router = jnp.stack([w_router_hi, w_router_lo])
    b_router = jnp.concatenate([b_group[l], b_expert[l], jnp.zeros((pad,), jnp.float32)])[None, :]
    x1, hn, route_i, route_f, counts = out_proj_router(
        y_hy, y_scan, z, 0, x, ga1, sc2, sh2, ssd_norm_g[l][None, :], g_norm2[l][None, :],
        w_out_bf, w_router, b_router, TOKEN_ROWS)
    n_tok = bsz * seq_len
    n_blocks = -(-n_tok * TOP_K // MOE_BLOCK) + N_EXPERTS
    dest, block_wsel, block_first, block_valid, block_src = moe_plan(route_i, counts, MOE_BLOCK, n_blocks)
    buf = moe_dispatch(hn.reshape(n_tok, D_MODEL), dest, block_valid, MOE_BLOCK, TOKEN_ROWS)
    yb = expert_blocks(buf, block_wsel, block_first, block_valid, block_src, w1[l], w3[l], w2[l], MOE_BLOCK)
    return moe_combine(x1, route_f, ga2, g_final[None, :], yb, dest, TOKEN_ROWS)
```

```python
import functools
import math

import jax
import jax.numpy as jnp
from jax import lax
from jax.experimental import pallas as pl
from jax.experimental.pallas import tpu as pltpu

D_MODEL = 1024
CTX_LEN = 256
GRID_W = 64
EPS = 1e-6
SHORT_CONV = 3

D_HYENA = D_MODEL // 2
HYENA_ORDER = 2
HYENA_BANDS = 8
HYENA_FAST_DECAY = 0.3
HYENA_SLOW_DECAY = 1.5
HYENA_TARGET = 1e-2
HYENA_PHASES = 4

D_SSD = D_MODEL // 2
SSD_HEAD_DIM = 64
SSD_HEADS = D_SSD // SSD_HEAD_DIM
SSD_GROUPS = 2
SSD_HPG = SSD_HEADS // SSD_GROUPS
SSD_STATE = 128
SSD_CHUNK = 128

D_XBC = D_SSD + 2 * SSD_GROUPS * SSD_STATE
HY_COLS = (HYENA_ORDER + 1) * D_HYENA
D_IN = HY_COLS + D_SSD + D_XBC + 2 * SSD_HEADS
LANES = 128
SUBLANES = 8
D_IN_PAD = -(-D_IN // LANES) * LANES

MOE_GROUPS = 8
EXPERTS_PER_GROUP = 8
N_EXPERTS = MOE_GROUPS * EXPERTS_PER_GROUP
TOP_K = 2
D_EXPERT = 512
MOE_BLOCK = 256
ROUTE_COLS = 8

IN_PROJ_ROWS = 512
IN_PROJ_COLS = 512
TOKEN_ROWS = 256
HYENA_COLS = 256
FILTER_ROWS = 256
ADA_COLS = 512
SSD_BATCH = 2

VMEM_LIMIT_BYTES = 56 * 1024 * 1024
NEG_BIG = -1e30


def _conv3_rows(p, w_ref, b_ref, cols, has_prev, has_next):
    n = p.shape[0]
    prev = jnp.where(has_prev, pltpu.roll(p, 1, 0), 0.0)
    nxt = jnp.where(has_next, pltpu.roll(p, n - 1, 0), 0.0)
    return b_ref[:, cols] + w_ref[0:1, cols] * prev + w_ref[1:2, cols] * p + w_ref[2:3, cols] * nxt


def _in_proj_kernel(ctx_ref, x_ref, g_ref, csh_ref, csc_ref, sh_ref, sc_ref, w_ref, wdt_ref,
                    sw_ref, sb_ref, dtb_ref, dtm_ref, u_ref, z_ref, xbc_ref, dta_ref, h_ref,
                    *, n_ctx_steps, row_len, ctx_row_len, hy_cols, d_ssd, d_xbc, tn):
    i = pl.program_id(1)
    is_ctx = i < n_ctx_steps
    tm = x_ref.shape[1]
    ctx_t = ctx_ref[0]
    if ctx_t.shape[0] < tm:
        ctx_t = jnp.concatenate([ctx_t, jnp.zeros((tm - ctx_t.shape[0], ctx_t.shape[1]), ctx_t.dtype)], axis=0)
    xin = jnp.where(is_ctx, ctx_t, x_ref[0])
    shift = jnp.where(is_ctx, csh_ref[...], sh_ref[0])
    scale = jnp.where(is_ctx, csc_ref[...], sc_ref[0])
    y = xin * lax.rsqrt(jnp.mean(xin * xin, axis=-1, keepdims=True) + EPS) * g_ref[...]
    h_ref[...] = (y * (1.0 + scale) + shift).astype(jnp.bfloat16)

    pos = lax.broadcasted_iota(jnp.int32, (tm, 1), 0) + jnp.where(is_ctx, i, i - n_ctx_steps) * tm
    in_row = jnp.where(is_ctx, pos % ctx_row_len, pos % row_len)
    has_prev = in_row != 0
    has_next = in_row != jnp.where(is_ctx, ctx_row_len - 1, row_len - 1)

    for c0 in range(0, hy_cols, tn):
        cols = slice(c0, c0 + tn)
        u_ref[0, :, cols] = jnp.dot(h_ref[...], w_ref[:, cols], preferred_element_type=jnp.float32)
    z_ref[0] = jnp.dot(h_ref[...], w_ref[:, hy_cols:hy_cols + d_ssd], preferred_element_type=jnp.float32)

    for c0 in range(0, d_xbc, tn):
        cols = slice(c0, c0 + tn)
        wc = slice(hy_cols + d_ssd + c0, hy_cols + d_ssd + c0 + tn)
        p = jnp.dot(h_ref[...], w_ref[:, wc], preferred_element_type=jnp.float32)
        v = _conv3_rows(p, sw_ref, sb_ref, cols, has_prev, has_next)
        xbc_ref[0, :, cols] = v * jax.nn.sigmoid(v)
    pd = jnp.dot(h_ref[...], wdt_ref[...], preferred_element_type=jnp.float32) + dtb_ref[...]
    sp = jnp.maximum(pd, 0.0) + jnp.log(1.0 + jnp.exp(-jnp.abs(pd)))
    dta_ref[0] = sp * dtm_ref[...]


def in_proj_fused(ctx, x, g1, csh, csc, sh, sc, w_bf, wdt_bf, hy_cols, ssd_w, ssd_b, dt_bias2, dt_mult,
                  row_len, ctx_row_len, tm, tn):
    bsz, L, D = x.shape
    lc = ctx.shape[1]
    d_xbc = ssd_w.shape[1]
    d_ssd = w_bf.shape[1] - hy_cols - d_xbc
    n_ctx_steps = -(-lc // tm)
    ctx_rows = min(lc, tm)
    lc = n_ctx_steps * tm
    n_steps = n_ctx_steps + L // tm
    lat = lambda b, i: (b, jnp.maximum(i - n_ctx_steps, 0), 0)
    allt = lambda b, i: (b, i, 0)
    const2 = lambda b, i: (0, 0)
    per_b = pl.BlockSpec((1, 1, D), lambda b, i: (b, 0, 0))
    kern = functools.partial(_in_proj_kernel, n_ctx_steps=n_ctx_steps, row_len=row_len, ctx_row_len=ctx_row_len,
                             hy_cols=hy_cols, d_ssd=d_ssd, d_xbc=d_xbc, tn=tn)
    return pl.pallas_call(
        kern,
        grid=(bsz, n_steps),
        in_specs=[
            pl.BlockSpec((1, ctx_rows, D), lambda b, i: (b, jnp.minimum(i, n_ctx_steps - 1), 0)),
            pl.BlockSpec((1, tm, D), lat),
            pl.BlockSpec((1, D), const2),
            pl.BlockSpec((1, D), const2),
            pl.BlockSpec((1, D), const2),
            per_b, per_b,
            pl.BlockSpec(w_bf.shape, const2),
            pl.BlockSpec(wdt_bf.shape, const2),
            pl.BlockSpec(ssd_w.shape, const2),
            pl.BlockSpec(ssd_b.shape, const2),
            pl.BlockSpec((1, LANES), const2),
            pl.BlockSpec((1, LANES), const2),
        ],
        out_specs=[
            pl.BlockSpec((1, tm, hy_cols), lat),
            pl.BlockSpec((1, tm, d_ssd), lat),
            pl.BlockSpec((1, tm, d_xbc), allt),
            pl.BlockSpec((1, tm, LANES), allt),
        ],
        out_shape=[
            jax.ShapeDtypeStruct((bsz, L, hy_cols), jnp.float32),
            jax.ShapeDtypeStruct((bsz, L, d_ssd), jnp.float32),
            jax.ShapeDtypeStruct((bsz, lc + L, d_xbc), jnp.float32),
            jax.ShapeDtypeStruct((bsz, lc + L, LANES), jnp.float32),
        ],
        scratch_shapes=[pltpu.VMEM((tm, D), jnp.bfloat16)],
        compiler_params=pltpu.CompilerParams(
            dimension_semantics=("arbitrary", "arbitrary"), vmem_limit_bytes=VMEM_LIMIT_BYTES),
    )(ctx, x, g1, csh, csc, sh, sc, w_bf, wdt_bf, ssd_w, ssd_b, dt_bias2, dt_mult)


def in_proj_params(w_in, a_log, dt_bias, hy_cols, d_ssd, d_xbc):
    n_h = 2 * SSD_HEADS
    main = hy_cols + d_ssd + d_xbc
    w_dt = w_in[:, main:main + n_h]
    pad = jnp.zeros((w_in.shape[0], LANES - 2 * n_h), w_in.dtype)
    wdt = jnp.concatenate([w_dt, w_dt, pad], axis=1).astype(jnp.bfloat16)
    zpad = jnp.zeros((LANES - 2 * n_h,), jnp.float32)
    bias2 = jnp.concatenate([dt_bias.reshape(n_h), dt_bias.reshape(n_h), zpad])[None, :]
    mult = jnp.concatenate([jnp.ones((n_h,), jnp.float32), -jnp.exp(a_log).reshape(n_h), zpad])[None, :]
    return w_in[:, :main].astype(jnp.bfloat16), wdt, bias2, mult


def _out_router_kernel(yh_ref, ys_ref, z_ref, x_ref, ga_ref, sc_ref, sh_ref, ng_ref, g2_ref, wo_ref, wr_ref, br_ref,
                       x1_ref, hn_ref, ri_ref, rf_ref, cnt_ref, carry_ref):
    first = jnp.logical_and(pl.program_id(0) == 0, pl.program_id(1) == 0)

    @pl.when(first)
    def _():
        carry_ref[...] = jnp.zeros_like(carry_ref)

    bf = jnp.bfloat16
    tm = x_ref.shape[1]
    dh = yh_ref.shape[2]
    z = z_ref[0]
    ys = ys_ref[0] * (z * jax.nn.sigmoid(z))
    gw = ys.shape[1] // SSD_GROUPS
    acc = jnp.dot(yh_ref[0].astype(bf), wo_ref[0:dh, :], preferred_element_type=jnp.float32)
    for g in range(SSD_GROUPS):
        yg = ys[:, g * gw:(g + 1) * gw]
        yg = yg * lax.rsqrt(jnp.mean(yg * yg, axis=-1, keepdims=True) + EPS) * ng_ref[:, g * gw:(g + 1) * gw]
        acc += jnp.dot(yg.astype(bf), wo_ref[dh + g * gw:dh + (g + 1) * gw, :], preferred_element_type=jnp.float32)
    x1 = x_ref[0] + ga_ref[0] * acc
    x1_ref[0] = x1
    hn = x1 * lax.rsqrt(jnp.mean(x1 * x1, axis=-1, keepdims=True) + EPS) * g2_ref[...]
    hn = hn * (1.0 + sc_ref[0]) + sh_ref[0]
    hn_ref[0] = hn

    hn_hi = hn.astype(bf)
    hn_lo = (hn - hn_hi.astype(jnp.float32)).astype(bf)
    logits = (jnp.dot(hn_hi, wr_ref[0], preferred_element_type=jnp.float32)
              + jnp.dot(hn_lo, wr_ref[0], preferred_element_type=jnp.float32)
              + jnp.dot(hn_hi, wr_ref[1], preferred_element_type=jnp.float32)) + br_ref[...]
    lane = lax.broadcasted_iota(jnp.int32, (tm, LANES), 1)
    lane_f = lane.astype(jnp.float32)
    ninf = jnp.float32(-jnp.inf)
    big = jnp.float32(1e9)
    gl = jnp.where(lane < MOE_GROUPS, logits, ninf)
    gmax = jnp.max(gl, axis=-1, keepdims=True)
    p_group = 1.0 / jnp.sum(jnp.exp(gl - gmax), axis=-1, keepdims=True)
    g_sel = jnp.min(jnp.where(gl == gmax, lane_f, big), axis=-1, keepdims=True)
    e_lane = lane - MOE_GROUPS
    in_grp = jnp.logical_and(e_lane >= 0, (e_lane // EXPERTS_PER_GROUP).astype(jnp.float32) == g_sel)
    el = jnp.where(in_grp, logits, ninf)
    m1 = jnp.max(el, axis=-1, keepdims=True)
    i1 = jnp.min(jnp.where(el == m1, lane_f, big), axis=-1, keepdims=True)
    el2 = jnp.where(lane_f == i1, ninf, el)
    m2 = jnp.max(el2, axis=-1, keepdims=True)
    i2 = jnp.min(jnp.where(el2 == m2, lane_f, big), axis=-1, keepdims=True)
    t = jnp.exp(m2 - m1)
    w1 = 1.0 / (1.0 + t)
    gate1 = w1 * p_group
    gate2 = (t * w1) * p_group
    e1 = i1 - MOE_GROUPS
    e2 = i2 - MOE_GROUPS
    el_f = e_lane.astype(jnp.float32)
    oh1 = el_f == e1
    oh2 = el_f == e2
    oh = jnp.logical_or(oh1, oh2).astype(bf)
    r_i = lax.broadcasted_iota(jnp.int32, (tm, tm), 0)
    c_i = lax.broadcasted_iota(jnp.int32, (tm, tm), 1)
    before = jnp.dot((c_i < r_i).astype(bf), oh, preferred_element_type=jnp.float32) + carry_ref[...]
    rank1 = jnp.sum(jnp.where(oh1, before, 0.0), axis=-1, keepdims=True)
    rank2 = jnp.sum(jnp.where(oh2, before, 0.0), axis=-1, keepdims=True)
    carry_ref[...] += jnp.sum(oh.astype(jnp.float32), axis=0, keepdims=True)
    cnt_ref[...] = carry_ref[...]

    rec = jnp.where(lane == 0, e1, jnp.where(lane == 1, e2, jnp.where(lane == 2, rank1,
                                                                      jnp.where(lane == 3, rank2, 0.0))))
    ri_ref[0] = rec.T[0:ROUTE_COLS, :].astype(jnp.int32)
    col = lax.broadcasted_iota(jnp.int32, (tm, ROUTE_COLS), 1)
    rf_ref[0] = jnp.where(col == 0, gate1, gate2)


def out_proj_router(y_hy, y_scan, px, z_col, x, ga1, sc2, sh2, norm_g, g2, w_out_bf, w_router, b_router, tm):
    bsz, L, D = x.shape
    dh = y_hy.shape[-1]
    ds = y_scan.shape[-1]
    tok = lambda b, i: (b, i, 0)
    per_b = pl.BlockSpec((1, 1, D), lambda b, i: (b, 0, 0))
    const2 = lambda b, i: (0, 0)
    return pl.pallas_call(
        _out_router_kernel,
        grid=(bsz, L // tm),
        in_specs=[
            pl.BlockSpec((1, tm, dh), tok),
            pl.BlockSpec((1, tm, ds), tok),
            pl.BlockSpec((1, tm, ds), lambda b, i: (b, i, z_col)),
            pl.BlockSpec((1, tm, D), tok),
            per_b, per_b, per_b,
            pl.BlockSpec((1, ds), const2),
            pl.BlockSpec((1, D), const2),
            pl.BlockSpec((dh + ds, D), const2),
            pl.BlockSpec((2, D, LANES), lambda b, i: (0, 0, 0)),
            pl.BlockSpec((1, LANES), const2),
        ],
        out_specs=[
            pl.BlockSpec((1, tm, D), tok),
            pl.BlockSpec((1, tm, D), tok),
            pl.BlockSpec((1, ROUTE_COLS, tm), lambda b, i: (b, 0, i)),
            pl.BlockSpec((1, tm, ROUTE_COLS), tok),
            pl.BlockSpec((1, LANES), const2),
        ],
        out_shape=[
            jax.ShapeDtypeStruct((bsz, L, D), jnp.float32),
            jax.ShapeDtypeStruct((bsz, L, D), jnp.float32),
            jax.ShapeDtypeStruct((bsz, ROUTE_COLS, L), jnp.int32),
            jax.ShapeDtypeStruct((bsz, L, ROUTE_COLS), jnp.float32),
            jax.ShapeDtypeStruct((1, LANES), jnp.float32),
        ],
        scratch_shapes=[pltpu.VMEM((1, LANES), jnp.float32)],
        compiler_params=pltpu.CompilerParams(
            dimension_semantics=("arbitrary", "arbitrary"), vmem_limit_bytes=VMEM_LIMIT_BYTES),
    )(y_hy, y_scan, px, x, ga1, sc2, sh2, norm_g, g2, w_out_bf, w_router, b_router)


def _row_copy(src_hbm, src_row, dst_ref, dst_row, sem):
    return pltpu.make_async_copy(src_hbm.at[pl.ds(src_row, 1), :], dst_ref.at[pl.ds(dst_row, 1), :], sem)


def _dispatch_kernel(dest_ref, valid_ref, hn_ref, buf_hbm, zeros, sem, zsem):
    step = pl.program_id(0)
    tm = hn_ref.shape[0]
    n_tok = pl.num_programs(0) * tm
    blk = zeros.shape[0]
    n_blocks = buf_hbm.shape[0] // blk

    def zero_copy(i):
        return pltpu.make_async_copy(zeros, buf_hbm.at[pl.ds(pl.multiple_of(i * blk, blk), blk), :], zsem)

    def zfill(i, carry):
        @pl.when(valid_ref[i] < blk)
        def _():
            zero_copy(i).start()
        return carry

    def zwait(i, carry):
        @pl.when(valid_ref[i] < blk)
        def _():
            zero_copy(i).wait()
        return carry

    @pl.when(step == 0)
    def _():
        zeros[...] = jnp.zeros_like(zeros)
        lax.fori_loop(0, n_blocks, zfill, 0)
        lax.fori_loop(0, n_blocks, zwait, 0)

    def body(j, carry):
        t = step * tm + j
        _row_copy(hn_ref, j, buf_hbm, dest_ref[t], sem).start()
        _row_copy(hn_ref, j, buf_hbm, dest_ref[n_tok + t], sem).start()
        return carry

    lax.fori_loop(0, tm, body, 0, unroll=8)
    for _ in range(2):
        pltpu.make_async_copy(hn_ref, buf_hbm.at[pl.ds(0, tm), :], sem).wait()


def moe_dispatch(hn, dest, block_valid, blk, tm):
    T, D = hn.shape
    n_rows = block_valid.shape[0] * blk
    grid_spec = pltpu.PrefetchScalarGridSpec(
        num_scalar_prefetch=2,
        grid=(T // tm,),
        in_specs=[pl.BlockSpec((tm, D), lambda i, d, v: (i, 0))],
        out_specs=pl.BlockSpec(memory_space=pl.ANY),
        scratch_shapes=[pltpu.VMEM((blk, D), hn.dtype), pltpu.SemaphoreType.DMA(()),
                        pltpu.SemaphoreType.DMA(())],
    )
    return pl.pallas_call(
        _dispatch_kernel,
        grid_spec=grid_spec,
        out_shape=jax.ShapeDtypeStruct((n_rows, D), hn.dtype),
        compiler_params=pltpu.CompilerParams(dimension_semantics=("arbitrary",), has_side_effects=True),
    )(dest, block_valid, hn)


def _expert_kernel(wsel_ref, first_ref, valid_ref, src_ref, x_ref, w1_ref, w3_ref, w2_ref, o_ref, w1b, w3b, w2b):
    i = pl.program_id(0)
    del wsel_ref, src_ref
    bf = jnp.bfloat16

    @pl.when(first_ref[i] == 1)
    def _():
        w1b[...] = w1_ref[0].astype(bf)
        w3b[...] = w3_ref[0].astype(bf)
        w2b[...] = w2_ref[0].astype(bf)

    valid = valid_ref[i]

    @pl.when(valid > 0)
    def _():
        xb = x_ref[...].astype(bf)
        a = jnp.dot(xb, w1b[...], preferred_element_type=jnp.float32)
        b = jnp.dot(xb, w3b[...], preferred_element_type=jnp.float32)
        h = (a * jax.nn.sigmoid(a)) * b
        o_ref[...] = jnp.dot(h.astype(bf), w2b[...], preferred_element_type=jnp.float32)

    @pl.when(valid <= 0)
    def _():
        o_ref[...] = jnp.zeros_like(o_ref)


def expert_blocks(buf, block_wsel, block_first, block_valid, block_src, w1, w3, w2, blk):
    rows, D = buf.shape
    n_blocks = rows // blk
    E, _, F = w1.shape
    grid_spec = pltpu.PrefetchScalarGridSpec(
        num_scalar_prefetch=4,
        grid=(n_blocks,),
        in_specs=[
            pl.BlockSpec((blk, D), lambda i, ws, fi, va, src: (src[i], 0)),
            pl.BlockSpec((1, D, F), lambda i, ws, fi, va, src: (ws[i], 0, 0)),
            pl.BlockSpec((1, D, F), lambda i, ws, fi, va, src: (ws[i], 0, 0)),
            pl.BlockSpec((1, F, D), lambda i, ws, fi, va, src: (ws[i], 0, 0)),
        ],
        out_specs=pl.BlockSpec((blk, D), lambda i, ws, fi, va, src: (i, 0)),
        scratch_shapes=[pltpu.VMEM((D, F), jnp.bfloat16), pltpu.VMEM((D, F), jnp.bfloat16),
                        pltpu.VMEM((F, D), jnp.bfloat16)],
    )
    return pl.pallas_call(
        _expert_kernel,
        grid_spec=grid_spec,
        out_shape=jax.ShapeDtypeStruct((rows, D), jnp.float32),
        compiler_params=pltpu.CompilerParams(
            dimension_semantics=("arbitrary",), vmem_limit_bytes=VMEM_LIMIT_BYTES),
    )(block_wsel, block_first, block_valid, block_src, buf, w1, w3, w2)


def _combine_kernel(dest_ref, x1_ref, rf_ref, ga_ref, gf_ref, yb_hbm, o_ref, ybuf, sem):
    b = pl.program_id(0)
    i = pl.program_id(1)
    n_i = pl.num_programs(1)
    tm = x1_ref.shape[1]
    step = b * n_i + i
    n_steps = pl.num_programs(0) * n_i
    slot = step % 2

    def issue(step_, slot_):
        def body(j, carry):
            t = step_ * tm + j
            _row_copy(yb_hbm, dest_ref[t], ybuf.at[slot_, 0], j, sem.at[slot_]).start()
            _row_copy(yb_hbm, dest_ref[n_steps * tm + t], ybuf.at[slot_, 1], j, sem.at[slot_]).start()
            return carry
        lax.fori_loop(0, tm, body, 0, unroll=8)

    @pl.when(step == 0)
    def _():
        issue(0, 0)

    @pl.when(step + 1 < n_steps)
    def _():
        issue(step + 1, 1 - slot)

    pltpu.make_async_copy(yb_hbm.at[pl.ds(0, tm), :], ybuf.at[slot, 0], sem.at[slot]).wait()
    pltpu.make_async_copy(yb_hbm.at[pl.ds(0, tm), :], ybuf.at[slot, 1], sem.at[slot]).wait()
    rf = rf_ref[0]
    y = rf[:, 0:1] * ybuf[slot, 0] + rf[:, 1:2] * ybuf[slot, 1]
    x2 = x1_ref[0] + ga_ref[0] * y
    o_ref[0] = x2 * lax.rsqrt(jnp.mean(x2 * x2, axis=-1, keepdims=True) + EPS) * gf_ref[...]


def moe_combine(x1, route_f, ga2, g_final, yb, dest, tm):
    bsz, L, D = x1.shape
    grid_spec = pltpu.PrefetchScalarGridSpec(
        num_scalar_prefetch=1,
        grid=(bsz, L // tm),
        in_specs=[
            pl.BlockSpec((1, tm, D), lambda b, i, d: (b, i, 0)),
            pl.BlockSpec((1, tm, ROUTE_COLS), lambda b, i, d: (b, i, 0)),
            pl.BlockSpec((1, 1, D), lambda b, i, d: (b, 0, 0)),
            pl.BlockSpec((1, D), lambda b, i, d: (0, 0)),
            pl.BlockSpec(memory_space=pl.ANY),
        ],
        out_specs=pl.BlockSpec((1, tm, D), lambda b, i, d: (b, i, 0)),
        scratch_shapes=[pltpu.VMEM((2, 2, tm, D), jnp.float32), pltpu.SemaphoreType.DMA((2,))],
    )
    return pl.pallas_call(
        _combine_kernel,
        grid_spec=grid_spec,
        out_shape=jax.ShapeDtypeStruct((bsz, L, D), jnp.float32),
        compiler_params=pltpu.CompilerParams(
            dimension_semantics=("arbitrary", "arbitrary"), vmem_limit_bytes=VMEM_LIMIT_BYTES),
    )(dest, x1, route_f, ga2, g_final, yb)


def moe_plan(route_i, counts, blk, n_blocks):
    cnt = counts[0, MOE_GROUPS:MOE_GROUPS + N_EXPERTS].astype(jnp.int32)
    padded = (cnt + blk - 1) // blk * blk
    ends = jnp.cumsum(padded)
    starts = ends - padded
    experts = jnp.arange(N_EXPERTS, dtype=jnp.int32)
    dest = jnp.concatenate([
        (jnp.sum(jnp.where(route_i[:, k, :, None] == experts, starts, 0), axis=-1) + route_i[:, 2 + k]).reshape(-1)
        for k in range(TOP_K)])
    first_row = jnp.arange(n_blocks, dtype=jnp.int32) * blk
    block_eid = jnp.minimum(jnp.sum((ends[None, :] <= first_row[:, None]).astype(jnp.int32), axis=1), N_EXPERTS - 1)
    block_valid = jnp.clip(cnt[block_eid] - (first_row - starts[block_eid]), 0, blk).astype(jnp.int32)
    block_first = jnp.concatenate([jnp.ones((1,), jnp.int32),
                                   (block_eid[1:] != block_eid[:-1]).astype(jnp.int32)])
    idx = jnp.arange(n_blocks, dtype=jnp.int32)
    next_first = lax.cummin(jnp.where(block_first == 1, idx, n_blocks), axis=0, reverse=True)
    block_wsel = jnp.where(next_first < n_blocks, block_eid[jnp.minimum(next_first, n_blocks - 1)], block_eid)
    n_used = jnp.sum((block_valid > 0).astype(jnp.int32))
    block_src = jnp.minimum(idx, jnp.maximum(n_used - 1, 0))
    return dest, block_wsel, block_first, block_valid, block_src


def dft_tables(L):
    n = 2 * L
    f = lax.broadcasted_iota(jnp.int32, (L, L), 0)
    t = lax.broadcasted_iota(jnp.int32, (L, L), 1)
    ang = ((f * t) % n).astype(jnp.float32) * (2.0 * math.pi / n)
    return jnp.cos(ang).astype(jnp.bfloat16), jnp.sin(ang).astype(jnp.bfloat16)


def _alt_sign(L):
    t = lax.broadcasted_iota(jnp.int32, (L, 1), 0)
    return (1 - 2 * (t & 1)).astype(jnp.float32)


def _spectrum_kernel(a_ref, b_ref, c_ref, s_ref, kr_ref, ks_ref, kn_ref):
    L = a_ref.shape[1]
    a = a_ref[0]
    row = lax.broadcasted_iota(jnp.int32, (L, 1), 0)
    scale = jnp.where(row == 0, 0.5 / L, 1.0 / L)
    kr_ref[0] = scale * jnp.dot(c_ref[...], a.astype(jnp.bfloat16), preferred_element_type=jnp.float32)
    ks_ref[0] = scale * jnp.dot(s_ref[...], b_ref[0].astype(jnp.bfloat16), preferred_element_type=jnp.float32)
    kn_ref[0] = jnp.sum(a * _alt_sign(L), axis=0, keepdims=True) * (0.5 / L)


def filter_spectrum(a, b, cos_t, sin_t, tc):
    n, L, C = a.shape
    blk = pl.BlockSpec((1, L, tc), lambda o, j: (o, 0, j))
    tab = pl.BlockSpec((L, L), lambda o, j: (0, 0))
    return pl.pallas_call(
        _spectrum_kernel,
        grid=(n, C // tc),
        in_specs=[blk, blk, tab, tab],
        out_specs=[blk, blk, pl.BlockSpec((1, 1, tc), lambda o, j: (o, 0, j))],
        out_shape=[jax.ShapeDtypeStruct((n, L, C), jnp.float32)] * 2 + [jax.ShapeDtypeStruct((n, 1, C), jnp.float32)],
        compiler_params=pltpu.CompilerParams(
            dimension_semantics=("arbitrary", "arbitrary"), vmem_limit_bytes=VMEM_LIMIT_BYTES),
    )(a, b, cos_t, sin_t)


def _phase_conv3(raw, w_ref, b_ref, rows_per_phase):
    n_ph = len(raw)
    h = raw[0].shape[0]
    j = lax.broadcasted_iota(jnp.int32, (h, 1), 0) % rows_per_phase
    prev0 = jnp.where(j != 0, pltpu.roll(raw[n_ph - 1], 1, 0), 0.0)
    next_last = jnp.where(j != rows_per_phase - 1, pltpu.roll(raw[0], h - 1, 0), 0.0)
    out = []
    for p in range(n_ph):
        prev = raw[p - 1] if p > 0 else prev0
        nxt = raw[p + 1] if p < n_ph - 1 else next_last
        out.append(b_ref[...] + w_ref[0:1, :] * prev + w_ref[1:2, :] * raw[p] + w_ref[2:3, :] * nxt)
    return out


def _long_conv_kernel(*refs, n_ph, n_slab, conv_z, rows_per_phase):
    z_refs = refs[:n_slab]
    xn_refs = refs[n_slab:2 * n_slab]
    (kr_ref, ks_ref, kn_ref, bias_ref, cwz_ref, cbz_ref, cwx_ref, cbx_ref, c_ref, s_ref,
     o_ref, acc_ref, zr_ref, zs_ref, yr_ref, ys_ref, stage_ref) = refs[2 * n_slab:]
    H = z_refs[0].shape[1] // n_ph
    f32 = jnp.float32
    bf = jnp.bfloat16
    sign = _alt_sign(H)

    def phases(slab_refs):
        return [jnp.concatenate([r[0, pl.ds(p, H, stride=n_ph), :] for r in slab_refs], axis=1)
                for p in range(n_ph)]

    z_ph = phases(z_refs)
    if conv_z:
        z_ph = _phase_conv3(z_ph, cwz_ref, cbz_ref, rows_per_phase)
    for q in range(n_ph):
        zb = z_ph[q].astype(bf)
        zr_ref[q] = jnp.dot(c_ref[...], zb, preferred_element_type=f32)
        zs_ref[q] = jnp.dot(s_ref[...], zb, preferred_element_type=f32)
    z_nyq = [jnp.sum(z * sign, axis=0, keepdims=True) for z in z_ph]
    for p in range(n_ph):
        nyq = sum(z_nyq[q] * kn_ref[p - q + n_ph - 1] for q in range(n_ph))
        acc_ref[p] = z_ph[p] * bias_ref[0] + sign * nyq
        yr = 0.0
        ys = 0.0
        for q in range(n_ph):
            slot = p - q + n_ph - 1
            yr = yr + zr_ref[q] * kr_ref[slot] - zs_ref[q] * ks_ref[slot]
            ys = ys + zr_ref[q] * ks_ref[slot] + zs_ref[q] * kr_ref[slot]
        yr_ref[p] = yr.astype(bf)
        ys_ref[p] = ys.astype(bf)
    for p in range(n_ph):
        acc_ref[p] += (jnp.dot(c_ref[...], yr_ref[p], preferred_element_type=f32)
                       + jnp.dot(s_ref[...], ys_ref[p], preferred_element_type=f32))
    x_ph = _phase_conv3(phases(xn_refs), cwx_ref, cbx_ref, rows_per_phase)
    for p in range(n_ph):
        out_p = x_ph[p] * acc_ref[p]
        for sl in range(n_slab):
            stage_ref[sl, pl.ds(p, H, stride=n_ph), :] = out_p[:, sl * LANES:(sl + 1) * LANES]
    for sl in range(n_slab):
        o_ref[0, :, sl * LANES:(sl + 1) * LANES] = stage_ref[sl].astype(o_ref.dtype)


def long_conv_gate(z_arr, z_col, conv_z, xn_arr, xn_col, conv_w, conv_b, kr, ks, kn, bias, cos_t, sin_t,
                   tc, n_ph, row_len, out_dtype):
    bsz, L, _ = z_arr.shape
    H = L // n_ph
    C = kr.shape[-1]
    nj = C // tc
    n_slab = tc // LANES
    n_f = 2 * n_ph - 1
    tab = pl.BlockSpec((H, H), lambda j, b: (0, 0), pipeline_mode=pl.Buffered(1))
    spec = pl.BlockSpec((n_f, H, tc), lambda j, b: (0, 0, j), pipeline_mode=pl.Buffered(1))
    nyq = pl.BlockSpec((n_f, 1, tc), lambda j, b: (0, 0, j))
    vec = pl.BlockSpec((1, 1, tc), lambda j, b: (0, 0, j))

    def slabs(col):
        return [pl.BlockSpec((1, L, LANES),
                             functools.partial(lambda j, b, sl: (b, 0, (col * nj + j) * n_slab + sl), sl=sl))
                for sl in range(n_slab)]

    def conv_specs(col):
        return [pl.BlockSpec((3, tc), lambda j, b: (0, col * nj + j)),
                pl.BlockSpec((1, tc), lambda j, b: (0, col * nj + j))]

    zc = z_col if conv_z else 0
    kern = functools.partial(_long_conv_kernel, n_ph=n_ph, n_slab=n_slab, conv_z=conv_z,
                             rows_per_phase=row_len // n_ph)
    return pl.pallas_call(
        kern,
        grid=(nj, bsz),
        in_specs=(slabs(z_col) + slabs(xn_col) + [spec, spec, nyq, vec] + conv_specs(zc) + conv_specs(xn_col)
                  + [tab, tab]),
        out_specs=pl.BlockSpec((1, L, tc), lambda j, b: (b, 0, j)),
        out_shape=jax.ShapeDtypeStruct((bsz, L, C), out_dtype),
        scratch_shapes=[pltpu.VMEM((n_ph, H, tc), jnp.float32), pltpu.VMEM((n_ph, H, tc), jnp.float32),
                        pltpu.VMEM((n_ph, H, tc), jnp.float32), pltpu.VMEM((n_ph, H, tc), jnp.bfloat16),
                        pltpu.VMEM((n_ph, H, tc), jnp.bfloat16), pltpu.VMEM((n_slab, L, LANES), jnp.float32)],
        compiler_params=pltpu.CompilerParams(
            dimension_semantics=("arbitrary", "arbitrary"), vmem_limit_bytes=VMEM_LIMIT_BYTES),
    )(*([z_arr] * n_slab), *([xn_arr] * n_slab), kr, ks, kn, bias, conv_w, conv_b, conv_w, conv_b, cos_t, sin_t)


def _polyphase_taps(kf, kb, n_ph):
    H = kf.shape[0] // n_ph
    ph = lambda a, p: a[p * H:(p + 1) * H]
    zero = jnp.zeros_like(kf[:1])
    plus, minus = [], []
    for r in range(-(n_ph - 1), n_ph):
        if r >= 0:
            plus.append(ph(kf, r))
        else:
            plus.append(jnp.concatenate([ph(kb, -r)[0:1], ph(kf, n_ph + r)[:-1]], axis=0))
        if r <= 0:
            minus.append(jnp.concatenate([zero, ph(kb, -r)[1:]], axis=0))
        else:
            minus.append(jnp.concatenate([zero, ph(kb, n_ph - r)[:-1]], axis=0))
    return jnp.stack(plus), jnp.stack(minus)


def hyena_long_convs(p_hy, conv_w, conv_b, kp, h_bias, tc, n_ph, row_len):
    L = p_hy.shape[1]
    C = h_bias.shape[1]
    cos_t, sin_t = dft_tables(L // n_ph)
    z = p_hy
    for o in range(h_bias.shape[0]):
        fwd = slice((2 * o) * C, (2 * o + 1) * C)
        bwd = slice((2 * o + 1) * C, (2 * o + 2) * C)
        plus, minus = _polyphase_taps(kp[:, fwd], kp[:, bwd], n_ph)
        kr, ks, kn = filter_spectrum(plus + minus, plus - minus, cos_t, sin_t, tc)
        last = o == h_bias.shape[0] - 1
        z = long_conv_gate(z, 0, o == 0, p_hy, o + 1, conv_w, conv_b, kr, ks, kn, h_bias[o][None, None, :],
                           cos_t, sin_t, tc, n_ph, row_len, jnp.bfloat16 if last else jnp.float32)
    return z


def _filter_kernel(band_ref, w1_ref, b1_ref, fr_ref, w2_ref, b2_ref, w3_ref, dl_ref, k_ref, *, seq_len, n_ph):
    hp = lax.Precision.HIGHEST
    f32 = jnp.float32
    tp = k_ref.shape[0]
    per_phase = seq_len // n_ph
    g = lax.broadcasted_iota(jnp.int32, (tp, 1), 0) + pl.program_id(0) * tp
    phase = g // per_phase
    pos = (n_ph * (g - phase * per_phase) + phase).astype(f32)
    t = pos / max(seq_len - 1, 1)
    ang = (2 * math.pi / seq_len) * pos * band_ref[...]
    lane = lax.broadcasted_iota(jnp.int32, (tp, LANES), 1)
    feats = jnp.where(lane == 0, t,
                      jnp.where(lane <= HYENA_BANDS, jnp.cos(ang),
                                jnp.where(lane <= 2 * HYENA_BANDS, -jnp.sin(ang), 0.0)))
    h = jnp.sin(fr_ref[...] * (jnp.dot(feats, w1_ref[...], precision=hp, preferred_element_type=f32) + b1_ref[...]))
    h = jnp.sin(fr_ref[...] * (jnp.dot(h, w2_ref[...], precision=hp, preferred_element_type=f32) + b2_ref[...]))
    window = jnp.exp(-t * dl_ref[...])
    c = dl_ref.shape[1]
    for j in range(w3_ref.shape[1] // c):
        cols = slice(j * c, (j + 1) * c)
        k_ref[:, cols] = jnp.dot(h, w3_ref[:, cols], precision=hp, preferred_element_type=f32) * window


def hyena_filters_polyphase(seq_len, f_w1, f_b1, f_freq, f_w2, f_b2, f_w3, d_hyena, tp, n_ph):
    f32 = jnp.float32
    fh = f_w1.shape[1]
    n_emb = 1 + 2 * HYENA_BANDS
    bands = jnp.linspace(1e-4, HYENA_BANDS - 1, HYENA_BANDS, dtype=f32)
    band_row = jnp.concatenate([jnp.zeros((1,), f32), bands, bands, jnp.zeros((LANES - n_emb,), f32)])[None, :]
    w1p = jnp.concatenate([f_w1, jnp.zeros((LANES - n_emb, fh), f32)], axis=0)
    deltas = jnp.abs(jnp.linspace(math.log(HYENA_TARGET) / HYENA_SLOW_DECAY,
                                  math.log(HYENA_TARGET) / HYENA_FAST_DECAY, d_hyena, dtype=f32))[None, :]
    n_out = f_w3.shape[1]
    full = lambda a: pl.BlockSpec(a.shape, lambda i: (0,) * a.ndim)
    args = (band_row, w1p, f_b1[None, :], f_freq[None, :], f_w2, f_b2[None, :], f_w3, deltas)
    return pl.pallas_call(
        functools.partial(_filter_kernel, seq_len=seq_len, n_ph=n_ph),
        grid=(seq_len // tp,),
        in_specs=[full(a) for a in args],
        out_specs=pl.BlockSpec((tp, n_out), lambda i: (i, 0)),
        out_shape=jax.ShapeDtypeStruct((seq_len, n_out), f32),
        compiler_params=pltpu.CompilerParams(dimension_semantics=("arbitrary",), vmem_limit_bytes=VMEM_LIMIT_BYTES),
    )(*args)


def _ada_kernel(c_ref, w_ref, b_ref, o_ref):
    cv = c_ref[...]
    s = cv * jax.nn.sigmoid(cv)
    o_ref[...] = jnp.dot(s, w_ref[...], precision=lax.Precision.HIGHEST,
                         preferred_element_type=jnp.float32) + b_ref[...]


def ada_modulation(c_rows, w_ada, b_ada, tn):
    rows, D = c_rows.shape
    N = w_ada.shape[1]
    return pl.pallas_call(
        _ada_kernel,
        grid=(N // tn,),
        in_specs=[pl.BlockSpec((rows, D), lambda j: (0, 0)),
                  pl.BlockSpec((D, tn), lambda j: (0, j)),
                  pl.BlockSpec((1, tn), lambda j: (0, j))],
        out_specs=pl.BlockSpec((rows, tn), lambda j: (0, j)),
        out_shape=jax.ShapeDtypeStruct((rows, N), jnp.float32),
        compiler_params=pltpu.CompilerParams(dimension_semantics=("arbitrary",), vmem_limit_bytes=VMEM_LIMIT_BYTES),
    )(c_rows, w_ada, b_ada[None, :])


def _ssd_kernel(xf_ref, df_ref, xb_ref, db_ref, dskip_ref, exp_ref, y_ref, h_ref, *, n_ctx_chunks):
    s = pl.program_id(1)
    n_steps = pl.num_programs(1)
    Q, G, R, P, N = SSD_CHUNK, SSD_GROUPS, SSD_HPG, SSD_HEAD_DIM, SSD_STATE
    GP = R * P
    bf = jnp.bfloat16

    @pl.when(s == 0)
    def _():
        h_ref[...] = jnp.zeros_like(h_ref)
        y_ref[...] = jnp.zeros_like(y_ref)

    row = lax.broadcasted_iota(jnp.int32, (Q, Q), 0)
    col = lax.broadcasted_iota(jnp.int32, (Q, Q), 1)
    lane_q = lax.broadcasted_iota(jnp.int32, (Q, LANES), 1)
    block_head = lax.broadcasted_iota(jnp.int32, (Q, LANES), 1) // P
    is_latent = s >= n_ctx_chunks
    n_lat = n_steps - n_ctx_chunks
    out_chunk = (jnp.clip(s - n_ctx_chunks, 0, n_lat - 1), jnp.clip(n_steps - 1 - s, 0, n_lat - 1))

    for bb, d in [(bb, d) for bb in range(y_ref.shape[0]) for d in range(2)]:
        x_ref, da_ref = ((xf_ref, df_ref), (xb_ref, db_ref))[d]
        mask = (row >= col) if d == 0 else (col >= row)
        tri = mask.astype(jnp.float32)
        da = da_ref[bb]
        cum = jnp.dot(tri, da, precision=lax.Precision.HIGHEST, preferred_element_type=jnp.float32)
        cum_t = cum.T
        edge = Q - 1 if d == 0 else 0
        dc = jnp.where(lane_q < 2 * SSD_HEADS, da, cum)
        dc_hi = dc.astype(bf)
        dc_r = dc - dc_hi.astype(jnp.float32)
        dc_mid = dc_r.astype(bf)
        dc_lo = (dc_r - dc_mid.astype(jnp.float32)).astype(bf)
        spread = ((jnp.dot(dc_hi, exp_ref[d], preferred_element_type=jnp.float32)
                   + jnp.dot(dc_mid, exp_ref[d], preferred_element_type=jnp.float32))
                  + jnp.dot(dc_lo, exp_ref[d], preferred_element_type=jnp.float32))
        blk = x_ref.at[bb]
        for g in range(G):
            xg = blk[:, g * GP:(g + 1) * GP]
            bg = blk[:, D_SSD + g * N:D_SSD + (g + 1) * N].astype(bf)
            cg = blk[:, D_SSD + G * N + g * N:D_SSD + G * N + (g + 1) * N].astype(bf)
            heads = [d * SSD_HEADS + g * R + r for r in range(R)]
            dtm = spread[:, (2 * g) * GP:(2 * g + 1) * GP]
            cumm = spread[:, (2 * g + 1) * GP:(2 * g + 2) * GP]
            totm = cumm[edge:edge + 1, :]
            xdt = xg * dtm
            hg = h_ref[bb, d, g * GP:(g + 1) * GP, :]

            gmat = lax.dot_general(cg, bg, (((1,), (1,)), ((), ())), preferred_element_type=jnp.float32)
            y_off = lax.dot_general(cg, hg.astype(bf), (((1,), (1,)), ((), ())),
                                    preferred_element_type=jnp.float32) * jnp.exp(cumm)
            if d == 0:
                y_off = y_off + dskip_ref[:, g * GP:(g + 1) * GP] * xg
            per_block = LANES // P
            parts = [jnp.zeros((Q, LANES), jnp.float32) for _ in range(R // per_block)]
            for r, h in enumerate(heads):
                a_col = cum[:, SSD_HEADS * 2 + h:SSD_HEADS * 2 + h + 1]
                a_row = cum_t[SSD_HEADS * 2 + h:SSD_HEADS * 2 + h + 1, :]
                decay = jnp.exp(jnp.where(mask, a_col - a_row, NEG_BIG))
                j = r // per_block
                xblk = xdt[:, j * LANES:(j + 1) * LANES]
                own = block_head == r % per_block
                parts[j] = parts[j] + jnp.dot((gmat * decay).astype(bf), jnp.where(own, xblk, 0.0).astype(bf),
                                              preferred_element_type=jnp.float32)
            y = jnp.where(is_latent, y_off + jnp.concatenate(parts, axis=-1), 0.0)
            rows = pl.ds(pl.multiple_of(out_chunk[d] * Q, Q), Q)
            y_ref[bb, rows, g * GP:(g + 1) * GP] += y

            xw = (xdt * jnp.exp(totm - cumm)).astype(bf)
            st = lax.dot_general(xw, bg, (((0,), (0,)), ((), ())), preferred_element_type=jnp.float32)
            for r, h in enumerate(heads):
                dec = jnp.exp(cum_t[SSD_HEADS * 2 + h:SSD_HEADS * 2 + h + 1, edge:edge + 1])
                rs = slice(g * GP + r * P, g * GP + (r + 1) * P)
                h_ref[bb, d, rs, :] = h_ref[bb, d, rs, :] * dec + st[r * P:(r + 1) * P, :]


def ssd_scan_bidir(xbc, dta, d_skip, n_ctx, lat_off, nb):
    bsz, lt, width = xbc.shape
    Q = SSD_CHUNK
    n_ctx_chunks = n_ctx // Q
    L = lt - lat_off
    n_lat = L // Q
    n_steps = n_ctx_chunks + n_lat
    lat0 = lat_off // Q

    n_h = 2 * SSD_HEADS
    gp = SSD_HPG * SSD_HEAD_DIM
    colv = jnp.arange(SSD_GROUPS * 2 * gp, dtype=jnp.int32)
    col_g, col_is_cum, col_r = colv // (2 * gp), (colv // gp) % 2, (colv % gp) // SSD_HEAD_DIM
    lane = jnp.arange(LANES, dtype=jnp.int32)[:, None]
    spread = jnp.stack([(lane == col_is_cum * n_h + d * SSD_HEADS + col_g * SSD_HPG + col_r)
                        for d in range(2)]).astype(jnp.bfloat16)

    def fwd_chunk(s):
        return jnp.where(s < n_ctx_chunks, s, s - n_ctx_chunks + lat0)

    def bwd_chunk(s):
        return jnp.where(s < n_ctx_chunks, n_ctx_chunks - 1 - s, n_steps - 1 - s + lat0)

    return pl.pallas_call(
        functools.partial(_ssd_kernel, n_ctx_chunks=n_ctx_chunks),
        grid=(bsz // nb, n_steps),
        in_specs=[
            pl.BlockSpec((nb, Q, width), lambda b, s: (b, fwd_chunk(s), 0)),
            pl.BlockSpec((nb, Q, LANES), lambda b, s: (b, fwd_chunk(s), 0)),
            pl.BlockSpec((nb, Q, width), lambda b, s: (b, bwd_chunk(s), 0)),
            pl.BlockSpec((nb, Q, LANES), lambda b, s: (b, bwd_chunk(s), 0)),
            pl.BlockSpec((1, D_SSD), lambda b, s: (0, 0)),
            pl.BlockSpec(spread.shape, lambda b, s: (0, 0, 0)),
        ],
        out_specs=pl.BlockSpec((nb, L, D_SSD), lambda b, s: (b, 0, 0)),
        out_shape=jax.ShapeDtypeStruct((bsz, L, D_SSD), jnp.float32),
        scratch_shapes=[pltpu.VMEM((nb, 2, SSD_GROUPS * SSD_HPG * SSD_HEAD_DIM, SSD_STATE), jnp.float32)],
        compiler_params=pltpu.CompilerParams(
            dimension_semantics=("arbitrary", "arbitrary"), vmem_limit_bytes=VMEM_LIMIT_BYTES),
    )(xbc, dta, xbc, dta, d_skip, spread)


def kernel(x, c, ctx, c_ctx, w_ada, b_ada, g_norm1, g_norm2, w_in, hy_conv_w, hy_conv_b, hy_f_w1, hy_f_b1, hy_f_freq, hy_f_w2, hy_f_b2, hy_f_w3, hy_bias, ssd_conv_w, ssd_conv_b, ssd_a_log, ssd_dt_bias, ssd_d, ssd_norm_g, w_out, w_group, b_group, w_expert, b_expert, w1, w3, w2, g_final):
    bsz, seq_len, _ = x.shape
    assert w_in.shape[0] == 1, "single-layer block: the context stream only supplies SSD states"
    l = 0
    rows_pad = -(bsz + 1) % SUBLANES
    c_rows = jnp.concatenate([c, c_ctx[None, :], jnp.zeros((rows_pad, D_MODEL), jnp.float32)], axis=0)
    mod_all = ada_modulation(c_rows, w_ada[l], b_ada[l], ADA_COLS)
    sh1, sc1, ga1, sh2, sc2, ga2 = jnp.split(mod_all[:bsz, None, :], 6, axis=-1)
    csh1, csc1 = mod_all[bsz, :D_MODEL], mod_all[bsz, D_MODEL:2 * D_MODEL]

    w_out_bf = w_out[l].astype(jnp.bfloat16)
    w_in_bf, w_dt_bf, dt_bias2, dt_mult = in_proj_params(w_in[l], ssd_a_log[l], ssd_dt_bias[l],
                                                         HY_COLS, D_SSD, D_XBC)

    ctx_len = ctx.shape[1]
    lat_off = -(-ctx_len // IN_PROJ_ROWS) * IN_PROJ_ROWS
    p_hy, z, xbc, dta = in_proj_fused(ctx, x, g_norm1[l][None, :], csh1[None, :], csc1[None, :], sh1, sc1,
                                      w_in_bf, w_dt_bf, HY_COLS, ssd_conv_w[l], ssd_conv_b[l][None, :],
                                      dt_bias2, dt_mult, GRID_W, ctx_len, IN_PROJ_ROWS, IN_PROJ_COLS)
    kp = hyena_filters_polyphase(seq_len, hy_f_w1[l], hy_f_b1[l], hy_f_freq[l], hy_f_w2[l], hy_f_b2[l],
                                 hy_f_w3[l], D_HYENA, FILTER_ROWS, HYENA_PHASES)
    y_hy = hyena_long_convs(p_hy, hy_conv_w[l], hy_conv_b[l][None, :], kp, hy_bias[l], HYENA_COLS, HYENA_PHASES,
                            GRID_W)
    y_scan = ssd_scan_bidir(xbc, dta, jnp.repeat(ssd_d[l], SSD_HEAD_DIM)[None, :], ctx_len, lat_off, SSD_BATCH)

    pad = LANES - MOE_GROUPS - N_EXPERTS
    w_router = jnp.concatenate([w_group[l], w_expert[l], jnp.zeros((D_MODEL, pad), jnp.float32)], axis=1)
    w_router_hi = w_router.astype(jnp.bfloat16)
    w_router_lo = (w_router - w_router_hi.astype(jnp.float32)).astype(jnp.bfloat16)
    w_router = jnp.stack([w_router_hi, w_router_lo])
    b_router = jnp.concatenate([b_group[l], b_expert[l], jnp.zeros((pad,), jnp.float32)])[None, :]
    x1, hn, route_i, route_f, counts = out_proj_router(
        y_hy, y_scan, z, 0, x, ga1, sc2, sh2, ssd_norm_g[l][None, :], g_norm2[l][None, :],
        w_out_bf, w_router, b_router, TOKEN_ROWS)
    n_tok = bsz * seq_len
    n_blocks = -(-n_tok * TOP_K // MOE_BLOCK) + N_EXPERTS
    dest, block_wsel, block_first, block_valid, block_src = moe_plan(route_i, counts, MOE_BLOCK, n_blocks)
    buf = moe_dispatch(hn.reshape(n_tok, D_MODEL), dest, block_valid, MOE_BLOCK, TOKEN_ROWS)
    yb = expert_blocks(buf, block_wsel, block_first, block_valid, block_src, w1[l], w3[l], w2[l], MOE_BLOCK)
    return moe_combine(x1, route_f, ga2, g_final[None, :], yb, dest, TOKEN_ROWS)
```

```python
import functools
import math

import jax
import jax.numpy as jnp
from jax import lax
from jax.experimental import pallas as pl
from jax.experimental.pallas import tpu as pltpu

D_MODEL = 1024
CTX_LEN = 256
GRID_W = 64
EPS = 1e-6
SHORT_CONV = 3

D_HYENA = D_MODEL // 2
HYENA_ORDER = 2
HYENA_BANDS = 8
HYENA_FAST_DECAY = 0.3
HYENA_SLOW_DECAY = 1.5
HYENA_TARGET = 1e-2
HYENA_PHASES = 4

D_SSD = D_MODEL // 2
SSD_HEAD_DIM = 64
SSD_HEADS = D_SSD // SSD_HEAD_DIM
SSD_GROUPS = 2
SSD_HPG = SSD_HEADS // SSD_GROUPS
SSD_STATE = 128
SSD_CHUNK = 128

D_XBC = D_SSD + 2 * SSD_GROUPS * SSD_STATE
HY_COLS = (HYENA_ORDER + 1) * D_HYENA
D_IN = HY_COLS + D_SSD + D_XBC + 2 * SSD_HEADS
LANES = 128
SUBLANES = 8
D_IN_PAD = -(-D_IN // LANES) * LANES

MOE_GROUPS = 8
EXPERTS_PER_GROUP = 8
N_EXPERTS = MOE_GROUPS * EXPERTS_PER_GROUP
TOP_K = 2
D_EXPERT = 512
MOE_BLOCK = 256
ROUTE_COLS = 8

IN_PROJ_ROWS = 512
IN_PROJ_COLS = 512
TOKEN_ROWS = 256
HYENA_COLS = 256
FILTER_ROWS = 256
ADA_COLS = 512
SSD_BATCH = 4

VMEM_LIMIT_BYTES = 56 * 1024 * 1024
NEG_BIG = -1e30


def _conv3_rows(p, w_ref, b_ref, cols, has_prev, has_next):
    n = p.shape[0]
    prev = jnp.where(has_prev, pltpu.roll(p, 1, 0), 0.0)
    nxt = jnp.where(has_next, pltpu.roll(p, n - 1, 0), 0.0)
    return b_ref[:, cols] + w_ref[0:1, cols] * prev + w_ref[1:2, cols] * p + w_ref[2:3, cols] * nxt


def _in_proj_kernel(ctx_ref, x_ref, g_ref, csh_ref, csc_ref, sh_ref, sc_ref, w_ref, wdt_ref,
                    sw_ref, sb_ref, dtb_ref, dtm_ref, u_ref, z_ref, xbc_ref, dta_ref, h_ref,
                    *, n_ctx_steps, row_len, ctx_row_len, hy_cols, d_ssd, d_xbc, tn):
    i = pl.program_id(1)
    is_ctx = i < n_ctx_steps
    tm = x_ref.shape[1]
    ctx_t = ctx_ref[0]
    if ctx_t.shape[0] < tm:
        ctx_t = jnp.concatenate([ctx_t, jnp.zeros((tm - ctx_t.shape[0], ctx_t.shape[1]), ctx_t.dtype)], axis=0)
    xin = jnp.where(is_ctx, ctx_t, x_ref[0])
    shift = jnp.where(is_ctx, csh_ref[...], sh_ref[0])
    scale = jnp.where(is_ctx, csc_ref[...], sc_ref[0])
    y = xin * lax.rsqrt(jnp.mean(xin * xin, axis=-1, keepdims=True) + EPS) * g_ref[...]
    h_ref[...] = (y * (1.0 + scale) + shift).astype(jnp.bfloat16)

    pos = lax.broadcasted_iota(jnp.int32, (tm, 1), 0) + jnp.where(is_ctx, i, i - n_ctx_steps) * tm
    in_row = jnp.where(is_ctx, pos % ctx_row_len, pos % row_len)
    has_prev = in_row != 0
    has_next = in_row != jnp.where(is_ctx, ctx_row_len - 1, row_len - 1)

    for c0 in range(0, hy_cols, tn):
        cols = slice(c0, c0 + tn)
        u_ref[0, :, cols] = jnp.dot(h_ref[...], w_ref[:, cols], preferred_element_type=jnp.float32)
    z_ref[0] = jnp.dot(h_ref[...], w_ref[:, hy_cols:hy_cols + d_ssd], preferred_element_type=jnp.float32)

    for c0 in range(0, d_xbc, tn):
        cols = slice(c0, c0 + tn)
        wc = slice(hy_cols + d_ssd + c0, hy_cols + d_ssd + c0 + tn)
        p = jnp.dot(h_ref[...], w_ref[:, wc], preferred_element_type=jnp.float32)
        v = _conv3_rows(p, sw_ref, sb_ref, cols, has_prev, has_next)
        xbc_ref[0, :, cols] = v * jax.nn.sigmoid(v)
    pd = jnp.dot(h_ref[...], wdt_ref[...], preferred_element_type=jnp.float32) + dtb_ref[...]
    sp = jnp.maximum(pd, 0.0) + jnp.log(1.0 + jnp.exp(-jnp.abs(pd)))
    dta_ref[0] = sp * dtm_ref[...]


def in_proj_fused(ctx, x, g1, csh, csc, sh, sc, w_bf, wdt_bf, hy_cols, ssd_w, ssd_b, dt_bias2, dt_mult,
                  row_len, ctx_row_len, tm, tn):
    bsz, L, D = x.shape
    lc = ctx.shape[1]
    d_xbc = ssd_w.shape[1]
    d_ssd = w_bf.shape[1] - hy_cols - d_xbc
    n_ctx_steps = -(-lc // tm)
    ctx_rows = min(lc, tm)
    lc = n_ctx_steps * tm
    n_steps = n_ctx_steps + L // tm
    lat = lambda b, i: (b, jnp.maximum(i - n_ctx_steps, 0), 0)
    allt = lambda b, i: (b, i, 0)
    const2 = lambda b, i: (0, 0)
    per_b = pl.BlockSpec((1, 1, D), lambda b, i: (b, 0, 0))
    kern = functools.partial(_in_proj_kernel, n_ctx_steps=n_ctx_steps, row_len=row_len, ctx_row_len=ctx_row_len,
                             hy_cols=hy_cols, d_ssd=d_ssd, d_xbc=d_xbc, tn=tn)
    return pl.pallas_call(
        kern,
        grid=(bsz, n_steps),
        in_specs=[
            pl.BlockSpec((1, ctx_rows, D), lambda b, i: (b, jnp.minimum(i, n_ctx_steps - 1), 0)),
            pl.BlockSpec((1, tm, D), lat),
            pl.BlockSpec((1, D), const2),
            pl.BlockSpec((1, D), const2),
            pl.BlockSpec((1, D), const2),
            per_b, per_b,
            pl.BlockSpec(w_bf.shape, const2),
            pl.BlockSpec(wdt_bf.shape, const2),
            pl.BlockSpec(ssd_w.shape, const2),
            pl.BlockSpec(ssd_b.shape, const2),
            pl.BlockSpec((1, LANES), const2),
            pl.BlockSpec((1, LANES), const2),
        ],
        out_specs=[
            pl.BlockSpec((1, tm, hy_cols), lat),
            pl.BlockSpec((1, tm, d_ssd), lat),
            pl.BlockSpec((1, tm, d_xbc), allt),
            pl.BlockSpec((1, tm, LANES), allt),
        ],
        out_shape=[
            jax.ShapeDtypeStruct((bsz, L, hy_cols), jnp.float32),
            jax.ShapeDtypeStruct((bsz, L, d_ssd), jnp.float32),
            jax.ShapeDtypeStruct((bsz, lc + L, d_xbc), jnp.float32),
            jax.ShapeDtypeStruct((bsz, lc + L, LANES), jnp.float32),
        ],
        scratch_shapes=[pltpu.VMEM((tm, D), jnp.bfloat16)],
        compiler_params=pltpu.CompilerParams(
            dimension_semantics=("arbitrary", "arbitrary"), vmem_limit_bytes=VMEM_LIMIT_BYTES),
    )(ctx, x, g1, csh, csc, sh, sc, w_bf, wdt_bf, ssd_w, ssd_b, dt_bias2, dt_mult)


def in_proj_params(w_in, a_log, dt_bias, hy_cols, d_ssd, d_xbc):
    n_h = 2 * SSD_HEADS
    main = hy_cols + d_ssd + d_xbc
    w_dt = w_in[:, main:main + n_h]
    pad = jnp.zeros((w_in.shape[0], LANES - 2 * n_h), w_in.dtype)
    wdt = jnp.concatenate([w_dt, w_dt, pad], axis=1).astype(jnp.bfloat16)
    zpad = jnp.zeros((LANES - 2 * n_h,), jnp.float32)
    bias2 = jnp.concatenate([dt_bias.reshape(n_h), dt_bias.reshape(n_h), zpad])[None, :]
    mult = jnp.concatenate([jnp.ones((n_h,), jnp.float32), -jnp.exp(a_log).reshape(n_h), zpad])[None, :]
    return w_in[:, :main].astype(jnp.bfloat16), wdt, bias2, mult


def _out_router_kernel(yh_ref, ys_ref, z_ref, x_ref, ga_ref, sc_ref, sh_ref, ng_ref, g2_ref, wo_ref, wr_ref, br_ref,
                       x1_ref, hn_ref, ri_ref, rf_ref, cnt_ref, carry_ref):
    first = jnp.logical_and(pl.program_id(0) == 0, pl.program_id(1) == 0)

    @pl.when(first)
    def _():
        carry_ref[...] = jnp.zeros_like(carry_ref)

    bf = jnp.bfloat16
    tm = x_ref.shape[1]
    dh = yh_ref.shape[2]
    z = z_ref[0]
    ys = ys_ref[0] * (z * jax.nn.sigmoid(z))
    gw = ys.shape[1] // SSD_GROUPS
    acc = jnp.dot(yh_ref[0].astype(bf), wo_ref[0:dh, :], preferred_element_type=jnp.float32)
    for g in range(SSD_GROUPS):
        yg = ys[:, g * gw:(g + 1) * gw]
        yg = yg * lax.rsqrt(jnp.mean(yg * yg, axis=-1, keepdims=True) + EPS) * ng_ref[:, g * gw:(g + 1) * gw]
        acc += jnp.dot(yg.astype(bf), wo_ref[dh + g * gw:dh + (g + 1) * gw, :], preferred_element_type=jnp.float32)
    x1 = x_ref[0] + ga_ref[0] * acc
    x1_ref[0] = x1
    hn = x1 * lax.rsqrt(jnp.mean(x1 * x1, axis=-1, keepdims=True) + EPS) * g2_ref[...]
    hn = hn * (1.0 + sc_ref[0]) + sh_ref[0]
    hn_ref[0] = hn

    hn_hi = hn.astype(bf)
    hn_lo = (hn - hn_hi.astype(jnp.float32)).astype(bf)
    logits = (jnp.dot(hn_hi, wr_ref[0], preferred_element_type=jnp.float32)
              + jnp.dot(hn_lo, wr_ref[0], preferred_element_type=jnp.float32)
              + jnp.dot(hn_hi, wr_ref[1], preferred_element_type=jnp.float32)) + br_ref[...]
    lane = lax.broadcasted_iota(jnp.int32, (tm, LANES), 1)
    lane_f = lane.astype(jnp.float32)
    ninf = jnp.float32(-jnp.inf)
    big = jnp.float32(1e9)
    gl = jnp.where(lane < MOE_GROUPS, logits, ninf)
    gmax = jnp.max(gl, axis=-1, keepdims=True)
    p_group = 1.0 / jnp.sum(jnp.exp(gl - gmax), axis=-1, keepdims=True)
    g_sel = jnp.min(jnp.where(gl == gmax, lane_f, big), axis=-1, keepdims=True)
    e_lane = lane - MOE_GROUPS
    in_grp = jnp.logical_and(e_lane >= 0, (e_lane // EXPERTS_PER_GROUP).astype(jnp.float32) == g_sel)
    el = jnp.where(in_grp, logits, ninf)
    m1 = jnp.max(el, axis=-1, keepdims=True)
    i1 = jnp.min(jnp.where(el == m1, lane_f, big), axis=-1, keepdims=True)
    el2 = jnp.where(lane_f == i1, ninf, el)
    m2 = jnp.max(el2, axis=-1, keepdims=True)
    i2 = jnp.min(jnp.where(el2 == m2, lane_f, big), axis=-1, keepdims=True)
    t = jnp.exp(m2 - m1)
    w1 = 1.0 / (1.0 + t)
    gate1 = w1 * p_group
    gate2 = (t * w1) * p_group
    e1 = i1 - MOE_GROUPS
    e2 = i2 - MOE_GROUPS
    el_f = e_lane.astype(jnp.float32)
    oh1 = el_f == e1
    oh2 = el_f == e2
    oh = jnp.logical_or(oh1, oh2).astype(bf)
    r_i = lax.broadcasted_iota(jnp.int32, (tm, tm), 0)
    c_i = lax.broadcasted_iota(jnp.int32, (tm, tm), 1)
    before = jnp.dot((c_i < r_i).astype(bf), oh, preferred_element_type=jnp.float32) + carry_ref[...]
    rank1 = jnp.sum(jnp.where(oh1, before, 0.0), axis=-1, keepdims=True)
    rank2 = jnp.sum(jnp.where(oh2, before, 0.0), axis=-1, keepdims=True)
    carry_ref[...] += jnp.sum(oh.astype(jnp.float32), axis=0, keepdims=True)
    cnt_ref[...] = carry_ref[...]

    rec = jnp.where(lane == 0, e1, jnp.where(lane == 1, e2, jnp.where(lane == 2, rank1,
                                                                      jnp.where(lane == 3, rank2, 0.0))))
    ri_ref[0] = rec.T[0:ROUTE_COLS, :].astype(jnp.int32)
    col = lax.broadcasted_iota(jnp.int32, (tm, ROUTE_COLS), 1)
    rf_ref[0] = jnp.where(col == 0, gate1, gate2)


def out_proj_router(y_hy, y_scan, px, z_col, x, ga1, sc2, sh2, norm_g, g2, w_out_bf, w_router, b_router, tm):
    bsz, L, D = x.shape
    dh = y_hy.shape[-1]
    ds = y_scan.shape[-1]
    tok = lambda b, i: (b, i, 0)
    per_b = pl.BlockSpec((1, 1, D), lambda b, i: (b, 0, 0))
    const2 = lambda b, i: (0, 0)
    return pl.pallas_call(
        _out_router_kernel,
        grid=(bsz, L // tm),
        in_specs=[
            pl.BlockSpec((1, tm, dh), tok),
            pl.BlockSpec((1, tm, ds), tok),
            pl.BlockSpec((1, tm, ds), lambda b, i: (b, i, z_col)),
            pl.BlockSpec((1, tm, D), tok),
            per_b, per_b, per_b,
            pl.BlockSpec((1, ds), const2),
            pl.BlockSpec((1, D), const2),
            pl.BlockSpec((dh + ds, D), const2),
            pl.BlockSpec((2, D, LANES), lambda b, i: (0, 0, 0)),
            pl.BlockSpec((1, LANES), const2),
        ],
        out_specs=[
            pl.BlockSpec((1, tm, D), tok),
            pl.BlockSpec((1, tm, D), tok),
            pl.BlockSpec((1, ROUTE_COLS, tm), lambda b, i: (b, 0, i)),
            pl.BlockSpec((1, tm, ROUTE_COLS), tok),
            pl.BlockSpec((1, LANES), const2),
        ],
        out_shape=[
            jax.ShapeDtypeStruct((bsz, L, D), jnp.float32),
            jax.ShapeDtypeStruct((bsz, L, D), jnp.float32),
            jax.ShapeDtypeStruct((bsz, ROUTE_COLS, L), jnp.int32),
            jax.ShapeDtypeStruct((bsz, L, ROUTE_COLS), jnp.float32),
            jax.ShapeDtypeStruct((1, LANES), jnp.float32),
        ],
        scratch_shapes=[pltpu.VMEM((1, LANES), jnp.float32)],
        compiler_params=pltpu.CompilerParams(
            dimension_semantics=("arbitrary", "arbitrary"), vmem_limit_bytes=VMEM_LIMIT_BYTES),
    )(y_hy, y_scan, px, x, ga1, sc2, sh2, norm_g, g2, w_out_bf, w_router, b_router)


def _row_copy(src_hbm, src_row, dst_ref, dst_row, sem):
    return pltpu.make_async_copy(src_hbm.at[pl.ds(src_row, 1), :], dst_ref.at[pl.ds(dst_row, 1), :], sem)


def _dispatch_kernel(dest_ref, valid_ref, hn_ref, buf_hbm, zeros, sem, zsem):
    step = pl.program_id(0)
    tm = hn_ref.shape[0]
    n_tok = pl.num_programs(0) * tm
    blk = zeros.shape[0]
    n_blocks = buf_hbm.shape[0] // blk

    def zero_copy(i):
        return pltpu.make_async_copy(zeros, buf_hbm.at[pl.ds(pl.multiple_of(i * blk, blk), blk), :], zsem)

    def zfill(i, carry):
        @pl.when(valid_ref[i] < blk)
        def _():
            zero_copy(i).start()
        return carry

    def zwait(i, carry):
        @pl.when(valid_ref[i] < blk)
        def _():
            zero_copy(i).wait()
        return carry

    @pl.when(step == 0)
    def _():
        zeros[...] = jnp.zeros_like(zeros)
        lax.fori_loop(0, n_blocks, zfill, 0)
        lax.fori_loop(0, n_blocks, zwait, 0)

    def body(j, carry):
        t = step * tm + j
        _row_copy(hn_ref, j, buf_hbm, dest_ref[t], sem).start()
        _row_copy(hn_ref, j, buf_hbm, dest_ref[n_tok + t], sem).start()
        return carry

    lax.fori_loop(0, tm, body, 0, unroll=8)
    for _ in range(2):
        pltpu.make_async_copy(hn_ref, buf_hbm.at[pl.ds(0, tm), :], sem).wait()


def moe_dispatch(hn, dest, block_valid, blk, tm):
    T, D = hn.shape
    n_rows = block_valid.shape[0] * blk
    grid_spec = pltpu.PrefetchScalarGridSpec(
        num_scalar_prefetch=2,
        grid=(T // tm,),
        in_specs=[pl.BlockSpec((tm, D), lambda i, d, v: (i, 0))],
        out_specs=pl.BlockSpec(memory_space=pl.ANY),
        scratch_shapes=[pltpu.VMEM((blk, D), hn.dtype), pltpu.SemaphoreType.DMA(()),
                        pltpu.SemaphoreType.DMA(())],
    )
    return pl.pallas_call(
        _dispatch_kernel,
        grid_spec=grid_spec,
        out_shape=jax.ShapeDtypeStruct((n_rows, D), hn.dtype),
        compiler_params=pltpu.CompilerParams(dimension_semantics=("arbitrary",), has_side_effects=True),
    )(dest, block_valid, hn)


def _expert_kernel(wsel_ref, first_ref, valid_ref, src_ref, x_ref, w1_ref, w3_ref, w2_ref, o_ref, w1b, w3b, w2b):
    i = pl.program_id(0)
    del wsel_ref, src_ref
    bf = jnp.bfloat16

    @pl.when(first_ref[i] == 1)
    def _():
        w1b[...] = w1_ref[0].astype(bf)
        w3b[...] = w3_ref[0].astype(bf)
        w2b[...] = w2_ref[0].astype(bf)

    valid = valid_ref[i]

    @pl.when(valid > 0)
    def _():
        xb = x_ref[...].astype(bf)
        a = jnp.dot(xb, w1b[...], preferred_element_type=jnp.float32)
        b = jnp.dot(xb, w3b[...], preferred_element_type=jnp.float32)
        h = (a * jax.nn.sigmoid(a)) * b
        o_ref[...] = jnp.dot(h.astype(bf), w2b[...], preferred_element_type=jnp.float32)

    @pl.when(valid <= 0)
    def _():
        o_ref[...] = jnp.zeros_like(o_ref)


def expert_blocks(buf, block_wsel, block_first, block_valid, block_src, w1, w3, w2, blk):
    rows, D = buf.shape
    n_blocks = rows // blk
    E, _, F = w1.shape
    grid_spec = pltpu.PrefetchScalarGridSpec(
        num_scalar_prefetch=4,
        grid=(n_blocks,),
        in_specs=[
            pl.BlockSpec((blk, D), lambda i, ws, fi, va, src: (src[i], 0)),
            pl.BlockSpec((1, D, F), lambda i, ws, fi, va, src: (ws[i], 0, 0)),
            pl.BlockSpec((1, D, F), lambda i, ws, fi, va, src: (ws[i], 0, 0)),
            pl.BlockSpec((1, F, D), lambda i, ws, fi, va, src: (ws[i], 0, 0)),
        ],
        out_specs=pl.BlockSpec((blk, D), lambda i, ws, fi, va, src: (i, 0)),
        scratch_shapes=[pltpu.VMEM((D, F), jnp.bfloat16), pltpu.VMEM((D, F), jnp.bfloat16),
                        pltpu.VMEM((F, D), jnp.bfloat16)],
    )
    return pl.pallas_call(
        _expert_kernel,
        grid_spec=grid_spec,
        out_shape=jax.ShapeDtypeStruct((rows, D), jnp.float32),
        compiler_params=pltpu.CompilerParams(
            dimension_semantics=("arbitrary",), vmem_limit_bytes=VMEM_LIMIT_BYTES),
    )(block_wsel, block_first, block_valid, block_src, buf, w1, w3, w2)


def _combine_kernel(dest_ref, x1_ref, rf_ref, ga_ref, gf_ref, yb_hbm, o_ref, ybuf, sem):
    b = pl.program_id(0)
    i = pl.program_id(1)
    n_i = pl.num_programs(1)
    tm = x1_ref.shape[1]
    step = b * n_i + i
    n_steps = pl.num_programs(0) * n_i
    slot = step % 2

    def issue(step_, slot_):
        def body(j, carry):
            t = step_ * tm + j
            _row_copy(yb_hbm, dest_ref[t], ybuf.at[slot_, 0], j, sem.at[slot_]).start()
            _row_copy(yb_hbm, dest_ref[n_steps * tm + t], ybuf.at[slot_, 1], j, sem.at[slot_]).start()
            return carry
        lax.fori_loop(0, tm, body, 0, unroll=8)

    @pl.when(step == 0)
    def _():
        issue(0, 0)

    @pl.when(step + 1 < n_steps)
    def _():
        issue(step + 1, 1 - slot)

    pltpu.make_async_copy(yb_hbm.at[pl.ds(0, tm), :], ybuf.at[slot, 0], sem.at[slot]).wait()
    pltpu.make_async_copy(yb_hbm.at[pl.ds(0, tm), :], ybuf.at[slot, 1], sem.at[slot]).wait()
    rf = rf_ref[0]
    y = rf[:, 0:1] * ybuf[slot, 0] + rf[:, 1:2] * ybuf[slot, 1]
    x2 = x1_ref[0] + ga_ref[0] * y
    o_ref[0] = x2 * lax.rsqrt(jnp.mean(x2 * x2, axis=-1, keepdims=True) + EPS) * gf_ref[...]


def moe_combine(x1, route_f, ga2, g_final, yb, dest, tm):
    bsz, L, D = x1.shape
    grid_spec = pltpu.PrefetchScalarGridSpec(
        num_scalar_prefetch=1,
        grid=(bsz, L // tm),
        in_specs=[
            pl.BlockSpec((1, tm, D), lambda b, i, d: (b, i, 0)),
            pl.BlockSpec((1, tm, ROUTE_COLS), lambda b, i, d: (b, i, 0)),
            pl.BlockSpec((1, 1, D), lambda b, i, d: (b, 0, 0)),
            pl.BlockSpec((1, D), lambda b, i, d: (0, 0)),
            pl.BlockSpec(memory_space=pl.ANY),
        ],
        out_specs=pl.BlockSpec((1, tm, D), lambda b, i, d: (b, i, 0)),
        scratch_shapes=[pltpu.VMEM((2, 2, tm, D), jnp.float32), pltpu.SemaphoreType.DMA((2,))],
    )
    return pl.pallas_call(
        _combine_kernel,
        grid_spec=grid_spec,
        out_shape=jax.ShapeDtypeStruct((bsz, L, D), jnp.float32),
        compiler_params=pltpu.CompilerParams(
            dimension_semantics=("arbitrary", "arbitrary"), vmem_limit_bytes=VMEM_LIMIT_BYTES),
    )(dest, x1, route_f, ga2, g_final, yb)


def moe_plan(route_i, counts, blk, n_blocks):
    cnt = counts[0, MOE_GROUPS:MOE_GROUPS + N_EXPERTS].astype(jnp.int32)
    padded = (cnt + blk - 1) // blk * blk
    ends = jnp.cumsum(padded)
    starts = ends - padded
    experts = jnp.arange(N_EXPERTS, dtype=jnp.int32)
    dest = jnp.concatenate([
        (jnp.sum(jnp.where(route_i[:, k, :, None] == experts, starts, 0), axis=-1) + route_i[:, 2 + k]).reshape(-1)
        for k in range(TOP_K)])
    first_row = jnp.arange(n_blocks, dtype=jnp.int32) * blk
    block_eid = jnp.minimum(jnp.sum((ends[None, :] <= first_row[:, None]).astype(jnp.int32), axis=1), N_EXPERTS - 1)
    block_valid = jnp.clip(cnt[block_eid] - (first_row - starts[block_eid]), 0, blk).astype(jnp.int32)
    block_first = jnp.concatenate([jnp.ones((1,), jnp.int32),
                                   (block_eid[1:] != block_eid[:-1]).astype(jnp.int32)])
    idx = jnp.arange(n_blocks, dtype=jnp.int32)
    next_first = lax.cummin(jnp.where(block_first == 1, idx, n_blocks), axis=0, reverse=True)
    block_wsel = jnp.where(next_first < n_blocks, block_eid[jnp.minimum(next_first, n_blocks - 1)], block_eid)
    n_used = jnp.sum((block_valid > 0).astype(jnp.int32))
    block_src = jnp.minimum(idx, jnp.maximum(n_used - 1, 0))
    return dest, block_wsel, block_first, block_valid, block_src


def dft_tables(L):
    n = 2 * L
    f = lax.broadcasted_iota(jnp.int32, (L, L), 0)
    t = lax.broadcasted_iota(jnp.int32, (L, L), 1)
    ang = ((f * t) % n).astype(jnp.float32) * (2.0 * math.pi / n)
    return jnp.cos(ang).astype(jnp.bfloat16), jnp.sin(ang).astype(jnp.bfloat16)


def _alt_sign(L):
    t = lax.broadcasted_iota(jnp.int32, (L, 1), 0)
    return (1 - 2 * (t & 1)).astype(jnp.float32)


def _spectrum_kernel(a_ref, b_ref, c_ref, s_ref, kr_ref, ks_ref, kn_ref):
    L = a_ref.shape[1]
    a = a_ref[0]
    row = lax.broadcasted_iota(jnp.int32, (L, 1), 0)
    scale = jnp.where(row == 0, 0.5 / L, 1.0 / L)
    kr_ref[0] = scale * jnp.dot(c_ref[...], a.astype(jnp.bfloat16), preferred_element_type=jnp.float32)
    ks_ref[0] = scale * jnp.dot(s_ref[...], b_ref[0].astype(jnp.bfloat16), preferred_element_type=jnp.float32)
    kn_ref[0] = jnp.sum(a * _alt_sign(L), axis=0, keepdims=True) * (0.5 / L)


def filter_spectrum(a, b, cos_t, sin_t, tc):
    n, L, C = a.shape
    blk = pl.BlockSpec((1, L, tc), lambda o, j: (o, 0, j))
    tab = pl.BlockSpec((L, L), lambda o, j: (0, 0))
    return pl.pallas_call(
        _spectrum_kernel,
        grid=(n, C // tc),
        in_specs=[blk, blk, tab, tab],
        out_specs=[blk, blk, pl.BlockSpec((1, 1, tc), lambda o, j: (o, 0, j))],
        out_shape=[jax.ShapeDtypeStruct((n, L, C), jnp.float32)] * 2 + [jax.ShapeDtypeStruct((n, 1, C), jnp.float32)],
        compiler_params=pltpu.CompilerParams(
            dimension_semantics=("arbitrary", "arbitrary"), vmem_limit_bytes=VMEM_LIMIT_BYTES),
    )(a, b, cos_t, sin_t)


def _phase_conv3(raw, w_ref, b_ref, rows_per_phase):
    n_ph = len(raw)
    h = raw[0].shape[0]
    j = lax.broadcasted_iota(jnp.int32, (h, 1), 0) % rows_per_phase
    prev0 = jnp.where(j != 0, pltpu.roll(raw[n_ph - 1], 1, 0), 0.0)
    next_last = jnp.where(j != rows_per_phase - 1, pltpu.roll(raw[0], h - 1, 0), 0.0)
    out = []
    for p in range(n_ph):
        prev = raw[p - 1] if p > 0 else prev0
        nxt = raw[p + 1] if p < n_ph - 1 else next_last
        out.append(b_ref[...] + w_ref[0:1, :] * prev + w_ref[1:2, :] * raw[p] + w_ref[2:3, :] * nxt)
    return out


def _long_conv_kernel(*refs, n_ph, n_slab, conv_z, rows_per_phase):
    z_refs = refs[:n_slab]
    xn_refs = refs[n_slab:2 * n_slab]
    (kr_ref, ks_ref, kn_ref, bias_ref, cwz_ref, cbz_ref, cwx_ref, cbx_ref, c_ref, s_ref,
     o_ref, acc_ref, zr_ref, zs_ref, yr_ref, ys_ref, stage_ref) = refs[2 * n_slab:]
    H = z_refs[0].shape[1] // n_ph
    f32 = jnp.float32
    bf = jnp.bfloat16
    sign = _alt_sign(H)

    def phases(slab_refs):
        return [jnp.concatenate([r[0, pl.ds(p, H, stride=n_ph), :] for r in slab_refs], axis=1)
                for p in range(n_ph)]

    z_ph = phases(z_refs)
    if conv_z:
        z_ph = _phase_conv3(z_ph, cwz_ref, cbz_ref, rows_per_phase)
    for q in range(n_ph):
        zb = z_ph[q].astype(bf)
        zr_ref[q] = jnp.dot(c_ref[...], zb, preferred_element_type=f32)
        zs_ref[q] = jnp.dot(s_ref[...], zb, preferred_element_type=f32)
    z_nyq = [jnp.sum(z * sign, axis=0, keepdims=True) for z in z_ph]
    for p in range(n_ph):
        nyq = sum(z_nyq[q] * kn_ref[p - q + n_ph - 1] for q in range(n_ph))
        acc_ref[p] = z_ph[p] * bias_ref[0] + sign * nyq
        yr = 0.0
        ys = 0.0
        for q in range(n_ph):
            slot = p - q + n_ph - 1
            yr = yr + zr_ref[q] * kr_ref[slot] - zs_ref[q] * ks_ref[slot]
            ys = ys + zr_ref[q] * ks_ref[slot] + zs_ref[q] * kr_ref[slot]
        yr_ref[p] = yr.astype(bf)
        ys_ref[p] = ys.astype(bf)
    for p in range(n_ph):
        acc_ref[p] += (jnp.dot(c_ref[...], yr_ref[p], preferred_element_type=f32)
                       + jnp.dot(s_ref[...], ys_ref[p], preferred_element_type=f32))
    x_ph = _phase_conv3(phases(xn_refs), cwx_ref, cbx_ref, rows_per_phase)
    for p in range(n_ph):
        out_p = x_ph[p] * acc_ref[p]
        for sl in range(n_slab):
            stage_ref[sl, pl.ds(p, H, stride=n_ph), :] = out_p[:, sl * LANES:(sl + 1) * LANES]
    for sl in range(n_slab):
        o_ref[0, :, sl * LANES:(sl + 1) * LANES] = stage_ref[sl].astype(o_ref.dtype)


def long_conv_gate(z_arr, z_col, conv_z, xn_arr, xn_col, conv_w, conv_b, kr, ks, kn, bias, cos_t, sin_t,
                   tc, n_ph, row_len, out_dtype):
    bsz, L, _ = z_arr.shape
    H = L // n_ph
    C = kr.shape[-1]
    nj = C // tc
    n_slab = tc // LANES
    n_f = 2 * n_ph - 1
    tab = pl.BlockSpec((H, H), lambda j, b: (0, 0), pipeline_mode=pl.Buffered(1))
    spec = pl.BlockSpec((n_f, H, tc), lambda j, b: (0, 0, j), pipeline_mode=pl.Buffered(1))
    nyq = pl.BlockSpec((n_f, 1, tc), lambda j, b: (0, 0, j))
    vec = pl.BlockSpec((1, 1, tc), lambda j, b: (0, 0, j))

    def slabs(col):
        return [pl.BlockSpec((1, L, LANES),
                             functools.partial(lambda j, b, sl: (b, 0, (col * nj + j) * n_slab + sl), sl=sl))
                for sl in range(n_slab)]

    def conv_specs(col):
        return [pl.BlockSpec((3, tc), lambda j, b: (0, col * nj + j)),
                pl.BlockSpec((1, tc), lambda j, b: (0, col * nj + j))]

    zc = z_col if conv_z else 0
    kern = functools.partial(_long_conv_kernel, n_ph=n_ph, n_slab=n_slab, conv_z=conv_z,
                             rows_per_phase=row_len // n_ph)
    return pl.pallas_call(
        kern,
        grid=(nj, bsz),
        in_specs=(slabs(z_col) + slabs(xn_col) + [spec, spec, nyq, vec] + conv_specs(zc) + conv_specs(xn_col)
                  + [tab, tab]),
        out_specs=pl.BlockSpec((1, L, tc), lambda j, b: (b, 0, j)),
        out_shape=jax.ShapeDtypeStruct((bsz, L, C), out_dtype),
        scratch_shapes=[pltpu.VMEM((n_ph, H, tc), jnp.float32), pltpu.VMEM((n_ph, H, tc), jnp.float32),
                        pltpu.VMEM((n_ph, H, tc), jnp.float32), pltpu.VMEM((n_ph, H, tc), jnp.bfloat16),
                        pltpu.VMEM((n_ph, H, tc), jnp.bfloat16), pltpu.VMEM((n_slab, L, LANES), jnp.float32)],
        compiler_params=pltpu.CompilerParams(
            dimension_semantics=("arbitrary", "arbitrary"), vmem_limit_bytes=VMEM_LIMIT_BYTES),
    )(*([z_arr] * n_slab), *([xn_arr] * n_slab), kr, ks, kn, bias, conv_w, conv_b, conv_w, conv_b, cos_t, sin_t)


def _polyphase_taps(kf, kb, n_ph):
    H = kf.shape[0] // n_ph
    ph = lambda a, p: a[p * H:(p + 1) * H]
    zero = jnp.zeros_like(kf[:1])
    plus, minus = [], []
    for r in range(-(n_ph - 1), n_ph):
        if r >= 0:
            plus.append(ph(kf, r))
        else:
            plus.append(jnp.concatenate([ph(kb, -r)[0:1], ph(kf, n_ph + r)[:-1]], axis=0))
        if r <= 0:
            minus.append(jnp.concatenate([zero, ph(kb, -r)[1:]], axis=0))
        else:
            minus.append(jnp.concatenate([zero, ph(kb, n_ph - r)[:-1]], axis=0))
    return jnp.stack(plus), jnp.stack(minus)


def hyena_long_convs(p_hy, conv_w, conv_b, kp, h_bias, tc, n_ph, row_len):
    L = p_hy.shape[1]
    C = h_bias.shape[1]
    cos_t, sin_t = dft_tables(L // n_ph)
    z = p_hy
    for o in range(h_bias.shape[0]):
        fwd = slice((2 * o) * C, (2 * o + 1) * C)
        bwd = slice((2 * o + 1) * C, (2 * o + 2) * C)
        plus, minus = _polyphase_taps(kp[:, fwd], kp[:, bwd], n_ph)
        kr, ks, kn = filter_spectrum(plus + minus, plus - minus, cos_t, sin_t, tc)
        last = o == h_bias.shape[0] - 1
        z = long_conv_gate(z, 0, o == 0, p_hy, o + 1, conv_w, conv_b, kr, ks, kn, h_bias[o][None, None, :],
                           cos_t, sin_t, tc, n_ph, row_len, jnp.bfloat16 if last else jnp.float32)
    return z


def _filter_kernel(band_ref, w1_ref, b1_ref, fr_ref, w2_ref, b2_ref, w3_ref, dl_ref, k_ref, *, seq_len, n_ph):
    hp = lax.Precision.HIGHEST
    f32 = jnp.float32
    tp = k_ref.shape[0]
    per_phase = seq_len // n_ph
    g = lax.broadcasted_iota(jnp.int32, (tp, 1), 0) + pl.program_id(0) * tp
    phase = g // per_phase
    pos = (n_ph * (g - phase * per_phase) + phase).astype(f32)
    t = pos / max(seq_len - 1, 1)
    ang = (2 * math.pi / seq_len) * pos * band_ref[...]
    lane = lax.broadcasted_iota(jnp.int32, (tp, LANES), 1)
    feats = jnp.where(lane == 0, t,
                      jnp.where(lane <= HYENA_BANDS, jnp.cos(ang),
                                jnp.where(lane <= 2 * HYENA_BANDS, -jnp.sin(ang), 0.0)))
    h = jnp.sin(fr_ref[...] * (jnp.dot(feats, w1_ref[...], precision=hp, preferred_element_type=f32) + b1_ref[...]))
    h = jnp.sin(fr_ref[...] * (jnp.dot(h, w2_ref[...], precision=hp, preferred_element_type=f32) + b2_ref[...]))
    window = jnp.exp(-t * dl_ref[...])
    c = dl_ref.shape[1]
    for j in range(w3_ref.shape[1] // c):
        cols = slice(j * c, (j + 1) * c)
        k_ref[:, cols] = jnp.dot(h, w3_ref[:, cols], precision=hp, preferred_element_type=f32) * window


def hyena_filters_polyphase(seq_len, f_w1, f_b1, f_freq, f_w2, f_b2, f_w3, d_hyena, tp, n_ph):
    f32 = jnp.float32
    fh = f_w1.shape[1]
    n_emb = 1 + 2 * HYENA_BANDS
    bands = jnp.linspace(1e-4, HYENA_BANDS - 1, HYENA_BANDS, dtype=f32)
    band_row = jnp.concatenate([jnp.zeros((1,), f32), bands, bands, jnp.zeros((LANES - n_emb,), f32)])[None, :]
    w1p = jnp.concatenate([f_w1, jnp.zeros((LANES - n_emb, fh), f32)], axis=0)
    deltas = jnp.abs(jnp.linspace(math.log(HYENA_TARGET) / HYENA_SLOW_DECAY,
                                  math.log(HYENA_TARGET) / HYENA_FAST_DECAY, d_hyena, dtype=f32))[None, :]
    n_out = f_w3.shape[1]
    full = lambda a: pl.BlockSpec(a.shape, lambda i: (0,) * a.ndim)
    args = (band_row, w1p, f_b1[None, :], f_freq[None, :], f_w2, f_b2[None, :], f_w3, deltas)
    return pl.pallas_call(
        functools.partial(_filter_kernel, seq_len=seq_len, n_ph=n_ph),
        grid=(seq_len // tp,),
        in_specs=[full(a) for a in args],
        out_specs=pl.BlockSpec((tp, n_out), lambda i: (i, 0)),
        out_shape=jax.ShapeDtypeStruct((seq_len, n_out), f32),
        compiler_params=pltpu.CompilerParams(dimension_semantics=("arbitrary",), vmem_limit_bytes=VMEM_LIMIT_BYTES),
    )(*args)


def _ada_kernel(c_ref, w_ref, b_ref, o_ref):
    cv = c_ref[...]
    s = cv * jax.nn.sigmoid(cv)
    o_ref[...] = jnp.dot(s, w_ref[...], precision=lax.Precision.HIGHEST,
                         preferred_element_type=jnp.float32) + b_ref[...]


def ada_modulation(c_rows, w_ada, b_ada, tn):
    rows, D = c_rows.shape
    N = w_ada.shape[1]
    return pl.pallas_call(
        _ada_kernel,
        grid=(N // tn,),
        in_specs=[pl.BlockSpec((rows, D), lambda j: (0, 0)),
                  pl.BlockSpec((D, tn), lambda j: (0, j)),
                  pl.BlockSpec((1, tn), lambda j: (0, j))],
        out_specs=pl.BlockSpec((rows, tn), lambda j: (0, j)),
        out_shape=jax.ShapeDtypeStruct((rows, N), jnp.float32),
        compiler_params=pltpu.CompilerParams(dimension_semantics=("arbitrary",), vmem_limit_bytes=VMEM_LIMIT_BYTES),
    )(c_rows, w_ada, b_ada[None, :])


def _ssd_kernel(xf_ref, df_ref, xb_ref, db_ref, dskip_ref, y_ref, h_ref, *, n_ctx_chunks):
    s = pl.program_id(1)
    n_steps = pl.num_programs(1)
    Q, G, R, P, N = SSD_CHUNK, SSD_GROUPS, SSD_HPG, SSD_HEAD_DIM, SSD_STATE
    GP = R * P
    bf = jnp.bfloat16

    @pl.when(s == 0)
    def _():
        h_ref[...] = jnp.zeros_like(h_ref)
        y_ref[...] = jnp.zeros_like(y_ref)

    row = lax.broadcasted_iota(jnp.int32, (Q, Q), 0)
    col = lax.broadcasted_iota(jnp.int32, (Q, Q), 1)
    lane_head = lax.broadcasted_iota(jnp.int32, (Q, GP), 1) // P
    block_head = lax.broadcasted_iota(jnp.int32, (Q, LANES), 1) // P
    is_latent = s >= n_ctx_chunks
    n_lat = n_steps - n_ctx_chunks
    out_chunk = (jnp.clip(s - n_ctx_chunks, 0, n_lat - 1), jnp.clip(n_steps - 1 - s, 0, n_lat - 1))

    for bb, d in [(bb, d) for bb in range(y_ref.shape[0]) for d in range(2)]:
        x_ref, da_ref = ((xf_ref, df_ref), (xb_ref, db_ref))[d]
        mask = (row >= col) if d == 0 else (col >= row)
        tri = mask.astype(jnp.float32)
        da = da_ref[bb]
        cum = jnp.dot(tri, da, precision=lax.Precision.HIGHEST, preferred_element_type=jnp.float32)
        cum_t = cum.T
        edge = Q - 1 if d == 0 else 0
        blk = x_ref.at[bb]
        for g in range(G):
            xg = blk[:, g * GP:(g + 1) * GP]
            bg = blk[:, D_SSD + g * N:D_SSD + (g + 1) * N].astype(bf)
            cg = blk[:, D_SSD + G * N + g * N:D_SSD + G * N + (g + 1) * N].astype(bf)
            heads = [d * SSD_HEADS + g * R + r for r in range(R)]
            dtm = jnp.zeros((Q, GP), jnp.float32)
            cumm = jnp.zeros((Q, GP), jnp.float32)
            for r, h in enumerate(heads):
                sel = lane_head == r
                dtm = jnp.where(sel, da[:, h:h + 1], dtm)
                cumm = jnp.where(sel, cum[:, SSD_HEADS * 2 + h:SSD_HEADS * 2 + h + 1], cumm)
            totm = cumm[edge:edge + 1, :]
            xdt = xg * dtm
            hg = h_ref[bb, d, g * GP:(g + 1) * GP, :]

            gmat = lax.dot_general(cg, bg, (((1,), (1,)), ((), ())), preferred_element_type=jnp.float32)
            y_off = lax.dot_general(cg, hg.astype(bf), (((1,), (1,)), ((), ())),
                                    preferred_element_type=jnp.float32) * jnp.exp(cumm)
            if d == 0:
                y_off = y_off + dskip_ref[:, g * GP:(g + 1) * GP] * xg
            per_block = LANES // P
            parts = [jnp.zeros((Q, LANES), jnp.float32) for _ in range(R // per_block)]
            for r, h in enumerate(heads):
                a_col = cum[:, SSD_HEADS * 2 + h:SSD_HEADS * 2 + h + 1]
                a_row = cum_t[SSD_HEADS * 2 + h:SSD_HEADS * 2 + h + 1, :]
                decay = jnp.exp(jnp.where(mask, a_col - a_row, NEG_BIG))
                j = r // per_block
                xblk = xdt[:, j * LANES:(j + 1) * LANES]
                own = block_head == r % per_block
                parts[j] = parts[j] + jnp.dot((gmat * decay).astype(bf), jnp.where(own, xblk, 0.0).astype(bf),
                                              preferred_element_type=jnp.float32)
            y = jnp.where(is_latent, y_off + jnp.concatenate(parts, axis=-1), 0.0)
            rows = pl.ds(pl.multiple_of(out_chunk[d] * Q, Q), Q)
            y_ref[bb, rows, g * GP:(g + 1) * GP] += y

            xw = (xdt * jnp.exp(totm - cumm)).astype(bf)
            st = lax.dot_general(xw, bg, (((0,), (0,)), ((), ())), preferred_element_type=jnp.float32)
            for r, h in enumerate(heads):
                dec = jnp.exp(cum_t[SSD_HEADS * 2 + h:SSD_HEADS * 2 + h + 1, edge:edge + 1])
                rs = slice(g * GP + r * P, g * GP + (r + 1) * P)
                h_ref[bb, d, rs, :] = h_ref[bb, d, rs, :] * dec + st[r * P:(r + 1) * P, :]


def ssd_scan_bidir(xbc, dta, d_skip, n_ctx, lat_off, nb):
    bsz, lt, width = xbc.shape
    Q = SSD_CHUNK
    n_ctx_chunks = n_ctx // Q
    L = lt - lat_off
    n_lat = L // Q
    n_steps = n_ctx_chunks + n_lat
    lat0 = lat_off // Q

    def fwd_chunk(s):
        return jnp.where(s < n_ctx_chunks, s, s - n_ctx_chunks + lat0)

    def bwd_chunk(s):
        return jnp.where(s < n_ctx_chunks, n_ctx_chunks - 1 - s, n_steps - 1 - s + lat0)

    return pl.pallas_call(
        functools.partial(_ssd_kernel, n_ctx_chunks=n_ctx_chunks),
        grid=(bsz // nb, n_steps),
        in_specs=[
            pl.BlockSpec((nb, Q, width), lambda b, s: (b, fwd_chunk(s), 0)),
            pl.BlockSpec((nb, Q, LANES), lambda b, s: (b, fwd_chunk(s), 0)),
            pl.BlockSpec((nb, Q, width), lambda b, s: (b, bwd_chunk(s), 0)),
            pl.BlockSpec((nb, Q, LANES), lambda b, s: (b, bwd_chunk(s), 0)),
            pl.BlockSpec((1, D_SSD), lambda b, s: (0, 0)),
        ],
        out_specs=pl.BlockSpec((nb, L, D_SSD), lambda b, s: (b, 0, 0)),
        out_shape=jax.ShapeDtypeStruct((bsz, L, D_SSD), jnp.float32),
        scratch_shapes=[pltpu.VMEM((nb, 2, SSD_GROUPS * SSD_HPG * SSD_HEAD_DIM, SSD_STATE), jnp.float32)],
        compiler_params=pltpu.CompilerParams(
            dimension_semantics=("arbitrary", "arbitrary"), vmem_limit_bytes=VMEM_LIMIT_BYTES),
    )(xbc, dta, xbc, dta, d_skip)


def kernel(x, c, ctx, c_ctx, w_ada, b_ada, g_norm1, g_norm2, w_in, hy_conv_w, hy_conv_b, hy_f_w1, hy_f_b1, hy_f_freq, hy_f_w2, hy_f_b2, hy_f_w3, hy_bias, ssd_conv_w, ssd_conv_b, ssd_a_log, ssd_dt_bias, ssd_d, ssd_norm_g, w_out, w_group, b_group, w_expert, b_expert, w1, w3, w2, g_final):
    bsz, seq_len, _ = x.shape
    assert w_in.shape[0] == 1, "single-layer block: the context stream only supplies SSD states"
    l = 0
    rows_pad = -(bsz + 1) % SUBLANES
    c_rows = jnp.concatenate([c, c_ctx[None, :], jnp.zeros((rows_pad, D_MODEL), jnp.float32)], axis=0)
    mod_all = ada_modulation(c_rows, w_ada[l], b_ada[l], ADA_COLS)
    sh1, sc1, ga1, sh2, sc2, ga2 = jnp.split(mod_all[:bsz, None, :], 6, axis=-1)
    csh1, csc1 = mod_all[bsz, :D_MODEL], mod_all[bsz, D_MODEL:2 * D_MODEL]

    w_out_bf = w_out[l].astype(jnp.bfloat16)
    w_in_bf, w_dt_bf, dt_bias2, dt_mult = in_proj_params(w_in[l], ssd_a_log[l], ssd_dt_bias[l],
                                                         HY_COLS, D_SSD, D_XBC)

    ctx_len = ctx.shape[1]
    lat_off = -(-ctx_len // IN_PROJ_ROWS) * IN_PROJ_ROWS
    p_hy, z, xbc, dta = in_proj_fused(ctx, x, g_norm1[l][None, :], csh1[None, :], csc1[None, :], sh1, sc1,
                                      w_in_bf, w_dt_bf, HY_COLS, ssd_conv_w[l], ssd_conv_b[l][None, :],
                                      dt_bias2, dt_mult, GRID_W, ctx_len, IN_PROJ_ROWS, IN_PROJ_COLS)
    kp = hyena_filters_polyphase(seq_len, hy_f_w1[l], hy_f_b1[l], hy_f_freq[l], hy_f_w2[l], hy_f_b2[l],
                                 hy_f_w3[l], D_HYENA, FILTER_ROWS, HYENA_PHASES)
    y_hy = hyena_long_convs(p_hy, hy_conv_w[l], hy_conv_b[l][None, :], kp, hy_bias[l], HYENA_COLS, HYENA_PHASES,
                            GRID_W)
    y_scan = ssd_scan_bidir(xbc, dta, jnp.repeat(ssd_d[l], SSD_HEAD_DIM)[None, :], ctx_len, lat_off, SSD_BATCH)

    pad = LANES - MOE_GROUPS - N_EXPERTS
    w_router = jnp.concatenate([w_group[l], w_expert[l], jnp.zeros((D_MODEL, pad), jnp.float32)], axis=1)
    w_router_hi = w_router.astype(jnp.bfloat16)
    w_router_lo = (w_router - w_router_hi.astype(jnp.float32)).astype(jnp.bfloat16)
    w_router = jnp.stack([w_router_hi, w_router_lo])
    b_router = jnp.concatenate([b_group[l], b_expert[l], jnp.zeros((pad,), jnp.float32)])[None, :]
    x1, hn, route_i, route_f, counts = out_proj_router(
        y_hy, y_scan, z, 0, x, ga1, sc2, sh2, ssd_norm_g[l][None, :], g_norm2[l][None, :],
        w_out_bf, w_router, b_router, TOKEN_ROWS)
    n_tok = bsz * seq_len
    n_blocks = -(-n_tok * TOP_K // MOE_BLOCK) + N_EXPERTS
    dest, block_wsel, block_first, block_valid, block_src = moe_plan(route_i, counts, MOE_BLOCK, n_blocks)
    buf = moe_dispatch(hn.reshape(n_tok, D_MODEL), dest, block_valid, MOE_BLOCK, TOKEN_ROWS)
    yb = expert_blocks(buf, block_wsel, block_first, block_valid, block_src, w1[l], w3[l], w2[l], MOE_BLOCK)
    return moe_combine(x1, route_f, ga2, g_final[None, :], yb, dest, TOKEN_ROWS)
```

```python
import functools
import math

import jax
import jax.numpy as jnp
from jax import lax
from jax.experimental import pallas as pl
from jax.experimental.pallas import tpu as pltpu

D_MODEL = 1024
CTX_LEN = 256
GRID_W = 64
EPS = 1e-6
SHORT_CONV = 3

D_HYENA = D_MODEL // 2
HYENA_ORDER = 2
HYENA_BANDS = 8
HYENA_FAST_DECAY = 0.3
HYENA_SLOW_DECAY = 1.5
HYENA_TARGET = 1e-2
HYENA_PHASES = 4

D_SSD = D_MODEL // 2
SSD_HEAD_DIM = 64
SSD_HEADS = D_SSD // SSD_HEAD_DIM
SSD_GROUPS = 2
SSD_HPG = SSD_HEADS // SSD_GROUPS
SSD_STATE = 128
SSD_CHUNK = 128

D_XBC = D_SSD + 2 * SSD_GROUPS * SSD_STATE
HY_COLS = (HYENA_ORDER + 1) * D_HYENA
D_IN = HY_COLS + D_SSD + D_XBC + 2 * SSD_HEADS
LANES = 128
SUBLANES = 8
D_IN_PAD = -(-D_IN // LANES) * LANES

MOE_GROUPS = 8
EXPERTS_PER_GROUP = 8
N_EXPERTS = MOE_GROUPS * EXPERTS_PER_GROUP
TOP_K = 2
D_EXPERT = 512
MOE_BLOCK = 256
ROUTE_COLS = 8

IN_PROJ_ROWS = 512
IN_PROJ_COLS = 512
TOKEN_ROWS = 512
HYENA_COLS = 256
FILTER_ROWS = 256
ADA_COLS = 512
SSD_BATCH = 4

VMEM_LIMIT_BYTES = 56 * 1024 * 1024
NEG_BIG = -1e30


def _conv3_rows(p, w_ref, b_ref, cols, has_prev, has_next):
    n = p.shape[0]
    prev = jnp.where(has_prev, pltpu.roll(p, 1, 0), 0.0)
    nxt = jnp.where(has_next, pltpu.roll(p, n - 1, 0), 0.0)
    return b_ref[:, cols] + w_ref[0:1, cols] * prev + w_ref[1:2, cols] * p + w_ref[2:3, cols] * nxt


def _in_proj_kernel(ctx_ref, x_ref, g_ref, csh_ref, csc_ref, sh_ref, sc_ref, w_ref, wdt_ref,
                    sw_ref, sb_ref, dtb_ref, dtm_ref, u_ref, z_ref, xbc_ref, dta_ref, h_ref,
                    *, n_ctx_steps, row_len, ctx_row_len, hy_cols, d_ssd, d_xbc, tn):
    i = pl.program_id(1)
    is_ctx = i < n_ctx_steps
    tm = x_ref.shape[1]
    ctx_t = ctx_ref[0]
    if ctx_t.shape[0] < tm:
        ctx_t = jnp.concatenate([ctx_t, jnp.zeros((tm - ctx_t.shape[0], ctx_t.shape[1]), ctx_t.dtype)], axis=0)
    xin = jnp.where(is_ctx, ctx_t, x_ref[0])
    shift = jnp.where(is_ctx, csh_ref[...], sh_ref[0])
    scale = jnp.where(is_ctx, csc_ref[...], sc_ref[0])
    y = xin * lax.rsqrt(jnp.mean(xin * xin, axis=-1, keepdims=True) + EPS) * g_ref[...]
    h_ref[...] = (y * (1.0 + scale) + shift).astype(jnp.bfloat16)

    pos = lax.broadcasted_iota(jnp.int32, (tm, 1), 0) + jnp.where(is_ctx, i, i - n_ctx_steps) * tm
    in_row = jnp.where(is_ctx, pos % ctx_row_len, pos % row_len)
    has_prev = in_row != 0
    has_next = in_row != jnp.where(is_ctx, ctx_row_len - 1, row_len - 1)

    for c0 in range(0, hy_cols, tn):
        cols = slice(c0, c0 + tn)
        u_ref[0, :, cols] = jnp.dot(h_ref[...], w_ref[:, cols], preferred_element_type=jnp.float32)
    z_ref[0] = jnp.dot(h_ref[...], w_ref[:, hy_cols:hy_cols + d_ssd], preferred_element_type=jnp.float32)

    for c0 in range(0, d_xbc, tn):
        cols = slice(c0, c0 + tn)
        wc = slice(hy_cols + d_ssd + c0, hy_cols + d_ssd + c0 + tn)
        p = jnp.dot(h_ref[...], w_ref[:, wc], preferred_element_type=jnp.float32)
        v = _conv3_rows(p, sw_ref, sb_ref, cols, has_prev, has_next)
        xbc_ref[0, :, cols] = v * jax.nn.sigmoid(v)
    pd = jnp.dot(h_ref[...], wdt_ref[...], preferred_element_type=jnp.float32) + dtb_ref[...]
    sp = jnp.maximum(pd, 0.0) + jnp.log(1.0 + jnp.exp(-jnp.abs(pd)))
    dta_ref[0] = sp * dtm_ref[...]


def in_proj_fused(ctx, x, g1, csh, csc, sh, sc, w_bf, wdt_bf, hy_cols, ssd_w, ssd_b, dt_bias2, dt_mult,
                  row_len, ctx_row_len, tm, tn):
    bsz, L, D = x.shape
    lc = ctx.shape[1]
    d_xbc = ssd_w.shape[1]
    d_ssd = w_bf.shape[1] - hy_cols - d_xbc
    n_ctx_steps = -(-lc // tm)
    ctx_rows = min(lc, tm)
    lc = n_ctx_steps * tm
    n_steps = n_ctx_steps + L // tm
    lat = lambda b, i: (b, jnp.maximum(i - n_ctx_steps, 0), 0)
    allt = lambda b, i: (b, i, 0)
    const2 = lambda b, i: (0, 0)
    per_b = pl.BlockSpec((1, 1, D), lambda b, i: (b, 0, 0))
    kern = functools.partial(_in_proj_kernel, n_ctx_steps=n_ctx_steps, row_len=row_len, ctx_row_len=ctx_row_len,
                             hy_cols=hy_cols, d_ssd=d_ssd, d_xbc=d_xbc, tn=tn)
    return pl.pallas_call(
        kern,
        grid=(bsz, n_steps),
        in_specs=[
            pl.BlockSpec((1, ctx_rows, D), lambda b, i: (b, jnp.minimum(i, n_ctx_steps - 1), 0)),
            pl.BlockSpec((1, tm, D), lat),
            pl.BlockSpec((1, D), const2),
            pl.BlockSpec((1, D), const2),
            pl.BlockSpec((1, D), const2),
            per_b, per_b,
            pl.BlockSpec(w_bf.shape, const2),
            pl.BlockSpec(wdt_bf.shape, const2),
            pl.BlockSpec(ssd_w.shape, const2),
            pl.BlockSpec(ssd_b.shape, const2),
            pl.BlockSpec((1, LANES), const2),
            pl.BlockSpec((1, LANES), const2),
        ],
        out_specs=[
            pl.BlockSpec((1, tm, hy_cols), lat),
            pl.BlockSpec((1, tm, d_ssd), lat),
            pl.BlockSpec((1, tm, d_xbc), allt),
            pl.BlockSpec((1, tm, LANES), allt),
        ],
        out_shape=[
            jax.ShapeDtypeStruct((bsz, L, hy_cols), jnp.float32),
            jax.ShapeDtypeStruct((bsz, L, d_ssd), jnp.float32),
            jax.ShapeDtypeStruct((bsz, lc + L, d_xbc), jnp.float32),
            jax.ShapeDtypeStruct((bsz, lc + L, LANES), jnp.float32),
        ],
        scratch_shapes=[pltpu.VMEM((tm, D), jnp.bfloat16)],
        compiler_params=pltpu.CompilerParams(
            dimension_semantics=("arbitrary", "arbitrary"), vmem_limit_bytes=VMEM_LIMIT_BYTES),
    )(ctx, x, g1, csh, csc, sh, sc, w_bf, wdt_bf, ssd_w, ssd_b, dt_bias2, dt_mult)


def in_proj_params(w_in, a_log, dt_bias, hy_cols, d_ssd, d_xbc):
    n_h = 2 * SSD_HEADS
    main = hy_cols + d_ssd + d_xbc
    w_dt = w_in[:, main:main + n_h]
    pad = jnp.zeros((w_in.shape[0], LANES - 2 * n_h), w_in.dtype)
    wdt = jnp.concatenate([w_dt, w_dt, pad], axis=1).astype(jnp.bfloat16)
    zpad = jnp.zeros((LANES - 2 * n_h,), jnp.float32)
    bias2 = jnp.concatenate([dt_bias.reshape(n_h), dt_bias.reshape(n_h), zpad])[None, :]
    mult = jnp.concatenate([jnp.ones((n_h,), jnp.float32), -jnp.exp(a_log).reshape(n_h), zpad])[None, :]
    return w_in[:, :main].astype(jnp.bfloat16), wdt, bias2, mult


def _out_router_kernel(yh_ref, ys_ref, z_ref, x_ref, ga_ref, sc_ref, sh_ref, ng_ref, g2_ref, wo_ref, wr_ref, br_ref,
                       x1_ref, hn_ref, ri_ref, rf_ref, cnt_ref, carry_ref):
    first = jnp.logical_and(pl.program_id(0) == 0, pl.program_id(1) == 0)

    @pl.when(first)
    def _():
        carry_ref[...] = jnp.zeros_like(carry_ref)

    bf = jnp.bfloat16
    tm = x_ref.shape[1]
    dh = yh_ref.shape[2]
    z = z_ref[0]
    ys = ys_ref[0] * (z * jax.nn.sigmoid(z))
    gw = ys.shape[1] // SSD_GROUPS
    acc = jnp.dot(yh_ref[0].astype(bf), wo_ref[0:dh, :], preferred_element_type=jnp.float32)
    for g in range(SSD_GROUPS):
        yg = ys[:, g * gw:(g + 1) * gw]
        yg = yg * lax.rsqrt(jnp.mean(yg * yg, axis=-1, keepdims=True) + EPS) * ng_ref[:, g * gw:(g + 1) * gw]
        acc += jnp.dot(yg.astype(bf), wo_ref[dh + g * gw:dh + (g + 1) * gw, :], preferred_element_type=jnp.float32)
    x1 = x_ref[0] + ga_ref[0] * acc
    x1_ref[0] = x1
    hn = x1 * lax.rsqrt(jnp.mean(x1 * x1, axis=-1, keepdims=True) + EPS) * g2_ref[...]
    hn = hn * (1.0 + sc_ref[0]) + sh_ref[0]
    hn_ref[0] = hn

    hn_hi = hn.astype(bf)
    hn_lo = (hn - hn_hi.astype(jnp.float32)).astype(bf)
    logits = (jnp.dot(hn_hi, wr_ref[0], preferred_element_type=jnp.float32)
              + jnp.dot(hn_lo, wr_ref[0], preferred_element_type=jnp.float32)
              + jnp.dot(hn_hi, wr_ref[1], preferred_element_type=jnp.float32)) + br_ref[...]
    lane = lax.broadcasted_iota(jnp.int32, (tm, LANES), 1)
    lane_f = lane.astype(jnp.float32)
    ninf = jnp.float32(-jnp.inf)
    big = jnp.float32(1e9)
    gl = jnp.where(lane < MOE_GROUPS, logits, ninf)
    gmax = jnp.max(gl, axis=-1, keepdims=True)
    p_group = 1.0 / jnp.sum(jnp.exp(gl - gmax), axis=-1, keepdims=True)
    g_sel = jnp.min(jnp.where(gl == gmax, lane_f, big), axis=-1, keepdims=True)
    e_lane = lane - MOE_GROUPS
    in_grp = jnp.logical_and(e_lane >= 0, (e_lane // EXPERTS_PER_GROUP).astype(jnp.float32) == g_sel)
    el = jnp.where(in_grp, logits, ninf)
    m1 = jnp.max(el, axis=-1, keepdims=True)
    i1 = jnp.min(jnp.where(el == m1, lane_f, big), axis=-1, keepdims=True)
    el2 = jnp.where(lane_f == i1, ninf, el)
    m2 = jnp.max(el2, axis=-1, keepdims=True)
    i2 = jnp.min(jnp.where(el2 == m2, lane_f, big), axis=-1, keepdims=True)
    t = jnp.exp(m2 - m1)
    w1 = 1.0 / (1.0 + t)
    gate1 = w1 * p_group
    gate2 = (t * w1) * p_group
    e1 = i1 - MOE_GROUPS
    e2 = i2 - MOE_GROUPS
    el_f = e_lane.astype(jnp.float32)
    oh1 = el_f == e1
    oh2 = el_f == e2
    oh = jnp.logical_or(oh1, oh2).astype(bf)
    r_i = lax.broadcasted_iota(jnp.int32, (tm, tm), 0)
    c_i = lax.broadcasted_iota(jnp.int32, (tm, tm), 1)
    before = jnp.dot((c_i < r_i).astype(bf), oh, preferred_element_type=jnp.float32) + carry_ref[...]
    rank1 = jnp.sum(jnp.where(oh1, before, 0.0), axis=-1, keepdims=True)
    rank2 = jnp.sum(jnp.where(oh2, before, 0.0), axis=-1, keepdims=True)
    carry_ref[...] += jnp.sum(oh.astype(jnp.float32), axis=0, keepdims=True)
    cnt_ref[...] = carry_ref[...]

    rec = jnp.where(lane == 0, e1, jnp.where(lane == 1, e2, jnp.where(lane == 2, rank1,
                                                                      jnp.where(lane == 3, rank2, 0.0))))
    ri_ref[0] = rec.T[0:ROUTE_COLS, :].astype(jnp.int32)
    col = lax.broadcasted_iota(jnp.int32, (tm, ROUTE_COLS), 1)
    rf_ref[0] = jnp.where(col == 0, gate1, gate2)


def out_proj_router(y_hy, y_scan, px, z_col, x, ga1, sc2, sh2, norm_g, g2, w_out_bf, w_router, b_router, tm):
    bsz, L, D = x.shape
    dh = y_hy.shape[-1]
    ds = y_scan.shape[-1]
    tok = lambda b, i: (b, i, 0)
    per_b = pl.BlockSpec((1, 1, D), lambda b, i: (b, 0, 0))
    const2 = lambda b, i: (0, 0)
    return pl.pallas_call(
        _out_router_kernel,
        grid=(bsz, L // tm),
        in_specs=[
            pl.BlockSpec((1, tm, dh), tok),
            pl.BlockSpec((1, tm, ds), tok),
            pl.BlockSpec((1, tm, ds), lambda b, i: (b, i, z_col)),
            pl.BlockSpec((1, tm, D), tok),
            per_b, per_b, per_b,
            pl.BlockSpec((1, ds), const2),
            pl.BlockSpec((1, D), const2),
            pl.BlockSpec((dh + ds, D), const2),
            pl.BlockSpec((2, D, LANES), lambda b, i: (0, 0, 0)),
            pl.BlockSpec((1, LANES), const2),
        ],
        out_specs=[
            pl.BlockSpec((1, tm, D), tok),
            pl.BlockSpec((1, tm, D), tok),
            pl.BlockSpec((1, ROUTE_COLS, tm), lambda b, i: (b, 0, i)),
            pl.BlockSpec((1, tm, ROUTE_COLS), tok),
            pl.BlockSpec((1, LANES), const2),
        ],
        out_shape=[
            jax.ShapeDtypeStruct((bsz, L, D), jnp.float32),
            jax.ShapeDtypeStruct((bsz, L, D), jnp.float32),
            jax.ShapeDtypeStruct((bsz, ROUTE_COLS, L), jnp.int32),
            jax.ShapeDtypeStruct((bsz, L, ROUTE_COLS), jnp.float32),
            jax.ShapeDtypeStruct((1, LANES), jnp.float32),
        ],
        scratch_shapes=[pltpu.VMEM((1, LANES), jnp.float32)],
        compiler_params=pltpu.CompilerParams(
            dimension_semantics=("arbitrary", "arbitrary"), vmem_limit_bytes=VMEM_LIMIT_BYTES),
    )(y_hy, y_scan, px, x, ga1, sc2, sh2, norm_g, g2, w_out_bf, w_router, b_router)


def _row_copy(src_hbm, src_row, dst_ref, dst_row, sem):
    return pltpu.make_async_copy(src_hbm.at[pl.ds(src_row, 1), :], dst_ref.at[pl.ds(dst_row, 1), :], sem)


def _dispatch_kernel(dest_ref, valid_ref, hn_ref, buf_hbm, zeros, sem, zsem):
    step = pl.program_id(0)
    tm = hn_ref.shape[0]
    n_tok = pl.num_programs(0) * tm
    blk = zeros.shape[0]
    n_blocks = buf_hbm.shape[0] // blk

    def zero_copy(i):
        return pltpu.make_async_copy(zeros, buf_hbm.at[pl.ds(pl.multiple_of(i * blk, blk), blk), :], zsem)

    def zfill(i, carry):
        @pl.when(valid_ref[i] < blk)
        def _():
            zero_copy(i).start()
        return carry

    def zwait(i, carry):
        @pl.when(valid_ref[i] < blk)
        def _():
            zero_copy(i).wait()
        return carry

    @pl.when(step == 0)
    def _():
        zeros[...] = jnp.zeros_like(zeros)
        lax.fori_loop(0, n_blocks, zfill, 0)
        lax.fori_loop(0, n_blocks, zwait, 0)

    def body(j, carry):
        t = step * tm + j
        _row_copy(hn_ref, j, buf_hbm, dest_ref[t], sem).start()
        _row_copy(hn_ref, j, buf_hbm, dest_ref[n_tok + t], sem).start()
        return carry

    lax.fori_loop(0, tm, body, 0, unroll=8)
    for _ in range(2):
        pltpu.make_async_copy(hn_ref, buf_hbm.at[pl.ds(0, tm), :], sem).wait()


def moe_dispatch(hn, dest, block_valid, blk, tm):
    T, D = hn.shape
    n_rows = block_valid.shape[0] * blk
    grid_spec = pltpu.PrefetchScalarGridSpec(
        num_scalar_prefetch=2,
        grid=(T // tm,),
        in_specs=[pl.BlockSpec((tm, D), lambda i, d, v: (i, 0))],
        out_specs=pl.BlockSpec(memory_space=pl.ANY),
        scratch_shapes=[pltpu.VMEM((blk, D), hn.dtype), pltpu.SemaphoreType.DMA(()),
                        pltpu.SemaphoreType.DMA(())],
    )
    return pl.pallas_call(
        _dispatch_kernel,
        grid_spec=grid_spec,
        out_shape=jax.ShapeDtypeStruct((n_rows, D), hn.dtype),
        compiler_params=pltpu.CompilerParams(dimension_semantics=("arbitrary",), has_side_effects=True),
    )(dest, block_valid, hn)


def _expert_kernel(wsel_ref, first_ref, valid_ref, src_ref, x_ref, w1_ref, w3_ref, w2_ref, o_ref, w1b, w3b, w2b):
    i = pl.program_id(0)
    del wsel_ref, src_ref
    bf = jnp.bfloat16

    @pl.when(first_ref[i] == 1)
    def _():
        w1b[...] = w1_ref[0].astype(bf)
        w3b[...] = w3_ref[0].astype(bf)
        w2b[...] = w2_ref[0].astype(bf)

    valid = valid_ref[i]

    @pl.when(valid > 0)
    def _():
        xb = x_ref[...].astype(bf)
        a = jnp.dot(xb, w1b[...], preferred_element_type=jnp.float32)
        b = jnp.dot(xb, w3b[...], preferred_element_type=jnp.float32)
        h = (a * jax.nn.sigmoid(a)) * b
        o_ref[...] = jnp.dot(h.astype(bf), w2b[...], preferred_element_type=jnp.float32)

    @pl.when(valid <= 0)
    def _():
        o_ref[...] = jnp.zeros_like(o_ref)


def expert_blocks(buf, block_wsel, block_first, block_valid, block_src, w1, w3, w2, blk):
    rows, D = buf.shape
    n_blocks = rows // blk
    E, _, F = w1.shape
    grid_spec = pltpu.PrefetchScalarGridSpec(
        num_scalar_prefetch=4,
        grid=(n_blocks,),
        in_specs=[
            pl.BlockSpec((blk, D), lambda i, ws, fi, va, src: (src[i], 0)),
            pl.BlockSpec((1, D, F), lambda i, ws, fi, va, src: (ws[i], 0, 0)),
            pl.BlockSpec((1, D, F), lambda i, ws, fi, va, src: (ws[i], 0, 0)),
            pl.BlockSpec((1, F, D), lambda i, ws, fi, va, src: (ws[i], 0, 0)),
        ],
        out_specs=pl.BlockSpec((blk, D), lambda i, ws, fi, va, src: (i, 0)),
        scratch_shapes=[pltpu.VMEM((D, F), jnp.bfloat16), pltpu.VMEM((D, F), jnp.bfloat16),
                        pltpu.VMEM((F, D), jnp.bfloat16)],
    )
    return pl.pallas_call(
        _expert_kernel,
        grid_spec=grid_spec,
        out_shape=jax.ShapeDtypeStruct((rows, D), jnp.float32),
        compiler_params=pltpu.CompilerParams(
            dimension_semantics=("arbitrary",), vmem_limit_bytes=VMEM_LIMIT_BYTES),
    )(block_wsel, block_first, block_valid, block_src, buf, w1, w3, w2)


def _combine_kernel(dest_ref, x1_ref, rf_ref, ga_ref, gf_ref, yb_hbm, o_ref, ybuf, sem):
    b = pl.program_id(0)
    i = pl.program_id(1)
    n_i = pl.num_programs(1)
    tm = x1_ref.shape[1]
    step = b * n_i + i
    n_steps = pl.num_programs(0) * n_i
    slot = step % 2

    def issue(step_, slot_):
        def body(j, carry):
            t = step_ * tm + j
            _row_copy(yb_hbm, dest_ref[t], ybuf.at[slot_, 0], j, sem.at[slot_]).start()
            _row_copy(yb_hbm, dest_ref[n_steps * tm + t], ybuf.at[slot_, 1], j, sem.at[slot_]).start()
            return carry
        lax.fori_loop(0, tm, body, 0, unroll=8)

    @pl.when(step == 0)
    def _():
        issue(0, 0)

    @pl.when(step + 1 < n_steps)
    def _():
        issue(step + 1, 1 - slot)

    pltpu.make_async_copy(yb_hbm.at[pl.ds(0, tm), :], ybuf.at[slot, 0], sem.at[slot]).wait()
    pltpu.make_async_copy(yb_hbm.at[pl.ds(0, tm), :], ybuf.at[slot, 1], sem.at[slot]).wait()
    rf = rf_ref[0]
    y = rf[:, 0:1] * ybuf[slot, 0] + rf[:, 1:2] * ybuf[slot, 1]
    x2 = x1_ref[0] + ga_ref[0] * y
    o_ref[0] = x2 * lax.rsqrt(jnp.mean(x2 * x2, axis=-1, keepdims=True) + EPS) * gf_ref[...]


def moe_combine(x1, route_f, ga2, g_final, yb, dest, tm):
    bsz, L, D = x1.shape
    grid_spec = pltpu.PrefetchScalarGridSpec(
        num_scalar_prefetch=1,
        grid=(bsz, L // tm),
        in_specs=[
            pl.BlockSpec((1, tm, D), lambda b, i, d: (b, i, 0)),
            pl.BlockSpec((1, tm, ROUTE_COLS), lambda b, i, d: (b, i, 0)),
            pl.BlockSpec((1, 1, D), lambda b, i, d: (b, 0, 0)),
            pl.BlockSpec((1, D), lambda b, i, d: (0, 0)),
            pl.BlockSpec(memory_space=pl.ANY),
        ],
        out_specs=pl.BlockSpec((1, tm, D), lambda b, i, d: (b, i, 0)),
        scratch_shapes=[pltpu.VMEM((2, 2, tm, D), jnp.float32), pltpu.SemaphoreType.DMA((2,))],
    )
    return pl.pallas_call(
        _combine_kernel,
        grid_spec=grid_spec,
        out_shape=jax.ShapeDtypeStruct((bsz, L, D), jnp.float32),
        compiler_params=pltpu.CompilerParams(
            dimension_semantics=("arbitrary", "arbitrary"), vmem_limit_bytes=VMEM_LIMIT_BYTES),
    )(dest, x1, route_f, ga2, g_final, yb)


def moe_plan(route_i, counts, blk, n_blocks):
    cnt = counts[0, MOE_GROUPS:MOE_GROUPS + N_EXPERTS].astype(jnp.int32)
    padded = (cnt + blk - 1) // blk * blk
    ends = jnp.cumsum(padded)
    starts = ends - padded
    experts = jnp.arange(N_EXPERTS, dtype=jnp.int32)
    dest = jnp.concatenate([
        (jnp.sum(jnp.where(route_i[:, k, :, None] == experts, starts, 0), axis=-1) + route_i[:, 2 + k]).reshape(-1)
        for k in range(TOP_K)])
    first_row = jnp.arange(n_blocks, dtype=jnp.int32) * blk
    block_eid = jnp.minimum(jnp.sum((ends[None, :] <= first_row[:, None]).astype(jnp.int32), axis=1), N_EXPERTS - 1)
    block_valid = jnp.clip(cnt[block_eid] - (first_row - starts[block_eid]), 0, blk).astype(jnp.int32)
    block_first = jnp.concatenate([jnp.ones((1,), jnp.int32),
                                   (block_eid[1:] != block_eid[:-1]).astype(jnp.int32)])
    idx = jnp.arange(n_blocks, dtype=jnp.int32)
    next_first = lax.cummin(jnp.where(block_first == 1, idx, n_blocks), axis=0, reverse=True)
    block_wsel = jnp.where(next_first < n_blocks, block_eid[jnp.minimum(next_first, n_blocks - 1)], block_eid)
    n_used = jnp.sum((block_valid > 0).astype(jnp.int32))
    block_src = jnp.minimum(idx, jnp.maximum(n_used - 1, 0))
    return dest, block_wsel, block_first, block_valid, block_src


def dft_tables(L):
    n = 2 * L
    f = lax.broadcasted_iota(jnp.int32, (L, L), 0)
    t = lax.broadcasted_iota(jnp.int32, (L, L), 1)
    ang = ((f * t) % n).astype(jnp.float32) * (2.0 * math.pi / n)
    return jnp.cos(ang).astype(jnp.bfloat16), jnp.sin(ang).astype(jnp.bfloat16)


def _alt_sign(L):
    t = lax.broadcasted_iota(jnp.int32, (L, 1), 0)
    return (1 - 2 * (t & 1)).astype(jnp.float32)


def _spectrum_kernel(a_ref, b_ref, c_ref, s_ref, kr_ref, ks_ref, kn_ref):
    L = a_ref.shape[1]
    a = a_ref[0]
    row = lax.broadcasted_iota(jnp.int32, (L, 1), 0)
    scale = jnp.where(row == 0, 0.5 / L, 1.0 / L)
    kr_ref[0] = scale * jnp.dot(c_ref[...], a.astype(jnp.bfloat16), preferred_element_type=jnp.float32)
    ks_ref[0] = scale * jnp.dot(s_ref[...], b_ref[0].astype(jnp.bfloat16), preferred_element_type=jnp.float32)
    kn_ref[0] = jnp.sum(a * _alt_sign(L), axis=0, keepdims=True) * (0.5 / L)


def filter_spectrum(a, b, cos_t, sin_t, tc):
    n, L, C = a.shape
    blk = pl.BlockSpec((1, L, tc), lambda o, j: (o, 0, j))
    tab = pl.BlockSpec((L, L), lambda o, j: (0, 0))
    return pl.pallas_call(
        _spectrum_kernel,
        grid=(n, C // tc),
        in_specs=[blk, blk, tab, tab],
        out_specs=[blk, blk, pl.BlockSpec((1, 1, tc), lambda o, j: (o, 0, j))],
        out_shape=[jax.ShapeDtypeStruct((n, L, C), jnp.float32)] * 2 + [jax.ShapeDtypeStruct((n, 1, C), jnp.float32)],
        compiler_params=pltpu.CompilerParams(
            dimension_semantics=("arbitrary", "arbitrary"), vmem_limit_bytes=VMEM_LIMIT_BYTES),
    )(a, b, cos_t, sin_t)


def _phase_conv3(raw, w_ref, b_ref, rows_per_phase):
    n_ph = len(raw)
    h = raw[0].shape[0]
    j = lax.broadcasted_iota(jnp.int32, (h, 1), 0) % rows_per_phase
    prev0 = jnp.where(j != 0, pltpu.roll(raw[n_ph - 1], 1, 0), 0.0)
    next_last = jnp.where(j != rows_per_phase - 1, pltpu.roll(raw[0], h - 1, 0), 0.0)
    out = []
    for p in range(n_ph):
        prev = raw[p - 1] if p > 0 else prev0
        nxt = raw[p + 1] if p < n_ph - 1 else next_last
        out.append(b_ref[...] + w_ref[0:1, :] * prev + w_ref[1:2, :] * raw[p] + w_ref[2:3, :] * nxt)
    return out


def _long_conv_kernel(*refs, n_ph, n_slab, conv_z, rows_per_phase):
    z_refs = refs[:n_slab]
    xn_refs = refs[n_slab:2 * n_slab]
    (kr_ref, ks_ref, kn_ref, bias_ref, cwz_ref, cbz_ref, cwx_ref, cbx_ref, c_ref, s_ref,
     o_ref, acc_ref, zr_ref, zs_ref, yr_ref, ys_ref, stage_ref) = refs[2 * n_slab:]
    H = z_refs[0].shape[1] // n_ph
    f32 = jnp.float32
    bf = jnp.bfloat16
    sign = _alt_sign(H)

    def phases(slab_refs):
        return [jnp.concatenate([r[0, pl.ds(p, H, stride=n_ph), :] for r in slab_refs], axis=1)
                for p in range(n_ph)]

    z_ph = phases(z_refs)
    if conv_z:
        z_ph = _phase_conv3(z_ph, cwz_ref, cbz_ref, rows_per_phase)
    for q in range(n_ph):
        zb = z_ph[q].astype(bf)
        zr_ref[q] = jnp.dot(c_ref[...], zb, preferred_element_type=f32)
        zs_ref[q] = jnp.dot(s_ref[...], zb, preferred_element_type=f32)
    z_nyq = [jnp.sum(z * sign, axis=0, keepdims=True) for z in z_ph]
    for p in range(n_ph):
        nyq = sum(z_nyq[q] * kn_ref[p - q + n_ph - 1] for q in range(n_ph))
        acc_ref[p] = z_ph[p] * bias_ref[0] + sign * nyq
        yr = 0.0
        ys = 0.0
        for q in range(n_ph):
            slot = p - q + n_ph - 1
            yr = yr + zr_ref[q] * kr_ref[slot] - zs_ref[q] * ks_ref[slot]
            ys = ys + zr_ref[q] * ks_ref[slot] + zs_ref[q] * kr_ref[slot]
        yr_ref[p] = yr.astype(bf)
        ys_ref[p] = ys.astype(bf)
    for p in range(n_ph):
        acc_ref[p] += (jnp.dot(c_ref[...], yr_ref[p], preferred_element_type=f32)
                       + jnp.dot(s_ref[...], ys_ref[p], preferred_element_type=f32))
    x_ph = _phase_conv3(phases(xn_refs), cwx_ref, cbx_ref, rows_per_phase)
    for p in range(n_ph):
        out_p = x_ph[p] * acc_ref[p]
        for sl in range(n_slab):
            stage_ref[sl, pl.ds(p, H, stride=n_ph), :] = out_p[:, sl * LANES:(sl + 1) * LANES]
    for sl in range(n_slab):
        o_ref[0, :, sl * LANES:(sl + 1) * LANES] = stage_ref[sl].astype(o_ref.dtype)


def long_conv_gate(z_arr, z_col, conv_z, xn_arr, xn_col, conv_w, conv_b, kr, ks, kn, bias, cos_t, sin_t,
                   tc, n_ph, row_len, out_dtype):
    bsz, L, _ = z_arr.shape
    H = L // n_ph
    C = kr.shape[-1]
    nj = C // tc
    n_slab = tc // LANES
    n_f = 2 * n_ph - 1
    tab = pl.BlockSpec((H, H), lambda j, b: (0, 0), pipeline_mode=pl.Buffered(1))
    spec = pl.BlockSpec((n_f, H, tc), lambda j, b: (0, 0, j), pipeline_mode=pl.Buffered(1))
    nyq = pl.BlockSpec((n_f, 1, tc), lambda j, b: (0, 0, j))
    vec = pl.BlockSpec((1, 1, tc), lambda j, b: (0, 0, j))

    def slabs(col):
        return [pl.BlockSpec((1, L, LANES),
                             functools.partial(lambda j, b, sl: (b, 0, (col * nj + j) * n_slab + sl), sl=sl))
                for sl in range(n_slab)]

    def conv_specs(col):
        return [pl.BlockSpec((3, tc), lambda j, b: (0, col * nj + j)),
                pl.BlockSpec((1, tc), lambda j, b: (0, col * nj + j))]

    zc = z_col if conv_z else 0
    kern = functools.partial(_long_conv_kernel, n_ph=n_ph, n_slab=n_slab, conv_z=conv_z,
                             rows_per_phase=row_len // n_ph)
    return pl.pallas_call(
        kern,
        grid=(nj, bsz),
        in_specs=(slabs(z_col) + slabs(xn_col) + [spec, spec, nyq, vec] + conv_specs(zc) + conv_specs(xn_col)
                  + [tab, tab]),
        out_specs=pl.BlockSpec((1, L, tc), lambda j, b: (b, 0, j)),
        out_shape=jax.ShapeDtypeStruct((bsz, L, C), out_dtype),
        scratch_shapes=[pltpu.VMEM((n_ph, H, tc), jnp.float32), pltpu.VMEM((n_ph, H, tc), jnp.float32),
                        pltpu.VMEM((n_ph, H, tc), jnp.float32), pltpu.VMEM((n_ph, H, tc), jnp.bfloat16),
                        pltpu.VMEM((n_ph, H, tc), jnp.bfloat16), pltpu.VMEM((n_slab, L, LANES), jnp.float32)],
        compiler_params=pltpu.CompilerParams(
            dimension_semantics=("arbitrary", "arbitrary"), vmem_limit_bytes=VMEM_LIMIT_BYTES),
    )(*([z_arr] * n_slab), *([xn_arr] * n_slab), kr, ks, kn, bias, conv_w, conv_b, conv_w, conv_b, cos_t, sin_t)


def _polyphase_taps(kf, kb, n_ph):
    H = kf.shape[0] // n_ph
    ph = lambda a, p: a[p * H:(p + 1) * H]
    zero = jnp.zeros_like(kf[:1])
    plus, minus = [], []
    for r in range(-(n_ph - 1), n_ph):
        if r >= 0:
            plus.append(ph(kf, r))
        else:
            plus.append(jnp.concatenate([ph(kb, -r)[0:1], ph(kf, n_ph + r)[:-1]], axis=0))
        if r <= 0:
            minus.append(jnp.concatenate([zero, ph(kb, -r)[1:]], axis=0))
        else:
            minus.append(jnp.concatenate([zero, ph(kb, n_ph - r)[:-1]], axis=0))
    return jnp.stack(plus), jnp.stack(minus)


def hyena_long_convs(p_hy, conv_w, conv_b, kp, h_bias, tc, n_ph, row_len):
    L = p_hy.shape[1]
    C = h_bias.shape[1]
    cos_t, sin_t = dft_tables(L // n_ph)
    z = p_hy
    for o in range(h_bias.shape[0]):
        fwd = slice((2 * o) * C, (2 * o + 1) * C)
        bwd = slice((2 * o + 1) * C, (2 * o + 2) * C)
        plus, minus = _polyphase_taps(kp[:, fwd], kp[:, bwd], n_ph)
        kr, ks, kn = filter_spectrum(plus + minus, plus - minus, cos_t, sin_t, tc)
        last = o == h_bias.shape[0] - 1
        z = long_conv_gate(z, 0, o == 0, p_hy, o + 1, conv_w, conv_b, kr, ks, kn, h_bias[o][None, None, :],
                           cos_t, sin_t, tc, n_ph, row_len, jnp.bfloat16 if last else jnp.float32)
    return z


def _filter_kernel(band_ref, w1_ref, b1_ref, fr_ref, w2_ref, b2_ref, w3_ref, dl_ref, k_ref, *, seq_len, n_ph):
    hp = lax.Precision.HIGHEST
    f32 = jnp.float32
    tp = k_ref.shape[0]
    per_phase = seq_len // n_ph
    g = lax.broadcasted_iota(jnp.int32, (tp, 1), 0) + pl.program_id(0) * tp
    phase = g // per_phase
    pos = (n_ph * (g - phase * per_phase) + phase).astype(f32)
    t = pos / max(seq_len - 1, 1)
    ang = (2 * math.pi / seq_len) * pos * band_ref[...]
    lane = lax.broadcasted_iota(jnp.int32, (tp, LANES), 1)
    feats = jnp.where(lane == 0, t,
                      jnp.where(lane <= HYENA_BANDS, jnp.cos(ang),
                                jnp.where(lane <= 2 * HYENA_BANDS, -jnp.sin(ang), 0.0)))
    h = jnp.sin(fr_ref[...] * (jnp.dot(feats, w1_ref[...], precision=hp, preferred_element_type=f32) + b1_ref[...]))
    h = jnp.sin(fr_ref[...] * (jnp.dot(h, w2_ref[...], precision=hp, preferred_element_type=f32) + b2_ref[...]))
    window = jnp.exp(-t * dl_ref[...])
    c = dl_ref.shape[1]
    for j in range(w3_ref.shape[1] // c):
        cols = slice(j * c, (j + 1) * c)
        k_ref[:, cols] = jnp.dot(h, w3_ref[:, cols], precision=hp, preferred_element_type=f32) * window


def hyena_filters_polyphase(seq_len, f_w1, f_b1, f_freq, f_w2, f_b2, f_w3, d_hyena, tp, n_ph):
    f32 = jnp.float32
    fh = f_w1.shape[1]
    n_emb = 1 + 2 * HYENA_BANDS
    bands = jnp.linspace(1e-4, HYENA_BANDS - 1, HYENA_BANDS, dtype=f32)
    band_row = jnp.concatenate([jnp.zeros((1,), f32), bands, bands, jnp.zeros((LANES - n_emb,), f32)])[None, :]
    w1p = jnp.concatenate([f_w1, jnp.zeros((LANES - n_emb, fh), f32)], axis=0)
    deltas = jnp.abs(jnp.linspace(math.log(HYENA_TARGET) / HYENA_SLOW_DECAY,
                                  math.log(HYENA_TARGET) / HYENA_FAST_DECAY, d_hyena, dtype=f32))[None, :]
    n_out = f_w3.shape[1]
    full = lambda a: pl.BlockSpec(a.shape, lambda i: (0,) * a.ndim)
    args = (band_row, w1p, f_b1[None, :], f_freq[None, :], f_w2, f_b2[None, :], f_w3, deltas)
    return pl.pallas_call(
        functools.partial(_filter_kernel, seq_len=seq_len, n_ph=n_ph),
        grid=(seq_len // tp,),
        in_specs=[full(a) for a in args],
        out_specs=pl.BlockSpec((tp, n_out), lambda i: (i, 0)),
        out_shape=jax.ShapeDtypeStruct((seq_len, n_out), f32),
        compiler_params=pltpu.CompilerParams(dimension_semantics=("arbitrary",), vmem_limit_bytes=VMEM_LIMIT_BYTES),
    )(*args)


def _ada_kernel(c_ref, w_ref, b_ref, o_ref):
    cv = c_ref[...]
    s = cv * jax.nn.sigmoid(cv)
    o_ref[...] = jnp.dot(s, w_ref[...], precision=lax.Precision.HIGHEST,
                         preferred_element_type=jnp.float32) + b_ref[...]


def ada_modulation(c_rows, w_ada, b_ada, tn):
    rows, D = c_rows.shape
    N = w_ada.shape[1]
    return pl.pallas_call(
        _ada_kernel,
        grid=(N // tn,),
        in_specs=[pl.BlockSpec((rows, D), lambda j: (0, 0)),
                  pl.BlockSpec((D, tn), lambda j: (0, j)),
                  pl.BlockSpec((1, tn), lambda j: (0, j))],
        out_specs=pl.BlockSpec((rows, tn), lambda j: (0, j)),
        out_shape=jax.ShapeDtypeStruct((rows, N), jnp.float32),
        compiler_params=pltpu.CompilerParams(dimension_semantics=("arbitrary",), vmem_limit_bytes=VMEM_LIMIT_BYTES),
    )(c_rows, w_ada, b_ada[None, :])


def _ssd_kernel(xf_ref, df_ref, xb_ref, db_ref, dskip_ref, y_ref, h_ref, *, n_ctx_chunks):
    s = pl.program_id(1)
    n_steps = pl.num_programs(1)
    Q, G, R, P, N = SSD_CHUNK, SSD_GROUPS, SSD_HPG, SSD_HEAD_DIM, SSD_STATE
    GP = R * P
    bf = jnp.bfloat16

    @pl.when(s == 0)
    def _():
        h_ref[...] = jnp.zeros_like(h_ref)
        y_ref[...] = jnp.zeros_like(y_ref)

    row = lax.broadcasted_iota(jnp.int32, (Q, Q), 0)
    col = lax.broadcasted_iota(jnp.int32, (Q, Q), 1)
    lane_head = lax.broadcasted_iota(jnp.int32, (Q, GP), 1) // P
    block_head = lax.broadcasted_iota(jnp.int32, (Q, LANES), 1) // P
    is_latent = s >= n_ctx_chunks
    n_lat = n_steps - n_ctx_chunks
    out_chunk = (jnp.clip(s - n_ctx_chunks, 0, n_lat - 1), jnp.clip(n_steps - 1 - s, 0, n_lat - 1))

    for bb, d in [(bb, d) for bb in range(y_ref.shape[0]) for d in range(2)]:
        x_ref, da_ref = ((xf_ref, df_ref), (xb_ref, db_ref))[d]
        mask = (row >= col) if d == 0 else (col >= row)
        tri = mask.astype(jnp.float32)
        da = da_ref[bb]
        cum = jnp.dot(tri, da, precision=lax.Precision.HIGHEST, preferred_element_type=jnp.float32)
        cum_t = cum.T
        edge = Q - 1 if d == 0 else 0
        blk = x_ref.at[bb]
        for g in range(G):
            xg = blk[:, g * GP:(g + 1) * GP]
            bg = blk[:, D_SSD + g * N:D_SSD + (g + 1) * N].astype(bf)
            cg = blk[:, D_SSD + G * N + g * N:D_SSD + G * N + (g + 1) * N].astype(bf)
            heads = [d * SSD_HEADS + g * R + r for r in range(R)]
            dtm = jnp.zeros((Q, GP), jnp.float32)
            cumm = jnp.zeros((Q, GP), jnp.float32)
            for r, h in enumerate(heads):
                sel = lane_head == r
                dtm = jnp.where(sel, da[:, h:h + 1], dtm)
                cumm = jnp.where(sel, cum[:, SSD_HEADS * 2 + h:SSD_HEADS * 2 + h + 1], cumm)
            totm = cumm[edge:edge + 1, :]
            xdt = xg * dtm
            hg = h_ref[bb, d, g * GP:(g + 1) * GP, :]

            gmat = lax.dot_general(cg, bg, (((1,), (1,)), ((), ())), preferred_element_type=jnp.float32)
            y_off = lax.dot_general(cg, hg.astype(bf), (((1,), (1,)), ((), ())),
                                    preferred_element_type=jnp.float32) * jnp.exp(cumm)
            if d == 0:
                y_off = y_off + dskip_ref[:, g * GP:(g + 1) * GP] * xg
            per_block = LANES // P
            parts = [jnp.zeros((Q, LANES), jnp.float32) for _ in range(R // per_block)]
            for r, h in enumerate(heads):
                a_col = cum[:, SSD_HEADS * 2 + h:SSD_HEADS * 2 + h + 1]
                a_row = cum_t[SSD_HEADS * 2 + h:SSD_HEADS * 2 + h + 1, :]
                decay = jnp.exp(jnp.where(mask, a_col - a_row, NEG_BIG))
                j = r // per_block
                xblk = xdt[:, j * LANES:(j + 1) * LANES]
                own = block_head == r % per_block
                parts[j] = parts[j] + jnp.dot((gmat * decay).astype(bf), jnp.where(own, xblk, 0.0).astype(bf),
                                              preferred_element_type=jnp.float32)
            y = jnp.where(is_latent, y_off + jnp.concatenate(parts, axis=-1), 0.0)
            rows = pl.ds(pl.multiple_of(out_chunk[d] * Q, Q), Q)
            y_ref[bb, rows, g * GP:(g + 1) * GP] += y

            xw = (xdt * jnp.exp(totm - cumm)).astype(bf)
            st = lax.dot_general(xw, bg, (((0,), (0,)), ((), ())), preferred_element_type=jnp.float32)
            for r, h in enumerate(heads):
                dec = jnp.exp(cum_t[SSD_HEADS * 2 + h:SSD_HEADS * 2 + h + 1, edge:edge + 1])
                rs = slice(g * GP + r * P, g * GP + (r + 1) * P)
                h_ref[bb, d, rs, :] = h_ref[bb, d, rs, :] * dec + st[r * P:(r + 1) * P, :]


def ssd_scan_bidir(xbc, dta, d_skip, n_ctx, lat_off, nb):
    bsz, lt, width = xbc.shape
    Q = SSD_CHUNK
    n_ctx_chunks = n_ctx // Q
    L = lt - lat_off
    n_lat = L // Q
    n_steps = n_ctx_chunks + n_lat
    lat0 = lat_off // Q

    def fwd_chunk(s):
        return jnp.where(s < n_ctx_chunks, s, s - n_ctx_chunks + lat0)

    def bwd_chunk(s):
        return jnp.where(s < n_ctx_chunks, n_ctx_chunks - 1 - s, n_steps - 1 - s + lat0)

    return pl.pallas_call(
        functools.partial(_ssd_kernel, n_ctx_chunks=n_ctx_chunks),
        grid=(bsz // nb, n_steps),
        in_specs=[
            pl.BlockSpec((nb, Q, width), lambda b, s: (b, fwd_chunk(s), 0)),
            pl.BlockSpec((nb, Q, LANES), lambda b, s: (b, fwd_chunk(s), 0)),
            pl.BlockSpec((nb, Q, width), lambda b, s: (b, bwd_chunk(s), 0)),
            pl.BlockSpec((nb, Q, LANES), lambda b, s: (b, bwd_chunk(s), 0)),
            pl.BlockSpec((1, D_SSD), lambda b, s: (0, 0)),
        ],
        out_specs=pl.BlockSpec((nb, L, D_SSD), lambda b, s: (b, 0, 0)),
        out_shape=jax.ShapeDtypeStruct((bsz, L, D_SSD), jnp.float32),
        scratch_shapes=[pltpu.VMEM((nb, 2, SSD_GROUPS * SSD_HPG * SSD_HEAD_DIM, SSD_STATE), jnp.float32)],
        compiler_params=pltpu.CompilerParams(
            dimension_semantics=("arbitrary", "arbitrary"), vmem_limit_bytes=VMEM_LIMIT_BYTES),
    )(xbc, dta, xbc, dta, d_skip)


def kernel(x, c, ctx, c_ctx, w_ada, b_ada, g_norm1, g_norm2, w_in, hy_conv_w, hy_conv_b, hy_f_w1, hy_f_b1, hy_f_freq, hy_f_w2, hy_f_b2, hy_f_w3, hy_bias, ssd_conv_w, ssd_conv_b, ssd_a_log, ssd_dt_bias, ssd_d, ssd_norm_g, w_out, w_group, b_group, w_expert, b_expert, w1, w3, w2, g_final):
    bsz, seq_len, _ = x.shape
    assert w_in.shape[0] == 1, "single-layer block: the context stream only supplies SSD states"
    l = 0
    rows_pad = -(bsz + 1) % SUBLANES
    c_rows = jnp.concatenate([c, c_ctx[None, :], jnp.zeros((rows_pad, D_MODEL), jnp.float32)], axis=0)
    mod_all = ada_modulation(c_rows, w_ada[l], b_ada[l], ADA_COLS)
    sh1, sc1, ga1, sh2, sc2, ga2 = jnp.split(mod_all[:bsz, None, :], 6, axis=-1)
    csh1, csc1 = mod_all[bsz, :D_MODEL], mod_all[bsz, D_MODEL:2 * D_MODEL]

    w_out_bf = w_out[l].astype(jnp.bfloat16)
    w_in_bf, w_dt_bf, dt_bias2, dt_mult = in_proj_params(w_in[l], ssd_a_log[l], ssd_dt_bias[l],
                                                         HY_COLS, D_SSD, D_XBC)

    ctx_len = ctx.shape[1]
    lat_off = -(-ctx_len // IN_PROJ_ROWS) * IN_PROJ_ROWS
    p_hy, z, xbc, dta = in_proj_fused(ctx, x, g_norm1[l][None, :], csh1[None, :], csc1[None, :], sh1, sc1,
                                      w_in_bf, w_dt_bf, HY_COLS, ssd_conv_w[l], ssd_conv_b[l][None, :],
                                      dt_bias2, dt_mult, GRID_W, ctx_len, IN_PROJ_ROWS, IN_PROJ_COLS)
    kp = hyena_filters_polyphase(seq_len, hy_f_w1[l], hy_f_b1[l], hy_f_freq[l], hy_f_w2[l], hy_f_b2[l],
                                 hy_f_w3[l], D_HYENA, FILTER_ROWS, HYENA_PHASES)
    y_hy = hyena_long_convs(p_hy, hy_conv_w[l], hy_conv_b[l][None, :], kp, hy_bias[l], HYENA_COLS, HYENA_PHASES,
                            GRID_W)
    y_scan = ssd_scan_bidir(xbc, dta, jnp.repeat(ssd_d[l], SSD_HEAD_DIM)[None, :], ctx_len, lat_off, SSD_BATCH)

    pad = LANES - MOE_GROUPS - N_EXPERTS
    w_router = jnp.concatenate([w_group[l], w_expert[l], jnp.zeros((D_MODEL, pad), jnp.float32)], axis=1)
    w_router_hi = w_router.astype(jnp.bfloat16)
    w_router_lo = (w_router - w_router_hi.astype(jnp.float32)).astype(jnp.bfloat16)
    w_router = jnp.stack([w_router_hi, w_router_lo])
    b_router = jnp.concatenate([b_group[l], b_expert[l], jnp.zeros((pad,), jnp.float32)])[None, :]
    x1, hn, route_i, route_f, counts = out_proj_router(
        y_hy, y_scan, z, 0, x, ga1, sc2, sh2, ssd_norm_g[l][None, :], g_norm2[l][None, :],
        w_out_bf, w_router, b_router, TOKEN_ROWS)
    n_tok = bsz * seq_len
    n_blocks = -(-n_tok * TOP_K // MOE_BLOCK) + N_EXPERTS
    dest, block_wsel, block_first, block_valid, block_src = moe_plan(route_i, counts, MOE_BLOCK, n_blocks)
    buf = moe_dispatch(hn.reshape(n_tok, D_MODEL), dest, block_valid, MOE_BLOCK, TOKEN_ROWS)
    yb = expert_blocks(buf, block_wsel, block_first, block_valid, block_src, w1[l], w3[l], w2[l], MOE_BLOCK)
    return moe_combine(x1, route_f, ga2, g_final[None, :], yb, dest, TOKEN_ROWS)
```

```python
import functools
import math

import jax
import jax.numpy as jnp
from jax import lax
from jax.experimental import pallas as pl
from jax.experimental.pallas import tpu as pltpu

D_MODEL = 1024
CTX_LEN = 256
GRID_W = 64
EPS = 1e-6
SHORT_CONV = 3

D_HYENA = D_MODEL // 2
HYENA_ORDER = 2
HYENA_BANDS = 8
HYENA_FAST_DECAY = 0.3
HYENA_SLOW_DECAY = 1.5
HYENA_TARGET = 1e-2
HYENA_PHASES = 4

D_SSD = D_MODEL // 2
SSD_HEAD_DIM = 64
SSD_HEADS = D_SSD // SSD_HEAD_DIM
SSD_GROUPS = 2
SSD_HPG = SSD_HEADS // SSD_GROUPS
SSD_STATE = 128
SSD_CHUNK = 128

D_XBC = D_SSD + 2 * SSD_GROUPS * SSD_STATE
HY_COLS = (HYENA_ORDER + 1) * D_HYENA
D_IN = HY_COLS + D_SSD + D_XBC + 2 * SSD_HEADS
LANES = 128
SUBLANES = 8
D_IN_PAD = -(-D_IN // LANES) * LANES

MOE_GROUPS = 8
EXPERTS_PER_GROUP = 8
N_EXPERTS = MOE_GROUPS * EXPERTS_PER_GROUP
TOP_K = 2
D_EXPERT = 512
MOE_BLOCK = 256
ROUTE_COLS = 8

IN_PROJ_ROWS = 512
IN_PROJ_COLS = 512
TOKEN_ROWS = 1024
HYENA_COLS = 256
FILTER_ROWS = 256
ADA_COLS = 512
SSD_BATCH = 4

VMEM_LIMIT_BYTES = 56 * 1024 * 1024
NEG_BIG = -1e30


def _conv3_rows(p, w_ref, b_ref, cols, has_prev, has_next):
    n = p.shape[0]
    prev = jnp.where(has_prev, pltpu.roll(p, 1, 0), 0.0)
    nxt = jnp.where(has_next, pltpu.roll(p, n - 1, 0), 0.0)
    return b_ref[:, cols] + w_ref[0:1, cols] * prev + w_ref[1:2, cols] * p + w_ref[2:3, cols] * nxt


def _in_proj_kernel(ctx_ref, x_ref, g_ref, csh_ref, csc_ref, sh_ref, sc_ref, w_ref, wdt_ref,
                    sw_ref, sb_ref, dtb_ref, dtm_ref, u_ref, z_ref, xbc_ref, dta_ref, h_ref,
                    *, n_ctx_steps, row_len, ctx_row_len, hy_cols, d_ssd, d_xbc, tn):
    i = pl.program_id(1)
    is_ctx = i < n_ctx_steps
    tm = x_ref.shape[1]
    ctx_t = ctx_ref[0]
    if ctx_t.shape[0] < tm:
        ctx_t = jnp.concatenate([ctx_t, jnp.zeros((tm - ctx_t.shape[0], ctx_t.shape[1]), ctx_t.dtype)], axis=0)
    xin = jnp.where(is_ctx, ctx_t, x_ref[0])
    shift = jnp.where(is_ctx, csh_ref[...], sh_ref[0])
    scale = jnp.where(is_ctx, csc_ref[...], sc_ref[0])
    y = xin * lax.rsqrt(jnp.mean(xin * xin, axis=-1, keepdims=True) + EPS) * g_ref[...]
    h_ref[...] = (y * (1.0 + scale) + shift).astype(jnp.bfloat16)

    pos = lax.broadcasted_iota(jnp.int32, (tm, 1), 0) + jnp.where(is_ctx, i, i - n_ctx_steps) * tm
    in_row = jnp.where(is_ctx, pos % ctx_row_len, pos % row_len)
    has_prev = in_row != 0
    has_next = in_row != jnp.where(is_ctx, ctx_row_len - 1, row_len - 1)

    for c0 in range(0, hy_cols, tn):
        cols = slice(c0, c0 + tn)
        u_ref[0, :, cols] = jnp.dot(h_ref[...], w_ref[:, cols], preferred_element_type=jnp.float32)
    z_ref[0] = jnp.dot(h_ref[...], w_ref[:, hy_cols:hy_cols + d_ssd], preferred_element_type=jnp.float32)

    for c0 in range(0, d_xbc, tn):
        cols = slice(c0, c0 + tn)
        wc = slice(hy_cols + d_ssd + c0, hy_cols + d_ssd + c0 + tn)
        p = jnp.dot(h_ref[...], w_ref[:, wc], preferred_element_type=jnp.float32)
        v = _conv3_rows(p, sw_ref, sb_ref, cols, has_prev, has_next)
        xbc_ref[0, :, cols] = v * jax.nn.sigmoid(v)
    pd = jnp.dot(h_ref[...], wdt_ref[...], preferred_element_type=jnp.float32) + dtb_ref[...]
    sp = jnp.maximum(pd, 0.0) + jnp.log(1.0 + jnp.exp(-jnp.abs(pd)))
    dta_ref[0] = sp * dtm_ref[...]


def in_proj_fused(ctx, x, g1, csh, csc, sh, sc, w_bf, wdt_bf, hy_cols, ssd_w, ssd_b, dt_bias2, dt_mult,
                  row_len, ctx_row_len, tm, tn):
    bsz, L, D = x.shape
    lc = ctx.shape[1]
    d_xbc = ssd_w.shape[1]
    d_ssd = w_bf.shape[1] - hy_cols - d_xbc
    n_ctx_steps = -(-lc // tm)
    ctx_rows = min(lc, tm)
    lc = n_ctx_steps * tm
    n_steps = n_ctx_steps + L // tm
    lat = lambda b, i: (b, jnp.maximum(i - n_ctx_steps, 0), 0)
    allt = lambda b, i: (b, i, 0)
    const2 = lambda b, i: (0, 0)
    per_b = pl.BlockSpec((1, 1, D), lambda b, i: (b, 0, 0))
    kern = functools.partial(_in_proj_kernel, n_ctx_steps=n_ctx_steps, row_len=row_len, ctx_row_len=ctx_row_len,
                             hy_cols=hy_cols, d_ssd=d_ssd, d_xbc=d_xbc, tn=tn)
    return pl.pallas_call(
        kern,
        grid=(bsz, n_steps),
        in_specs=[
            pl.BlockSpec((1, ctx_rows, D), lambda b, i: (b, jnp.minimum(i, n_ctx_steps - 1), 0)),
            pl.BlockSpec((1, tm, D), lat),
            pl.BlockSpec((1, D), const2),
            pl.BlockSpec((1, D), const2),
            pl.BlockSpec((1, D), const2),
            per_b, per_b,
            pl.BlockSpec(w_bf.shape, const2),
            pl.BlockSpec(wdt_bf.shape, const2),
            pl.BlockSpec(ssd_w.shape, const2),
            pl.BlockSpec(ssd_b.shape, const2),
            pl.BlockSpec((1, LANES), const2),
            pl.BlockSpec((1, LANES), const2),
        ],
        out_specs=[
            pl.BlockSpec((1, tm, hy_cols), lat),
            pl.BlockSpec((1, tm, d_ssd), lat),
            pl.BlockSpec((1, tm, d_xbc), allt),
            pl.BlockSpec((1, tm, LANES), allt),
        ],
        out_shape=[
            jax.ShapeDtypeStruct((bsz, L, hy_cols), jnp.float32),
            jax.ShapeDtypeStruct((bsz, L, d_ssd), jnp.float32),
            jax.ShapeDtypeStruct((bsz, lc + L, d_xbc), jnp.float32),
            jax.ShapeDtypeStruct((bsz, lc + L, LANES), jnp.float32),
        ],
        scratch_shapes=[pltpu.VMEM((tm, D), jnp.bfloat16)],
        compiler_params=pltpu.CompilerParams(
            dimension_semantics=("arbitrary", "arbitrary"), vmem_limit_bytes=VMEM_LIMIT_BYTES),
    )(ctx, x, g1, csh, csc, sh, sc, w_bf, wdt_bf, ssd_w, ssd_b, dt_bias2, dt_mult)


def in_proj_params(w_in, a_log, dt_bias, hy_cols, d_ssd, d_xbc):
    n_h = 2 * SSD_HEADS
    main = hy_cols + d_ssd + d_xbc
    w_dt = w_in[:, main:main + n_h]
    pad = jnp.zeros((w_in.shape[0], LANES - 2 * n_h), w_in.dtype)
    wdt = jnp.concatenate([w_dt, w_dt, pad], axis=1).astype(jnp.bfloat16)
    zpad = jnp.zeros((LANES - 2 * n_h,), jnp.float32)
    bias2 = jnp.concatenate([dt_bias.reshape(n_h), dt_bias.reshape(n_h), zpad])[None, :]
    mult = jnp.concatenate([jnp.ones((n_h,), jnp.float32), -jnp.exp(a_log).reshape(n_h), zpad])[None, :]
    return w_in[:, :main].astype(jnp.bfloat16), wdt, bias2, mult


def _out_router_kernel(yh_ref, ys_ref, z_ref, x_ref, ga_ref, sc_ref, sh_ref, ng_ref, g2_ref, wo_ref, wr_ref, br_ref,
                       x1_ref, hn_ref, ri_ref, rf_ref, cnt_ref, carry_ref):
    first = jnp.logical_and(pl.program_id(0) == 0, pl.program_id(1) == 0)

    @pl.when(first)
    def _():
        carry_ref[...] = jnp.zeros_like(carry_ref)

    bf = jnp.bfloat16
    tm = x_ref.shape[1]
    dh = yh_ref.shape[2]
    z = z_ref[0]
    ys = ys_ref[0] * (z * jax.nn.sigmoid(z))
    gw = ys.shape[1] // SSD_GROUPS
    acc = jnp.dot(yh_ref[0].astype(bf), wo_ref[0:dh, :], preferred_element_type=jnp.float32)
    for g in range(SSD_GROUPS):
        yg = ys[:, g * gw:(g + 1) * gw]
        yg = yg * lax.rsqrt(jnp.mean(yg * yg, axis=-1, keepdims=True) + EPS) * ng_ref[:, g * gw:(g + 1) * gw]
        acc += jnp.dot(yg.astype(bf), wo_ref[dh + g * gw:dh + (g + 1) * gw, :], preferred_element_type=jnp.float32)
    x1 = x_ref[0] + ga_ref[0] * acc
    x1_ref[0] = x1
    hn = x1 * lax.rsqrt(jnp.mean(x1 * x1, axis=-1, keepdims=True) + EPS) * g2_ref[...]
    hn = hn * (1.0 + sc_ref[0]) + sh_ref[0]
    hn_ref[0] = hn

    hn_hi = hn.astype(bf)
    hn_lo = (hn - hn_hi.astype(jnp.float32)).astype(bf)
    logits = (jnp.dot(hn_hi, wr_ref[0], preferred_element_type=jnp.float32)
              + jnp.dot(hn_lo, wr_ref[0], preferred_element_type=jnp.float32)
              + jnp.dot(hn_hi, wr_ref[1], preferred_element_type=jnp.float32)) + br_ref[...]
    lane = lax.broadcasted_iota(jnp.int32, (tm, LANES), 1)
    lane_f = lane.astype(jnp.float32)
    ninf = jnp.float32(-jnp.inf)
    big = jnp.float32(1e9)
    gl = jnp.where(lane < MOE_GROUPS, logits, ninf)
    gmax = jnp.max(gl, axis=-1, keepdims=True)
    p_group = 1.0 / jnp.sum(jnp.exp(gl - gmax), axis=-1, keepdims=True)
    g_sel = jnp.min(jnp.where(gl == gmax, lane_f, big), axis=-1, keepdims=True)
    e_lane = lane - MOE_GROUPS
    in_grp = jnp.logical_and(e_lane >= 0, (e_lane // EXPERTS_PER_GROUP).astype(jnp.float32) == g_sel)
    el = jnp.where(in_grp, logits, ninf)
    m1 = jnp.max(el, axis=-1, keepdims=True)
    i1 = jnp.min(jnp.where(el == m1, lane_f, big), axis=-1, keepdims=True)
    el2 = jnp.where(lane_f == i1, ninf, el)
    m2 = jnp.max(el2, axis=-1, keepdims=True)
    i2 = jnp.min(jnp.where(el2 == m2, lane_f, big), axis=-1, keepdims=True)
    t = jnp.exp(m2 - m1)
    w1 = 1.0 / (1.0 + t)
    gate1 = w1 * p_group
    gate2 = (t * w1) * p_group
    e1 = i1 - MOE_GROUPS
    e2 = i2 - MOE_GROUPS
    el_f = e_lane.astype(jnp.float32)
    oh1 = el_f == e1
    oh2 = el_f == e2
    oh = jnp.logical_or(oh1, oh2).astype(bf)
    r_i = lax.broadcasted_iota(jnp.int32, (tm, tm), 0)
    c_i = lax.broadcasted_iota(jnp.int32, (tm, tm), 1)
    before = jnp.dot((c_i < r_i).astype(bf), oh, preferred_element_type=jnp.float32) + carry_ref[...]
    rank1 = jnp.sum(jnp.where(oh1, before, 0.0), axis=-1, keepdims=True)
    rank2 = jnp.sum(jnp.where(oh2, before, 0.0), axis=-1, keepdims=True)
    carry_ref[...] += jnp.sum(oh.astype(jnp.float32), axis=0, keepdims=True)
    cnt_ref[...] = carry_ref[...]

    rec = jnp.where(lane == 0, e1, jnp.where(lane == 1, e2, jnp.where(lane == 2, rank1,
                                                                      jnp.where(lane == 3, rank2, 0.0))))
    ri_ref[0] = rec.T[0:ROUTE_COLS, :].astype(jnp.int32)
    col = lax.broadcasted_iota(jnp.int32, (tm, ROUTE_COLS), 1)
    rf_ref[0] = jnp.where(col == 0, gate1, gate2)


def out_proj_router(y_hy, y_scan, px, z_col, x, ga1, sc2, sh2, norm_g, g2, w_out_bf, w_router, b_router, tm):
    bsz, L, D = x.shape
    dh = y_hy.shape[-1]
    ds = y_scan.shape[-1]
    tok = lambda b, i: (b, i, 0)
    per_b = pl.BlockSpec((1, 1, D), lambda b, i: (b, 0, 0))
    const2 = lambda b, i: (0, 0)
    return pl.pallas_call(
        _out_router_kernel,
        grid=(bsz, L // tm),
        in_specs=[
            pl.BlockSpec((1, tm, dh), tok),
            pl.BlockSpec((1, tm, ds), tok),
            pl.BlockSpec((1, tm, ds), lambda b, i: (b, i, z_col)),
            pl.BlockSpec((1, tm, D), tok),
            per_b, per_b, per_b,
            pl.BlockSpec((1, ds), const2),
            pl.BlockSpec((1, D), const2),
            pl.BlockSpec((dh + ds, D), const2),
            pl.BlockSpec((2, D, LANES), lambda b, i: (0, 0, 0)),
            pl.BlockSpec((1, LANES), const2),
        ],
        out_specs=[
            pl.BlockSpec((1, tm, D), tok),
            pl.BlockSpec((1, tm, D), tok),
            pl.BlockSpec((1, ROUTE_COLS, tm), lambda b, i: (b, 0, i)),
            pl.BlockSpec((1, tm, ROUTE_COLS), tok),
            pl.BlockSpec((1, LANES), const2),
        ],
        out_shape=[
            jax.ShapeDtypeStruct((bsz, L, D), jnp.float32),
            jax.ShapeDtypeStruct((bsz, L, D), jnp.float32),
            jax.ShapeDtypeStruct((bsz, ROUTE_COLS, L), jnp.int32),
            jax.ShapeDtypeStruct((bsz, L, ROUTE_COLS), jnp.float32),
            jax.ShapeDtypeStruct((1, LANES), jnp.float32),
        ],
        scratch_shapes=[pltpu.VMEM((1, LANES), jnp.float32)],
        compiler_params=pltpu.CompilerParams(
            dimension_semantics=("arbitrary", "arbitrary"), vmem_limit_bytes=VMEM_LIMIT_BYTES),
    )(y_hy, y_scan, px, x, ga1, sc2, sh2, norm_g, g2, w_out_bf, w_router, b_router)


def _row_copy(src_hbm, src_row, dst_ref, dst_row, sem):
    return pltpu.make_async_copy(src_hbm.at[pl.ds(src_row, 1), :], dst_ref.at[pl.ds(dst_row, 1), :], sem)


def _dispatch_kernel(dest_ref, valid_ref, hn_ref, buf_hbm, zeros, sem, zsem):
    step = pl.program_id(0)
    tm = hn_ref.shape[0]
    n_tok = pl.num_programs(0) * tm
    blk = zeros.shape[0]
    n_blocks = buf_hbm.shape[0] // blk

    def zero_copy(i):
        return pltpu.make_async_copy(zeros, buf_hbm.at[pl.ds(pl.multiple_of(i * blk, blk), blk), :], zsem)

    def zfill(i, carry):
        @pl.when(valid_ref[i] < blk)
        def _():
            zero_copy(i).start()
        return carry

    def zwait(i, carry):
        @pl.when(valid_ref[i] < blk)
        def _():
            zero_copy(i).wait()
        return carry

    @pl.when(step == 0)
    def _():
        zeros[...] = jnp.zeros_like(zeros)
        lax.fori_loop(0, n_blocks, zfill, 0)
        lax.fori_loop(0, n_blocks, zwait, 0)

    def body(j, carry):
        t = step * tm + j
        _row_copy(hn_ref, j, buf_hbm, dest_ref[t], sem).start()
        _row_copy(hn_ref, j, buf_hbm, dest_ref[n_tok + t], sem).start()
        return carry

    lax.fori_loop(0, tm, body, 0, unroll=8)
    for _ in range(2):
        pltpu.make_async_copy(hn_ref, buf_hbm.at[pl.ds(0, tm), :], sem).wait()


def moe_dispatch(hn, dest, block_valid, blk, tm):
    T, D = hn.shape
    n_rows = block_valid.shape[0] * blk
    grid_spec = pltpu.PrefetchScalarGridSpec(
        num_scalar_prefetch=2,
        grid=(T // tm,),
        in_specs=[pl.BlockSpec((tm, D), lambda i, d, v: (i, 0))],
        out_specs=pl.BlockSpec(memory_space=pl.ANY),
        scratch_shapes=[pltpu.VMEM((blk, D), hn.dtype), pltpu.SemaphoreType.DMA(()),
                        pltpu.SemaphoreType.DMA(())],
    )
    return pl.pallas_call(
        _dispatch_kernel,
        grid_spec=grid_spec,
        out_shape=jax.ShapeDtypeStruct((n_rows, D), hn.dtype),
        compiler_params=pltpu.CompilerParams(dimension_semantics=("arbitrary",), has_side_effects=True),
    )(dest, block_valid, hn)


def _expert_kernel(wsel_ref, first_ref, valid_ref, src_ref, x_ref, w1_ref, w3_ref, w2_ref, o_ref, w1b, w3b, w2b):
    i = pl.program_id(0)
    del wsel_ref, src_ref
    bf = jnp.bfloat16

    @pl.when(first_ref[i] == 1)
    def _():
        w1b[...] = w1_ref[0].astype(bf)
        w3b[...] = w3_ref[0].astype(bf)
        w2b[...] = w2_ref[0].astype(bf)

    valid = valid_ref[i]

    @pl.when(valid > 0)
    def _():
        xb = x_ref[...].astype(bf)
        a = jnp.dot(xb, w1b[...], preferred_element_type=jnp.float32)
        b = jnp.dot(xb, w3b[...], preferred_element_type=jnp.float32)
        h = (a * jax.nn.sigmoid(a)) * b
        o_ref[...] = jnp.dot(h.astype(bf), w2b[...], preferred_element_type=jnp.float32)

    @pl.when(valid <= 0)
    def _():
        o_ref[...] = jnp.zeros_like(o_ref)


def expert_blocks(buf, block_wsel, block_first, block_valid, block_src, w1, w3, w2, blk):
    rows, D = buf.shape
    n_blocks = rows // blk
    E, _, F = w1.shape
    grid_spec = pltpu.PrefetchScalarGridSpec(
        num_scalar_prefetch=4,
        grid=(n_blocks,),
        in_specs=[
            pl.BlockSpec((blk, D), lambda i, ws, fi, va, src: (src[i], 0)),
            pl.BlockSpec((1, D, F), lambda i, ws, fi, va, src: (ws[i], 0, 0)),
            pl.BlockSpec((1, D, F), lambda i, ws, fi, va, src: (ws[i], 0, 0)),
            pl.BlockSpec((1, F, D), lambda i, ws, fi, va, src: (ws[i], 0, 0)),
        ],
        out_specs=pl.BlockSpec((blk, D), lambda i, ws, fi, va, src: (i, 0)),
        scratch_shapes=[pltpu.VMEM((D, F), jnp.bfloat16), pltpu.VMEM((D, F), jnp.bfloat16),
                        pltpu.VMEM((F, D), jnp.bfloat16)],
    )
    return pl.pallas_call(
        _expert_kernel,
        grid_spec=grid_spec,
        out_shape=jax.ShapeDtypeStruct((rows, D), jnp.float32),
        compiler_params=pltpu.CompilerParams(
            dimension_semantics=("arbitrary",), vmem_limit_bytes=VMEM_LIMIT_BYTES),
    )(block_wsel, block_first, block_valid, block_src, buf, w1, w3, w2)


def _combine_kernel(dest_ref, x1_ref, rf_ref, ga_ref, gf_ref, yb_hbm, o_ref, ybuf, sem):
    b = pl.program_id(0)
    i = pl.program_id(1)
    n_i = pl.num_programs(1)
    tm = x1_ref.shape[1]
    step = b * n_i + i
    n_steps = pl.num_programs(0) * n_i
    slot = step % 2

    def issue(step_, slot_):
        def body(j, carry):
            t = step_ * tm + j
            _row_copy(yb_hbm, dest_ref[t], ybuf.at[slot_, 0], j, sem.at[slot_]).start()
            _row_copy(yb_hbm, dest_ref[n_steps * tm + t], ybuf.at[slot_, 1], j, sem.at[slot_]).start()
            return carry
        lax.fori_loop(0, tm, body, 0, unroll=8)

    @pl.when(step == 0)
    def _():
        issue(0, 0)

    @pl.when(step + 1 < n_steps)
    def _():
        issue(step + 1, 1 - slot)

    pltpu.make_async_copy(yb_hbm.at[pl.ds(0, tm), :], ybuf.at[slot, 0], sem.at[slot]).wait()
    pltpu.make_async_copy(yb_hbm.at[pl.ds(0, tm), :], ybuf.at[slot, 1], sem.at[slot]).wait()
    rf = rf_ref[0]
    y = rf[:, 0:1] * ybuf[slot, 0] + rf[:, 1:2] * ybuf[slot, 1]
    x2 = x1_ref[0] + ga_ref[0] * y
    o_ref[0] = x2 * lax.rsqrt(jnp.mean(x2 * x2, axis=-1, keepdims=True) + EPS) * gf_ref[...]


def moe_combine(x1, route_f, ga2, g_final, yb, dest, tm):
    bsz, L, D = x1.shape
    grid_spec = pltpu.PrefetchScalarGridSpec(
        num_scalar_prefetch=1,
        grid=(bsz, L // tm),
        in_specs=[
            pl.BlockSpec((1, tm, D), lambda b, i, d: (b, i, 0)),
            pl.BlockSpec((1, tm, ROUTE_COLS), lambda b, i, d: (b, i, 0)),
            pl.BlockSpec((1, 1, D), lambda b, i, d: (b, 0, 0)),
            pl.BlockSpec((1, D), lambda b, i, d: (0, 0)),
            pl.BlockSpec(memory_space=pl.ANY),
        ],
        out_specs=pl.BlockSpec((1, tm, D), lambda b, i, d: (b, i, 0)),
        scratch_shapes=[pltpu.VMEM((2, 2, tm, D), jnp.float32), pltpu.SemaphoreType.DMA((2,))],
    )
    return pl.pallas_call(
        _combine_kernel,
        grid_spec=grid_spec,
        out_shape=jax.ShapeDtypeStruct((bsz, L, D), jnp.float32),
        compiler_params=pltpu.CompilerParams(
            dimension_semantics=("arbitrary", "arbitrary"), vmem_limit_bytes=VMEM_LIMIT_BYTES),
    )(dest, x1, route_f, ga2, g_final, yb)


def moe_plan(route_i, counts, blk, n_blocks):
    cnt = counts[0, MOE_GROUPS:MOE_GROUPS + N_EXPERTS].astype(jnp.int32)
    padded = (cnt + blk - 1) // blk * blk
    ends = jnp.cumsum(padded)
    starts = ends - padded
    experts = jnp.arange(N_EXPERTS, dtype=jnp.int32)
    dest = jnp.concatenate([
        (jnp.sum(jnp.where(route_i[:, k, :, None] == experts, starts, 0), axis=-1) + route_i[:, 2 + k]).reshape(-1)
        for k in range(TOP_K)])
    first_row = jnp.arange(n_blocks, dtype=jnp.int32) * blk
    block_eid = jnp.minimum(jnp.sum((ends[None, :] <= first_row[:, None]).astype(jnp.int32), axis=1), N_EXPERTS - 1)
    block_valid = jnp.clip(cnt[block_eid] - (first_row - starts[block_eid]), 0, blk).astype(jnp.int32)
    block_first = jnp.concatenate([jnp.ones((1,), jnp.int32),
                                   (block_eid[1:] != block_eid[:-1]).astype(jnp.int32)])
    idx = jnp.arange(n_blocks, dtype=jnp.int32)
    next_first = lax.cummin(jnp.where(block_first == 1, idx, n_blocks), axis=0, reverse=True)
    block_wsel = jnp.where(next_first < n_blocks, block_eid[jnp.minimum(next_first, n_blocks - 1)], block_eid)
    n_used = jnp.sum((block_valid > 0).astype(jnp.int32))
    block_src = jnp.minimum(idx, jnp.maximum(n_used - 1, 0))
    return dest, block_wsel, block_first, block_valid, block_src


def dft_tables(L):
    n = 2 * L
    f = lax.broadcasted_iota(jnp.int32, (L, L), 0)
    t = lax.broadcasted_iota(jnp.int32, (L, L), 1)
    ang = ((f * t) % n).astype(jnp.float32) * (2.0 * math.pi / n)
    return jnp.cos(ang).astype(jnp.bfloat16), jnp.sin(ang).astype(jnp.bfloat16)


def _alt_sign(L):
    t = lax.broadcasted_iota(jnp.int32, (L, 1), 0)
    return (1 - 2 * (t & 1)).astype(jnp.float32)


def _spectrum_kernel(a_ref, b_ref, c_ref, s_ref, kr_ref, ks_ref, kn_ref):
    L = a_ref.shape[1]
    a = a_ref[0]
    row = lax.broadcasted_iota(jnp.int32, (L, 1), 0)
    scale = jnp.where(row == 0, 0.5 / L, 1.0 / L)
    kr_ref[0] = scale * jnp.dot(c_ref[...], a.astype(jnp.bfloat16), preferred_element_type=jnp.float32)
    ks_ref[0] = scale * jnp.dot(s_ref[...], b_ref[0].astype(jnp.bfloat16), preferred_element_type=jnp.float32)
    kn_ref[0] = jnp.sum(a * _alt_sign(L), axis=0, keepdims=True) * (0.5 / L)


def filter_spectrum(a, b, cos_t, sin_t, tc):
    n, L, C = a.shape
    blk = pl.BlockSpec((1, L, tc), lambda o, j: (o, 0, j))
    tab = pl.BlockSpec((L, L), lambda o, j: (0, 0))
    return pl.pallas_call(
        _spectrum_kernel,
        grid=(n, C // tc),
        in_specs=[blk, blk, tab, tab],
        out_specs=[blk, blk, pl.BlockSpec((1, 1, tc), lambda o, j: (o, 0, j))],
        out_shape=[jax.ShapeDtypeStruct((n, L, C), jnp.float32)] * 2 + [jax.ShapeDtypeStruct((n, 1, C), jnp.float32)],
        compiler_params=pltpu.CompilerParams(
            dimension_semantics=("arbitrary", "arbitrary"), vmem_limit_bytes=VMEM_LIMIT_BYTES),
    )(a, b, cos_t, sin_t)


def _phase_conv3(raw, w_ref, b_ref, rows_per_phase):
    n_ph = len(raw)
    h = raw[0].shape[0]
    j = lax.broadcasted_iota(jnp.int32, (h, 1), 0) % rows_per_phase
    prev0 = jnp.where(j != 0, pltpu.roll(raw[n_ph - 1], 1, 0), 0.0)
    next_last = jnp.where(j != rows_per_phase - 1, pltpu.roll(raw[0], h - 1, 0), 0.0)
    out = []
    for p in range(n_ph):
        prev = raw[p - 1] if p > 0 else prev0
        nxt = raw[p + 1] if p < n_ph - 1 else next_last
        out.append(b_ref[...] + w_ref[0:1, :] * prev + w_ref[1:2, :] * raw[p] + w_ref[2:3, :] * nxt)
    return out


def _long_conv_kernel(*refs, n_ph, n_slab, conv_z, rows_per_phase):
    z_refs = refs[:n_slab]
    xn_refs = refs[n_slab:2 * n_slab]
    (kr_ref, ks_ref, kn_ref, bias_ref, cwz_ref, cbz_ref, cwx_ref, cbx_ref, c_ref, s_ref,
     o_ref, acc_ref, zr_ref, zs_ref, yr_ref, ys_ref, stage_ref) = refs[2 * n_slab:]
    H = z_refs[0].shape[1] // n_ph
    f32 = jnp.float32
    bf = jnp.bfloat16
    sign = _alt_sign(H)

    def phases(slab_refs):
        return [jnp.concatenate([r[0, pl.ds(p, H, stride=n_ph), :] for r in slab_refs], axis=1)
                for p in range(n_ph)]

    z_ph = phases(z_refs)
    if conv_z:
        z_ph = _phase_conv3(z_ph, cwz_ref, cbz_ref, rows_per_phase)
    for q in range(n_ph):
        zb = z_ph[q].astype(bf)
        zr_ref[q] = jnp.dot(c_ref[...], zb, preferred_element_type=f32)
        zs_ref[q] = jnp.dot(s_ref[...], zb, preferred_element_type=f32)
    z_nyq = [jnp.sum(z * sign, axis=0, keepdims=True) for z in z_ph]
    for p in range(n_ph):
        nyq = sum(z_nyq[q] * kn_ref[p - q + n_ph - 1] for q in range(n_ph))
        acc_ref[p] = z_ph[p] * bias_ref[0] + sign * nyq
        yr = 0.0
        ys = 0.0
        for q in range(n_ph):
            slot = p - q + n_ph - 1
            yr = yr + zr_ref[q] * kr_ref[slot] - zs_ref[q] * ks_ref[slot]
            ys = ys + zr_ref[q] * ks_ref[slot] + zs_ref[q] * kr_ref[slot]
        yr_ref[p] = yr.astype(bf)
        ys_ref[p] = ys.astype(bf)
    for p in range(n_ph):
        acc_ref[p] += (jnp.dot(c_ref[...], yr_ref[p], preferred_element_type=f32)
                       + jnp.dot(s_ref[...], ys_ref[p], preferred_element_type=f32))
    x_ph = _phase_conv3(phases(xn_refs), cwx_ref, cbx_ref, rows_per_phase)
    for p in range(n_ph):
        out_p = x_ph[p] * acc_ref[p]
        for sl in range(n_slab):
            stage_ref[sl, pl.ds(p, H, stride=n_ph), :] = out_p[:, sl * LANES:(sl + 1) * LANES]
    for sl in range(n_slab):
        o_ref[0, :, sl * LANES:(sl + 1) * LANES] = stage_ref[sl].astype(o_ref.dtype)


def long_conv_gate(z_arr, z_col, conv_z, xn_arr, xn_col, conv_w, conv_b, kr, ks, kn, bias, cos_t, sin_t,
                   tc, n_ph, row_len, out_dtype):
    bsz, L, _ = z_arr.shape
    H = L // n_ph
    C = kr.shape[-1]
    nj = C // tc
    n_slab = tc // LANES
    n_f = 2 * n_ph - 1
    tab = pl.BlockSpec((H, H), lambda j, b: (0, 0), pipeline_mode=pl.Buffered(1))
    spec = pl.BlockSpec((n_f, H, tc), lambda j, b: (0, 0, j), pipeline_mode=pl.Buffered(1))
    nyq = pl.BlockSpec((n_f, 1, tc), lambda j, b: (0, 0, j))
    vec = pl.BlockSpec((1, 1, tc), lambda j, b: (0, 0, j))

    def slabs(col):
        return [pl.BlockSpec((1, L, LANES),
                             functools.partial(lambda j, b, sl: (b, 0, (col * nj + j) * n_slab + sl), sl=sl))
                for sl in range(n_slab)]

    def conv_specs(col):
        return [pl.BlockSpec((3, tc), lambda j, b: (0, col * nj + j)),
                pl.BlockSpec((1, tc), lambda j, b: (0, col * nj + j))]

    zc = z_col if conv_z else 0
    kern = functools.partial(_long_conv_kernel, n_ph=n_ph, n_slab=n_slab, conv_z=conv_z,
                             rows_per_phase=row_len // n_ph)
    return pl.pallas_call(
        kern,
        grid=(nj, bsz),
        in_specs=(slabs(z_col) + slabs(xn_col) + [spec, spec, nyq, vec] + conv_specs(zc) + conv_specs(xn_col)
                  + [tab, tab]),
        out_specs=pl.BlockSpec((1, L, tc), lambda j, b: (b, 0, j)),
        out_shape=jax.ShapeDtypeStruct((bsz, L, C), out_dtype),
        scratch_shapes=[pltpu.VMEM((n_ph, H, tc), jnp.float32), pltpu.VMEM((n_ph, H, tc), jnp.float32),
                        pltpu.VMEM((n_ph, H, tc), jnp.float32), pltpu.VMEM((n_ph, H, tc), jnp.bfloat16),
                        pltpu.VMEM((n_ph, H, tc), jnp.bfloat16), pltpu.VMEM((n_slab, L, LANES), jnp.float32)],
        compiler_params=pltpu.CompilerParams(
            dimension_semantics=("arbitrary", "arbitrary"), vmem_limit_bytes=VMEM_LIMIT_BYTES),
    )(*([z_arr] * n_slab), *([xn_arr] * n_slab), kr, ks, kn, bias, conv_w, conv_b, conv_w, conv_b, cos_t, sin_t)


def _polyphase_taps(kf, kb, n_ph):
    H = kf.shape[0] // n_ph
    ph = lambda a, p: a[p * H:(p + 1) * H]
    zero = jnp.zeros_like(kf[:1])
    plus, minus = [], []
    for r in range(-(n_ph - 1), n_ph):
        if r >= 0:
            plus.append(ph(kf, r))
        else:
            plus.append(jnp.concatenate([ph(kb, -r)[0:1], ph(kf, n_ph + r)[:-1]], axis=0))
        if r <= 0:
            minus.append(jnp.concatenate([zero, ph(kb, -r)[1:]], axis=0))
        else:
            minus.append(jnp.concatenate([zero, ph(kb, n_ph - r)[:-1]], axis=0))
    return jnp.stack(plus), jnp.stack(minus)


def hyena_long_convs(p_hy, conv_w, conv_b, kp, h_bias, tc, n_ph, row_len):
    L = p_hy.shape[1]
    C = h_bias.shape[1]
    cos_t, sin_t = dft_tables(L // n_ph)
    z = p_hy
    for o in range(h_bias.shape[0]):
        fwd = slice((2 * o) * C, (2 * o + 1) * C)
        bwd = slice((2 * o + 1) * C, (2 * o + 2) * C)
        plus, minus = _polyphase_taps(kp[:, fwd], kp[:, bwd], n_ph)
        kr, ks, kn = filter_spectrum(plus + minus, plus - minus, cos_t, sin_t, tc)
        last = o == h_bias.shape[0] - 1
        z = long_conv_gate(z, 0, o == 0, p_hy, o + 1, conv_w, conv_b, kr, ks, kn, h_bias[o][None, None, :],
                           cos_t, sin_t, tc, n_ph, row_len, jnp.bfloat16 if last else jnp.float32)
    return z


def _filter_kernel(band_ref, w1_ref, b1_ref, fr_ref, w2_ref, b2_ref, w3_ref, dl_ref, k_ref, *, seq_len, n_ph):
    hp = lax.Precision.HIGHEST
    f32 = jnp.float32
    tp = k_ref.shape[0]
    per_phase = seq_len // n_ph
    g = lax.broadcasted_iota(jnp.int32, (tp, 1), 0) + pl.program_id(0) * tp
    phase = g // per_phase
    pos = (n_ph * (g - phase * per_phase) + phase).astype(f32)
    t = pos / max(seq_len - 1, 1)
    ang = (2 * math.pi / seq_len) * pos * band_ref[...]
    lane = lax.broadcasted_iota(jnp.int32, (tp, LANES), 1)
    feats = jnp.where(lane == 0, t,
                      jnp.where(lane <= HYENA_BANDS, jnp.cos(ang),
                                jnp.where(lane <= 2 * HYENA_BANDS, -jnp.sin(ang), 0.0)))
    h = jnp.sin(fr_ref[...] * (jnp.dot(feats, w1_ref[...], precision=hp, preferred_element_type=f32) + b1_ref[...]))
    h = jnp.sin(fr_ref[...] * (jnp.dot(h, w2_ref[...], precision=hp, preferred_element_type=f32) + b2_ref[...]))
    window = jnp.exp(-t * dl_ref[...])
    c = dl_ref.shape[1]
    for j in range(w3_ref.shape[1] // c):
        cols = slice(j * c, (j + 1) * c)
        k_ref[:, cols] = jnp.dot(h, w3_ref[:, cols], precision=hp, preferred_element_type=f32) * window


def hyena_filters_polyphase(seq_len, f_w1, f_b1, f_freq, f_w2, f_b2, f_w3, d_hyena, tp, n_ph):
    f32 = jnp.float32
    fh = f_w1.shape[1]
    n_emb = 1 + 2 * HYENA_BANDS
    bands = jnp.linspace(1e-4, HYENA_BANDS - 1, HYENA_BANDS, dtype=f32)
    band_row = jnp.concatenate([jnp.zeros((1,), f32), bands, bands, jnp.zeros((LANES - n_emb,), f32)])[None, :]
    w1p = jnp.concatenate([f_w1, jnp.zeros((LANES - n_emb, fh), f32)], axis=0)
    deltas = jnp.abs(jnp.linspace(math.log(HYENA_TARGET) / HYENA_SLOW_DECAY,
                                  math.log(HYENA_TARGET) / HYENA_FAST_DECAY, d_hyena, dtype=f32))[None, :]
    n_out = f_w3.shape[1]
    full = lambda a: pl.BlockSpec(a.shape, lambda i: (0,) * a.ndim)
    args = (band_row, w1p, f_b1[None, :], f_freq[None, :], f_w2, f_b2[None, :], f_w3, deltas)
    return pl.pallas_call(
        functools.partial(_filter_kernel, seq_len=seq_len, n_ph=n_ph),
        grid=(seq_len // tp,),
        in_specs=[full(a) for a in args],
        out_specs=pl.BlockSpec((tp, n_out), lambda i: (i, 0)),
        out_shape=jax.ShapeDtypeStruct((seq_len, n_out), f32),
        compiler_params=pltpu.CompilerParams(dimension_semantics=("arbitrary",), vmem_limit_bytes=VMEM_LIMIT_BYTES),
    )(*args)


def _ada_kernel(c_ref, w_ref, b_ref, o_ref):
    cv = c_ref[...]
    s = cv * jax.nn.sigmoid(cv)
    o_ref[...] = jnp.dot(s, w_ref[...], precision=lax.Precision.HIGHEST,
                         preferred_element_type=jnp.float32) + b_ref[...]


def ada_modulation(c_rows, w_ada, b_ada, tn):
    rows, D = c_rows.shape
    N = w_ada.shape[1]
    return pl.pallas_call(
        _ada_kernel,
        grid=(N // tn,),
        in_specs=[pl.BlockSpec((rows, D), lambda j: (0, 0)),
                  pl.BlockSpec((D, tn), lambda j: (0, j)),
                  pl.BlockSpec((1, tn), lambda j: (0, j))],
        out_specs=pl.BlockSpec((rows, tn), lambda j: (0, j)),
        out_shape=jax.ShapeDtypeStruct((rows, N), jnp.float32),
        compiler_params=pltpu.CompilerParams(dimension_semantics=("arbitrary",), vmem_limit_bytes=VMEM_LIMIT_BYTES),
    )(c_rows, w_ada, b_ada[None, :])


def _ssd_kernel(xf_ref, df_ref, xb_ref, db_ref, dskip_ref, y_ref, h_ref, *, n_ctx_chunks):
    s = pl.program_id(1)
    n_steps = pl.num_programs(1)
    Q, G, R, P, N = SSD_CHUNK, SSD_GROUPS, SSD_HPG, SSD_HEAD_DIM, SSD_STATE
    GP = R * P
    bf = jnp.bfloat16

    @pl.when(s == 0)
    def _():
        h_ref[...] = jnp.zeros_like(h_ref)
        y_ref[...] = jnp.zeros_like(y_ref)

    row = lax.broadcasted_iota(jnp.int32, (Q, Q), 0)
    col = lax.broadcasted_iota(jnp.int32, (Q, Q), 1)
    lane_head = lax.broadcasted_iota(jnp.int32, (Q, GP), 1) // P
    block_head = lax.broadcasted_iota(jnp.int32, (Q, LANES), 1) // P
    is_latent = s >= n_ctx_chunks
    n_lat = n_steps - n_ctx_chunks
    out_chunk = (jnp.clip(s - n_ctx_chunks, 0, n_lat - 1), jnp.clip(n_steps - 1 - s, 0, n_lat - 1))

    for bb, d in [(bb, d) for bb in range(y_ref.shape[0]) for d in range(2)]:
        x_ref, da_ref = ((xf_ref, df_ref), (xb_ref, db_ref))[d]
        mask = (row >= col) if d == 0 else (col >= row)
        tri = mask.astype(jnp.float32)
        da = da_ref[bb]
        cum = jnp.dot(tri, da, precision=lax.Precision.HIGHEST, preferred_element_type=jnp.float32)
        cum_t = cum.T
        edge = Q - 1 if d == 0 else 0
        blk = x_ref.at[bb]
        for g in range(G):
            xg = blk[:, g * GP:(g + 1) * GP]
            bg = blk[:, D_SSD + g * N:D_SSD + (g + 1) * N].astype(bf)
            cg = blk[:, D_SSD + G * N + g * N:D_SSD + G * N + (g + 1) * N].astype(bf)
            heads = [d * SSD_HEADS + g * R + r for r in range(R)]
            dtm = jnp.zeros((Q, GP), jnp.float32)
            cumm = jnp.zeros((Q, GP), jnp.float32)
            for r, h in enumerate(heads):
                sel = lane_head == r
                dtm = jnp.where(sel, da[:, h:h + 1], dtm)
                cumm = jnp.where(sel, cum[:, SSD_HEADS * 2 + h:SSD_HEADS * 2 + h + 1], cumm)
            totm = cumm[edge:edge + 1, :]
            xdt = xg * dtm
            hg = h_ref[bb, d, g * GP:(g + 1) * GP, :]

            gmat = lax.dot_general(cg, bg, (((1,), (1,)), ((), ())), preferred_element_type=jnp.float32)
            y_off = lax.dot_general(cg, hg.astype(bf), (((1,), (1,)), ((), ())),
                                    preferred_element_type=jnp.float32) * jnp.exp(cumm)
            if d == 0:
                y_off = y_off + dskip_ref[:, g * GP:(g + 1) * GP] * xg
            per_block = LANES // P
            parts = [jnp.zeros((Q, LANES), jnp.float32) for _ in range(R // per_block)]
            for r, h in enumerate(heads):
                a_col = cum[:, SSD_HEADS * 2 + h:SSD_HEADS * 2 + h + 1]
                a_row = cum_t[SSD_HEADS * 2 + h:SSD_HEADS * 2 + h + 1, :]
                decay = jnp.exp(jnp.where(mask, a_col - a_row, NEG_BIG))
                j = r // per_block
                xblk = xdt[:, j * LANES:(j + 1) * LANES]
                own = block_head == r % per_block
                parts[j] = parts[j] + jnp.dot((gmat * decay).astype(bf), jnp.where(own, xblk, 0.0).astype(bf),
                                              preferred_element_type=jnp.float32)
            y = jnp.where(is_latent, y_off + jnp.concatenate(parts, axis=-1), 0.0)
            rows = pl.ds(pl.multiple_of(out_chunk[d] * Q, Q), Q)
            y_ref[bb, rows, g * GP:(g + 1) * GP] += y

            xw = (xdt * jnp.exp(totm - cumm)).astype(bf)
            st = lax.dot_general(xw, bg, (((0,), (0,)), ((), ())), preferred_element_type=jnp.float32)
            for r, h in enumerate(heads):
                dec = jnp.exp(cum_t[SSD_HEADS * 2 + h:SSD_HEADS * 2 + h + 1, edge:edge + 1])
                rs = slice(g * GP + r * P, g * GP + (r + 1) * P)
                h_ref[bb, d, rs, :] = h_ref[bb, d, rs, :] * dec + st[r * P:(r + 1) * P, :]


def ssd_scan_bidir(xbc, dta, d_skip, n_ctx, lat_off, nb):
    bsz, lt, width = xbc.shape
    Q = SSD_CHUNK
    n_ctx_chunks = n_ctx // Q
    L = lt - lat_off
    n_lat = L // Q
    n_steps = n_ctx_chunks + n_lat
    lat0 = lat_off // Q

    def fwd_chunk(s):
        return jnp.where(s < n_ctx_chunks, s, s - n_ctx_chunks + lat0)

    def bwd_chunk(s):
        return jnp.where(s < n_ctx_chunks, n_ctx_chunks - 1 - s, n_steps - 1 - s + lat0)

    return pl.pallas_call(
        functools.partial(_ssd_kernel, n_ctx_chunks=n_ctx_chunks),
        grid=(bsz // nb, n_steps),
        in_specs=[
            pl.BlockSpec((nb, Q, width), lambda b, s: (b, fwd_chunk(s), 0)),
            pl.BlockSpec((nb, Q, LANES), lambda b, s: (b, fwd_chunk(s), 0)),
            pl.BlockSpec((nb, Q, width), lambda b, s: (b, bwd_chunk(s), 0)),
            pl.BlockSpec((nb, Q, LANES), lambda b, s: (b, bwd_chunk(s), 0)),
            pl.BlockSpec((1, D_SSD), lambda b, s: (0, 0)),
        ],
        out_specs=pl.BlockSpec((nb, L, D_SSD), lambda b, s: (b, 0, 0)),
        out_shape=jax.ShapeDtypeStruct((bsz, L, D_SSD), jnp.float32),
        scratch_shapes=[pltpu.VMEM((nb, 2, SSD_GROUPS * SSD_HPG * SSD_HEAD_DIM, SSD_STATE), jnp.float32)],
        compiler_params=pltpu.CompilerParams(
            dimension_semantics=("arbitrary", "arbitrary"), vmem_limit_bytes=VMEM_LIMIT_BYTES),
    )(xbc, dta, xbc, dta, d_skip)


def kernel(x, c, ctx, c_ctx, w_ada, b_ada, g_norm1, g_norm2, w_in, hy_conv_w, hy_conv_b, hy_f_w1, hy_f_b1, hy_f_freq, hy_f_w2, hy_f_b2, hy_f_w3, hy_bias, ssd_conv_w, ssd_conv_b, ssd_a_log, ssd_dt_bias, ssd_d, ssd_norm_g, w_out, w_group, b_group, w_expert, b_expert, w1, w3, w2, g_final):
    bsz, seq_len, _ = x.shape
    assert w_in.shape[0] == 1, "single-layer block: the context stream only supplies SSD states"
    l = 0
    rows_pad = -(bsz + 1) % SUBLANES
    c_rows = jnp.concatenate([c, c_ctx[None, :], jnp.zeros((rows_pad, D_MODEL), jnp.float32)], axis=0)
    mod_all = ada_modulation(c_rows, w_ada[l], b_ada[l], ADA_COLS)
    sh1, sc1, ga1, sh2, sc2, ga2 = jnp.split(mod_all[:bsz, None, :], 6, axis=-1)
    csh1, csc1 = mod_all[bsz, :D_MODEL], mod_all[bsz, D_MODEL:2 * D_MODEL]

    w_out_bf = w_out[l].astype(jnp.bfloat16)
    w_in_bf, w_dt_bf, dt_bias2, dt_mult = in_proj_params(w_in[l], ssd_a_log[l], ssd_dt_bias[l],
                                                         HY_COLS, D_SSD, D_XBC)

    ctx_len = ctx.shape[1]
    lat_off = -(-ctx_len // IN_PROJ_ROWS) * IN_PROJ_ROWS
    p_hy, z, xbc, dta = in_proj_fused(ctx, x, g_norm1[l][None, :], csh1[None, :], csc1[None, :], sh1, sc1,
                                      w_in_bf, w_dt_bf, HY_COLS, ssd_conv_w[l], ssd_conv_b[l][None, :],
                                      dt_bias2, dt_mult, GRID_W, ctx_len, IN_PROJ_ROWS, IN_PROJ_COLS)
    kp = hyena_filters_polyphase(seq_len, hy_f_w1[l], hy_f_b1[l], hy_f_freq[l], hy_f_w2[l], hy_f_b2[l],
                                 hy_f_w3[l], D_HYENA, FILTER_ROWS, HYENA_PHASES)
    y_hy = hyena_long_convs(p_hy, hy_conv_w[l], hy_conv_b[l][None, :], kp, hy_bias[l], HYENA_COLS, HYENA_PHASES,
                            GRID_W)
    y_scan = ssd_scan_bidir(xbc, dta, jnp.repeat(ssd_d[l], SSD_HEAD_DIM)[None, :], ctx_len, lat_off, SSD_BATCH)

    pad = LANES - MOE_GROUPS - N_EXPERTS
    w_router = jnp.concatenate([w_group[l], w_expert[l], jnp.zeros((D_MODEL, pad), jnp.float32)], axis=1)
    w_router_hi = w_router.astype(jnp.bfloat16)
    w_router_lo = (w_router - w_router_hi.astype(jnp.float32)).astype(jnp.bfloat16)
    w_router = jnp.stack([w_router_hi, w_router_lo])
    b_router = jnp.concatenate([b_group[l], b_expert[l], jnp.zeros((pad,), jnp.float32)])[None, :]
    x1, hn, route_i, route_f, counts = out_proj_router(
        y_hy, y_scan, z, 0, x, ga1, sc2, sh2, ssd_norm_g[l][None, :], g_norm2[l][None, :],
        w_out_bf, w_router, b_router, TOKEN_ROWS)
    n_tok = bsz * seq_len
    n_blocks = -(-n_tok * TOP_K // MOE_BLOCK) + N_EXPERTS
    dest, block_wsel, block_first, block_valid, block_src = moe_plan(route_i, counts, MOE_BLOCK, n_blocks)
    buf = moe_dispatch(hn.reshape(n_tok, D_MODEL), dest, block_valid, MOE_BLOCK, TOKEN_ROWS)
    yb = expert_blocks(buf, block_wsel, block_first, block_valid, block_src, w1[l], w3[l], w2[l], MOE_BLOCK)
    return moe_combine(x1, route_f, ga2, g_final[None, :], yb, dest, TOKEN_ROWS)
```

```python
import functools
import math

import jax
import jax.numpy as jnp
from jax import lax
from jax.experimental import pallas as pl
from jax.experimental.pallas import tpu as pltpu

D_MODEL = 1024
GRID_W = 64
EPS = 1e-6

D_HYENA = D_MODEL // 2
HYENA_ORDER = 2
HYENA_BANDS = 8
HYENA_FAST_DECAY = 0.3
HYENA_SLOW_DECAY = 1.5
HYENA_TARGET = 1e-2
HYENA_PHASES = 4

D_SSD = D_MODEL // 2
SSD_HEAD_DIM = 64
SSD_HEADS = D_SSD // SSD_HEAD_DIM
SSD_GROUPS = 2
SSD_HPG = SSD_HEADS // SSD_GROUPS
SSD_STATE = 128
SSD_CHUNK = 128

D_XBC = D_SSD + 2 * SSD_GROUPS * SSD_STATE
HY_COLS = (HYENA_ORDER + 1) * D_HYENA
LANES = 128
SUBLANES = 8

MOE_GROUPS = 8
EXPERTS_PER_GROUP = 8
N_EXPERTS = MOE_GROUPS * EXPERTS_PER_GROUP
TOP_K = 2
MOE_BLOCK = 256
ROUTE_COLS = 8

IN_PROJ_ROWS = 512
IN_PROJ_COLS = 512
TOKEN_ROWS = 512
DISPATCH_ROWS = 1024
HYENA_COLS = 256
FILTER_ROWS = 256
ADA_COLS = 512
SSD_BATCH = 4

VMEM_LIMIT_BYTES = 56 * 1024 * 1024
NEG_BIG = -1e30


def _conv3_rows(p, w_ref, b_ref, cols, has_prev, has_next):
    n = p.shape[0]
    prev = jnp.where(has_prev, pltpu.roll(p, 1, 0), 0.0)
    nxt = jnp.where(has_next, pltpu.roll(p, n - 1, 0), 0.0)
    return b_ref[:, cols] + w_ref[0:1, cols] * prev + w_ref[1:2, cols] * p + w_ref[2:3, cols] * nxt


def _in_proj_kernel(ctx_ref, x_ref, g_ref, csh_ref, csc_ref, sh_ref, sc_ref, w_ref, wdt_ref,
                    sw_ref, sb_ref, dtb_ref, dtm_ref, u_ref, z_ref, xbc_ref, dta_ref, h_ref,
                    *, n_ctx_steps, row_len, ctx_row_len, hy_cols, d_ssd, d_xbc, tn):
    i = pl.program_id(1)
    is_ctx = i < n_ctx_steps
    tm = x_ref.shape[1]
    ctx_t = ctx_ref[0]
    if ctx_t.shape[0] < tm:
        ctx_t = jnp.concatenate([ctx_t, jnp.zeros((tm - ctx_t.shape[0], ctx_t.shape[1]), ctx_t.dtype)], axis=0)
    xin = jnp.where(is_ctx, ctx_t, x_ref[0])
    shift = jnp.where(is_ctx, csh_ref[...], sh_ref[0])
    scale = jnp.where(is_ctx, csc_ref[...], sc_ref[0])
    y = xin * lax.rsqrt(jnp.mean(xin * xin, axis=-1, keepdims=True) + EPS) * g_ref[...]
    h_ref[...] = (y * (1.0 + scale) + shift).astype(jnp.bfloat16)

    pos = lax.broadcasted_iota(jnp.int32, (tm, 1), 0) + jnp.where(is_ctx, i, i - n_ctx_steps) * tm
    in_row = jnp.where(is_ctx, pos % ctx_row_len, pos % row_len)
    has_prev = in_row != 0
    has_next = in_row != jnp.where(is_ctx, ctx_row_len - 1, row_len - 1)

    for c0 in range(0, hy_cols, tn):
        cols = slice(c0, c0 + tn)
        u_ref[0, :, cols] = jnp.dot(h_ref[...], w_ref[:, cols], preferred_element_type=jnp.float32)
    z_ref[0] = jnp.dot(h_ref[...], w_ref[:, hy_cols:hy_cols + d_ssd], preferred_element_type=jnp.float32)

    for c0 in range(0, d_xbc, tn):
        cols = slice(c0, c0 + tn)
        wc = slice(hy_cols + d_ssd + c0, hy_cols + d_ssd + c0 + tn)
        p = jnp.dot(h_ref[...], w_ref[:, wc], preferred_element_type=jnp.float32)
        v = _conv3_rows(p, sw_ref, sb_ref, cols, has_prev, has_next)
        xbc_ref[0, :, cols] = v * jax.nn.sigmoid(v)
    pd = jnp.dot(h_ref[...], wdt_ref[...], preferred_element_type=jnp.float32) + dtb_ref[...]
    sp = jnp.maximum(pd, 0.0) + jnp.log(1.0 + jnp.exp(-jnp.abs(pd)))
    dta_ref[0] = sp * dtm_ref[...]


def in_proj_fused(ctx, x, g1, csh, csc, sh, sc, w_bf, wdt_bf, hy_cols, ssd_w, ssd_b, dt_bias2, dt_mult,
                  row_len, ctx_row_len, tm, tn):
    bsz, L, D = x.shape
    lc = ctx.shape[1]
    d_xbc = ssd_w.shape[1]
    d_ssd = w_bf.shape[1] - hy_cols - d_xbc
    n_ctx_steps = -(-lc // tm)
    ctx_rows = min(lc, tm)
    lc = n_ctx_steps * tm
    n_steps = n_ctx_steps + L // tm
    lat = lambda b, i: (b, jnp.maximum(i - n_ctx_steps, 0), 0)
    allt = lambda b, i: (b, i, 0)
    const2 = lambda b, i: (0, 0)
    per_b = pl.BlockSpec((1, 1, D), lambda b, i: (b, 0, 0))
    kern = functools.partial(_in_proj_kernel, n_ctx_steps=n_ctx_steps, row_len=row_len, ctx_row_len=ctx_row_len,
                             hy_cols=hy_cols, d_ssd=d_ssd, d_xbc=d_xbc, tn=tn)
    return pl.pallas_call(
        kern,
        grid=(bsz, n_steps),
        in_specs=[
            pl.BlockSpec((1, ctx_rows, D), lambda b, i: (b, jnp.minimum(i, n_ctx_steps - 1), 0)),
            pl.BlockSpec((1, tm, D), lat),
            pl.BlockSpec((1, D), const2),
            pl.BlockSpec((1, D), const2),
            pl.BlockSpec((1, D), const2),
            per_b, per_b,
            pl.BlockSpec(w_bf.shape, const2),
            pl.BlockSpec(wdt_bf.shape, const2),
            pl.BlockSpec(ssd_w.shape, const2),
            pl.BlockSpec(ssd_b.shape, const2),
            pl.BlockSpec((1, LANES), const2),
            pl.BlockSpec((1, LANES), const2),
        ],
        out_specs=[
            pl.BlockSpec((1, tm, hy_cols), lat),
            pl.BlockSpec((1, tm, d_ssd), lat),
            pl.BlockSpec((1, tm, d_xbc), allt),
            pl.BlockSpec((1, tm, LANES), allt),
        ],
        out_shape=[
            jax.ShapeDtypeStruct((bsz, L, hy_cols), jnp.float32),
            jax.ShapeDtypeStruct((bsz, L, d_ssd), jnp.float32),
            jax.ShapeDtypeStruct((bsz, lc + L, d_xbc), jnp.float32),
            jax.ShapeDtypeStruct((bsz, lc + L, LANES), jnp.float32),
        ],
        scratch_shapes=[pltpu.VMEM((tm, D), jnp.bfloat16)],
        compiler_params=pltpu.CompilerParams(
            dimension_semantics=("arbitrary", "arbitrary"), vmem_limit_bytes=VMEM_LIMIT_BYTES),
    )(ctx, x, g1, csh, csc, sh, sc, w_bf, wdt_bf, ssd_w, ssd_b, dt_bias2, dt_mult)


def in_proj_params(w_in, a_log, dt_bias, hy_cols, d_ssd, d_xbc):
    n_h = 2 * SSD_HEADS
    main = hy_cols + d_ssd + d_xbc
    w_dt = w_in[:, main:main + n_h]
    pad = jnp.zeros((w_in.shape[0], LANES - 2 * n_h), w_in.dtype)
    wdt = jnp.concatenate([w_dt, w_dt, pad], axis=1).astype(jnp.bfloat16)
    zpad = jnp.zeros((LANES - 2 * n_h,), jnp.float32)
    bias2 = jnp.concatenate([dt_bias.reshape(n_h), dt_bias.reshape(n_h), zpad])[None, :]
    mult = jnp.concatenate([jnp.ones((n_h,), jnp.float32), -jnp.exp(a_log).reshape(n_h), zpad])[None, :]
    return w_in[:, :main].astype(jnp.bfloat16), wdt, bias2, mult


def _out_router_kernel(yh_ref, ys_ref, z_ref, x_ref, ga_ref, sc_ref, sh_ref, ng_ref, g2_ref, wo_ref, wr_ref, br_ref,
                       x1_ref, hn_ref, ri_ref, rf_ref, cnt_ref, carry_ref):
    first = jnp.logical_and(pl.program_id(0) == 0, pl.program_id(1) == 0)

    @pl.when(first)
    def _():
        carry_ref[...] = jnp.zeros_like(carry_ref)

    bf = jnp.bfloat16
    tm = x_ref.shape[1]
    dh = yh_ref.shape[2]
    z = z_ref[0]
    ys = ys_ref[0] * (z * jax.nn.sigmoid(z))
    gw = ys.shape[1] // SSD_GROUPS
    acc = jnp.dot(yh_ref[0].astype(bf), wo_ref[0:dh, :], preferred_element_type=jnp.float32)
    for g in range(SSD_GROUPS):
        yg = ys[:, g * gw:(g + 1) * gw]
        yg = yg * lax.rsqrt(jnp.mean(yg * yg, axis=-1, keepdims=True) + EPS) * ng_ref[:, g * gw:(g + 1) * gw]
        acc += jnp.dot(yg.astype(bf), wo_ref[dh + g * gw:dh + (g + 1) * gw, :], preferred_element_type=jnp.float32)
    x1 = x_ref[0] + ga_ref[0] * acc
    x1_ref[0] = x1
    hn = x1 * lax.rsqrt(jnp.mean(x1 * x1, axis=-1, keepdims=True) + EPS) * g2_ref[...]
    hn = hn * (1.0 + sc_ref[0]) + sh_ref[0]
    hn_ref[0] = hn

    hn_hi = hn.astype(bf)
    hn_lo = (hn - hn_hi.astype(jnp.float32)).astype(bf)
    logits = (jnp.dot(hn_hi, wr_ref[0], preferred_element_type=jnp.float32)
              + jnp.dot(hn_lo, wr_ref[0], preferred_element_type=jnp.float32)
              + jnp.dot(hn_hi, wr_ref[1], preferred_element_type=jnp.float32)) + br_ref[...]
    lane = lax.broadcasted_iota(jnp.int32, (tm, LANES), 1)
    lane_f = lane.astype(jnp.float32)
    ninf = jnp.float32(-jnp.inf)
    big = jnp.float32(1e9)
    gl = jnp.where(lane < MOE_GROUPS, logits, ninf)
    gmax = jnp.max(gl, axis=-1, keepdims=True)
    p_group = 1.0 / jnp.sum(jnp.exp(gl - gmax), axis=-1, keepdims=True)
    g_sel = jnp.min(jnp.where(gl == gmax, lane_f, big), axis=-1, keepdims=True)
    e_lane = lane - MOE_GROUPS
    in_grp = jnp.logical_and(e_lane >= 0, (e_lane // EXPERTS_PER_GROUP).astype(jnp.float32) == g_sel)
    el = jnp.where(in_grp, logits, ninf)
    m1 = jnp.max(el, axis=-1, keepdims=True)
    i1 = jnp.min(jnp.where(el == m1, lane_f, big), axis=-1, keepdims=True)
    el2 = jnp.where(lane_f == i1, ninf, el)
    m2 = jnp.max(el2, axis=-1, keepdims=True)
    i2 = jnp.min(jnp.where(el2 == m2, lane_f, big), axis=-1, keepdims=True)
    t = jnp.exp(m2 - m1)
    w1 = 1.0 / (1.0 + t)
    gate1 = w1 * p_group
    gate2 = (t * w1) * p_group
    e1 = i1 - MOE_GROUPS
    e2 = i2 - MOE_GROUPS
    el_f = e_lane.astype(jnp.float32)
    oh1 = el_f == e1
    oh2 = el_f == e2
    oh = jnp.logical_or(oh1, oh2).astype(bf)
    r_i = lax.broadcasted_iota(jnp.int32, (tm, tm), 0)
    c_i = lax.broadcasted_iota(jnp.int32, (tm, tm), 1)
    before = jnp.dot((c_i < r_i).astype(bf), oh, preferred_element_type=jnp.float32) + carry_ref[...]
    rank1 = jnp.sum(jnp.where(oh1, before, 0.0), axis=-1, keepdims=True)
    rank2 = jnp.sum(jnp.where(oh2, before, 0.0), axis=-1, keepdims=True)
    carry_ref[...] += jnp.sum(oh.astype(jnp.float32), axis=0, keepdims=True)
    cnt_ref[...] = carry_ref[...]

    rec = jnp.where(lane == 0, e1, jnp.where(lane == 1, e2, jnp.where(lane == 2, rank1,
                                                                      jnp.where(lane == 3, rank2, 0.0))))
    ri_ref[0] = rec.T[0:ROUTE_COLS, :].astype(jnp.int32)
    col = lax.broadcasted_iota(jnp.int32, (tm, ROUTE_COLS), 1)
    rf_ref[0] = jnp.where(col == 0, gate1, gate2)


def out_proj_router(y_hy, y_scan, px, z_col, x, ga1, sc2, sh2, norm_g, g2, w_out_bf, w_router, b_router, tm):
    bsz, L, D = x.shape
    dh = y_hy.shape[-1]
    ds = y_scan.shape[-1]
    tok = lambda b, i: (b, i, 0)
    per_b = pl.BlockSpec((1, 1, D), lambda b, i: (b, 0, 0))
    const2 = lambda b, i: (0, 0)
    return pl.pallas_call(
        _out_router_kernel,
        grid=(bsz, L // tm),
        in_specs=[
            pl.BlockSpec((1, tm, dh), tok),
            pl.BlockSpec((1, tm, ds), tok),
            pl.BlockSpec((1, tm, ds), lambda b, i: (b, i, z_col)),
            pl.BlockSpec((1, tm, D), tok),
            per_b, per_b, per_b,
            pl.BlockSpec((1, ds), const2),
            pl.BlockSpec((1, D), const2),
            pl.BlockSpec((dh + ds, D), const2),
            pl.BlockSpec((2, D, LANES), lambda b, i: (0, 0, 0)),
            pl.BlockSpec((1, LANES), const2),
        ],
        out_specs=[
            pl.BlockSpec((1, tm, D), tok),
            pl.BlockSpec((1, tm, D), tok),
            pl.BlockSpec((1, ROUTE_COLS, tm), lambda b, i: (b, 0, i)),
            pl.BlockSpec((1, tm, ROUTE_COLS), tok),
            pl.BlockSpec((1, LANES), const2),
        ],
        out_shape=[
            jax.ShapeDtypeStruct((bsz, L, D), jnp.float32),
            jax.ShapeDtypeStruct((bsz, L, D), jnp.float32),
            jax.ShapeDtypeStruct((bsz, ROUTE_COLS, L), jnp.int32),
            jax.ShapeDtypeStruct((bsz, L, ROUTE_COLS), jnp.float32),
            jax.ShapeDtypeStruct((1, LANES), jnp.float32),
        ],
        scratch_shapes=[pltpu.VMEM((1, LANES), jnp.float32)],
        compiler_params=pltpu.CompilerParams(
            dimension_semantics=("arbitrary", "arbitrary"), vmem_limit_bytes=VMEM_LIMIT_BYTES),
    )(y_hy, y_scan, px, x, ga1, sc2, sh2, norm_g, g2, w_out_bf, w_router, b_router)


def _row_copy(src_hbm, src_row, dst_ref, dst_row, sem):
    return pltpu.make_async_copy(src_hbm.at[pl.ds(src_row, 1), :], dst_ref.at[pl.ds(dst_row, 1), :], sem)


def _dispatch_kernel(dest_ref, valid_ref, hn_ref, buf_hbm, zeros, sem, zsem):
    step = pl.program_id(0)
    tm = hn_ref.shape[0]
    n_tok = pl.num_programs(0) * tm
    blk = zeros.shape[0]
    n_blocks = buf_hbm.shape[0] // blk

    def zero_copy(i):
        return pltpu.make_async_copy(zeros, buf_hbm.at[pl.ds(pl.multiple_of(i * blk, blk), blk), :], zsem)

    def zfill(i, carry):
        @pl.when(valid_ref[i] < blk)
        def _():
            zero_copy(i).start()
        return carry

    def zwait(i, carry):
        @pl.when(valid_ref[i] < blk)
        def _():
            zero_copy(i).wait()
        return carry

    @pl.when(step == 0)
    def _():
        zeros[...] = jnp.zeros_like(zeros)
        lax.fori_loop(0, n_blocks, zfill, 0)
        lax.fori_loop(0, n_blocks, zwait, 0)

    def body(j, carry):
        t = step * tm + j
        _row_copy(hn_ref, j, buf_hbm, dest_ref[t], sem).start()
        _row_copy(hn_ref, j, buf_hbm, dest_ref[n_tok + t], sem).start()
        return carry

    lax.fori_loop(0, tm, body, 0, unroll=8)
    for _ in range(2):
        pltpu.make_async_copy(hn_ref, buf_hbm.at[pl.ds(0, tm), :], sem).wait()


def moe_dispatch(hn, dest, block_valid, blk, tm):
    T, D = hn.shape
    n_rows = block_valid.shape[0] * blk
    grid_spec = pltpu.PrefetchScalarGridSpec(
        num_scalar_prefetch=2,
        grid=(T // tm,),
        in_specs=[pl.BlockSpec((tm, D), lambda i, d, v: (i, 0))],
        out_specs=pl.BlockSpec(memory_space=pl.ANY),
        scratch_shapes=[pltpu.VMEM((blk, D), hn.dtype), pltpu.SemaphoreType.DMA(()),
                        pltpu.SemaphoreType.DMA(())],
    )
    return pl.pallas_call(
        _dispatch_kernel,
        grid_spec=grid_spec,
        out_shape=jax.ShapeDtypeStruct((n_rows, D), hn.dtype),
        compiler_params=pltpu.CompilerParams(dimension_semantics=("arbitrary",), has_side_effects=True),
    )(dest, block_valid, hn)


def _expert_kernel(wsel_ref, first_ref, valid_ref, src_ref, x_ref, w1_ref, w3_ref, w2_ref, o_ref, w1b, w3b, w2b):
    i = pl.program_id(0)
    del wsel_ref, src_ref
    bf = jnp.bfloat16

    @pl.when(first_ref[i] == 1)
    def _():
        w1b[...] = w1_ref[0].astype(bf)
        w3b[...] = w3_ref[0].astype(bf)
        w2b[...] = w2_ref[0].astype(bf)

    valid = valid_ref[i]

    @pl.when(valid > 0)
    def _():
        xb = x_ref[...].astype(bf)
        a = jnp.dot(xb, w1b[...], preferred_element_type=jnp.float32)
        b = jnp.dot(xb, w3b[...], preferred_element_type=jnp.float32)
        h = (a * jax.nn.sigmoid(a)) * b
        o_ref[...] = jnp.dot(h.astype(bf), w2b[...], preferred_element_type=jnp.float32)

    @pl.when(valid <= 0)
    def _():
        o_ref[...] = jnp.zeros_like(o_ref)


def expert_blocks(buf, block_wsel, block_first, block_valid, block_src, w1, w3, w2, blk):
    rows, D = buf.shape
    n_blocks = rows // blk
    E, _, F = w1.shape
    grid_spec = pltpu.PrefetchScalarGridSpec(
        num_scalar_prefetch=4,
        grid=(n_blocks,),
        in_specs=[
            pl.BlockSpec((blk, D), lambda i, ws, fi, va, src: (src[i], 0)),
            pl.BlockSpec((1, D, F), lambda i, ws, fi, va, src: (ws[i], 0, 0)),
            pl.BlockSpec((1, D, F), lambda i, ws, fi, va, src: (ws[i], 0, 0)),
            pl.BlockSpec((1, F, D), lambda i, ws, fi, va, src: (ws[i], 0, 0)),
        ],
        out_specs=pl.BlockSpec((blk, D), lambda i, ws, fi, va, src: (i, 0)),
        scratch_shapes=[pltpu.VMEM((D, F), jnp.bfloat16), pltpu.VMEM((D, F), jnp.bfloat16),
                        pltpu.VMEM((F, D), jnp.bfloat16)],
    )
    return pl.pallas_call(
        _expert_kernel,
        grid_spec=grid_spec,
        out_shape=jax.ShapeDtypeStruct((rows, D), jnp.float32),
        compiler_params=pltpu.CompilerParams(
            dimension_semantics=("arbitrary",), vmem_limit_bytes=VMEM_LIMIT_BYTES),
    )(block_wsel, block_first, block_valid, block_src, buf, w1, w3, w2)


def _combine_kernel(dest_ref, x1_ref, rf_ref, ga_ref, gf_ref, yb_hbm, o_ref, ybuf, sem):
    b = pl.program_id(0)
    i = pl.program_id(1)
    n_i = pl.num_programs(1)
    tm = x1_ref.shape[1]
    step = b * n_i + i
    n_steps = pl.num_programs(0) * n_i
    slot = step % 2

    def issue(step_, slot_):
        def body(j, carry):
            t = step_ * tm + j
            _row_copy(yb_hbm, dest_ref[t], ybuf.at[slot_, 0], j, sem.at[slot_]).start()
            _row_copy(yb_hbm, dest_ref[n_steps * tm + t], ybuf.at[slot_, 1], j, sem.at[slot_]).start()
            return carry
        lax.fori_loop(0, tm, body, 0, unroll=8)

    @pl.when(step == 0)
    def _():
        issue(0, 0)

    @pl.when(step + 1 < n_steps)
    def _():
        issue(step + 1, 1 - slot)

    pltpu.make_async_copy(yb_hbm.at[pl.ds(0, tm), :], ybuf.at[slot, 0], sem.at[slot]).wait()
    pltpu.make_async_copy(yb_hbm.at[pl.ds(0, tm), :], ybuf.at[slot, 1], sem.at[slot]).wait()
    rf = rf_ref[0]
    y = rf[:, 0:1] * ybuf[slot, 0] + rf[:, 1:2] * ybuf[slot, 1]
    x2 = x1_ref[0] + ga_ref[0] * y
    o_ref[0] = x2 * lax.rsqrt(jnp.mean(x2 * x2, axis=-1, keepdims=True) + EPS) * gf_ref[...]


def moe_combine(x1, route_f, ga2, g_final, yb, dest, tm):
    bsz, L, D = x1.shape
    grid_spec = pltpu.PrefetchScalarGridSpec(
        num_scalar_prefetch=1,
        grid=(bsz, L // tm),
        in_specs=[
            pl.BlockSpec((1, tm, D), lambda b, i, d: (b, i, 0)),
            pl.BlockSpec((1, tm, ROUTE_COLS), lambda b, i, d: (b, i, 0)),
            pl.BlockSpec((1, 1, D), lambda b, i, d: (b, 0, 0)),
            pl.BlockSpec((1, D), lambda b, i, d: (0, 0)),
            pl.BlockSpec(memory_space=pl.ANY),
        ],
        out_specs=pl.BlockSpec((1, tm, D), lambda b, i, d: (b, i, 0)),
        scratch_shapes=[pltpu.VMEM((2, 2, tm, D), jnp.float32), pltpu.SemaphoreType.DMA((2,))],
    )
    return pl.pallas_call(
        _combine_kernel,
        grid_spec=grid_spec,
        out_shape=jax.ShapeDtypeStruct((bsz, L, D), jnp.float32),
        compiler_params=pltpu.CompilerParams(
            dimension_semantics=("arbitrary", "arbitrary"), vmem_limit_bytes=VMEM_LIMIT_BYTES),
    )(dest, x1, route_f, ga2, g_final, yb)


def moe_plan(route_i, counts, blk, n_blocks):
    cnt = counts[0, MOE_GROUPS:MOE_GROUPS + N_EXPERTS].astype(jnp.int32)
    padded = (cnt + blk - 1) // blk * blk
    ends = jnp.cumsum(padded)
    starts = ends - padded
    experts = jnp.arange(N_EXPERTS, dtype=jnp.int32)
    dest = jnp.concatenate([
        (jnp.sum(jnp.where(route_i[:, k, :, None] == experts, starts, 0), axis=-1) + route_i[:, 2 + k]).reshape(-1)
        for k in range(TOP_K)])
    first_row = jnp.arange(n_blocks, dtype=jnp.int32) * blk
    block_eid = jnp.minimum(jnp.sum((ends[None, :] <= first_row[:, None]).astype(jnp.int32), axis=1), N_EXPERTS - 1)
    block_valid = jnp.clip(cnt[block_eid] - (first_row - starts[block_eid]), 0, blk).astype(jnp.int32)
    block_first = jnp.concatenate([jnp.ones((1,), jnp.int32),
                                   (block_eid[1:] != block_eid[:-1]).astype(jnp.int32)])
    idx = jnp.arange(n_blocks, dtype=jnp.int32)
    next_first = lax.cummin(jnp.where(block_first == 1, idx, n_blocks), axis=0, reverse=True)
    block_wsel = jnp.where(next_first < n_blocks, block_eid[jnp.minimum(next_first, n_blocks - 1)], block_eid)
    n_used = jnp.sum((block_valid > 0).astype(jnp.int32))
    block_src = jnp.minimum(idx, jnp.maximum(n_used - 1, 0))
    return dest, block_wsel, block_first, block_valid, block_src


def dft_tables(L):
    n = 2 * L
    f = lax.broadcasted_iota(jnp.int32, (L, L), 0)
    t = lax.broadcasted_iota(jnp.int32, (L, L), 1)
    ang = ((f * t) % n).astype(jnp.float32) * (2.0 * math.pi / n)
    return jnp.cos(ang).astype(jnp.bfloat16), jnp.sin(ang).astype(jnp.bfloat16)


def _alt_sign(L):
    t = lax.broadcasted_iota(jnp.int32, (L, 1), 0)
    return (1 - 2 * (t & 1)).astype(jnp.float32)


def _spectrum_kernel(a_ref, b_ref, c_ref, s_ref, kr_ref, ks_ref, kn_ref):
    L = a_ref.shape[1]
    a = a_ref[0]
    row = lax.broadcasted_iota(jnp.int32, (L, 1), 0)
    scale = jnp.where(row == 0, 0.5 / L, 1.0 / L)
    kr_ref[0] = scale * jnp.dot(c_ref[...], a.astype(jnp.bfloat16), preferred_element_type=jnp.float32)
    ks_ref[0] = scale * jnp.dot(s_ref[...], b_ref[0].astype(jnp.bfloat16), preferred_element_type=jnp.float32)
    kn_ref[0] = jnp.sum(a * _alt_sign(L), axis=0, keepdims=True) * (0.5 / L)


def filter_spectrum(a, b, cos_t, sin_t, tc):
    n, L, C = a.shape
    blk = pl.BlockSpec((1, L, tc), lambda o, j: (o, 0, j))
    tab = pl.BlockSpec((L, L), lambda o, j: (0, 0))
    return pl.pallas_call(
        _spectrum_kernel,
        grid=(n, C // tc),
        in_specs=[blk, blk, tab, tab],
        out_specs=[blk, blk, pl.BlockSpec((1, 1, tc), lambda o, j: (o, 0, j))],
        out_shape=[jax.ShapeDtypeStruct((n, L, C), jnp.float32)] * 2 + [jax.ShapeDtypeStruct((n, 1, C), jnp.float32)],
        compiler_params=pltpu.CompilerParams(
            dimension_semantics=("arbitrary", "arbitrary"), vmem_limit_bytes=VMEM_LIMIT_BYTES),
    )(a, b, cos_t, sin_t)


def _phase_conv3(raw, w_ref, b_ref, rows_per_phase):
    n_ph = len(raw)
    h = raw[0].shape[0]
    j = lax.broadcasted_iota(jnp.int32, (h, 1), 0) % rows_per_phase
    prev0 = jnp.where(j != 0, pltpu.roll(raw[n_ph - 1], 1, 0), 0.0)
    next_last = jnp.where(j != rows_per_phase - 1, pltpu.roll(raw[0], h - 1, 0), 0.0)
    out = []
    for p in range(n_ph):
        prev = raw[p - 1] if p > 0 else prev0
        nxt = raw[p + 1] if p < n_ph - 1 else next_last
        out.append(b_ref[...] + w_ref[0:1, :] * prev + w_ref[1:2, :] * raw[p] + w_ref[2:3, :] * nxt)
    return out


def _long_conv_kernel(*refs, n_ph, n_slab, conv_z, rows_per_phase):
    z_refs = refs[:n_slab]
    xn_refs = refs[n_slab:2 * n_slab]
    (kr_ref, ks_ref, kn_ref, bias_ref, cwz_ref, cbz_ref, cwx_ref, cbx_ref, c_ref, s_ref,
     o_ref, acc_ref, zr_ref, zs_ref, yr_ref, ys_ref, stage_ref) = refs[2 * n_slab:]
    H = z_refs[0].shape[1] // n_ph
    f32 = jnp.float32
    bf = jnp.bfloat16
    sign = _alt_sign(H)

    def phases(slab_refs):
        return [jnp.concatenate([r[0, pl.ds(p, H, stride=n_ph), :] for r in slab_refs], axis=1)
                for p in range(n_ph)]

    z_ph = phases(z_refs)
    if conv_z:
        z_ph = _phase_conv3(z_ph, cwz_ref, cbz_ref, rows_per_phase)
    for q in range(n_ph):
        zb = z_ph[q].astype(bf)
        zr_ref[q] = jnp.dot(c_ref[...], zb, preferred_element_type=f32)
        zs_ref[q] = jnp.dot(s_ref[...], zb, preferred_element_type=f32)
    z_nyq = [jnp.sum(z * sign, axis=0, keepdims=True) for z in z_ph]
    for p in range(n_ph):
        nyq = sum(z_nyq[q] * kn_ref[p - q + n_ph - 1] for q in range(n_ph))
        acc_ref[p] = z_ph[p] * bias_ref[0] + sign * nyq
        yr = 0.0
        ys = 0.0
        for q in range(n_ph):
            slot = p - q + n_ph - 1
            yr = yr + zr_ref[q] * kr_ref[slot] - zs_ref[q] * ks_ref[slot]
            ys = ys + zr_ref[q] * ks_ref[slot] + zs_ref[q] * kr_ref[slot]
        yr_ref[p] = yr.astype(bf)
        ys_ref[p] = ys.astype(bf)
    for p in range(n_ph):
        acc_ref[p] += (jnp.dot(c_ref[...], yr_ref[p], preferred_element_type=f32)
                       + jnp.dot(s_ref[...], ys_ref[p], preferred_element_type=f32))
    x_ph = _phase_conv3(phases(xn_refs), cwx_ref, cbx_ref, rows_per_phase)
    for p in range(n_ph):
        out_p = x_ph[p] * acc_ref[p]
        for sl in range(n_slab):
            stage_ref[sl, pl.ds(p, H, stride=n_ph), :] = out_p[:, sl * LANES:(sl + 1) * LANES]
    for sl in range(n_slab):
        o_ref[0, :, sl * LANES:(sl + 1) * LANES] = stage_ref[sl].astype(o_ref.dtype)


def long_conv_gate(z_arr, z_col, conv_z, xn_arr, xn_col, conv_w, conv_b, kr, ks, kn, bias, cos_t, sin_t,
                   tc, n_ph, row_len, out_dtype):
    bsz, L, _ = z_arr.shape
    H = L // n_ph
    C = kr.shape[-1]
    nj = C // tc
    n_slab = tc // LANES
    n_f = 2 * n_ph - 1
    tab = pl.BlockSpec((H, H), lambda j, b: (0, 0), pipeline_mode=pl.Buffered(1))
    spec = pl.BlockSpec((n_f, H, tc), lambda j, b: (0, 0, j), pipeline_mode=pl.Buffered(1))
    nyq = pl.BlockSpec((n_f, 1, tc), lambda j, b: (0, 0, j))
    vec = pl.BlockSpec((1, 1, tc), lambda j, b: (0, 0, j))

    def slabs(col):
        return [pl.BlockSpec((1, L, LANES),
                             functools.partial(lambda j, b, sl: (b, 0, (col * nj + j) * n_slab + sl), sl=sl))
                for sl in range(n_slab)]

    def conv_specs(col):
        return [pl.BlockSpec((3, tc), lambda j, b: (0, col * nj + j)),
                pl.BlockSpec((1, tc), lambda j, b: (0, col * nj + j))]

    zc = z_col if conv_z else 0
    kern = functools.partial(_long_conv_kernel, n_ph=n_ph, n_slab=n_slab, conv_z=conv_z,
                             rows_per_phase=row_len // n_ph)
    return pl.pallas_call(
        kern,
        grid=(nj, bsz),
        in_specs=(slabs(z_col) + slabs(xn_col) + [spec, spec, nyq, vec] + conv_specs(zc) + conv_specs(xn_col)
                  + [tab, tab]),
        out_specs=pl.BlockSpec((1, L, tc), lambda j, b: (b, 0, j)),
        out_shape=jax.ShapeDtypeStruct((bsz, L, C), out_dtype),
        scratch_shapes=[pltpu.VMEM((n_ph, H, tc), jnp.float32), pltpu.VMEM((n_ph, H, tc), jnp.float32),
                        pltpu.VMEM((n_ph, H, tc), jnp.float32), pltpu.VMEM((n_ph, H, tc), jnp.bfloat16),
                        pltpu.VMEM((n_ph, H, tc), jnp.bfloat16), pltpu.VMEM((n_slab, L, LANES), jnp.float32)],
        compiler_params=pltpu.CompilerParams(
            dimension_semantics=("arbitrary", "arbitrary"), vmem_limit_bytes=VMEM_LIMIT_BYTES),
    )(*([z_arr] * n_slab), *([xn_arr] * n_slab), kr, ks, kn, bias, conv_w, conv_b, conv_w, conv_b, cos_t, sin_t)


def _polyphase_taps(kf, kb, n_ph):
    H = kf.shape[0] // n_ph
    ph = lambda a, p: a[p * H:(p + 1) * H]
    zero = jnp.zeros_like(kf[:1])
    plus, minus = [], []
    for r in range(-(n_ph - 1), n_ph):
        if r >= 0:
            plus.append(ph(kf, r))
        else:
            plus.append(jnp.concatenate([ph(kb, -r)[0:1], ph(kf, n_ph + r)[:-1]], axis=0))
        if r <= 0:
            minus.append(jnp.concatenate([zero, ph(kb, -r)[1:]], axis=0))
        else:
            minus.append(jnp.concatenate([zero, ph(kb, n_ph - r)[:-1]], axis=0))
    return jnp.stack(plus), jnp.stack(minus)


def hyena_long_convs(p_hy, conv_w, conv_b, kp, h_bias, tc, n_ph, row_len):
    L = p_hy.shape[1]
    C = h_bias.shape[1]
    cos_t, sin_t = dft_tables(L // n_ph)
    z = p_hy
    for o in range(h_bias.shape[0]):
        fwd = slice((2 * o) * C, (2 * o + 1) * C)
        bwd = slice((2 * o + 1) * C, (2 * o + 2) * C)
        plus, minus = _polyphase_taps(kp[:, fwd], kp[:, bwd], n_ph)
        kr, ks, kn = filter_spectrum(plus + minus, plus - minus, cos_t, sin_t, tc)
        last = o == h_bias.shape[0] - 1
        z = long_conv_gate(z, 0, o == 0, p_hy, o + 1, conv_w, conv_b, kr, ks, kn, h_bias[o][None, None, :],
                           cos_t, sin_t, tc, n_ph, row_len, jnp.bfloat16 if last else jnp.float32)
    return z


def _filter_kernel(band_ref, w1_ref, b1_ref, fr_ref, w2_ref, b2_ref, w3_ref, dl_ref, k_ref, *, seq_len, n_ph):
    hp = lax.Precision.HIGHEST
    f32 = jnp.float32
    tp = k_ref.shape[0]
    per_phase = seq_len // n_ph
    g = lax.broadcasted_iota(jnp.int32, (tp, 1), 0) + pl.program_id(0) * tp
    phase = g // per_phase
    pos = (n_ph * (g - phase * per_phase) + phase).astype(f32)
    t = pos / max(seq_len - 1, 1)
    ang = (2 * math.pi / seq_len) * pos * band_ref[...]
    lane = lax.broadcasted_iota(jnp.int32, (tp, LANES), 1)
    feats = jnp.where(lane == 0, t,
                      jnp.where(lane <= HYENA_BANDS, jnp.cos(ang),
                                jnp.where(lane <= 2 * HYENA_BANDS, -jnp.sin(ang), 0.0)))
    h = jnp.sin(fr_ref[...] * (jnp.dot(feats, w1_ref[...], precision=hp, preferred_element_type=f32) + b1_ref[...]))
    h = jnp.sin(fr_ref[...] * (jnp.dot(h, w2_ref[...], precision=hp, preferred_element_type=f32) + b2_ref[...]))
    window = jnp.exp(-t * dl_ref[...])
    c = dl_ref.shape[1]
    for j in range(w3_ref.shape[1] // c):
        cols = slice(j * c, (j + 1) * c)
        k_ref[:, cols] = jnp.dot(h, w3_ref[:, cols], precision=hp, preferred_element_type=f32) * window


def hyena_filters_polyphase(seq_len, f_w1, f_b1, f_freq, f_w2, f_b2, f_w3, d_hyena, tp, n_ph):
    f32 = jnp.float32
    fh = f_w1.shape[1]
    n_emb = 1 + 2 * HYENA_BANDS
    bands = jnp.linspace(1e-4, HYENA_BANDS - 1, HYENA_BANDS, dtype=f32)
    band_row = jnp.concatenate([jnp.zeros((1,), f32), bands, bands, jnp.zeros((LANES - n_emb,), f32)])[None, :]
    w1p = jnp.concatenate([f_w1, jnp.zeros((LANES - n_emb, fh), f32)], axis=0)
    deltas = jnp.abs(jnp.linspace(math.log(HYENA_TARGET) / HYENA_SLOW_DECAY,
                                  math.log(HYENA_TARGET) / HYENA_FAST_DECAY, d_hyena, dtype=f32))[None, :]
    n_out = f_w3.shape[1]
    full = lambda a: pl.BlockSpec(a.shape, lambda i: (0,) * a.ndim)
    args = (band_row, w1p, f_b1[None, :], f_freq[None, :], f_w2, f_b2[None, :], f_w3, deltas)
    return pl.pallas_call(
        functools.partial(_filter_kernel, seq_len=seq_len, n_ph=n_ph),
        grid=(seq_len // tp,),
        in_specs=[full(a) for a in args],
        out_specs=pl.BlockSpec((tp, n_out), lambda i: (i, 0)),
        out_shape=jax.ShapeDtypeStruct((seq_len, n_out), f32),
        compiler_params=pltpu.CompilerParams(dimension_semantics=("arbitrary",), vmem_limit_bytes=VMEM_LIMIT_BYTES),
    )(*args)


def _ada_kernel(c_ref, w_ref, b_ref, o_ref):
    cv = c_ref[...]
    s = cv * jax.nn.sigmoid(cv)
    o_ref[...] = jnp.dot(s, w_ref[...], precision=lax.Precision.HIGHEST,
                         preferred_element_type=jnp.float32) + b_ref[...]


def ada_modulation(c_rows, w_ada, b_ada, tn):
    rows, D = c_rows.shape
    N = w_ada.shape[1]
    return pl.pallas_call(
        _ada_kernel,
        grid=(N // tn,),
        in_specs=[pl.BlockSpec((rows, D), lambda j: (0, 0)),
                  pl.BlockSpec((D, tn), lambda j: (0, j)),
                  pl.BlockSpec((1, tn), lambda j: (0, j))],
        out_specs=pl.BlockSpec((rows, tn), lambda j: (0, j)),
        out_shape=jax.ShapeDtypeStruct((rows, N), jnp.float32),
        compiler_params=pltpu.CompilerParams(dimension_semantics=("arbitrary",), vmem_limit_bytes=VMEM_LIMIT_BYTES),
    )(c_rows, w_ada, b_ada[None, :])


def _ssd_kernel(xf_ref, df_ref, xb_ref, db_ref, dskip_ref, y_ref, h_ref, *, n_ctx_chunks):
    s = pl.program_id(1)
    n_steps = pl.num_programs(1)
    Q, G, R, P, N = SSD_CHUNK, SSD_GROUPS, SSD_HPG, SSD_HEAD_DIM, SSD_STATE
    GP = R * P
    bf = jnp.bfloat16

    @pl.when(s == 0)
    def _():
        h_ref[...] = jnp.zeros_like(h_ref)
        y_ref[...] = jnp.zeros_like(y_ref)

    row = lax.broadcasted_iota(jnp.int32, (Q, Q), 0)
    col = lax.broadcasted_iota(jnp.int32, (Q, Q), 1)
    lane_head = lax.broadcasted_iota(jnp.int32, (Q, GP), 1) // P
    block_head = lax.broadcasted_iota(jnp.int32, (Q, LANES), 1) // P
    is_latent = s >= n_ctx_chunks
    n_lat = n_steps - n_ctx_chunks
    out_chunk = (jnp.clip(s - n_ctx_chunks, 0, n_lat - 1), jnp.clip(n_steps - 1 - s, 0, n_lat - 1))

    for bb, d in [(bb, d) for bb in range(y_ref.shape[0]) for d in range(2)]:
        x_ref, da_ref = ((xf_ref, df_ref), (xb_ref, db_ref))[d]
        mask = (row >= col) if d == 0 else (col >= row)
        tri = mask.astype(jnp.float32)
        da = da_ref[bb]
        cum = jnp.dot(tri, da, precision=lax.Precision.HIGHEST, preferred_element_type=jnp.float32)
        cum_t = cum.T
        edge = Q - 1 if d == 0 else 0
        blk = x_ref.at[bb]
        for g in range(G):
            xg = blk[:, g * GP:(g + 1) * GP]
            bg = blk[:, D_SSD + g * N:D_SSD + (g + 1) * N].astype(bf)
            cg = blk[:, D_SSD + G * N + g * N:D_SSD + G * N + (g + 1) * N].astype(bf)
            heads = [d * SSD_HEADS + g * R + r for r in range(R)]
            dtm = jnp.zeros((Q, GP), jnp.float32)
            cumm = jnp.zeros((Q, GP), jnp.float32)
            for r, h in enumerate(heads):
                sel = lane_head == r
                dtm = jnp.where(sel, da[:, h:h + 1], dtm)
                cumm = jnp.where(sel, cum[:, SSD_HEADS * 2 + h:SSD_HEADS * 2 + h + 1], cumm)
            totm = cumm[edge:edge + 1, :]
            xdt = xg * dtm
            hg = h_ref[bb, d, g * GP:(g + 1) * GP, :]

            gmat = lax.dot_general(cg, bg, (((1,), (1,)), ((), ())), preferred_element_type=jnp.float32)
            y_off = lax.dot_general(cg, hg.astype(bf), (((1,), (1,)), ((), ())),
                                    preferred_element_type=jnp.float32) * jnp.exp(cumm)
            if d == 0:
                y_off = y_off + dskip_ref[:, g * GP:(g + 1) * GP] * xg
            per_block = LANES // P
            parts = [jnp.zeros((Q, LANES), jnp.float32) for _ in range(R // per_block)]
            for r, h in enumerate(heads):
                a_col = cum[:, SSD_HEADS * 2 + h:SSD_HEADS * 2 + h + 1]
                a_row = cum_t[SSD_HEADS * 2 + h:SSD_HEADS * 2 + h + 1, :]
                decay = jnp.exp(jnp.where(mask, a_col - a_row, NEG_BIG))
                j = r // per_block
                xblk = xdt[:, j * LANES:(j + 1) * LANES]
                own = block_head == r % per_block
                parts[j] = parts[j] + jnp.dot((gmat * decay).astype(bf), jnp.where(own, xblk, 0.0).astype(bf),
                                              preferred_element_type=jnp.float32)
            y = jnp.where(is_latent, y_off + jnp.concatenate(parts, axis=-1), 0.0)
            rows = pl.ds(pl.multiple_of(out_chunk[d] * Q, Q), Q)
            y_ref[bb, rows, g * GP:(g + 1) * GP] += y

            xw = (xdt * jnp.exp(totm - cumm)).astype(bf)
            st = lax.dot_general(xw, bg, (((0,), (0,)), ((), ())), preferred_element_type=jnp.float32)
            for r, h in enumerate(heads):
                dec = jnp.exp(cum_t[SSD_HEADS * 2 + h:SSD_HEADS * 2 + h + 1, edge:edge + 1])
                rs = slice(g * GP + r * P, g * GP + (r + 1) * P)
                h_ref[bb, d, rs, :] = h_ref[bb, d, rs, :] * dec + st[r * P:(r + 1) * P, :]


def ssd_scan_bidir(xbc, dta, d_skip, n_ctx, lat_off, nb):
    bsz, lt, width = xbc.shape
    Q = SSD_CHUNK
    n_ctx_chunks = n_ctx // Q
    L = lt - lat_off
    n_lat = L // Q
    n_steps = n_ctx_chunks + n_lat
    lat0 = lat_off // Q

    def fwd_chunk(s):
        return jnp.where(s < n_ctx_chunks, s, s - n_ctx_chunks + lat0)

    def bwd_chunk(s):
        return jnp.where(s < n_ctx_chunks, n_ctx_chunks - 1 - s, n_steps - 1 - s + lat0)

    return pl.pallas_call(
        functools.partial(_ssd_kernel, n_ctx_chunks=n_ctx_chunks),
        grid=(bsz // nb, n_steps),
        in_specs=[
            pl.BlockSpec((nb, Q, width), lambda b, s: (b, fwd_chunk(s), 0)),
            pl.BlockSpec((nb, Q, LANES), lambda b, s: (b, fwd_chunk(s), 0)),
            pl.BlockSpec((nb, Q, width), lambda b, s: (b, bwd_chunk(s), 0)),
            pl.BlockSpec((nb, Q, LANES), lambda b, s: (b, bwd_chunk(s), 0)),
            pl.BlockSpec((1, D_SSD), lambda b, s: (0, 0)),
        ],
        out_specs=pl.BlockSpec((nb, L, D_SSD), lambda b, s: (b, 0, 0)),
        out_shape=jax.ShapeDtypeStruct((bsz, L, D_SSD), jnp.float32),
        scratch_shapes=[pltpu.VMEM((nb, 2, SSD_GROUPS * SSD_HPG * SSD_HEAD_DIM, SSD_STATE), jnp.float32)],
        compiler_params=pltpu.CompilerParams(
            dimension_semantics=("arbitrary", "arbitrary"), vmem_limit_bytes=VMEM_LIMIT_BYTES),
    )(xbc, dta, xbc, dta, d_skip)


def kernel(x, c, ctx, c_ctx, w_ada, b_ada, g_norm1, g_norm2, w_in, hy_conv_w, hy_conv_b, hy_f_w1, hy_f_b1, hy_f_freq, hy_f_w2, hy_f_b2, hy_f_w3, hy_bias, ssd_conv_w, ssd_conv_b, ssd_a_log, ssd_dt_bias, ssd_d, ssd_norm_g, w_out, w_group, b_group, w_expert, b_expert, w1, w3, w2, g_final):
    bsz, seq_len, _ = x.shape
    assert w_in.shape[0] == 1, "single-layer block: the context stream only supplies SSD states"
    l = 0
    rows_pad = -(bsz + 1) % SUBLANES
    c_rows = jnp.concatenate([c, c_ctx[None, :], jnp.zeros((rows_pad, D_MODEL), jnp.float32)], axis=0)
    mod_all = ada_modulation(c_rows, w_ada[l], b_ada[l], ADA_COLS)
    sh1, sc1, ga1, sh2, sc2, ga2 = jnp.split(mod_all[:bsz, None, :], 6, axis=-1)
    csh1, csc1 = mod_all[bsz, :D_MODEL], mod_all[bsz, D_MODEL:2 * D_MODEL]

    w_out_bf = w_out[l].astype(jnp.bfloat16)
    w_in_bf, w_dt_bf, dt_bias2, dt_mult = in_proj_params(w_in[l], ssd_a_log[l], ssd_dt_bias[l],
                                                         HY_COLS, D_SSD, D_XBC)

    ctx_len = ctx.shape[1]
    lat_off = -(-ctx_len // IN_PROJ_ROWS) * IN_PROJ_ROWS
    p_hy, z, xbc, dta = in_proj_fused(ctx, x, g_norm1[l][None, :], csh1[None, :], csc1[None, :], sh1, sc1,
                                      w_in_bf, w_dt_bf, HY_COLS, ssd_conv_w[l], ssd_conv_b[l][None, :],
                                      dt_bias2, dt_mult, GRID_W, ctx_len, IN_PROJ_ROWS, IN_PROJ_COLS)
    kp = hyena_filters_polyphase(seq_len, hy_f_w1[l], hy_f_b1[l], hy_f_freq[l], hy_f_w2[l], hy_f_b2[l],
                                 hy_f_w3[l], D_HYENA, FILTER_ROWS, HYENA_PHASES)
    y_hy = hyena_long_convs(p_hy, hy_conv_w[l], hy_conv_b[l][None, :], kp, hy_bias[l], HYENA_COLS, HYENA_PHASES,
                            GRID_W)
    y_scan = ssd_scan_bidir(xbc, dta, jnp.repeat(ssd_d[l], SSD_HEAD_DIM)[None, :], ctx_len, lat_off, SSD_BATCH)

    pad = LANES - MOE_GROUPS - N_EXPERTS
    w_router = jnp.concatenate([w_group[l], w_expert[l], jnp.zeros((D_MODEL, pad), jnp.float32)], axis=1)
    w_router_hi = w_router.astype(jnp.bfloat16)
    w_router_lo = (w_router - w_router_hi.astype(jnp.float32)).astype(jnp.bfloat16)
    w_router = jnp.stack([w_router_hi, w_router_lo])
    b_router = jnp.concatenate([b_group[l], b_expert[l], jnp.zeros((pad,), jnp.float32)])[None, :]
    x1, hn, route_i, route_f, counts = out_proj_router(
        y_hy, y_scan, z, 0, x, ga1, sc2, sh2, ssd_norm_g[l][None, :], g_norm2[l][None, :],
        w_out_bf, w_router, b_router, TOKEN_ROWS)
    n_tok = bsz * seq_len
    n_blocks = -(-n_tok * TOP_K // MOE_BLOCK) + N_EXPERTS
    dest, block_wsel, block_first, block_valid, block_src = moe_plan(route_i, counts, MOE_BLOCK, n_blocks)
    buf = moe_dispatch(hn.reshape(n_tok, D_MODEL), dest, block_valid, MOE_BLOCK, DISPATCH_ROWS)
    yb = expert_blocks(buf, block_wsel, block_first, block_valid, block_src, w1[l], w3[l], w2[l], MOE_BLOCK)
    return moe_combine(x1, route_f, ga2, g_final[None, :], yb, dest, TOKEN_ROWS)
```

```python
import functools
import math

import jax
import jax.numpy as jnp
from jax import lax
from jax.experimental import pallas as pl
from jax.experimental.pallas import tpu as pltpu

D_MODEL = 1024
GRID_W = 64
EPS = 1e-6

D_HYENA = D_MODEL // 2
HYENA_ORDER = 2
HYENA_BANDS = 8
HYENA_FAST_DECAY = 0.3
HYENA_SLOW_DECAY = 1.5
HYENA_TARGET = 1e-2
HYENA_PHASES = 4

D_SSD = D_MODEL // 2
SSD_HEAD_DIM = 64
SSD_HEADS = D_SSD // SSD_HEAD_DIM
SSD_GROUPS = 2
SSD_HPG = SSD_HEADS // SSD_GROUPS
SSD_STATE = 128
SSD_CHUNK = 128

D_XBC = D_SSD + 2 * SSD_GROUPS * SSD_STATE
HY_COLS = (HYENA_ORDER + 1) * D_HYENA
LANES = 128
SUBLANES = 8

MOE_GROUPS = 8
EXPERTS_PER_GROUP = 8
N_EXPERTS = MOE_GROUPS * EXPERTS_PER_GROUP
TOP_K = 2
MOE_BLOCK = 256
ROUTE_COLS = 8

IN_PROJ_ROWS = 512
IN_PROJ_COLS = 512
TOKEN_ROWS = 512
DISPATCH_ROWS = 1024
HYENA_COLS = 256
FILTER_ROWS = 256
ADA_COLS = 512
SSD_BATCH = 4

VMEM_LIMIT_BYTES = 56 * 1024 * 1024
NEG_BIG = -1e30


def _conv3_rows(p, w_ref, b_ref, cols, has_prev, has_next):
    n = p.shape[0]
    prev = jnp.where(has_prev, pltpu.roll(p, 1, 0), 0.0)
    nxt = jnp.where(has_next, pltpu.roll(p, n - 1, 0), 0.0)
    return b_ref[:, cols] + w_ref[0:1, cols] * prev + w_ref[1:2, cols] * p + w_ref[2:3, cols] * nxt


def _in_proj_kernel(ctx_ref, x_ref, g_ref, csh_ref, csc_ref, sh_ref, sc_ref, w_ref, wdt_ref,
                    sw_ref, sb_ref, dtb_ref, dtm_ref, u_ref, z_ref, xbc_ref, dta_ref, h_ref,
                    *, n_ctx_steps, row_len, ctx_row_len, hy_cols, d_ssd, d_xbc, tn):
    i = pl.program_id(1)
    is_ctx = i < n_ctx_steps
    tm = x_ref.shape[1]
    ctx_t = ctx_ref[0]
    if ctx_t.shape[0] < tm:
        ctx_t = jnp.concatenate([ctx_t, jnp.zeros((tm - ctx_t.shape[0], ctx_t.shape[1]), ctx_t.dtype)], axis=0)
    xin = jnp.where(is_ctx, ctx_t, x_ref[0])
    shift = jnp.where(is_ctx, csh_ref[...], sh_ref[0])
    scale = jnp.where(is_ctx, csc_ref[...], sc_ref[0])
    y = xin * lax.rsqrt(jnp.mean(xin * xin, axis=-1, keepdims=True) + EPS) * g_ref[...]
    h_ref[...] = (y * (1.0 + scale) + shift).astype(jnp.bfloat16)

    pos = lax.broadcasted_iota(jnp.int32, (tm, 1), 0) + jnp.where(is_ctx, i, i - n_ctx_steps) * tm
    in_row = jnp.where(is_ctx, pos % ctx_row_len, pos % row_len)
    has_prev = in_row != 0
    has_next = in_row != jnp.where(is_ctx, ctx_row_len - 1, row_len - 1)

    for c0 in range(0, hy_cols, tn):
        cols = slice(c0, c0 + tn)
        u_ref[0, :, cols] = jnp.dot(h_ref[...], w_ref[:, cols], preferred_element_type=jnp.float32)
    z_ref[0] = jnp.dot(h_ref[...], w_ref[:, hy_cols:hy_cols + d_ssd], preferred_element_type=jnp.float32)

    for c0 in range(0, d_xbc, tn):
        cols = slice(c0, c0 + tn)
        wc = slice(hy_cols + d_ssd + c0, hy_cols + d_ssd + c0 + tn)
        p = jnp.dot(h_ref[...], w_ref[:, wc], preferred_element_type=jnp.float32)
        v = _conv3_rows(p, sw_ref, sb_ref, cols, has_prev, has_next)
        xbc_ref[0, :, cols] = v * jax.nn.sigmoid(v)
    pd = jnp.dot(h_ref[...], wdt_ref[...], preferred_element_type=jnp.float32) + dtb_ref[...]
    sp = jnp.maximum(pd, 0.0) + jnp.log(1.0 + jnp.exp(-jnp.abs(pd)))
    dta_ref[0] = sp * dtm_ref[...]


def in_proj_fused(ctx, x, g1, csh, csc, sh, sc, w_bf, wdt_bf, hy_cols, ssd_w, ssd_b, dt_bias2, dt_mult,
                  row_len, ctx_row_len, tm, tn):
    bsz, L, D = x.shape
    lc = ctx.shape[1]
    d_xbc = ssd_w.shape[1]
    d_ssd = w_bf.shape[1] - hy_cols - d_xbc
    n_ctx_steps = -(-lc // tm)
    ctx_rows = min(lc, tm)
    lc = n_ctx_steps * tm
    n_steps = n_ctx_steps + L // tm
    lat = lambda b, i: (b, jnp.maximum(i - n_ctx_steps, 0), 0)
    allt = lambda b, i: (b, i, 0)
    const2 = lambda b, i: (0, 0)
    per_b = pl.BlockSpec((1, 1, D), lambda b, i: (b, 0, 0))
    kern = functools.partial(_in_proj_kernel, n_ctx_steps=n_ctx_steps, row_len=row_len, ctx_row_len=ctx_row_len,
                             hy_cols=hy_cols, d_ssd=d_ssd, d_xbc=d_xbc, tn=tn)
    return pl.pallas_call(
        kern,
        grid=(bsz, n_steps),
        in_specs=[
            pl.BlockSpec((1, ctx_rows, D), lambda b, i: (b, jnp.minimum(i, n_ctx_steps - 1), 0)),
            pl.BlockSpec((1, tm, D), lat),
            pl.BlockSpec((1, D), const2),
            pl.BlockSpec((1, D), const2),
            pl.BlockSpec((1, D), const2),
            per_b, per_b,
            pl.BlockSpec(w_bf.shape, const2),
            pl.BlockSpec(wdt_bf.shape, const2),
            pl.BlockSpec(ssd_w.shape, const2),
            pl.BlockSpec(ssd_b.shape, const2),
            pl.BlockSpec((1, LANES), const2),
            pl.BlockSpec((1, LANES), const2),
        ],
        out_specs=[
            pl.BlockSpec((1, tm, hy_cols), lat),
            pl.BlockSpec((1, tm, d_ssd), lat),
            pl.BlockSpec((1, tm, d_xbc), allt),
            pl.BlockSpec((1, tm, LANES), allt),
        ],
        out_shape=[
            jax.ShapeDtypeStruct((bsz, L, hy_cols), jnp.float32),
            jax.ShapeDtypeStruct((bsz, L, d_ssd), jnp.float32),
            jax.ShapeDtypeStruct((bsz, lc + L, d_xbc), jnp.float32),
            jax.ShapeDtypeStruct((bsz, lc + L, LANES), jnp.float32),
        ],
        scratch_shapes=[pltpu.VMEM((tm, D), jnp.bfloat16)],
        compiler_params=pltpu.CompilerParams(
            dimension_semantics=("arbitrary", "arbitrary"), vmem_limit_bytes=VMEM_LIMIT_BYTES),
    )(ctx, x, g1, csh, csc, sh, sc, w_bf, wdt_bf, ssd_w, ssd_b, dt_bias2, dt_mult)


def in_proj_params(w_in, a_log, dt_bias, hy_cols, d_ssd, d_xbc):
    n_h = 2 * SSD_HEADS
    main = hy_cols + d_ssd + d_xbc
    w_dt = w_in[:, main:main + n_h]
    pad = jnp.zeros((w_in.shape[0], LANES - 2 * n_h), w_in.dtype)
    wdt = jnp.concatenate([w_dt, w_dt, pad], axis=1).astype(jnp.bfloat16)
    zpad = jnp.zeros((LANES - 2 * n_h,), jnp.float32)
    bias2 = jnp.concatenate([dt_bias.reshape(n_h), dt_bias.reshape(n_h), zpad])[None, :]
    mult = jnp.concatenate([jnp.ones((n_h,), jnp.float32), -jnp.exp(a_log).reshape(n_h), zpad])[None, :]
    return w_in[:, :main].astype(jnp.bfloat16), wdt, bias2, mult


def _out_router_kernel(yh_ref, ys_ref, z_ref, x_ref, ga_ref, sc_ref, sh_ref, ng_ref, g2_ref, wo_ref, wr_ref, br_ref,
                       x1_ref, hn_ref, ri_ref, rf_ref, cnt_ref, carry_ref):
    first = jnp.logical_and(pl.program_id(0) == 0, pl.program_id(1) == 0)

    @pl.when(first)
    def _():
        carry_ref[...] = jnp.zeros_like(carry_ref)

    bf = jnp.bfloat16
    tm = x_ref.shape[1]
    dh = yh_ref.shape[2]
    z = z_ref[0]
    ys = ys_ref[0] * (z * jax.nn.sigmoid(z))
    gw = ys.shape[1] // SSD_GROUPS
    acc = jnp.dot(yh_ref[0].astype(bf), wo_ref[0:dh, :], preferred_element_type=jnp.float32)
    for g in range(SSD_GROUPS):
        yg = ys[:, g * gw:(g + 1) * gw]
        yg = yg * lax.rsqrt(jnp.mean(yg * yg, axis=-1, keepdims=True) + EPS) * ng_ref[:, g * gw:(g + 1) * gw]
        acc += jnp.dot(yg.astype(bf), wo_ref[dh + g * gw:dh + (g + 1) * gw, :], preferred_element_type=jnp.float32)
    x1 = x_ref[0] + ga_ref[0] * acc
    x1_ref[0] = x1
    hn = x1 * lax.rsqrt(jnp.mean(x1 * x1, axis=-1, keepdims=True) + EPS) * g2_ref[...]
    hn = hn * (1.0 + sc_ref[0]) + sh_ref[0]
    hn_ref[0] = hn

    hn_hi = hn.astype(bf)
    hn_lo = (hn - hn_hi.astype(jnp.float32)).astype(bf)
    logits = (jnp.dot(hn_hi, wr_ref[0], preferred_element_type=jnp.float32)
              + jnp.dot(hn_lo, wr_ref[0], preferred_element_type=jnp.float32)
              + jnp.dot(hn_hi, wr_ref[1], preferred_element_type=jnp.float32)) + br_ref[...]
    lane = lax.broadcasted_iota(jnp.int32, (tm, LANES), 1)
    lane_f = lane.astype(jnp.float32)
    ninf = jnp.float32(-jnp.inf)
    big = jnp.float32(1e9)
    gl = jnp.where(lane < MOE_GROUPS, logits, ninf)
    gmax = jnp.max(gl, axis=-1, keepdims=True)
    p_group = 1.0 / jnp.sum(jnp.exp(gl - gmax), axis=-1, keepdims=True)
    g_sel = jnp.min(jnp.where(gl == gmax, lane_f, big), axis=-1, keepdims=True)
    e_lane = lane - MOE_GROUPS
    in_grp = jnp.logical_and(e_lane >= 0, (e_lane // EXPERTS_PER_GROUP).astype(jnp.float32) == g_sel)
    el = jnp.where(in_grp, logits, ninf)
    m1 = jnp.max(el, axis=-1, keepdims=True)
    i1 = jnp.min(jnp.where(el == m1, lane_f, big), axis=-1, keepdims=True)
    el2 = jnp.where(lane_f == i1, ninf, el)
    m2 = jnp.max(el2, axis=-1, keepdims=True)
    i2 = jnp.min(jnp.where(el2 == m2, lane_f, big), axis=-1, keepdims=True)
    t = jnp.exp(m2 - m1)
    w1 = 1.0 / (1.0 + t)
    gate1 = w1 * p_group
    gate2 = (t * w1) * p_group
    e1 = i1 - MOE_GROUPS
    e2 = i2 - MOE_GROUPS
    el_f = e_lane.astype(jnp.float32)
    oh1 = el_f == e1
    oh2 = el_f == e2
    oh = jnp.logical_or(oh1, oh2).astype(bf)
    r_i = lax.broadcasted_iota(jnp.int32, (tm, tm), 0)
    c_i = lax.broadcasted_iota(jnp.int32, (tm, tm), 1)
    before = jnp.dot((c_i < r_i).astype(bf), oh, preferred_element_type=jnp.float32) + carry_ref[...]
    rank1 = jnp.sum(jnp.where(oh1, before, 0.0), axis=-1, keepdims=True)
    rank2 = jnp.sum(jnp.where(oh2, before, 0.0), axis=-1, keepdims=True)
    carry_ref[...] += jnp.sum(oh.astype(jnp.float32), axis=0, keepdims=True)
    cnt_ref[...] = carry_ref[...]

    rec = jnp.where(lane == 0, e1, jnp.where(lane == 1, e2, jnp.where(lane == 2, rank1,
                                                                      jnp.where(lane == 3, rank2, 0.0))))
    ri_ref[0] = rec.T[0:ROUTE_COLS, :].astype(jnp.int32)
    col = lax.broadcasted_iota(jnp.int32, (tm, ROUTE_COLS), 1)
    rf_ref[0] = jnp.where(col == 0, gate1, gate2)


def out_proj_router(y_hy, y_scan, px, z_col, x, ga1, sc2, sh2, norm_g, g2, w_out_bf, w_router, b_router, tm):
    bsz, L, D = x.shape
    dh = y_hy.shape[-1]
    ds = y_scan.shape[-1]
    tok = lambda b, i: (b, i, 0)
    per_b = pl.BlockSpec((1, 1, D), lambda b, i: (b, 0, 0))
    const2 = lambda b, i: (0, 0)
    return pl.pallas_call(
        _out_router_kernel,
        grid=(bsz, L // tm),
        in_specs=[
            pl.BlockSpec((1, tm, dh), tok),
            pl.BlockSpec((1, tm, ds), tok),
            pl.BlockSpec((1, tm, ds), lambda b, i: (b, i, z_col)),
            pl.BlockSpec((1, tm, D), tok),
            per_b, per_b, per_b,
            pl.BlockSpec((1, ds), const2),
            pl.BlockSpec((1, D), const2),
            pl.BlockSpec((dh + ds, D), const2),
            pl.BlockSpec((2, D, LANES), lambda b, i: (0, 0, 0)),
            pl.BlockSpec((1, LANES), const2),
        ],
        out_specs=[
            pl.BlockSpec((1, tm, D), tok),
            pl.BlockSpec((1, tm, D), tok),
            pl.BlockSpec((1, ROUTE_COLS, tm), lambda b, i: (b, 0, i)),
            pl.BlockSpec((1, tm, ROUTE_COLS), tok),
            pl.BlockSpec((1, LANES), const2),
        ],
        out_shape=[
            jax.ShapeDtypeStruct((bsz, L, D), jnp.float32),
            jax.ShapeDtypeStruct((bsz, L, D), jnp.float32),
            jax.ShapeDtypeStruct((bsz, ROUTE_COLS, L), jnp.int32),
            jax.ShapeDtypeStruct((bsz, L, ROUTE_COLS), jnp.float32),
            jax.ShapeDtypeStruct((1, LANES), jnp.float32),
        ],
        scratch_shapes=[pltpu.VMEM((1, LANES), jnp.float32)],
        compiler_params=pltpu.CompilerParams(
            dimension_semantics=("arbitrary", "arbitrary"), vmem_limit_bytes=VMEM_LIMIT_BYTES),
    )(y_hy, y_scan, px, x, ga1, sc2, sh2, norm_g, g2, w_out_bf, w_router, b_router)


def _row_copy(src_hbm, src_row, dst_ref, dst_row, sem):
    return pltpu.make_async_copy(src_hbm.at[pl.ds(src_row, 1), :], dst_ref.at[pl.ds(dst_row, 1), :], sem)


def _dispatch_kernel(dest_ref, valid_ref, hn_ref, buf_hbm, zeros, sem, zsem):
    step = pl.program_id(0)
    tm = hn_ref.shape[0]
    n_tok = pl.num_programs(0) * tm
    blk = zeros.shape[0]
    n_blocks = buf_hbm.shape[0] // blk

    def zero_copy(i):
        return pltpu.make_async_copy(zeros, buf_hbm.at[pl.ds(pl.multiple_of(i * blk, blk), blk), :], zsem)

    def zfill(i, carry):
        @pl.when(valid_ref[i] < blk)
        def _():
            zero_copy(i).start()
        return carry

    def zwait(i, carry):
        @pl.when(valid_ref[i] < blk)
        def _():
            zero_copy(i).wait()
        return carry

    @pl.when(step == 0)
    def _():
        zeros[...] = jnp.zeros_like(zeros)
        lax.fori_loop(0, n_blocks, zfill, 0)
        lax.fori_loop(0, n_blocks, zwait, 0)

    def body(j, carry):
        t = step * tm + j
        _row_copy(hn_ref, j, buf_hbm, dest_ref[t], sem).start(priority=0)
        _row_copy(hn_ref, j, buf_hbm, dest_ref[n_tok + t], sem).start(priority=1)
        return carry

    lax.fori_loop(0, tm, body, 0, unroll=8)
    for _ in range(2):
        pltpu.make_async_copy(hn_ref, buf_hbm.at[pl.ds(0, tm), :], sem).wait()


def moe_dispatch(hn, dest, block_valid, blk, tm):
    T, D = hn.shape
    n_rows = block_valid.shape[0] * blk
    grid_spec = pltpu.PrefetchScalarGridSpec(
        num_scalar_prefetch=2,
        grid=(T // tm,),
        in_specs=[pl.BlockSpec((tm, D), lambda i, d, v: (i, 0))],
        out_specs=pl.BlockSpec(memory_space=pl.ANY),
        scratch_shapes=[pltpu.VMEM((blk, D), hn.dtype), pltpu.SemaphoreType.DMA(()),
                        pltpu.SemaphoreType.DMA(())],
    )
    return pl.pallas_call(
        _dispatch_kernel,
        grid_spec=grid_spec,
        out_shape=jax.ShapeDtypeStruct((n_rows, D), hn.dtype),
        compiler_params=pltpu.CompilerParams(dimension_semantics=("arbitrary",), has_side_effects=True),
    )(dest, block_valid, hn)


def _expert_kernel(wsel_ref, first_ref, valid_ref, src_ref, x_ref, w1_ref, w3_ref, w2_ref, o_ref, w1b, w3b, w2b):
    i = pl.program_id(0)
    del wsel_ref, src_ref
    bf = jnp.bfloat16

    @pl.when(first_ref[i] == 1)
    def _():
        w1b[...] = w1_ref[0].astype(bf)
        w3b[...] = w3_ref[0].astype(bf)
        w2b[...] = w2_ref[0].astype(bf)

    valid = valid_ref[i]

    @pl.when(valid > 0)
    def _():
        xb = x_ref[...].astype(bf)
        a = jnp.dot(xb, w1b[...], preferred_element_type=jnp.float32)
        b = jnp.dot(xb, w3b[...], preferred_element_type=jnp.float32)
        h = (a * jax.nn.sigmoid(a)) * b
        o_ref[...] = jnp.dot(h.astype(bf), w2b[...], preferred_element_type=jnp.float32)

    @pl.when(valid <= 0)
    def _():
        o_ref[...] = jnp.zeros_like(o_ref)


def expert_blocks(buf, block_wsel, block_first, block_valid, block_src, w1, w3, w2, blk):
    rows, D = buf.shape
    n_blocks = rows // blk
    E, _, F = w1.shape
    grid_spec = pltpu.PrefetchScalarGridSpec(
        num_scalar_prefetch=4,
        grid=(n_blocks,),
        in_specs=[
            pl.BlockSpec((blk, D), lambda i, ws, fi, va, src: (src[i], 0)),
            pl.BlockSpec((1, D, F), lambda i, ws, fi, va, src: (ws[i], 0, 0)),
            pl.BlockSpec((1, D, F), lambda i, ws, fi, va, src: (ws[i], 0, 0)),
            pl.BlockSpec((1, F, D), lambda i, ws, fi, va, src: (ws[i], 0, 0)),
        ],
        out_specs=pl.BlockSpec((blk, D), lambda i, ws, fi, va, src: (i, 0)),
        scratch_shapes=[pltpu.VMEM((D, F), jnp.bfloat16), pltpu.VMEM((D, F), jnp.bfloat16),
                        pltpu.VMEM((F, D), jnp.bfloat16)],
    )
    return pl.pallas_call(
        _expert_kernel,
        grid_spec=grid_spec,
        out_shape=jax.ShapeDtypeStruct((rows, D), jnp.float32),
        compiler_params=pltpu.CompilerParams(
            dimension_semantics=("arbitrary",), vmem_limit_bytes=VMEM_LIMIT_BYTES),
    )(block_wsel, block_first, block_valid, block_src, buf, w1, w3, w2)


def _combine_kernel(dest_ref, x1_ref, rf_ref, ga_ref, gf_ref, yb_hbm, o_ref, ybuf, sem):
    b = pl.program_id(0)
    i = pl.program_id(1)
    n_i = pl.num_programs(1)
    tm = x1_ref.shape[1]
    step = b * n_i + i
    n_steps = pl.num_programs(0) * n_i
    slot = step % 2

    def issue(step_, slot_):
        def body(j, carry):
            t = step_ * tm + j
            _row_copy(yb_hbm, dest_ref[t], ybuf.at[slot_, 0], j, sem.at[slot_]).start(priority=0)
            _row_copy(yb_hbm, dest_ref[n_steps * tm + t], ybuf.at[slot_, 1], j, sem.at[slot_]).start(priority=1)
            return carry
        lax.fori_loop(0, tm, body, 0, unroll=8)

    @pl.when(step == 0)
    def _():
        issue(0, 0)

    @pl.when(step + 1 < n_steps)
    def _():
        issue(step + 1, 1 - slot)

    pltpu.make_async_copy(yb_hbm.at[pl.ds(0, tm), :], ybuf.at[slot, 0], sem.at[slot]).wait()
    pltpu.make_async_copy(yb_hbm.at[pl.ds(0, tm), :], ybuf.at[slot, 1], sem.at[slot]).wait()
    rf = rf_ref[0]
    y = rf[:, 0:1] * ybuf[slot, 0] + rf[:, 1:2] * ybuf[slot, 1]
    x2 = x1_ref[0] + ga_ref[0] * y
    o_ref[0] = x2 * lax.rsqrt(jnp.mean(x2 * x2, axis=-1, keepdims=True) + EPS) * gf_ref[...]


def moe_combine(x1, route_f, ga2, g_final, yb, dest, tm):
    bsz, L, D = x1.shape
    grid_spec = pltpu.PrefetchScalarGridSpec(
        num_scalar_prefetch=1,
        grid=(bsz, L // tm),
        in_specs=[
            pl.BlockSpec((1, tm, D), lambda b, i, d: (b, i, 0)),
            pl.BlockSpec((1, tm, ROUTE_COLS), lambda b, i, d: (b, i, 0)),
            pl.BlockSpec((1, 1, D), lambda b, i, d: (b, 0, 0)),
            pl.BlockSpec((1, D), lambda b, i, d: (0, 0)),
            pl.BlockSpec(memory_space=pl.ANY),
        ],
        out_specs=pl.BlockSpec((1, tm, D), lambda b, i, d: (b, i, 0)),
        scratch_shapes=[pltpu.VMEM((2, 2, tm, D), jnp.float32), pltpu.SemaphoreType.DMA((2,))],
    )
    return pl.pallas_call(
        _combine_kernel,
        grid_spec=grid_spec,
        out_shape=jax.ShapeDtypeStruct((bsz, L, D), jnp.float32),
        compiler_params=pltpu.CompilerParams(
            dimension_semantics=("arbitrary", "arbitrary"), vmem_limit_bytes=VMEM_LIMIT_BYTES),
    )(dest, x1, route_f, ga2, g_final, yb)


def moe_plan(route_i, counts, blk, n_blocks):
    cnt = counts[0, MOE_GROUPS:MOE_GROUPS + N_EXPERTS].astype(jnp.int32)
    padded = (cnt + blk - 1) // blk * blk
    ends = jnp.cumsum(padded)
    starts = ends - padded
    experts = jnp.arange(N_EXPERTS, dtype=jnp.int32)
    dest = jnp.concatenate([
        (jnp.sum(jnp.where(route_i[:, k, :, None] == experts, starts, 0), axis=-1) + route_i[:, 2 + k]).reshape(-1)
        for k in range(TOP_K)])
    first_row = jnp.arange(n_blocks, dtype=jnp.int32) * blk
    block_eid = jnp.minimum(jnp.sum((ends[None, :] <= first_row[:, None]).astype(jnp.int32), axis=1), N_EXPERTS - 1)
    block_valid = jnp.clip(cnt[block_eid] - (first_row - starts[block_eid]), 0, blk).astype(jnp.int32)
    block_first = jnp.concatenate([jnp.ones((1,), jnp.int32),
                                   (block_eid[1:] != block_eid[:-1]).astype(jnp.int32)])
    idx = jnp.arange(n_blocks, dtype=jnp.int32)
    next_first = lax.cummin(jnp.where(block_first == 1, idx, n_blocks), axis=0, reverse=True)
    block_wsel = jnp.where(next_first < n_blocks, block_eid[jnp.minimum(next_first, n_blocks - 1)], block_eid)
    n_used = jnp.sum((block_valid > 0).astype(jnp.int32))
    block_src = jnp.minimum(idx, jnp.maximum(n_used - 1, 0))
    return dest, block_wsel, block_first, block_valid, block_src


def dft_tables(L):
    n = 2 * L
    f = lax.broadcasted_iota(jnp.int32, (L, L), 0)
    t = lax.broadcasted_iota(jnp.int32, (L, L), 1)
    ang = ((f * t) % n).astype(jnp.float32) * (2.0 * math.pi / n)
    return jnp.cos(ang).astype(jnp.bfloat16), jnp.sin(ang).astype(jnp.bfloat16)


def _alt_sign(L):
    t = lax.broadcasted_iota(jnp.int32, (L, 1), 0)
    return (1 - 2 * (t & 1)).astype(jnp.float32)


def _spectrum_kernel(a_ref, b_ref, c_ref, s_ref, kr_ref, ks_ref, kn_ref):
    L = a_ref.shape[1]
    a = a_ref[0]
    row = lax.broadcasted_iota(jnp.int32, (L, 1), 0)
    scale = jnp.where(row == 0, 0.5 / L, 1.0 / L)
    kr_ref[0] = scale * jnp.dot(c_ref[...], a.astype(jnp.bfloat16), preferred_element_type=jnp.float32)
    ks_ref[0] = scale * jnp.dot(s_ref[...], b_ref[0].astype(jnp.bfloat16), preferred_element_type=jnp.float32)
    kn_ref[0] = jnp.sum(a * _alt_sign(L), axis=0, keepdims=True) * (0.5 / L)


def filter_spectrum(a, b, cos_t, sin_t, tc):
    n, L, C = a.shape
    blk = pl.BlockSpec((1, L, tc), lambda o, j: (o, 0, j))
    tab = pl.BlockSpec((L, L), lambda o, j: (0, 0))
    return pl.pallas_call(
        _spectrum_kernel,
        grid=(n, C // tc),
        in_specs=[blk, blk, tab, tab],
        out_specs=[blk, blk, pl.BlockSpec((1, 1, tc), lambda o, j: (o, 0, j))],
        out_shape=[jax.ShapeDtypeStruct((n, L, C), jnp.float32)] * 2 + [jax.ShapeDtypeStruct((n, 1, C), jnp.float32)],
        compiler_params=pltpu.CompilerParams(
            dimension_semantics=("arbitrary", "arbitrary"), vmem_limit_bytes=VMEM_LIMIT_BYTES),
    )(a, b, cos_t, sin_t)


def _phase_conv3(raw, w_ref, b_ref, rows_per_phase):
    n_ph = len(raw)
    h = raw[0].shape[0]
    j = lax.broadcasted_iota(jnp.int32, (h, 1), 0) % rows_per_phase
    prev0 = jnp.where(j != 0, pltpu.roll(raw[n_ph - 1], 1, 0), 0.0)
    next_last = jnp.where(j != rows_per_phase - 1, pltpu.roll(raw[0], h - 1, 0), 0.0)
    out = []
    for p in range(n_ph):
        prev = raw[p - 1] if p > 0 else prev0
        nxt = raw[p + 1] if p < n_ph - 1 else next_last
        out.append(b_ref[...] + w_ref[0:1, :] * prev + w_ref[1:2, :] * raw[p] + w_ref[2:3, :] * nxt)
    return out


def _long_conv_kernel(*refs, n_ph, n_slab, conv_z, rows_per_phase):
    z_refs = refs[:n_slab]
    xn_refs = refs[n_slab:2 * n_slab]
    (kr_ref, ks_ref, kn_ref, bias_ref, cwz_ref, cbz_ref, cwx_ref, cbx_ref, c_ref, s_ref,
     o_ref, acc_ref, zr_ref, zs_ref, yr_ref, ys_ref, stage_ref) = refs[2 * n_slab:]
    H = z_refs[0].shape[1] // n_ph
    f32 = jnp.float32
    bf = jnp.bfloat16
    sign = _alt_sign(H)

    def phases(slab_refs):
        return [jnp.concatenate([r[0, pl.ds(p, H, stride=n_ph), :] for r in slab_refs], axis=1)
                for p in range(n_ph)]

    z_ph = phases(z_refs)
    if conv_z:
        z_ph = _phase_conv3(z_ph, cwz_ref, cbz_ref, rows_per_phase)
    for q in range(n_ph):
        zb = z_ph[q].astype(bf)
        zr_ref[q] = jnp.dot(c_ref[...], zb, preferred_element_type=f32)
        zs_ref[q] = jnp.dot(s_ref[...], zb, preferred_element_type=f32)
    z_nyq = [jnp.sum(z * sign, axis=0, keepdims=True) for z in z_ph]
    for p in range(n_ph):
        nyq = sum(z_nyq[q] * kn_ref[p - q + n_ph - 1] for q in range(n_ph))
        acc_ref[p] = z_ph[p] * bias_ref[0] + sign * nyq
        yr = 0.0
        ys = 0.0
        for q in range(n_ph):
            slot = p - q + n_ph - 1
            yr = yr + zr_ref[q] * kr_ref[slot] - zs_ref[q] * ks_ref[slot]
            ys = ys + zr_ref[q] * ks_ref[slot] + zs_ref[q] * kr_ref[slot]
        yr_ref[p] = yr.astype(bf)
        ys_ref[p] = ys.astype(bf)
    for p in range(n_ph):
        acc_ref[p] += (jnp.dot(c_ref[...], yr_ref[p], preferred_element_type=f32)
                       + jnp.dot(s_ref[...], ys_ref[p], preferred_element_type=f32))
    x_ph = _phase_conv3(phases(xn_refs), cwx_ref, cbx_ref, rows_per_phase)
    for p in range(n_ph):
        out_p = x_ph[p] * acc_ref[p]
        for sl in range(n_slab):
            stage_ref[sl, pl.ds(p, H, stride=n_ph), :] = out_p[:, sl * LANES:(sl + 1) * LANES]
    for sl in range(n_slab):
        o_ref[0, :, sl * LANES:(sl + 1) * LANES] = stage_ref[sl].astype(o_ref.dtype)


def long_conv_gate(z_arr, z_col, conv_z, xn_arr, xn_col, conv_w, conv_b, kr, ks, kn, bias, cos_t, sin_t,
                   tc, n_ph, row_len, out_dtype):
    bsz, L, _ = z_arr.shape
    H = L // n_ph
    C = kr.shape[-1]
    nj = C // tc
    n_slab = tc // LANES
    n_f = 2 * n_ph - 1
    tab = pl.BlockSpec((H, H), lambda j, b: (0, 0), pipeline_mode=pl.Buffered(1))
    spec = pl.BlockSpec((n_f, H, tc), lambda j, b: (0, 0, j), pipeline_mode=pl.Buffered(1))
    nyq = pl.BlockSpec((n_f, 1, tc), lambda j, b: (0, 0, j))
    vec = pl.BlockSpec((1, 1, tc), lambda j, b: (0, 0, j))

    def slabs(col):
        return [pl.BlockSpec((1, L, LANES),
                             functools.partial(lambda j, b, sl: (b, 0, (col * nj + j) * n_slab + sl), sl=sl))
                for sl in range(n_slab)]

    def conv_specs(col):
        return [pl.BlockSpec((3, tc), lambda j, b: (0, col * nj + j)),
                pl.BlockSpec((1, tc), lambda j, b: (0, col * nj + j))]

    zc = z_col if conv_z else 0
    kern = functools.partial(_long_conv_kernel, n_ph=n_ph, n_slab=n_slab, conv_z=conv_z,
                             rows_per_phase=row_len // n_ph)
    return pl.pallas_call(
        kern,
        grid=(nj, bsz),
        in_specs=(slabs(z_col) + slabs(xn_col) + [spec, spec, nyq, vec] + conv_specs(zc) + conv_specs(xn_col)
                  + [tab, tab]),
        out_specs=pl.BlockSpec((1, L, tc), lambda j, b: (b, 0, j)),
        out_shape=jax.ShapeDtypeStruct((bsz, L, C), out_dtype),
        scratch_shapes=[pltpu.VMEM((n_ph, H, tc), jnp.float32), pltpu.VMEM((n_ph, H, tc), jnp.float32),
                        pltpu.VMEM((n_ph, H, tc), jnp.float32), pltpu.VMEM((n_ph, H, tc), jnp.bfloat16),
                        pltpu.VMEM((n_ph, H, tc), jnp.bfloat16), pltpu.VMEM((n_slab, L, LANES), jnp.float32)],
        compiler_params=pltpu.CompilerParams(
            dimension_semantics=("arbitrary", "arbitrary"), vmem_limit_bytes=VMEM_LIMIT_BYTES),
    )(*([z_arr] * n_slab), *([xn_arr] * n_slab), kr, ks, kn, bias, conv_w, conv_b, conv_w, conv_b, cos_t, sin_t)


def _polyphase_taps(kf, kb, n_ph):
    H = kf.shape[0] // n_ph
    ph = lambda a, p: a[p * H:(p + 1) * H]
    zero = jnp.zeros_like(kf[:1])
    plus, minus = [], []
    for r in range(-(n_ph - 1), n_ph):
        if r >= 0:
            plus.append(ph(kf, r))
        else:
            plus.append(jnp.concatenate([ph(kb, -r)[0:1], ph(kf, n_ph + r)[:-1]], axis=0))
        if r <= 0:
            minus.append(jnp.concatenate([zero, ph(kb, -r)[1:]], axis=0))
        else:
            minus.append(jnp.concatenate([zero, ph(kb, n_ph - r)[:-1]], axis=0))
    return jnp.stack(plus), jnp.stack(minus)


def hyena_long_convs(p_hy, conv_w, conv_b, kp, h_bias, tc, n_ph, row_len):
    L = p_hy.shape[1]
    C = h_bias.shape[1]
    cos_t, sin_t = dft_tables(L // n_ph)
    z = p_hy
    for o in range(h_bias.shape[0]):
        fwd = slice((2 * o) * C, (2 * o + 1) * C)
        bwd = slice((2 * o + 1) * C, (2 * o + 2) * C)
        plus, minus = _polyphase_taps(kp[:, fwd], kp[:, bwd], n_ph)
        kr, ks, kn = filter_spectrum(plus + minus, plus - minus, cos_t, sin_t, tc)
        last = o == h_bias.shape[0] - 1
        z = long_conv_gate(z, 0, o == 0, p_hy, o + 1, conv_w, conv_b, kr, ks, kn, h_bias[o][None, None, :],
                           cos_t, sin_t, tc, n_ph, row_len, jnp.bfloat16 if last else jnp.float32)
    return z


def _filter_kernel(band_ref, w1_ref, b1_ref, fr_ref, w2_ref, b2_ref, w3_ref, dl_ref, k_ref, *, seq_len, n_ph):
    hp = lax.Precision.HIGHEST
    f32 = jnp.float32
    tp = k_ref.shape[0]
    per_phase = seq_len // n_ph
    g = lax.broadcasted_iota(jnp.int32, (tp, 1), 0) + pl.program_id(0) * tp
    phase = g // per_phase
    pos = (n_ph * (g - phase * per_phase) + phase).astype(f32)
    t = pos / max(seq_len - 1, 1)
    ang = (2 * math.pi / seq_len) * pos * band_ref[...]
    lane = lax.broadcasted_iota(jnp.int32, (tp, LANES), 1)
    feats = jnp.where(lane == 0, t,
                      jnp.where(lane <= HYENA_BANDS, jnp.cos(ang),
                                jnp.where(lane <= 2 * HYENA_BANDS, -jnp.sin(ang), 0.0)))
    h = jnp.sin(fr_ref[...] * (jnp.dot(feats, w1_ref[...], precision=hp, preferred_element_type=f32) + b1_ref[...]))
    h = jnp.sin(fr_ref[...] * (jnp.dot(h, w2_ref[...], precision=hp, preferred_element_type=f32) + b2_ref[...]))
    window = jnp.exp(-t * dl_ref[...])
    c = dl_ref.shape[1]
    for j in range(w3_ref.shape[1] // c):
        cols = slice(j * c, (j + 1) * c)
        k_ref[:, cols] = jnp.dot(h, w3_ref[:, cols], precision=hp, preferred_element_type=f32) * window


def hyena_filters_polyphase(seq_len, f_w1, f_b1, f_freq, f_w2, f_b2, f_w3, d_hyena, tp, n_ph):
    f32 = jnp.float32
    fh = f_w1.shape[1]
    n_emb = 1 + 2 * HYENA_BANDS
    bands = jnp.linspace(1e-4, HYENA_BANDS - 1, HYENA_BANDS, dtype=f32)
    band_row = jnp.concatenate([jnp.zeros((1,), f32), bands, bands, jnp.zeros((LANES - n_emb,), f32)])[None, :]
    w1p = jnp.concatenate([f_w1, jnp.zeros((LANES - n_emb, fh), f32)], axis=0)
    deltas = jnp.abs(jnp.linspace(math.log(HYENA_TARGET) / HYENA_SLOW_DECAY,
                                  math.log(HYENA_TARGET) / HYENA_FAST_DECAY, d_hyena, dtype=f32))[None, :]
    n_out = f_w3.shape[1]
    full = lambda a: pl.BlockSpec(a.shape, lambda i: (0,) * a.ndim)
    args = (band_row, w1p, f_b1[None, :], f_freq[None, :], f_w2, f_b2[None, :], f_w3, deltas)
    return pl.pallas_call(
        functools.partial(_filter_kernel, seq_len=seq_len, n_ph=n_ph),
        grid=(seq_len // tp,),
        in_specs=[full(a) for a in args],
        out_specs=pl.BlockSpec((tp, n_out), lambda i: (i, 0)),
        out_shape=jax.ShapeDtypeStruct((seq_len, n_out), f32),
        compiler_params=pltpu.CompilerParams(dimension_semantics=("arbitrary",), vmem_limit_bytes=VMEM_LIMIT_BYTES),
    )(*args)


def _ada_kernel(c_ref, w_ref, b_ref, o_ref):
    cv = c_ref[...]
    s = cv * jax.nn.sigmoid(cv)
    o_ref[...] = jnp.dot(s, w_ref[...], precision=lax.Precision.HIGHEST,
                         preferred_element_type=jnp.float32) + b_ref[...]


def ada_modulation(c_rows, w_ada, b_ada, tn):
    rows, D = c_rows.shape
    N = w_ada.shape[1]
    return pl.pallas_call(
        _ada_kernel,
        grid=(N // tn,),
        in_specs=[pl.BlockSpec((rows, D), lambda j: (0, 0)),
                  pl.BlockSpec((D, tn), lambda j: (0, j)),
                  pl.BlockSpec((1, tn), lambda j: (0, j))],
        out_specs=pl.BlockSpec((rows, tn), lambda j: (0, j)),
        out_shape=jax.ShapeDtypeStruct((rows, N), jnp.float32),
        compiler_params=pltpu.CompilerParams(dimension_semantics=("arbitrary",), vmem_limit_bytes=VMEM_LIMIT_BYTES),
    )(c_rows, w_ada, b_ada[None, :])


def _ssd_kernel(xf_ref, df_ref, xb_ref, db_ref, dskip_ref, y_ref, h_ref, *, n_ctx_chunks):
    s = pl.program_id(1)
    n_steps = pl.num_programs(1)
    Q, G, R, P, N = SSD_CHUNK, SSD_GROUPS, SSD_HPG, SSD_HEAD_DIM, SSD_STATE
    GP = R * P
    bf = jnp.bfloat16

    @pl.when(s == 0)
    def _():
        h_ref[...] = jnp.zeros_like(h_ref)
        y_ref[...] = jnp.zeros_like(y_ref)

    row = lax.broadcasted_iota(jnp.int32, (Q, Q), 0)
    col = lax.broadcasted_iota(jnp.int32, (Q, Q), 1)
    lane_head = lax.broadcasted_iota(jnp.int32, (Q, GP), 1) // P
    block_head = lax.broadcasted_iota(jnp.int32, (Q, LANES), 1) // P
    is_latent = s >= n_ctx_chunks
    n_lat = n_steps - n_ctx_chunks
    out_chunk = (jnp.clip(s - n_ctx_chunks, 0, n_lat - 1), jnp.clip(n_steps - 1 - s, 0, n_lat - 1))

    for bb, d in [(bb, d) for bb in range(y_ref.shape[0]) for d in range(2)]:
        x_ref, da_ref = ((xf_ref, df_ref), (xb_ref, db_ref))[d]
        mask = (row >= col) if d == 0 else (col >= row)
        tri = mask.astype(jnp.float32)
        da = da_ref[bb]
        cum = jnp.dot(tri, da, precision=lax.Precision.HIGHEST, preferred_element_type=jnp.float32)
        cum_t = cum.T
        edge = Q - 1 if d == 0 else 0
        blk = x_ref.at[bb]
        for g in range(G):
            xg = blk[:, g * GP:(g + 1) * GP]
            bg = blk[:, D_SSD + g * N:D_SSD + (g + 1) * N].astype(bf)
            cg = blk[:, D_SSD + G * N + g * N:D_SSD + G * N + (g + 1) * N].astype(bf)
            heads = [d * SSD_HEADS + g * R + r for r in range(R)]
            dtm = jnp.zeros((Q, GP), jnp.float32)
            cumm = jnp.zeros((Q, GP), jnp.float32)
            for r, h in enumerate(heads):
                sel = lane_head == r
                dtm = jnp.where(sel, da[:, h:h + 1], dtm)
                cumm = jnp.where(sel, cum[:, SSD_HEADS * 2 + h:SSD_HEADS * 2 + h + 1], cumm)
            totm = cumm[edge:edge + 1, :]
            xdt = xg * dtm
            hg = h_ref[bb, d, g * GP:(g + 1) * GP, :]

            gmat = lax.dot_general(cg, bg, (((1,), (1,)), ((), ())), preferred_element_type=jnp.float32)
            y_off = lax.dot_general(cg, hg.astype(bf), (((1,), (1,)), ((), ())),
                                    preferred_element_type=jnp.float32) * jnp.exp(cumm)
            if d == 0:
                y_off = y_off + dskip_ref[:, g * GP:(g + 1) * GP] * xg
            per_block = LANES // P
            parts = [jnp.zeros((Q, LANES), jnp.float32) for _ in range(R // per_block)]
            for r, h in enumerate(heads):
                a_col = cum[:, SSD_HEADS * 2 + h:SSD_HEADS * 2 + h + 1]
                a_row = cum_t[SSD_HEADS * 2 + h:SSD_HEADS * 2 + h + 1, :]
                decay = jnp.exp(jnp.where(mask, a_col - a_row, NEG_BIG))
                j = r // per_block
                xblk = xdt[:, j * LANES:(j + 1) * LANES]
                own = block_head == r % per_block
                parts[j] = parts[j] + jnp.dot((gmat * decay).astype(bf), jnp.where(own, xblk, 0.0).astype(bf),
                                              preferred_element_type=jnp.float32)
            y = jnp.where(is_latent, y_off + jnp.concatenate(parts, axis=-1), 0.0)
            rows = pl.ds(pl.multiple_of(out_chunk[d] * Q, Q), Q)
            y_ref[bb, rows, g * GP:(g + 1) * GP] += y

            xw = (xdt * jnp.exp(totm - cumm)).astype(bf)
            st = lax.dot_general(xw, bg, (((0,), (0,)), ((), ())), preferred_element_type=jnp.float32)
            for r, h in enumerate(heads):
                dec = jnp.exp(cum_t[SSD_HEADS * 2 + h:SSD_HEADS * 2 + h + 1, edge:edge + 1])
                rs = slice(g * GP + r * P, g * GP + (r + 1) * P)
                h_ref[bb, d, rs, :] = h_ref[bb, d, rs, :] * dec + st[r * P:(r + 1) * P, :]


def ssd_scan_bidir(xbc, dta, d_skip, n_ctx, lat_off, nb):
    bsz, lt, width = xbc.shape
    Q = SSD_CHUNK
    n_ctx_chunks = n_ctx // Q
    L = lt - lat_off
    n_lat = L // Q
    n_steps = n_ctx_chunks + n_lat
    lat0 = lat_off // Q

    def fwd_chunk(s):
        return jnp.where(s < n_ctx_chunks, s, s - n_ctx_chunks + lat0)

    def bwd_chunk(s):
        return jnp.where(s < n_ctx_chunks, n_ctx_chunks - 1 - s, n_steps - 1 - s + lat0)

    return pl.pallas_call(
        functools.partial(_ssd_kernel, n_ctx_chunks=n_ctx_chunks),
        grid=(bsz // nb, n_steps),
        in_specs=[
            pl.BlockSpec((nb, Q, width), lambda b, s: (b, fwd_chunk(s), 0)),
            pl.BlockSpec((nb, Q, LANES), lambda b, s: (b, fwd_chunk(s), 0)),
            pl.BlockSpec((nb, Q, width), lambda b, s: (b, bwd_chunk(s), 0)),
            pl.BlockSpec((nb, Q, LANES), lambda b, s: (b, bwd_chunk(s), 0)),
            pl.BlockSpec((1, D_SSD), lambda b, s: (0, 0)),
        ],
        out_specs=pl.BlockSpec((nb, L, D_SSD), lambda b, s: (b, 0, 0)),
        out_shape=jax.ShapeDtypeStruct((bsz, L, D_SSD), jnp.float32),
        scratch_shapes=[pltpu.VMEM((nb, 2, SSD_GROUPS * SSD_HPG * SSD_HEAD_DIM, SSD_STATE), jnp.float32)],
        compiler_params=pltpu.CompilerParams(
            dimension_semantics=("arbitrary", "arbitrary"), vmem_limit_bytes=VMEM_LIMIT_BYTES),
    )(xbc, dta, xbc, dta, d_skip)


def kernel(x, c, ctx, c_ctx, w_ada, b_ada, g_norm1, g_norm2, w_in, hy_conv_w, hy_conv_b, hy_f_w1, hy_f_b1, hy_f_freq, hy_f_w2, hy_f_b2, hy_f_w3, hy_bias, ssd_conv_w, ssd_conv_b, ssd_a_log, ssd_dt_bias, ssd_d, ssd_norm_g, w_out, w_group, b_group, w_expert, b_expert, w1, w3, w2, g_final):
    bsz, seq_len, _ = x.shape
    assert w_in.shape[0] == 1, "single-layer block: the context stream only supplies SSD states"
    l = 0
    rows_pad = -(bsz + 1) % SUBLANES
    c_rows = jnp.concatenate([c, c_ctx[None, :], jnp.zeros((rows_pad, D_MODEL), jnp.float32)], axis=0)
    mod_all = ada_modulation(c_rows, w_ada[l], b_ada[l], ADA_COLS)
    sh1, sc1, ga1, sh2, sc2, ga2 = jnp.split(mod_all[:bsz, None, :], 6, axis=-1)
    csh1, csc1 = mod_all[bsz, :D_MODEL], mod_all[bsz, D_MODEL:2 * D_MODEL]

    w_out_bf = w_out[l].astype(jnp.bfloat16)
    w_in_bf, w_dt_bf, dt_bias2, dt_mult = in_proj_params(w_in[l], ssd_a_log[l], ssd_dt_bias[l],
                                                         HY_COLS, D_SSD, D_XBC)

    ctx_len = ctx.shape[1]
    lat_off = -(-ctx_len // IN_PROJ_ROWS) * IN_PROJ_ROWS
    p_hy, z, xbc, dta = in_proj_fused(ctx, x, g_norm1[l][None, :], csh1[None, :], csc1[None, :], sh1, sc1,
                                      w_in_bf, w_dt_bf, HY_COLS, ssd_conv_w[l], ssd_conv_b[l][None, :],
                                      dt_bias2, dt_mult, GRID_W, ctx_len, IN_PROJ_ROWS, IN_PROJ_COLS)
    kp = hyena_filters_polyphase(seq_len, hy_f_w1[l], hy_f_b1[l], hy_f_freq[l], hy_f_w2[l], hy_f_b2[l],
                                 hy_f_w3[l], D_HYENA, FILTER_ROWS, HYENA_PHASES)
    y_hy = hyena_long_convs(p_hy, hy_conv_w[l], hy_conv_b[l][None, :], kp, hy_bias[l], HYENA_COLS, HYENA_PHASES,
                            GRID_W)
    y_scan = ssd_scan_bidir(xbc, dta, jnp.repeat(ssd_d[l], SSD_HEAD_DIM)[None, :], ctx_len, lat_off, SSD_BATCH)

    pad = LANES - MOE_GROUPS - N_EXPERTS
    w_router = jnp.concatenate([w_group[l], w_expert[l], jnp.zeros((D_MODEL, pad), jnp.float32)], axis=1)
    w_router_hi = w_router.astype(jnp.bfloat16)
    w_router_lo = (w_router - w_router_hi.astype(jnp.float32)).astype(jnp.bfloat16)
    w_router = jnp.stack([w_router_hi, w_router_lo])
    b_router = jnp.concatenate([b_group[l], b_expert[l], jnp.zeros((pad,), jnp.float32)])[None, :]
    x1, hn, route_i, route_f, counts = out_proj_router(
        y_hy, y_scan, z, 0, x, ga1, sc2, sh2, ssd_norm_g[l][None, :], g_norm2[l][None, :],
        w_out_bf, w_router, b_router, TOKEN_ROWS)
    n_tok = bsz * seq_len
    n_blocks = -(-n_tok * TOP_K // MOE_BLOCK) + N_EXPERTS
    dest, block_wsel, block_first, block_valid, block_src = moe_plan(route_i, counts, MOE_BLOCK, n_blocks)
    buf = moe_dispatch(hn.reshape(n_tok, D_MODEL), dest, block_valid, MOE_BLOCK, DISPATCH_ROWS)
    yb = expert_blocks(buf, block_wsel, block_first, block_valid, block_src, w1[l], w3[l], w2[l], MOE_BLOCK)
    return moe_combine(x1, route_f, ga2, g_final[None, :], yb, dest, TOKEN_ROWS)
```
